```python
import jax, jax.numpy as jnp
from jax import lax
import numpy as np

D_MODEL = 1024
BATCH = 8
SEQ = 2048
DEPTH = 4

EPS_RMS = 1e-6
EPS_LN = 1e-5
W_CONV = D_MODEL
CONV_K = 31
W_POOL = D_MODEL
POOL_WINDOWS = (2, 4, 8, 16)
POOL_GROUPS = len(POOL_WINDOWS)
POOL_GW = W_POOL // POOL_GROUPS
W_EVEN_IN = 3 * W_CONV + 2 * W_POOL
W_EVEN_MIX = W_CONV + W_POOL
LRU_HEADS = 12
LRU_HD = 128
W_LRU = LRU_HEADS * LRU_HD
LRU_CONV_K = 4
LRU_C = 8.0
N_EVEN = (DEPTH + 1) // 2
N_ODD = DEPTH // 2

kernel_name = "hybrid_conv_pool_rglru_trunk"


def rmsnorm(x, g):
    xf = x.astype(jnp.float32)
    y = xf * lax.rsqrt(jnp.mean(xf * xf, axis=-1, keepdims=True) + EPS_RMS)
    return (y * g.astype(jnp.float32)).astype(x.dtype)


def layernorm(x, g, b):
    xf = x.astype(jnp.float32)
    mu = jnp.mean(xf, axis=-1, keepdims=True)
    var = jnp.mean(jnp.square(xf - mu), axis=-1, keepdims=True)
    y = (xf - mu) * lax.rsqrt(var + EPS_LN)
    return (y * g.astype(jnp.float32) + b.astype(jnp.float32)).astype(x.dtype)


def causal_depthwise_conv(x, w, b):
    k = w.shape[0]
    y = lax.conv_general_dilated(
        x, w[:, None, :].astype(x.dtype), window_strides=(1,), padding=[(k - 1, 0)],
        dimension_numbers=("NWC", "WIO", "NWC"), feature_group_count=x.shape[-1])
    return y + b.astype(x.dtype)


def multiscale_pool_diff(v):
    bsz, t, _ = v.shape
    vf = v.astype(jnp.float32)
    cs_pad = jnp.pad(jnp.cumsum(vf, axis=1), ((0, 0), (1, 0), (0, 0)))
    pos = jnp.arange(1, t + 1, dtype=jnp.float32)
    outs = []
    for g, w in enumerate(POOL_WINDOWS):
        seg = cs_pad[:, :, g * POOL_GW:(g + 1) * POOL_GW]
        upper = seg[:, 1:]
        lower = jnp.pad(seg[:, :t - w + 1], ((0, 0), (w - 1, 0), (0, 0)))
        cnt = jnp.minimum(pos, jnp.float32(w))[None, :, None]
        outs.append((upper - lower) / cnt)
    return jnp.concatenate(outs, axis=-1) - vf


def even_mixer(h, w_in, conv_w, conv_b, ln_g, ln_b, pool_w, pool_b, pool_scale, w_out):
    bsz, t, _ = h.shape
    p = jnp.einsum("btd,dn->btn", h, w_in)
    a_val, a_glu, a_gate, b_val, b_gate = jnp.split(
        p, [W_CONV, 2 * W_CONV, 3 * W_CONV, 3 * W_CONV + W_POOL], axis=-1)
    u = a_val * jax.nn.sigmoid(a_glu)
    u = causal_depthwise_conv(u, conv_w, conv_b)
    u = jax.nn.silu(layernorm(u, ln_g, ln_b))
    ya = u * jax.nn.silu(a_gate)
    d = multiscale_pool_diff(b_val).reshape(bsz, t, POOL_GROUPS, POOL_GW)
    d = jnp.einsum("btgc,gce->btge", d, pool_w.astype(jnp.float32)) + pool_b.astype(jnp.float32)
    yb = (d.reshape(bsz, t, W_POOL) * pool_scale.astype(jnp.float32)).astype(h.dtype)
    yb = yb * jax.nn.silu(b_gate)
    y = jnp.concatenate([ya, yb], axis=-1)
    return jnp.einsum("btn,nd->btd", y, w_out)


def _lin_combine(c1, c2):
    a1, b1 = c1
    a2, b2 = c2
    return a1 * a2, a2 * b1 + b2


def odd_mixer(h, w_in, conv_w, conv_b, w_rg, b_rg, w_ig, b_ig, lam, w_out):
    bsz, t, _ = h.shape
    p = jnp.einsum("btd,dn->btn", h, w_in)
    xr, gate = jnp.split(p, [W_LRU], axis=-1)
    xc = causal_depthwise_conv(xr, conv_w, conv_b)
    xh = xc.reshape(bsz, t, LRU_HEADS, LRU_HD)
    r = jax.nn.sigmoid(jnp.einsum("bthi,hij->bthj", xh, w_rg).reshape(bsz, t, W_LRU).astype(jnp.float32)
                       + b_rg.astype(jnp.float32))
    i = jax.nn.sigmoid(jnp.einsum("bthi,hij->bthj", xh, w_ig).reshape(bsz, t, W_LRU).astype(jnp.float32)
                       + b_ig.astype(jnp.float32))
    log_a = -LRU_C * r * jax.nn.softplus(-lam.astype(jnp.float32))
    a = jnp.exp(log_a)
    mult = jnp.sqrt(-jnp.expm1(2.0 * log_a))
    bterm = mult * (i * xc.astype(jnp.float32))
    _, hs = lax.associative_scan(_lin_combine, (a, bterm), axis=1)
    y = hs.astype(h.dtype) * jax.nn.silu(gate)
    return jnp.einsum("btn,nd->btd", y, w_out)


def _fwd_setup_inputs(seed: int = 0) -> dict:
    key = jax.random.key(seed)
    ks = iter(jax.random.split(key, 32))
    f32 = jnp.float32

    def nrm(shape, scale):
        return jax.random.normal(next(ks), shape, f32) * scale

    x = jax.random.normal(next(ks), (BATCH, SEQ, D_MODEL), f32)
    a0 = jax.random.uniform(next(ks), (N_ODD, W_LRU), f32, 0.9, 0.999)
    s = a0 ** (1.0 / LRU_C)
    lru_lambda = jnp.log(s) - jnp.log1p(-s)
    return {
        "x": x,
        "norm_even": 1.0 + nrm((N_EVEN, D_MODEL), 0.02),
        "w_in_even": nrm((N_EVEN, D_MODEL, W_EVEN_IN), D_MODEL ** -0.5),
        "conv_a_w": nrm((N_EVEN, CONV_K, W_CONV), CONV_K ** -0.5),
        "conv_a_b": nrm((N_EVEN, W_CONV), 0.01),
        "ln_a_g": 1.0 + nrm((N_EVEN, W_CONV), 0.02),
        "ln_a_b": nrm((N_EVEN, W_CONV), 0.01),
        "pool_w": nrm((N_EVEN, POOL_GROUPS, POOL_GW, POOL_GW), POOL_GW ** -0.5),
        "pool_b": nrm((N_EVEN, POOL_GROUPS, POOL_GW), 0.01),
        "pool_scale": 1.0 + nrm((N_EVEN, W_POOL), 0.1),
        "w_out_even": nrm((N_EVEN, W_EVEN_MIX, D_MODEL), W_EVEN_MIX ** -0.5),
        "norm_odd": 1.0 + nrm((N_ODD, D_MODEL), 0.02),
        "w_in_odd": nrm((N_ODD, D_MODEL, 2 * W_LRU), D_MODEL ** -0.5),
        "conv_c_w": nrm((N_ODD, LRU_CONV_K, W_LRU), LRU_CONV_K ** -0.5),
        "conv_c_b": nrm((N_ODD, W_LRU), 0.01),
        "w_rg": nrm((N_ODD, LRU_HEADS, LRU_HD, LRU_HD), LRU_HD ** -0.5),
        "b_rg": nrm((N_ODD, W_LRU), 0.01),
        "w_ig": nrm((N_ODD, LRU_HEADS, LRU_HD, LRU_HD), LRU_HD ** -0.5),
        "b_ig": nrm((N_ODD, W_LRU), 0.01),
        "lru_lambda": lru_lambda,
        "w_out_odd": nrm((N_ODD, W_LRU, D_MODEL), W_LRU ** -0.5),
        "final_norm": 1.0 + nrm((D_MODEL,), 0.02),
    }


def _fwd_reference(x, norm_even, w_in_even, conv_a_w, conv_a_b, ln_a_g, ln_a_b, pool_w, pool_b,
              pool_scale, w_out_even, norm_odd, w_in_odd, conv_c_w, conv_c_b, w_rg, b_rg,
              w_ig, b_ig, lru_lambda, w_out_odd, final_norm):
    h = x
    for layer in range(DEPTH):
        if layer % 2 == 0:
            j = layer // 2
            h = h + even_mixer(rmsnorm(h, norm_even[j]), w_in_even[j], conv_a_w[j], conv_a_b[j],
                               ln_a_g[j], ln_a_b[j], pool_w[j], pool_b[j], pool_scale[j],
                               w_out_even[j])
        else:
            j = layer // 2
            h = h + odd_mixer(rmsnorm(h, norm_odd[j]), w_in_odd[j], conv_c_w[j], conv_c_b[j],
                              w_rg[j], b_rg[j], w_ig[j], b_ig[j], lru_lambda[j], w_out_odd[j])
    return rmsnorm(h, final_norm)


import jax as _jax
import jax.numpy as _jnp

TWIN_FORMAT = 'train_step'
FWD_PARAMS = ['x', 'norm_even', 'w_in_even', 'conv_a_w', 'conv_a_b', 'ln_a_g', 'ln_a_b', 'pool_w', 'pool_b', 'pool_scale', 'w_out_even', 'norm_odd', 'w_in_odd', 'conv_c_w', 'conv_c_b', 'w_rg', 'b_rg', 'w_ig', 'b_ig', 'lru_lambda', 'w_out_odd', 'final_norm']
TWIN_WEIGHTS = ['norm_even', 'w_in_even', 'conv_a_w', 'conv_a_b', 'ln_a_g', 'ln_a_b', 'pool_w', 'pool_b', 'pool_scale', 'w_out_even', 'norm_odd', 'w_in_odd', 'conv_c_w', 'conv_c_b', 'w_rg', 'b_rg', 'w_ig', 'b_ig', 'lru_lambda', 'w_out_odd', 'final_norm']
TWIN_DIFF_INPUT = 'x'
TWIN_INPUTS = ['x', 'norm_even', 'w_in_even', 'conv_a_w', 'conv_a_b', 'ln_a_g', 'ln_a_b', 'pool_w', 'pool_b', 'pool_scale', 'w_out_even', 'norm_odd', 'w_in_odd', 'conv_c_w', 'conv_c_b', 'w_rg', 'b_rg', 'w_ig', 'b_ig', 'lru_lambda', 'w_out_odd', 'final_norm', 'loss_target', 'm_norm_even', 'm_w_in_even', 'm_conv_a_w', 'm_conv_a_b', 'm_ln_a_g', 'm_ln_a_b', 'm_pool_w', 'm_pool_b', 'm_pool_scale', 'm_w_out_even', 'm_norm_odd', 'm_w_in_odd', 'm_conv_c_w', 'm_conv_c_b', 'm_w_rg', 'm_b_rg', 'm_w_ig', 'm_b_ig', 'm_lru_lambda', 'm_w_out_odd', 'm_final_norm', 'v_norm_even', 'v_w_in_even', 'v_conv_a_w', 'v_conv_a_b', 'v_ln_a_g', 'v_ln_a_b', 'v_pool_w', 'v_pool_b', 'v_pool_scale', 'v_w_out_even', 'v_norm_odd', 'v_w_in_odd', 'v_conv_c_w', 'v_conv_c_b', 'v_w_rg', 'v_b_rg', 'v_w_ig', 'v_b_ig', 'v_lru_lambda', 'v_w_out_odd', 'v_final_norm']
TWIN_OUTPUTS = ['loss', 'grad_x', 'grad_norm_even', 'grad_w_in_even', 'grad_conv_a_w', 'grad_conv_a_b', 'grad_ln_a_g', 'grad_ln_a_b', 'grad_pool_w', 'grad_pool_b', 'grad_pool_scale', 'grad_w_out_even', 'grad_norm_odd', 'grad_w_in_odd', 'grad_conv_c_w', 'grad_conv_c_b', 'grad_w_rg', 'grad_b_rg', 'grad_w_ig', 'grad_b_ig', 'grad_lru_lambda', 'grad_w_out_odd', 'grad_final_norm', 'delta_norm_even', 'delta_w_in_even', 'delta_conv_a_w', 'delta_conv_a_b', 'delta_ln_a_g', 'delta_ln_a_b', 'delta_pool_w', 'delta_pool_b', 'delta_pool_scale', 'delta_w_out_even', 'delta_norm_odd', 'delta_w_in_odd', 'delta_conv_c_w', 'delta_conv_c_b', 'delta_w_rg', 'delta_b_rg', 'delta_w_ig', 'delta_b_ig', 'delta_lru_lambda', 'delta_w_out_odd', 'delta_final_norm', 'new_m_norm_even', 'new_m_w_in_even', 'new_m_conv_a_w', 'new_m_conv_a_b', 'new_m_ln_a_g', 'new_m_ln_a_b', 'new_m_pool_w', 'new_m_pool_b', 'new_m_pool_scale', 'new_m_w_out_even', 'new_m_norm_odd', 'new_m_w_in_odd', 'new_m_conv_c_w', 'new_m_conv_c_b', 'new_m_w_rg', 'new_m_b_rg', 'new_m_w_ig', 'new_m_b_ig', 'new_m_lru_lambda', 'new_m_w_out_odd', 'new_m_final_norm', 'new_v_norm_even', 'new_v_w_in_even', 'new_v_conv_a_w', 'new_v_conv_a_b', 'new_v_ln_a_g', 'new_v_ln_a_b', 'new_v_pool_w', 'new_v_pool_b', 'new_v_pool_scale', 'new_v_w_out_even', 'new_v_norm_odd', 'new_v_w_in_odd', 'new_v_conv_c_w', 'new_v_conv_c_b', 'new_v_w_rg', 'new_v_b_rg', 'new_v_w_ig', 'new_v_b_ig', 'new_v_lru_lambda', 'new_v_w_out_odd', 'new_v_final_norm']
TWIN_LEAF_KINDS = {'loss': 'loss', 'grad_x': 'grad_x', 'grad_norm_even': 'grad_w', 'grad_w_in_even': 'grad_w', 'grad_conv_a_w': 'grad_w', 'grad_conv_a_b': 'grad_w', 'grad_ln_a_g': 'grad_w', 'grad_ln_a_b': 'grad_w', 'grad_pool_w': 'grad_w', 'grad_pool_b': 'grad_w', 'grad_pool_scale': 'grad_w', 'grad_w_out_even': 'grad_w', 'grad_norm_odd': 'grad_w', 'grad_w_in_odd': 'grad_w', 'grad_conv_c_w': 'grad_w', 'grad_conv_c_b': 'grad_w', 'grad_w_rg': 'grad_w', 'grad_b_rg': 'grad_w', 'grad_w_ig': 'grad_w', 'grad_b_ig': 'grad_w', 'grad_lru_lambda': 'grad_w', 'grad_w_out_odd': 'grad_w', 'grad_final_norm': 'grad_w', 'delta_norm_even': 'delta_w', 'delta_w_in_even': 'delta_w', 'delta_conv_a_w': 'delta_w', 'delta_conv_a_b': 'delta_w', 'delta_ln_a_g': 'delta_w', 'delta_ln_a_b': 'delta_w', 'delta_pool_w': 'delta_w', 'delta_pool_b': 'delta_w', 'delta_pool_scale': 'delta_w', 'delta_w_out_even': 'delta_w', 'delta_norm_odd': 'delta_w', 'delta_w_in_odd': 'delta_w', 'delta_conv_c_w': 'delta_w', 'delta_conv_c_b': 'delta_w', 'delta_w_rg': 'delta_w', 'delta_b_rg': 'delta_w', 'delta_w_ig': 'delta_w', 'delta_b_ig': 'delta_w', 'delta_lru_lambda': 'delta_w', 'delta_w_out_odd': 'delta_w', 'delta_final_norm': 'delta_w', 'new_m_norm_even': 'new_m', 'new_m_w_in_even': 'new_m', 'new_m_conv_a_w': 'new_m', 'new_m_conv_a_b': 'new_m', 'new_m_ln_a_g': 'new_m', 'new_m_ln_a_b': 'new_m', 'new_m_pool_w': 'new_m', 'new_m_pool_b': 'new_m', 'new_m_pool_scale': 'new_m', 'new_m_w_out_even': 'new_m', 'new_m_norm_odd': 'new_m', 'new_m_w_in_odd': 'new_m', 'new_m_conv_c_w': 'new_m', 'new_m_conv_c_b': 'new_m', 'new_m_w_rg': 'new_m', 'new_m_b_rg': 'new_m', 'new_m_w_ig': 'new_m', 'new_m_b_ig': 'new_m', 'new_m_lru_lambda': 'new_m', 'new_m_w_out_odd': 'new_m', 'new_m_final_norm': 'new_m', 'new_v_norm_even': 'new_v', 'new_v_w_in_even': 'new_v', 'new_v_conv_a_w': 'new_v', 'new_v_conv_a_b': 'new_v', 'new_v_ln_a_g': 'new_v', 'new_v_ln_a_b': 'new_v', 'new_v_pool_w': 'new_v', 'new_v_pool_b': 'new_v', 'new_v_pool_scale': 'new_v', 'new_v_w_out_even': 'new_v', 'new_v_norm_odd': 'new_v', 'new_v_w_in_odd': 'new_v', 'new_v_conv_c_w': 'new_v', 'new_v_conv_c_b': 'new_v', 'new_v_w_rg': 'new_v', 'new_v_b_rg': 'new_v', 'new_v_w_ig': 'new_v', 'new_v_b_ig': 'new_v', 'new_v_lru_lambda': 'new_v', 'new_v_w_out_odd': 'new_v', 'new_v_final_norm': 'new_v'}


def _forward(args):
    return _fwd_reference(*[args[k] for k in FWD_PARAMS])


def _output_shape():
    out = _jax.eval_shape(lambda: _forward(_fwd_setup_inputs(0)))
    return out.shape, out.dtype

N_MICROBATCH = 1
ADAM_LR = 0.001
ADAM_B1 = 0.9
ADAM_B2 = 0.999
ADAM_EPS = 1e-08
ADAM_WD = 0.01
ADAM_STEP = 10
PER_EXAMPLE_BATCH_AXIS = {'x': 0, 'loss_target': 0}
SHARED_INPUTS = []
_WEIGHT_DTYPES = {'norm_even': _jnp.float32, 'w_in_even': _jnp.float32, 'conv_a_w': _jnp.float32, 'conv_a_b': _jnp.float32, 'ln_a_g': _jnp.float32, 'ln_a_b': _jnp.float32, 'pool_w': _jnp.float32, 'pool_b': _jnp.float32, 'pool_scale': _jnp.float32, 'w_out_even': _jnp.float32, 'norm_odd': _jnp.float32, 'w_in_odd': _jnp.float32, 'conv_c_w': _jnp.float32, 'conv_c_b': _jnp.float32, 'w_rg': _jnp.float32, 'b_rg': _jnp.float32, 'w_ig': _jnp.float32, 'b_ig': _jnp.float32, 'lru_lambda': _jnp.float32, 'w_out_odd': _jnp.float32, 'final_norm': _jnp.float32}
MOMENT_SCALE = {'norm_even': 8.367462e-02, 'w_in_even': 3.574598e-02, 'conv_a_w': 3.185415e-02, 'conv_a_b': 8.044798e-02, 'ln_a_g': 4.365497e-02, 'ln_a_b': 4.743934e-02, 'pool_w': 4.531431e-02, 'pool_b': 7.857487e-02, 'pool_scale': 4.544444e-02, 'w_out_even': 5.622594e-02, 'norm_odd': 6.782498e-02, 'w_in_odd': 4.053492e-02, 'conv_c_w': 4.220113e-02, 'conv_c_b': 4.372314e-01, 'w_rg': 1.257923e-02, 'b_rg': 9.898503e-03, 'w_ig': 2.253242e-02, 'b_ig': 1.566541e-02, 'lru_lambda': 2.032671e-02, 'w_out_odd': 5.161129e-02, 'final_norm': 1.603404e+01}


def _to_microbatches(a, axis):
    t = _jnp.moveaxis(a, axis, 0)
    t = t.reshape((N_MICROBATCH, t.shape[0] // N_MICROBATCH) + t.shape[1:])
    return _jnp.moveaxis(t, 1, axis + 1)


def setup_inputs(seed: int = 0) -> dict:
    inp = _fwd_setup_inputs(seed)
    key = _jax.random.fold_in(_jax.random.key(seed), 7919)
    shape, _ = _output_shape()
    out = dict(inp)
    out["loss_target"] = _jax.random.normal(_jax.random.fold_in(key, 0), shape, _jnp.float32)
    for i, name in enumerate(TWIN_WEIGHTS):
        w = inp[name].astype(_jnp.float32)
        if MOMENT_SCALE is None:
            s = _jnp.sqrt(_jnp.mean(_jnp.square(w)) + 1e-30)
        else:
            s = MOMENT_SCALE[name]
        km, kv = _jax.random.split(_jax.random.fold_in(key, i + 1))
        out[name] = w
        out["m_" + name] = s * _jax.random.normal(km, w.shape, _jnp.float32)
        out["v_" + name] = (s * s) * _jax.random.uniform(kv, w.shape, _jnp.float32, 0.5, 1.5)
    if N_MICROBATCH > 1:
        for name, axis in PER_EXAMPLE_BATCH_AXIS.items():
            out[name] = _to_microbatches(out[name], axis)
    return {'x': out['x'], 'norm_even': out['norm_even'], 'w_in_even': out['w_in_even'], 'conv_a_w': out['conv_a_w'], 'conv_a_b': out['conv_a_b'], 'ln_a_g': out['ln_a_g'], 'ln_a_b': out['ln_a_b'], 'pool_w': out['pool_w'], 'pool_b': out['pool_b'], 'pool_scale': out['pool_scale'], 'w_out_even': out['w_out_even'], 'norm_odd': out['norm_odd'], 'w_in_odd': out['w_in_odd'], 'conv_c_w': out['conv_c_w'], 'conv_c_b': out['conv_c_b'], 'w_rg': out['w_rg'], 'b_rg': out['b_rg'], 'w_ig': out['w_ig'], 'b_ig': out['b_ig'], 'lru_lambda': out['lru_lambda'], 'w_out_odd': out['w_out_odd'], 'final_norm': out['final_norm'], 'loss_target': out['loss_target'], 'm_norm_even': out['m_norm_even'], 'm_w_in_even': out['m_w_in_even'], 'm_conv_a_w': out['m_conv_a_w'], 'm_conv_a_b': out['m_conv_a_b'], 'm_ln_a_g': out['m_ln_a_g'], 'm_ln_a_b': out['m_ln_a_b'], 'm_pool_w': out['m_pool_w'], 'm_pool_b': out['m_pool_b'], 'm_pool_scale': out['m_pool_scale'], 'm_w_out_even': out['m_w_out_even'], 'm_norm_odd': out['m_norm_odd'], 'm_w_in_odd': out['m_w_in_odd'], 'm_conv_c_w': out['m_conv_c_w'], 'm_conv_c_b': out['m_conv_c_b'], 'm_w_rg': out['m_w_rg'], 'm_b_rg': out['m_b_rg'], 'm_w_ig': out['m_w_ig'], 'm_b_ig': out['m_b_ig'], 'm_lru_lambda': out['m_lru_lambda'], 'm_w_out_odd': out['m_w_out_odd'], 'm_final_norm': out['m_final_norm'], 'v_norm_even': out['v_norm_even'], 'v_w_in_even': out['v_w_in_even'], 'v_conv_a_w': out['v_conv_a_w'], 'v_conv_a_b': out['v_conv_a_b'], 'v_ln_a_g': out['v_ln_a_g'], 'v_ln_a_b': out['v_ln_a_b'], 'v_pool_w': out['v_pool_w'], 'v_pool_b': out['v_pool_b'], 'v_pool_scale': out['v_pool_scale'], 'v_w_out_even': out['v_w_out_even'], 'v_norm_odd': out['v_norm_odd'], 'v_w_in_odd': out['v_w_in_odd'], 'v_conv_c_w': out['v_conv_c_w'], 'v_conv_c_b': out['v_conv_c_b'], 'v_w_rg': out['v_w_rg'], 'v_b_rg': out['v_b_rg'], 'v_w_ig': out['v_w_ig'], 'v_b_ig': out['v_b_ig'], 'v_lru_lambda': out['v_lru_lambda'], 'v_w_out_odd': out['v_w_out_odd'], 'v_final_norm': out['v_final_norm']}


def _loss(weights, diff, rest, loss_target):
    with _jax.named_scope("forward"):
        args = {**rest, TWIN_DIFF_INPUT: diff, **{k: w.astype(_WEIGHT_DTYPES[k]) for k, w in weights.items()}}
        y = _forward(args)
    with _jax.named_scope("loss_head"):
        err = _jnp.square(y.astype(_jnp.float32) - loss_target)
        return 0.5 * _jnp.sum(_jnp.mean(err, axis=-1)) if err.ndim else 0.5 * err


def _adamw(w, g, m, v):
    m = ADAM_B1 * m + (1.0 - ADAM_B1) * g
    v = ADAM_B2 * v + (1.0 - ADAM_B2) * _jnp.square(g)
    m_hat = m / (1.0 - ADAM_B1 ** ADAM_STEP)
    v_hat = v / (1.0 - ADAM_B2 ** ADAM_STEP)
    delta = -ADAM_LR * (m_hat / (_jnp.sqrt(v_hat) + ADAM_EPS) + ADAM_WD * w)
    return delta, m, v


def reference(x, norm_even, w_in_even, conv_a_w, conv_a_b, ln_a_g, ln_a_b, pool_w, pool_b, pool_scale, w_out_even, norm_odd, w_in_odd, conv_c_w, conv_c_b, w_rg, b_rg, w_ig, b_ig, lru_lambda, w_out_odd, final_norm, loss_target, m_norm_even, m_w_in_even, m_conv_a_w, m_conv_a_b, m_ln_a_g, m_ln_a_b, m_pool_w, m_pool_b, m_pool_scale, m_w_out_even, m_norm_odd, m_w_in_odd, m_conv_c_w, m_conv_c_b, m_w_rg, m_b_rg, m_w_ig, m_b_ig, m_lru_lambda, m_w_out_odd, m_final_norm, v_norm_even, v_w_in_even, v_conv_a_w, v_conv_a_b, v_ln_a_g, v_ln_a_b, v_pool_w, v_pool_b, v_pool_scale, v_w_out_even, v_norm_odd, v_w_in_odd, v_conv_c_w, v_conv_c_b, v_w_rg, v_b_rg, v_w_ig, v_b_ig, v_lru_lambda, v_w_out_odd, v_final_norm):
    given = dict(x=x, norm_even=norm_even, w_in_even=w_in_even, conv_a_w=conv_a_w, conv_a_b=conv_a_b, ln_a_g=ln_a_g, ln_a_b=ln_a_b, pool_w=pool_w, pool_b=pool_b, pool_scale=pool_scale, w_out_even=w_out_even, norm_odd=norm_odd, w_in_odd=w_in_odd, conv_c_w=conv_c_w, conv_c_b=conv_c_b, w_rg=w_rg, b_rg=b_rg, w_ig=w_ig, b_ig=b_ig, lru_lambda=lru_lambda, w_out_odd=w_out_odd, final_norm=final_norm, loss_target=loss_target, m_norm_even=m_norm_even, m_w_in_even=m_w_in_even, m_conv_a_w=m_conv_a_w, m_conv_a_b=m_conv_a_b, m_ln_a_g=m_ln_a_g, m_ln_a_b=m_ln_a_b, m_pool_w=m_pool_w, m_pool_b=m_pool_b, m_pool_scale=m_pool_scale, m_w_out_even=m_w_out_even, m_norm_odd=m_norm_odd, m_w_in_odd=m_w_in_odd, m_conv_c_w=m_conv_c_w, m_conv_c_b=m_conv_c_b, m_w_rg=m_w_rg, m_b_rg=m_b_rg, m_w_ig=m_w_ig, m_b_ig=m_b_ig, m_lru_lambda=m_lru_lambda, m_w_out_odd=m_w_out_odd, m_final_norm=m_final_norm, v_norm_even=v_norm_even, v_w_in_even=v_w_in_even, v_conv_a_w=v_conv_a_w, v_conv_a_b=v_conv_a_b, v_ln_a_g=v_ln_a_g, v_ln_a_b=v_ln_a_b, v_pool_w=v_pool_w, v_pool_b=v_pool_b, v_pool_scale=v_pool_scale, v_w_out_even=v_w_out_even, v_norm_odd=v_norm_odd, v_w_in_odd=v_w_in_odd, v_conv_c_w=v_conv_c_w, v_conv_c_b=v_conv_c_b, v_w_rg=v_w_rg, v_b_rg=v_b_rg, v_w_ig=v_w_ig, v_b_ig=v_b_ig, v_lru_lambda=v_lru_lambda, v_w_out_odd=v_w_out_odd, v_final_norm=v_final_norm)
    weights = {n: given[n] for n in TWIN_WEIGHTS}
    shared = {n: given[n] for n in SHARED_INPUTS}
    per_example = {n: given[n] for n in ['x']}
    grad_fn = _jax.value_and_grad(_loss, argnums=(0, 1))

    def one_microbatch(ex, loss_target):
        ex = dict(ex)
        diff = ex.pop(TWIN_DIFF_INPUT)
        return grad_fn(weights, diff, {**shared, **ex}, loss_target)

    if N_MICROBATCH == 1:
        loss, (grad_w, grad_x) = one_microbatch(per_example, given["loss_target"])
    else:
        def body(carry, xs):
            loss_sum, grad_sum = carry
            l_k, (gw_k, gx_k) = one_microbatch(xs[0], xs[1])
            with _jax.named_scope("update"):
                return (loss_sum + l_k, _jax.tree.map(_jnp.add, grad_sum, gw_k)), gx_k

        init = (_jnp.zeros((), _jnp.float32), _jax.tree.map(_jnp.zeros_like, weights))
        (loss, grad_w), grad_x = _jax.lax.scan(body, init, (per_example, given["loss_target"]))
    with _jax.named_scope("update"):
        delta_w, new_m, new_v = {}, {}, {}
        for n in TWIN_WEIGHTS:
            delta_w[n], new_m[n], new_v[n] = _adamw(weights[n], grad_w[n], given["m_" + n], given["v_" + n])
    return (loss, grad_x, *[grad_w[n] for n in TWIN_WEIGHTS], *[delta_w[n] for n in TWIN_WEIGHTS],
            *[new_m[n] for n in TWIN_WEIGHTS], *[new_v[n] for n in TWIN_WEIGHTS])
```

```python
import functools

import jax
import jax.numpy as jnp
from jax import lax
from jax.experimental import pallas as pl
from jax.experimental.pallas import tpu as pltpu

F32 = jnp.float32
BF16 = jnp.bfloat16
MESH = pl.DeviceIdType.MESH

D_MODEL = 1024
N_CHIPS = 4
N_DEV = 8
EPS_RMS = 1e-6
EPS_LN = 1e-5
CONV_K = 31
POOL_WINDOWS = (2, 4, 8, 16)
POOL_GW = 256
LRU_HEADS = 12
LRU_HD = 128
W_LRU = LRU_HEADS * LRU_HD
LRU_CONV_K = 4
LRU_C = 8.0
ADAM_LR = 0.001
ADAM_B1 = 0.9
ADAM_B2 = 0.999
ADAM_EPS = 1e-08
ADAM_WD = 0.01
ADAM_STEP = 10

VMEM_LIMIT_BYTES = 56 * 1024 * 1024
ROW_TILE = 512
MIX_TILE = 256
EVEN_HALO = 32
ODD_HALO = 8


def _pallas(body, **kw):
    return pl.pallas_call(body, **kw)


def _params(*sem):
    return pltpu.CompilerParams(dimension_semantics=sem if sem else None, vmem_limit_bytes=VMEM_LIMIT_BYTES)


def _sigmoid(x):
    return 1.0 / (1.0 + jnp.exp(-x))


def _dsilu(x, s):
    return s * (1.0 + x * (1.0 - s))


def _nt(a, b):
    return lax.dot_general(a, b, (((1,), (1,)), ((), ())), preferred_element_type=F32)


def _tn(a, b):
    return lax.dot_general(a, b, (((0,), (0,)), ((), ())), preferred_element_type=F32)


def _in_proj(h, g, wg, layer, name):
    T, D = h.shape
    _, nblk, _, nb = wg.shape

    def body(h_ref, g_ref, w_ref, p_ref, n_ref):
        @pl.when(pl.program_id(1) == 0)
        def _():
            x = h_ref[...]
            r = lax.rsqrt(jnp.mean(x * x, axis=-1, keepdims=True) + EPS_RMS)
            n_ref[...] = (x * r * g_ref[...]).astype(BF16)

        p_ref[...] = jnp.dot(n_ref[...], w_ref[0], preferred_element_type=F32)

    return _pallas(
        body, name=name, grid=(T // ROW_TILE, nblk),
        in_specs=[pl.BlockSpec((ROW_TILE, D), lambda i, j: (i, 0)), pl.BlockSpec((1, D), lambda i, j: (0, 0)),
                  pl.BlockSpec((None, 1, D, nb), lambda i, j: (layer, j, 0, 0))],
        out_specs=[pl.BlockSpec((ROW_TILE, nb), lambda i, j: (i, j)), pl.BlockSpec((ROW_TILE, D), lambda i, j: (i, 0))],
        out_shape=[jax.ShapeDtypeStruct((T, nblk * nb), F32), jax.ShapeDtypeStruct((T, D), BF16)],
        compiler_params=_params("parallel", "arbitrary"))(h, g, wg)


def _out_proj(y, w, layer, hres, name):
    T, K = y.shape
    D = w.shape[2]

    def body(y_ref, w_ref, r_ref, o_ref):
        o_ref[...] = r_ref[...] + jnp.dot(y_ref[...], w_ref[...], preferred_element_type=F32)

    return _pallas(
        body, name=name, grid=(T // ROW_TILE,),
        in_specs=[pl.BlockSpec((ROW_TILE, K), lambda i: (i, 0)), pl.BlockSpec((None, K, D), lambda i: (layer, 0, 0)),
                  pl.BlockSpec((ROW_TILE, D), lambda i: (i, 0))],
        out_specs=pl.BlockSpec((ROW_TILE, D), lambda i: (i, 0)),
        out_shape=jax.ShapeDtypeStruct((T, D), F32),
        compiler_params=_params("parallel"))(y, w, hres)


def _dy_proj(dout, w, layer, name):
    T, D = dout.shape
    K = w.shape[1]

    def body(d_ref, w_ref, o_ref):
        o_ref[...] = _nt(d_ref[...], w_ref[...])

    return _pallas(
        body, name=name, grid=(T // ROW_TILE,),
        in_specs=[pl.BlockSpec((ROW_TILE, D), lambda i: (i, 0)), pl.BlockSpec((None, K, D), lambda i: (layer, 0, 0))],
        out_specs=pl.BlockSpec((ROW_TILE, K), lambda i: (i, 0)),
        out_shape=jax.ShapeDtypeStruct((T, K), F32),
        compiler_params=_params("parallel"))(dout, w)


def _dn_proj(dp, wg, layer, h, g, dres, name):
    T, D = h.shape
    _, nblk, _, nb = wg.shape

    def body(dp_ref, w_ref, h_ref, g_ref, dres_ref, dh_ref, dhb_ref, dg_ref, acc_ref):
        i, j = pl.program_id(0), pl.program_id(1)
        part = _nt(dp_ref[...], w_ref[0])

        @pl.when(j == 0)
        def _():
            acc_ref[...] = part

        @pl.when(j > 0)
        def _():
            acc_ref[...] += part

        @pl.when(j == nblk - 1)
        def _():
            x = h_ref[...]
            r = lax.rsqrt(jnp.mean(x * x, axis=-1, keepdims=True) + EPS_RMS)
            dn = acc_ref[...]
            q = dn * g_ref[...]
            dh = dres_ref[...] + r * q - x * ((r * r * r) * jnp.mean(q * x, axis=-1, keepdims=True))
            dh_ref[...] = dh
            dhb_ref[...] = dh.astype(BF16)
            dgp = jnp.sum(dn * (x * r), axis=0, keepdims=True)

            @pl.when(i == 0)
            def _():
                dg_ref[...] = dgp

            @pl.when(i > 0)
            def _():
                dg_ref[...] += dgp

    return _pallas(
        body, name=name, grid=(T // ROW_TILE, nblk),
        in_specs=[pl.BlockSpec((ROW_TILE, nb), lambda i, j: (i, j)),
                  pl.BlockSpec((None, 1, D, nb), lambda i, j: (layer, j, 0, 0)),
                  pl.BlockSpec((ROW_TILE, D), lambda i, j: (i, 0)), pl.BlockSpec((1, D), lambda i, j: (0, 0)),
                  pl.BlockSpec((ROW_TILE, D), lambda i, j: (i, 0))],
        out_specs=[pl.BlockSpec((ROW_TILE, D), lambda i, j: (i, 0)), pl.BlockSpec((ROW_TILE, D), lambda i, j: (i, 0)),
                   pl.BlockSpec((1, D), lambda i, j: (0, 0))],
        out_shape=[jax.ShapeDtypeStruct((T, D), F32), jax.ShapeDtypeStruct((T, D), BF16),
                   jax.ShapeDtypeStruct((1, D), F32)],
        scratch_shapes=[pltpu.VMEM((ROW_TILE, D), F32)],
        compiler_params=_params("arbitrary", "arbitrary"))(dp, wg, h, g, dres)


def _dw_in(n, dp, nblk, name):
    T, D = n.shape
    nb = dp.shape[1] // nblk
    ta = 512

    def body(n_ref, dp_ref, o_ref, ob_ref):
        acc = _tn(n_ref[...], dp_ref[...])
        o_ref[0] = acc
        ob_ref[0] = acc.astype(BF16)

    return _pallas(
        body, name=name, grid=(nblk, D // ta),
        in_specs=[pl.BlockSpec((T, ta), lambda j, i: (0, i)), pl.BlockSpec((T, nb), lambda j, i: (0, j))],
        out_specs=[pl.BlockSpec((1, ta, nb), lambda j, i: (j, i, 0)), pl.BlockSpec((1, ta, nb), lambda j, i: (j, i, 0))],
        out_shape=[jax.ShapeDtypeStruct((nblk, D, nb), F32), jax.ShapeDtypeStruct((nblk, D, nb), BF16)],
        compiler_params=_params("parallel", "parallel"))(n, dp)


def _dw_out(y, dout, name):
    T, K = y.shape
    D = dout.shape[1]
    tk = 512

    def body(y_ref, d_ref, o_ref, ob_ref):
        acc = _tn(y_ref[...], d_ref[...])
        o_ref[...] = acc
        ob_ref[...] = acc.astype(BF16)

    return _pallas(
        body, name=name, grid=(K // tk,),
        in_specs=[pl.BlockSpec((T, tk), lambda i: (0, i)), pl.BlockSpec((T, D), lambda i: (0, 0))],
        out_specs=[pl.BlockSpec((tk, D), lambda i: (i, 0)), pl.BlockSpec((tk, D), lambda i: (i, 0))],
        out_shape=[jax.ShapeDtypeStruct((K, D), F32), jax.ShapeDtypeStruct((K, D), BF16)],
        compiler_params=_params("parallel"))(y, dout)


def _loss_head(h, g, tgt):
    T, D = h.shape
    tm = MIX_TILE

    def body(h_ref, g_ref, t_ref, dh_ref, dhb_ref, dg_ref, loss_ref):
        i = pl.program_id(0)
        x = h_ref[...]
        gg = g_ref[...]
        r = lax.rsqrt(jnp.mean(x * x, axis=-1, keepdims=True) + EPS_RMS)
        xr = x * r
        e = xr * gg - t_ref[...]
        lp = 0.5 * jnp.sum(jnp.mean(e * e, axis=-1, keepdims=True), axis=0, keepdims=True)
        dn = e * (1.0 / D)
        q = dn * gg
        dh = r * q - x * ((r * r * r) * jnp.mean(q * x, axis=-1, keepdims=True))
        dh_ref[...] = dh
        dhb_ref[...] = dh.astype(BF16)
        dgp = jnp.sum(dn * xr, axis=0, keepdims=True)

        @pl.when(i == 0)
        def _():
            dg_ref[...] = dgp
            loss_ref[...] = lp

        @pl.when(i > 0)
        def _():
            dg_ref[...] += dgp
            loss_ref[...] += lp

    return _pallas(
        body, name="loss_head", grid=(T // tm,),
        in_specs=[pl.BlockSpec((tm, D), lambda i: (i, 0)), pl.BlockSpec((1, D), lambda i: (0, 0)),
                  pl.BlockSpec((tm, D), lambda i: (i, 0))],
        out_specs=[pl.BlockSpec((tm, D), lambda i: (i, 0)), pl.BlockSpec((tm, D), lambda i: (i, 0)),
                   pl.BlockSpec((1, D), lambda i: (0, 0)), pl.BlockSpec((1, 1), lambda i: (0, 0))],
        out_shape=[jax.ShapeDtypeStruct((T, D), F32), jax.ShapeDtypeStruct((T, D), BF16),
                   jax.ShapeDtypeStruct((1, D), F32), jax.ShapeDtypeStruct((1, 1), F32)],
        compiler_params=_params("arbitrary"))(h, g, tgt)


def _shift_up(x, j):
    return x if j == 0 else pltpu.roll(x, x.shape[0] - j, 0)


def _shift_down(x, j):
    return x if j == 0 else pltpu.roll(x, j, 0)


def _fill_shifted(dst_ref, x):
    rows = dst_ref.shape[1]
    for s in range(8):
        dst_ref[s] = _shift_up(x, s)[0:rows]


def _tap_sum(sh_ref, w_ref, r0, nrows, offsets):
    acc = None
    for k, o in enumerate(offsets):
        win = sh_ref[o % 8, pl.ds(r0 + (o // 8) * 8, nrows), :]
        term = w_ref[k:k + 1, :] * win
        acc = term if acc is None else acc + term
    return acc


def _pool_sums(vx, up):
    sh = _shift_up if up else _shift_down
    outs = []
    for gi, w in enumerate(POOL_WINDOWS):
        s = vx[:, gi * POOL_GW:(gi + 1) * POOL_GW]
        j = 1
        while j < w:
            s = s + sh(s, j)
            j *= 2
        outs.append(s)
    return outs


def _inv_count(row0, nrows):
    pos = (row0 + 1 + lax.broadcasted_iota(jnp.int32, (nrows, 1), 0)).astype(F32)
    return [1.0 / jnp.minimum(pos, float(w)) for w in POOL_WINDOWS]


def _even_mixer_fwd(p, cw, cb, lg, lb, pw, pb, sc, name):
    T = p.shape[0]
    C = D_MODEL
    tT, HL = MIX_TILE, EVEN_HALO
    hb = tT // HL
    chunk = 16

    def body(pm_ref, ph_ref, cw_ref, cb_ref, lg_ref, lb_ref, pw_ref, pb_ref, sc_ref, y_ref, u1_ref, u0x_ref, sh_ref):
        i = pl.program_id(0)
        keep = (i > 0).astype(F32)
        u0x_ref[0:HL] = ph_ref[:, 0:C] * _sigmoid(ph_ref[:, C:2 * C]) * keep
        u0x_ref[HL:HL + tT] = pm_ref[:, 0:C] * _sigmoid(pm_ref[:, C:2 * C])
        u0x_ref[HL + tT:HL + tT + 8] = jnp.zeros((8, C), F32)
        _fill_shifted(sh_ref, u0x_ref[...])
        offs = [HL - (CONV_K - 1) + k for k in range(CONV_K)]

        def conv_chunk(c, carry):
            r0 = pl.multiple_of(c * chunk, chunk)
            u1_ref[pl.ds(r0, chunk), :] = _tap_sum(sh_ref, cw_ref, r0, chunk, offs) + cb_ref[...]
            return carry

        lax.fori_loop(0, tT // chunk, conv_chunk, 0)
        u1 = u1_ref[...]
        mu = jnp.mean(u1, axis=-1, keepdims=True)
        xc = u1 - mu
        rs = lax.rsqrt(jnp.mean(xc * xc, axis=-1, keepdims=True) + EPS_LN)
        u2 = xc * rs * lg_ref[...] + lb_ref[...]
        u3 = u2 * _sigmoid(u2)
        ag = pm_ref[:, 2 * C:3 * C]
        y_ref[:, 0:C] = (u3 * (ag * _sigmoid(ag))).astype(BF16)
        vx = jnp.concatenate([ph_ref[:, 3 * C:4 * C] * keep, pm_ref[:, 3 * C:4 * C]], axis=0)
        sums = _pool_sums(vx, up=False)
        inv = _inv_count(i * tT, tT)
        for gi in range(len(POOL_WINDOWS)):
            cols = slice(gi * POOL_GW, (gi + 1) * POOL_GW)
            d0 = sums[gi][HL:] * inv[gi] - vx[HL:, cols]
            d1 = jnp.dot(d0.astype(BF16), pw_ref[gi], preferred_element_type=F32) + pb_ref[:, cols]
            bg = pm_ref[:, 4 * C + gi * POOL_GW:4 * C + (gi + 1) * POOL_GW]
            y_ref[:, C + gi * POOL_GW:C + (gi + 1) * POOL_GW] = (d1 * sc_ref[:, cols] * (bg * _sigmoid(bg))).astype(BF16)

    vec = pl.BlockSpec((1, C), lambda i: (0, 0))
    return _pallas(
        body, name=name, grid=(T // tT,),
        in_specs=[pl.BlockSpec((tT, 5 * C), lambda i: (i, 0)),
                  pl.BlockSpec((HL, 5 * C), lambda i: (jnp.maximum(i * hb - 1, 0), 0)),
                  pl.BlockSpec((32, C), lambda i: (0, 0)), vec, vec, vec,
                  pl.BlockSpec((4, POOL_GW, POOL_GW), lambda i: (0, 0, 0)), vec, vec],
        out_specs=[pl.BlockSpec((tT, 2 * C), lambda i: (i, 0)), pl.BlockSpec((tT, C), lambda i: (i, 0))],
        out_shape=[jax.ShapeDtypeStruct((T, 2 * C), BF16), jax.ShapeDtypeStruct((T, C), F32)],
        scratch_shapes=[pltpu.VMEM((HL + tT + 8, C), F32), pltpu.VMEM((8, HL + tT, C), F32)],
        compiler_params=_params("parallel"))(p, p, cw, cb, lg, lb, pw, pb, sc)


def _even_mixer_bwd(p, u1, dy, cw, cwr, lg, lb, pw, pb, sc, name):
    T = p.shape[0]
    C = D_MODEL
    tT, HL = MIX_TILE, EVEN_HALO
    hb = tT // HL
    nT = T // tT
    R1 = tT + HL
    chunk = 16

    def body(pm_ref, pp_ref, pn_ref, u1m_ref, u1n_ref, dym_ref, dyn_ref, cw_ref, cwr_ref, lg_ref, lb_ref, pw_ref,
             pb_ref, sc_ref, dp_ref, dcw_ref, dvec_ref, dpw_ref, x_ref, sh_ref, du0_ref):
        i = pl.program_id(0)
        keep_prev = (i > 0).astype(F32)
        keep_next = (i < nT - 1).astype(F32)
        row = lax.broadcasted_iota(jnp.int32, (R1, 1), 0)
        live = jnp.where(row < tT, 1.0, keep_next)

        def cat(m, n):
            return jnp.concatenate([m, n], axis=0)

        u1 = cat(u1m_ref[...], u1n_ref[...])
        mu = jnp.mean(u1, axis=-1, keepdims=True)
        xc = u1 - mu
        rs = lax.rsqrt(jnp.mean(xc * xc, axis=-1, keepdims=True) + EPS_LN)
        xh = xc * rs
        u2 = xh * lg_ref[...] + lb_ref[...]
        s2 = _sigmoid(u2)
        u3 = u2 * s2
        ag = cat(pm_ref[:, 2 * C:3 * C], pn_ref[:, 2 * C:3 * C])
        sa = _sigmoid(ag)
        dya = cat(dym_ref[:, 0:C], dyn_ref[:, 0:C])
        dp_ref[:, 2 * C:3 * C] = (dya * u3 * _dsilu(ag, sa))[0:tT].astype(BF16)
        du2 = dya * (ag * sa) * _dsilu(u2, s2)
        dlg = jnp.sum((du2 * xh)[0:tT], axis=0, keepdims=True)
        dlb = jnp.sum(du2[0:tT], axis=0, keepdims=True)
        dxh = du2 * lg_ref[...]
        du1 = rs * (dxh - jnp.mean(dxh, axis=-1, keepdims=True) - xh * jnp.mean(dxh * xh, axis=-1, keepdims=True))
        du1 = du1 * live
        dcb = jnp.sum(du1[0:tT], axis=0, keepdims=True)
        x_ref[0:R1] = du1
        x_ref[R1:R1 + 8] = jnp.zeros((8, C), F32)
        _fill_shifted(sh_ref, x_ref[...])

        def du0_chunk(c, carry):
            r0 = pl.multiple_of(c * chunk, chunk)
            du0_ref[pl.ds(r0, chunk), :] = _tap_sum(sh_ref, cwr_ref, r0, chunk, list(range(CONV_K)))
            return carry

        lax.fori_loop(0, tT // chunk, du0_chunk, 0)
        av, agl = pm_ref[:, 0:C], pm_ref[:, C:2 * C]
        sg = _sigmoid(agl)
        du0 = du0_ref[...]
        dp_ref[:, 0:C] = (du0 * sg).astype(BF16)
        dp_ref[:, C:2 * C] = (du0 * av * sg * (1.0 - sg)).astype(BF16)
        du0_ref[...] = du1[0:tT]
        x_ref[0:HL] = pp_ref[:, 0:C] * _sigmoid(pp_ref[:, C:2 * C]) * keep_prev
        x_ref[HL:HL + tT] = av * sg
        x_ref[HL + tT:HL + tT + 8] = jnp.zeros((8, C), F32)
        _fill_shifted(sh_ref, x_ref[...])

        @pl.when(i == 0)
        def _():
            dcw_ref[...] = jnp.zeros_like(dcw_ref)

        for k in range(CONV_K):
            o = HL - (CONV_K - 1) + k

            def dw_chunk(c, acc, o=o):
                r0 = pl.multiple_of(c * 8, 8)
                return acc + du0_ref[pl.ds(r0, 8), :] * sh_ref[o % 8, pl.ds(r0 + (o // 8) * 8, 8), :]

            dcw_ref[8 * k:8 * k + 8, :] += lax.fori_loop(0, tT // 8, dw_chunk, jnp.zeros((8, C), F32))

        bg = cat(pm_ref[:, 4 * C:5 * C], pn_ref[:, 4 * C:5 * C])
        sb = _sigmoid(bg)
        dyb = cat(dym_ref[:, C:2 * C], dyn_ref[:, C:2 * C])
        dyb0 = dyb * (bg * sb)
        dd1 = dyb0 * sc_ref[...]
        dpb = jnp.sum(dd1[0:tT], axis=0, keepdims=True)
        inv1 = _inv_count(i * tT, R1)
        z_parts, dd0_parts = [], []
        for gi in range(len(POOL_WINDOWS)):
            cols = slice(gi * POOL_GW, (gi + 1) * POOL_GW)
            dd0 = _nt(dd1[:, cols].astype(BF16), pw_ref[gi])
            dd0_parts.append(dd0)
            z_parts.append(dd0 * inv1[gi] * live)
        fsum = _pool_sums(jnp.concatenate(z_parts, axis=1), up=True)
        vx = cat(pp_ref[:, 3 * C:4 * C] * keep_prev, pm_ref[:, 3 * C:4 * C])
        sums = _pool_sums(vx, up=False)
        inv0 = _inv_count(i * tT, tT)
        dsc_parts = []
        for gi in range(len(POOL_WINDOWS)):
            cols = slice(gi * POOL_GW, (gi + 1) * POOL_GW)
            dp_ref[:, 3 * C + gi * POOL_GW:3 * C + (gi + 1) * POOL_GW] = (fsum[gi][0:tT] - dd0_parts[gi][0:tT]).astype(BF16)
            d0 = (sums[gi][HL:] * inv0[gi] - vx[HL:, cols]).astype(BF16)
            d1 = jnp.dot(d0, pw_ref[gi], preferred_element_type=F32) + pb_ref[:, cols]
            bgm, sbm = bg[0:tT, cols], sb[0:tT, cols]
            dp_ref[:, 4 * C + gi * POOL_GW:4 * C + (gi + 1) * POOL_GW] = (
                dyb[0:tT, cols] * d1 * sc_ref[:, cols] * _dsilu(bgm, sbm)).astype(BF16)
            dsc_parts.append(jnp.sum(dyb0[0:tT, cols] * d1, axis=0, keepdims=True))
            dpw_g = _tn(d0, dd1[0:tT, cols].astype(BF16))

            @pl.when(i == 0)
            def _(gi=gi, dpw_g=dpw_g):
                dpw_ref[gi] = dpw_g

            @pl.when(i > 0)
            def _(gi=gi, dpw_g=dpw_g):
                dpw_ref[gi] += dpw_g

        dsc = jnp.concatenate(dsc_parts, axis=1)
        vecs = jnp.concatenate([dcb, dlg, dlb, dsc, dpb, jnp.zeros((3, C), F32)], axis=0)

        @pl.when(i == 0)
        def _():
            dvec_ref[...] = vecs

        @pl.when(i > 0)
        def _():
            dvec_ref[...] += vecs

    vec = pl.BlockSpec((1, C), lambda i: (0, 0))
    taps = pl.BlockSpec((32, C), lambda i: (0, 0))

    def prev_blk(i):
        return (jnp.maximum(i * hb - 1, 0), 0)

    def next_blk(i):
        return (jnp.minimum((i + 1) * hb, T // HL - 1), 0)

    return _pallas(
        body, name=name, grid=(nT,),
        in_specs=[pl.BlockSpec((tT, 5 * C), lambda i: (i, 0)), pl.BlockSpec((HL, 5 * C), prev_blk),
                  pl.BlockSpec((HL, 5 * C), next_blk),
                  pl.BlockSpec((tT, C), lambda i: (i, 0)), pl.BlockSpec((HL, C), next_blk),
                  pl.BlockSpec((tT, 2 * C), lambda i: (i, 0)), pl.BlockSpec((HL, 2 * C), next_blk),
                  taps, taps, vec, vec, pl.BlockSpec((4, POOL_GW, POOL_GW), lambda i: (0, 0, 0)), vec, vec],
        out_specs=[pl.BlockSpec((tT, 5 * C), lambda i: (i, 0)), pl.BlockSpec((32 * 8, C), lambda i: (0, 0)),
                   pl.BlockSpec((8, C), lambda i: (0, 0)), pl.BlockSpec((4, POOL_GW, POOL_GW), lambda i: (0, 0, 0))],
        out_shape=[jax.ShapeDtypeStruct((T, 5 * C), BF16), jax.ShapeDtypeStruct((32 * 8, C), F32),
                   jax.ShapeDtypeStruct((8, C), F32), jax.ShapeDtypeStruct((4, POOL_GW, POOL_GW), F32)],
        scratch_shapes=[pltpu.VMEM((R1 + 8, C), F32), pltpu.VMEM((8, R1, C), F32), pltpu.VMEM((tT, C), F32)],
        compiler_params=_params("arbitrary"))(p, p, p, u1, u1, dy, dy, cw, cwr, lg, lb, pw, pb, sc)


def _softplus(z):
    u = jnp.exp(-jnp.abs(z))
    w = 1.0 + u
    l1p = jnp.where(w == 1.0, u, u * jnp.log(w) / jnp.where(w == 1.0, 1.0, w - 1.0))
    return jnp.maximum(z, 0.0) + l1p


def _lru_gates(xrx, cw_ref, cb_ref, wr_ref, br_ref, wi_ref, bi_ref, lam_ref):
    HL = ODD_HALO
    xc = cb_ref[...] + cw_ref[LRU_CONV_K - 1:LRU_CONV_K, :] * xrx[HL:]
    for k in range(LRU_CONV_K - 1):
        xc = xc + cw_ref[k:k + 1, :] * _shift_down(xrx, LRU_CONV_K - 1 - k)[HL:]
    xcb = xc.astype(BF16)
    rp, ip = [], []
    for hd in range(LRU_HEADS):
        cols = slice(hd * LRU_HD, (hd + 1) * LRU_HD)
        rp.append(jnp.dot(xcb[:, cols], wr_ref[hd], preferred_element_type=F32))
        ip.append(jnp.dot(xcb[:, cols], wi_ref[hd], preferred_element_type=F32))
    r = _sigmoid(jnp.concatenate(rp, axis=1) + br_ref[...])
    ig = _sigmoid(jnp.concatenate(ip, axis=1) + bi_ref[...])
    sp = _softplus(-lam_ref[...])
    log_a = (-LRU_C) * r * sp
    a = jnp.exp(log_a)
    mult = jnp.sqrt(-jnp.tanh(log_a) * (a * a + 1.0))
    return xc, xcb, r, ig, sp, a, mult


def _odd_mixer_fwd(p, cw, cb, wr, br, wi, bi, lam, name):
    T = p.shape[0]
    W = W_LRU
    tT, HL = MIX_TILE, ODD_HALO
    hb = tT // HL

    def body(pm_ref, ph_ref, cw_ref, cb_ref, wr_ref, br_ref, wi_ref, bi_ref, lam_ref, y_ref, hs_ref, carry_ref):
        i = pl.program_id(0)
        keep = (i > 0).astype(F32)

        @pl.when(i == 0)
        def _():
            carry_ref[...] = jnp.zeros_like(carry_ref)

        xrx = jnp.concatenate([ph_ref[:, 0:W] * keep, pm_ref[:, 0:W]], axis=0)
        xc, _, _, ig, _, a, mult = _lru_gates(xrx, cw_ref, cb_ref, wr_ref, br_ref, wi_ref, bi_ref, lam_ref)
        b = mult * (ig * xc)
        row = lax.broadcasted_iota(jnp.int32, (tT, 1), 0)
        s = 1
        while s < tT:
            ok = row >= s
            a_sh = jnp.where(ok, _shift_down(a, s), 1.0)
            b_sh = jnp.where(ok, _shift_down(b, s), 0.0)
            b = a * b_sh + b
            a = a * a_sh
            s *= 2
        hs = a * carry_ref[0:1, :] + b
        hs_ref[...] = hs
        carry_ref[...] = jnp.broadcast_to(hs[tT - 1:tT, :], (8, W))
        gt = pm_ref[:, W:2 * W]
        y_ref[...] = (hs * (gt * _sigmoid(gt))).astype(BF16)

    vec = pl.BlockSpec((1, W), lambda i: (0, 0))
    heads = pl.BlockSpec((LRU_HEADS, LRU_HD, LRU_HD), lambda i: (0, 0, 0))
    return _pallas(
        body, name=name, grid=(T // tT,),
        in_specs=[pl.BlockSpec((tT, 2 * W), lambda i: (i, 0)),
                  pl.BlockSpec((HL, 2 * W), lambda i: (jnp.maximum(i * hb - 1, 0), 0)),
                  pl.BlockSpec((8, W), lambda i: (0, 0)), vec, heads, vec, heads, vec, vec],
        out_specs=[pl.BlockSpec((tT, W), lambda i: (i, 0)), pl.BlockSpec((tT, W), lambda i: (i, 0))],
        out_shape=[jax.ShapeDtypeStruct((T, W), BF16), jax.ShapeDtypeStruct((T, W), F32)],
        scratch_shapes=[pltpu.VMEM((8, W), F32)],
        compiler_params=_params("arbitrary"))(p, p, cw, cb, wr, br, wi, bi, lam)


def _odd_mixer_bwd(p, hs, dy, cw, cb, wr, br, wi, bi, lam, name):
    T = p.shape[0]
    W = W_LRU
    tT, HL = MIX_TILE, ODD_HALO
    hb = tT // HL
    nT = T // tT

    def body(pm_ref, ph_ref, hsm_ref, hsh_ref, dy_ref, cw_ref, cb_ref, wr_ref, br_ref, wi_ref, bi_ref, lam_ref,
             dp_ref, dwr_ref, dwi_ref, dvec_ref, gcarry_ref, xcarry_ref):
        i = pl.program_id(0)
        keep = (i < nT - 1).astype(F32)

        @pl.when(i == 0)
        def _():
            gcarry_ref[...] = jnp.zeros_like(gcarry_ref)
            xcarry_ref[...] = jnp.zeros_like(xcarry_ref)

        xrx = jnp.concatenate([ph_ref[:, 0:W] * keep, pm_ref[:, 0:W]], axis=0)
        xc, xcb, r, ig, sp, a, mult = _lru_gates(xrx, cw_ref, cb_ref, wr_ref, br_ref, wi_ref, bi_ref, lam_ref)
        hs = hsm_ref[...]
        gt = pm_ref[:, W:2 * W]
        sg = _sigmoid(gt)
        dyv = dy_ref[...]
        dp_ref[:, W:2 * W] = (dyv * hs * _dsilu(gt, sg)).astype(BF16)
        row = lax.broadcasted_iota(jnp.int32, (tT, 1), 0)
        e = dyv * (gt * sg) + jnp.where(row == tT - 1, gcarry_ref[0:1, :], 0.0)
        m = jnp.where(row == tT - 1, 1.0, _shift_up(a, 1))
        s = 1
        while s < tT:
            ok = row < tT - s
            m_sh = jnp.where(ok, _shift_up(m, s), 1.0)
            e_sh = jnp.where(ok, _shift_up(e, s), 0.0)
            e = m * e_sh + e
            m = m * m_sh
            s *= 2
        G = e
        gcarry_ref[...] = jnp.broadcast_to(a[0:1, :] * G[0:1, :], (8, W))
        hs_prev = jnp.where(row == 0, hsh_ref[HL - 1:HL, :] * keep, _shift_down(hs, 1))
        da = G * hs_prev
        dmult = G * (ig * xc)
        di = G * mult * xc
        dxc = G * mult * ig
        dlog_a = da * a - dmult * (a * a) / mult
        drp = dlog_a * ((-LRU_C) * sp) * r * (1.0 - r)
        dip = di * ig * (1.0 - ig)
        dlam = jnp.sum(dlog_a * ((-LRU_C) * r), axis=0, keepdims=True) * (-_sigmoid(-lam_ref[...]))
        drb, dib = drp.astype(BF16), dip.astype(BF16)
        back = []
        for hd in range(LRU_HEADS):
            cols = slice(hd * LRU_HD, (hd + 1) * LRU_HD)
            back.append(_nt(drb[:, cols], wr_ref[hd]) + _nt(dib[:, cols], wi_ref[hd]))
            dwr_h = _tn(xcb[:, cols], drb[:, cols])
            dwi_h = _tn(xcb[:, cols], dib[:, cols])

            @pl.when(i == 0)
            def _(hd=hd, dwr_h=dwr_h, dwi_h=dwi_h):
                dwr_ref[hd] = dwr_h
                dwi_ref[hd] = dwi_h

            @pl.when(i > 0)
            def _(hd=hd, dwr_h=dwr_h, dwi_h=dwi_h):
                dwr_ref[hd] += dwr_h
                dwi_ref[hd] += dwi_h

        dxc = dxc + jnp.concatenate(back, axis=1)
        dxcx = jnp.concatenate([dxc, xcarry_ref[...]], axis=0)
        dxr = cw_ref[LRU_CONV_K - 1:LRU_CONV_K, :] * dxc
        rows = []
        for k in range(LRU_CONV_K - 1):
            j = LRU_CONV_K - 1 - k
            dxr = dxr + cw_ref[k:k + 1, :] * _shift_up(dxcx, j)[0:tT]
            rows.append(jnp.sum(dxc * _shift_down(xrx, j)[HL:], axis=0, keepdims=True))
        rows.append(jnp.sum(dxc * xrx[HL:], axis=0, keepdims=True))
        dp_ref[:, 0:W] = dxr.astype(BF16)
        xcarry_ref[...] = dxc[0:8]
        rows += [jnp.sum(dxc, axis=0, keepdims=True), jnp.sum(drp, axis=0, keepdims=True),
                 jnp.sum(dip, axis=0, keepdims=True), dlam]
        vecs = jnp.concatenate(rows, axis=0)

        @pl.when(i == 0)
        def _():
            dvec_ref[...] = vecs

        @pl.when(i > 0)
        def _():
            dvec_ref[...] += vecs

    vec = pl.BlockSpec((1, W), lambda i: (0, 0))
    heads = pl.BlockSpec((LRU_HEADS, LRU_HD, LRU_HD), lambda i: (0, 0, 0))

    def tile(i):
        return (nT - 1 - i, 0)

    def prev_blk(i):
        return (jnp.maximum((nT - 1 - i) * hb - 1, 0), 0)

    return _pallas(
        body, name=name, grid=(nT,),
        in_specs=[pl.BlockSpec((tT, 2 * W), tile), pl.BlockSpec((HL, 2 * W), prev_blk),
                  pl.BlockSpec((tT, W), tile), pl.BlockSpec((HL, W), prev_blk), pl.BlockSpec((tT, W), tile),
                  pl.BlockSpec((8, W), lambda i: (0, 0)), vec, heads, vec, heads, vec, vec],
        out_specs=[pl.BlockSpec((tT, 2 * W), tile), heads, heads, pl.BlockSpec((8, W), lambda i: (0, 0))],
        out_shape=[jax.ShapeDtypeStruct((T, 2 * W), BF16), jax.ShapeDtypeStruct((LRU_HEADS, LRU_HD, LRU_HD), F32),
                   jax.ShapeDtypeStruct((LRU_HEADS, LRU_HD, LRU_HD), F32), jax.ShapeDtypeStruct((8, W), F32)],
        scratch_shapes=[pltpu.VMEM((8, W), F32), pltpu.VMEM((8, W), F32)],
        compiler_params=_params("arbitrary"))(p, p, hs, hs, dy, cw, cb, wr, br, wi, bi, lam)


def _pad_rows(a, rows):
    return jnp.concatenate([a, jnp.zeros((rows - a.shape[0], a.shape[1]), a.dtype)], axis=0)


def _local_step(x, tgt, wt):
    even, odd = wt["even"], wt["odd"]
    depth = len(even) + len(odd)
    h = x
    saved = []
    for layer in range(depth):
        j = layer // 2
        if layer % 2 == 0:
            w = even[j]
            p, n = _in_proj(h, w["norm"], wt["w_in_even"], j, "in_proj_even")
            y, aux = _even_mixer_fwd(p, w["conv_w"], w["conv_b"], w["ln_g"], w["ln_b"], w["pool_w"], w["pool_b"],
                                     w["pool_scale"], "even_mixer_fwd")
            h_next = _out_proj(y, wt["w_out_even"], j, h, "out_proj_even")
        else:
            w = odd[j]
            p, n = _in_proj(h, w["norm"], wt["w_in_odd"], j, "in_proj_odd")
            y, aux = _odd_mixer_fwd(p, w["conv_w"], w["conv_b"], w["w_rg"], w["b_rg"], w["w_ig"], w["b_ig"], w["lam"],
                                    "odd_mixer_fwd")
            h_next = _out_proj(y, wt["w_out_odd"], j, h, "out_proj_odd")
        saved.append((h, n, p, aux, y))
        h = h_next
    dh, dhb, d_final, loss = _loss_head(h, wt["final_norm"], tgt)
    g_even = [None] * len(even)
    g_odd = [None] * len(odd)
    for layer in reversed(range(depth)):
        h, n, p, aux, y = saved[layer]
        j = layer // 2
        if layer % 2 == 0:
            w = even[j]
            dw_out, dw_out_b = _dw_out(y, dhb, "dw_out_even")
            dy = _dy_proj(dhb, wt["w_out_even"], j, "dy_proj_even")
            dp, dcw, dvec, dpw = _even_mixer_bwd(p, aux, dy, w["conv_w"], w["conv_w_rev"], w["ln_g"], w["ln_b"],
                                                 w["pool_w"], w["pool_b"], w["pool_scale"], "even_mixer_bwd")
            dw_in, dw_in_b = _dw_in(n, dp, N_CHIPS, "dw_in_even")
            dh, dhb, dnorm = _dn_proj(dp, wt["w_in_even"], j, h, w["norm"], dh, "dn_proj_even")
            g_even[j] = dict(w_in=dw_in, w_in_b=dw_in_b, w_out=dw_out, w_out_b=dw_out_b, conv_w=dcw, vec=dvec,
                             pool_w=dpw, norm=dnorm)
        else:
            w = odd[j]
            dw_out, dw_out_b = _dw_out(y, dhb, "dw_out_odd")
            dy = _dy_proj(dhb, wt["w_out_odd"], j, "dy_proj_odd")
            dp, dwr, dwi, dvec = _odd_mixer_bwd(p, aux, dy, w["conv_w"], w["conv_b"], w["w_rg"], w["b_rg"], w["w_ig"],
                                                w["b_ig"], w["lam"], "odd_mixer_bwd")
            dw_in, dw_in_b = _dw_in(n, dp, N_CHIPS, "dw_in_odd")
            dh, dhb, dnorm = _dn_proj(dp, wt["w_in_odd"], j, h, w["norm"], dh, "dn_proj_odd")
            g_odd[j] = dict(w_in=dw_in, w_in_b=dw_in_b, w_out=dw_out, w_out_b=dw_out_b, w_rg=dwr, w_ig=dwi,
                            vec=dvec, norm=dnorm)
    return loss, dh, g_even, g_odd, d_final


ANY = pl.BlockSpec(memory_space=pl.ANY)


def _mesh_pos():
    return lax.axis_index("x"), lax.axis_index("y"), lax.axis_index("c")


def _other_chips(x, y):
    return [(1 - x, y), (x, 1 - y), (1 - x, 1 - y)]


def _other_devices(x, y, c):
    out = []
    for p in range(1, N_DEV):
        out.append((1 - x if p & 4 else x, 1 - y if p & 2 else y, 1 - c if p & 1 else c))
    return out


def _remote(src, dst, ssem, rsem, dev):
    return pltpu.make_async_remote_copy(src_ref=src, dst_ref=dst, send_sem=ssem, recv_sem=rsem, device_id=dev,
                                        device_id_type=MESH)


def _comm_call(body, name, ins, out_shape, scratch):
    return _pallas(body, name=name, in_specs=[ANY] * len(ins), out_specs=[ANY] * len(out_shape), out_shape=out_shape,
                   scratch_shapes=scratch, compiler_params=pltpu.CompilerParams(has_side_effects=True))(*ins)


def _gather_weights(big, small):
    nA = len(big)
    half = [a.shape[1] // 2 for a in big]

    def body(*refs):
        ins, outs = refs[:nA + 1], refs[nA + 1:2 * nA + 2]
        ssem, rsem, fsem, frsem, lsem = refs[2 * nA + 2:]
        x, y, c = _mesh_pos()
        k = 2 * x + y
        chips = _other_chips(x, y)
        sib = (x, y, 1 - c)

        def slab(a, chip, core):
            return outs[a].at[:, chip, pl.ds(core * half[a], half[a]), :]

        local = [pltpu.make_async_copy(ins[a], outs[a].at[:, k], lsem.at[a]) for a in range(nA)]
        local.append(pltpu.make_async_copy(ins[nA], outs[nA].at[k], lsem.at[nA]))
        for cp in local:
            cp.start()
        sends = []
        for j, (ox, oy) in enumerate(chips):
            for a in range(nA):
                src = ins[a].at[:, pl.ds(c * half[a], half[a]), :]
                sends.append(_remote(src, slab(a, k, c), ssem.at[a, j], rsem.at[a, j], (ox, oy, c)))
            sends.append(_remote(ins[nA], outs[nA].at[k], ssem.at[nA, j], rsem.at[nA, j], (ox, oy, c)))
        for cp in sends:
            cp.start()
        for j, (ox, oy) in enumerate(chips):
            kj = 2 * ox + oy
            for a in range(nA):
                got = slab(a, kj, c)
                _remote(got, got, ssem.at[a, j], rsem.at[a, j], (ox, oy, c)).wait_recv()
                fw = _remote(got, got, fsem.at[a, j], frsem.at[a, j], sib)
                fw.start()
                sends.append(fw)
            gs = outs[nA].at[kj]
            _remote(gs, gs, ssem.at[nA, j], rsem.at[nA, j], (ox, oy, c)).wait_recv()
        for j, (ox, oy) in enumerate(chips):
            kj = 2 * ox + oy
            for a in range(nA):
                theirs = slab(a, kj, 1 - c)
                _remote(theirs, theirs, fsem.at[a, j], frsem.at[a, j], sib).wait_recv()
        for cp in sends:
            cp.wait_send()
        for cp in local:
            cp.wait()

    out_shape = [jax.ShapeDtypeStruct((a.shape[0], N_CHIPS) + a.shape[1:], a.dtype) for a in big]
    out_shape.append(jax.ShapeDtypeStruct((N_CHIPS,) + small.shape, small.dtype))
    scratch = [pltpu.SemaphoreType.DMA((nA + 1, 3)), pltpu.SemaphoreType.DMA((nA + 1, 3)),
               pltpu.SemaphoreType.DMA((nA, 3)), pltpu.SemaphoreType.DMA((nA, 3)), pltpu.SemaphoreType.DMA((nA + 1,))]
    return _comm_call(body, "gather_weights", list(big) + [small], out_shape, scratch)


def _exchange_cores(big, small):
    nA = len(big)
    half = [a.shape[1] // 2 for a in big]

    def body(*refs):
        ins, outs = refs[:nA + 1], refs[nA + 1:2 * nA + 2]
        ssem, rsem, ssem2, rsem2, lsem = refs[2 * nA + 2:]
        x, y, c = _mesh_pos()
        me = 4 * x + 2 * y + c
        sib = (x, y, 1 - c)
        peers = _other_devices(x, y, c)
        local = pltpu.make_async_copy(ins[nA], outs[nA].at[me], lsem.at[0])
        local.start()
        sends = []
        for a in range(nA):
            src = ins[a].at[:, pl.ds((1 - c) * half[a], half[a]), :]
            sends.append(_remote(src, outs[a], ssem.at[a], rsem.at[a], sib))
        for p, dev in enumerate(peers):
            sends.append(_remote(ins[nA], outs[nA].at[me], ssem2.at[p], rsem2.at[p], dev))
        for cp in sends:
            cp.start()
        for a in range(nA):
            _remote(outs[a], outs[a], ssem.at[a], rsem.at[a], sib).wait_recv()
        for p, (px, py, pc) in enumerate(peers):
            got = outs[nA].at[4 * px + 2 * py + pc]
            _remote(got, got, ssem2.at[p], rsem2.at[p], (px, py, pc)).wait_recv()
        for cp in sends:
            cp.wait_send()
        local.wait()

    out_shape = [jax.ShapeDtypeStruct((N_CHIPS, h, a.shape[2]), a.dtype) for a, h in zip(big, half)]
    out_shape.append(jax.ShapeDtypeStruct((N_DEV,) + small.shape, small.dtype))
    scratch = [pltpu.SemaphoreType.DMA((nA,)), pltpu.SemaphoreType.DMA((nA,)), pltpu.SemaphoreType.DMA((N_DEV - 1,)),
               pltpu.SemaphoreType.DMA((N_DEV - 1,)), pltpu.SemaphoreType.DMA((1,))]
    return _comm_call(body, "exchange_cores", list(big) + [small], out_shape, scratch)


def _exchange_chips(parts):
    nA = len(parts)

    def body(*refs):
        ins, outs = refs[:nA], refs[nA:2 * nA]
        ssem, rsem, lsem = refs[2 * nA:]
        x, y, c = _mesh_pos()
        k = 2 * x + y
        chips = _other_chips(x, y)
        local = [pltpu.make_async_copy(ins[a].at[k], outs[a].at[k], lsem.at[a]) for a in range(nA)]
        for cp in local:
            cp.start()
        sends = []
        for j, (ox, oy) in enumerate(chips):
            for a in range(nA):
                sends.append(_remote(ins[a].at[2 * ox + oy], outs[a].at[k], ssem.at[a, j], rsem.at[a, j], (ox, oy, c)))
        for cp in sends:
            cp.start()
        for j, (ox, oy) in enumerate(chips):
            for a in range(nA):
                got = outs[a].at[2 * ox + oy]
                _remote(got, got, ssem.at[a, j], rsem.at[a, j], (ox, oy, c)).wait_recv()
        for cp in sends:
            cp.wait_send()
        for cp in local:
            cp.wait()

    out_shape = [jax.ShapeDtypeStruct(a.shape, a.dtype) for a in parts]
    scratch = [pltpu.SemaphoreType.DMA((nA, 3)), pltpu.SemaphoreType.DMA((nA, 3)), pltpu.SemaphoreType.DMA((nA,))]
    return _comm_call(body, "exchange_chips", list(parts), out_shape, scratch)


def _exchange_final(pieces, plan, out_shape):
    nP = len(pieces)
    n_remote = sum(N_DEV - 1 if ev else 1 for _, _, ev in plan)

    def body(*refs):
        ins, outs = refs[:nP], refs[nP:nP + len(out_shape)]
        ssem, rsem, lsem = refs[nP + len(out_shape):]
        x, y, c = _mesh_pos()
        k = 2 * x + y
        sib = (x, y, 1 - c)
        peers = _other_devices(x, y, c)
        local, sends, waits = [], [], []
        s = 0
        for i, (o, layer, everywhere) in enumerate(plan):
            r2 = pieces[i].shape[0]
            if everywhere:
                mine = outs[o].at[layer, pl.ds((2 * k + c) * r2, r2), :]
                local.append(pltpu.make_async_copy(ins[i], mine, lsem.at[i]))
                for (px, py, pc) in peers:
                    sends.append(_remote(ins[i], mine, ssem.at[s], rsem.at[s], (px, py, pc)))
                    got = outs[o].at[layer, pl.ds((2 * (2 * px + py) + pc) * r2, r2), :]
                    waits.append(_remote(got, got, ssem.at[s], rsem.at[s], (px, py, pc)))
                    s += 1
            else:
                mine = outs[o].at[layer, pl.ds(c * r2, r2), :]
                local.append(pltpu.make_async_copy(ins[i], mine, lsem.at[i]))
                sends.append(_remote(ins[i], mine, ssem.at[s], rsem.at[s], sib))
                got = outs[o].at[layer, pl.ds((1 - c) * r2, r2), :]
                waits.append(_remote(got, got, ssem.at[s], rsem.at[s], sib))
                s += 1
        for cp in local + sends:
            cp.start()
        for cp in waits:
            cp.wait_recv()
        for cp in sends:
            cp.wait_send()
        for cp in local:
            cp.wait()

    scratch = [pltpu.SemaphoreType.DMA((n_remote,)), pltpu.SemaphoreType.DMA((n_remote,)), pltpu.SemaphoreType.DMA((nP,))]
    return _comm_call(body, "exchange_final", list(pieces), out_shape, scratch)


BLOCK_BYTES = 1 << 20


def _row_tile(rows, cols, mult=16):
    best = mult
    for t in range(mult, rows + 1, mult):
        if rows % t == 0 and t * cols * 4 <= BLOCK_BYTES:
            best = t
    return best


def _add_cores(a, recv, core):
    _, R, C = a.shape
    r2 = R // 2
    tr = _row_tile(r2, C)
    nb = r2 // tr

    def body(core_ref, a_ref, r_ref, o_ref):
        o_ref[...] = (a_ref[...] + r_ref[...].astype(F32)).astype(BF16)

    grid_spec = pltpu.PrefetchScalarGridSpec(
        num_scalar_prefetch=1, grid=(N_CHIPS, nb),
        in_specs=[pl.BlockSpec((1, tr, C), lambda s, i, cr: (s, cr[0] * nb + i, 0)),
                  pl.BlockSpec((1, tr, C), lambda s, i, cr: (s, i, 0))],
        out_specs=pl.BlockSpec((1, tr, C), lambda s, i, cr: (s, i, 0)))
    return _pallas(body, name="add_cores", grid_spec=grid_spec, out_shape=jax.ShapeDtypeStruct((N_CHIPS, r2, C), BF16),
                   compiler_params=_params("parallel", "parallel"))(core, a, recv)


def _sum_slabs(parts, name):
    n, R, C = parts.shape
    tr = _row_tile(R, C * n // 2 if parts.dtype == BF16 else C * n, 16 if parts.dtype == BF16 else 8)

    def body(p_ref, o_ref):
        acc = p_ref[0].astype(F32)
        for s in range(1, n):
            acc = acc + p_ref[s].astype(F32)
        o_ref[...] = acc

    return _pallas(body, name=name, grid=(R // tr,), in_specs=[pl.BlockSpec((n, tr, C), lambda i: (0, i, 0))],
                   out_specs=pl.BlockSpec((tr, C), lambda i: (i, 0)), out_shape=jax.ShapeDtypeStruct((R, C), F32),
                   compiler_params=_params("parallel"))(parts)


def _adamw(w, g, m, v, name):
    L, R, C = w.shape
    tr = _row_tile(R, C, 8)

    def body(w_ref, g_ref, m_ref, v_ref, d_ref, m2_ref, v2_ref):
        gg = g_ref[...]
        m2 = ADAM_B1 * m_ref[...] + (1.0 - ADAM_B1) * gg
        v2 = ADAM_B2 * v_ref[...] + (1.0 - ADAM_B2) * (gg * gg)
        m_hat = m2 / (1.0 - ADAM_B1 ** ADAM_STEP)
        v_hat = v2 / (1.0 - ADAM_B2 ** ADAM_STEP)
        d_ref[...] = -ADAM_LR * (m_hat / (jnp.sqrt(v_hat) + ADAM_EPS) + ADAM_WD * w_ref[...])
        m2_ref[...] = m2
        v2_ref[...] = v2

    blk = pl.BlockSpec((1, tr, C), lambda l, i: (l, i, 0))
    shp = jax.ShapeDtypeStruct((L, R, C), F32)
    return _pallas(body, name=name, grid=(L, R // tr), in_specs=[blk] * 4, out_specs=[blk] * 3, out_shape=[shp] * 3,
                   compiler_params=_params("parallel", "parallel"))(w, g, m, v)


WEIGHTS = ("norm_even", "w_in_even", "conv_a_w", "conv_a_b", "ln_a_g", "ln_a_b", "pool_w", "pool_b", "pool_scale",
           "w_out_even", "norm_odd", "w_in_odd", "conv_c_w", "conv_c_b", "w_rg", "b_rg", "w_ig", "b_ig", "lru_lambda",
           "w_out_odd", "final_norm")
BIG = ("w_in_even", "w_out_even", "pool_w", "w_in_odd", "w_out_odd", "w_rg", "w_ig")
SMALL = tuple(n for n in WEIGHTS if n not in BIG)
SMALL_SHARDED = ("conv_a_w", "pool_b", "norm_odd", "conv_c_w", "conv_c_b", "b_rg", "b_ig", "lru_lambda")


def _pack(arrs):
    flat = jnp.concatenate([a.reshape(-1) for a in arrs])
    rows = -(-flat.shape[0] // (8 * 128)) * 8
    return jnp.pad(flat, (0, rows * 128 - flat.shape[0])).reshape(rows, 128)


def _unpack(buf, shapes, lead=()):
    flat = buf.reshape(tuple(lead) + (-1,))
    out, o = [], 0
    for s in shapes:
        n = 1
        for d in s:
            n *= d
        out.append(flat[..., o:o + n].reshape(tuple(lead) + tuple(s)))
        o += n
    return out


def _shard(full, axis, k):
    n = full.shape[axis] // N_CHIPS
    return lax.dynamic_slice_in_dim(full, k * n, n, axis)


def kernel(x, norm_even, w_in_even, conv_a_w, conv_a_b, ln_a_g, ln_a_b, pool_w, pool_b, pool_scale, w_out_even, norm_odd, w_in_odd, conv_c_w, conv_c_b, w_rg, b_rg, w_ig, b_ig, lru_lambda, w_out_odd, final_norm, loss_target, m_norm_even, m_w_in_even, m_conv_a_w, m_conv_a_b, m_ln_a_g, m_ln_a_b, m_pool_w, m_pool_b, m_pool_scale, m_w_out_even, m_norm_odd, m_w_in_odd, m_conv_c_w, m_conv_c_b, m_w_rg, m_b_rg, m_w_ig, m_b_ig, m_lru_lambda, m_w_out_odd, m_final_norm, v_norm_even, v_w_in_even, v_conv_a_w, v_conv_a_b, v_ln_a_g, v_ln_a_b, v_pool_w, v_pool_b, v_pool_scale, v_w_out_even, v_norm_odd, v_w_in_odd, v_conv_c_w, v_conv_c_b, v_w_rg, v_b_rg, v_w_ig, v_b_ig, v_lru_lambda, v_w_out_odd, v_final_norm):
    P = dict(locals())
    xi, yi, ci = _mesh_pos()
    k = 2 * xi + yi
    L = w_in_even.shape[0]
    D = D_MODEL

    big = [w_in_even.astype(BF16), w_out_even.astype(BF16), w_in_odd.astype(BF16), w_out_odd.astype(BF16),
           pool_w.reshape(L, 4 * 64, POOL_GW).astype(BF16)]
    g_wie, g_woe, g_wio, g_woo, g_pw, g_small = _gather_weights(big, _pack([P[n] for n in SMALL_SHARDED]))
    full = {}
    for n, a in zip(SMALL_SHARDED, _unpack(g_small, [P[n].shape for n in SMALL_SHARDED], lead=(N_CHIPS,))):
        a = jnp.moveaxis(a, 0, -2)
        full[n] = a.reshape(a.shape[:-2] + (N_CHIPS * a.shape[-1],))
    pw_full = g_pw.reshape(L, N_CHIPS, 4, 64, POOL_GW).transpose(0, 2, 1, 3, 4).reshape(L, 4, POOL_GW, POOL_GW)
    even, odd = [], []
    for j in range(L):
        cw = full["conv_a_w"][j]
        even.append(dict(norm=norm_even[j][None], conv_w=_pad_rows(cw, 32), conv_w_rev=_pad_rows(cw[::-1], 32),
                         conv_b=conv_a_b[j][None], ln_g=ln_a_g[j][None], ln_b=ln_a_b[j][None], pool_w=pw_full[j],
                         pool_b=full["pool_b"][j].reshape(1, D), pool_scale=pool_scale[j][None]))
        odd.append(dict(norm=full["norm_odd"][j][None], conv_w=_pad_rows(full["conv_c_w"][j], 8),
                        conv_b=full["conv_c_b"][j][None], w_rg=w_rg[j].astype(BF16), b_rg=full["b_rg"][j][None],
                        w_ig=w_ig[j].astype(BF16), b_ig=full["b_ig"][j][None], lam=full["lru_lambda"][j][None]))
    wt = dict(w_in_even=g_wie, w_out_even=g_woe.reshape(L, 2 * D, D), w_in_odd=g_wio,
              w_out_odd=g_woo.reshape(L, W_LRU, D), even=even, odd=odd, final_norm=final_norm[None])

    loss, grad_x, g_even, g_odd, d_final = _local_step(x[0], loss_target[0], wt)
    loss = lax.psum(loss[0, 0], ("x", "y", "c"))

    entries, small_g = [], []
    for j in range(L):
        ge, go = g_even[j], g_odd[j]
        dpw = ge["pool_w"].reshape(4, N_CHIPS, 64, POOL_GW).transpose(1, 0, 2, 3).reshape(N_CHIPS, 4 * 64, POOL_GW)
        drg = go["w_rg"].reshape(N_CHIPS, W_LRU // N_CHIPS, LRU_HD)
        dig = go["w_ig"].reshape(N_CHIPS, W_LRU // N_CHIPS, LRU_HD)
        entries += [
            (ge["w_in"], ge["w_in_b"], 0, j, False),
            (ge["w_out"].reshape(N_CHIPS, -1, D), ge["w_out_b"].reshape(N_CHIPS, -1, D), 1, j, False),
            (dpw, dpw.astype(BF16), 2, j, False),
            (go["w_in"], go["w_in_b"], 3, j, False),
            (go["w_out"].reshape(N_CHIPS, -1, D), go["w_out_b"].reshape(N_CHIPS, -1, D), 4, j, False),
            (drg, drg.astype(BF16), 5, j, True),
            (dig, dig.astype(BF16), 6, j, True)]
        small_g += [ge["conv_w"].reshape(32, 8, D).sum(axis=1)[:CONV_K], ge["vec"][0:5], ge["norm"], go["vec"], go["norm"]]
    small_g.append(d_final)
    small_shapes = [a.shape for a in small_g]
    recv = _exchange_cores([e[1] for e in entries], _pack(small_g))
    core = ci.astype(jnp.int32).reshape(1)
    pair_sums = [_add_cores(e[0], r, core) for e, r in zip(entries, recv[:-1])]
    pieces = [_sum_slabs(p, "sum_chips") for p in _exchange_chips(pair_sums)]
    shard_shapes = [(L, D, 1280), (L, 512, D), (L, 4 * 64, POOL_GW), (L, D, 768), (L, W_LRU // N_CHIPS, D),
                    (L, W_LRU, LRU_HD), (L, W_LRU, LRU_HD)]
    gw = _exchange_final(pieces, [(e[2], e[3], e[4]) for e in entries],
                         [jax.ShapeDtypeStruct(s, F32) for s in shard_shapes])
    sg = _unpack(_sum_slabs(recv[-1], "sum_devices"), small_shapes)

    grads = dict(w_in_even=gw[0], w_out_even=gw[1], pool_w=gw[2].reshape(pool_w.shape), w_in_odd=gw[3], w_out_odd=gw[4],
                 w_rg=gw[5].reshape(w_rg.shape), w_ig=gw[6].reshape(w_ig.shape), final_norm=sg[-1][0])
    ev = [sg[5 * j + 1] for j in range(L)]
    ov = [sg[5 * j + 3] for j in range(L)]
    grads["conv_a_w"] = _shard(jnp.stack([sg[5 * j] for j in range(L)]), 2, k)
    grads["norm_even"] = jnp.stack([sg[5 * j + 2][0] for j in range(L)])
    grads["norm_odd"] = _shard(jnp.stack([sg[5 * j + 4][0] for j in range(L)]), 1, k)
    for r, n in enumerate(("conv_a_b", "ln_a_g", "ln_a_b", "pool_scale")):
        grads[n] = jnp.stack([e[r] for e in ev])
    grads["pool_b"] = _shard(jnp.stack([e[4].reshape(4, POOL_GW) for e in ev]), 2, k)
    grads["conv_c_w"] = _shard(jnp.stack([o[0:4] for o in ov]), 2, k)
    for r, n in zip((4, 5, 6, 7), ("conv_c_b", "b_rg", "b_ig", "lru_lambda")):
        grads[n] = _shard(jnp.stack([o[r] for o in ov]), 1, k)

    delta, new_m, new_v = {}, {}, {}
    for n in BIG:
        s3 = (L, -1, P[n].shape[-1])
        d, m2, v2 = _adamw(P[n].reshape(s3), grads[n].reshape(s3), P["m_" + n].reshape(s3), P["v_" + n].reshape(s3), "adamw")
        delta[n], new_m[n], new_v[n] = d.reshape(P[n].shape), m2.reshape(P[n].shape), v2.reshape(P[n].shape)
    shapes = [P[n].shape for n in SMALL]
    packed = [_pack([src[n] for n in SMALL])[None] for src in
              (P, grads, {n: P["m_" + n] for n in SMALL}, {n: P["v_" + n] for n in SMALL})]
    for res, out in zip(_adamw(*packed, "adamw_small"), (delta, new_m, new_v)):
        for n, a in zip(SMALL, _unpack(res[0], shapes)):
            out[n] = a

    return (loss, grad_x[None], *[grads[n] for n in WEIGHTS], *[delta[n] for n in WEIGHTS],
            *[new_m[n] for n in WEIGHTS], *[new_v[n] for n in WEIGHTS])
```

```python
import functools

import jax
import jax.numpy as jnp
from jax import lax
from jax.experimental import pallas as pl
from jax.experimental.pallas import tpu as pltpu

F32 = jnp.float32
BF16 = jnp.bfloat16
MESH = pl.DeviceIdType.MESH

D_MODEL = 1024
N_CHIPS = 4
N_DEV = 8
EPS_RMS = 1e-6
EPS_LN = 1e-5
CONV_K = 31
POOL_WINDOWS = (2, 4, 8, 16)
POOL_GW = 256
LRU_HEADS = 12
LRU_HD = 128
W_LRU = LRU_HEADS * LRU_HD
LRU_CONV_K = 4
LRU_C = 8.0
ADAM_LR = 0.001
ADAM_B1 = 0.9
ADAM_B2 = 0.999
ADAM_EPS = 1e-08
ADAM_WD = 0.01
ADAM_STEP = 10

VMEM_LIMIT_BYTES = 56 * 1024 * 1024
ROW_TILE = 512
MIX_TILE = 256
EVEN_HALO = 32
ODD_HALO = 8


def _pallas(body, **kw):
    return pl.pallas_call(body, **kw)


def _params(*sem):
    return pltpu.CompilerParams(dimension_semantics=sem if sem else None, vmem_limit_bytes=VMEM_LIMIT_BYTES)


def _sigmoid(x):
    return 1.0 / (1.0 + jnp.exp(-x))


def _dsilu(x, s):
    return s * (1.0 + x * (1.0 - s))


def _nt(a, b):
    return lax.dot_general(a, b, (((1,), (1,)), ((), ())), preferred_element_type=F32)


def _tn(a, b):
    return lax.dot_general(a, b, (((0,), (0,)), ((), ())), preferred_element_type=F32)


def _in_proj(h, g, wg, layer, name):
    T, D = h.shape
    _, nblk, _, nb = wg.shape

    def body(h_ref, g_ref, w_ref, p_ref, n_ref):
        @pl.when(pl.program_id(1) == 0)
        def _():
            x = h_ref[...]
            r = lax.rsqrt(jnp.mean(x * x, axis=-1, keepdims=True) + EPS_RMS)
            n_ref[...] = (x * r * g_ref[...]).astype(BF16)

        p_ref[...] = jnp.dot(n_ref[...], w_ref[0], preferred_element_type=F32)

    return _pallas(
        body, name=name, grid=(T // ROW_TILE, nblk),
        in_specs=[pl.BlockSpec((ROW_TILE, D), lambda i, j: (i, 0)), pl.BlockSpec((1, D), lambda i, j: (0, 0)),
                  pl.BlockSpec((None, 1, D, nb), lambda i, j: (layer, j, 0, 0))],
        out_specs=[pl.BlockSpec((ROW_TILE, nb), lambda i, j: (i, j)), pl.BlockSpec((ROW_TILE, D), lambda i, j: (i, 0))],
        out_shape=[jax.ShapeDtypeStruct((T, nblk * nb), F32), jax.ShapeDtypeStruct((T, D), BF16)],
        compiler_params=_params("parallel", "arbitrary"))(h, g, wg)


def _out_proj(y, w, layer, hres, name):
    T, K = y.shape
    D = w.shape[2]

    def body(y_ref, w_ref, r_ref, o_ref):
        o_ref[...] = r_ref[...] + jnp.dot(y_ref[...], w_ref[...], preferred_element_type=F32)

    return _pallas(
        body, name=name, grid=(T // ROW_TILE,),
        in_specs=[pl.BlockSpec((ROW_TILE, K), lambda i: (i, 0)), pl.BlockSpec((None, K, D), lambda i: (layer, 0, 0)),
                  pl.BlockSpec((ROW_TILE, D), lambda i: (i, 0))],
        out_specs=pl.BlockSpec((ROW_TILE, D), lambda i: (i, 0)),
        out_shape=jax.ShapeDtypeStruct((T, D), F32),
        compiler_params=_params("parallel"))(y, w, hres)


def _dy_proj(dout, w, layer, name):
    T, D = dout.shape
    K = w.shape[1]

    def body(d_ref, w_ref, o_ref):
        o_ref[...] = _nt(d_ref[...], w_ref[...])

    return _pallas(
        body, name=name, grid=(T // ROW_TILE,),
        in_specs=[pl.BlockSpec((ROW_TILE, D), lambda i: (i, 0)), pl.BlockSpec((None, K, D), lambda i: (layer, 0, 0))],
        out_specs=pl.BlockSpec((ROW_TILE, K), lambda i: (i, 0)),
        out_shape=jax.ShapeDtypeStruct((T, K), F32),
        compiler_params=_params("parallel"))(dout, w)


def _dn_proj(dp, wg, layer, h, g, dres, name):
    T, D = h.shape
    _, nblk, _, nb = wg.shape

    def body(dp_ref, w_ref, h_ref, g_ref, dres_ref, dh_ref, dhb_ref, dg_ref, acc_ref):
        i, j = pl.program_id(0), pl.program_id(1)
        part = _nt(dp_ref[...], w_ref[0])

        @pl.when(j == 0)
        def _():
            acc_ref[...] = part

        @pl.when(j > 0)
        def _():
            acc_ref[...] += part

        @pl.when(j == nblk - 1)
        def _():
            x = h_ref[...]
            r = lax.rsqrt(jnp.mean(x * x, axis=-1, keepdims=True) + EPS_RMS)
            dn = acc_ref[...]
            q = dn * g_ref[...]
            dh = dres_ref[...] + r * q - x * ((r * r * r) * jnp.mean(q * x, axis=-1, keepdims=True))
            dh_ref[...] = dh
            dhb_ref[...] = dh.astype(BF16)
            dgp = jnp.sum(dn * (x * r), axis=0, keepdims=True)

            @pl.when(i == 0)
            def _():
                dg_ref[...] = dgp

            @pl.when(i > 0)
            def _():
                dg_ref[...] += dgp

    return _pallas(
        body, name=name, grid=(T // ROW_TILE, nblk),
        in_specs=[pl.BlockSpec((ROW_TILE, nb), lambda i, j: (i, j)),
                  pl.BlockSpec((None, 1, D, nb), lambda i, j: (layer, j, 0, 0)),
                  pl.BlockSpec((ROW_TILE, D), lambda i, j: (i, 0)), pl.BlockSpec((1, D), lambda i, j: (0, 0)),
                  pl.BlockSpec((ROW_TILE, D), lambda i, j: (i, 0))],
        out_specs=[pl.BlockSpec((ROW_TILE, D), lambda i, j: (i, 0)), pl.BlockSpec((ROW_TILE, D), lambda i, j: (i, 0)),
                   pl.BlockSpec((1, D), lambda i, j: (0, 0))],
        out_shape=[jax.ShapeDtypeStruct((T, D), F32), jax.ShapeDtypeStruct((T, D), BF16),
                   jax.ShapeDtypeStruct((1, D), F32)],
        scratch_shapes=[pltpu.VMEM((ROW_TILE, D), F32)],
        compiler_params=_params("arbitrary", "arbitrary"))(dp, wg, h, g, dres)


def _dw_in(n, dp, nblk, layer, nlayers, prev, name):
    T, D = n.shape
    nb = dp.shape[1] // nblk
    ta = 512

    def body(n_ref, dp_ref, *rest):
        rest[-1][0] = _tn(n_ref[...], dp_ref[...]).astype(BF16)

    in_specs = [pl.BlockSpec((T, ta), lambda j, i: (0, i)), pl.BlockSpec((T, nb), lambda j, i: (0, j))]
    args = (n, dp) if prev is None else (n, dp, prev)
    return _pallas(
        body, name=name, grid=(nblk, D // ta), in_specs=in_specs + ([] if prev is None else [ANY]),
        out_specs=pl.BlockSpec((None, 1, ta, nb), lambda j, i: (layer, j, i, 0)),
        out_shape=jax.ShapeDtypeStruct((nlayers, nblk, D, nb), BF16),
        input_output_aliases={} if prev is None else {2: 0},
        compiler_params=_params("parallel", "parallel"))(*args)


def _dw_out(y, dout, layer, nlayers, prev, name):
    T, K = y.shape
    D = dout.shape[1]
    tk = 512

    def body(y_ref, d_ref, *rest):
        rest[-1][...] = _tn(y_ref[...], d_ref[...]).astype(BF16)

    in_specs = [pl.BlockSpec((T, tk), lambda i: (0, i)), pl.BlockSpec((T, D), lambda i: (0, 0))]
    args = (y, dout) if prev is None else (y, dout, prev)
    return _pallas(
        body, name=name, grid=(K // tk,), in_specs=in_specs + ([] if prev is None else [ANY]),
        out_specs=pl.BlockSpec((None, tk, D), lambda i: (layer, i, 0)),
        out_shape=jax.ShapeDtypeStruct((nlayers, K, D), BF16),
        input_output_aliases={} if prev is None else {2: 0},
        compiler_params=_params("parallel"))(*args)


def _loss_head(h, g, tgt):
    T, D = h.shape
    tm = MIX_TILE

    def body(h_ref, g_ref, t_ref, dh_ref, dhb_ref, dg_ref, loss_ref):
        i = pl.program_id(0)
        x = h_ref[...]
        gg = g_ref[...]
        r = lax.rsqrt(jnp.mean(x * x, axis=-1, keepdims=True) + EPS_RMS)
        xr = x * r
        e = xr * gg - t_ref[...]
        lp = 0.5 * jnp.sum(jnp.mean(e * e, axis=-1, keepdims=True), axis=0, keepdims=True)
        dn = e * (1.0 / D)
        q = dn * gg
        dh = r * q - x * ((r * r * r) * jnp.mean(q * x, axis=-1, keepdims=True))
        dh_ref[...] = dh
        dhb_ref[...] = dh.astype(BF16)
        dgp = jnp.sum(dn * xr, axis=0, keepdims=True)

        @pl.when(i == 0)
        def _():
            dg_ref[...] = dgp
            loss_ref[...] = lp

        @pl.when(i > 0)
        def _():
            dg_ref[...] += dgp
            loss_ref[...] += lp

    return _pallas(
        body, name="loss_head", grid=(T // tm,),
        in_specs=[pl.BlockSpec((tm, D), lambda i: (i, 0)), pl.BlockSpec((1, D), lambda i: (0, 0)),
                  pl.BlockSpec((tm, D), lambda i: (i, 0))],
        out_specs=[pl.BlockSpec((tm, D), lambda i: (i, 0)), pl.BlockSpec((tm, D), lambda i: (i, 0)),
                   pl.BlockSpec((1, D), lambda i: (0, 0)), pl.BlockSpec((1, 1), lambda i: (0, 0))],
        out_shape=[jax.ShapeDtypeStruct((T, D), F32), jax.ShapeDtypeStruct((T, D), BF16),
                   jax.ShapeDtypeStruct((1, D), F32), jax.ShapeDtypeStruct((1, 1), F32)],
        compiler_params=_params("arbitrary"))(h, g, tgt)


def _shift_up(x, j):
    return x if j == 0 else pltpu.roll(x, x.shape[0] - j, 0)


def _shift_down(x, j):
    return x if j == 0 else pltpu.roll(x, j, 0)


def _fill_shifted(dst_ref, x):
    rows = dst_ref.shape[1]
    for s in range(8):
        dst_ref[s] = _shift_up(x, s)[0:rows]


def _tap_sum(sh_ref, w_ref, r0, nrows, offsets):
    acc = None
    for k, o in enumerate(offsets):
        win = sh_ref[o % 8, pl.ds(r0 + (o // 8) * 8, nrows), :]
        term = w_ref[k:k + 1, :] * win
        acc = term if acc is None else acc + term
    return acc


def _pool_sums(vx, up):
    sh = _shift_up if up else _shift_down
    outs = []
    for gi, w in enumerate(POOL_WINDOWS):
        s = vx[:, gi * POOL_GW:(gi + 1) * POOL_GW]
        j = 1
        while j < w:
            s = s + sh(s, j)
            j *= 2
        outs.append(s)
    return outs


def _inv_count(row0, nrows):
    pos = (row0 + 1 + lax.broadcasted_iota(jnp.int32, (nrows, 1), 0)).astype(F32)
    return [1.0 / jnp.minimum(pos, float(w)) for w in POOL_WINDOWS]


def _even_mixer_fwd(p, cw, cb, lg, lb, pw, pb, sc, name):
    T = p.shape[0]
    C = D_MODEL
    tT, HL = MIX_TILE, EVEN_HALO
    hb = tT // HL
    chunk = 16

    def body(pm_ref, ph_ref, cw_ref, cb_ref, lg_ref, lb_ref, pw_ref, pb_ref, sc_ref, y_ref, u1_ref, u0x_ref, sh_ref):
        i = pl.program_id(0)
        keep = (i > 0).astype(F32)
        u0x_ref[0:HL] = ph_ref[:, 0:C] * _sigmoid(ph_ref[:, C:2 * C]) * keep
        u0x_ref[HL:HL + tT] = pm_ref[:, 0:C] * _sigmoid(pm_ref[:, C:2 * C])
        u0x_ref[HL + tT:HL + tT + 8] = jnp.zeros((8, C), F32)
        _fill_shifted(sh_ref, u0x_ref[...])
        offs = [HL - (CONV_K - 1) + k for k in range(CONV_K)]

        def conv_chunk(c, carry):
            r0 = pl.multiple_of(c * chunk, chunk)
            u1_ref[pl.ds(r0, chunk), :] = _tap_sum(sh_ref, cw_ref, r0, chunk, offs) + cb_ref[...]
            return carry

        lax.fori_loop(0, tT // chunk, conv_chunk, 0)
        u1 = u1_ref[...]
        mu = jnp.mean(u1, axis=-1, keepdims=True)
        xc = u1 - mu
        rs = lax.rsqrt(jnp.mean(xc * xc, axis=-1, keepdims=True) + EPS_LN)
        u2 = xc * rs * lg_ref[...] + lb_ref[...]
        u3 = u2 * _sigmoid(u2)
        ag = pm_ref[:, 2 * C:3 * C]
        y_ref[:, 0:C] = (u3 * (ag * _sigmoid(ag))).astype(BF16)
        vx = jnp.concatenate([ph_ref[:, 3 * C:4 * C] * keep, pm_ref[:, 3 * C:4 * C]], axis=0)
        sums = _pool_sums(vx, up=False)
        inv = _inv_count(i * tT, tT)
        for gi in range(len(POOL_WINDOWS)):
            cols = slice(gi * POOL_GW, (gi + 1) * POOL_GW)
            d0 = sums[gi][HL:] * inv[gi] - vx[HL:, cols]
            d1 = jnp.dot(d0.astype(BF16), pw_ref[gi], preferred_element_type=F32) + pb_ref[:, cols]
            bg = pm_ref[:, 4 * C + gi * POOL_GW:4 * C + (gi + 1) * POOL_GW]
            y_ref[:, C + gi * POOL_GW:C + (gi + 1) * POOL_GW] = (d1 * sc_ref[:, cols] * (bg * _sigmoid(bg))).astype(BF16)

    vec = pl.BlockSpec((1, C), lambda i: (0, 0))
    return _pallas(
        body, name=name, grid=(T // tT,),
        in_specs=[pl.BlockSpec((tT, 5 * C), lambda i: (i, 0)),
                  pl.BlockSpec((HL, 5 * C), lambda i: (jnp.maximum(i * hb - 1, 0), 0)),
                  pl.BlockSpec((32, C), lambda i: (0, 0)), vec, vec, vec,
                  pl.BlockSpec((4, POOL_GW, POOL_GW), lambda i: (0, 0, 0)), vec, vec],
        out_specs=[pl.BlockSpec((tT, 2 * C), lambda i: (i, 0)), pl.BlockSpec((tT, C), lambda i: (i, 0))],
        out_shape=[jax.ShapeDtypeStruct((T, 2 * C), BF16), jax.ShapeDtypeStruct((T, C), F32)],
        scratch_shapes=[pltpu.VMEM((HL + tT + 8, C), F32), pltpu.VMEM((8, HL + tT, C), F32)],
        compiler_params=_params("parallel"))(p, p, cw, cb, lg, lb, pw, pb, sc)


def _even_mixer_bwd(p, u1, dy, cw, cwr, lg, lb, pw, pb, sc, name):
    T = p.shape[0]
    C = D_MODEL
    tT, HL = MIX_TILE, EVEN_HALO
    hb = tT // HL
    nT = T // tT
    R1 = tT + HL
    chunk = 16

    def body(pm_ref, pp_ref, pn_ref, u1m_ref, u1n_ref, dym_ref, dyn_ref, cw_ref, cwr_ref, lg_ref, lb_ref, pw_ref,
             pb_ref, sc_ref, dp_ref, dcw_ref, dvec_ref, dpw_ref, x_ref, sh_ref, du0_ref):
        i = pl.program_id(0)
        keep_prev = (i > 0).astype(F32)
        keep_next = (i < nT - 1).astype(F32)
        row = lax.broadcasted_iota(jnp.int32, (R1, 1), 0)
        live = jnp.where(row < tT, 1.0, keep_next)

        def cat(m, n):
            return jnp.concatenate([m, n], axis=0)

        u1 = cat(u1m_ref[...], u1n_ref[...])
        mu = jnp.mean(u1, axis=-1, keepdims=True)
        xc = u1 - mu
        rs = lax.rsqrt(jnp.mean(xc * xc, axis=-1, keepdims=True) + EPS_LN)
        xh = xc * rs
        u2 = xh * lg_ref[...] + lb_ref[...]
        s2 = _sigmoid(u2)
        u3 = u2 * s2
        ag = cat(pm_ref[:, 2 * C:3 * C], pn_ref[:, 2 * C:3 * C])
        sa = _sigmoid(ag)
        dya = cat(dym_ref[:, 0:C], dyn_ref[:, 0:C])
        dp_ref[:, 2 * C:3 * C] = (dya * u3 * _dsilu(ag, sa))[0:tT].astype(BF16)
        du2 = dya * (ag * sa) * _dsilu(u2, s2)
        dlg = jnp.sum((du2 * xh)[0:tT], axis=0, keepdims=True)
        dlb = jnp.sum(du2[0:tT], axis=0, keepdims=True)
        dxh = du2 * lg_ref[...]
        du1 = rs * (dxh - jnp.mean(dxh, axis=-1, keepdims=True) - xh * jnp.mean(dxh * xh, axis=-1, keepdims=True))
        du1 = du1 * live
        dcb = jnp.sum(du1[0:tT], axis=0, keepdims=True)
        x_ref[0:R1] = du1
        x_ref[R1:R1 + 8] = jnp.zeros((8, C), F32)
        _fill_shifted(sh_ref, x_ref[...])

        def du0_chunk(c, carry):
            r0 = pl.multiple_of(c * chunk, chunk)
            du0_ref[pl.ds(r0, chunk), :] = _tap_sum(sh_ref, cwr_ref, r0, chunk, list(range(CONV_K)))
            return carry

        lax.fori_loop(0, tT // chunk, du0_chunk, 0)
        av, agl = pm_ref[:, 0:C], pm_ref[:, C:2 * C]
        sg = _sigmoid(agl)
        du0 = du0_ref[...]
        dp_ref[:, 0:C] = (du0 * sg).astype(BF16)
        dp_ref[:, C:2 * C] = (du0 * av * sg * (1.0 - sg)).astype(BF16)
        du0_ref[...] = du1[0:tT]
        x_ref[0:HL] = pp_ref[:, 0:C] * _sigmoid(pp_ref[:, C:2 * C]) * keep_prev
        x_ref[HL:HL + tT] = av * sg
        x_ref[HL + tT:HL + tT + 8] = jnp.zeros((8, C), F32)
        _fill_shifted(sh_ref, x_ref[...])

        @pl.when(i == 0)
        def _():
            dcw_ref[...] = jnp.zeros_like(dcw_ref)

        for k in range(CONV_K):
            o = HL - (CONV_K - 1) + k

            def dw_chunk(c, acc, o=o):
                r0 = pl.multiple_of(c * 8, 8)
                return acc + du0_ref[pl.ds(r0, 8), :] * sh_ref[o % 8, pl.ds(r0 + (o // 8) * 8, 8), :]

            dcw_ref[8 * k:8 * k + 8, :] += lax.fori_loop(0, tT // 8, dw_chunk, jnp.zeros((8, C), F32))

        bg = cat(pm_ref[:, 4 * C:5 * C], pn_ref[:, 4 * C:5 * C])
        sb = _sigmoid(bg)
        dyb = cat(dym_ref[:, C:2 * C], dyn_ref[:, C:2 * C])
        dyb0 = dyb * (bg * sb)
        dd1 = dyb0 * sc_ref[...]
        dpb = jnp.sum(dd1[0:tT], axis=0, keepdims=True)
        inv1 = _inv_count(i * tT, R1)
        z_parts, dd0_parts = [], []
        for gi in range(len(POOL_WINDOWS)):
            cols = slice(gi * POOL_GW, (gi + 1) * POOL_GW)
            dd0 = _nt(dd1[:, cols].astype(BF16), pw_ref[gi])
            dd0_parts.append(dd0)
            z_parts.append(dd0 * inv1[gi] * live)
        fsum = _pool_sums(jnp.concatenate(z_parts, axis=1), up=True)
        vx = cat(pp_ref[:, 3 * C:4 * C] * keep_prev, pm_ref[:, 3 * C:4 * C])
        sums = _pool_sums(vx, up=False)
        inv0 = _inv_count(i * tT, tT)
        dsc_parts = []
        for gi in range(len(POOL_WINDOWS)):
            cols = slice(gi * POOL_GW, (gi + 1) * POOL_GW)
            dp_ref[:, 3 * C + gi * POOL_GW:3 * C + (gi + 1) * POOL_GW] = (fsum[gi][0:tT] - dd0_parts[gi][0:tT]).astype(BF16)
            d0 = (sums[gi][HL:] * inv0[gi] - vx[HL:, cols]).astype(BF16)
            d1 = jnp.dot(d0, pw_ref[gi], preferred_element_type=F32) + pb_ref[:, cols]
            bgm, sbm = bg[0:tT, cols], sb[0:tT, cols]
            dp_ref[:, 4 * C + gi * POOL_GW:4 * C + (gi + 1) * POOL_GW] = (
                dyb[0:tT, cols] * d1 * sc_ref[:, cols] * _dsilu(bgm, sbm)).astype(BF16)
            dsc_parts.append(jnp.sum(dyb0[0:tT, cols] * d1, axis=0, keepdims=True))
            dpw_g = _tn(d0, dd1[0:tT, cols].astype(BF16))

            @pl.when(i == 0)
            def _(gi=gi, dpw_g=dpw_g):
                dpw_ref[gi] = dpw_g

            @pl.when(i > 0)
            def _(gi=gi, dpw_g=dpw_g):
                dpw_ref[gi] += dpw_g

        dsc = jnp.concatenate(dsc_parts, axis=1)
        vecs = jnp.concatenate([dcb, dlg, dlb, dsc, dpb, jnp.zeros((3, C), F32)], axis=0)

        @pl.when(i == 0)
        def _():
            dvec_ref[...] = vecs

        @pl.when(i > 0)
        def _():
            dvec_ref[...] += vecs

    vec = pl.BlockSpec((1, C), lambda i: (0, 0))
    taps = pl.BlockSpec((32, C), lambda i: (0, 0))

    def prev_blk(i):
        return (jnp.maximum(i * hb - 1, 0), 0)

    def next_blk(i):
        return (jnp.minimum((i + 1) * hb, T // HL - 1), 0)

    return _pallas(
        body, name=name, grid=(nT,),
        in_specs=[pl.BlockSpec((tT, 5 * C), lambda i: (i, 0)), pl.BlockSpec((HL, 5 * C), prev_blk),
                  pl.BlockSpec((HL, 5 * C), next_blk),
                  pl.BlockSpec((tT, C), lambda i: (i, 0)), pl.BlockSpec((HL, C), next_blk),
                  pl.BlockSpec((tT, 2 * C), lambda i: (i, 0)), pl.BlockSpec((HL, 2 * C), next_blk),
                  taps, taps, vec, vec, pl.BlockSpec((4, POOL_GW, POOL_GW), lambda i: (0, 0, 0)), vec, vec],
        out_specs=[pl.BlockSpec((tT, 5 * C), lambda i: (i, 0)), pl.BlockSpec((32 * 8, C), lambda i: (0, 0)),
                   pl.BlockSpec((8, C), lambda i: (0, 0)), pl.BlockSpec((4, POOL_GW, POOL_GW), lambda i: (0, 0, 0))],
        out_shape=[jax.ShapeDtypeStruct((T, 5 * C), BF16), jax.ShapeDtypeStruct((32 * 8, C), F32),
                   jax.ShapeDtypeStruct((8, C), F32), jax.ShapeDtypeStruct((4, POOL_GW, POOL_GW), F32)],
        scratch_shapes=[pltpu.VMEM((R1 + 8, C), F32), pltpu.VMEM((8, R1, C), F32), pltpu.VMEM((tT, C), F32)],
        compiler_params=_params("arbitrary"))(p, p, p, u1, u1, dy, dy, cw, cwr, lg, lb, pw, pb, sc)


def _softplus(z):
    u = jnp.exp(-jnp.abs(z))
    w = 1.0 + u
    l1p = jnp.where(w == 1.0, u, u * jnp.log(w) / jnp.where(w == 1.0, 1.0, w - 1.0))
    return jnp.maximum(z, 0.0) + l1p


def _lru_gates(xrx, cw_ref, cb_ref, wr_ref, br_ref, wi_ref, bi_ref, lam_ref):
    HL = ODD_HALO
    xc = cb_ref[...] + cw_ref[LRU_CONV_K - 1:LRU_CONV_K, :] * xrx[HL:]
    for k in range(LRU_CONV_K - 1):
        xc = xc + cw_ref[k:k + 1, :] * _shift_down(xrx, LRU_CONV_K - 1 - k)[HL:]
    xcb = xc.astype(BF16)
    rp, ip = [], []
    for hd in range(LRU_HEADS):
        cols = slice(hd * LRU_HD, (hd + 1) * LRU_HD)
        rp.append(jnp.dot(xcb[:, cols], wr_ref[hd], preferred_element_type=F32))
        ip.append(jnp.dot(xcb[:, cols], wi_ref[hd], preferred_element_type=F32))
    r = _sigmoid(jnp.concatenate(rp, axis=1) + br_ref[...])
    ig = _sigmoid(jnp.concatenate(ip, axis=1) + bi_ref[...])
    sp = _softplus(-lam_ref[...])
    log_a = (-LRU_C) * r * sp
    a = jnp.exp(log_a)
    mult = jnp.sqrt(-jnp.tanh(log_a) * (a * a + 1.0))
    return xc, xcb, r, ig, sp, a, mult


def _odd_mixer_fwd(p, cw, cb, wr, br, wi, bi, lam, name):
    T = p.shape[0]
    W = W_LRU
    tT, HL = MIX_TILE, ODD_HALO
    hb = tT // HL

    def body(pm_ref, ph_ref, cw_ref, cb_ref, wr_ref, br_ref, wi_ref, bi_ref, lam_ref, y_ref, hs_ref, carry_ref):
        i = pl.program_id(0)
        keep = (i > 0).astype(F32)

        @pl.when(i == 0)
        def _():
            carry_ref[...] = jnp.zeros_like(carry_ref)

        xrx = jnp.concatenate([ph_ref[:, 0:W] * keep, pm_ref[:, 0:W]], axis=0)
        xc, _, _, ig, _, a, mult = _lru_gates(xrx, cw_ref, cb_ref, wr_ref, br_ref, wi_ref, bi_ref, lam_ref)
        b = mult * (ig * xc)
        row = lax.broadcasted_iota(jnp.int32, (tT, 1), 0)
        s = 1
        while s < tT:
            ok = row >= s
            a_sh = jnp.where(ok, _shift_down(a, s), 1.0)
            b_sh = jnp.where(ok, _shift_down(b, s), 0.0)
            b = a * b_sh + b
            a = a * a_sh
            s *= 2
        hs = a * carry_ref[0:1, :] + b
        hs_ref[...] = hs
        carry_ref[...] = jnp.broadcast_to(hs[tT - 1:tT, :], (8, W))
        gt = pm_ref[:, W:2 * W]
        y_ref[...] = (hs * (gt * _sigmoid(gt))).astype(BF16)

    vec = pl.BlockSpec((1, W), lambda i: (0, 0))
    heads = pl.BlockSpec((LRU_HEADS, LRU_HD, LRU_HD), lambda i: (0, 0, 0))
    return _pallas(
        body, name=name, grid=(T // tT,),
        in_specs=[pl.BlockSpec((tT, 2 * W), lambda i: (i, 0)),
                  pl.BlockSpec((HL, 2 * W), lambda i: (jnp.maximum(i * hb - 1, 0), 0)),
                  pl.BlockSpec((8, W), lambda i: (0, 0)), vec, heads, vec, heads, vec, vec],
        out_specs=[pl.BlockSpec((tT, W), lambda i: (i, 0)), pl.BlockSpec((tT, W), lambda i: (i, 0))],
        out_shape=[jax.ShapeDtypeStruct((T, W), BF16), jax.ShapeDtypeStruct((T, W), F32)],
        scratch_shapes=[pltpu.VMEM((8, W), F32)],
        compiler_params=_params("arbitrary"))(p, p, cw, cb, wr, br, wi, bi, lam)


def _odd_mixer_bwd(p, hs, dy, cw, cb, wr, br, wi, bi, lam, name):
    T = p.shape[0]
    W = W_LRU
    tT, HL = MIX_TILE, ODD_HALO
    hb = tT // HL
    nT = T // tT

    def body(pm_ref, ph_ref, hsm_ref, hsh_ref, dy_ref, cw_ref, cb_ref, wr_ref, br_ref, wi_ref, bi_ref, lam_ref,
             dp_ref, dwr_ref, dwi_ref, dvec_ref, gcarry_ref, xcarry_ref):
        i = pl.program_id(0)
        keep = (i < nT - 1).astype(F32)

        @pl.when(i == 0)
        def _():
            gcarry_ref[...] = jnp.zeros_like(gcarry_ref)
            xcarry_ref[...] = jnp.zeros_like(xcarry_ref)

        xrx = jnp.concatenate([ph_ref[:, 0:W] * keep, pm_ref[:, 0:W]], axis=0)
        xc, xcb, r, ig, sp, a, mult = _lru_gates(xrx, cw_ref, cb_ref, wr_ref, br_ref, wi_ref, bi_ref, lam_ref)
        hs = hsm_ref[...]
        gt = pm_ref[:, W:2 * W]
        sg = _sigmoid(gt)
        dyv = dy_ref[...]
        dp_ref[:, W:2 * W] = (dyv * hs * _dsilu(gt, sg)).astype(BF16)
        row = lax.broadcasted_iota(jnp.int32, (tT, 1), 0)
        e = dyv * (gt * sg) + jnp.where(row == tT - 1, gcarry_ref[0:1, :], 0.0)
        m = jnp.where(row == tT - 1, 1.0, _shift_up(a, 1))
        s = 1
        while s < tT:
            ok = row < tT - s
            m_sh = jnp.where(ok, _shift_up(m, s), 1.0)
            e_sh = jnp.where(ok, _shift_up(e, s), 0.0)
            e = m * e_sh + e
            m = m * m_sh
            s *= 2
        G = e
        gcarry_ref[...] = jnp.broadcast_to(a[0:1, :] * G[0:1, :], (8, W))
        hs_prev = jnp.where(row == 0, hsh_ref[HL - 1:HL, :] * keep, _shift_down(hs, 1))
        da = G * hs_prev
        dmult = G * (ig * xc)
        di = G * mult * xc
        dxc = G * mult * ig
        dlog_a = da * a - dmult * (a * a) / mult
        drp = dlog_a * ((-LRU_C) * sp) * r * (1.0 - r)
        dip = di * ig * (1.0 - ig)
        dlam = jnp.sum(dlog_a * ((-LRU_C) * r), axis=0, keepdims=True) * (-_sigmoid(-lam_ref[...]))
        drb, dib = drp.astype(BF16), dip.astype(BF16)
        back = []
        for hd in range(LRU_HEADS):
            cols = slice(hd * LRU_HD, (hd + 1) * LRU_HD)
            back.append(_nt(drb[:, cols], wr_ref[hd]) + _nt(dib[:, cols], wi_ref[hd]))
            dwr_h = _tn(xcb[:, cols], drb[:, cols])
            dwi_h = _tn(xcb[:, cols], dib[:, cols])

            @pl.when(i == 0)
            def _(hd=hd, dwr_h=dwr_h, dwi_h=dwi_h):
                dwr_ref[hd] = dwr_h
                dwi_ref[hd] = dwi_h

            @pl.when(i > 0)
            def _(hd=hd, dwr_h=dwr_h, dwi_h=dwi_h):
                dwr_ref[hd] += dwr_h
                dwi_ref[hd] += dwi_h

        dxc = dxc + jnp.concatenate(back, axis=1)
        dxcx = jnp.concatenate([dxc, xcarry_ref[...]], axis=0)
        dxr = cw_ref[LRU_CONV_K - 1:LRU_CONV_K, :] * dxc
        rows = []
        for k in range(LRU_CONV_K - 1):
            j = LRU_CONV_K - 1 - k
            dxr = dxr + cw_ref[k:k + 1, :] * _shift_up(dxcx, j)[0:tT]
            rows.append(jnp.sum(dxc * _shift_down(xrx, j)[HL:], axis=0, keepdims=True))
        rows.append(jnp.sum(dxc * xrx[HL:], axis=0, keepdims=True))
        dp_ref[:, 0:W] = dxr.astype(BF16)
        xcarry_ref[...] = dxc[0:8]
        rows += [jnp.sum(dxc, axis=0, keepdims=True), jnp.sum(drp, axis=0, keepdims=True),
                 jnp.sum(dip, axis=0, keepdims=True), dlam]
        vecs = jnp.concatenate(rows, axis=0)

        @pl.when(i == 0)
        def _():
            dvec_ref[...] = vecs

        @pl.when(i > 0)
        def _():
            dvec_ref[...] += vecs

    vec = pl.BlockSpec((1, W), lambda i: (0, 0))
    heads = pl.BlockSpec((LRU_HEADS, LRU_HD, LRU_HD), lambda i: (0, 0, 0))

    def tile(i):
        return (nT - 1 - i, 0)

    def prev_blk(i):
        return (jnp.maximum((nT - 1 - i) * hb - 1, 0), 0)

    return _pallas(
        body, name=name, grid=(nT,),
        in_specs=[pl.BlockSpec((tT, 2 * W), tile), pl.BlockSpec((HL, 2 * W), prev_blk),
                  pl.BlockSpec((tT, W), tile), pl.BlockSpec((HL, W), prev_blk), pl.BlockSpec((tT, W), tile),
                  pl.BlockSpec((8, W), lambda i: (0, 0)), vec, heads, vec, heads, vec, vec],
        out_specs=[pl.BlockSpec((tT, 2 * W), tile), heads, heads, pl.BlockSpec((8, W), lambda i: (0, 0))],
        out_shape=[jax.ShapeDtypeStruct((T, 2 * W), BF16), jax.ShapeDtypeStruct((LRU_HEADS, LRU_HD, LRU_HD), F32),
                   jax.ShapeDtypeStruct((LRU_HEADS, LRU_HD, LRU_HD), F32), jax.ShapeDtypeStruct((8, W), F32)],
        scratch_shapes=[pltpu.VMEM((8, W), F32), pltpu.VMEM((8, W), F32)],
        compiler_params=_params("arbitrary"))(p, p, hs, hs, dy, cw, cb, wr, br, wi, bi, lam)


def _pad_rows(a, rows):
    return jnp.concatenate([a, jnp.zeros((rows - a.shape[0], a.shape[1]), a.dtype)], axis=0)


def _local_step(x, tgt, wt):
    even, odd = wt["even"], wt["odd"]
    depth = len(even) + len(odd)
    h = x
    saved = []
    for layer in range(depth):
        j = layer // 2
        if layer % 2 == 0:
            w = even[j]
            p, n = _in_proj(h, w["norm"], wt["w_in_even"], j, "in_proj_even")
            y, aux = _even_mixer_fwd(p, w["conv_w"], w["conv_b"], w["ln_g"], w["ln_b"], w["pool_w"], w["pool_b"],
                                     w["pool_scale"], "even_mixer_fwd")
            h_next = _out_proj(y, wt["w_out_even"], j, h, "out_proj_even")
        else:
            w = odd[j]
            p, n = _in_proj(h, w["norm"], wt["w_in_odd"], j, "in_proj_odd")
            y, aux = _odd_mixer_fwd(p, w["conv_w"], w["conv_b"], w["w_rg"], w["b_rg"], w["w_ig"], w["b_ig"], w["lam"],
                                    "odd_mixer_fwd")
            h_next = _out_proj(y, wt["w_out_odd"], j, h, "out_proj_odd")
        saved.append((h, n, p, aux, y))
        h = h_next
    dh, dhb, d_final, loss = _loss_head(h, wt["final_norm"], tgt)
    g_even = [None] * len(even)
    g_odd = [None] * len(odd)
    big = dict(w_in_even=None, w_out_even=None, w_in_odd=None, w_out_odd=None)
    for layer in reversed(range(depth)):
        h, n, p, aux, y = saved[layer]
        j = layer // 2
        if layer % 2 == 0:
            w = even[j]
            big["w_out_even"] = _dw_out(y, dhb, j, len(even), big["w_out_even"], "dw_out_even")
            dy = _dy_proj(dhb, wt["w_out_even"], j, "dy_proj_even")
            dp, dcw, dvec, dpw = _even_mixer_bwd(p, aux, dy, w["conv_w"], w["conv_w_rev"], w["ln_g"], w["ln_b"],
                                                 w["pool_w"], w["pool_b"], w["pool_scale"], "even_mixer_bwd")
            big["w_in_even"] = _dw_in(n, dp, N_CHIPS, j, len(even), big["w_in_even"], "dw_in_even")
            dh, dhb, dnorm = _dn_proj(dp, wt["w_in_even"], j, h, w["norm"], dh, "dn_proj_even")
            g_even[j] = dict(conv_w=dcw, vec=dvec, pool_w=dpw, norm=dnorm)
        else:
            w = odd[j]
            big["w_out_odd"] = _dw_out(y, dhb, j, len(odd), big["w_out_odd"], "dw_out_odd")
            dy = _dy_proj(dhb, wt["w_out_odd"], j, "dy_proj_odd")
            dp, dwr, dwi, dvec = _odd_mixer_bwd(p, aux, dy, w["conv_w"], w["conv_b"], w["w_rg"], w["b_rg"], w["w_ig"],
                                                w["b_ig"], w["lam"], "odd_mixer_bwd")
            big["w_in_odd"] = _dw_in(n, dp, N_CHIPS, j, len(odd), big["w_in_odd"], "dw_in_odd")
            dh, dhb, dnorm = _dn_proj(dp, wt["w_in_odd"], j, h, w["norm"], dh, "dn_proj_odd")
            g_odd[j] = dict(w_rg=dwr, w_ig=dwi, vec=dvec, norm=dnorm)
    return loss, dh, big, g_even, g_odd, d_final


ANY = pl.BlockSpec(memory_space=pl.ANY)


def _mesh_pos():
    return lax.axis_index("x"), lax.axis_index("y"), lax.axis_index("c")


def _other_chips(x, y):
    return [(1 - x, y), (x, 1 - y), (1 - x, 1 - y)]


def _other_devices(x, y, c):
    out = []
    for p in range(1, N_DEV):
        out.append((1 - x if p & 4 else x, 1 - y if p & 2 else y, 1 - c if p & 1 else c))
    return out


def _remote(src, dst, ssem, rsem, dev):
    return pltpu.make_async_remote_copy(src_ref=src, dst_ref=dst, send_sem=ssem, recv_sem=rsem, device_id=dev,
                                        device_id_type=MESH)


def _comm_call(body, name, ins, out_shape, scratch, aliases=None):
    return _pallas(body, name=name, in_specs=[ANY] * len(ins), out_specs=[ANY] * len(out_shape), out_shape=out_shape,
                   scratch_shapes=scratch, input_output_aliases=aliases or {},
                   compiler_params=pltpu.CompilerParams(has_side_effects=True))(*ins)


def _cast_shard(w, k):
    L, R, C = w.shape
    tr = _row_tile(R, C)

    def body(k_ref, w_ref, o_ref):
        o_ref[...] = w_ref[...].astype(BF16)

    grid_spec = pltpu.PrefetchScalarGridSpec(
        num_scalar_prefetch=1, grid=(L, R // tr),
        in_specs=[pl.BlockSpec((None, tr, C), lambda l, i, kr: (l, i, 0))],
        out_specs=pl.BlockSpec((None, None, tr, C), lambda l, i, kr: (l, kr[0], i, 0)))
    return _pallas(body, name="cast_shard", grid_spec=grid_spec,
                   out_shape=jax.ShapeDtypeStruct((L, N_CHIPS, R, C), BF16),
                   compiler_params=_params("parallel", "parallel"))(k, w)


def _gather_weights(big, small):
    nA = len(big)
    half = [a.shape[2] // 2 for a in big]

    def body(*refs):
        ins, outs = refs[:nA + 1], refs[nA + 1:2 * nA + 2]
        ssem, rsem, fsem, frsem, lsem = refs[2 * nA + 2:]
        x, y, c = _mesh_pos()
        k = 2 * x + y
        chips = _other_chips(x, y)
        sib = (x, y, 1 - c)

        def slab(a, chip, core):
            return outs[a].at[:, chip, pl.ds(core * half[a], half[a]), :]

        local = [pltpu.make_async_copy(ins[nA], outs[nA].at[k], lsem.at[0])]
        for cp in local:
            cp.start()
        sends = []
        for j, (ox, oy) in enumerate(chips):
            for a in range(nA):
                sends.append(_remote(slab(a, k, c), slab(a, k, c), ssem.at[a, j], rsem.at[a, j], (ox, oy, c)))
            sends.append(_remote(ins[nA], outs[nA].at[k], ssem.at[nA, j], rsem.at[nA, j], (ox, oy, c)))
        for cp in sends:
            cp.start()
        for j, (ox, oy) in enumerate(chips):
            kj = 2 * ox + oy
            for a in range(nA):
                got = slab(a, kj, c)
                _remote(got, got, ssem.at[a, j], rsem.at[a, j], (ox, oy, c)).wait_recv()
                fw = _remote(got, got, fsem.at[a, j], frsem.at[a, j], sib)
                fw.start()
                sends.append(fw)
            gs = outs[nA].at[kj]
            _remote(gs, gs, ssem.at[nA, j], rsem.at[nA, j], (ox, oy, c)).wait_recv()
        for j, (ox, oy) in enumerate(chips):
            kj = 2 * ox + oy
            for a in range(nA):
                theirs = slab(a, kj, 1 - c)
                _remote(theirs, theirs, fsem.at[a, j], frsem.at[a, j], sib).wait_recv()
        for cp in sends:
            cp.wait_send()
        for cp in local:
            cp.wait()

    out_shape = [jax.ShapeDtypeStruct(a.shape, a.dtype) for a in big]
    out_shape.append(jax.ShapeDtypeStruct((N_CHIPS,) + small.shape, small.dtype))
    scratch = [pltpu.SemaphoreType.DMA((nA + 1, 3)), pltpu.SemaphoreType.DMA((nA + 1, 3)),
               pltpu.SemaphoreType.DMA((nA, 3)), pltpu.SemaphoreType.DMA((nA, 3)), pltpu.SemaphoreType.DMA((1,))]
    return _comm_call(body, "gather_weights", list(big) + [small], out_shape, scratch, {a: a for a in range(nA)})


def _exchange_cores(big, small):
    nA = len(big)
    half = [a.shape[2] // 2 for a in big]

    def body(*refs):
        ins, outs = refs[:nA + 1], refs[nA + 1:2 * nA + 2]
        ssem, rsem, ssem2, rsem2, lsem = refs[2 * nA + 2:]
        x, y, c = _mesh_pos()
        me = 4 * x + 2 * y + c
        sib = (x, y, 1 - c)
        peers = _other_devices(x, y, c)
        local = pltpu.make_async_copy(ins[nA], outs[nA].at[me], lsem.at[0])
        local.start()
        sends = []
        for a in range(nA):
            src = ins[a].at[:, :, pl.ds((1 - c) * half[a], half[a]), :]
            sends.append(_remote(src, outs[a], ssem.at[a], rsem.at[a], sib))
        for p, dev in enumerate(peers):
            sends.append(_remote(ins[nA], outs[nA].at[me], ssem2.at[p], rsem2.at[p], dev))
        for cp in sends:
            cp.start()
        for a in range(nA):
            _remote(outs[a], outs[a], ssem.at[a], rsem.at[a], sib).wait_recv()
        for p, (px, py, pc) in enumerate(peers):
            got = outs[nA].at[4 * px + 2 * py + pc]
            _remote(got, got, ssem2.at[p], rsem2.at[p], (px, py, pc)).wait_recv()
        for cp in sends:
            cp.wait_send()
        local.wait()

    out_shape = [jax.ShapeDtypeStruct((a.shape[0], N_CHIPS, h, a.shape[3]), a.dtype) for a, h in zip(big, half)]
    out_shape.append(jax.ShapeDtypeStruct((N_DEV,) + small.shape, small.dtype))
    scratch = [pltpu.SemaphoreType.DMA((nA,)), pltpu.SemaphoreType.DMA((nA,)), pltpu.SemaphoreType.DMA((N_DEV - 1,)),
               pltpu.SemaphoreType.DMA((N_DEV - 1,)), pltpu.SemaphoreType.DMA((1,))]
    return _comm_call(body, "exchange_cores", list(big) + [small], out_shape, scratch)


def _exchange_chips(parts):
    nA = len(parts)

    def body(*refs):
        ins, outs = refs[:nA], refs[nA:2 * nA]
        ssem, rsem = refs[2 * nA:]
        x, y, c = _mesh_pos()
        k = 2 * x + y
        chips = _other_chips(x, y)
        sends = []
        for j, (ox, oy) in enumerate(chips):
            for a in range(nA):
                sends.append(_remote(ins[a].at[:, 2 * ox + oy], outs[a].at[:, k], ssem.at[a, j], rsem.at[a, j],
                                     (ox, oy, c)))
        for cp in sends:
            cp.start()
        for j, (ox, oy) in enumerate(chips):
            for a in range(nA):
                got = outs[a].at[:, 2 * ox + oy]
                _remote(got, got, ssem.at[a, j], rsem.at[a, j], (ox, oy, c)).wait_recv()
        for cp in sends:
            cp.wait_send()

    out_shape = [jax.ShapeDtypeStruct(a.shape, a.dtype) for a in parts]
    scratch = [pltpu.SemaphoreType.DMA((nA, 3)), pltpu.SemaphoreType.DMA((nA, 3))]
    return _comm_call(body, "exchange_chips", list(parts), out_shape, scratch)


def _exchange_final(grads, everywhere):
    nA = len(grads)
    n_remote = sum(N_DEV - 1 if ev else 1 for ev in everywhere)

    def body(*refs):
        outs = refs[nA:2 * nA]
        ssem, rsem = refs[2 * nA:]
        x, y, c = _mesh_pos()
        k = 2 * x + y
        sib = (x, y, 1 - c)
        peers = _other_devices(x, y, c)
        sends, waits = [], []
        s = 0
        for a in range(nA):
            if everywhere[a]:
                r2 = grads[a].shape[1] // N_DEV
                mine = outs[a].at[:, pl.ds((2 * k + c) * r2, r2), :]
                for (px, py, pc) in peers:
                    sends.append(_remote(mine, mine, ssem.at[s], rsem.at[s], (px, py, pc)))
                    got = outs[a].at[:, pl.ds((2 * (2 * px + py) + pc) * r2, r2), :]
                    waits.append(_remote(got, got, ssem.at[s], rsem.at[s], (px, py, pc)))
                    s += 1
            else:
                r2 = grads[a].shape[1] // 2
                mine = outs[a].at[:, pl.ds(c * r2, r2), :]
                sends.append(_remote(mine, mine, ssem.at[s], rsem.at[s], sib))
                got = outs[a].at[:, pl.ds((1 - c) * r2, r2), :]
                waits.append(_remote(got, got, ssem.at[s], rsem.at[s], sib))
                s += 1
        for cp in sends:
            cp.start()
        for cp in waits:
            cp.wait_recv()
        for cp in sends:
            cp.wait_send()

    out_shape = [jax.ShapeDtypeStruct(g.shape, g.dtype) for g in grads]
    scratch = [pltpu.SemaphoreType.DMA((n_remote,)), pltpu.SemaphoreType.DMA((n_remote,))]
    return _comm_call(body, "exchange_final", list(grads), out_shape, scratch, {a: a for a in range(nA)})


BLOCK_BYTES = 1 << 20


def _row_tile(rows, cols, mult=16):
    best = mult
    for t in range(mult, rows + 1, mult):
        if rows % t == 0 and t * cols * 4 <= BLOCK_BYTES:
            best = t
    return best


def _add_cores(own, recv, pos):
    L, _, R, C = own.shape
    r2 = R // 2
    tr = _row_tile(r2, C)
    nb = r2 // tr

    def body(pos_ref, a_ref, r_ref, o_ref):
        o_ref[...] = (a_ref[...].astype(F32) + r_ref[...].astype(F32)).astype(BF16)

    blk = (None, None, tr, C)
    grid_spec = pltpu.PrefetchScalarGridSpec(
        num_scalar_prefetch=1, grid=(L, N_CHIPS, nb),
        in_specs=[pl.BlockSpec(blk, lambda l, s, i, pr: (l, s, pr[1] * nb + i, 0)),
                  pl.BlockSpec(blk, lambda l, s, i, pr: (l, s, i, 0))],
        out_specs=pl.BlockSpec(blk, lambda l, s, i, pr: (l, s, i, 0)))
    return _pallas(body, name="add_cores", grid_spec=grid_spec,
                   out_shape=jax.ShapeDtypeStruct((L, N_CHIPS, r2, C), BF16),
                   compiler_params=_params("parallel", "parallel", "parallel"))(pos, own, recv)


def _sum_chips(own, recv, pos, everywhere):
    L, _, r2, C = own.shape
    tr = _row_tile(r2, 2 * C)
    nb = r2 // tr

    def body(pos_ref, a_ref, r_ref, o_ref):
        acc = None
        for s in range(N_CHIPS):
            term = jnp.where(pos_ref[0] == s, a_ref[...], r_ref[s]).astype(F32)
            acc = term if acc is None else acc + term
        o_ref[...] = acc

    if everywhere:
        def out_map(l, i, pr):
            return (l, (2 * pr[0] + pr[1]) * nb + i, 0)
    else:
        def out_map(l, i, pr):
            return (l, pr[1] * nb + i, 0)

    grid_spec = pltpu.PrefetchScalarGridSpec(
        num_scalar_prefetch=1, grid=(L, nb),
        in_specs=[pl.BlockSpec((None, None, tr, C), lambda l, i, pr: (l, pr[0], i, 0)),
                  pl.BlockSpec((None, N_CHIPS, tr, C), lambda l, i, pr: (l, 0, i, 0))],
        out_specs=pl.BlockSpec((None, tr, C), out_map))
    rows = (N_DEV if everywhere else 2) * r2
    return _pallas(body, name="sum_chips", grid_spec=grid_spec, out_shape=jax.ShapeDtypeStruct((L, rows, C), F32),
                   compiler_params=_params("parallel", "parallel"))(pos, own, recv)


def _sum_devices(parts):
    n, R, C = parts.shape
    tr = _row_tile(R, C * n, 8)

    def body(p_ref, o_ref):
        acc = p_ref[0]
        for s in range(1, n):
            acc = acc + p_ref[s]
        o_ref[...] = acc

    return _pallas(body, name="sum_devices", grid=(R // tr,), in_specs=[pl.BlockSpec((n, tr, C), lambda i: (0, i, 0))],
                   out_specs=pl.BlockSpec((tr, C), lambda i: (i, 0)), out_shape=jax.ShapeDtypeStruct((R, C), F32),
                   compiler_params=_params("parallel"))(parts)


def _adamw(w, g, m, v, name):
    L, R, C = w.shape
    tr = _row_tile(R, C, 8)

    def body(w_ref, g_ref, m_ref, v_ref, d_ref, m2_ref, v2_ref):
        gg = g_ref[...]
        m2 = ADAM_B1 * m_ref[...] + (1.0 - ADAM_B1) * gg
        v2 = ADAM_B2 * v_ref[...] + (1.0 - ADAM_B2) * (gg * gg)
        m_hat = m2 / (1.0 - ADAM_B1 ** ADAM_STEP)
        v_hat = v2 / (1.0 - ADAM_B2 ** ADAM_STEP)
        d_ref[...] = -ADAM_LR * (m_hat / (jnp.sqrt(v_hat) + ADAM_EPS) + ADAM_WD * w_ref[...])
        m2_ref[...] = m2
        v2_ref[...] = v2

    blk = pl.BlockSpec((1, tr, C), lambda l, i: (l, i, 0))
    shp = jax.ShapeDtypeStruct((L, R, C), F32)
    return _pallas(body, name=name, grid=(L, R // tr), in_specs=[blk] * 4, out_specs=[blk] * 3, out_shape=[shp] * 3,
                   compiler_params=_params("parallel", "parallel"))(w, g, m, v)


WEIGHTS = ("norm_even", "w_in_even", "conv_a_w", "conv_a_b", "ln_a_g", "ln_a_b", "pool_w", "pool_b", "pool_scale",
           "w_out_even", "norm_odd", "w_in_odd", "conv_c_w", "conv_c_b", "w_rg", "b_rg", "w_ig", "b_ig", "lru_lambda",
           "w_out_odd", "final_norm")
BIG = ("w_in_even", "w_out_even", "pool_w", "w_in_odd", "w_out_odd", "w_rg", "w_ig")
SMALL = tuple(n for n in WEIGHTS if n not in BIG)
SMALL_SHARDED = ("conv_a_w", "pool_b", "norm_odd", "conv_c_w", "conv_c_b", "b_rg", "b_ig", "lru_lambda")


def _pack(arrs):
    flat = jnp.concatenate([a.reshape(-1) for a in arrs])
    rows = -(-flat.shape[0] // (64 * 128)) * 64
    return jnp.pad(flat, (0, rows * 128 - flat.shape[0])).reshape(rows, 128)


def _unpack(buf, shapes, lead=()):
    flat = buf.reshape(tuple(lead) + (-1,))
    out, o = [], 0
    for s in shapes:
        n = 1
        for d in s:
            n *= d
        out.append(flat[..., o:o + n].reshape(tuple(lead) + tuple(s)))
        o += n
    return out


def _shard(full, axis, k):
    n = full.shape[axis] // N_CHIPS
    return lax.dynamic_slice_in_dim(full, k * n, n, axis)


def kernel(x, norm_even, w_in_even, conv_a_w, conv_a_b, ln_a_g, ln_a_b, pool_w, pool_b, pool_scale, w_out_even, norm_odd, w_in_odd, conv_c_w, conv_c_b, w_rg, b_rg, w_ig, b_ig, lru_lambda, w_out_odd, final_norm, loss_target, m_norm_even, m_w_in_even, m_conv_a_w, m_conv_a_b, m_ln_a_g, m_ln_a_b, m_pool_w, m_pool_b, m_pool_scale, m_w_out_even, m_norm_odd, m_w_in_odd, m_conv_c_w, m_conv_c_b, m_w_rg, m_b_rg, m_w_ig, m_b_ig, m_lru_lambda, m_w_out_odd, m_final_norm, v_norm_even, v_w_in_even, v_conv_a_w, v_conv_a_b, v_ln_a_g, v_ln_a_b, v_pool_w, v_pool_b, v_pool_scale, v_w_out_even, v_norm_odd, v_w_in_odd, v_conv_c_w, v_conv_c_b, v_w_rg, v_b_rg, v_w_ig, v_b_ig, v_lru_lambda, v_w_out_odd, v_final_norm):
    P = dict(locals())
    xi, yi, ci = _mesh_pos()
    k = 2 * xi + yi
    L = w_in_even.shape[0]
    D = D_MODEL

    pos = jnp.stack([k, ci]).astype(jnp.int32)
    big = [_cast_shard(w, pos[0:1]) for w in
           (w_in_even, w_out_even, w_in_odd, w_out_odd, pool_w.reshape(L, 4 * 64, POOL_GW))]
    g_wie, g_woe, g_wio, g_woo, g_pw, g_small = _gather_weights(big, _pack([P[n] for n in SMALL_SHARDED]))
    full = {}
    for n, a in zip(SMALL_SHARDED, _unpack(g_small, [P[n].shape for n in SMALL_SHARDED], lead=(N_CHIPS,))):
        a = jnp.moveaxis(a, 0, -2)
        full[n] = a.reshape(a.shape[:-2] + (N_CHIPS * a.shape[-1],))
    pw_full = g_pw.reshape(L, N_CHIPS, 4, 64, POOL_GW).transpose(0, 2, 1, 3, 4).reshape(L, 4, POOL_GW, POOL_GW)
    even, odd = [], []
    for j in range(L):
        cw = full["conv_a_w"][j]
        even.append(dict(norm=norm_even[j][None], conv_w=_pad_rows(cw, 32), conv_w_rev=_pad_rows(cw[::-1], 32),
                         conv_b=conv_a_b[j][None], ln_g=ln_a_g[j][None], ln_b=ln_a_b[j][None], pool_w=pw_full[j],
                         pool_b=full["pool_b"][j].reshape(1, D), pool_scale=pool_scale[j][None]))
        odd.append(dict(norm=full["norm_odd"][j][None], conv_w=_pad_rows(full["conv_c_w"][j], 8),
                        conv_b=full["conv_c_b"][j][None], w_rg=w_rg[j].astype(BF16), b_rg=full["b_rg"][j][None],
                        w_ig=w_ig[j].astype(BF16), b_ig=full["b_ig"][j][None], lam=full["lru_lambda"][j][None]))
    wt = dict(w_in_even=g_wie, w_out_even=g_woe.reshape(L, 2 * D, D), w_in_odd=g_wio,
              w_out_odd=g_woo.reshape(L, W_LRU, D), even=even, odd=odd, final_norm=final_norm[None])

    loss, grad_x, gbig, g_even, g_odd, d_final = _local_step(x[0], loss_target[0], wt)
    loss = lax.psum(loss[0, 0], ("x", "y", "c"))

    dpw = jnp.stack([g["pool_w"] for g in g_even]).reshape(L, 4, N_CHIPS, 64, POOL_GW).transpose(0, 2, 1, 3, 4)
    parts = [gbig["w_in_even"], gbig["w_out_even"].reshape(L, N_CHIPS, -1, D),
             dpw.reshape(L, N_CHIPS, 4 * 64, POOL_GW).astype(BF16),
             gbig["w_in_odd"], gbig["w_out_odd"].reshape(L, N_CHIPS, -1, D),
             jnp.stack([g["w_rg"] for g in g_odd]).reshape(L, N_CHIPS, -1, LRU_HD).astype(BF16),
             jnp.stack([g["w_ig"] for g in g_odd]).reshape(L, N_CHIPS, -1, LRU_HD).astype(BF16)]
    everywhere = [False, False, False, False, False, True, True]
    small_g = []
    for j in range(L):
        ge, go = g_even[j], g_odd[j]
        small_g += [ge["conv_w"].reshape(32, 8, D).sum(axis=1)[:CONV_K], ge["vec"][0:5], ge["norm"], go["vec"], go["norm"]]
    small_g.append(d_final)
    small_shapes = [a.shape for a in small_g]
    recv = _exchange_cores(parts, _pack(small_g))
    pair_sums = [_add_cores(a, r, pos) for a, r in zip(parts, recv[:-1])]
    from_chips = _exchange_chips(pair_sums)
    gw = _exchange_final([_sum_chips(a, r, pos, ev) for a, r, ev in zip(pair_sums, from_chips, everywhere)], everywhere)
    sg = _unpack(_sum_devices(recv[-1]), small_shapes)

    grads = dict(w_in_even=gw[0], w_out_even=gw[1], pool_w=gw[2].reshape(pool_w.shape), w_in_odd=gw[3], w_out_odd=gw[4],
                 w_rg=gw[5].reshape(w_rg.shape), w_ig=gw[6].reshape(w_ig.shape), final_norm=sg[-1][0])
    ev = [sg[5 * j + 1] for j in range(L)]
    ov = [sg[5 * j + 3] for j in range(L)]
    grads["conv_a_w"] = _shard(jnp.stack([sg[5 * j] for j in range(L)]), 2, k)
    grads["norm_even"] = jnp.stack([sg[5 * j + 2][0] for j in range(L)])
    grads["norm_odd"] = _shard(jnp.stack([sg[5 * j + 4][0] for j in range(L)]), 1, k)
    for r, n in enumerate(("conv_a_b", "ln_a_g", "ln_a_b", "pool_scale")):
        grads[n] = jnp.stack([e[r] for e in ev])
    grads["pool_b"] = _shard(jnp.stack([e[4].reshape(4, POOL_GW) for e in ev]), 2, k)
    grads["conv_c_w"] = _shard(jnp.stack([o[0:4] for o in ov]), 2, k)
    for r, n in zip((4, 5, 6, 7), ("conv_c_b", "b_rg", "b_ig", "lru_lambda")):
        grads[n] = _shard(jnp.stack([o[r] for o in ov]), 1, k)

    delta, new_m, new_v = {}, {}, {}
    for n in BIG:
        s3 = (L, -1, P[n].shape[-1])
        d, m2, v2 = _adamw(P[n].reshape(s3), grads[n].reshape(s3), P["m_" + n].reshape(s3), P["v_" + n].reshape(s3), "adamw")
        delta[n], new_m[n], new_v[n] = d.reshape(P[n].shape), m2.reshape(P[n].shape), v2.reshape(P[n].shape)
    shapes = [P[n].shape for n in SMALL]
    packed = [_pack([src[n] for n in SMALL])[None] for src in
              (P, grads, {n: P["m_" + n] for n in SMALL}, {n: P["v_" + n] for n in SMALL})]
    for res, out in zip(_adamw(*packed, "adamw_small"), (delta, new_m, new_v)):
        for n, a in zip(SMALL, _unpack(res[0], shapes)):
            out[n] = a

    return (loss, grad_x[None], *[grads[n] for n in WEIGHTS], *[delta[n] for n in WEIGHTS],
            *[new_m[n] for n in WEIGHTS], *[new_v[n] for n in WEIGHTS])
```

```python
import functools

import jax
import jax.numpy as jnp
from jax import lax
from jax.experimental import pallas as pl
from jax.experimental.pallas import tpu as pltpu

F32 = jnp.float32
BF16 = jnp.bfloat16
MESH = pl.DeviceIdType.MESH

D_MODEL = 1024
N_CHIPS = 4
N_DEV = 8
EPS_RMS = 1e-6
EPS_LN = 1e-5
CONV_K = 31
POOL_WINDOWS = (2, 4, 8, 16)
POOL_GW = 256
LRU_HEADS = 12
LRU_HD = 128
W_LRU = LRU_HEADS * LRU_HD
LRU_CONV_K = 4
LRU_C = 8.0
ADAM_LR = 0.001
ADAM_B1 = 0.9
ADAM_B2 = 0.999
ADAM_EPS = 1e-08
ADAM_WD = 0.01
ADAM_STEP = 10

VMEM_LIMIT_BYTES = 56 * 1024 * 1024
ROW_TILE = 512
MIX_TILE = 256
EVEN_HALO = 32
ODD_HALO = 8


def _pallas(body, **kw):
    return pl.pallas_call(body, **kw)


def _params(*sem):
    return pltpu.CompilerParams(dimension_semantics=sem if sem else None, vmem_limit_bytes=VMEM_LIMIT_BYTES)


def _sigmoid(x):
    return 1.0 / (1.0 + jnp.exp(-x))


def _dsilu(x, s):
    return s * (1.0 + x * (1.0 - s))


def _nt(a, b):
    return lax.dot_general(a, b, (((1,), (1,)), ((), ())), preferred_element_type=F32)


def _tn(a, b):
    return lax.dot_general(a, b, (((0,), (0,)), ((), ())), preferred_element_type=F32)


def _in_proj(h, g, wg, layer, name):
    T, D = h.shape
    _, nblk, _, nb = wg.shape

    def body(h_ref, g_ref, w_ref, p_ref, n_ref):
        @pl.when(pl.program_id(1) == 0)
        def _():
            x = h_ref[...]
            r = lax.rsqrt(jnp.mean(x * x, axis=-1, keepdims=True) + EPS_RMS)
            n_ref[...] = (x * r * g_ref[...]).astype(BF16)

        p_ref[...] = jnp.dot(n_ref[...], w_ref[0], preferred_element_type=F32)

    return _pallas(
        body, name=name, grid=(T // ROW_TILE, nblk),
        in_specs=[pl.BlockSpec((ROW_TILE, D), lambda i, j: (i, 0)), pl.BlockSpec((1, D), lambda i, j: (0, 0)),
                  pl.BlockSpec((None, 1, D, nb), lambda i, j: (layer, j, 0, 0))],
        out_specs=[pl.BlockSpec((ROW_TILE, nb), lambda i, j: (i, j)), pl.BlockSpec((ROW_TILE, D), lambda i, j: (i, 0))],
        out_shape=[jax.ShapeDtypeStruct((T, nblk * nb), F32), jax.ShapeDtypeStruct((T, D), BF16)],
        compiler_params=_params("parallel", "arbitrary"))(h, g, wg)


def _out_proj(y, w, layer, hres, name):
    T, K = y.shape
    D = w.shape[2]

    def body(y_ref, w_ref, r_ref, o_ref):
        o_ref[...] = r_ref[...] + jnp.dot(y_ref[...], w_ref[...], preferred_element_type=F32)

    return _pallas(
        body, name=name, grid=(T // ROW_TILE,),
        in_specs=[pl.BlockSpec((ROW_TILE, K), lambda i: (i, 0)), pl.BlockSpec((None, K, D), lambda i: (layer, 0, 0)),
                  pl.BlockSpec((ROW_TILE, D), lambda i: (i, 0))],
        out_specs=pl.BlockSpec((ROW_TILE, D), lambda i: (i, 0)),
        out_shape=jax.ShapeDtypeStruct((T, D), F32),
        compiler_params=_params("parallel"))(y, w, hres)


def _dy_proj(dout, w, layer, name):
    T, D = dout.shape
    K = w.shape[1]

    def body(d_ref, w_ref, o_ref):
        o_ref[...] = _nt(d_ref[...], w_ref[...])

    return _pallas(
        body, name=name, grid=(T // ROW_TILE,),
        in_specs=[pl.BlockSpec((ROW_TILE, D), lambda i: (i, 0)), pl.BlockSpec((None, K, D), lambda i: (layer, 0, 0))],
        out_specs=pl.BlockSpec((ROW_TILE, K), lambda i: (i, 0)),
        out_shape=jax.ShapeDtypeStruct((T, K), F32),
        compiler_params=_params("parallel"))(dout, w)


def _dn_proj(dp, wg, layer, h, g, dres, name):
    T, D = h.shape
    _, nblk, _, nb = wg.shape

    def body(dp_ref, w_ref, h_ref, g_ref, dres_ref, dh_ref, dhb_ref, dg_ref, acc_ref):
        i, j = pl.program_id(0), pl.program_id(1)
        part = _nt(dp_ref[...], w_ref[0])

        @pl.when(j == 0)
        def _():
            acc_ref[...] = part

        @pl.when(j > 0)
        def _():
            acc_ref[...] += part

        @pl.when(j == nblk - 1)
        def _():
            x = h_ref[...]
            r = lax.rsqrt(jnp.mean(x * x, axis=-1, keepdims=True) + EPS_RMS)
            dn = acc_ref[...]
            q = dn * g_ref[...]
            dh = dres_ref[...] + r * q - x * ((r * r * r) * jnp.mean(q * x, axis=-1, keepdims=True))
            dh_ref[...] = dh
            dhb_ref[...] = dh.astype(BF16)
            dgp = jnp.sum(dn * (x * r), axis=0, keepdims=True)

            @pl.when(i == 0)
            def _():
                dg_ref[...] = dgp

            @pl.when(i > 0)
            def _():
                dg_ref[...] += dgp

    return _pallas(
        body, name=name, grid=(T // ROW_TILE, nblk),
        in_specs=[pl.BlockSpec((ROW_TILE, nb), lambda i, j: (i, j)),
                  pl.BlockSpec((None, 1, D, nb), lambda i, j: (layer, j, 0, 0)),
                  pl.BlockSpec((ROW_TILE, D), lambda i, j: (i, 0)), pl.BlockSpec((1, D), lambda i, j: (0, 0)),
                  pl.BlockSpec((ROW_TILE, D), lambda i, j: (i, 0))],
        out_specs=[pl.BlockSpec((ROW_TILE, D), lambda i, j: (i, 0)), pl.BlockSpec((ROW_TILE, D), lambda i, j: (i, 0)),
                   pl.BlockSpec((1, D), lambda i, j: (0, 0))],
        out_shape=[jax.ShapeDtypeStruct((T, D), F32), jax.ShapeDtypeStruct((T, D), BF16),
                   jax.ShapeDtypeStruct((1, D), F32)],
        scratch_shapes=[pltpu.VMEM((ROW_TILE, D), F32)],
        compiler_params=_params("arbitrary", "arbitrary"))(dp, wg, h, g, dres)


def _dw_in(n, dp, nblk, layer, nlayers, prev, name):
    T, D = n.shape
    nb = dp.shape[1] // nblk
    ta = 512

    def body(n_ref, dp_ref, *rest):
        rest[-1][0] = _tn(n_ref[...], dp_ref[...]).astype(BF16)

    in_specs = [pl.BlockSpec((T, ta), lambda j, i: (0, i)), pl.BlockSpec((T, nb), lambda j, i: (0, j))]
    args = (n, dp) if prev is None else (n, dp, prev)
    return _pallas(
        body, name=name, grid=(nblk, D // ta), in_specs=in_specs + ([] if prev is None else [ANY]),
        out_specs=pl.BlockSpec((None, 1, ta, nb), lambda j, i: (layer, j, i, 0)),
        out_shape=jax.ShapeDtypeStruct((nlayers, nblk, D, nb), BF16),
        input_output_aliases={} if prev is None else {2: 0},
        compiler_params=_params("parallel", "parallel"))(*args)


def _dw_out(y, dout, layer, nlayers, prev, name):
    T, K = y.shape
    D = dout.shape[1]
    tk = 512

    def body(y_ref, d_ref, *rest):
        rest[-1][...] = _tn(y_ref[...], d_ref[...]).astype(BF16)

    in_specs = [pl.BlockSpec((T, tk), lambda i: (0, i)), pl.BlockSpec((T, D), lambda i: (0, 0))]
    args = (y, dout) if prev is None else (y, dout, prev)
    return _pallas(
        body, name=name, grid=(K // tk,), in_specs=in_specs + ([] if prev is None else [ANY]),
        out_specs=pl.BlockSpec((None, tk, D), lambda i: (layer, i, 0)),
        out_shape=jax.ShapeDtypeStruct((nlayers, K, D), BF16),
        input_output_aliases={} if prev is None else {2: 0},
        compiler_params=_params("parallel"))(*args)


def _loss_head(h, g, tgt):
    T, D = h.shape
    tm = MIX_TILE

    def body(h_ref, g_ref, t_ref, dh_ref, dhb_ref, dg_ref, loss_ref):
        i = pl.program_id(0)
        x = h_ref[...]
        gg = g_ref[...]
        r = lax.rsqrt(jnp.mean(x * x, axis=-1, keepdims=True) + EPS_RMS)
        xr = x * r
        e = xr * gg - t_ref[...]
        lp = 0.5 * jnp.sum(jnp.mean(e * e, axis=-1, keepdims=True), axis=0, keepdims=True)
        dn = e * (1.0 / D)
        q = dn * gg
        dh = r * q - x * ((r * r * r) * jnp.mean(q * x, axis=-1, keepdims=True))
        dh_ref[...] = dh
        dhb_ref[...] = dh.astype(BF16)
        dgp = jnp.sum(dn * xr, axis=0, keepdims=True)

        @pl.when(i == 0)
        def _():
            dg_ref[...] = dgp
            loss_ref[...] = lp

        @pl.when(i > 0)
        def _():
            dg_ref[...] += dgp
            loss_ref[...] += lp

    return _pallas(
        body, name="loss_head", grid=(T // tm,),
        in_specs=[pl.BlockSpec((tm, D), lambda i: (i, 0)), pl.BlockSpec((1, D), lambda i: (0, 0)),
                  pl.BlockSpec((tm, D), lambda i: (i, 0))],
        out_specs=[pl.BlockSpec((tm, D), lambda i: (i, 0)), pl.BlockSpec((tm, D), lambda i: (i, 0)),
                   pl.BlockSpec((1, D), lambda i: (0, 0)), pl.BlockSpec((1, 1), lambda i: (0, 0))],
        out_shape=[jax.ShapeDtypeStruct((T, D), F32), jax.ShapeDtypeStruct((T, D), BF16),
                   jax.ShapeDtypeStruct((1, D), F32), jax.ShapeDtypeStruct((1, 1), F32)],
        compiler_params=_params("arbitrary"))(h, g, tgt)


def _shift_up(x, j):
    return x if j == 0 else pltpu.roll(x, x.shape[0] - j, 0)


def _shift_down(x, j):
    return x if j == 0 else pltpu.roll(x, j, 0)


def _fill_shifted(dst_ref, x):
    rows = dst_ref.shape[1]
    for s in range(8):
        dst_ref[s] = _shift_up(x, s)[0:rows]


def _tap_sum(sh_ref, w_ref, r0, nrows, offsets):
    acc = None
    for k, o in enumerate(offsets):
        win = sh_ref[o % 8, pl.ds(r0 + (o // 8) * 8, nrows), :]
        term = w_ref[k:k + 1, :] * win
        acc = term if acc is None else acc + term
    return acc


def _pool_sums(vx, up):
    sh = _shift_up if up else _shift_down
    outs = []
    for gi, w in enumerate(POOL_WINDOWS):
        s = vx[:, gi * POOL_GW:(gi + 1) * POOL_GW]
        j = 1
        while j < w:
            s = s + sh(s, j)
            j *= 2
        outs.append(s)
    return outs


def _inv_count(row0, nrows):
    pos = (row0 + 1 + lax.broadcasted_iota(jnp.int32, (nrows, 1), 0)).astype(F32)
    return [1.0 / jnp.minimum(pos, float(w)) for w in POOL_WINDOWS]


def _even_mixer_fwd(p, cw, cb, lg, lb, pw, pb, sc, name):
    T = p.shape[0]
    C = D_MODEL
    tT, HL = MIX_TILE, EVEN_HALO
    hb = tT // HL
    chunk = 16

    def body(pm_ref, ph_ref, cw_ref, cb_ref, lg_ref, lb_ref, pw_ref, pb_ref, sc_ref, y_ref, u1_ref, u0x_ref, sh_ref):
        i = pl.program_id(0)
        keep = (i > 0).astype(F32)
        u0x_ref[0:HL] = ph_ref[:, 0:C] * _sigmoid(ph_ref[:, C:2 * C]) * keep
        u0x_ref[HL:HL + tT] = pm_ref[:, 0:C] * _sigmoid(pm_ref[:, C:2 * C])
        u0x_ref[HL + tT:HL + tT + 8] = jnp.zeros((8, C), F32)
        _fill_shifted(sh_ref, u0x_ref[...])
        offs = [HL - (CONV_K - 1) + k for k in range(CONV_K)]

        def conv_chunk(c, carry):
            r0 = pl.multiple_of(c * chunk, chunk)
            u1_ref[pl.ds(r0, chunk), :] = _tap_sum(sh_ref, cw_ref, r0, chunk, offs) + cb_ref[...]
            return carry

        lax.fori_loop(0, tT // chunk, conv_chunk, 0)
        u1 = u1_ref[...]
        mu = jnp.mean(u1, axis=-1, keepdims=True)
        xc = u1 - mu
        rs = lax.rsqrt(jnp.mean(xc * xc, axis=-1, keepdims=True) + EPS_LN)
        u2 = xc * rs * lg_ref[...] + lb_ref[...]
        u3 = u2 * _sigmoid(u2)
        ag = pm_ref[:, 2 * C:3 * C]
        y_ref[:, 0:C] = (u3 * (ag * _sigmoid(ag))).astype(BF16)
        vx = jnp.concatenate([ph_ref[:, 3 * C:4 * C] * keep, pm_ref[:, 3 * C:4 * C]], axis=0)
        sums = _pool_sums(vx, up=False)
        inv = _inv_count(i * tT, tT)
        for gi in range(len(POOL_WINDOWS)):
            cols = slice(gi * POOL_GW, (gi + 1) * POOL_GW)
            d0 = sums[gi][HL:] * inv[gi] - vx[HL:, cols]
            d1 = jnp.dot(d0.astype(BF16), pw_ref[gi], preferred_element_type=F32) + pb_ref[:, cols]
            bg = pm_ref[:, 4 * C + gi * POOL_GW:4 * C + (gi + 1) * POOL_GW]
            y_ref[:, C + gi * POOL_GW:C + (gi + 1) * POOL_GW] = (d1 * sc_ref[:, cols] * (bg * _sigmoid(bg))).astype(BF16)

    vec = pl.BlockSpec((1, C), lambda i: (0, 0))
    return _pallas(
        body, name=name, grid=(T // tT,),
        in_specs=[pl.BlockSpec((tT, 5 * C), lambda i: (i, 0)),
                  pl.BlockSpec((HL, 5 * C), lambda i: (jnp.maximum(i * hb - 1, 0), 0)),
                  pl.BlockSpec((32, C), lambda i: (0, 0)), vec, vec, vec,
                  pl.BlockSpec((4, POOL_GW, POOL_GW), lambda i: (0, 0, 0)), vec, vec],
        out_specs=[pl.BlockSpec((tT, 2 * C), lambda i: (i, 0)), pl.BlockSpec((tT, C), lambda i: (i, 0))],
        out_shape=[jax.ShapeDtypeStruct((T, 2 * C), BF16), jax.ShapeDtypeStruct((T, C), F32)],
        scratch_shapes=[pltpu.VMEM((HL + tT + 8, C), F32), pltpu.VMEM((8, HL + tT, C), F32)],
        compiler_params=_params("parallel"))(p, p, cw, cb, lg, lb, pw, pb, sc)


def _even_mixer_bwd(p, u1, dy, cw, cwr, lg, lb, pw, pb, sc, name):
    T = p.shape[0]
    C = D_MODEL
    tT, HL = MIX_TILE, EVEN_HALO
    hb = tT // HL
    nT = T // tT
    R1 = tT + HL
    chunk = 16

    def body(pm_ref, pp_ref, pn_ref, u1m_ref, u1n_ref, dym_ref, dyn_ref, cw_ref, cwr_ref, lg_ref, lb_ref, pw_ref,
             pb_ref, sc_ref, dp_ref, dcw_ref, dvec_ref, dpw_ref, x_ref, sh_ref, du0_ref):
        i = pl.program_id(0)
        keep_prev = (i > 0).astype(F32)
        keep_next = (i < nT - 1).astype(F32)
        row = lax.broadcasted_iota(jnp.int32, (R1, 1), 0)
        live = jnp.where(row < tT, 1.0, keep_next)

        def cat(m, n):
            return jnp.concatenate([m, n], axis=0)

        u1 = cat(u1m_ref[...], u1n_ref[...])
        mu = jnp.mean(u1, axis=-1, keepdims=True)
        xc = u1 - mu
        rs = lax.rsqrt(jnp.mean(xc * xc, axis=-1, keepdims=True) + EPS_LN)
        xh = xc * rs
        u2 = xh * lg_ref[...] + lb_ref[...]
        s2 = _sigmoid(u2)
        u3 = u2 * s2
        ag = cat(pm_ref[:, 2 * C:3 * C], pn_ref[:, 2 * C:3 * C])
        sa = _sigmoid(ag)
        dya = cat(dym_ref[:, 0:C], dyn_ref[:, 0:C])
        dp_ref[:, 2 * C:3 * C] = (dya * u3 * _dsilu(ag, sa))[0:tT].astype(BF16)
        du2 = dya * (ag * sa) * _dsilu(u2, s2)
        dlg = jnp.sum((du2 * xh)[0:tT], axis=0, keepdims=True)
        dlb = jnp.sum(du2[0:tT], axis=0, keepdims=True)
        dxh = du2 * lg_ref[...]
        du1 = rs * (dxh - jnp.mean(dxh, axis=-1, keepdims=True) - xh * jnp.mean(dxh * xh, axis=-1, keepdims=True))
        du1 = du1 * live
        dcb = jnp.sum(du1[0:tT], axis=0, keepdims=True)
        x_ref[0:R1] = du1
        x_ref[R1:R1 + 8] = jnp.zeros((8, C), F32)
        _fill_shifted(sh_ref, x_ref[...])

        def du0_chunk(c, carry):
            r0 = pl.multiple_of(c * chunk, chunk)
            du0_ref[pl.ds(r0, chunk), :] = _tap_sum(sh_ref, cwr_ref, r0, chunk, list(range(CONV_K)))
            return carry

        lax.fori_loop(0, tT // chunk, du0_chunk, 0)
        av, agl = pm_ref[:, 0:C], pm_ref[:, C:2 * C]
        sg = _sigmoid(agl)
        du0 = du0_ref[...]
        dp_ref[:, 0:C] = (du0 * sg).astype(BF16)
        dp_ref[:, C:2 * C] = (du0 * av * sg * (1.0 - sg)).astype(BF16)
        du0_ref[...] = du1[0:tT]
        x_ref[0:HL] = pp_ref[:, 0:C] * _sigmoid(pp_ref[:, C:2 * C]) * keep_prev
        x_ref[HL:HL + tT] = av * sg
        x_ref[HL + tT:HL + tT + 8] = jnp.zeros((8, C), F32)
        _fill_shifted(sh_ref, x_ref[...])

        @pl.when(i == 0)
        def _():
            dcw_ref[...] = jnp.zeros_like(dcw_ref)

        for k in range(CONV_K):
            o = HL - (CONV_K - 1) + k

            def dw_chunk(c, acc, o=o):
                r0 = pl.multiple_of(c * 64, 64)
                for u in range(0, 64, 8):
                    acc = acc + du0_ref[pl.ds(r0 + u, 8), :] * sh_ref[o % 8, pl.ds(r0 + u + (o // 8) * 8, 8), :]
                return acc

            dcw_ref[8 * k:8 * k + 8, :] += lax.fori_loop(0, tT // 64, dw_chunk, jnp.zeros((8, C), F32))

        bg = cat(pm_ref[:, 4 * C:5 * C], pn_ref[:, 4 * C:5 * C])
        sb = _sigmoid(bg)
        dyb = cat(dym_ref[:, C:2 * C], dyn_ref[:, C:2 * C])
        dyb0 = dyb * (bg * sb)
        dd1 = dyb0 * sc_ref[...]
        dpb = jnp.sum(dd1[0:tT], axis=0, keepdims=True)
        inv1 = _inv_count(i * tT, R1)
        z_parts, dd0_parts = [], []
        for gi in range(len(POOL_WINDOWS)):
            cols = slice(gi * POOL_GW, (gi + 1) * POOL_GW)
            dd0 = _nt(dd1[:, cols].astype(BF16), pw_ref[gi])
            dd0_parts.append(dd0)
            z_parts.append(dd0 * inv1[gi] * live)
        fsum = _pool_sums(jnp.concatenate(z_parts, axis=1), up=True)
        vx = cat(pp_ref[:, 3 * C:4 * C] * keep_prev, pm_ref[:, 3 * C:4 * C])
        sums = _pool_sums(vx, up=False)
        inv0 = _inv_count(i * tT, tT)
        dsc_parts = []
        for gi in range(len(POOL_WINDOWS)):
            cols = slice(gi * POOL_GW, (gi + 1) * POOL_GW)
            dp_ref[:, 3 * C + gi * POOL_GW:3 * C + (gi + 1) * POOL_GW] = (fsum[gi][0:tT] - dd0_parts[gi][0:tT]).astype(BF16)
            d0 = (sums[gi][HL:] * inv0[gi] - vx[HL:, cols]).astype(BF16)
            d1 = jnp.dot(d0, pw_ref[gi], preferred_element_type=F32) + pb_ref[:, cols]
            bgm, sbm = bg[0:tT, cols], sb[0:tT, cols]
            dp_ref[:, 4 * C + gi * POOL_GW:4 * C + (gi + 1) * POOL_GW] = (
                dyb[0:tT, cols] * d1 * sc_ref[:, cols] * _dsilu(bgm, sbm)).astype(BF16)
            dsc_parts.append(jnp.sum(dyb0[0:tT, cols] * d1, axis=0, keepdims=True))
            dpw_g = _tn(d0, dd1[0:tT, cols].astype(BF16))

            @pl.when(i == 0)
            def _(gi=gi, dpw_g=dpw_g):
                dpw_ref[gi] = dpw_g

            @pl.when(i > 0)
            def _(gi=gi, dpw_g=dpw_g):
                dpw_ref[gi] += dpw_g

        dsc = jnp.concatenate(dsc_parts, axis=1)
        vecs = jnp.concatenate([dcb, dlg, dlb, dsc, dpb, jnp.zeros((3, C), F32)], axis=0)

        @pl.when(i == 0)
        def _():
            dvec_ref[...] = vecs

        @pl.when(i > 0)
        def _():
            dvec_ref[...] += vecs

    vec = pl.BlockSpec((1, C), lambda i: (0, 0))
    taps = pl.BlockSpec((32, C), lambda i: (0, 0))

    def prev_blk(i):
        return (jnp.maximum(i * hb - 1, 0), 0)

    def next_blk(i):
        return (jnp.minimum((i + 1) * hb, T // HL - 1), 0)

    return _pallas(
        body, name=name, grid=(nT,),
        in_specs=[pl.BlockSpec((tT, 5 * C), lambda i: (i, 0)), pl.BlockSpec((HL, 5 * C), prev_blk),
                  pl.BlockSpec((HL, 5 * C), next_blk),
                  pl.BlockSpec((tT, C), lambda i: (i, 0)), pl.BlockSpec((HL, C), next_blk),
                  pl.BlockSpec((tT, 2 * C), lambda i: (i, 0)), pl.BlockSpec((HL, 2 * C), next_blk),
                  taps, taps, vec, vec, pl.BlockSpec((4, POOL_GW, POOL_GW), lambda i: (0, 0, 0)), vec, vec],
        out_specs=[pl.BlockSpec((tT, 5 * C), lambda i: (i, 0)), pl.BlockSpec((32 * 8, C), lambda i: (0, 0)),
                   pl.BlockSpec((8, C), lambda i: (0, 0)), pl.BlockSpec((4, POOL_GW, POOL_GW), lambda i: (0, 0, 0))],
        out_shape=[jax.ShapeDtypeStruct((T, 5 * C), BF16), jax.ShapeDtypeStruct((32 * 8, C), F32),
                   jax.ShapeDtypeStruct((8, C), F32), jax.ShapeDtypeStruct((4, POOL_GW, POOL_GW), F32)],
        scratch_shapes=[pltpu.VMEM((R1 + 8, C), F32), pltpu.VMEM((8, R1, C), F32), pltpu.VMEM((tT, C), F32)],
        compiler_params=_params("arbitrary"))(p, p, p, u1, u1, dy, dy, cw, cwr, lg, lb, pw, pb, sc)


def _softplus(z):
    u = jnp.exp(-jnp.abs(z))
    w = 1.0 + u
    l1p = jnp.where(w == 1.0, u, u * jnp.log(w) / jnp.where(w == 1.0, 1.0, w - 1.0))
    return jnp.maximum(z, 0.0) + l1p


def _lru_gates(xrx, cw_ref, cb_ref, wr_ref, br_ref, wi_ref, bi_ref, lam_ref):
    HL = ODD_HALO
    xc = cb_ref[...] + cw_ref[LRU_CONV_K - 1:LRU_CONV_K, :] * xrx[HL:]
    for k in range(LRU_CONV_K - 1):
        xc = xc + cw_ref[k:k + 1, :] * _shift_down(xrx, LRU_CONV_K - 1 - k)[HL:]
    xcb = xc.astype(BF16)
    rp, ip = [], []
    for hd in range(LRU_HEADS):
        cols = slice(hd * LRU_HD, (hd + 1) * LRU_HD)
        rp.append(jnp.dot(xcb[:, cols], wr_ref[hd], preferred_element_type=F32))
        ip.append(jnp.dot(xcb[:, cols], wi_ref[hd], preferred_element_type=F32))
    r = _sigmoid(jnp.concatenate(rp, axis=1) + br_ref[...])
    ig = _sigmoid(jnp.concatenate(ip, axis=1) + bi_ref[...])
    sp = _softplus(-lam_ref[...])
    log_a = (-LRU_C) * r * sp
    a = jnp.exp(log_a)
    mult = jnp.sqrt(-jnp.tanh(log_a) * (a * a + 1.0))
    return xc, xcb, r, ig, sp, a, mult, 1.0 / mult


def _odd_mixer_fwd(p, cw, cb, wr, br, wi, bi, lam, name):
    T = p.shape[0]
    W = W_LRU
    tT, HL = MIX_TILE, ODD_HALO
    hb = tT // HL

    def body(pm_ref, ph_ref, cw_ref, cb_ref, wr_ref, br_ref, wi_ref, bi_ref, lam_ref, y_ref, hs_ref, carry_ref):
        i = pl.program_id(0)
        keep = (i > 0).astype(F32)

        @pl.when(i == 0)
        def _():
            carry_ref[...] = jnp.zeros_like(carry_ref)

        xrx = jnp.concatenate([ph_ref[:, 0:W] * keep, pm_ref[:, 0:W]], axis=0)
        xc, _, _, ig, _, a, mult, _ = _lru_gates(xrx, cw_ref, cb_ref, wr_ref, br_ref, wi_ref, bi_ref, lam_ref)
        b = mult * (ig * xc)
        row = lax.broadcasted_iota(jnp.int32, (tT, 1), 0)
        s = 1
        while s < tT:
            ok = row >= s
            a_sh = jnp.where(ok, _shift_down(a, s), 1.0)
            b_sh = jnp.where(ok, _shift_down(b, s), 0.0)
            b = a * b_sh + b
            a = a * a_sh
            s *= 2
        hs = a * carry_ref[0:1, :] + b
        hs_ref[...] = hs
        carry_ref[...] = jnp.broadcast_to(hs[tT - 1:tT, :], (8, W))
        gt = pm_ref[:, W:2 * W]
        y_ref[...] = (hs * (gt * _sigmoid(gt))).astype(BF16)

    vec = pl.BlockSpec((1, W), lambda i: (0, 0))
    heads = pl.BlockSpec((LRU_HEADS, LRU_HD, LRU_HD), lambda i: (0, 0, 0))
    return _pallas(
        body, name=name, grid=(T // tT,),
        in_specs=[pl.BlockSpec((tT, 2 * W), lambda i: (i, 0)),
                  pl.BlockSpec((HL, 2 * W), lambda i: (jnp.maximum(i * hb - 1, 0), 0)),
                  pl.BlockSpec((8, W), lambda i: (0, 0)), vec, heads, vec, heads, vec, vec],
        out_specs=[pl.BlockSpec((tT, W), lambda i: (i, 0)), pl.BlockSpec((tT, W), lambda i: (i, 0))],
        out_shape=[jax.ShapeDtypeStruct((T, W), BF16), jax.ShapeDtypeStruct((T, W), F32)],
        scratch_shapes=[pltpu.VMEM((8, W), F32)],
        compiler_params=_params("arbitrary"))(p, p, cw, cb, wr, br, wi, bi, lam)


def _odd_mixer_bwd(p, hs, dy, cw, cb, wr, br, wi, bi, lam, name):
    T = p.shape[0]
    W = W_LRU
    tT, HL = MIX_TILE, ODD_HALO
    hb = tT // HL
    nT = T // tT

    def body(pm_ref, ph_ref, hsm_ref, hsh_ref, dy_ref, cw_ref, cb_ref, wr_ref, br_ref, wi_ref, bi_ref, lam_ref,
             dp_ref, dwr_ref, dwi_ref, dvec_ref, gcarry_ref, xcarry_ref):
        i = pl.program_id(0)
        keep = (i < nT - 1).astype(F32)

        @pl.when(i == 0)
        def _():
            gcarry_ref[...] = jnp.zeros_like(gcarry_ref)
            xcarry_ref[...] = jnp.zeros_like(xcarry_ref)

        xrx = jnp.concatenate([ph_ref[:, 0:W] * keep, pm_ref[:, 0:W]], axis=0)
        xc, xcb, r, ig, sp, a, mult, inv_mult = _lru_gates(xrx, cw_ref, cb_ref, wr_ref, br_ref, wi_ref, bi_ref, lam_ref)
        hs = hsm_ref[...]
        gt = pm_ref[:, W:2 * W]
        sg = _sigmoid(gt)
        dyv = dy_ref[...]
        dp_ref[:, W:2 * W] = (dyv * hs * _dsilu(gt, sg)).astype(BF16)
        row = lax.broadcasted_iota(jnp.int32, (tT, 1), 0)
        e = dyv * (gt * sg) + jnp.where(row == tT - 1, gcarry_ref[0:1, :], 0.0)
        m = jnp.where(row == tT - 1, 1.0, _shift_up(a, 1))
        s = 1
        while s < tT:
            ok = row < tT - s
            m_sh = jnp.where(ok, _shift_up(m, s), 1.0)
            e_sh = jnp.where(ok, _shift_up(e, s), 0.0)
            e = m * e_sh + e
            m = m * m_sh
            s *= 2
        G = e
        gcarry_ref[...] = jnp.broadcast_to(a[0:1, :] * G[0:1, :], (8, W))
        hs_prev = jnp.where(row == 0, hsh_ref[HL - 1:HL, :] * keep, _shift_down(hs, 1))
        da = G * hs_prev
        dmult = G * (ig * xc)
        di = G * mult * xc
        dxc = G * mult * ig
        dlog_a = da * a - dmult * (a * a) * inv_mult
        drp = dlog_a * ((-LRU_C) * sp) * r * (1.0 - r)
        dip = di * ig * (1.0 - ig)
        dlam = jnp.sum(dlog_a * ((-LRU_C) * r), axis=0, keepdims=True) * (-_sigmoid(-lam_ref[...]))
        drb, dib = drp.astype(BF16), dip.astype(BF16)
        back = []
        for hd in range(LRU_HEADS):
            cols = slice(hd * LRU_HD, (hd + 1) * LRU_HD)
            back.append(_nt(drb[:, cols], wr_ref[hd]) + _nt(dib[:, cols], wi_ref[hd]))
            dwr_h = _tn(xcb[:, cols], drb[:, cols])
            dwi_h = _tn(xcb[:, cols], dib[:, cols])

            @pl.when(i == 0)
            def _(hd=hd, dwr_h=dwr_h, dwi_h=dwi_h):
                dwr_ref[hd] = dwr_h
                dwi_ref[hd] = dwi_h

            @pl.when(i > 0)
            def _(hd=hd, dwr_h=dwr_h, dwi_h=dwi_h):
                dwr_ref[hd] += dwr_h
                dwi_ref[hd] += dwi_h

        dxc = dxc + jnp.concatenate(back, axis=1)
        dxcx = jnp.concatenate([dxc, xcarry_ref[...]], axis=0)
        dxr = cw_ref[LRU_CONV_K - 1:LRU_CONV_K, :] * dxc
        rows = []
        for k in range(LRU_CONV_K - 1):
            j = LRU_CONV_K - 1 - k
            dxr = dxr + cw_ref[k:k + 1, :] * _shift_up(dxcx, j)[0:tT]
            rows.append(jnp.sum(dxc * _shift_down(xrx, j)[HL:], axis=0, keepdims=True))
        rows.append(jnp.sum(dxc * xrx[HL:], axis=0, keepdims=True))
        dp_ref[:, 0:W] = dxr.astype(BF16)
        xcarry_ref[...] = dxc[0:8]
        rows += [jnp.sum(dxc, axis=0, keepdims=True), jnp.sum(drp, axis=0, keepdims=True),
                 jnp.sum(dip, axis=0, keepdims=True), dlam]
        vecs = jnp.concatenate(rows, axis=0)

        @pl.when(i == 0)
        def _():
            dvec_ref[...] = vecs

        @pl.when(i > 0)
        def _():
            dvec_ref[...] += vecs

    vec = pl.BlockSpec((1, W), lambda i: (0, 0))
    heads = pl.BlockSpec((LRU_HEADS, LRU_HD, LRU_HD), lambda i: (0, 0, 0))

    def tile(i):
        return (nT - 1 - i, 0)

    def prev_blk(i):
        return (jnp.maximum((nT - 1 - i) * hb - 1, 0), 0)

    return _pallas(
        body, name=name, grid=(nT,),
        in_specs=[pl.BlockSpec((tT, 2 * W), tile), pl.BlockSpec((HL, 2 * W), prev_blk),
                  pl.BlockSpec((tT, W), tile), pl.BlockSpec((HL, W), prev_blk), pl.BlockSpec((tT, W), tile),
                  pl.BlockSpec((8, W), lambda i: (0, 0)), vec, heads, vec, heads, vec, vec],
        out_specs=[pl.BlockSpec((tT, 2 * W), tile), heads, heads, pl.BlockSpec((8, W), lambda i: (0, 0))],
        out_shape=[jax.ShapeDtypeStruct((T, 2 * W), BF16), jax.ShapeDtypeStruct((LRU_HEADS, LRU_HD, LRU_HD), F32),
                   jax.ShapeDtypeStruct((LRU_HEADS, LRU_HD, LRU_HD), F32), jax.ShapeDtypeStruct((8, W), F32)],
        scratch_shapes=[pltpu.VMEM((8, W), F32), pltpu.VMEM((8, W), F32)],
        compiler_params=_params("arbitrary"))(p, p, hs, hs, dy, cw, cb, wr, br, wi, bi, lam)


def _pad_rows(a, rows):
    return jnp.concatenate([a, jnp.zeros((rows - a.shape[0], a.shape[1]), a.dtype)], axis=0)


def _local_step(x, tgt, wt):
    even, odd = wt["even"], wt["odd"]
    depth = len(even) + len(odd)
    h = x
    saved = []
    for layer in range(depth):
        j = layer // 2
        if layer % 2 == 0:
            w = even[j]
            p, n = _in_proj(h, w["norm"], wt["w_in_even"], j, "in_proj_even")
            y, aux = _even_mixer_fwd(p, w["conv_w"], w["conv_b"], w["ln_g"], w["ln_b"], w["pool_w"], w["pool_b"],
                                     w["pool_scale"], "even_mixer_fwd")
            h_next = _out_proj(y, wt["w_out_even"], j, h, "out_proj_even")
        else:
            w = odd[j]
            p, n = _in_proj(h, w["norm"], wt["w_in_odd"], j, "in_proj_odd")
            y, aux = _odd_mixer_fwd(p, w["conv_w"], w["conv_b"], w["w_rg"], w["b_rg"], w["w_ig"], w["b_ig"], w["lam"],
                                    "odd_mixer_fwd")
            h_next = _out_proj(y, wt["w_out_odd"], j, h, "out_proj_odd")
        saved.append((h, n, p, aux, y))
        h = h_next
    dh, dhb, d_final, loss = _loss_head(h, wt["final_norm"], tgt)
    g_even = [None] * len(even)
    g_odd = [None] * len(odd)
    big = dict(w_in_even=None, w_out_even=None, w_in_odd=None, w_out_odd=None)
    for layer in reversed(range(depth)):
        h, n, p, aux, y = saved[layer]
        j = layer // 2
        if layer % 2 == 0:
            w = even[j]
            big["w_out_even"] = _dw_out(y, dhb, j, len(even), big["w_out_even"], "dw_out_even")
            dy = _dy_proj(dhb, wt["w_out_even"], j, "dy_proj_even")
            dp, dcw, dvec, dpw = _even_mixer_bwd(p, aux, dy, w["conv_w"], w["conv_w_rev"], w["ln_g"], w["ln_b"],
                                                 w["pool_w"], w["pool_b"], w["pool_scale"], "even_mixer_bwd")
            big["w_in_even"] = _dw_in(n, dp, N_CHIPS, j, len(even), big["w_in_even"], "dw_in_even")
            dh, dhb, dnorm = _dn_proj(dp, wt["w_in_even"], j, h, w["norm"], dh, "dn_proj_even")
            g_even[j] = dict(conv_w=dcw, vec=dvec, pool_w=dpw, norm=dnorm)
        else:
            w = odd[j]
            big["w_out_odd"] = _dw_out(y, dhb, j, len(odd), big["w_out_odd"], "dw_out_odd")
            dy = _dy_proj(dhb, wt["w_out_odd"], j, "dy_proj_odd")
            dp, dwr, dwi, dvec = _odd_mixer_bwd(p, aux, dy, w["conv_w"], w["conv_b"], w["w_rg"], w["b_rg"], w["w_ig"],
                                                w["b_ig"], w["lam"], "odd_mixer_bwd")
            big["w_in_odd"] = _dw_in(n, dp, N_CHIPS, j, len(odd), big["w_in_odd"], "dw_in_odd")
            dh, dhb, dnorm = _dn_proj(dp, wt["w_in_odd"], j, h, w["norm"], dh, "dn_proj_odd")
            g_odd[j] = dict(w_rg=dwr, w_ig=dwi, vec=dvec, norm=dnorm)
    return loss, dh, big, g_even, g_odd, d_final


ANY = pl.BlockSpec(memory_space=pl.ANY)


def _mesh_pos():
    return lax.axis_index("x"), lax.axis_index("y"), lax.axis_index("c")


def _other_chips(x, y):
    return [(1 - x, y), (x, 1 - y), (1 - x, 1 - y)]


def _other_devices(x, y, c):
    out = []
    for p in range(1, N_DEV):
        out.append((1 - x if p & 4 else x, 1 - y if p & 2 else y, 1 - c if p & 1 else c))
    return out


def _remote(src, dst, ssem, rsem, dev):
    return pltpu.make_async_remote_copy(src_ref=src, dst_ref=dst, send_sem=ssem, recv_sem=rsem, device_id=dev,
                                        device_id_type=MESH)


def _comm_call(body, name, ins, out_shape, scratch, aliases=None):
    return _pallas(body, name=name, in_specs=[ANY] * len(ins), out_specs=[ANY] * len(out_shape), out_shape=out_shape,
                   scratch_shapes=scratch, input_output_aliases=aliases or {},
                   compiler_params=pltpu.CompilerParams(has_side_effects=True))(*ins)


def _cast_shard(w, k):
    L, R, C = w.shape
    tr = _row_tile(R, C)

    def body(k_ref, w_ref, o_ref):
        o_ref[...] = w_ref[...].astype(BF16)

    grid_spec = pltpu.PrefetchScalarGridSpec(
        num_scalar_prefetch=1, grid=(L, R // tr),
        in_specs=[pl.BlockSpec((None, tr, C), lambda l, i, kr: (l, i, 0))],
        out_specs=pl.BlockSpec((None, None, tr, C), lambda l, i, kr: (l, kr[0], i, 0)))
    return _pallas(body, name="cast_shard", grid_spec=grid_spec,
                   out_shape=jax.ShapeDtypeStruct((L, N_CHIPS, R, C), BF16),
                   compiler_params=_params("parallel", "parallel"))(k, w)


def _gather_weights(big, small):
    nA = len(big)
    half = [a.shape[2] // 2 for a in big]

    def body(*refs):
        ins, outs = refs[:nA + 1], refs[nA + 1:2 * nA + 2]
        ssem, rsem, fsem, frsem, lsem = refs[2 * nA + 2:]
        x, y, c = _mesh_pos()
        k = 2 * x + y
        chips = _other_chips(x, y)
        sib = (x, y, 1 - c)

        def slab(a, chip, core):
            return outs[a].at[:, chip, pl.ds(core * half[a], half[a]), :]

        local = [pltpu.make_async_copy(ins[nA], outs[nA].at[k], lsem.at[0])]
        for cp in local:
            cp.start()
        sends = []
        for j, (ox, oy) in enumerate(chips):
            for a in range(nA):
                sends.append(_remote(slab(a, k, c), slab(a, k, c), ssem.at[a, j], rsem.at[a, j], (ox, oy, c)))
            sends.append(_remote(ins[nA], outs[nA].at[k], ssem.at[nA, j], rsem.at[nA, j], (ox, oy, c)))
        for cp in sends:
            cp.start()
        for j, (ox, oy) in enumerate(chips):
            kj = 2 * ox + oy
            for a in range(nA):
                got = slab(a, kj, c)
                _remote(got, got, ssem.at[a, j], rsem.at[a, j], (ox, oy, c)).wait_recv()
                fw = _remote(got, got, fsem.at[a, j], frsem.at[a, j], sib)
                fw.start()
                sends.append(fw)
            gs = outs[nA].at[kj]
            _remote(gs, gs, ssem.at[nA, j], rsem.at[nA, j], (ox, oy, c)).wait_recv()
        for j, (ox, oy) in enumerate(chips):
            kj = 2 * ox + oy
            for a in range(nA):
                theirs = slab(a, kj, 1 - c)
                _remote(theirs, theirs, fsem.at[a, j], frsem.at[a, j], sib).wait_recv()
        for cp in sends:
            cp.wait_send()
        for cp in local:
            cp.wait()

    out_shape = [jax.ShapeDtypeStruct(a.shape, a.dtype) for a in big]
    out_shape.append(jax.ShapeDtypeStruct((N_CHIPS,) + small.shape, small.dtype))
    scratch = [pltpu.SemaphoreType.DMA((nA + 1, 3)), pltpu.SemaphoreType.DMA((nA + 1, 3)),
               pltpu.SemaphoreType.DMA((nA, 3)), pltpu.SemaphoreType.DMA((nA, 3)), pltpu.SemaphoreType.DMA((1,))]
    return _comm_call(body, "gather_weights", list(big) + [small], out_shape, scratch, {a: a for a in range(nA)})


def _exchange_cores(big, small):
    nA = len(big)
    half = [a.shape[2] // 2 for a in big]

    def body(*refs):
        ins, outs = refs[:nA + 1], refs[nA + 1:2 * nA + 2]
        ssem, rsem, ssem2, rsem2, lsem = refs[2 * nA + 2:]
        x, y, c = _mesh_pos()
        me = 4 * x + 2 * y + c
        sib = (x, y, 1 - c)
        peers = _other_devices(x, y, c)
        local = pltpu.make_async_copy(ins[nA], outs[nA].at[me], lsem.at[0])
        local.start()
        sends = []
        for a in range(nA):
            src = ins[a].at[:, :, pl.ds((1 - c) * half[a], half[a]), :]
            sends.append(_remote(src, outs[a], ssem.at[a], rsem.at[a], sib))
        for p, dev in enumerate(peers):
            sends.append(_remote(ins[nA], outs[nA].at[me], ssem2.at[p], rsem2.at[p], dev))
        for cp in sends:
            cp.start()
        for a in range(nA):
            _remote(outs[a], outs[a], ssem.at[a], rsem.at[a], sib).wait_recv()
        for p, (px, py, pc) in enumerate(peers):
            got = outs[nA].at[4 * px + 2 * py + pc]
            _remote(got, got, ssem2.at[p], rsem2.at[p], (px, py, pc)).wait_recv()
        for cp in sends:
            cp.wait_send()
        local.wait()

    out_shape = [jax.ShapeDtypeStruct((a.shape[0], N_CHIPS, h, a.shape[3]), a.dtype) for a, h in zip(big, half)]
    out_shape.append(jax.ShapeDtypeStruct((N_DEV,) + small.shape, small.dtype))
    scratch = [pltpu.SemaphoreType.DMA((nA,)), pltpu.SemaphoreType.DMA((nA,)), pltpu.SemaphoreType.DMA((N_DEV - 1,)),
               pltpu.SemaphoreType.DMA((N_DEV - 1,)), pltpu.SemaphoreType.DMA((1,))]
    return _comm_call(body, "exchange_cores", list(big) + [small], out_shape, scratch)


def _exchange_chips(parts):
    nA = len(parts)

    def body(*refs):
        ins, outs = refs[:nA], refs[nA:2 * nA]
        ssem, rsem = refs[2 * nA:]
        x, y, c = _mesh_pos()
        k = 2 * x + y
        chips = _other_chips(x, y)
        sends = []
        for j, (ox, oy) in enumerate(chips):
            for a in range(nA):
                sends.append(_remote(ins[a].at[:, 2 * ox + oy], outs[a].at[:, k], ssem.at[a, j], rsem.at[a, j],
                                     (ox, oy, c)))
        for cp in sends:
            cp.start()
        for j, (ox, oy) in enumerate(chips):
            for a in range(nA):
                got = outs[a].at[:, 2 * ox + oy]
                _remote(got, got, ssem.at[a, j], rsem.at[a, j], (ox, oy, c)).wait_recv()
        for cp in sends:
            cp.wait_send()

    out_shape = [jax.ShapeDtypeStruct(a.shape, a.dtype) for a in parts]
    scratch = [pltpu.SemaphoreType.DMA((nA, 3)), pltpu.SemaphoreType.DMA((nA, 3))]
    return _comm_call(body, "exchange_chips", list(parts), out_shape, scratch)


def _exchange_final(grads, everywhere):
    nA = len(grads)
    n_remote = sum(N_DEV - 1 if ev else 1 for ev in everywhere)

    def body(*refs):
        outs = refs[nA:2 * nA]
        ssem, rsem = refs[2 * nA:]
        x, y, c = _mesh_pos()
        k = 2 * x + y
        sib = (x, y, 1 - c)
        peers = _other_devices(x, y, c)
        sends, waits = [], []
        s = 0
        for a in range(nA):
            if everywhere[a]:
                r2 = grads[a].shape[1] // N_DEV
                mine = outs[a].at[:, pl.ds((2 * k + c) * r2, r2), :]
                for (px, py, pc) in peers:
                    sends.append(_remote(mine, mine, ssem.at[s], rsem.at[s], (px, py, pc)))
                    got = outs[a].at[:, pl.ds((2 * (2 * px + py) + pc) * r2, r2), :]
                    waits.append(_remote(got, got, ssem.at[s], rsem.at[s], (px, py, pc)))
                    s += 1
            else:
                r2 = grads[a].shape[1] // 2
                mine = outs[a].at[:, pl.ds(c * r2, r2), :]
                sends.append(_remote(mine, mine, ssem.at[s], rsem.at[s], sib))
                got = outs[a].at[:, pl.ds((1 - c) * r2, r2), :]
                waits.append(_remote(got, got, ssem.at[s], rsem.at[s], sib))
                s += 1
        for cp in sends:
            cp.start()
        for cp in waits:
            cp.wait_recv()
        for cp in sends:
            cp.wait_send()

    out_shape = [jax.ShapeDtypeStruct(g.shape, g.dtype) for g in grads]
    scratch = [pltpu.SemaphoreType.DMA((n_remote,)), pltpu.SemaphoreType.DMA((n_remote,))]
    return _comm_call(body, "exchange_final", list(grads), out_shape, scratch, {a: a for a in range(nA)})


BLOCK_BYTES = 1 << 20


def _row_tile(rows, cols, mult=16):
    best = mult
    for t in range(mult, rows + 1, mult):
        if rows % t == 0 and t * cols * 4 <= BLOCK_BYTES:
            best = t
    return best


def _add_cores(own, recv, pos):
    L, _, R, C = own.shape
    r2 = R // 2
    tr = _row_tile(r2, C)
    nb = r2 // tr

    def body(pos_ref, a_ref, r_ref, o_ref):
        o_ref[...] = (a_ref[...].astype(F32) + r_ref[...].astype(F32)).astype(BF16)

    blk = (None, None, tr, C)
    grid_spec = pltpu.PrefetchScalarGridSpec(
        num_scalar_prefetch=1, grid=(L, N_CHIPS, nb),
        in_specs=[pl.BlockSpec(blk, lambda l, s, i, pr: (l, s, pr[1] * nb + i, 0)),
                  pl.BlockSpec(blk, lambda l, s, i, pr: (l, s, i, 0))],
        out_specs=pl.BlockSpec(blk, lambda l, s, i, pr: (l, s, i, 0)))
    return _pallas(body, name="add_cores", grid_spec=grid_spec,
                   out_shape=jax.ShapeDtypeStruct((L, N_CHIPS, r2, C), BF16),
                   compiler_params=_params("parallel", "parallel", "parallel"))(pos, own, recv)


def _sum_chips(own, recv, pos, everywhere):
    L, _, r2, C = own.shape
    tr = _row_tile(r2, 2 * C)
    nb = r2 // tr

    def body(pos_ref, a_ref, r_ref, o_ref):
        acc = None
        for s in range(N_CHIPS):
            term = jnp.where(pos_ref[0] == s, a_ref[...], r_ref[s]).astype(F32)
            acc = term if acc is None else acc + term
        o_ref[...] = acc

    if everywhere:
        def out_map(l, i, pr):
            return (l, (2 * pr[0] + pr[1]) * nb + i, 0)
    else:
        def out_map(l, i, pr):
            return (l, pr[1] * nb + i, 0)

    grid_spec = pltpu.PrefetchScalarGridSpec(
        num_scalar_prefetch=1, grid=(L, nb),
        in_specs=[pl.BlockSpec((None, None, tr, C), lambda l, i, pr: (l, pr[0], i, 0)),
                  pl.BlockSpec((None, N_CHIPS, tr, C), lambda l, i, pr: (l, 0, i, 0))],
        out_specs=pl.BlockSpec((None, tr, C), out_map))
    rows = (N_DEV if everywhere else 2) * r2
    return _pallas(body, name="sum_chips", grid_spec=grid_spec, out_shape=jax.ShapeDtypeStruct((L, rows, C), F32),
                   compiler_params=_params("parallel", "parallel"))(pos, own, recv)


def _sum_devices(parts):
    n, R, C = parts.shape
    tr = _row_tile(R, C * n, 8)

    def body(p_ref, o_ref):
        acc = p_ref[0]
        for s in range(1, n):
            acc = acc + p_ref[s]
        o_ref[...] = acc

    return _pallas(body, name="sum_devices", grid=(R // tr,), in_specs=[pl.BlockSpec((n, tr, C), lambda i: (0, i, 0))],
                   out_specs=pl.BlockSpec((tr, C), lambda i: (i, 0)), out_shape=jax.ShapeDtypeStruct((R, C), F32),
                   compiler_params=_params("parallel"))(parts)


def _adamw(w, g, m, v, name):
    L, R, C = w.shape
    tr = _row_tile(R, C, 8)

    def body(w_ref, g_ref, m_ref, v_ref, d_ref, m2_ref, v2_ref):
        gg = g_ref[...]
        m2 = ADAM_B1 * m_ref[...] + (1.0 - ADAM_B1) * gg
        v2 = ADAM_B2 * v_ref[...] + (1.0 - ADAM_B2) * (gg * gg)
        m_hat = m2 / (1.0 - ADAM_B1 ** ADAM_STEP)
        v_hat = v2 / (1.0 - ADAM_B2 ** ADAM_STEP)
        d_ref[...] = -ADAM_LR * (m_hat / (jnp.sqrt(v_hat) + ADAM_EPS) + ADAM_WD * w_ref[...])
        m2_ref[...] = m2
        v2_ref[...] = v2

    blk = pl.BlockSpec((1, tr, C), lambda l, i: (l, i, 0))
    shp = jax.ShapeDtypeStruct((L, R, C), F32)
    return _pallas(body, name=name, grid=(L, R // tr), in_specs=[blk] * 4, out_specs=[blk] * 3, out_shape=[shp] * 3,
                   compiler_params=_params("parallel", "parallel"))(w, g, m, v)


WEIGHTS = ("norm_even", "w_in_even", "conv_a_w", "conv_a_b", "ln_a_g", "ln_a_b", "pool_w", "pool_b", "pool_scale",
           "w_out_even", "norm_odd", "w_in_odd", "conv_c_w", "conv_c_b", "w_rg", "b_rg", "w_ig", "b_ig", "lru_lambda",
           "w_out_odd", "final_norm")
BIG = ("w_in_even", "w_out_even", "pool_w", "w_in_odd", "w_out_odd", "w_rg", "w_ig")
SMALL = tuple(n for n in WEIGHTS if n not in BIG)
SMALL_SHARDED = ("conv_a_w", "pool_b", "norm_odd", "conv_c_w", "conv_c_b", "b_rg", "b_ig", "lru_lambda")


def _pack(arrs):
    flat = jnp.concatenate([a.reshape(-1) for a in arrs])
    rows = -(-flat.shape[0] // (64 * 128)) * 64
    return jnp.pad(flat, (0, rows * 128 - flat.shape[0])).reshape(rows, 128)


def _unpack(buf, shapes, lead=()):
    flat = buf.reshape(tuple(lead) + (-1,))
    out, o = [], 0
    for s in shapes:
        n = 1
        for d in s:
            n *= d
        out.append(flat[..., o:o + n].reshape(tuple(lead) + tuple(s)))
        o += n
    return out


def _shard(full, axis, k):
    n = full.shape[axis] // N_CHIPS
    return lax.dynamic_slice_in_dim(full, k * n, n, axis)


def kernel(x, norm_even, w_in_even, conv_a_w, conv_a_b, ln_a_g, ln_a_b, pool_w, pool_b, pool_scale, w_out_even, norm_odd, w_in_odd, conv_c_w, conv_c_b, w_rg, b_rg, w_ig, b_ig, lru_lambda, w_out_odd, final_norm, loss_target, m_norm_even, m_w_in_even, m_conv_a_w, m_conv_a_b, m_ln_a_g, m_ln_a_b, m_pool_w, m_pool_b, m_pool_scale, m_w_out_even, m_norm_odd, m_w_in_odd, m_conv_c_w, m_conv_c_b, m_w_rg, m_b_rg, m_w_ig, m_b_ig, m_lru_lambda, m_w_out_odd, m_final_norm, v_norm_even, v_w_in_even, v_conv_a_w, v_conv_a_b, v_ln_a_g, v_ln_a_b, v_pool_w, v_pool_b, v_pool_scale, v_w_out_even, v_norm_odd, v_w_in_odd, v_conv_c_w, v_conv_c_b, v_w_rg, v_b_rg, v_w_ig, v_b_ig, v_lru_lambda, v_w_out_odd, v_final_norm):
    P = dict(locals())
    xi, yi, ci = _mesh_pos()
    k = 2 * xi + yi
    L = w_in_even.shape[0]
    D = D_MODEL

    pos = jnp.stack([k, ci]).astype(jnp.int32)
    big = [_cast_shard(w, pos[0:1]) for w in
           (w_in_even, w_out_even, w_in_odd, w_out_odd, pool_w.reshape(L, 4 * 64, POOL_GW))]
    g_wie, g_woe, g_wio, g_woo, g_pw, g_small = _gather_weights(big, _pack([P[n] for n in SMALL_SHARDED]))
    full = {}
    for n, a in zip(SMALL_SHARDED, _unpack(g_small, [P[n].shape for n in SMALL_SHARDED], lead=(N_CHIPS,))):
        a = jnp.moveaxis(a, 0, -2)
        full[n] = a.reshape(a.shape[:-2] + (N_CHIPS * a.shape[-1],))
    pw_full = g_pw.reshape(L, N_CHIPS, 4, 64, POOL_GW).transpose(0, 2, 1, 3, 4).reshape(L, 4, POOL_GW, POOL_GW)
    even, odd = [], []
    for j in range(L):
        cw = full["conv_a_w"][j]
        even.append(dict(norm=norm_even[j][None], conv_w=_pad_rows(cw, 32), conv_w_rev=_pad_rows(cw[::-1], 32),
                         conv_b=conv_a_b[j][None], ln_g=ln_a_g[j][None], ln_b=ln_a_b[j][None], pool_w=pw_full[j],
                         pool_b=full["pool_b"][j].reshape(1, D), pool_scale=pool_scale[j][None]))
        odd.append(dict(norm=full["norm_odd"][j][None], conv_w=_pad_rows(full["conv_c_w"][j], 8),
                        conv_b=full["conv_c_b"][j][None], w_rg=w_rg[j].astype(BF16), b_rg=full["b_rg"][j][None],
                        w_ig=w_ig[j].astype(BF16), b_ig=full["b_ig"][j][None], lam=full["lru_lambda"][j][None]))
    wt = dict(w_in_even=g_wie, w_out_even=g_woe.reshape(L, 2 * D, D), w_in_odd=g_wio,
              w_out_odd=g_woo.reshape(L, W_LRU, D), even=even, odd=odd, final_norm=final_norm[None])

    loss, grad_x, gbig, g_even, g_odd, d_final = _local_step(x[0], loss_target[0], wt)
    loss = lax.psum(loss[0, 0], ("x", "y", "c"))

    dpw = jnp.stack([g["pool_w"] for g in g_even]).reshape(L, 4, N_CHIPS, 64, POOL_GW).transpose(0, 2, 1, 3, 4)
    parts = [gbig["w_in_even"], gbig["w_out_even"].reshape(L, N_CHIPS, -1, D),
             dpw.reshape(L, N_CHIPS, 4 * 64, POOL_GW).astype(BF16),
             gbig["w_in_odd"], gbig["w_out_odd"].reshape(L, N_CHIPS, -1, D),
             jnp.stack([g["w_rg"] for g in g_odd]).reshape(L, N_CHIPS, -1, LRU_HD).astype(BF16),
             jnp.stack([g["w_ig"] for g in g_odd]).reshape(L, N_CHIPS, -1, LRU_HD).astype(BF16)]
    everywhere = [False, False, False, False, False, True, True]
    small_g = []
    for j in range(L):
        ge, go = g_even[j], g_odd[j]
        small_g += [ge["conv_w"].reshape(32, 8, D).sum(axis=1)[:CONV_K], ge["vec"][0:5], ge["norm"], go["vec"], go["norm"]]
    small_g.append(d_final)
    small_shapes = [a.shape for a in small_g]
    recv = _exchange_cores(parts, _pack(small_g))
    pair_sums = [_add_cores(a, r, pos) for a, r in zip(parts, recv[:-1])]
    from_chips = _exchange_chips(pair_sums)
    gw = _exchange_final([_sum_chips(a, r, pos, ev) for a, r, ev in zip(pair_sums, from_chips, everywhere)], everywhere)
    sg = _unpack(_sum_devices(recv[-1]), small_shapes)

    grads = dict(w_in_even=gw[0], w_out_even=gw[1], pool_w=gw[2].reshape(pool_w.shape), w_in_odd=gw[3], w_out_odd=gw[4],
                 w_rg=gw[5].reshape(w_rg.shape), w_ig=gw[6].reshape(w_ig.shape), final_norm=sg[-1][0])
    ev = [sg[5 * j + 1] for j in range(L)]
    ov = [sg[5 * j + 3] for j in range(L)]
    grads["conv_a_w"] = _shard(jnp.stack([sg[5 * j] for j in range(L)]), 2, k)
    grads["norm_even"] = jnp.stack([sg[5 * j + 2][0] for j in range(L)])
    grads["norm_odd"] = _shard(jnp.stack([sg[5 * j + 4][0] for j in range(L)]), 1, k)
    for r, n in enumerate(("conv_a_b", "ln_a_g", "ln_a_b", "pool_scale")):
        grads[n] = jnp.stack([e[r] for e in ev])
    grads["pool_b"] = _shard(jnp.stack([e[4].reshape(4, POOL_GW) for e in ev]), 2, k)
    grads["conv_c_w"] = _shard(jnp.stack([o[0:4] for o in ov]), 2, k)
    for r, n in zip((4, 5, 6, 7), ("conv_c_b", "b_rg", "b_ig", "lru_lambda")):
        grads[n] = _shard(jnp.stack([o[r] for o in ov]), 1, k)

    delta, new_m, new_v = {}, {}, {}
    for n in BIG:
        s3 = (L, -1, P[n].shape[-1])
        d, m2, v2 = _adamw(P[n].reshape(s3), grads[n].reshape(s3), P["m_" + n].reshape(s3), P["v_" + n].reshape(s3), "adamw")
        delta[n], new_m[n], new_v[n] = d.reshape(P[n].shape), m2.reshape(P[n].shape), v2.reshape(P[n].shape)
    shapes = [P[n].shape for n in SMALL]
    packed = [_pack([src[n] for n in SMALL])[None] for src in
              (P, grads, {n: P["m_" + n] for n in SMALL}, {n: P["v_" + n] for n in SMALL})]
    for res, out in zip(_adamw(*packed, "adamw_small"), (delta, new_m, new_v)):
        for n, a in zip(SMALL, _unpack(res[0], shapes)):
            out[n] = a

    return (loss, grad_x[None], *[grads[n] for n in WEIGHTS], *[delta[n] for n in WEIGHTS],
            *[new_m[n] for n in WEIGHTS], *[new_v[n] for n in WEIGHTS])
```

```python
import functools

import jax
import jax.numpy as jnp
from jax import lax
from jax.experimental import pallas as pl
from jax.experimental.pallas import tpu as pltpu

F32 = jnp.float32
BF16 = jnp.bfloat16
MESH = pl.DeviceIdType.MESH

D_MODEL = 1024
N_CHIPS = 4
N_DEV = 8
EPS_RMS = 1e-6
EPS_LN = 1e-5
CONV_K = 31
POOL_WINDOWS = (2, 4, 8, 16)
POOL_GW = 256
LRU_HEADS = 12
LRU_HD = 128
W_LRU = LRU_HEADS * LRU_HD
LRU_CONV_K = 4
LRU_C = 8.0
ADAM_LR = 0.001
ADAM_B1 = 0.9
ADAM_B2 = 0.999
ADAM_EPS = 1e-08
ADAM_WD = 0.01
ADAM_STEP = 10

VMEM_LIMIT_BYTES = 56 * 1024 * 1024
ROW_TILE = 512
MIX_TILE = 256
EVEN_HALO = 32
ODD_HALO = 8


def _pallas(body, **kw):
    return pl.pallas_call(body, **kw)


def _params(*sem):
    return pltpu.CompilerParams(dimension_semantics=sem if sem else None, vmem_limit_bytes=VMEM_LIMIT_BYTES)


def _sigmoid(x):
    return 1.0 / (1.0 + jnp.exp(-x))


def _dsilu(x, s):
    return s * (1.0 + x * (1.0 - s))


def _nt(a, b):
    return lax.dot_general(a, b, (((1,), (1,)), ((), ())), preferred_element_type=F32)


def _tn(a, b):
    return lax.dot_general(a, b, (((0,), (0,)), ((), ())), preferred_element_type=F32)


def _in_proj(h, g, wg, layer, after, name):
    T, D = h.shape
    _, nblk, _, nb = wg.shape

    def body(h_ref, g_ref, w_ref, after_ref, p_ref, n_ref):
        @pl.when(pl.program_id(1) == 0)
        def _():
            x = h_ref[...]
            r = lax.rsqrt(jnp.mean(x * x, axis=-1, keepdims=True) + EPS_RMS)
            n_ref[...] = (x * r * g_ref[...]).astype(BF16)

        p_ref[...] = jnp.dot(n_ref[...], w_ref[0], preferred_element_type=F32)

    return _pallas(
        body, name=name, grid=(T // ROW_TILE, nblk),
        in_specs=[pl.BlockSpec((ROW_TILE, D), lambda i, j: (i, 0)), pl.BlockSpec((1, D), lambda i, j: (0, 0)),
                  pl.BlockSpec((None, 1, D, nb), lambda i, j: (layer, j, 0, 0)),
                  pl.BlockSpec((8, 128), lambda i, j: (0, 0))],
        out_specs=[pl.BlockSpec((ROW_TILE, nb), lambda i, j: (i, j)), pl.BlockSpec((ROW_TILE, D), lambda i, j: (i, 0))],
        out_shape=[jax.ShapeDtypeStruct((T, nblk * nb), F32), jax.ShapeDtypeStruct((T, D), BF16)],
        compiler_params=_params("parallel", "arbitrary"))(h, g, wg, after)


def _out_proj(y, w, layer, hres, name):
    T, K = y.shape
    D = w.shape[2]

    def body(y_ref, w_ref, r_ref, o_ref):
        o_ref[...] = r_ref[...] + jnp.dot(y_ref[...], w_ref[...], preferred_element_type=F32)

    return _pallas(
        body, name=name, grid=(T // ROW_TILE,),
        in_specs=[pl.BlockSpec((ROW_TILE, K), lambda i: (i, 0)), pl.BlockSpec((None, K, D), lambda i: (layer, 0, 0)),
                  pl.BlockSpec((ROW_TILE, D), lambda i: (i, 0))],
        out_specs=pl.BlockSpec((ROW_TILE, D), lambda i: (i, 0)),
        out_shape=jax.ShapeDtypeStruct((T, D), F32),
        compiler_params=_params("parallel"))(y, w, hres)


def _dy_proj(dout, w, layer, after, name):
    T, D = dout.shape
    K = w.shape[1]

    def body(d_ref, w_ref, after_ref, o_ref):
        o_ref[...] = _nt(d_ref[...], w_ref[...])

    return _pallas(
        body, name=name, grid=(T // ROW_TILE,),
        in_specs=[pl.BlockSpec((ROW_TILE, D), lambda i: (i, 0)), pl.BlockSpec((None, K, D), lambda i: (layer, 0, 0)),
                  pl.BlockSpec((8, 128), lambda i: (0, 0))],
        out_specs=pl.BlockSpec((ROW_TILE, K), lambda i: (i, 0)),
        out_shape=jax.ShapeDtypeStruct((T, K), F32),
        compiler_params=_params("parallel"))(dout, w, after)


def _dn_proj(dp, wg, layer, h, g, dres, name):
    T, D = h.shape
    _, nblk, _, nb = wg.shape

    def body(dp_ref, w_ref, h_ref, g_ref, dres_ref, dh_ref, dhb_ref, dg_ref, acc_ref):
        i, j = pl.program_id(0), pl.program_id(1)
        part = _nt(dp_ref[...], w_ref[0])

        @pl.when(j == 0)
        def _():
            acc_ref[...] = part

        @pl.when(j > 0)
        def _():
            acc_ref[...] += part

        @pl.when(j == nblk - 1)
        def _():
            x = h_ref[...]
            r = lax.rsqrt(jnp.mean(x * x, axis=-1, keepdims=True) + EPS_RMS)
            dn = acc_ref[...]
            q = dn * g_ref[...]
            dh = dres_ref[...] + r * q - x * ((r * r * r) * jnp.mean(q * x, axis=-1, keepdims=True))
            dh_ref[...] = dh
            dhb_ref[...] = dh.astype(BF16)
            dgp = jnp.sum(dn * (x * r), axis=0, keepdims=True)

            @pl.when(i == 0)
            def _():
                dg_ref[...] = dgp

            @pl.when(i > 0)
            def _():
                dg_ref[...] += dgp

    return _pallas(
        body, name=name, grid=(T // ROW_TILE, nblk),
        in_specs=[pl.BlockSpec((ROW_TILE, nb), lambda i, j: (i, j)),
                  pl.BlockSpec((None, 1, D, nb), lambda i, j: (layer, j, 0, 0)),
                  pl.BlockSpec((ROW_TILE, D), lambda i, j: (i, 0)), pl.BlockSpec((1, D), lambda i, j: (0, 0)),
                  pl.BlockSpec((ROW_TILE, D), lambda i, j: (i, 0))],
        out_specs=[pl.BlockSpec((ROW_TILE, D), lambda i, j: (i, 0)), pl.BlockSpec((ROW_TILE, D), lambda i, j: (i, 0)),
                   pl.BlockSpec((1, D), lambda i, j: (0, 0))],
        out_shape=[jax.ShapeDtypeStruct((T, D), F32), jax.ShapeDtypeStruct((T, D), BF16),
                   jax.ShapeDtypeStruct((1, D), F32)],
        scratch_shapes=[pltpu.VMEM((ROW_TILE, D), F32)],
        compiler_params=_params("arbitrary", "arbitrary"))(dp, wg, h, g, dres)


def _dw_in(n, dp, nblk, layer, nlayers, prev, name):
    T, D = n.shape
    nb = dp.shape[1] // nblk
    ta = 512

    def body(n_ref, dp_ref, *rest):
        rest[-1][0] = _tn(n_ref[...], dp_ref[...]).astype(BF16)

    in_specs = [pl.BlockSpec((T, ta), lambda j, i: (0, i)), pl.BlockSpec((T, nb), lambda j, i: (0, j))]
    args = (n, dp) if prev is None else (n, dp, prev)
    return _pallas(
        body, name=name, grid=(nblk, D // ta), in_specs=in_specs + ([] if prev is None else [ANY]),
        out_specs=pl.BlockSpec((None, 1, ta, nb), lambda j, i: (layer, j, i, 0)),
        out_shape=jax.ShapeDtypeStruct((nlayers, nblk, D, nb), BF16),
        input_output_aliases={} if prev is None else {2: 0},
        compiler_params=_params("parallel", "parallel"))(*args)


def _dw_out(y, dout, layer, nlayers, prev, name):
    T, K = y.shape
    D = dout.shape[1]
    tk = 512

    def body(y_ref, d_ref, *rest):
        rest[-1][...] = _tn(y_ref[...], d_ref[...]).astype(BF16)

    in_specs = [pl.BlockSpec((T, tk), lambda i: (0, i)), pl.BlockSpec((T, D), lambda i: (0, 0))]
    args = (y, dout) if prev is None else (y, dout, prev)
    return _pallas(
        body, name=name, grid=(K // tk,), in_specs=in_specs + ([] if prev is None else [ANY]),
        out_specs=pl.BlockSpec((None, tk, D), lambda i: (layer, i, 0)),
        out_shape=jax.ShapeDtypeStruct((nlayers, K, D), BF16),
        input_output_aliases={} if prev is None else {2: 0},
        compiler_params=_params("parallel"))(*args)


def _loss_head(h, g, tgt):
    T, D = h.shape
    tm = MIX_TILE

    def body(h_ref, g_ref, t_ref, dh_ref, dhb_ref, dg_ref, loss_ref):
        i = pl.program_id(0)
        x = h_ref[...]
        gg = g_ref[...]
        r = lax.rsqrt(jnp.mean(x * x, axis=-1, keepdims=True) + EPS_RMS)
        xr = x * r
        e = xr * gg - t_ref[...]
        lp = 0.5 * jnp.sum(jnp.mean(e * e, axis=-1, keepdims=True), axis=0, keepdims=True)
        dn = e * (1.0 / D)
        q = dn * gg
        dh = r * q - x * ((r * r * r) * jnp.mean(q * x, axis=-1, keepdims=True))
        dh_ref[...] = dh
        dhb_ref[...] = dh.astype(BF16)
        dgp = jnp.sum(dn * xr, axis=0, keepdims=True)

        @pl.when(i == 0)
        def _():
            dg_ref[...] = dgp
            loss_ref[...] = lp

        @pl.when(i > 0)
        def _():
            dg_ref[...] += dgp
            loss_ref[...] += lp

    return _pallas(
        body, name="loss_head", grid=(T // tm,),
        in_specs=[pl.BlockSpec((tm, D), lambda i: (i, 0)), pl.BlockSpec((1, D), lambda i: (0, 0)),
                  pl.BlockSpec((tm, D), lambda i: (i, 0))],
        out_specs=[pl.BlockSpec((tm, D), lambda i: (i, 0)), pl.BlockSpec((tm, D), lambda i: (i, 0)),
                   pl.BlockSpec((1, D), lambda i: (0, 0)), pl.BlockSpec((1, 1), lambda i: (0, 0))],
        out_shape=[jax.ShapeDtypeStruct((T, D), F32), jax.ShapeDtypeStruct((T, D), BF16),
                   jax.ShapeDtypeStruct((1, D), F32), jax.ShapeDtypeStruct((1, 1), F32)],
        compiler_params=_params("arbitrary"))(h, g, tgt)


def _shift_up(x, j):
    return x if j == 0 else pltpu.roll(x, x.shape[0] - j, 0)


def _shift_down(x, j):
    return x if j == 0 else pltpu.roll(x, j, 0)


def _fill_shifted(dst_ref, x):
    rows = dst_ref.shape[1]
    for s in range(8):
        dst_ref[s] = _shift_up(x, s)[0:rows]


def _tap_sum(sh_ref, w_ref, r0, nrows, offsets):
    acc = None
    for k, o in enumerate(offsets):
        win = sh_ref[o % 8, pl.ds(r0 + (o // 8) * 8, nrows), :]
        term = w_ref[k:k + 1, :] * win
        acc = term if acc is None else acc + term
    return acc


def _pool_sums(vx, up):
    sh = _shift_up if up else _shift_down
    outs = []
    for gi, w in enumerate(POOL_WINDOWS):
        s = vx[:, gi * POOL_GW:(gi + 1) * POOL_GW]
        j = 1
        while j < w:
            s = s + sh(s, j)
            j *= 2
        outs.append(s)
    return outs


def _inv_count(row0, nrows):
    pos = (row0 + 1 + lax.broadcasted_iota(jnp.int32, (nrows, 1), 0)).astype(F32)
    return [1.0 / jnp.minimum(pos, float(w)) for w in POOL_WINDOWS]


def _even_mixer_fwd(p, cw, cb, lg, lb, pw, pb, sc, name):
    T = p.shape[0]
    C = D_MODEL
    tT, HL = MIX_TILE, EVEN_HALO
    hb = tT // HL
    chunk = 16

    def body(pm_ref, ph_ref, cw_ref, cb_ref, lg_ref, lb_ref, pw_ref, pb_ref, sc_ref, y_ref, u1_ref, u0x_ref, sh_ref):
        i = pl.program_id(0)
        keep = (i > 0).astype(F32)
        u0x_ref[0:HL] = ph_ref[:, 0:C] * _sigmoid(ph_ref[:, C:2 * C]) * keep
        u0x_ref[HL:HL + tT] = pm_ref[:, 0:C] * _sigmoid(pm_ref[:, C:2 * C])
        u0x_ref[HL + tT:HL + tT + 8] = jnp.zeros((8, C), F32)
        _fill_shifted(sh_ref, u0x_ref[...])
        offs = [HL - (CONV_K - 1) + k for k in range(CONV_K)]

        def conv_chunk(c, carry):
            r0 = pl.multiple_of(c * chunk, chunk)
            u1_ref[pl.ds(r0, chunk), :] = _tap_sum(sh_ref, cw_ref, r0, chunk, offs) + cb_ref[...]
            return carry

        lax.fori_loop(0, tT // chunk, conv_chunk, 0)
        u1 = u1_ref[...]
        mu = jnp.mean(u1, axis=-1, keepdims=True)
        xc = u1 - mu
        rs = lax.rsqrt(jnp.mean(xc * xc, axis=-1, keepdims=True) + EPS_LN)
        u2 = xc * rs * lg_ref[...] + lb_ref[...]
        u3 = u2 * _sigmoid(u2)
        ag = pm_ref[:, 2 * C:3 * C]
        y_ref[:, 0:C] = (u3 * (ag * _sigmoid(ag))).astype(BF16)
        vx = jnp.concatenate([ph_ref[:, 3 * C:4 * C] * keep, pm_ref[:, 3 * C:4 * C]], axis=0)
        sums = _pool_sums(vx, up=False)
        inv = _inv_count(i * tT, tT)
        for gi in range(len(POOL_WINDOWS)):
            cols = slice(gi * POOL_GW, (gi + 1) * POOL_GW)
            d0 = sums[gi][HL:] * inv[gi] - vx[HL:, cols]
            d1 = jnp.dot(d0.astype(BF16), pw_ref[gi], preferred_element_type=F32) + pb_ref[:, cols]
            bg = pm_ref[:, 4 * C + gi * POOL_GW:4 * C + (gi + 1) * POOL_GW]
            y_ref[:, C + gi * POOL_GW:C + (gi + 1) * POOL_GW] = (d1 * sc_ref[:, cols] * (bg * _sigmoid(bg))).astype(BF16)

    vec = pl.BlockSpec((1, C), lambda i: (0, 0))
    return _pallas(
        body, name=name, grid=(T // tT,),
        in_specs=[pl.BlockSpec((tT, 5 * C), lambda i: (i, 0)),
                  pl.BlockSpec((HL, 5 * C), lambda i: (jnp.maximum(i * hb - 1, 0), 0)),
                  pl.BlockSpec((32, C), lambda i: (0, 0)), vec, vec, vec,
                  pl.BlockSpec((4, POOL_GW, POOL_GW), lambda i: (0, 0, 0)), vec, vec],
        out_specs=[pl.BlockSpec((tT, 2 * C), lambda i: (i, 0)), pl.BlockSpec((tT, C), lambda i: (i, 0))],
        out_shape=[jax.ShapeDtypeStruct((T, 2 * C), BF16), jax.ShapeDtypeStruct((T, C), F32)],
        scratch_shapes=[pltpu.VMEM((HL + tT + 8, C), F32), pltpu.VMEM((8, HL + tT, C), F32)],
        compiler_params=_params("parallel"))(p, p, cw, cb, lg, lb, pw, pb, sc)


def _even_mixer_bwd(p, u1, dy, cw, cwr, lg, lb, pw, pb, sc, name):
    T = p.shape[0]
    C = D_MODEL
    tT, HL = MIX_TILE, EVEN_HALO
    hb = tT // HL
    nT = T // tT
    R1 = tT + HL
    chunk = 16

    def body(pm_ref, pp_ref, pn_ref, u1m_ref, u1n_ref, dym_ref, dyn_ref, cw_ref, cwr_ref, lg_ref, lb_ref, pw_ref,
             pb_ref, sc_ref, dp_ref, dcw_ref, dvec_ref, dpw_ref, x_ref, sh_ref, du0_ref):
        i = pl.program_id(0)
        keep_prev = (i > 0).astype(F32)
        keep_next = (i < nT - 1).astype(F32)
        row = lax.broadcasted_iota(jnp.int32, (R1, 1), 0)
        live = jnp.where(row < tT, 1.0, keep_next)

        def cat(m, n):
            return jnp.concatenate([m, n], axis=0)

        u1 = cat(u1m_ref[...], u1n_ref[...])
        mu = jnp.mean(u1, axis=-1, keepdims=True)
        xc = u1 - mu
        rs = lax.rsqrt(jnp.mean(xc * xc, axis=-1, keepdims=True) + EPS_LN)
        xh = xc * rs
        u2 = xh * lg_ref[...] + lb_ref[...]
        s2 = _sigmoid(u2)
        u3 = u2 * s2
        ag = cat(pm_ref[:, 2 * C:3 * C], pn_ref[:, 2 * C:3 * C])
        sa = _sigmoid(ag)
        dya = cat(dym_ref[:, 0:C], dyn_ref[:, 0:C])
        dp_ref[:, 2 * C:3 * C] = (dya * u3 * _dsilu(ag, sa))[0:tT].astype(BF16)
        du2 = dya * (ag * sa) * _dsilu(u2, s2)
        dlg = jnp.sum((du2 * xh)[0:tT], axis=0, keepdims=True)
        dlb = jnp.sum(du2[0:tT], axis=0, keepdims=True)
        dxh = du2 * lg_ref[...]
        du1 = rs * (dxh - jnp.mean(dxh, axis=-1, keepdims=True) - xh * jnp.mean(dxh * xh, axis=-1, keepdims=True))
        du1 = du1 * live
        dcb = jnp.sum(du1[0:tT], axis=0, keepdims=True)
        x_ref[0:R1] = du1
        x_ref[R1:R1 + 8] = jnp.zeros((8, C), F32)
        _fill_shifted(sh_ref, x_ref[...])

        def du0_chunk(c, carry):
            r0 = pl.multiple_of(c * chunk, chunk)
            du0_ref[pl.ds(r0, chunk), :] = _tap_sum(sh_ref, cwr_ref, r0, chunk, list(range(CONV_K)))
            return carry

        lax.fori_loop(0, tT // chunk, du0_chunk, 0)
        av, agl = pm_ref[:, 0:C], pm_ref[:, C:2 * C]
        sg = _sigmoid(agl)
        du0 = du0_ref[...]
        dp_ref[:, 0:C] = (du0 * sg).astype(BF16)
        dp_ref[:, C:2 * C] = (du0 * av * sg * (1.0 - sg)).astype(BF16)
        du0_ref[...] = du1[0:tT]
        x_ref[0:HL] = pp_ref[:, 0:C] * _sigmoid(pp_ref[:, C:2 * C]) * keep_prev
        x_ref[HL:HL + tT] = av * sg
        x_ref[HL + tT:HL + tT + 8] = jnp.zeros((8, C), F32)
        _fill_shifted(sh_ref, x_ref[...])

        @pl.when(i == 0)
        def _():
            dcw_ref[...] = jnp.zeros_like(dcw_ref)

        for k in range(CONV_K):
            o = HL - (CONV_K - 1) + k

            def dw_chunk(c, acc, o=o):
                r0 = pl.multiple_of(c * 64, 64)
                for u in range(0, 64, 8):
                    acc = acc + du0_ref[pl.ds(r0 + u, 8), :] * sh_ref[o % 8, pl.ds(r0 + u + (o // 8) * 8, 8), :]
                return acc

            dcw_ref[8 * k:8 * k + 8, :] += lax.fori_loop(0, tT // 64, dw_chunk, jnp.zeros((8, C), F32))

        bg = cat(pm_ref[:, 4 * C:5 * C], pn_ref[:, 4 * C:5 * C])
        sb = _sigmoid(bg)
        dyb = cat(dym_ref[:, C:2 * C], dyn_ref[:, C:2 * C])
        dyb0 = dyb * (bg * sb)
        dd1 = dyb0 * sc_ref[...]
        dpb = jnp.sum(dd1[0:tT], axis=0, keepdims=True)
        inv1 = _inv_count(i * tT, R1)
        z_parts, dd0_parts = [], []
        for gi in range(len(POOL_WINDOWS)):
            cols = slice(gi * POOL_GW, (gi + 1) * POOL_GW)
            dd0 = _nt(dd1[:, cols].astype(BF16), pw_ref[gi])
            dd0_parts.append(dd0)
            z_parts.append(dd0 * inv1[gi] * live)
        fsum = _pool_sums(jnp.concatenate(z_parts, axis=1), up=True)
        vx = cat(pp_ref[:, 3 * C:4 * C] * keep_prev, pm_ref[:, 3 * C:4 * C])
        sums = _pool_sums(vx, up=False)
        inv0 = _inv_count(i * tT, tT)
        dsc_parts = []
        for gi in range(len(POOL_WINDOWS)):
            cols = slice(gi * POOL_GW, (gi + 1) * POOL_GW)
            dp_ref[:, 3 * C + gi * POOL_GW:3 * C + (gi + 1) * POOL_GW] = (fsum[gi][0:tT] - dd0_parts[gi][0:tT]).astype(BF16)
            d0 = (sums[gi][HL:] * inv0[gi] - vx[HL:, cols]).astype(BF16)
            d1 = jnp.dot(d0, pw_ref[gi], preferred_element_type=F32) + pb_ref[:, cols]
            bgm, sbm = bg[0:tT, cols], sb[0:tT, cols]
            dp_ref[:, 4 * C + gi * POOL_GW:4 * C + (gi + 1) * POOL_GW] = (
                dyb[0:tT, cols] * d1 * sc_ref[:, cols] * _dsilu(bgm, sbm)).astype(BF16)
            dsc_parts.append(jnp.sum(dyb0[0:tT, cols] * d1, axis=0, keepdims=True))
            dpw_g = _tn(d0, dd1[0:tT, cols].astype(BF16))

            @pl.when(i == 0)
            def _(gi=gi, dpw_g=dpw_g):
                dpw_ref[gi] = dpw_g

            @pl.when(i > 0)
            def _(gi=gi, dpw_g=dpw_g):
                dpw_ref[gi] += dpw_g

        dsc = jnp.concatenate(dsc_parts, axis=1)
        vecs = jnp.concatenate([dcb, dlg, dlb, dsc, dpb, jnp.zeros((3, C), F32)], axis=0)

        @pl.when(i == 0)
        def _():
            dvec_ref[...] = vecs

        @pl.when(i > 0)
        def _():
            dvec_ref[...] += vecs

    vec = pl.BlockSpec((1, C), lambda i: (0, 0))
    taps = pl.BlockSpec((32, C), lambda i: (0, 0))

    def prev_blk(i):
        return (jnp.maximum(i * hb - 1, 0), 0)

    def next_blk(i):
        return (jnp.minimum((i + 1) * hb, T // HL - 1), 0)

    return _pallas(
        body, name=name, grid=(nT,),
        in_specs=[pl.BlockSpec((tT, 5 * C), lambda i: (i, 0)), pl.BlockSpec((HL, 5 * C), prev_blk),
                  pl.BlockSpec((HL, 5 * C), next_blk),
                  pl.BlockSpec((tT, C), lambda i: (i, 0)), pl.BlockSpec((HL, C), next_blk),
                  pl.BlockSpec((tT, 2 * C), lambda i: (i, 0)), pl.BlockSpec((HL, 2 * C), next_blk),
                  taps, taps, vec, vec, pl.BlockSpec((4, POOL_GW, POOL_GW), lambda i: (0, 0, 0)), vec, vec],
        out_specs=[pl.BlockSpec((tT, 5 * C), lambda i: (i, 0)), pl.BlockSpec((32 * 8, C), lambda i: (0, 0)),
                   pl.BlockSpec((8, C), lambda i: (0, 0)), pl.BlockSpec((4, POOL_GW, POOL_GW), lambda i: (0, 0, 0))],
        out_shape=[jax.ShapeDtypeStruct((T, 5 * C), BF16), jax.ShapeDtypeStruct((32 * 8, C), F32),
                   jax.ShapeDtypeStruct((8, C), F32), jax.ShapeDtypeStruct((4, POOL_GW, POOL_GW), F32)],
        scratch_shapes=[pltpu.VMEM((R1 + 8, C), F32), pltpu.VMEM((8, R1, C), F32), pltpu.VMEM((tT, C), F32)],
        compiler_params=_params("arbitrary"))(p, p, p, u1, u1, dy, dy, cw, cwr, lg, lb, pw, pb, sc)


def _softplus(z):
    u = jnp.exp(-jnp.abs(z))
    w = 1.0 + u
    l1p = jnp.where(w == 1.0, u, u * jnp.log(w) / jnp.where(w == 1.0, 1.0, w - 1.0))
    return jnp.maximum(z, 0.0) + l1p


def _lru_gates(xrx, cw_ref, cb_ref, wr_ref, br_ref, wi_ref, bi_ref, lam_ref):
    HL = ODD_HALO
    xc = cb_ref[...] + cw_ref[LRU_CONV_K - 1:LRU_CONV_K, :] * xrx[HL:]
    for k in range(LRU_CONV_K - 1):
        xc = xc + cw_ref[k:k + 1, :] * _shift_down(xrx, LRU_CONV_K - 1 - k)[HL:]
    xcb = xc.astype(BF16)
    rp, ip = [], []
    for hd in range(LRU_HEADS):
        cols = slice(hd * LRU_HD, (hd + 1) * LRU_HD)
        rp.append(jnp.dot(xcb[:, cols], wr_ref[hd], preferred_element_type=F32))
        ip.append(jnp.dot(xcb[:, cols], wi_ref[hd], preferred_element_type=F32))
    r = _sigmoid(jnp.concatenate(rp, axis=1) + br_ref[...])
    ig = _sigmoid(jnp.concatenate(ip, axis=1) + bi_ref[...])
    sp = _softplus(-lam_ref[...])
    log_a = (-LRU_C) * r * sp
    a = jnp.exp(log_a)
    mult = jnp.sqrt(-jnp.tanh(log_a) * (a * a + 1.0))
    return xc, xcb, r, ig, sp, a, mult, 1.0 / mult


def _odd_mixer_fwd(p, cw, cb, wr, br, wi, bi, lam, name):
    T = p.shape[0]
    W = W_LRU
    tT, HL = MIX_TILE, ODD_HALO
    hb = tT // HL

    def body(pm_ref, ph_ref, cw_ref, cb_ref, wr_ref, br_ref, wi_ref, bi_ref, lam_ref, y_ref, hs_ref, carry_ref):
        i = pl.program_id(0)
        keep = (i > 0).astype(F32)

        @pl.when(i == 0)
        def _():
            carry_ref[...] = jnp.zeros_like(carry_ref)

        xrx = jnp.concatenate([ph_ref[:, 0:W] * keep, pm_ref[:, 0:W]], axis=0)
        xc, _, _, ig, _, a, mult, _ = _lru_gates(xrx, cw_ref, cb_ref, wr_ref, br_ref, wi_ref, bi_ref, lam_ref)
        b = mult * (ig * xc)
        row = lax.broadcasted_iota(jnp.int32, (tT, 1), 0)
        s = 1
        while s < tT:
            ok = row >= s
            a_sh = jnp.where(ok, _shift_down(a, s), 1.0)
            b_sh = jnp.where(ok, _shift_down(b, s), 0.0)
            b = a * b_sh + b
            a = a * a_sh
            s *= 2
        hs = a * carry_ref[0:1, :] + b
        hs_ref[...] = hs
        carry_ref[...] = jnp.broadcast_to(hs[tT - 1:tT, :], (8, W))
        gt = pm_ref[:, W:2 * W]
        y_ref[...] = (hs * (gt * _sigmoid(gt))).astype(BF16)

    vec = pl.BlockSpec((1, W), lambda i: (0, 0))
    heads = pl.BlockSpec((LRU_HEADS, LRU_HD, LRU_HD), lambda i: (0, 0, 0))
    return _pallas(
        body, name=name, grid=(T // tT,),
        in_specs=[pl.BlockSpec((tT, 2 * W), lambda i: (i, 0)),
                  pl.BlockSpec((HL, 2 * W), lambda i: (jnp.maximum(i * hb - 1, 0), 0)),
                  pl.BlockSpec((8, W), lambda i: (0, 0)), vec, heads, vec, heads, vec, vec],
        out_specs=[pl.BlockSpec((tT, W), lambda i: (i, 0)), pl.BlockSpec((tT, W), lambda i: (i, 0))],
        out_shape=[jax.ShapeDtypeStruct((T, W), BF16), jax.ShapeDtypeStruct((T, W), F32)],
        scratch_shapes=[pltpu.VMEM((8, W), F32)],
        compiler_params=_params("arbitrary"))(p, p, cw, cb, wr, br, wi, bi, lam)


def _odd_mixer_bwd(p, hs, dy, cw, cb, wr, br, wi, bi, lam, name):
    T = p.shape[0]
    W = W_LRU
    tT, HL = MIX_TILE, ODD_HALO
    hb = tT // HL
    nT = T // tT

    def body(pm_ref, ph_ref, hsm_ref, hsh_ref, dy_ref, cw_ref, cb_ref, wr_ref, br_ref, wi_ref, bi_ref, lam_ref,
             dp_ref, dwr_ref, dwi_ref, dvec_ref, gcarry_ref, xcarry_ref):
        i = pl.program_id(0)
        keep = (i < nT - 1).astype(F32)

        @pl.when(i == 0)
        def _():
            gcarry_ref[...] = jnp.zeros_like(gcarry_ref)
            xcarry_ref[...] = jnp.zeros_like(xcarry_ref)

        xrx = jnp.concatenate([ph_ref[:, 0:W] * keep, pm_ref[:, 0:W]], axis=0)
        xc, xcb, r, ig, sp, a, mult, inv_mult = _lru_gates(xrx, cw_ref, cb_ref, wr_ref, br_ref, wi_ref, bi_ref, lam_ref)
        hs = hsm_ref[...]
        gt = pm_ref[:, W:2 * W]
        sg = _sigmoid(gt)
        dyv = dy_ref[...]
        dp_ref[:, W:2 * W] = (dyv * hs * _dsilu(gt, sg)).astype(BF16)
        row = lax.broadcasted_iota(jnp.int32, (tT, 1), 0)
        e = dyv * (gt * sg) + jnp.where(row == tT - 1, gcarry_ref[0:1, :], 0.0)
        m = jnp.where(row == tT - 1, 1.0, _shift_up(a, 1))
        s = 1
        while s < tT:
            ok = row < tT - s
            m_sh = jnp.where(ok, _shift_up(m, s), 1.0)
            e_sh = jnp.where(ok, _shift_up(e, s), 0.0)
            e = m * e_sh + e
            m = m * m_sh
            s *= 2
        G = e
        gcarry_ref[...] = jnp.broadcast_to(a[0:1, :] * G[0:1, :], (8, W))
        hs_prev = jnp.where(row == 0, hsh_ref[HL - 1:HL, :] * keep, _shift_down(hs, 1))
        da = G * hs_prev
        dmult = G * (ig * xc)
        di = G * mult * xc
        dxc = G * mult * ig
        dlog_a = da * a - dmult * (a * a) * inv_mult
        drp = dlog_a * ((-LRU_C) * sp) * r * (1.0 - r)
        dip = di * ig * (1.0 - ig)
        dlam = jnp.sum(dlog_a * ((-LRU_C) * r), axis=0, keepdims=True) * (-_sigmoid(-lam_ref[...]))
        drb, dib = drp.astype(BF16), dip.astype(BF16)
        back = []
        for hd in range(LRU_HEADS):
            cols = slice(hd * LRU_HD, (hd + 1) * LRU_HD)
            back.append(_nt(drb[:, cols], wr_ref[hd]) + _nt(dib[:, cols], wi_ref[hd]))
            dwr_h = _tn(xcb[:, cols], drb[:, cols])
            dwi_h = _tn(xcb[:, cols], dib[:, cols])

            @pl.when(i == 0)
            def _(hd=hd, dwr_h=dwr_h, dwi_h=dwi_h):
                dwr_ref[hd] = dwr_h
                dwi_ref[hd] = dwi_h

            @pl.when(i > 0)
            def _(hd=hd, dwr_h=dwr_h, dwi_h=dwi_h):
                dwr_ref[hd] += dwr_h
                dwi_ref[hd] += dwi_h

        dxc = dxc + jnp.concatenate(back, axis=1)
        dxcx = jnp.concatenate([dxc, xcarry_ref[...]], axis=0)
        dxr = cw_ref[LRU_CONV_K - 1:LRU_CONV_K, :] * dxc
        rows = []
        for k in range(LRU_CONV_K - 1):
            j = LRU_CONV_K - 1 - k
            dxr = dxr + cw_ref[k:k + 1, :] * _shift_up(dxcx, j)[0:tT]
            rows.append(jnp.sum(dxc * _shift_down(xrx, j)[HL:], axis=0, keepdims=True))
        rows.append(jnp.sum(dxc * xrx[HL:], axis=0, keepdims=True))
        dp_ref[:, 0:W] = dxr.astype(BF16)
        xcarry_ref[...] = dxc[0:8]
        rows += [jnp.sum(dxc, axis=0, keepdims=True), jnp.sum(drp, axis=0, keepdims=True),
                 jnp.sum(dip, axis=0, keepdims=True), dlam]
        vecs = jnp.concatenate(rows, axis=0)

        @pl.when(i == 0)
        def _():
            dvec_ref[...] = vecs

        @pl.when(i > 0)
        def _():
            dvec_ref[...] += vecs

    vec = pl.BlockSpec((1, W), lambda i: (0, 0))
    heads = pl.BlockSpec((LRU_HEADS, LRU_HD, LRU_HD), lambda i: (0, 0, 0))

    def tile(i):
        return (nT - 1 - i, 0)

    def prev_blk(i):
        return (jnp.maximum((nT - 1 - i) * hb - 1, 0), 0)

    return _pallas(
        body, name=name, grid=(nT,),
        in_specs=[pl.BlockSpec((tT, 2 * W), tile), pl.BlockSpec((HL, 2 * W), prev_blk),
                  pl.BlockSpec((tT, W), tile), pl.BlockSpec((HL, W), prev_blk), pl.BlockSpec((tT, W), tile),
                  pl.BlockSpec((8, W), lambda i: (0, 0)), vec, heads, vec, heads, vec, vec],
        out_specs=[pl.BlockSpec((tT, 2 * W), tile), heads, heads, pl.BlockSpec((8, W), lambda i: (0, 0))],
        out_shape=[jax.ShapeDtypeStruct((T, 2 * W), BF16), jax.ShapeDtypeStruct((LRU_HEADS, LRU_HD, LRU_HD), F32),
                   jax.ShapeDtypeStruct((LRU_HEADS, LRU_HD, LRU_HD), F32), jax.ShapeDtypeStruct((8, W), F32)],
        scratch_shapes=[pltpu.VMEM((8, W), F32), pltpu.VMEM((8, W), F32)],
        compiler_params=_params("arbitrary"))(p, p, hs, hs, dy, cw, cb, wr, br, wi, bi, lam)


def _pad_rows(a, rows):
    return jnp.concatenate([a, jnp.zeros((rows - a.shape[0], a.shape[1]), a.dtype)], axis=0)


def _layer_fwd(even, h, w, w_in, w_out, after):
    if even:
        p, n = _in_proj(h, w["norm"], w_in, 0, after, "in_proj_even")
        y, aux = _even_mixer_fwd(p, w["conv_w"], w["conv_b"], w["ln_g"], w["ln_b"], w["pool_w"], w["pool_b"],
                                 w["pool_scale"], "even_mixer_fwd")
        h_next = _out_proj(y, w_out, 0, h, "out_proj_even")
    else:
        p, n = _in_proj(h, w["norm"], w_in, 0, after, "in_proj_odd")
        y, aux = _odd_mixer_fwd(p, w["conv_w"], w["conv_b"], w["w_rg"], w["b_rg"], w["w_ig"], w["b_ig"], w["lam"],
                                "odd_mixer_fwd")
        h_next = _out_proj(y, w_out, 0, h, "out_proj_odd")
    return h_next, (h, n, p, aux, y)


def _layer_bwd(even, saved, w, w_in, w_out, dh, dhb, after):
    h, n, p, aux, y = saved
    if even:
        dw_out = _dw_out(y, dhb, 0, 1, None, "dw_out_even")
        dy = _dy_proj(dhb, w_out, 0, after, "dy_proj_even")
        dp, dcw, dvec, dpw = _even_mixer_bwd(p, aux, dy, w["conv_w"], w["conv_w_rev"], w["ln_g"], w["ln_b"],
                                             w["pool_w"], w["pool_b"], w["pool_scale"], "even_mixer_bwd")
        dw_in = _dw_in(n, dp, N_CHIPS, 0, 1, None, "dw_in_even")
        dh, dhb, dnorm = _dn_proj(dp, w_in, 0, h, w["norm"], dh, "dn_proj_even")
        return dh, dhb, dw_in, dw_out, dict(conv_w=dcw, vec=dvec, pool_w=dpw, norm=dnorm)
    dw_out = _dw_out(y, dhb, 0, 1, None, "dw_out_odd")
    dy = _dy_proj(dhb, w_out, 0, after, "dy_proj_odd")
    dp, dwr, dwi, dvec = _odd_mixer_bwd(p, aux, dy, w["conv_w"], w["conv_b"], w["w_rg"], w["b_rg"], w["w_ig"],
                                        w["b_ig"], w["lam"], "odd_mixer_bwd")
    dw_in = _dw_in(n, dp, N_CHIPS, 0, 1, None, "dw_in_odd")
    dh, dhb, dnorm = _dn_proj(dp, w_in, 0, h, w["norm"], dh, "dn_proj_odd")
    return dh, dhb, dw_in, dw_out, dict(w_rg=dwr, w_ig=dwi, vec=dvec, norm=dnorm)


ANY = pl.BlockSpec(memory_space=pl.ANY)


def _mesh_pos():
    return lax.axis_index("x"), lax.axis_index("y"), lax.axis_index("c")


def _other_chips(x, y):
    return [(1 - x, y), (x, 1 - y), (1 - x, 1 - y)]


def _other_devices(x, y, c):
    out = []
    for p in range(1, N_DEV):
        out.append((1 - x if p & 4 else x, 1 - y if p & 2 else y, 1 - c if p & 1 else c))
    return out


def _remote(src, dst, ssem, rsem, dev):
    return pltpu.make_async_remote_copy(src_ref=src, dst_ref=dst, send_sem=ssem, recv_sem=rsem, device_id=dev,
                                        device_id_type=MESH)


def _comm_call(body, name, ins, out_shape, scratch, aliases=None):
    return _pallas(body, name=name, in_specs=[ANY] * len(ins), out_specs=[ANY] * len(out_shape), out_shape=out_shape,
                   scratch_shapes=scratch, input_output_aliases=aliases or {},
                   compiler_params=pltpu.CompilerParams(has_side_effects=True))(*ins)


def _cast_shard(w, layer, pos):
    _, R, C = w.shape
    tr = _row_tile(R, C)

    def body(pos_ref, w_ref, o_ref):
        o_ref[...] = w_ref[...].astype(BF16)

    grid_spec = pltpu.PrefetchScalarGridSpec(
        num_scalar_prefetch=1, grid=(R // tr,),
        in_specs=[pl.BlockSpec((None, tr, C), lambda i, pr: (layer, i, 0))],
        out_specs=pl.BlockSpec((None, None, tr, C), lambda i, pr: (0, pr[0], i, 0)))
    return _pallas(body, name="cast_shard", grid_spec=grid_spec,
                   out_shape=jax.ShapeDtypeStruct((1, N_CHIPS, R, C), BF16),
                   compiler_params=_params("parallel"))(pos, w)


def _gather_weights(big, small):
    nA = len(big)
    half = [a.shape[2] // 2 for a in big]

    def body(*refs):
        ins, outs = refs[:nA + 1], refs[nA + 1:2 * nA + 2]
        ssem, rsem, fsem, frsem, lsem = refs[2 * nA + 2:]
        x, y, c = _mesh_pos()
        k = 2 * x + y
        chips = _other_chips(x, y)
        sib = (x, y, 1 - c)

        def slab(a, chip, core):
            return outs[a].at[:, chip, pl.ds(core * half[a], half[a]), :]

        local = [pltpu.make_async_copy(ins[nA], outs[nA].at[k], lsem.at[0])]
        for cp in local:
            cp.start()
        sends = []
        for j, (ox, oy) in enumerate(chips):
            for a in range(nA):
                sends.append(_remote(slab(a, k, c), slab(a, k, c), ssem.at[a, j], rsem.at[a, j], (ox, oy, c)))
            sends.append(_remote(ins[nA], outs[nA].at[k], ssem.at[nA, j], rsem.at[nA, j], (ox, oy, c)))
        for cp in sends:
            cp.start()
        for j, (ox, oy) in enumerate(chips):
            kj = 2 * ox + oy
            for a in range(nA):
                got = slab(a, kj, c)
                _remote(got, got, ssem.at[a, j], rsem.at[a, j], (ox, oy, c)).wait_recv()
                fw = _remote(got, got, fsem.at[a, j], frsem.at[a, j], sib)
                fw.start()
                sends.append(fw)
            gs = outs[nA].at[kj]
            _remote(gs, gs, ssem.at[nA, j], rsem.at[nA, j], (ox, oy, c)).wait_recv()
        for j, (ox, oy) in enumerate(chips):
            kj = 2 * ox + oy
            for a in range(nA):
                theirs = slab(a, kj, 1 - c)
                _remote(theirs, theirs, fsem.at[a, j], frsem.at[a, j], sib).wait_recv()
        for cp in sends:
            cp.wait_send()
        for cp in local:
            cp.wait()

    out_shape = [jax.ShapeDtypeStruct(a.shape, a.dtype) for a in big]
    out_shape.append(jax.ShapeDtypeStruct((N_CHIPS,) + small.shape, small.dtype))
    scratch = [pltpu.SemaphoreType.DMA((nA + 1, 3)), pltpu.SemaphoreType.DMA((nA + 1, 3)),
               pltpu.SemaphoreType.DMA((nA, 3)), pltpu.SemaphoreType.DMA((nA, 3)), pltpu.SemaphoreType.DMA((1,))]
    return _comm_call(body, "gather_weights", list(big) + [small], out_shape, scratch, {a: a for a in range(nA)})


HBM = pl.BlockSpec(memory_space=pltpu.HBM)
SEM = pl.BlockSpec(memory_space=pltpu.SEMAPHORE)
EFFECT = pltpu.SideEffectType.DATAFLOW_SIDE_EFFECTING


def _split_start(arrays, copies, n, name):
    k = len(arrays)

    def body(*refs):
        for cp in copies(refs[k + 2:2 * k + 2], refs[k], refs[k + 1]):
            cp.start()
        refs[2 * k + 2][...] = jnp.zeros((8, 128), F32)

    out = _pallas(
        body, name=name,
        out_shape=(pltpu.SemaphoreType.DMA((n,)), pltpu.SemaphoreType.DMA((n,)),
                   *[pltpu.HBM(a.shape, a.dtype) for a in arrays], jax.ShapeDtypeStruct((8, 128), F32)),
        in_specs=(HBM,) * k, out_specs=(SEM, SEM) + (HBM,) * k + (pl.BlockSpec(memory_space=pltpu.VMEM),),
        input_output_aliases={i: i + 2 for i in range(k)},
        compiler_params=pltpu.CompilerParams(has_side_effects=EFFECT),
    )(*[pltpu.with_memory_space_constraint(a, pltpu.HBM) for a in arrays])
    return out[0], out[1], list(out[2:2 + k]), out[2 + k]


def _split_wait(ssem, rsem, arrays, copies, after, name):
    k = len(arrays)

    def body(*refs):
        for cp in copies(refs[:k], refs[k], refs[k + 1]):
            cp.wait_send()
            cp.wait_recv()

    out = _pallas(
        body, name=name, out_shape=tuple(pltpu.HBM(a.shape, a.dtype) for a in arrays),
        in_specs=(HBM,) * k + (SEM, SEM, ANY), out_specs=(HBM,) * k, input_output_aliases={i: i for i in range(k)},
        compiler_params=pltpu.CompilerParams(has_side_effects=EFFECT),
    )(*arrays, ssem, rsem, after)
    return list(out)


def _gather_copies(shapes):
    half = [s[2] // 2 for s in shapes]

    def copies(refs, ssem, rsem):
        x, y, c = _mesh_pos()
        out = []
        for j, (ox, oy) in enumerate(_other_chips(x, y)):
            for a, ref in enumerate(refs):
                slab = ref.at[:, 2 * x + y, pl.ds(c * half[a], half[a]), :]
                out.append(_remote(slab, slab, ssem.at[3 * a + j], rsem.at[3 * a + j], (ox, oy, c)))
        return out

    return copies


def _chips_copies(n_arr):
    def copies(refs, ssem, rsem):
        x, y, c = _mesh_pos()
        out = []
        for j, (ox, oy) in enumerate(_other_chips(x, y)):
            for a in range(n_arr):
                out.append(_remote(refs[a].at[:, 2 * ox + oy], refs[n_arr + a].at[:, 2 * x + y], ssem.at[3 * a + j],
                                   rsem.at[3 * a + j], (ox, oy, c)))
        return out

    return copies


def _forward_cores(arrays):
    nA = len(arrays)
    half = [a.shape[2] // 2 for a in arrays]

    def body(*refs):
        outs = refs[nA:2 * nA]
        ssem, rsem = refs[2 * nA:]
        x, y, c = _mesh_pos()
        sib = (x, y, 1 - c)
        sends, waits = [], []
        for j, (ox, oy) in enumerate(_other_chips(x, y)):
            for a in range(nA):
                got = outs[a].at[:, 2 * ox + oy, pl.ds(c * half[a], half[a]), :]
                sends.append(_remote(got, got, ssem.at[a, j], rsem.at[a, j], sib))
                theirs = outs[a].at[:, 2 * ox + oy, pl.ds((1 - c) * half[a], half[a]), :]
                waits.append(_remote(theirs, theirs, ssem.at[a, j], rsem.at[a, j], sib))
        for cp in sends:
            cp.start()
        for cp in waits:
            cp.wait_recv()
        for cp in sends:
            cp.wait_send()

    out_shape = [jax.ShapeDtypeStruct(a.shape, a.dtype) for a in arrays]
    scratch = [pltpu.SemaphoreType.DMA((nA, 3)), pltpu.SemaphoreType.DMA((nA, 3))]
    return _comm_call(body, "forward_cores", list(arrays), out_shape, scratch, {a: a for a in range(nA)})


def _exchange_halves(big):
    nA = len(big)
    half = [a.shape[2] // 2 for a in big]

    def body(*refs):
        ins, outs = refs[:nA], refs[nA:2 * nA]
        ssem, rsem = refs[2 * nA:]
        x, y, c = _mesh_pos()
        sib = (x, y, 1 - c)
        sends = [_remote(ins[a].at[:, :, pl.ds((1 - c) * half[a], half[a]), :], outs[a], ssem.at[a], rsem.at[a], sib)
                 for a in range(nA)]
        for cp in sends:
            cp.start()
        for a in range(nA):
            _remote(outs[a], outs[a], ssem.at[a], rsem.at[a], sib).wait_recv()
        for cp in sends:
            cp.wait_send()

    out_shape = [jax.ShapeDtypeStruct((a.shape[0], N_CHIPS, h, a.shape[3]), a.dtype) for a, h in zip(big, half)]
    scratch = [pltpu.SemaphoreType.DMA((nA,)), pltpu.SemaphoreType.DMA((nA,))]
    return _comm_call(body, "exchange_halves", list(big), out_shape, scratch)


def _exchange_cores(big, small):
    nA = len(big)
    half = [a.shape[2] // 2 for a in big]

    def body(*refs):
        ins, outs = refs[:nA + 1], refs[nA + 1:2 * nA + 2]
        ssem, rsem, ssem2, rsem2, lsem = refs[2 * nA + 2:]
        x, y, c = _mesh_pos()
        me = 4 * x + 2 * y + c
        sib = (x, y, 1 - c)
        peers = _other_devices(x, y, c)
        local = pltpu.make_async_copy(ins[nA], outs[nA].at[me], lsem.at[0])
        local.start()
        sends = []
        for a in range(nA):
            src = ins[a].at[:, :, pl.ds((1 - c) * half[a], half[a]), :]
            sends.append(_remote(src, outs[a], ssem.at[a], rsem.at[a], sib))
        for p, dev in enumerate(peers):
            sends.append(_remote(ins[nA], outs[nA].at[me], ssem2.at[p], rsem2.at[p], dev))
        for cp in sends:
            cp.start()
        for a in range(nA):
            _remote(outs[a], outs[a], ssem.at[a], rsem.at[a], sib).wait_recv()
        for p, (px, py, pc) in enumerate(peers):
            got = outs[nA].at[4 * px + 2 * py + pc]
            _remote(got, got, ssem2.at[p], rsem2.at[p], (px, py, pc)).wait_recv()
        for cp in sends:
            cp.wait_send()
        local.wait()

    out_shape = [jax.ShapeDtypeStruct((a.shape[0], N_CHIPS, h, a.shape[3]), a.dtype) for a, h in zip(big, half)]
    out_shape.append(jax.ShapeDtypeStruct((N_DEV,) + small.shape, small.dtype))
    scratch = [pltpu.SemaphoreType.DMA((nA,)), pltpu.SemaphoreType.DMA((nA,)), pltpu.SemaphoreType.DMA((N_DEV - 1,)),
               pltpu.SemaphoreType.DMA((N_DEV - 1,)), pltpu.SemaphoreType.DMA((1,))]
    return _comm_call(body, "exchange_cores", list(big) + [small], out_shape, scratch)


def _exchange_chips(parts):
    nA = len(parts)

    def body(*refs):
        ins, outs = refs[:nA], refs[nA:2 * nA]
        ssem, rsem = refs[2 * nA:]
        x, y, c = _mesh_pos()
        k = 2 * x + y
        chips = _other_chips(x, y)
        sends = []
        for j, (ox, oy) in enumerate(chips):
            for a in range(nA):
                sends.append(_remote(ins[a].at[:, 2 * ox + oy], outs[a].at[:, k], ssem.at[a, j], rsem.at[a, j],
                                     (ox, oy, c)))
        for cp in sends:
            cp.start()
        for j, (ox, oy) in enumerate(chips):
            for a in range(nA):
                got = outs[a].at[:, 2 * ox + oy]
                _remote(got, got, ssem.at[a, j], rsem.at[a, j], (ox, oy, c)).wait_recv()
        for cp in sends:
            cp.wait_send()

    out_shape = [jax.ShapeDtypeStruct(a.shape, a.dtype) for a in parts]
    scratch = [pltpu.SemaphoreType.DMA((nA, 3)), pltpu.SemaphoreType.DMA((nA, 3))]
    return _comm_call(body, "exchange_chips", list(parts), out_shape, scratch)


def _exchange_final(grads, everywhere):
    nA = len(grads)
    n_remote = sum(N_DEV - 1 if ev else 1 for ev in everywhere)

    def body(*refs):
        outs = refs[nA:2 * nA]
        ssem, rsem = refs[2 * nA:]
        x, y, c = _mesh_pos()
        k = 2 * x + y
        sib = (x, y, 1 - c)
        peers = _other_devices(x, y, c)
        sends, waits = [], []
        s = 0
        for a in range(nA):
            if everywhere[a]:
                r2 = grads[a].shape[1] // N_DEV
                mine = outs[a].at[:, pl.ds((2 * k + c) * r2, r2), :]
                for (px, py, pc) in peers:
                    sends.append(_remote(mine, mine, ssem.at[s], rsem.at[s], (px, py, pc)))
                    got = outs[a].at[:, pl.ds((2 * (2 * px + py) + pc) * r2, r2), :]
                    waits.append(_remote(got, got, ssem.at[s], rsem.at[s], (px, py, pc)))
                    s += 1
            else:
                r2 = grads[a].shape[1] // 2
                mine = outs[a].at[:, pl.ds(c * r2, r2), :]
                sends.append(_remote(mine, mine, ssem.at[s], rsem.at[s], sib))
                got = outs[a].at[:, pl.ds((1 - c) * r2, r2), :]
                waits.append(_remote(got, got, ssem.at[s], rsem.at[s], sib))
                s += 1
        for cp in sends:
            cp.start()
        for cp in waits:
            cp.wait_recv()
        for cp in sends:
            cp.wait_send()

    out_shape = [jax.ShapeDtypeStruct(g.shape, g.dtype) for g in grads]
    scratch = [pltpu.SemaphoreType.DMA((n_remote,)), pltpu.SemaphoreType.DMA((n_remote,))]
    return _comm_call(body, "exchange_final", list(grads), out_shape, scratch, {a: a for a in range(nA)})


BLOCK_BYTES = 1 << 20


def _row_tile(rows, cols, mult=16):
    best = mult
    for t in range(mult, rows + 1, mult):
        if rows % t == 0 and t * cols * 4 <= BLOCK_BYTES:
            best = t
    return best


def _add_cores(own, recv, pos):
    L, _, R, C = own.shape
    r2 = R // 2
    tr = _row_tile(r2, C)
    nb = r2 // tr

    def body(pos_ref, a_ref, r_ref, o_ref):
        o_ref[...] = (a_ref[...].astype(F32) + r_ref[...].astype(F32)).astype(BF16)

    blk = (None, None, tr, C)
    grid_spec = pltpu.PrefetchScalarGridSpec(
        num_scalar_prefetch=1, grid=(L, N_CHIPS, nb),
        in_specs=[pl.BlockSpec(blk, lambda l, s, i, pr: (l, s, pr[1] * nb + i, 0)),
                  pl.BlockSpec(blk, lambda l, s, i, pr: (l, s, i, 0))],
        out_specs=pl.BlockSpec(blk, lambda l, s, i, pr: (l, s, i, 0)))
    return _pallas(body, name="add_cores", grid_spec=grid_spec,
                   out_shape=jax.ShapeDtypeStruct((L, N_CHIPS, r2, C), BF16),
                   compiler_params=_params("parallel", "parallel", "parallel"))(pos, own, recv)


def _sum_chips(own, recv, pos, everywhere, layer, nlayers, prev):
    _, _, r2, C = own.shape
    tr = _row_tile(r2, 2 * C)
    nb = r2 // tr

    def body(pos_ref, a_ref, r_ref, *rest):
        acc = None
        for s in range(N_CHIPS):
            term = jnp.where(pos_ref[0] == s, a_ref[...], r_ref[s]).astype(F32)
            acc = term if acc is None else acc + term
        rest[-1][...] = acc

    if everywhere:
        def out_map(i, pr):
            return (layer, (2 * pr[0] + pr[1]) * nb + i, 0)
    else:
        def out_map(i, pr):
            return (layer, pr[1] * nb + i, 0)

    in_specs = [pl.BlockSpec((None, None, tr, C), lambda i, pr: (0, pr[0], i, 0)),
                pl.BlockSpec((None, N_CHIPS, tr, C), lambda i, pr: (0, 0, i, 0))]
    grid_spec = pltpu.PrefetchScalarGridSpec(
        num_scalar_prefetch=1, grid=(nb,), in_specs=in_specs + ([] if prev is None else [ANY]),
        out_specs=pl.BlockSpec((None, tr, C), out_map))
    rows = (N_DEV if everywhere else 2) * r2
    args = (pos, own, recv) if prev is None else (pos, own, recv, prev)
    return _pallas(body, name="sum_chips", grid_spec=grid_spec, out_shape=jax.ShapeDtypeStruct((nlayers, rows, C), F32),
                   input_output_aliases={} if prev is None else {3: 0},
                   compiler_params=_params("parallel"))(*args)


def _sum_devices(parts):
    n, R, C = parts.shape
    tr = _row_tile(R, C * n, 8)

    def body(p_ref, o_ref):
        acc = p_ref[0]
        for s in range(1, n):
            acc = acc + p_ref[s]
        o_ref[...] = acc

    return _pallas(body, name="sum_devices", grid=(R // tr,), in_specs=[pl.BlockSpec((n, tr, C), lambda i: (0, i, 0))],
                   out_specs=pl.BlockSpec((tr, C), lambda i: (i, 0)), out_shape=jax.ShapeDtypeStruct((R, C), F32),
                   compiler_params=_params("parallel"))(parts)


def _adamw(w, g, m, v, name):
    L, R, C = w.shape
    tr = _row_tile(R, C, 8)

    def body(w_ref, g_ref, m_ref, v_ref, d_ref, m2_ref, v2_ref):
        gg = g_ref[...]
        m2 = ADAM_B1 * m_ref[...] + (1.0 - ADAM_B1) * gg
        v2 = ADAM_B2 * v_ref[...] + (1.0 - ADAM_B2) * (gg * gg)
        m_hat = m2 / (1.0 - ADAM_B1 ** ADAM_STEP)
        v_hat = v2 / (1.0 - ADAM_B2 ** ADAM_STEP)
        d_ref[...] = -ADAM_LR * (m_hat / (jnp.sqrt(v_hat) + ADAM_EPS) + ADAM_WD * w_ref[...])
        m2_ref[...] = m2
        v2_ref[...] = v2

    blk = pl.BlockSpec((1, tr, C), lambda l, i: (l, i, 0))
    shp = jax.ShapeDtypeStruct((L, R, C), F32)
    return _pallas(body, name=name, grid=(L, R // tr), in_specs=[blk] * 4, out_specs=[blk] * 3, out_shape=[shp] * 3,
                   compiler_params=_params("parallel", "parallel"))(w, g, m, v)


WEIGHTS = ("norm_even", "w_in_even", "conv_a_w", "conv_a_b", "ln_a_g", "ln_a_b", "pool_w", "pool_b", "pool_scale",
           "w_out_even", "norm_odd", "w_in_odd", "conv_c_w", "conv_c_b", "w_rg", "b_rg", "w_ig", "b_ig", "lru_lambda",
           "w_out_odd", "final_norm")
BIG = ("w_in_even", "w_out_even", "pool_w", "w_in_odd", "w_out_odd", "w_rg", "w_ig")
SMALL = tuple(n for n in WEIGHTS if n not in BIG)
SMALL_SHARDED = ("conv_a_w", "pool_b", "norm_odd", "conv_c_w", "conv_c_b", "b_rg", "b_ig", "lru_lambda")


def _pack(arrs):
    flat = jnp.concatenate([a.reshape(-1) for a in arrs])
    rows = -(-flat.shape[0] // (64 * 128)) * 64
    return jnp.pad(flat, (0, rows * 128 - flat.shape[0])).reshape(rows, 128)


def _unpack(buf, shapes, lead=()):
    flat = buf.reshape(tuple(lead) + (-1,))
    out, o = [], 0
    for s in shapes:
        n = 1
        for d in s:
            n *= d
        out.append(flat[..., o:o + n].reshape(tuple(lead) + tuple(s)))
        o += n
    return out


def _shard(full, axis, k):
    n = full.shape[axis] // N_CHIPS
    return lax.dynamic_slice_in_dim(full, k * n, n, axis)


def kernel(x, norm_even, w_in_even, conv_a_w, conv_a_b, ln_a_g, ln_a_b, pool_w, pool_b, pool_scale, w_out_even, norm_odd, w_in_odd, conv_c_w, conv_c_b, w_rg, b_rg, w_ig, b_ig, lru_lambda, w_out_odd, final_norm, loss_target, m_norm_even, m_w_in_even, m_conv_a_w, m_conv_a_b, m_ln_a_g, m_ln_a_b, m_pool_w, m_pool_b, m_pool_scale, m_w_out_even, m_norm_odd, m_w_in_odd, m_conv_c_w, m_conv_c_b, m_w_rg, m_b_rg, m_w_ig, m_b_ig, m_lru_lambda, m_w_out_odd, m_final_norm, v_norm_even, v_w_in_even, v_conv_a_w, v_conv_a_b, v_ln_a_g, v_ln_a_b, v_pool_w, v_pool_b, v_pool_scale, v_w_out_even, v_norm_odd, v_w_in_odd, v_conv_c_w, v_conv_c_b, v_w_rg, v_b_rg, v_w_ig, v_b_ig, v_lru_lambda, v_w_out_odd, v_final_norm):
    P = dict(locals())
    xi, yi, ci = _mesh_pos()
    k = 2 * xi + yi
    L = w_in_even.shape[0]
    D = D_MODEL

    pos = jnp.stack([k, ci]).astype(jnp.int32)
    depth = 2 * L
    pool_w3 = pool_w.reshape(L, 4 * 64, POOL_GW)

    def cast_group(layer):
        j = layer // 2
        if layer % 2 == 0:
            return [_cast_shard(w_in_even, j, pos), _cast_shard(w_out_even, j, pos), _cast_shard(pool_w3, j, pos)]
        return [_cast_shard(w_in_odd, j, pos), _cast_shard(w_out_odd, j, pos)]

    *group, g_small = _gather_weights(cast_group(0), _pack([P[n] for n in SMALL_SHARDED]))
    full = {}
    for n, a in zip(SMALL_SHARDED, _unpack(g_small, [P[n].shape for n in SMALL_SHARDED], lead=(N_CHIPS,))):
        a = jnp.moveaxis(a, 0, -2)
        full[n] = a.reshape(a.shape[:-2] + (N_CHIPS * a.shape[-1],))

    def small_weights(layer, group):
        j = layer // 2
        if layer % 2 == 0:
            cw = full["conv_a_w"][j]
            pw = group[2].reshape(N_CHIPS, 4, 64, POOL_GW).transpose(1, 0, 2, 3).reshape(4, POOL_GW, POOL_GW)
            return dict(norm=norm_even[j][None], conv_w=_pad_rows(cw, 32), conv_w_rev=_pad_rows(cw[::-1], 32),
                        conv_b=conv_a_b[j][None], ln_g=ln_a_g[j][None], ln_b=ln_a_b[j][None], pool_w=pw,
                        pool_b=full["pool_b"][j].reshape(1, D), pool_scale=pool_scale[j][None])
        return dict(norm=full["norm_odd"][j][None], conv_w=_pad_rows(full["conv_c_w"][j], 8),
                    conv_b=full["conv_c_b"][j][None], w_rg=w_rg[j].astype(BF16), b_rg=full["b_rg"][j][None],
                    w_ig=w_ig[j].astype(BF16), b_ig=full["b_ig"][j][None], lam=full["lru_lambda"][j][None])

    no_token = jnp.zeros((8, 128), F32)
    h = x[0]
    saved, big_w, small_w = [], [], []
    for layer in range(depth):
        token = no_token
        if layer + 1 < depth:
            nxt = cast_group(layer + 1)
            copies = _gather_copies([a.shape for a in nxt])
            ssem, rsem, nxt, token = _split_start(nxt, copies, 3 * len(nxt), "gather_start%d" % (layer + 1))
        small_w.append(small_weights(layer, group))
        big_w.append((group[0], group[1].reshape(1, -1, D)))
        h, sv = _layer_fwd(layer % 2 == 0, h, small_w[layer], *big_w[layer], token)
        saved.append(sv)
        if layer + 1 < depth:
            group = _forward_cores(_split_wait(ssem, rsem, nxt, copies, h, "gather_wait%d" % (layer + 1)))

    dh, dhb, d_final, loss = _loss_head(h, final_norm[None], loss_target[0])
    loss = lax.psum(loss[0, 0], ("x", "y", "c"))
    everywhere = [False, False, False, False, False, True, True]
    final = [None] * len(everywhere)
    small_of = [None] * depth
    recv_small = None

    def finish(pair, land, slots, j):
        for a, r, s in zip(pair, land, slots):
            final[s] = _sum_chips(a, r, pos, everywhere[s], j, L, final[s])

    pending = None
    token = no_token
    for layer in reversed(range(depth)):
        j = layer // 2
        dh, dhb, dw_in, dw_out, sm = _layer_bwd(layer % 2 == 0, saved[layer], small_w[layer], *big_w[layer], dh, dhb,
                                                token)
        small_of[layer] = sm
        if pending is not None:
            ssem, rsem, arrs, copies, slots, pj, pl_ = pending
            arrs = _split_wait(ssem, rsem, arrs, copies, dh, "chips_wait%d" % pl_)
            finish(arrs[:len(slots)], arrs[len(slots):], slots, pj)
            pending = None
        if layer % 2 == 0:
            dpw = sm["pool_w"].reshape(4, N_CHIPS, 64, POOL_GW).transpose(1, 0, 2, 3)
            parts = [dw_in, dw_out.reshape(1, N_CHIPS, -1, D), dpw.reshape(1, N_CHIPS, 4 * 64, POOL_GW).astype(BF16)]
            slots = [0, 1, 2]
        else:
            parts = [dw_in, dw_out.reshape(1, N_CHIPS, -1, D),
                     sm["w_rg"].reshape(1, N_CHIPS, -1, LRU_HD).astype(BF16),
                     sm["w_ig"].reshape(1, N_CHIPS, -1, LRU_HD).astype(BF16)]
            slots = [3, 4, 5, 6]
        if layer > 0:
            recv = _exchange_halves(parts)
        else:
            small_g = []
            for jj in range(L):
                ge, go = small_of[2 * jj], small_of[2 * jj + 1]
                small_g += [ge["conv_w"].reshape(32, 8, D).sum(axis=1)[:CONV_K], ge["vec"][0:5], ge["norm"], go["vec"],
                            go["norm"]]
            small_g.append(d_final)
            small_shapes = [a.shape for a in small_g]
            *recv, recv_small = _exchange_cores(parts, _pack(small_g))
        pair = [_add_cores(a, r, pos) for a, r in zip(parts, recv)]
        if layer > 0:
            copies = _chips_copies(len(pair))
            land = [lax.empty(a.shape, a.dtype) for a in pair]
            ssem, rsem, arrs, token = _split_start(pair + land, copies, 3 * len(pair), "chips_start%d" % layer)
            pending = (ssem, rsem, arrs, copies, slots, j, layer)
        else:
            finish(pair, _exchange_chips(pair), slots, j)
    grad_x = dh
    gw = _exchange_final(final, everywhere)
    sg = _unpack(_sum_devices(recv_small), small_shapes)

    grads = dict(w_in_even=gw[0], w_out_even=gw[1], pool_w=gw[2].reshape(pool_w.shape), w_in_odd=gw[3], w_out_odd=gw[4],
                 w_rg=gw[5].reshape(w_rg.shape), w_ig=gw[6].reshape(w_ig.shape), final_norm=sg[-1][0])
    ev = [sg[5 * j + 1] for j in range(L)]
    ov = [sg[5 * j + 3] for j in range(L)]
    grads["conv_a_w"] = _shard(jnp.stack([sg[5 * j] for j in range(L)]), 2, k)
    grads["norm_even"] = jnp.stack([sg[5 * j + 2][0] for j in range(L)])
    grads["norm_odd"] = _shard(jnp.stack([sg[5 * j + 4][0] for j in range(L)]), 1, k)
    for r, n in enumerate(("conv_a_b", "ln_a_g", "ln_a_b", "pool_scale")):
        grads[n] = jnp.stack([e[r] for e in ev])
    grads["pool_b"] = _shard(jnp.stack([e[4].reshape(4, POOL_GW) for e in ev]), 2, k)
    grads["conv_c_w"] = _shard(jnp.stack([o[0:4] for o in ov]), 2, k)
    for r, n in zip((4, 5, 6, 7), ("conv_c_b", "b_rg", "b_ig", "lru_lambda")):
        grads[n] = _shard(jnp.stack([o[r] for o in ov]), 1, k)

    delta, new_m, new_v = {}, {}, {}
    for n in BIG:
        s3 = (L, -1, P[n].shape[-1])
        d, m2, v2 = _adamw(P[n].reshape(s3), grads[n].reshape(s3), P["m_" + n].reshape(s3), P["v_" + n].reshape(s3), "adamw")
        delta[n], new_m[n], new_v[n] = d.reshape(P[n].shape), m2.reshape(P[n].shape), v2.reshape(P[n].shape)
    shapes = [P[n].shape for n in SMALL]
    packed = [_pack([src[n] for n in SMALL])[None] for src in
              (P, grads, {n: P["m_" + n] for n in SMALL}, {n: P["v_" + n] for n in SMALL})]
    for res, out in zip(_adamw(*packed, "adamw_small"), (delta, new_m, new_v)):
        for n, a in zip(SMALL, _unpack(res[0], shapes)):
            out[n] = a

    return (loss, grad_x[None], *[grads[n] for n in WEIGHTS], *[delta[n] for n in WEIGHTS],
            *[new_m[n] for n in WEIGHTS], *[new_v[n] for n in WEIGHTS])
```

```python
import functools

import jax
import jax.numpy as jnp
from jax import lax
from jax.experimental import pallas as pl
from jax.experimental.pallas import tpu as pltpu

F32 = jnp.float32
BF16 = jnp.bfloat16
MESH = pl.DeviceIdType.MESH

D_MODEL = 1024
N_CHIPS = 4
N_DEV = 8
EPS_RMS = 1e-6
EPS_LN = 1e-5
CONV_K = 31
POOL_WINDOWS = (2, 4, 8, 16)
POOL_GW = 256
LRU_HEADS = 12
LRU_HD = 128
W_LRU = LRU_HEADS * LRU_HD
LRU_CONV_K = 4
LRU_C = 8.0
ADAM_LR = 0.001
ADAM_B1 = 0.9
ADAM_B2 = 0.999
ADAM_EPS = 1e-08
ADAM_WD = 0.01
ADAM_STEP = 10

VMEM_LIMIT_BYTES = 56 * 1024 * 1024
ROW_TILE = 512
MIX_TILE = 256
EVEN_HALO = 32
ODD_HALO = 8


def _pallas(body, **kw):
    return pl.pallas_call(body, **kw)


def _params(*sem):
    return pltpu.CompilerParams(dimension_semantics=sem if sem else None, vmem_limit_bytes=VMEM_LIMIT_BYTES)


def _sigmoid(x):
    return 1.0 / (1.0 + jnp.exp(-x))


def _dsilu(x, s):
    return s * (1.0 + x * (1.0 - s))


def _nt(a, b):
    return lax.dot_general(a, b, (((1,), (1,)), ((), ())), preferred_element_type=F32)


def _tn(a, b):
    return lax.dot_general(a, b, (((0,), (0,)), ((), ())), preferred_element_type=F32)


def _in_proj(h, g, wg, layer, after, name):
    T, D = h.shape
    _, nblk, _, nb = wg.shape

    nrow = T // ROW_TILE

    def body(h_ref, g_ref, w_ref, after_ref, p_ref, n_ref, n_all):
        j, i = pl.program_id(0), pl.program_id(1)

        @pl.when(j == 0)
        def _():
            x = h_ref[...]
            r = lax.rsqrt(jnp.mean(x * x, axis=-1, keepdims=True) + EPS_RMS)
            nn = (x * r * g_ref[...]).astype(BF16)
            n_ref[...] = nn
            n_all[i] = nn

        p_ref[...] = jnp.dot(n_all[i], w_ref[0], preferred_element_type=F32)

    def rows_once(j, i):
        return (jnp.where(j == 0, i, nrow - 1), 0)

    return _pallas(
        body, name=name, grid=(nblk, nrow),
        in_specs=[pl.BlockSpec((ROW_TILE, D), rows_once), pl.BlockSpec((1, D), lambda j, i: (0, 0)),
                  pl.BlockSpec((None, 1, D, nb), lambda j, i: (layer, j, 0, 0)),
                  pl.BlockSpec((8, 128), lambda j, i: (0, 0))],
        out_specs=[pl.BlockSpec((ROW_TILE, nb), lambda j, i: (i, j)), pl.BlockSpec((ROW_TILE, D), rows_once)],
        out_shape=[jax.ShapeDtypeStruct((T, nblk * nb), F32), jax.ShapeDtypeStruct((T, D), BF16)],
        scratch_shapes=[pltpu.VMEM((nrow, ROW_TILE, D), BF16)],
        compiler_params=_params("arbitrary", "arbitrary"))(h, g, wg, after)


def _out_proj(y, w, layer, hres, name):
    T, K = y.shape
    D = w.shape[2]

    def body(y_ref, w_ref, r_ref, o_ref):
        o_ref[...] = r_ref[...] + jnp.dot(y_ref[...], w_ref[...], preferred_element_type=F32)

    return _pallas(
        body, name=name, grid=(T // ROW_TILE,),
        in_specs=[pl.BlockSpec((ROW_TILE, K), lambda i: (i, 0)), pl.BlockSpec((None, K, D), lambda i: (layer, 0, 0)),
                  pl.BlockSpec((ROW_TILE, D), lambda i: (i, 0))],
        out_specs=pl.BlockSpec((ROW_TILE, D), lambda i: (i, 0)),
        out_shape=jax.ShapeDtypeStruct((T, D), F32),
        compiler_params=_params("parallel"))(y, w, hres)


def _dy_proj(dout, w, layer, after, name):
    T, D = dout.shape
    K = w.shape[1]

    def body(d_ref, w_ref, after_ref, o_ref):
        o_ref[...] = _nt(d_ref[...], w_ref[...])

    return _pallas(
        body, name=name, grid=(T // ROW_TILE,),
        in_specs=[pl.BlockSpec((ROW_TILE, D), lambda i: (i, 0)), pl.BlockSpec((None, K, D), lambda i: (layer, 0, 0)),
                  pl.BlockSpec((8, 128), lambda i: (0, 0))],
        out_specs=pl.BlockSpec((ROW_TILE, K), lambda i: (i, 0)),
        out_shape=jax.ShapeDtypeStruct((T, K), F32),
        compiler_params=_params("parallel"))(dout, w, after)


def _dn_proj(dp, wg, layer, h, g, dres, name):
    T, D = h.shape
    _, nblk, _, nb = wg.shape

    nrow = T // ROW_TILE

    def body(dp_ref, w_ref, h_ref, g_ref, dres_ref, dh_ref, dhb_ref, dg_ref, acc_ref):
        j, i = pl.program_id(0), pl.program_id(1)
        part = _nt(dp_ref[...], w_ref[0])

        @pl.when(j == 0)
        def _():
            acc_ref[i] = part

        @pl.when(j > 0)
        def _():
            acc_ref[i] += part

        @pl.when(j == nblk - 1)
        def _():
            x = h_ref[...]
            r = lax.rsqrt(jnp.mean(x * x, axis=-1, keepdims=True) + EPS_RMS)
            dn = acc_ref[i]
            q = dn * g_ref[...]
            dh = dres_ref[...] + r * q - x * ((r * r * r) * jnp.mean(q * x, axis=-1, keepdims=True))
            dh_ref[...] = dh
            dhb_ref[...] = dh.astype(BF16)
            dgp = jnp.sum(dn * (x * r), axis=0, keepdims=True)

            @pl.when(i == 0)
            def _():
                dg_ref[...] = dgp

            @pl.when(i > 0)
            def _():
                dg_ref[...] += dgp

    def rows_last(j, i):
        return (jnp.where(j == nblk - 1, i, 0), 0)

    return _pallas(
        body, name=name, grid=(nblk, nrow),
        in_specs=[pl.BlockSpec((ROW_TILE, nb), lambda j, i: (i, j)),
                  pl.BlockSpec((None, 1, D, nb), lambda j, i: (layer, j, 0, 0)),
                  pl.BlockSpec((ROW_TILE, D), rows_last), pl.BlockSpec((1, D), lambda j, i: (0, 0)),
                  pl.BlockSpec((ROW_TILE, D), rows_last)],
        out_specs=[pl.BlockSpec((ROW_TILE, D), rows_last), pl.BlockSpec((ROW_TILE, D), rows_last),
                   pl.BlockSpec((1, D), lambda j, i: (0, 0))],
        out_shape=[jax.ShapeDtypeStruct((T, D), F32), jax.ShapeDtypeStruct((T, D), BF16),
                   jax.ShapeDtypeStruct((1, D), F32)],
        scratch_shapes=[pltpu.VMEM((nrow, ROW_TILE, D), F32)],
        compiler_params=_params("arbitrary", "arbitrary"))(dp, wg, h, g, dres)


def _dw_in(n, dp, nblk, layer, nlayers, prev, name):
    T, D = n.shape
    nb = dp.shape[1] // nblk
    ta = 512

    def body(n_ref, dp_ref, *rest):
        rest[-1][0] = _tn(n_ref[...], dp_ref[...]).astype(BF16)

    in_specs = [pl.BlockSpec((T, ta), lambda j, i: (0, i)), pl.BlockSpec((T, nb), lambda j, i: (0, j))]
    args = (n, dp) if prev is None else (n, dp, prev)
    return _pallas(
        body, name=name, grid=(nblk, D // ta), in_specs=in_specs + ([] if prev is None else [ANY]),
        out_specs=pl.BlockSpec((None, 1, ta, nb), lambda j, i: (layer, j, i, 0)),
        out_shape=jax.ShapeDtypeStruct((nlayers, nblk, D, nb), BF16),
        input_output_aliases={} if prev is None else {2: 0},
        compiler_params=_params("parallel", "parallel"))(*args)


def _dw_out(y, dout, layer, nlayers, prev, name):
    T, K = y.shape
    D = dout.shape[1]
    tk = 512

    def body(y_ref, d_ref, *rest):
        rest[-1][...] = _tn(y_ref[...], d_ref[...]).astype(BF16)

    in_specs = [pl.BlockSpec((T, tk), lambda i: (0, i)), pl.BlockSpec((T, D), lambda i: (0, 0))]
    args = (y, dout) if prev is None else (y, dout, prev)
    return _pallas(
        body, name=name, grid=(K // tk,), in_specs=in_specs + ([] if prev is None else [ANY]),
        out_specs=pl.BlockSpec((None, tk, D), lambda i: (layer, i, 0)),
        out_shape=jax.ShapeDtypeStruct((nlayers, K, D), BF16),
        input_output_aliases={} if prev is None else {2: 0},
        compiler_params=_params("parallel"))(*args)


def _loss_head(h, g, tgt):
    T, D = h.shape
    tm = MIX_TILE

    def body(h_ref, g_ref, t_ref, dh_ref, dhb_ref, dg_ref, loss_ref):
        i = pl.program_id(0)
        x = h_ref[...]
        gg = g_ref[...]
        r = lax.rsqrt(jnp.mean(x * x, axis=-1, keepdims=True) + EPS_RMS)
        xr = x * r
        e = xr * gg - t_ref[...]
        lp = 0.5 * jnp.sum(jnp.mean(e * e, axis=-1, keepdims=True), axis=0, keepdims=True)
        dn = e * (1.0 / D)
        q = dn * gg
        dh = r * q - x * ((r * r * r) * jnp.mean(q * x, axis=-1, keepdims=True))
        dh_ref[...] = dh
        dhb_ref[...] = dh.astype(BF16)
        dgp = jnp.sum(dn * xr, axis=0, keepdims=True)

        @pl.when(i == 0)
        def _():
            dg_ref[...] = dgp
            loss_ref[...] = lp

        @pl.when(i > 0)
        def _():
            dg_ref[...] += dgp
            loss_ref[...] += lp

    return _pallas(
        body, name="loss_head", grid=(T // tm,),
        in_specs=[pl.BlockSpec((tm, D), lambda i: (i, 0)), pl.BlockSpec((1, D), lambda i: (0, 0)),
                  pl.BlockSpec((tm, D), lambda i: (i, 0))],
        out_specs=[pl.BlockSpec((tm, D), lambda i: (i, 0)), pl.BlockSpec((tm, D), lambda i: (i, 0)),
                   pl.BlockSpec((1, D), lambda i: (0, 0)), pl.BlockSpec((1, 1), lambda i: (0, 0))],
        out_shape=[jax.ShapeDtypeStruct((T, D), F32), jax.ShapeDtypeStruct((T, D), BF16),
                   jax.ShapeDtypeStruct((1, D), F32), jax.ShapeDtypeStruct((1, 1), F32)],
        compiler_params=_params("arbitrary"))(h, g, tgt)


def _shift_up(x, j):
    return x if j == 0 else pltpu.roll(x, x.shape[0] - j, 0)


def _shift_down(x, j):
    return x if j == 0 else pltpu.roll(x, j, 0)


def _fill_shifted(dst_ref, x):
    rows = dst_ref.shape[1]
    for s in range(8):
        dst_ref[s] = _shift_up(x, s)[0:rows]


def _tap_sum(sh_ref, w_ref, r0, nrows, offsets):
    acc = None
    for k, o in enumerate(offsets):
        win = sh_ref[o % 8, pl.ds(r0 + (o // 8) * 8, nrows), :]
        term = w_ref[k:k + 1, :] * win
        acc = term if acc is None else acc + term
    return acc


def _pool_sums(vx, up):
    sh = _shift_up if up else _shift_down
    outs = []
    for gi, w in enumerate(POOL_WINDOWS):
        s = vx[:, gi * POOL_GW:(gi + 1) * POOL_GW]
        j = 1
        while j < w:
            s = s + sh(s, j)
            j *= 2
        outs.append(s)
    return outs


def _inv_count(row0, nrows):
    pos = (row0 + 1 + lax.broadcasted_iota(jnp.int32, (nrows, 1), 0)).astype(F32)
    return [1.0 / jnp.minimum(pos, float(w)) for w in POOL_WINDOWS]


def _even_mixer_fwd(p, cw, cb, lg, lb, pw, pb, sc, name):
    T = p.shape[0]
    C = D_MODEL
    tT, HL = MIX_TILE, EVEN_HALO
    hb = tT // HL
    chunk = 16

    def body(pm_ref, ph_ref, cw_ref, cb_ref, lg_ref, lb_ref, pw_ref, pb_ref, sc_ref, y_ref, u1_ref, u0x_ref, sh_ref):
        i = pl.program_id(0)
        keep = (i > 0).astype(F32)
        u0x_ref[0:HL] = ph_ref[:, 0:C] * _sigmoid(ph_ref[:, C:2 * C]) * keep
        u0x_ref[HL:HL + tT] = pm_ref[:, 0:C] * _sigmoid(pm_ref[:, C:2 * C])
        u0x_ref[HL + tT:HL + tT + 8] = jnp.zeros((8, C), F32)
        _fill_shifted(sh_ref, u0x_ref[...])
        offs = [HL - (CONV_K - 1) + k for k in range(CONV_K)]

        def conv_chunk(c, carry):
            r0 = pl.multiple_of(c * chunk, chunk)
            u1_ref[pl.ds(r0, chunk), :] = _tap_sum(sh_ref, cw_ref, r0, chunk, offs) + cb_ref[...]
            return carry

        lax.fori_loop(0, tT // chunk, conv_chunk, 0)
        u1 = u1_ref[...]
        mu = jnp.mean(u1, axis=-1, keepdims=True)
        xc = u1 - mu
        rs = lax.rsqrt(jnp.mean(xc * xc, axis=-1, keepdims=True) + EPS_LN)
        u2 = xc * rs * lg_ref[...] + lb_ref[...]
        u3 = u2 * _sigmoid(u2)
        ag = pm_ref[:, 2 * C:3 * C]
        y_ref[:, 0:C] = (u3 * (ag * _sigmoid(ag))).astype(BF16)
        vx = jnp.concatenate([ph_ref[:, 3 * C:4 * C] * keep, pm_ref[:, 3 * C:4 * C]], axis=0)
        sums = _pool_sums(vx, up=False)
        inv = _inv_count(i * tT, tT)
        for gi in range(len(POOL_WINDOWS)):
            cols = slice(gi * POOL_GW, (gi + 1) * POOL_GW)
            d0 = sums[gi][HL:] * inv[gi] - vx[HL:, cols]
            d1 = jnp.dot(d0.astype(BF16), pw_ref[gi], preferred_element_type=F32) + pb_ref[:, cols]
            bg = pm_ref[:, 4 * C + gi * POOL_GW:4 * C + (gi + 1) * POOL_GW]
            y_ref[:, C + gi * POOL_GW:C + (gi + 1) * POOL_GW] = (d1 * sc_ref[:, cols] * (bg * _sigmoid(bg))).astype(BF16)

    vec = pl.BlockSpec((1, C), lambda i: (0, 0))
    return _pallas(
        body, name=name, grid=(T // tT,),
        in_specs=[pl.BlockSpec((tT, 5 * C), lambda i: (i, 0)),
                  pl.BlockSpec((HL, 5 * C), lambda i: (jnp.maximum(i * hb - 1, 0), 0)),
                  pl.BlockSpec((32, C), lambda i: (0, 0)), vec, vec, vec,
                  pl.BlockSpec((4, POOL_GW, POOL_GW), lambda i: (0, 0, 0)), vec, vec],
        out_specs=[pl.BlockSpec((tT, 2 * C), lambda i: (i, 0)), pl.BlockSpec((tT, C), lambda i: (i, 0))],
        out_shape=[jax.ShapeDtypeStruct((T, 2 * C), BF16), jax.ShapeDtypeStruct((T, C), F32)],
        scratch_shapes=[pltpu.VMEM((HL + tT + 8, C), F32), pltpu.VMEM((8, HL + tT, C), F32)],
        compiler_params=_params("parallel"))(p, p, cw, cb, lg, lb, pw, pb, sc)


def _even_mixer_bwd(p, u1, dy, cw, cwr, lg, lb, pw, pb, sc, name):
    T = p.shape[0]
    C = D_MODEL
    tT, HL = MIX_TILE, EVEN_HALO
    hb = tT // HL
    nT = T // tT
    R1 = tT + HL
    chunk = 16

    def body(pm_ref, pp_ref, pn_ref, u1m_ref, u1n_ref, dym_ref, dyn_ref, cw_ref, cwr_ref, lg_ref, lb_ref, pw_ref,
             pb_ref, sc_ref, dp_ref, dcw_ref, dvec_ref, dpw_ref, x_ref, sh_ref, du0_ref):
        i = pl.program_id(0)
        keep_prev = (i > 0).astype(F32)
        keep_next = (i < nT - 1).astype(F32)
        row = lax.broadcasted_iota(jnp.int32, (R1, 1), 0)
        live = jnp.where(row < tT, 1.0, keep_next)

        def cat(m, n):
            return jnp.concatenate([m, n], axis=0)

        u1 = cat(u1m_ref[...], u1n_ref[...])
        mu = jnp.mean(u1, axis=-1, keepdims=True)
        xc = u1 - mu
        rs = lax.rsqrt(jnp.mean(xc * xc, axis=-1, keepdims=True) + EPS_LN)
        xh = xc * rs
        u2 = xh * lg_ref[...] + lb_ref[...]
        s2 = _sigmoid(u2)
        u3 = u2 * s2
        ag = cat(pm_ref[:, 2 * C:3 * C], pn_ref[:, 2 * C:3 * C])
        sa = _sigmoid(ag)
        dya = cat(dym_ref[:, 0:C], dyn_ref[:, 0:C])
        dp_ref[:, 2 * C:3 * C] = (dya * u3 * _dsilu(ag, sa))[0:tT].astype(BF16)
        du2 = dya * (ag * sa) * _dsilu(u2, s2)
        dlg = jnp.sum((du2 * xh)[0:tT], axis=0, keepdims=True)
        dlb = jnp.sum(du2[0:tT], axis=0, keepdims=True)
        dxh = du2 * lg_ref[...]
        du1 = rs * (dxh - jnp.mean(dxh, axis=-1, keepdims=True) - xh * jnp.mean(dxh * xh, axis=-1, keepdims=True))
        du1 = du1 * live
        dcb = jnp.sum(du1[0:tT], axis=0, keepdims=True)
        x_ref[0:R1] = du1
        x_ref[R1:R1 + 8] = jnp.zeros((8, C), F32)
        _fill_shifted(sh_ref, x_ref[...])

        def du0_chunk(c, carry):
            r0 = pl.multiple_of(c * chunk, chunk)
            du0_ref[pl.ds(r0, chunk), :] = _tap_sum(sh_ref, cwr_ref, r0, chunk, list(range(CONV_K)))
            return carry

        lax.fori_loop(0, tT // chunk, du0_chunk, 0)
        av, agl = pm_ref[:, 0:C], pm_ref[:, C:2 * C]
        sg = _sigmoid(agl)
        du0 = du0_ref[...]
        dp_ref[:, 0:C] = (du0 * sg).astype(BF16)
        dp_ref[:, C:2 * C] = (du0 * av * sg * (1.0 - sg)).astype(BF16)
        du0_ref[...] = du1[0:tT]
        x_ref[0:HL] = pp_ref[:, 0:C] * _sigmoid(pp_ref[:, C:2 * C]) * keep_prev
        x_ref[HL:HL + tT] = av * sg
        x_ref[HL + tT:HL + tT + 8] = jnp.zeros((8, C), F32)
        _fill_shifted(sh_ref, x_ref[...])

        @pl.when(i == 0)
        def _():
            dcw_ref[...] = jnp.zeros_like(dcw_ref)

        for k in range(CONV_K):
            o = HL - (CONV_K - 1) + k

            def dw_chunk(c, acc, o=o):
                r0 = pl.multiple_of(c * 64, 64)
                for u in range(0, 64, 8):
                    acc = acc + du0_ref[pl.ds(r0 + u, 8), :] * sh_ref[o % 8, pl.ds(r0 + u + (o // 8) * 8, 8), :]
                return acc

            dcw_ref[8 * k:8 * k + 8, :] += lax.fori_loop(0, tT // 64, dw_chunk, jnp.zeros((8, C), F32))

        bg = cat(pm_ref[:, 4 * C:5 * C], pn_ref[:, 4 * C:5 * C])
        sb = _sigmoid(bg)
        dyb = cat(dym_ref[:, C:2 * C], dyn_ref[:, C:2 * C])
        dyb0 = dyb * (bg * sb)
        dd1 = dyb0 * sc_ref[...]
        dpb = jnp.sum(dd1[0:tT], axis=0, keepdims=True)
        inv1 = _inv_count(i * tT, R1)
        z_parts, dd0_parts = [], []
        for gi in range(len(POOL_WINDOWS)):
            cols = slice(gi * POOL_GW, (gi + 1) * POOL_GW)
            dd0 = _nt(dd1[:, cols].astype(BF16), pw_ref[gi])
            dd0_parts.append(dd0)
            z_parts.append(dd0 * inv1[gi] * live)
        fsum = _pool_sums(jnp.concatenate(z_parts, axis=1), up=True)
        vx = cat(pp_ref[:, 3 * C:4 * C] * keep_prev, pm_ref[:, 3 * C:4 * C])
        sums = _pool_sums(vx, up=False)
        inv0 = _inv_count(i * tT, tT)
        dsc_parts = []
        for gi in range(len(POOL_WINDOWS)):
            cols = slice(gi * POOL_GW, (gi + 1) * POOL_GW)
            dp_ref[:, 3 * C + gi * POOL_GW:3 * C + (gi + 1) * POOL_GW] = (fsum[gi][0:tT] - dd0_parts[gi][0:tT]).astype(BF16)
            d0 = (sums[gi][HL:] * inv0[gi] - vx[HL:, cols]).astype(BF16)
            d1 = jnp.dot(d0, pw_ref[gi], preferred_element_type=F32) + pb_ref[:, cols]
            bgm, sbm = bg[0:tT, cols], sb[0:tT, cols]
            dp_ref[:, 4 * C + gi * POOL_GW:4 * C + (gi + 1) * POOL_GW] = (
                dyb[0:tT, cols] * d1 * sc_ref[:, cols] * _dsilu(bgm, sbm)).astype(BF16)
            dsc_parts.append(jnp.sum(dyb0[0:tT, cols] * d1, axis=0, keepdims=True))
            dpw_g = _tn(d0, dd1[0:tT, cols].astype(BF16))

            @pl.when(i == 0)
            def _(gi=gi, dpw_g=dpw_g):
                dpw_ref[gi] = dpw_g

            @pl.when(i > 0)
            def _(gi=gi, dpw_g=dpw_g):
                dpw_ref[gi] += dpw_g

        dsc = jnp.concatenate(dsc_parts, axis=1)
        vecs = jnp.concatenate([dcb, dlg, dlb, dsc, dpb, jnp.zeros((3, C), F32)], axis=0)

        @pl.when(i == 0)
        def _():
            dvec_ref[...] = vecs

        @pl.when(i > 0)
        def _():
            dvec_ref[...] += vecs

    vec = pl.BlockSpec((1, C), lambda i: (0, 0))
    taps = pl.BlockSpec((32, C), lambda i: (0, 0))

    def prev_blk(i):
        return (jnp.maximum(i * hb - 1, 0), 0)

    def next_blk(i):
        return (jnp.minimum((i + 1) * hb, T // HL - 1), 0)

    return _pallas(
        body, name=name, grid=(nT,),
        in_specs=[pl.BlockSpec((tT, 5 * C), lambda i: (i, 0)), pl.BlockSpec((HL, 5 * C), prev_blk),
                  pl.BlockSpec((HL, 5 * C), next_blk),
                  pl.BlockSpec((tT, C), lambda i: (i, 0)), pl.BlockSpec((HL, C), next_blk),
                  pl.BlockSpec((tT, 2 * C), lambda i: (i, 0)), pl.BlockSpec((HL, 2 * C), next_blk),
                  taps, taps, vec, vec, pl.BlockSpec((4, POOL_GW, POOL_GW), lambda i: (0, 0, 0)), vec, vec],
        out_specs=[pl.BlockSpec((tT, 5 * C), lambda i: (i, 0)), pl.BlockSpec((32 * 8, C), lambda i: (0, 0)),
                   pl.BlockSpec((8, C), lambda i: (0, 0)), pl.BlockSpec((4, POOL_GW, POOL_GW), lambda i: (0, 0, 0))],
        out_shape=[jax.ShapeDtypeStruct((T, 5 * C), BF16), jax.ShapeDtypeStruct((32 * 8, C), F32),
                   jax.ShapeDtypeStruct((8, C), F32), jax.ShapeDtypeStruct((4, POOL_GW, POOL_GW), F32)],
        scratch_shapes=[pltpu.VMEM((R1 + 8, C), F32), pltpu.VMEM((8, R1, C), F32), pltpu.VMEM((tT, C), F32)],
        compiler_params=_params("arbitrary"))(p, p, p, u1, u1, dy, dy, cw, cwr, lg, lb, pw, pb, sc)


def _softplus(z):
    u = jnp.exp(-jnp.abs(z))
    w = 1.0 + u
    l1p = jnp.where(w == 1.0, u, u * jnp.log(w) / jnp.where(w == 1.0, 1.0, w - 1.0))
    return jnp.maximum(z, 0.0) + l1p


def _lru_gates(xrx, cw_ref, cb_ref, wr_ref, br_ref, wi_ref, bi_ref, lam_ref):
    HL = ODD_HALO
    xc = cb_ref[...] + cw_ref[LRU_CONV_K - 1:LRU_CONV_K, :] * xrx[HL:]
    for k in range(LRU_CONV_K - 1):
        xc = xc + cw_ref[k:k + 1, :] * _shift_down(xrx, LRU_CONV_K - 1 - k)[HL:]
    xcb = xc.astype(BF16)
    rp, ip = [], []
    for hd in range(LRU_HEADS):
        cols = slice(hd * LRU_HD, (hd + 1) * LRU_HD)
        rp.append(jnp.dot(xcb[:, cols], wr_ref[hd], preferred_element_type=F32))
        ip.append(jnp.dot(xcb[:, cols], wi_ref[hd], preferred_element_type=F32))
    r = _sigmoid(jnp.concatenate(rp, axis=1) + br_ref[...])
    ig = _sigmoid(jnp.concatenate(ip, axis=1) + bi_ref[...])
    sp = _softplus(-lam_ref[...])
    log_a = (-LRU_C) * r * sp
    a = jnp.exp(log_a)
    mult = jnp.sqrt(-jnp.tanh(log_a) * (a * a + 1.0))
    return xc, xcb, r, ig, sp, a, mult, 1.0 / mult


def _odd_mixer_fwd(p, cw, cb, wr, br, wi, bi, lam, name):
    T = p.shape[0]
    W = W_LRU
    tT, HL = MIX_TILE, ODD_HALO
    hb = tT // HL

    def body(pm_ref, ph_ref, cw_ref, cb_ref, wr_ref, br_ref, wi_ref, bi_ref, lam_ref, y_ref, hs_ref, carry_ref):
        i = pl.program_id(0)
        keep = (i > 0).astype(F32)

        @pl.when(i == 0)
        def _():
            carry_ref[...] = jnp.zeros_like(carry_ref)

        xrx = jnp.concatenate([ph_ref[:, 0:W] * keep, pm_ref[:, 0:W]], axis=0)
        xc, _, _, ig, _, a, mult, _ = _lru_gates(xrx, cw_ref, cb_ref, wr_ref, br_ref, wi_ref, bi_ref, lam_ref)
        b = mult * (ig * xc)
        row = lax.broadcasted_iota(jnp.int32, (tT, 1), 0)
        s = 1
        while s < tT:
            ok = row >= s
            a_sh = jnp.where(ok, _shift_down(a, s), 1.0)
            b_sh = jnp.where(ok, _shift_down(b, s), 0.0)
            b = a * b_sh + b
            a = a * a_sh
            s *= 2
        hs = a * carry_ref[0:1, :] + b
        hs_ref[...] = hs
        carry_ref[...] = jnp.broadcast_to(hs[tT - 1:tT, :], (8, W))
        gt = pm_ref[:, W:2 * W]
        y_ref[...] = (hs * (gt * _sigmoid(gt))).astype(BF16)

    vec = pl.BlockSpec((1, W), lambda i: (0, 0))
    heads = pl.BlockSpec((LRU_HEADS, LRU_HD, LRU_HD), lambda i: (0, 0, 0))
    return _pallas(
        body, name=name, grid=(T // tT,),
        in_specs=[pl.BlockSpec((tT, 2 * W), lambda i: (i, 0)),
                  pl.BlockSpec((HL, 2 * W), lambda i: (jnp.maximum(i * hb - 1, 0), 0)),
                  pl.BlockSpec((8, W), lambda i: (0, 0)), vec, heads, vec, heads, vec, vec],
        out_specs=[pl.BlockSpec((tT, W), lambda i: (i, 0)), pl.BlockSpec((tT, W), lambda i: (i, 0))],
        out_shape=[jax.ShapeDtypeStruct((T, W), BF16), jax.ShapeDtypeStruct((T, W), F32)],
        scratch_shapes=[pltpu.VMEM((8, W), F32)],
        compiler_params=_params("arbitrary"))(p, p, cw, cb, wr, br, wi, bi, lam)


def _odd_mixer_bwd(p, hs, dy, cw, cb, wr, br, wi, bi, lam, name):
    T = p.shape[0]
    W = W_LRU
    tT, HL = MIX_TILE, ODD_HALO
    hb = tT // HL
    nT = T // tT

    def body(pm_ref, ph_ref, hsm_ref, hsh_ref, dy_ref, cw_ref, cb_ref, wr_ref, br_ref, wi_ref, bi_ref, lam_ref,
             dp_ref, dwr_ref, dwi_ref, dvec_ref, gcarry_ref, xcarry_ref):
        i = pl.program_id(0)
        keep = (i < nT - 1).astype(F32)

        @pl.when(i == 0)
        def _():
            gcarry_ref[...] = jnp.zeros_like(gcarry_ref)
            xcarry_ref[...] = jnp.zeros_like(xcarry_ref)

        xrx = jnp.concatenate([ph_ref[:, 0:W] * keep, pm_ref[:, 0:W]], axis=0)
        xc, xcb, r, ig, sp, a, mult, inv_mult = _lru_gates(xrx, cw_ref, cb_ref, wr_ref, br_ref, wi_ref, bi_ref, lam_ref)
        hs = hsm_ref[...]
        gt = pm_ref[:, W:2 * W]
        sg = _sigmoid(gt)
        dyv = dy_ref[...]
        dp_ref[:, W:2 * W] = (dyv * hs * _dsilu(gt, sg)).astype(BF16)
        row = lax.broadcasted_iota(jnp.int32, (tT, 1), 0)
        e = dyv * (gt * sg) + jnp.where(row == tT - 1, gcarry_ref[0:1, :], 0.0)
        m = jnp.where(row == tT - 1, 1.0, _shift_up(a, 1))
        s = 1
        while s < tT:
            ok = row < tT - s
            m_sh = jnp.where(ok, _shift_up(m, s), 1.0)
            e_sh = jnp.where(ok, _shift_up(e, s), 0.0)
            e = m * e_sh + e
            m = m * m_sh
            s *= 2
        G = e
        gcarry_ref[...] = jnp.broadcast_to(a[0:1, :] * G[0:1, :], (8, W))
        hs_prev = jnp.where(row == 0, hsh_ref[HL - 1:HL, :] * keep, _shift_down(hs, 1))
        da = G * hs_prev
        dmult = G * (ig * xc)
        di = G * mult * xc
        dxc = G * mult * ig
        dlog_a = da * a - dmult * (a * a) * inv_mult
        drp = dlog_a * ((-LRU_C) * sp) * r * (1.0 - r)
        dip = di * ig * (1.0 - ig)
        dlam = jnp.sum(dlog_a * ((-LRU_C) * r), axis=0, keepdims=True) * (-_sigmoid(-lam_ref[...]))
        drb, dib = drp.astype(BF16), dip.astype(BF16)
        back = []
        for hd in range(LRU_HEADS):
            cols = slice(hd * LRU_HD, (hd + 1) * LRU_HD)
            back.append(_nt(drb[:, cols], wr_ref[hd]) + _nt(dib[:, cols], wi_ref[hd]))
            dwr_h = _tn(xcb[:, cols], drb[:, cols])
            dwi_h = _tn(xcb[:, cols], dib[:, cols])

            @pl.when(i == 0)
            def _(hd=hd, dwr_h=dwr_h, dwi_h=dwi_h):
                dwr_ref[hd] = dwr_h
                dwi_ref[hd] = dwi_h

            @pl.when(i > 0)
            def _(hd=hd, dwr_h=dwr_h, dwi_h=dwi_h):
                dwr_ref[hd] += dwr_h
                dwi_ref[hd] += dwi_h

        dxc = dxc + jnp.concatenate(back, axis=1)
        dxcx = jnp.concatenate([dxc, xcarry_ref[...]], axis=0)
        dxr = cw_ref[LRU_CONV_K - 1:LRU_CONV_K, :] * dxc
        rows = []
        for k in range(LRU_CONV_K - 1):
            j = LRU_CONV_K - 1 - k
            dxr = dxr + cw_ref[k:k + 1, :] * _shift_up(dxcx, j)[0:tT]
            rows.append(jnp.sum(dxc * _shift_down(xrx, j)[HL:], axis=0, keepdims=True))
        rows.append(jnp.sum(dxc * xrx[HL:], axis=0, keepdims=True))
        dp_ref[:, 0:W] = dxr.astype(BF16)
        xcarry_ref[...] = dxc[0:8]
        rows += [jnp.sum(dxc, axis=0, keepdims=True), jnp.sum(drp, axis=0, keepdims=True),
                 jnp.sum(dip, axis=0, keepdims=True), dlam]
        vecs = jnp.concatenate(rows, axis=0)

        @pl.when(i == 0)
        def _():
            dvec_ref[...] = vecs

        @pl.when(i > 0)
        def _():
            dvec_ref[...] += vecs

    vec = pl.BlockSpec((1, W), lambda i: (0, 0))
    heads = pl.BlockSpec((LRU_HEADS, LRU_HD, LRU_HD), lambda i: (0, 0, 0))

    def tile(i):
        return (nT - 1 - i, 0)

    def prev_blk(i):
        return (jnp.maximum((nT - 1 - i) * hb - 1, 0), 0)

    return _pallas(
        body, name=name, grid=(nT,),
        in_specs=[pl.BlockSpec((tT, 2 * W), tile), pl.BlockSpec((HL, 2 * W), prev_blk),
                  pl.BlockSpec((tT, W), tile), pl.BlockSpec((HL, W), prev_blk), pl.BlockSpec((tT, W), tile),
                  pl.BlockSpec((8, W), lambda i: (0, 0)), vec, heads, vec, heads, vec, vec],
        out_specs=[pl.BlockSpec((tT, 2 * W), tile), heads, heads, pl.BlockSpec((8, W), lambda i: (0, 0))],
        out_shape=[jax.ShapeDtypeStruct((T, 2 * W), BF16), jax.ShapeDtypeStruct((LRU_HEADS, LRU_HD, LRU_HD), F32),
                   jax.ShapeDtypeStruct((LRU_HEADS, LRU_HD, LRU_HD), F32), jax.ShapeDtypeStruct((8, W), F32)],
        scratch_shapes=[pltpu.VMEM((8, W), F32), pltpu.VMEM((8, W), F32)],
        compiler_params=_params("arbitrary"))(p, p, hs, hs, dy, cw, cb, wr, br, wi, bi, lam)


def _pad_rows(a, rows):
    return jnp.concatenate([a, jnp.zeros((rows - a.shape[0], a.shape[1]), a.dtype)], axis=0)


def _layer_fwd(even, h, w, w_in, w_out, after):
    if even:
        p, n = _in_proj(h, w["norm"], w_in, 0, after, "in_proj_even")
        y, aux = _even_mixer_fwd(p, w["conv_w"], w["conv_b"], w["ln_g"], w["ln_b"], w["pool_w"], w["pool_b"],
                                 w["pool_scale"], "even_mixer_fwd")
        h_next = _out_proj(y, w_out, 0, h, "out_proj_even")
    else:
        p, n = _in_proj(h, w["norm"], w_in, 0, after, "in_proj_odd")
        y, aux = _odd_mixer_fwd(p, w["conv_w"], w["conv_b"], w["w_rg"], w["b_rg"], w["w_ig"], w["b_ig"], w["lam"],
                                "odd_mixer_fwd")
        h_next = _out_proj(y, w_out, 0, h, "out_proj_odd")
    return h_next, (h, n, p, aux, y)


def _layer_bwd(even, saved, w, w_in, w_out, dh, dhb, after):
    h, n, p, aux, y = saved
    if even:
        dw_out = _dw_out(y, dhb, 0, 1, None, "dw_out_even")
        dy = _dy_proj(dhb, w_out, 0, after, "dy_proj_even")
        dp, dcw, dvec, dpw = _even_mixer_bwd(p, aux, dy, w["conv_w"], w["conv_w_rev"], w["ln_g"], w["ln_b"],
                                             w["pool_w"], w["pool_b"], w["pool_scale"], "even_mixer_bwd")
        dw_in = _dw_in(n, dp, N_CHIPS, 0, 1, None, "dw_in_even")
        dh, dhb, dnorm = _dn_proj(dp, w_in, 0, h, w["norm"], dh, "dn_proj_even")
        return dh, dhb, dw_in, dw_out, dict(conv_w=dcw, vec=dvec, pool_w=dpw, norm=dnorm)
    dw_out = _dw_out(y, dhb, 0, 1, None, "dw_out_odd")
    dy = _dy_proj(dhb, w_out, 0, after, "dy_proj_odd")
    dp, dwr, dwi, dvec = _odd_mixer_bwd(p, aux, dy, w["conv_w"], w["conv_b"], w["w_rg"], w["b_rg"], w["w_ig"],
                                        w["b_ig"], w["lam"], "odd_mixer_bwd")
    dw_in = _dw_in(n, dp, N_CHIPS, 0, 1, None, "dw_in_odd")
    dh, dhb, dnorm = _dn_proj(dp, w_in, 0, h, w["norm"], dh, "dn_proj_odd")
    return dh, dhb, dw_in, dw_out, dict(w_rg=dwr, w_ig=dwi, vec=dvec, norm=dnorm)


ANY = pl.BlockSpec(memory_space=pl.ANY)


def _mesh_pos():
    return lax.axis_index("x"), lax.axis_index("y"), lax.axis_index("c")


def _other_chips(x, y):
    return [(1 - x, y), (x, 1 - y), (1 - x, 1 - y)]


def _other_devices(x, y, c):
    out = []
    for p in range(1, N_DEV):
        out.append((1 - x if p & 4 else x, 1 - y if p & 2 else y, 1 - c if p & 1 else c))
    return out


def _remote(src, dst, ssem, rsem, dev):
    return pltpu.make_async_remote_copy(src_ref=src, dst_ref=dst, send_sem=ssem, recv_sem=rsem, device_id=dev,
                                        device_id_type=MESH)


def _comm_call(body, name, ins, out_shape, scratch, aliases=None):
    return _pallas(body, name=name, in_specs=[ANY] * len(ins), out_specs=[ANY] * len(out_shape), out_shape=out_shape,
                   scratch_shapes=scratch, input_output_aliases=aliases or {},
                   compiler_params=pltpu.CompilerParams(has_side_effects=True))(*ins)


def _cast_shard(w, layer, pos):
    _, R, C = w.shape
    tr = _row_tile(R, C)

    def body(pos_ref, w_ref, o_ref):
        o_ref[...] = w_ref[...].astype(BF16)

    grid_spec = pltpu.PrefetchScalarGridSpec(
        num_scalar_prefetch=1, grid=(R // tr,),
        in_specs=[pl.BlockSpec((None, tr, C), lambda i, pr: (layer, i, 0))],
        out_specs=pl.BlockSpec((None, None, tr, C), lambda i, pr: (0, pr[0], i, 0)))
    return _pallas(body, name="cast_shard", grid_spec=grid_spec,
                   out_shape=jax.ShapeDtypeStruct((1, N_CHIPS, R, C), BF16),
                   compiler_params=_params("parallel"))(pos, w)


def _gather_weights(big, small):
    nA = len(big)
    half = [a.shape[2] // 2 for a in big]

    def body(*refs):
        ins, outs = refs[:nA + 1], refs[nA + 1:2 * nA + 2]
        ssem, rsem, fsem, frsem, lsem = refs[2 * nA + 2:]
        x, y, c = _mesh_pos()
        k = 2 * x + y
        chips = _other_chips(x, y)
        sib = (x, y, 1 - c)

        def slab(a, chip, core):
            return outs[a].at[:, chip, pl.ds(core * half[a], half[a]), :]

        local = [pltpu.make_async_copy(ins[nA], outs[nA].at[k], lsem.at[0])]
        for cp in local:
            cp.start()
        sends = []
        for j, (ox, oy) in enumerate(chips):
            for a in range(nA):
                sends.append(_remote(slab(a, k, c), slab(a, k, c), ssem.at[a, j], rsem.at[a, j], (ox, oy, c)))
            sends.append(_remote(ins[nA], outs[nA].at[k], ssem.at[nA, j], rsem.at[nA, j], (ox, oy, c)))
        for cp in sends:
            cp.start()
        for j, (ox, oy) in enumerate(chips):
            kj = 2 * ox + oy
            for a in range(nA):
                got = slab(a, kj, c)
                _remote(got, got, ssem.at[a, j], rsem.at[a, j], (ox, oy, c)).wait_recv()
                fw = _remote(got, got, fsem.at[a, j], frsem.at[a, j], sib)
                fw.start()
                sends.append(fw)
            gs = outs[nA].at[kj]
            _remote(gs, gs, ssem.at[nA, j], rsem.at[nA, j], (ox, oy, c)).wait_recv()
        for j, (ox, oy) in enumerate(chips):
            kj = 2 * ox + oy
            for a in range(nA):
                theirs = slab(a, kj, 1 - c)
                _remote(theirs, theirs, fsem.at[a, j], frsem.at[a, j], sib).wait_recv()
        for cp in sends:
            cp.wait_send()
        for cp in local:
            cp.wait()

    out_shape = [jax.ShapeDtypeStruct(a.shape, a.dtype) for a in big]
    out_shape.append(jax.ShapeDtypeStruct((N_CHIPS,) + small.shape, small.dtype))
    scratch = [pltpu.SemaphoreType.DMA((nA + 1, 3)), pltpu.SemaphoreType.DMA((nA + 1, 3)),
               pltpu.SemaphoreType.DMA((nA, 3)), pltpu.SemaphoreType.DMA((nA, 3)), pltpu.SemaphoreType.DMA((1,))]
    return _comm_call(body, "gather_weights", list(big) + [small], out_shape, scratch, {a: a for a in range(nA)})


HBM = pl.BlockSpec(memory_space=pltpu.HBM)
SEM = pl.BlockSpec(memory_space=pltpu.SEMAPHORE)
EFFECT = pltpu.SideEffectType.DATAFLOW_SIDE_EFFECTING


def _split_start(arrays, copies, n, name):
    k = len(arrays)

    def body(*refs):
        for cp in copies(refs[k + 2:2 * k + 2], refs[k], refs[k + 1]):
            cp.start()
        refs[2 * k + 2][...] = jnp.zeros((8, 128), F32)

    out = _pallas(
        body, name=name,
        out_shape=(pltpu.SemaphoreType.DMA((n,)), pltpu.SemaphoreType.DMA((n,)),
                   *[pltpu.HBM(a.shape, a.dtype) for a in arrays], jax.ShapeDtypeStruct((8, 128), F32)),
        in_specs=(HBM,) * k, out_specs=(SEM, SEM) + (HBM,) * k + (pl.BlockSpec(memory_space=pltpu.VMEM),),
        input_output_aliases={i: i + 2 for i in range(k)},
        compiler_params=pltpu.CompilerParams(has_side_effects=EFFECT),
    )(*[pltpu.with_memory_space_constraint(a, pltpu.HBM) for a in arrays])
    return out[0], out[1], list(out[2:2 + k]), out[2 + k]


def _split_wait(ssem, rsem, arrays, copies, after, name):
    k = len(arrays)

    def body(*refs):
        for cp in copies(refs[:k], refs[k], refs[k + 1]):
            cp.wait_send()
            cp.wait_recv()

    out = _pallas(
        body, name=name, out_shape=tuple(pltpu.HBM(a.shape, a.dtype) for a in arrays),
        in_specs=(HBM,) * k + (SEM, SEM, ANY), out_specs=(HBM,) * k, input_output_aliases={i: i for i in range(k)},
        compiler_params=pltpu.CompilerParams(has_side_effects=EFFECT),
    )(*arrays, ssem, rsem, after)
    return list(out)


def _gather_copies(shapes):
    half = [s[2] // 2 for s in shapes]

    def copies(refs, ssem, rsem):
        x, y, c = _mesh_pos()
        out = []
        for j, (ox, oy) in enumerate(_other_chips(x, y)):
            for a, ref in enumerate(refs):
                slab = ref.at[:, 2 * x + y, pl.ds(c * half[a], half[a]), :]
                out.append(_remote(slab, slab, ssem.at[3 * a + j], rsem.at[3 * a + j], (ox, oy, c)))
        return out

    return copies


def _chips_copies(n_arr):
    def copies(refs, ssem, rsem):
        x, y, c = _mesh_pos()
        out = []
        for j, (ox, oy) in enumerate(_other_chips(x, y)):
            for a in range(n_arr):
                out.append(_remote(refs[a].at[:, 2 * ox + oy], refs[n_arr + a].at[:, 2 * x + y], ssem.at[3 * a + j],
                                   rsem.at[3 * a + j], (ox, oy, c)))
        return out

    return copies


def _forward_cores(arrays):
    nA = len(arrays)
    half = [a.shape[2] // 2 for a in arrays]

    def body(*refs):
        outs = refs[nA:2 * nA]
        ssem, rsem = refs[2 * nA:]
        x, y, c = _mesh_pos()
        sib = (x, y, 1 - c)
        sends, waits = [], []
        for j, (ox, oy) in enumerate(_other_chips(x, y)):
            for a in range(nA):
                got = outs[a].at[:, 2 * ox + oy, pl.ds(c * half[a], half[a]), :]
                sends.append(_remote(got, got, ssem.at[a, j], rsem.at[a, j], sib))
                theirs = outs[a].at[:, 2 * ox + oy, pl.ds((1 - c) * half[a], half[a]), :]
                waits.append(_remote(theirs, theirs, ssem.at[a, j], rsem.at[a, j], sib))
        for cp in sends:
            cp.start()
        for cp in waits:
            cp.wait_recv()
        for cp in sends:
            cp.wait_send()

    out_shape = [jax.ShapeDtypeStruct(a.shape, a.dtype) for a in arrays]
    scratch = [pltpu.SemaphoreType.DMA((nA, 3)), pltpu.SemaphoreType.DMA((nA, 3))]
    return _comm_call(body, "forward_cores", list(arrays), out_shape, scratch, {a: a for a in range(nA)})


def _exchange_halves(big):
    nA = len(big)
    half = [a.shape[2] // 2 for a in big]

    def body(*refs):
        ins, outs = refs[:nA], refs[nA:2 * nA]
        ssem, rsem = refs[2 * nA:]
        x, y, c = _mesh_pos()
        sib = (x, y, 1 - c)
        sends = [_remote(ins[a].at[:, :, pl.ds((1 - c) * half[a], half[a]), :], outs[a], ssem.at[a], rsem.at[a], sib)
                 for a in range(nA)]
        for cp in sends:
            cp.start()
        for a in range(nA):
            _remote(outs[a], outs[a], ssem.at[a], rsem.at[a], sib).wait_recv()
        for cp in sends:
            cp.wait_send()

    out_shape = [jax.ShapeDtypeStruct((a.shape[0], N_CHIPS, h, a.shape[3]), a.dtype) for a, h in zip(big, half)]
    scratch = [pltpu.SemaphoreType.DMA((nA,)), pltpu.SemaphoreType.DMA((nA,))]
    return _comm_call(body, "exchange_halves", list(big), out_shape, scratch)


def _exchange_cores(big, small):
    nA = len(big)
    half = [a.shape[2] // 2 for a in big]

    def body(*refs):
        ins, outs = refs[:nA + 1], refs[nA + 1:2 * nA + 2]
        ssem, rsem, ssem2, rsem2, lsem = refs[2 * nA + 2:]
        x, y, c = _mesh_pos()
        me = 4 * x + 2 * y + c
        sib = (x, y, 1 - c)
        peers = _other_devices(x, y, c)
        local = pltpu.make_async_copy(ins[nA], outs[nA].at[me], lsem.at[0])
        local.start()
        sends = []
        for a in range(nA):
            src = ins[a].at[:, :, pl.ds((1 - c) * half[a], half[a]), :]
            sends.append(_remote(src, outs[a], ssem.at[a], rsem.at[a], sib))
        for p, dev in enumerate(peers):
            sends.append(_remote(ins[nA], outs[nA].at[me], ssem2.at[p], rsem2.at[p], dev))
        for cp in sends:
            cp.start()
        for a in range(nA):
            _remote(outs[a], outs[a], ssem.at[a], rsem.at[a], sib).wait_recv()
        for p, (px, py, pc) in enumerate(peers):
            got = outs[nA].at[4 * px + 2 * py + pc]
            _remote(got, got, ssem2.at[p], rsem2.at[p], (px, py, pc)).wait_recv()
        for cp in sends:
            cp.wait_send()
        local.wait()

    out_shape = [jax.ShapeDtypeStruct((a.shape[0], N_CHIPS, h, a.shape[3]), a.dtype) for a, h in zip(big, half)]
    out_shape.append(jax.ShapeDtypeStruct((N_DEV,) + small.shape, small.dtype))
    scratch = [pltpu.SemaphoreType.DMA((nA,)), pltpu.SemaphoreType.DMA((nA,)), pltpu.SemaphoreType.DMA((N_DEV - 1,)),
               pltpu.SemaphoreType.DMA((N_DEV - 1,)), pltpu.SemaphoreType.DMA((1,))]
    return _comm_call(body, "exchange_cores", list(big) + [small], out_shape, scratch)


def _exchange_chips(parts):
    nA = len(parts)

    def body(*refs):
        ins, outs = refs[:nA], refs[nA:2 * nA]
        ssem, rsem = refs[2 * nA:]
        x, y, c = _mesh_pos()
        k = 2 * x + y
        chips = _other_chips(x, y)
        sends = []
        for j, (ox, oy) in enumerate(chips):
            for a in range(nA):
                sends.append(_remote(ins[a].at[:, 2 * ox + oy], outs[a].at[:, k], ssem.at[a, j], rsem.at[a, j],
                                     (ox, oy, c)))
        for cp in sends:
            cp.start()
        for j, (ox, oy) in enumerate(chips):
            for a in range(nA):
                got = outs[a].at[:, 2 * ox + oy]
                _remote(got, got, ssem.at[a, j], rsem.at[a, j], (ox, oy, c)).wait_recv()
        for cp in sends:
            cp.wait_send()

    out_shape = [jax.ShapeDtypeStruct(a.shape, a.dtype) for a in parts]
    scratch = [pltpu.SemaphoreType.DMA((nA, 3)), pltpu.SemaphoreType.DMA((nA, 3))]
    return _comm_call(body, "exchange_chips", list(parts), out_shape, scratch)


def _exchange_final(grads, everywhere):
    nA = len(grads)
    n_remote = sum(N_DEV - 1 if ev else 1 for ev in everywhere)

    def body(*refs):
        outs = refs[nA:2 * nA]
        ssem, rsem = refs[2 * nA:]
        x, y, c = _mesh_pos()
        k = 2 * x + y
        sib = (x, y, 1 - c)
        peers = _other_devices(x, y, c)
        sends, waits = [], []
        s = 0
        for a in range(nA):
            if everywhere[a]:
                r2 = grads[a].shape[1] // N_DEV
                mine = outs[a].at[:, pl.ds((2 * k + c) * r2, r2), :]
                for (px, py, pc) in peers:
                    sends.append(_remote(mine, mine, ssem.at[s], rsem.at[s], (px, py, pc)))
                    got = outs[a].at[:, pl.ds((2 * (2 * px + py) + pc) * r2, r2), :]
                    waits.append(_remote(got, got, ssem.at[s], rsem.at[s], (px, py, pc)))
                    s += 1
            else:
                r2 = grads[a].shape[1] // 2
                mine = outs[a].at[:, pl.ds(c * r2, r2), :]
                sends.append(_remote(mine, mine, ssem.at[s], rsem.at[s], sib))
                got = outs[a].at[:, pl.ds((1 - c) * r2, r2), :]
                waits.append(_remote(got, got, ssem.at[s], rsem.at[s], sib))
                s += 1
        for cp in sends:
            cp.start()
        for cp in waits:
            cp.wait_recv()
        for cp in sends:
            cp.wait_send()

    out_shape = [jax.ShapeDtypeStruct(g.shape, g.dtype) for g in grads]
    scratch = [pltpu.SemaphoreType.DMA((n_remote,)), pltpu.SemaphoreType.DMA((n_remote,))]
    return _comm_call(body, "exchange_final", list(grads), out_shape, scratch, {a: a for a in range(nA)})


BLOCK_BYTES = 1 << 20


def _row_tile(rows, cols, mult=16):
    best = mult
    for t in range(mult, rows + 1, mult):
        if rows % t == 0 and t * cols * 4 <= BLOCK_BYTES:
            best = t
    return best


def _add_cores(own, recv, pos):
    L, _, R, C = own.shape
    r2 = R // 2
    tr = _row_tile(r2, C)
    nb = r2 // tr

    def body(pos_ref, a_ref, r_ref, o_ref):
        o_ref[...] = (a_ref[...].astype(F32) + r_ref[...].astype(F32)).astype(BF16)

    blk = (None, None, tr, C)
    grid_spec = pltpu.PrefetchScalarGridSpec(
        num_scalar_prefetch=1, grid=(L, N_CHIPS, nb),
        in_specs=[pl.BlockSpec(blk, lambda l, s, i, pr: (l, s, pr[1] * nb + i, 0)),
                  pl.BlockSpec(blk, lambda l, s, i, pr: (l, s, i, 0))],
        out_specs=pl.BlockSpec(blk, lambda l, s, i, pr: (l, s, i, 0)))
    return _pallas(body, name="add_cores", grid_spec=grid_spec,
                   out_shape=jax.ShapeDtypeStruct((L, N_CHIPS, r2, C), BF16),
                   compiler_params=_params("parallel", "parallel", "parallel"))(pos, own, recv)


def _sum_chips(own, recv, pos, everywhere, layer, nlayers, prev):
    _, _, r2, C = own.shape
    tr = _row_tile(r2, 2 * C)
    nb = r2 // tr

    def body(pos_ref, a_ref, r_ref, *rest):
        acc = None
        for s in range(N_CHIPS):
            term = jnp.where(pos_ref[0] == s, a_ref[...], r_ref[s]).astype(F32)
            acc = term if acc is None else acc + term
        rest[-1][...] = acc

    if everywhere:
        def out_map(i, pr):
            return (layer, (2 * pr[0] + pr[1]) * nb + i, 0)
    else:
        def out_map(i, pr):
            return (layer, pr[1] * nb + i, 0)

    in_specs = [pl.BlockSpec((None, None, tr, C), lambda i, pr: (0, pr[0], i, 0)),
                pl.BlockSpec((None, N_CHIPS, tr, C), lambda i, pr: (0, 0, i, 0))]
    grid_spec = pltpu.PrefetchScalarGridSpec(
        num_scalar_prefetch=1, grid=(nb,), in_specs=in_specs + ([] if prev is None else [ANY]),
        out_specs=pl.BlockSpec((None, tr, C), out_map))
    rows = (N_DEV if everywhere else 2) * r2
    args = (pos, own, recv) if prev is None else (pos, own, recv, prev)
    return _pallas(body, name="sum_chips", grid_spec=grid_spec, out_shape=jax.ShapeDtypeStruct((nlayers, rows, C), F32),
                   input_output_aliases={} if prev is None else {3: 0},
                   compiler_params=_params("parallel"))(*args)


def _sum_devices(parts):
    n, R, C = parts.shape
    tr = _row_tile(R, C * n, 8)

    def body(p_ref, o_ref):
        acc = p_ref[0]
        for s in range(1, n):
            acc = acc + p_ref[s]
        o_ref[...] = acc

    return _pallas(body, name="sum_devices", grid=(R // tr,), in_specs=[pl.BlockSpec((n, tr, C), lambda i: (0, i, 0))],
                   out_specs=pl.BlockSpec((tr, C), lambda i: (i, 0)), out_shape=jax.ShapeDtypeStruct((R, C), F32),
                   compiler_params=_params("parallel"))(parts)


def _adamw(w, g, m, v, name):
    L, R, C = w.shape
    tr = _row_tile(R, C, 8)

    def body(w_ref, g_ref, m_ref, v_ref, d_ref, m2_ref, v2_ref):
        gg = g_ref[...]
        m2 = ADAM_B1 * m_ref[...] + (1.0 - ADAM_B1) * gg
        v2 = ADAM_B2 * v_ref[...] + (1.0 - ADAM_B2) * (gg * gg)
        m_hat = m2 / (1.0 - ADAM_B1 ** ADAM_STEP)
        v_hat = v2 / (1.0 - ADAM_B2 ** ADAM_STEP)
        d_ref[...] = -ADAM_LR * (m_hat / (jnp.sqrt(v_hat) + ADAM_EPS) + ADAM_WD * w_ref[...])
        m2_ref[...] = m2
        v2_ref[...] = v2

    blk = pl.BlockSpec((1, tr, C), lambda l, i: (l, i, 0))
    shp = jax.ShapeDtypeStruct((L, R, C), F32)
    return _pallas(body, name=name, grid=(L, R // tr), in_specs=[blk] * 4, out_specs=[blk] * 3, out_shape=[shp] * 3,
                   compiler_params=_params("parallel", "parallel"))(w, g, m, v)


WEIGHTS = ("norm_even", "w_in_even", "conv_a_w", "conv_a_b", "ln_a_g", "ln_a_b", "pool_w", "pool_b", "pool_scale",
           "w_out_even", "norm_odd", "w_in_odd", "conv_c_w", "conv_c_b", "w_rg", "b_rg", "w_ig", "b_ig", "lru_lambda",
           "w_out_odd", "final_norm")
BIG = ("w_in_even", "w_out_even", "pool_w", "w_in_odd", "w_out_odd", "w_rg", "w_ig")
SMALL = tuple(n for n in WEIGHTS if n not in BIG)
SMALL_SHARDED = ("conv_a_w", "pool_b", "norm_odd", "conv_c_w", "conv_c_b", "b_rg", "b_ig", "lru_lambda")


def _pack(arrs):
    flat = jnp.concatenate([a.reshape(-1) for a in arrs])
    rows = -(-flat.shape[0] // (64 * 128)) * 64
    return jnp.pad(flat, (0, rows * 128 - flat.shape[0])).reshape(rows, 128)


def _unpack(buf, shapes, lead=()):
    flat = buf.reshape(tuple(lead) + (-1,))
    out, o = [], 0
    for s in shapes:
        n = 1
        for d in s:
            n *= d
        out.append(flat[..., o:o + n].reshape(tuple(lead) + tuple(s)))
        o += n
    return out


def _shard(full, axis, k):
    n = full.shape[axis] // N_CHIPS
    return lax.dynamic_slice_in_dim(full, k * n, n, axis)


def kernel(x, norm_even, w_in_even, conv_a_w, conv_a_b, ln_a_g, ln_a_b, pool_w, pool_b, pool_scale, w_out_even, norm_odd, w_in_odd, conv_c_w, conv_c_b, w_rg, b_rg, w_ig, b_ig, lru_lambda, w_out_odd, final_norm, loss_target, m_norm_even, m_w_in_even, m_conv_a_w, m_conv_a_b, m_ln_a_g, m_ln_a_b, m_pool_w, m_pool_b, m_pool_scale, m_w_out_even, m_norm_odd, m_w_in_odd, m_conv_c_w, m_conv_c_b, m_w_rg, m_b_rg, m_w_ig, m_b_ig, m_lru_lambda, m_w_out_odd, m_final_norm, v_norm_even, v_w_in_even, v_conv_a_w, v_conv_a_b, v_ln_a_g, v_ln_a_b, v_pool_w, v_pool_b, v_pool_scale, v_w_out_even, v_norm_odd, v_w_in_odd, v_conv_c_w, v_conv_c_b, v_w_rg, v_b_rg, v_w_ig, v_b_ig, v_lru_lambda, v_w_out_odd, v_final_norm):
    P = dict(locals())
    xi, yi, ci = _mesh_pos()
    k = 2 * xi + yi
    L = w_in_even.shape[0]
    D = D_MODEL

    pos = jnp.stack([k, ci]).astype(jnp.int32)
    depth = 2 * L
    pool_w3 = pool_w.reshape(L, 4 * 64, POOL_GW)

    def cast_group(layer):
        j = layer // 2
        if layer % 2 == 0:
            return [_cast_shard(w_in_even, j, pos), _cast_shard(w_out_even, j, pos), _cast_shard(pool_w3, j, pos)]
        return [_cast_shard(w_in_odd, j, pos), _cast_shard(w_out_odd, j, pos)]

    *group, g_small = _gather_weights(cast_group(0), _pack([P[n] for n in SMALL_SHARDED]))
    full = {}
    for n, a in zip(SMALL_SHARDED, _unpack(g_small, [P[n].shape for n in SMALL_SHARDED], lead=(N_CHIPS,))):
        a = jnp.moveaxis(a, 0, -2)
        full[n] = a.reshape(a.shape[:-2] + (N_CHIPS * a.shape[-1],))

    def small_weights(layer, group):
        j = layer // 2
        if layer % 2 == 0:
            cw = full["conv_a_w"][j]
            pw = group[2].reshape(N_CHIPS, 4, 64, POOL_GW).transpose(1, 0, 2, 3).reshape(4, POOL_GW, POOL_GW)
            return dict(norm=norm_even[j][None], conv_w=_pad_rows(cw, 32), conv_w_rev=_pad_rows(cw[::-1], 32),
                        conv_b=conv_a_b[j][None], ln_g=ln_a_g[j][None], ln_b=ln_a_b[j][None], pool_w=pw,
                        pool_b=full["pool_b"][j].reshape(1, D), pool_scale=pool_scale[j][None])
        return dict(norm=full["norm_odd"][j][None], conv_w=_pad_rows(full["conv_c_w"][j], 8),
                    conv_b=full["conv_c_b"][j][None], w_rg=w_rg[j].astype(BF16), b_rg=full["b_rg"][j][None],
                    w_ig=w_ig[j].astype(BF16), b_ig=full["b_ig"][j][None], lam=full["lru_lambda"][j][None])

    no_token = jnp.zeros((8, 128), F32)
    h = x[0]
    saved, big_w, small_w = [], [], []
    for layer in range(depth):
        token = no_token
        if layer + 1 < depth:
            nxt = cast_group(layer + 1)
            copies = _gather_copies([a.shape for a in nxt])
            ssem, rsem, nxt, token = _split_start(nxt, copies, 3 * len(nxt), "gather_start%d" % (layer + 1))
        small_w.append(small_weights(layer, group))
        big_w.append((group[0], group[1].reshape(1, -1, D)))
        h, sv = _layer_fwd(layer % 2 == 0, h, small_w[layer], *big_w[layer], token)
        saved.append(sv)
        if layer + 1 < depth:
            group = _forward_cores(_split_wait(ssem, rsem, nxt, copies, h, "gather_wait%d" % (layer + 1)))

    dh, dhb, d_final, loss = _loss_head(h, final_norm[None], loss_target[0])
    loss = lax.psum(loss[0, 0], ("x", "y", "c"))
    everywhere = [False, False, False, False, False, True, True]
    final = [None] * len(everywhere)
    small_of = [None] * depth
    recv_small = None

    def finish(pair, land, slots, j):
        for a, r, s in zip(pair, land, slots):
            final[s] = _sum_chips(a, r, pos, everywhere[s], j, L, final[s])

    pending = None
    token = no_token
    for layer in reversed(range(depth)):
        j = layer // 2
        dh, dhb, dw_in, dw_out, sm = _layer_bwd(layer % 2 == 0, saved[layer], small_w[layer], *big_w[layer], dh, dhb,
                                                token)
        small_of[layer] = sm
        if pending is not None:
            ssem, rsem, arrs, copies, slots, pj, pl_ = pending
            arrs = _split_wait(ssem, rsem, arrs, copies, dh, "chips_wait%d" % pl_)
            finish(arrs[:len(slots)], arrs[len(slots):], slots, pj)
            pending = None
        if layer % 2 == 0:
            dpw = sm["pool_w"].reshape(4, N_CHIPS, 64, POOL_GW).transpose(1, 0, 2, 3)
            parts = [dw_in, dw_out.reshape(1, N_CHIPS, -1, D), dpw.reshape(1, N_CHIPS, 4 * 64, POOL_GW).astype(BF16)]
            slots = [0, 1, 2]
        else:
            parts = [dw_in, dw_out.reshape(1, N_CHIPS, -1, D),
                     sm["w_rg"].reshape(1, N_CHIPS, -1, LRU_HD).astype(BF16),
                     sm["w_ig"].reshape(1, N_CHIPS, -1, LRU_HD).astype(BF16)]
            slots = [3, 4, 5, 6]
        if layer > 0:
            recv = _exchange_halves(parts)
        else:
            small_g = []
            for jj in range(L):
                ge, go = small_of[2 * jj], small_of[2 * jj + 1]
                small_g += [ge["conv_w"].reshape(32, 8, D).sum(axis=1)[:CONV_K], ge["vec"][0:5], ge["norm"], go["vec"],
                            go["norm"]]
            small_g.append(d_final)
            small_shapes = [a.shape for a in small_g]
            *recv, recv_small = _exchange_cores(parts, _pack(small_g))
        pair = [_add_cores(a, r, pos) for a, r in zip(parts, recv)]
        if layer > 0:
            copies = _chips_copies(len(pair))
            land = [lax.empty(a.shape, a.dtype) for a in pair]
            ssem, rsem, arrs, token = _split_start(pair + land, copies, 3 * len(pair), "chips_start%d" % layer)
            pending = (ssem, rsem, arrs, copies, slots, j, layer)
        else:
            finish(pair, _exchange_chips(pair), slots, j)
    grad_x = dh
    gw = _exchange_final(final, everywhere)
    sg = _unpack(_sum_devices(recv_small), small_shapes)

    grads = dict(w_in_even=gw[0], w_out_even=gw[1], pool_w=gw[2].reshape(pool_w.shape), w_in_odd=gw[3], w_out_odd=gw[4],
                 w_rg=gw[5].reshape(w_rg.shape), w_ig=gw[6].reshape(w_ig.shape), final_norm=sg[-1][0])
    ev = [sg[5 * j + 1] for j in range(L)]
    ov = [sg[5 * j + 3] for j in range(L)]
    grads["conv_a_w"] = _shard(jnp.stack([sg[5 * j] for j in range(L)]), 2, k)
    grads["norm_even"] = jnp.stack([sg[5 * j + 2][0] for j in range(L)])
    grads["norm_odd"] = _shard(jnp.stack([sg[5 * j + 4][0] for j in range(L)]), 1, k)
    for r, n in enumerate(("conv_a_b", "ln_a_g", "ln_a_b", "pool_scale")):
        grads[n] = jnp.stack([e[r] for e in ev])
    grads["pool_b"] = _shard(jnp.stack([e[4].reshape(4, POOL_GW) for e in ev]), 2, k)
    grads["conv_c_w"] = _shard(jnp.stack([o[0:4] for o in ov]), 2, k)
    for r, n in zip((4, 5, 6, 7), ("conv_c_b", "b_rg", "b_ig", "lru_lambda")):
        grads[n] = _shard(jnp.stack([o[r] for o in ov]), 1, k)

    delta, new_m, new_v = {}, {}, {}
    for n in BIG:
        s3 = (L, -1, P[n].shape[-1])
        d, m2, v2 = _adamw(P[n].reshape(s3), grads[n].reshape(s3), P["m_" + n].reshape(s3), P["v_" + n].reshape(s3), "adamw")
        delta[n], new_m[n], new_v[n] = d.reshape(P[n].shape), m2.reshape(P[n].shape), v2.reshape(P[n].shape)
    shapes = [P[n].shape for n in SMALL]
    packed = [_pack([src[n] for n in SMALL])[None] for src in
              (P, grads, {n: P["m_" + n] for n in SMALL}, {n: P["v_" + n] for n in SMALL})]
    for res, out in zip(_adamw(*packed, "adamw_small"), (delta, new_m, new_v)):
        for n, a in zip(SMALL, _unpack(res[0], shapes)):
            out[n] = a

    return (loss, grad_x[None], *[grads[n] for n in WEIGHTS], *[delta[n] for n in WEIGHTS],
            *[new_m[n] for n in WEIGHTS], *[new_v[n] for n in WEIGHTS])
```

```python
import functools

import jax
import jax.numpy as jnp
from jax import lax
from jax.experimental import pallas as pl
from jax.experimental.pallas import tpu as pltpu

F32 = jnp.float32
BF16 = jnp.bfloat16
MESH = pl.DeviceIdType.MESH

D_MODEL = 1024
N_CHIPS = 4
N_DEV = 8
EPS_RMS = 1e-6
EPS_LN = 1e-5
CONV_K = 31
POOL_WINDOWS = (2, 4, 8, 16)
POOL_GW = 256
LRU_HEADS = 12
LRU_HD = 128
W_LRU = LRU_HEADS * LRU_HD
LRU_CONV_K = 4
LRU_C = 8.0
ADAM_LR = 0.001
ADAM_B1 = 0.9
ADAM_B2 = 0.999
ADAM_EPS = 1e-08
ADAM_WD = 0.01
ADAM_STEP = 10

VMEM_LIMIT_BYTES = 56 * 1024 * 1024
ROW_TILE = 512
MIX_TILE = 256
EVEN_HALO = 32
ODD_HALO = 8


def _pallas(body, **kw):
    return pl.pallas_call(body, **kw)


def _params(*sem):
    return pltpu.CompilerParams(dimension_semantics=sem if sem else None, vmem_limit_bytes=VMEM_LIMIT_BYTES)


def _sigmoid(x):
    return 1.0 / (1.0 + jnp.exp(-x))


def _dsilu(x, s):
    return s * (1.0 + x * (1.0 - s))


def _nt(a, b):
    return lax.dot_general(a, b, (((1,), (1,)), ((), ())), preferred_element_type=F32)


def _tn(a, b):
    return lax.dot_general(a, b, (((0,), (0,)), ((), ())), preferred_element_type=F32)


def _in_proj(h, g, wg, layer, after, name):
    T, D = h.shape
    _, nblk, _, nb = wg.shape

    nrow = T // ROW_TILE

    def body(h_ref, g_ref, w_ref, after_ref, p_ref, n_ref, n_all):
        j, i = pl.program_id(0), pl.program_id(1)

        @pl.when(j == 0)
        def _():
            x = h_ref[...]
            r = lax.rsqrt(jnp.mean(x * x, axis=-1, keepdims=True) + EPS_RMS)
            nn = (x * r * g_ref[...]).astype(BF16)
            n_ref[...] = nn
            n_all[i] = nn

        p_ref[...] = jnp.dot(n_all[i], w_ref[0], preferred_element_type=F32)

    def rows_once(j, i):
        return (jnp.where(j == 0, i, nrow - 1), 0)

    return _pallas(
        body, name=name, grid=(nblk, nrow),
        in_specs=[pl.BlockSpec((ROW_TILE, D), rows_once), pl.BlockSpec((1, D), lambda j, i: (0, 0)),
                  pl.BlockSpec((None, 1, D, nb), lambda j, i: (layer, j, 0, 0)),
                  pl.BlockSpec((8, 128), lambda j, i: (0, 0))],
        out_specs=[pl.BlockSpec((ROW_TILE, nb), lambda j, i: (i, j)), pl.BlockSpec((ROW_TILE, D), rows_once)],
        out_shape=[jax.ShapeDtypeStruct((T, nblk * nb), F32), jax.ShapeDtypeStruct((T, D), BF16)],
        scratch_shapes=[pltpu.VMEM((nrow, ROW_TILE, D), BF16)],
        compiler_params=_params("arbitrary", "arbitrary"))(h, g, wg, after)


def _out_proj(y, w, layer, hres, name):
    T, K = y.shape
    D = w.shape[2]

    def body(y_ref, w_ref, r_ref, o_ref):
        o_ref[...] = r_ref[...] + jnp.dot(y_ref[...], w_ref[...], preferred_element_type=F32)

    return _pallas(
        body, name=name, grid=(T // ROW_TILE,),
        in_specs=[pl.BlockSpec((ROW_TILE, K), lambda i: (i, 0)), pl.BlockSpec((None, K, D), lambda i: (layer, 0, 0)),
                  pl.BlockSpec((ROW_TILE, D), lambda i: (i, 0))],
        out_specs=pl.BlockSpec((ROW_TILE, D), lambda i: (i, 0)),
        out_shape=jax.ShapeDtypeStruct((T, D), F32),
        compiler_params=_params("parallel"))(y, w, hres)


def _dy_proj(dout, w, layer, after, name):
    T, D = dout.shape
    K = w.shape[1]

    def body(d_ref, w_ref, after_ref, o_ref):
        o_ref[...] = _nt(d_ref[...], w_ref[...])

    return _pallas(
        body, name=name, grid=(T // ROW_TILE,),
        in_specs=[pl.BlockSpec((ROW_TILE, D), lambda i: (i, 0)), pl.BlockSpec((None, K, D), lambda i: (layer, 0, 0)),
                  pl.BlockSpec((8, 128), lambda i: (0, 0))],
        out_specs=pl.BlockSpec((ROW_TILE, K), lambda i: (i, 0)),
        out_shape=jax.ShapeDtypeStruct((T, K), F32),
        compiler_params=_params("parallel"))(dout, w, after)


def _dn_proj(dp, wg, layer, h, g, dres, name):
    T, D = h.shape
    _, nblk, _, nb = wg.shape

    nrow = T // ROW_TILE

    def body(dp_ref, w_ref, h_ref, g_ref, dres_ref, dh_ref, dhb_ref, dg_ref, acc_ref):
        j, i = pl.program_id(0), pl.program_id(1)
        part = _nt(dp_ref[...], w_ref[0])

        @pl.when(j == 0)
        def _():
            acc_ref[i] = part

        @pl.when(j > 0)
        def _():
            acc_ref[i] += part

        @pl.when(j == nblk - 1)
        def _():
            x = h_ref[...]
            r = lax.rsqrt(jnp.mean(x * x, axis=-1, keepdims=True) + EPS_RMS)
            dn = acc_ref[i]
            q = dn * g_ref[...]
            dh = dres_ref[...] + r * q - x * ((r * r * r) * jnp.mean(q * x, axis=-1, keepdims=True))
            dh_ref[...] = dh
            dhb_ref[...] = dh.astype(BF16)
            dgp = jnp.sum(dn * (x * r), axis=0, keepdims=True)

            @pl.when(i == 0)
            def _():
                dg_ref[...] = dgp

            @pl.when(i > 0)
            def _():
                dg_ref[...] += dgp

    def rows_last(j, i):
        return (jnp.where(j == nblk - 1, i, 0), 0)

    return _pallas(
        body, name=name, grid=(nblk, nrow),
        in_specs=[pl.BlockSpec((ROW_TILE, nb), lambda j, i: (i, j)),
                  pl.BlockSpec((None, 1, D, nb), lambda j, i: (layer, j, 0, 0)),
                  pl.BlockSpec((ROW_TILE, D), rows_last), pl.BlockSpec((1, D), lambda j, i: (0, 0)),
                  pl.BlockSpec((ROW_TILE, D), rows_last)],
        out_specs=[pl.BlockSpec((ROW_TILE, D), rows_last), pl.BlockSpec((ROW_TILE, D), rows_last),
                   pl.BlockSpec((1, D), lambda j, i: (0, 0))],
        out_shape=[jax.ShapeDtypeStruct((T, D), F32), jax.ShapeDtypeStruct((T, D), BF16),
                   jax.ShapeDtypeStruct((1, D), F32)],
        scratch_shapes=[pltpu.VMEM((nrow, ROW_TILE, D), F32)],
        compiler_params=_params("arbitrary", "arbitrary"))(dp, wg, h, g, dres)


def _dw_in(n, dp, nblk, layer, nlayers, prev, name):
    T, D = n.shape
    nb = dp.shape[1] // nblk
    ta = 512

    def body(n_ref, dp_ref, *rest):
        rest[-1][0] = _tn(n_ref[...], dp_ref[...]).astype(BF16)

    in_specs = [pl.BlockSpec((T, ta), lambda j, i: (0, i)), pl.BlockSpec((T, nb), lambda j, i: (0, j))]
    args = (n, dp) if prev is None else (n, dp, prev)
    return _pallas(
        body, name=name, grid=(nblk, D // ta), in_specs=in_specs + ([] if prev is None else [ANY]),
        out_specs=pl.BlockSpec((None, 1, ta, nb), lambda j, i: (layer, j, i, 0)),
        out_shape=jax.ShapeDtypeStruct((nlayers, nblk, D, nb), BF16),
        input_output_aliases={} if prev is None else {2: 0},
        compiler_params=_params("parallel", "parallel"))(*args)


def _dw_out(y, dout, layer, nlayers, prev, name):
    T, K = y.shape
    D = dout.shape[1]
    tk = 512

    def body(y_ref, d_ref, *rest):
        rest[-1][...] = _tn(y_ref[...], d_ref[...]).astype(BF16)

    in_specs = [pl.BlockSpec((T, tk), lambda i: (0, i)), pl.BlockSpec((T, D), lambda i: (0, 0))]
    args = (y, dout) if prev is None else (y, dout, prev)
    return _pallas(
        body, name=name, grid=(K // tk,), in_specs=in_specs + ([] if prev is None else [ANY]),
        out_specs=pl.BlockSpec((None, tk, D), lambda i: (layer, i, 0)),
        out_shape=jax.ShapeDtypeStruct((nlayers, K, D), BF16),
        input_output_aliases={} if prev is None else {2: 0},
        compiler_params=_params("parallel"))(*args)


def _loss_head(h, g, tgt):
    T, D = h.shape
    tm = MIX_TILE

    def body(h_ref, g_ref, t_ref, dh_ref, dhb_ref, dg_ref, loss_ref):
        i = pl.program_id(0)
        x = h_ref[...]
        gg = g_ref[...]
        r = lax.rsqrt(jnp.mean(x * x, axis=-1, keepdims=True) + EPS_RMS)
        xr = x * r
        e = xr * gg - t_ref[...]
        lp = 0.5 * jnp.sum(jnp.mean(e * e, axis=-1, keepdims=True), axis=0, keepdims=True)
        dn = e * (1.0 / D)
        q = dn * gg
        dh = r * q - x * ((r * r * r) * jnp.mean(q * x, axis=-1, keepdims=True))
        dh_ref[...] = dh
        dhb_ref[...] = dh.astype(BF16)
        dgp = jnp.sum(dn * xr, axis=0, keepdims=True)

        @pl.when(i == 0)
        def _():
            dg_ref[...] = dgp
            loss_ref[...] = lp

        @pl.when(i > 0)
        def _():
            dg_ref[...] += dgp
            loss_ref[...] += lp

    return _pallas(
        body, name="loss_head", grid=(T // tm,),
        in_specs=[pl.BlockSpec((tm, D), lambda i: (i, 0)), pl.BlockSpec((1, D), lambda i: (0, 0)),
                  pl.BlockSpec((tm, D), lambda i: (i, 0))],
        out_specs=[pl.BlockSpec((tm, D), lambda i: (i, 0)), pl.BlockSpec((tm, D), lambda i: (i, 0)),
                   pl.BlockSpec((1, D), lambda i: (0, 0)), pl.BlockSpec((1, 1), lambda i: (0, 0))],
        out_shape=[jax.ShapeDtypeStruct((T, D), F32), jax.ShapeDtypeStruct((T, D), BF16),
                   jax.ShapeDtypeStruct((1, D), F32), jax.ShapeDtypeStruct((1, 1), F32)],
        compiler_params=_params("arbitrary"))(h, g, tgt)


def _shift_up(x, j):
    return x if j == 0 else pltpu.roll(x, x.shape[0] - j, 0)


def _shift_down(x, j):
    return x if j == 0 else pltpu.roll(x, j, 0)


def _fill_shifted(dst_ref, src_ref):
    rows = dst_ref.shape[1]
    for s in range(8):
        dst_ref[s] = src_ref[pl.ds(s, rows), :]


def _fill_taps(wb_ref, w_ref):
    for k in range(w_ref.shape[0]):
        wb_ref[k] = jnp.broadcast_to(w_ref[k:k + 1, :], wb_ref.shape[1:])


def _tap_sum(sh_ref, wb_ref, r0, nrows, offsets):
    accs = [None] * (nrows // 8)
    for k, o in enumerate(offsets):
        wk = wb_ref[k]
        for u in range(nrows // 8):
            term = wk * sh_ref[o % 8, pl.ds(r0 + (o // 8) * 8 + 8 * u, 8), :]
            accs[u] = term if accs[u] is None else accs[u] + term
    return jnp.concatenate(accs, axis=0)


def _pool_sums(vx, up):
    sh = _shift_up if up else _shift_down
    outs = []
    for gi, w in enumerate(POOL_WINDOWS):
        s = vx[:, gi * POOL_GW:(gi + 1) * POOL_GW]
        j = 1
        while j < w:
            s = s + sh(s, j)
            j *= 2
        outs.append(s)
    return outs


def _inv_count(row0, nrows):
    pos = (row0 + 1 + lax.broadcasted_iota(jnp.int32, (nrows, 1), 0)).astype(F32)
    return [1.0 / jnp.minimum(pos, float(w)) for w in POOL_WINDOWS]


def _even_mixer_fwd(p, cw, cb, lg, lb, pw, pb, sc, name):
    T = p.shape[0]
    C = D_MODEL
    tT, HL = MIX_TILE, EVEN_HALO
    hb = tT // HL
    chunk = 32

    def body(pm_ref, ph_ref, cw_ref, cb_ref, lg_ref, lb_ref, pw_ref, pb_ref, sc_ref, y_ref, u1_ref, u0x_ref, sh_ref,
             wb_ref):
        i = pl.program_id(0)
        keep = (i > 0).astype(F32)

        @pl.when(i == 0)
        def _():
            _fill_taps(wb_ref, cw_ref)

        u0x_ref[0:HL] = ph_ref[:, 0:C] * _sigmoid(ph_ref[:, C:2 * C]) * keep
        u0x_ref[HL:HL + tT] = pm_ref[:, 0:C] * _sigmoid(pm_ref[:, C:2 * C])
        u0x_ref[HL + tT:HL + tT + 8] = jnp.zeros((8, C), F32)
        _fill_shifted(sh_ref, u0x_ref)
        offs = [HL - (CONV_K - 1) + k for k in range(CONV_K)]

        def conv_chunk(c, carry):
            r0 = pl.multiple_of(c * chunk, chunk)
            u1_ref[pl.ds(r0, chunk), :] = _tap_sum(sh_ref, wb_ref, r0, chunk, offs) + cb_ref[...]
            return carry

        lax.fori_loop(0, tT // chunk, conv_chunk, 0)
        u1 = u1_ref[...]
        mu = jnp.mean(u1, axis=-1, keepdims=True)
        xc = u1 - mu
        rs = lax.rsqrt(jnp.mean(xc * xc, axis=-1, keepdims=True) + EPS_LN)
        u2 = xc * rs * lg_ref[...] + lb_ref[...]
        u3 = u2 * _sigmoid(u2)
        ag = pm_ref[:, 2 * C:3 * C]
        y_ref[:, 0:C] = (u3 * (ag * _sigmoid(ag))).astype(BF16)
        vx = jnp.concatenate([ph_ref[:, 3 * C:4 * C] * keep, pm_ref[:, 3 * C:4 * C]], axis=0)
        sums = _pool_sums(vx, up=False)
        inv = _inv_count(i * tT, tT)
        for gi in range(len(POOL_WINDOWS)):
            cols = slice(gi * POOL_GW, (gi + 1) * POOL_GW)
            d0 = sums[gi][HL:] * inv[gi] - vx[HL:, cols]
            d1 = jnp.dot(d0.astype(BF16), pw_ref[gi], preferred_element_type=F32) + pb_ref[:, cols]
            bg = pm_ref[:, 4 * C + gi * POOL_GW:4 * C + (gi + 1) * POOL_GW]
            y_ref[:, C + gi * POOL_GW:C + (gi + 1) * POOL_GW] = (d1 * sc_ref[:, cols] * (bg * _sigmoid(bg))).astype(BF16)

    vec = pl.BlockSpec((1, C), lambda i: (0, 0))
    return _pallas(
        body, name=name, grid=(T // tT,),
        in_specs=[pl.BlockSpec((tT, 5 * C), lambda i: (i, 0)),
                  pl.BlockSpec((HL, 5 * C), lambda i: (jnp.maximum(i * hb - 1, 0), 0)),
                  pl.BlockSpec((32, C), lambda i: (0, 0)), vec, vec, vec,
                  pl.BlockSpec((4, POOL_GW, POOL_GW), lambda i: (0, 0, 0)), vec, vec],
        out_specs=[pl.BlockSpec((tT, 2 * C), lambda i: (i, 0)), pl.BlockSpec((tT, C), lambda i: (i, 0))],
        out_shape=[jax.ShapeDtypeStruct((T, 2 * C), BF16), jax.ShapeDtypeStruct((T, C), F32)],
        scratch_shapes=[pltpu.VMEM((HL + tT + 8, C), F32), pltpu.VMEM((8, HL + tT, C), F32),
                        pltpu.VMEM((32, 8, C), F32)],
        compiler_params=_params("arbitrary"))(p, p, cw, cb, lg, lb, pw, pb, sc)


def _even_mixer_bwd(p, u1, dy, cwr, lg, lb, pw, pb, sc, name):
    T = p.shape[0]
    C = D_MODEL
    tT, HL = MIX_TILE, EVEN_HALO
    hb = tT // HL
    nT = T // tT
    R1 = tT + HL
    chunk = 32

    def body(pm_ref, pp_ref, pn_ref, u1m_ref, u1n_ref, dym_ref, dyn_ref, cwr_ref, lg_ref, lb_ref, pw_ref,
             pb_ref, sc_ref, dp_ref, dcw_ref, dvec_ref, dpw_ref, x_ref, sh_ref, du0_ref, wb_ref):
        i = pl.program_id(0)

        @pl.when(i == 0)
        def _():
            _fill_taps(wb_ref, cwr_ref)

        keep_prev = (i > 0).astype(F32)
        keep_next = (i < nT - 1).astype(F32)
        row = lax.broadcasted_iota(jnp.int32, (R1, 1), 0)
        live = jnp.where(row < tT, 1.0, keep_next)

        def cat(m, n):
            return jnp.concatenate([m, n], axis=0)

        u1 = cat(u1m_ref[...], u1n_ref[...])
        mu = jnp.mean(u1, axis=-1, keepdims=True)
        xc = u1 - mu
        rs = lax.rsqrt(jnp.mean(xc * xc, axis=-1, keepdims=True) + EPS_LN)
        xh = xc * rs
        u2 = xh * lg_ref[...] + lb_ref[...]
        s2 = _sigmoid(u2)
        u3 = u2 * s2
        ag = cat(pm_ref[:, 2 * C:3 * C], pn_ref[:, 2 * C:3 * C])
        sa = _sigmoid(ag)
        dya = cat(dym_ref[:, 0:C], dyn_ref[:, 0:C])
        dp_ref[:, 2 * C:3 * C] = (dya * u3 * _dsilu(ag, sa))[0:tT].astype(BF16)
        du2 = dya * (ag * sa) * _dsilu(u2, s2)
        dlg = jnp.sum((du2 * xh)[0:tT], axis=0, keepdims=True)
        dlb = jnp.sum(du2[0:tT], axis=0, keepdims=True)
        dxh = du2 * lg_ref[...]
        du1 = rs * (dxh - jnp.mean(dxh, axis=-1, keepdims=True) - xh * jnp.mean(dxh * xh, axis=-1, keepdims=True))
        du1 = du1 * live
        dcb = jnp.sum(du1[0:tT], axis=0, keepdims=True)
        x_ref[0:R1] = du1
        x_ref[R1:R1 + 8] = jnp.zeros((8, C), F32)
        _fill_shifted(sh_ref, x_ref)

        def du0_chunk(c, carry):
            r0 = pl.multiple_of(c * chunk, chunk)
            du0_ref[pl.ds(r0, chunk), :] = _tap_sum(sh_ref, wb_ref, r0, chunk, list(range(CONV_K)))
            return carry

        lax.fori_loop(0, tT // chunk, du0_chunk, 0)
        av, agl = pm_ref[:, 0:C], pm_ref[:, C:2 * C]
        sg = _sigmoid(agl)
        du0 = du0_ref[...]
        dp_ref[:, 0:C] = (du0 * sg).astype(BF16)
        dp_ref[:, C:2 * C] = (du0 * av * sg * (1.0 - sg)).astype(BF16)
        du0_ref[...] = du1[0:tT]
        x_ref[0:HL] = pp_ref[:, 0:C] * _sigmoid(pp_ref[:, C:2 * C]) * keep_prev
        x_ref[HL:HL + tT] = av * sg
        x_ref[HL + tT:HL + tT + 8] = jnp.zeros((8, C), F32)
        _fill_shifted(sh_ref, x_ref)

        @pl.when(i == 0)
        def _():
            dcw_ref[...] = jnp.zeros_like(dcw_ref)

        for k in range(CONV_K):
            o = HL - (CONV_K - 1) + k

            def dw_chunk(c, acc, o=o):
                r0 = pl.multiple_of(c * 64, 64)
                for u in range(0, 64, 8):
                    acc = acc + du0_ref[pl.ds(r0 + u, 8), :] * sh_ref[o % 8, pl.ds(r0 + u + (o // 8) * 8, 8), :]
                return acc

            dcw_ref[8 * k:8 * k + 8, :] += lax.fori_loop(0, tT // 64, dw_chunk, jnp.zeros((8, C), F32))

        bg = cat(pm_ref[:, 4 * C:5 * C], pn_ref[:, 4 * C:5 * C])
        sb = _sigmoid(bg)
        dyb = cat(dym_ref[:, C:2 * C], dyn_ref[:, C:2 * C])
        dyb0 = dyb * (bg * sb)
        dd1 = dyb0 * sc_ref[...]
        dpb = jnp.sum(dd1[0:tT], axis=0, keepdims=True)
        inv1 = _inv_count(i * tT, R1)
        z_parts, dd0_parts = [], []
        for gi in range(len(POOL_WINDOWS)):
            cols = slice(gi * POOL_GW, (gi + 1) * POOL_GW)
            dd0 = _nt(dd1[:, cols].astype(BF16), pw_ref[gi])
            dd0_parts.append(dd0)
            z_parts.append(dd0 * inv1[gi] * live)
        fsum = _pool_sums(jnp.concatenate(z_parts, axis=1), up=True)
        vx = cat(pp_ref[:, 3 * C:4 * C] * keep_prev, pm_ref[:, 3 * C:4 * C])
        sums = _pool_sums(vx, up=False)
        inv0 = _inv_count(i * tT, tT)
        dsc_parts = []
        for gi in range(len(POOL_WINDOWS)):
            cols = slice(gi * POOL_GW, (gi + 1) * POOL_GW)
            dp_ref[:, 3 * C + gi * POOL_GW:3 * C + (gi + 1) * POOL_GW] = (fsum[gi][0:tT] - dd0_parts[gi][0:tT]).astype(BF16)
            d0 = (sums[gi][HL:] * inv0[gi] - vx[HL:, cols]).astype(BF16)
            d1 = jnp.dot(d0, pw_ref[gi], preferred_element_type=F32) + pb_ref[:, cols]
            bgm, sbm = bg[0:tT, cols], sb[0:tT, cols]
            dp_ref[:, 4 * C + gi * POOL_GW:4 * C + (gi + 1) * POOL_GW] = (
                dyb[0:tT, cols] * d1 * sc_ref[:, cols] * _dsilu(bgm, sbm)).astype(BF16)
            dsc_parts.append(jnp.sum(dyb0[0:tT, cols] * d1, axis=0, keepdims=True))
            dpw_g = _tn(d0, dd1[0:tT, cols].astype(BF16))

            @pl.when(i == 0)
            def _(gi=gi, dpw_g=dpw_g):
                dpw_ref[gi] = dpw_g

            @pl.when(i > 0)
            def _(gi=gi, dpw_g=dpw_g):
                dpw_ref[gi] += dpw_g

        dsc = jnp.concatenate(dsc_parts, axis=1)
        vecs = jnp.concatenate([dcb, dlg, dlb, dsc, dpb, jnp.zeros((3, C), F32)], axis=0)

        @pl.when(i == 0)
        def _():
            dvec_ref[...] = vecs

        @pl.when(i > 0)
        def _():
            dvec_ref[...] += vecs

    vec = pl.BlockSpec((1, C), lambda i: (0, 0))
    taps = pl.BlockSpec((32, C), lambda i: (0, 0))

    def prev_blk(i):
        return (jnp.maximum(i * hb - 1, 0), 0)

    def next_blk(i):
        return (jnp.minimum((i + 1) * hb, T // HL - 1), 0)

    return _pallas(
        body, name=name, grid=(nT,),
        in_specs=[pl.BlockSpec((tT, 5 * C), lambda i: (i, 0)), pl.BlockSpec((HL, 5 * C), prev_blk),
                  pl.BlockSpec((HL, 5 * C), next_blk),
                  pl.BlockSpec((tT, C), lambda i: (i, 0)), pl.BlockSpec((HL, C), next_blk),
                  pl.BlockSpec((tT, 2 * C), lambda i: (i, 0)), pl.BlockSpec((HL, 2 * C), next_blk),
                  taps, vec, vec, pl.BlockSpec((4, POOL_GW, POOL_GW), lambda i: (0, 0, 0)), vec, vec],
        out_specs=[pl.BlockSpec((tT, 5 * C), lambda i: (i, 0)), pl.BlockSpec((32 * 8, C), lambda i: (0, 0)),
                   pl.BlockSpec((8, C), lambda i: (0, 0)), pl.BlockSpec((4, POOL_GW, POOL_GW), lambda i: (0, 0, 0))],
        out_shape=[jax.ShapeDtypeStruct((T, 5 * C), BF16), jax.ShapeDtypeStruct((32 * 8, C), F32),
                   jax.ShapeDtypeStruct((8, C), F32), jax.ShapeDtypeStruct((4, POOL_GW, POOL_GW), F32)],
        scratch_shapes=[pltpu.VMEM((R1 + 8, C), F32), pltpu.VMEM((8, R1, C), F32), pltpu.VMEM((tT, C), F32),
                        pltpu.VMEM((32, 8, C), F32)],
        compiler_params=_params("arbitrary"))(p, p, p, u1, u1, dy, dy, cwr, lg, lb, pw, pb, sc)


def _softplus(z):
    u = jnp.exp(-jnp.abs(z))
    w = 1.0 + u
    l1p = jnp.where(w == 1.0, u, u * jnp.log(w) / jnp.where(w == 1.0, 1.0, w - 1.0))
    return jnp.maximum(z, 0.0) + l1p


def _lru_gates(xrx, cw_ref, cb_ref, wr_ref, br_ref, wi_ref, bi_ref, lam_ref):
    HL = ODD_HALO
    xc = cb_ref[...] + cw_ref[LRU_CONV_K - 1:LRU_CONV_K, :] * xrx[HL:]
    for k in range(LRU_CONV_K - 1):
        xc = xc + cw_ref[k:k + 1, :] * _shift_down(xrx, LRU_CONV_K - 1 - k)[HL:]
    xcb = xc.astype(BF16)
    rp, ip = [], []
    for hd in range(LRU_HEADS):
        cols = slice(hd * LRU_HD, (hd + 1) * LRU_HD)
        rp.append(jnp.dot(xcb[:, cols], wr_ref[hd], preferred_element_type=F32))
        ip.append(jnp.dot(xcb[:, cols], wi_ref[hd], preferred_element_type=F32))
    r = _sigmoid(jnp.concatenate(rp, axis=1) + br_ref[...])
    ig = _sigmoid(jnp.concatenate(ip, axis=1) + bi_ref[...])
    sp = _softplus(-lam_ref[...])
    log_a = (-LRU_C) * r * sp
    a = jnp.exp(log_a)
    mult = jnp.sqrt(-jnp.tanh(log_a) * (a * a + 1.0))
    return xc, xcb, r, ig, sp, a, mult, 1.0 / mult


def _odd_mixer_fwd(p, cw, cb, wr, br, wi, bi, lam, name):
    T = p.shape[0]
    W = W_LRU
    tT, HL = MIX_TILE, ODD_HALO
    hb = tT // HL

    def body(pm_ref, ph_ref, cw_ref, cb_ref, wr_ref, br_ref, wi_ref, bi_ref, lam_ref, y_ref, hs_ref, carry_ref):
        i = pl.program_id(0)
        keep = (i > 0).astype(F32)

        @pl.when(i == 0)
        def _():
            carry_ref[...] = jnp.zeros_like(carry_ref)

        xrx = jnp.concatenate([ph_ref[:, 0:W] * keep, pm_ref[:, 0:W]], axis=0)
        xc, _, _, ig, _, a, mult, _ = _lru_gates(xrx, cw_ref, cb_ref, wr_ref, br_ref, wi_ref, bi_ref, lam_ref)
        b = mult * (ig * xc)
        row = lax.broadcasted_iota(jnp.int32, (tT, 1), 0)
        s = 1
        while s < tT:
            ok = row >= s
            a_sh = jnp.where(ok, _shift_down(a, s), 1.0)
            b_sh = jnp.where(ok, _shift_down(b, s), 0.0)
            b = a * b_sh + b
            a = a * a_sh
            s *= 2
        hs = a * carry_ref[0:1, :] + b
        hs_ref[...] = hs
        carry_ref[...] = jnp.broadcast_to(hs[tT - 1:tT, :], (8, W))
        gt = pm_ref[:, W:2 * W]
        y_ref[...] = (hs * (gt * _sigmoid(gt))).astype(BF16)

    vec = pl.BlockSpec((1, W), lambda i: (0, 0))
    heads = pl.BlockSpec((LRU_HEADS, LRU_HD, LRU_HD), lambda i: (0, 0, 0))
    return _pallas(
        body, name=name, grid=(T // tT,),
        in_specs=[pl.BlockSpec((tT, 2 * W), lambda i: (i, 0)),
                  pl.BlockSpec((HL, 2 * W), lambda i: (jnp.maximum(i * hb - 1, 0), 0)),
                  pl.BlockSpec((8, W), lambda i: (0, 0)), vec, heads, vec, heads, vec, vec],
        out_specs=[pl.BlockSpec((tT, W), lambda i: (i, 0)), pl.BlockSpec((tT, W), lambda i: (i, 0))],
        out_shape=[jax.ShapeDtypeStruct((T, W), BF16), jax.ShapeDtypeStruct((T, W), F32)],
        scratch_shapes=[pltpu.VMEM((8, W), F32)],
        compiler_params=_params("arbitrary"))(p, p, cw, cb, wr, br, wi, bi, lam)


def _odd_mixer_bwd(p, hs, dy, cw, cb, wr, br, wi, bi, lam, name):
    T = p.shape[0]
    W = W_LRU
    tT, HL = MIX_TILE, ODD_HALO
    hb = tT // HL
    nT = T // tT

    def body(pm_ref, ph_ref, hsm_ref, hsh_ref, dy_ref, cw_ref, cb_ref, wr_ref, br_ref, wi_ref, bi_ref, lam_ref,
             dp_ref, dwr_ref, dwi_ref, dvec_ref, gcarry_ref, xcarry_ref):
        i = pl.program_id(0)
        keep = (i < nT - 1).astype(F32)

        @pl.when(i == 0)
        def _():
            gcarry_ref[...] = jnp.zeros_like(gcarry_ref)
            xcarry_ref[...] = jnp.zeros_like(xcarry_ref)

        xrx = jnp.concatenate([ph_ref[:, 0:W] * keep, pm_ref[:, 0:W]], axis=0)
        xc, xcb, r, ig, sp, a, mult, inv_mult = _lru_gates(xrx, cw_ref, cb_ref, wr_ref, br_ref, wi_ref, bi_ref, lam_ref)
        hs = hsm_ref[...]
        gt = pm_ref[:, W:2 * W]
        sg = _sigmoid(gt)
        dyv = dy_ref[...]
        dp_ref[:, W:2 * W] = (dyv * hs * _dsilu(gt, sg)).astype(BF16)
        row = lax.broadcasted_iota(jnp.int32, (tT, 1), 0)
        e = dyv * (gt * sg) + jnp.where(row == tT - 1, gcarry_ref[0:1, :], 0.0)
        m = jnp.where(row == tT - 1, 1.0, _shift_up(a, 1))
        s = 1
        while s < tT:
            ok = row < tT - s
            m_sh = jnp.where(ok, _shift_up(m, s), 1.0)
            e_sh = jnp.where(ok, _shift_up(e, s), 0.0)
            e = m * e_sh + e
            m = m * m_sh
            s *= 2
        G = e
        gcarry_ref[...] = jnp.broadcast_to(a[0:1, :] * G[0:1, :], (8, W))
        hs_prev = jnp.where(row == 0, hsh_ref[HL - 1:HL, :] * keep, _shift_down(hs, 1))
        da = G * hs_prev
        dmult = G * (ig * xc)
        di = G * mult * xc
        dxc = G * mult * ig
        dlog_a = da * a - dmult * (a * a) * inv_mult
        drp = dlog_a * ((-LRU_C) * sp) * r * (1.0 - r)
        dip = di * ig * (1.0 - ig)
        dlam = jnp.sum(dlog_a * ((-LRU_C) * r), axis=0, keepdims=True) * (-_sigmoid(-lam_ref[...]))
        drb, dib = drp.astype(BF16), dip.astype(BF16)
        back = []
        for hd in range(LRU_HEADS):
            cols = slice(hd * LRU_HD, (hd + 1) * LRU_HD)
            back.append(_nt(drb[:, cols], wr_ref[hd]) + _nt(dib[:, cols], wi_ref[hd]))
            dwr_h = _tn(xcb[:, cols], drb[:, cols])
            dwi_h = _tn(xcb[:, cols], dib[:, cols])

            @pl.when(i == 0)
            def _(hd=hd, dwr_h=dwr_h, dwi_h=dwi_h):
                dwr_ref[hd] = dwr_h
                dwi_ref[hd] = dwi_h

            @pl.when(i > 0)
            def _(hd=hd, dwr_h=dwr_h, dwi_h=dwi_h):
                dwr_ref[hd] += dwr_h
                dwi_ref[hd] += dwi_h

        dxc = dxc + jnp.concatenate(back, axis=1)
        dxcx = jnp.concatenate([dxc, xcarry_ref[...]], axis=0)
        dxr = cw_ref[LRU_CONV_K - 1:LRU_CONV_K, :] * dxc
        rows = []
        for k in range(LRU_CONV_K - 1):
            j = LRU_CONV_K - 1 - k
            dxr = dxr + cw_ref[k:k + 1, :] * _shift_up(dxcx, j)[0:tT]
            rows.append(jnp.sum(dxc * _shift_down(xrx, j)[HL:], axis=0, keepdims=True))
        rows.append(jnp.sum(dxc * xrx[HL:], axis=0, keepdims=True))
        dp_ref[:, 0:W] = dxr.astype(BF16)
        xcarry_ref[...] = dxc[0:8]
        rows += [jnp.sum(dxc, axis=0, keepdims=True), jnp.sum(drp, axis=0, keepdims=True),
                 jnp.sum(dip, axis=0, keepdims=True), dlam]
        vecs = jnp.concatenate(rows, axis=0)

        @pl.when(i == 0)
        def _():
            dvec_ref[...] = vecs

        @pl.when(i > 0)
        def _():
            dvec_ref[...] += vecs

    vec = pl.BlockSpec((1, W), lambda i: (0, 0))
    heads = pl.BlockSpec((LRU_HEADS, LRU_HD, LRU_HD), lambda i: (0, 0, 0))

    def tile(i):
        return (nT - 1 - i, 0)

    def prev_blk(i):
        return (jnp.maximum((nT - 1 - i) * hb - 1, 0), 0)

    return _pallas(
        body, name=name, grid=(nT,),
        in_specs=[pl.BlockSpec((tT, 2 * W), tile), pl.BlockSpec((HL, 2 * W), prev_blk),
                  pl.BlockSpec((tT, W), tile), pl.BlockSpec((HL, W), prev_blk), pl.BlockSpec((tT, W), tile),
                  pl.BlockSpec((8, W), lambda i: (0, 0)), vec, heads, vec, heads, vec, vec],
        out_specs=[pl.BlockSpec((tT, 2 * W), tile), heads, heads, pl.BlockSpec((8, W), lambda i: (0, 0))],
        out_shape=[jax.ShapeDtypeStruct((T, 2 * W), BF16), jax.ShapeDtypeStruct((LRU_HEADS, LRU_HD, LRU_HD), F32),
                   jax.ShapeDtypeStruct((LRU_HEADS, LRU_HD, LRU_HD), F32), jax.ShapeDtypeStruct((8, W), F32)],
        scratch_shapes=[pltpu.VMEM((8, W), F32), pltpu.VMEM((8, W), F32)],
        compiler_params=_params("arbitrary"))(p, p, hs, hs, dy, cw, cb, wr, br, wi, bi, lam)


def _pad_rows(a, rows):
    return jnp.concatenate([a, jnp.zeros((rows - a.shape[0], a.shape[1]), a.dtype)], axis=0)


def _layer_fwd(even, h, w, w_in, w_out, after):
    if even:
        p, n = _in_proj(h, w["norm"], w_in, 0, after, "in_proj_even")
        y, aux = _even_mixer_fwd(p, w["conv_w"], w["conv_b"], w["ln_g"], w["ln_b"], w["pool_w"], w["pool_b"],
                                 w["pool_scale"], "even_mixer_fwd")
        h_next = _out_proj(y, w_out, 0, h, "out_proj_even")
    else:
        p, n = _in_proj(h, w["norm"], w_in, 0, after, "in_proj_odd")
        y, aux = _odd_mixer_fwd(p, w["conv_w"], w["conv_b"], w["w_rg"], w["b_rg"], w["w_ig"], w["b_ig"], w["lam"],
                                "odd_mixer_fwd")
        h_next = _out_proj(y, w_out, 0, h, "out_proj_odd")
    return h_next, (h, n, p, aux, y)


def _layer_bwd(even, saved, w, w_in, w_out, dh, dhb, after):
    h, n, p, aux, y = saved
    if even:
        dw_out = _dw_out(y, dhb, 0, 1, None, "dw_out_even")
        dy = _dy_proj(dhb, w_out, 0, after, "dy_proj_even")
        dp, dcw, dvec, dpw = _even_mixer_bwd(p, aux, dy, w["conv_w_rev"], w["ln_g"], w["ln_b"],
                                             w["pool_w"], w["pool_b"], w["pool_scale"], "even_mixer_bwd")
        dw_in = _dw_in(n, dp, N_CHIPS, 0, 1, None, "dw_in_even")
        dh, dhb, dnorm = _dn_proj(dp, w_in, 0, h, w["norm"], dh, "dn_proj_even")
        return dh, dhb, dw_in, dw_out, dict(conv_w=dcw, vec=dvec, pool_w=dpw, norm=dnorm)
    dw_out = _dw_out(y, dhb, 0, 1, None, "dw_out_odd")
    dy = _dy_proj(dhb, w_out, 0, after, "dy_proj_odd")
    dp, dwr, dwi, dvec = _odd_mixer_bwd(p, aux, dy, w["conv_w"], w["conv_b"], w["w_rg"], w["b_rg"], w["w_ig"],
                                        w["b_ig"], w["lam"], "odd_mixer_bwd")
    dw_in = _dw_in(n, dp, N_CHIPS, 0, 1, None, "dw_in_odd")
    dh, dhb, dnorm = _dn_proj(dp, w_in, 0, h, w["norm"], dh, "dn_proj_odd")
    return dh, dhb, dw_in, dw_out, dict(w_rg=dwr, w_ig=dwi, vec=dvec, norm=dnorm)


ANY = pl.BlockSpec(memory_space=pl.ANY)


def _mesh_pos():
    return lax.axis_index("x"), lax.axis_index("y"), lax.axis_index("c")


def _other_chips(x, y):
    return [(1 - x, y), (x, 1 - y), (1 - x, 1 - y)]


def _other_devices(x, y, c):
    out = []
    for p in range(1, N_DEV):
        out.append((1 - x if p & 4 else x, 1 - y if p & 2 else y, 1 - c if p & 1 else c))
    return out


def _remote(src, dst, ssem, rsem, dev):
    return pltpu.make_async_remote_copy(src_ref=src, dst_ref=dst, send_sem=ssem, recv_sem=rsem, device_id=dev,
                                        device_id_type=MESH)


def _comm_call(body, name, ins, out_shape, scratch, aliases=None):
    return _pallas(body, name=name, in_specs=[ANY] * len(ins), out_specs=[ANY] * len(out_shape), out_shape=out_shape,
                   scratch_shapes=scratch, input_output_aliases=aliases or {},
                   compiler_params=pltpu.CompilerParams(has_side_effects=True))(*ins)


def _cast_shard(w, layer, pos):
    _, R, C = w.shape
    tr = _row_tile(R, C)

    def body(pos_ref, w_ref, o_ref):
        o_ref[...] = w_ref[...].astype(BF16)

    grid_spec = pltpu.PrefetchScalarGridSpec(
        num_scalar_prefetch=1, grid=(R // tr,),
        in_specs=[pl.BlockSpec((None, tr, C), lambda i, pr: (layer, i, 0))],
        out_specs=pl.BlockSpec((None, None, tr, C), lambda i, pr: (0, pr[0], i, 0)))
    return _pallas(body, name="cast_shard", grid_spec=grid_spec,
                   out_shape=jax.ShapeDtypeStruct((1, N_CHIPS, R, C), BF16),
                   compiler_params=_params("parallel"))(pos, w)


def _gather_weights(big, small):
    nA = len(big)
    half = [a.shape[2] // 2 for a in big]

    def body(*refs):
        ins, outs = refs[:nA + 1], refs[nA + 1:2 * nA + 2]
        ssem, rsem, fsem, frsem, lsem = refs[2 * nA + 2:]
        x, y, c = _mesh_pos()
        k = 2 * x + y
        chips = _other_chips(x, y)
        sib = (x, y, 1 - c)

        def slab(a, chip, core):
            return outs[a].at[:, chip, pl.ds(core * half[a], half[a]), :]

        local = [pltpu.make_async_copy(ins[nA], outs[nA].at[k], lsem.at[0])]
        for cp in local:
            cp.start()
        sends = []
        for j, (ox, oy) in enumerate(chips):
            for a in range(nA):
                sends.append(_remote(slab(a, k, c), slab(a, k, c), ssem.at[a, j], rsem.at[a, j], (ox, oy, c)))
            sends.append(_remote(ins[nA], outs[nA].at[k], ssem.at[nA, j], rsem.at[nA, j], (ox, oy, c)))
        for cp in sends:
            cp.start()
        for j, (ox, oy) in enumerate(chips):
            kj = 2 * ox + oy
            for a in range(nA):
                got = slab(a, kj, c)
                _remote(got, got, ssem.at[a, j], rsem.at[a, j], (ox, oy, c)).wait_recv()
                fw = _remote(got, got, fsem.at[a, j], frsem.at[a, j], sib)
                fw.start()
                sends.append(fw)
            gs = outs[nA].at[kj]
            _remote(gs, gs, ssem.at[nA, j], rsem.at[nA, j], (ox, oy, c)).wait_recv()
        for j, (ox, oy) in enumerate(chips):
            kj = 2 * ox + oy
            for a in range(nA):
                theirs = slab(a, kj, 1 - c)
                _remote(theirs, theirs, fsem.at[a, j], frsem.at[a, j], sib).wait_recv()
        for cp in sends:
            cp.wait_send()
        for cp in local:
            cp.wait()

    out_shape = [jax.ShapeDtypeStruct(a.shape, a.dtype) for a in big]
    out_shape.append(jax.ShapeDtypeStruct((N_CHIPS,) + small.shape, small.dtype))
    scratch = [pltpu.SemaphoreType.DMA((nA + 1, 3)), pltpu.SemaphoreType.DMA((nA + 1, 3)),
               pltpu.SemaphoreType.DMA((nA, 3)), pltpu.SemaphoreType.DMA((nA, 3)), pltpu.SemaphoreType.DMA((1,))]
    return _comm_call(body, "gather_weights", list(big) + [small], out_shape, scratch, {a: a for a in range(nA)})


HBM = pl.BlockSpec(memory_space=pltpu.HBM)
SEM = pl.BlockSpec(memory_space=pltpu.SEMAPHORE)
EFFECT = pltpu.SideEffectType.DATAFLOW_SIDE_EFFECTING


def _split_start(arrays, copies, n, name):
    k = len(arrays)

    def body(*refs):
        for cp in copies(refs[k + 2:2 * k + 2], refs[k], refs[k + 1]):
            cp.start()
        refs[2 * k + 2][...] = jnp.zeros((8, 128), F32)

    out = _pallas(
        body, name=name,
        out_shape=(pltpu.SemaphoreType.DMA((n,)), pltpu.SemaphoreType.DMA((n,)),
                   *[pltpu.HBM(a.shape, a.dtype) for a in arrays], jax.ShapeDtypeStruct((8, 128), F32)),
        in_specs=(HBM,) * k, out_specs=(SEM, SEM) + (HBM,) * k + (pl.BlockSpec(memory_space=pltpu.VMEM),),
        input_output_aliases={i: i + 2 for i in range(k)},
        compiler_params=pltpu.CompilerParams(has_side_effects=EFFECT),
    )(*[pltpu.with_memory_space_constraint(a, pltpu.HBM) for a in arrays])
    return out[0], out[1], list(out[2:2 + k]), out[2 + k]


def _split_wait(ssem, rsem, arrays, copies, after, name):
    k = len(arrays)

    def body(*refs):
        for cp in copies(refs[:k], refs[k], refs[k + 1]):
            cp.wait_send()
            cp.wait_recv()

    out = _pallas(
        body, name=name, out_shape=tuple(pltpu.HBM(a.shape, a.dtype) for a in arrays),
        in_specs=(HBM,) * k + (SEM, SEM, ANY), out_specs=(HBM,) * k, input_output_aliases={i: i for i in range(k)},
        compiler_params=pltpu.CompilerParams(has_side_effects=EFFECT),
    )(*arrays, ssem, rsem, after)
    return list(out)


def _gather_copies(shapes):
    half = [s[2] // 2 for s in shapes]

    def copies(refs, ssem, rsem):
        x, y, c = _mesh_pos()
        out = []
        for j, (ox, oy) in enumerate(_other_chips(x, y)):
            for a, ref in enumerate(refs):
                slab = ref.at[:, 2 * x + y, pl.ds(c * half[a], half[a]), :]
                out.append(_remote(slab, slab, ssem.at[3 * a + j], rsem.at[3 * a + j], (ox, oy, c)))
        return out

    return copies


def _chips_copies(n_arr):
    def copies(refs, ssem, rsem):
        x, y, c = _mesh_pos()
        out = []
        for j, (ox, oy) in enumerate(_other_chips(x, y)):
            for a in range(n_arr):
                out.append(_remote(refs[a].at[:, 2 * ox + oy], refs[n_arr + a].at[:, 2 * x + y], ssem.at[3 * a + j],
                                   rsem.at[3 * a + j], (ox, oy, c)))
        return out

    return copies


def _forward_cores(arrays):
    nA = len(arrays)
    half = [a.shape[2] // 2 for a in arrays]

    def body(*refs):
        outs = refs[nA:2 * nA]
        ssem, rsem = refs[2 * nA:]
        x, y, c = _mesh_pos()
        sib = (x, y, 1 - c)
        sends, waits = [], []
        for j, (ox, oy) in enumerate(_other_chips(x, y)):
            for a in range(nA):
                got = outs[a].at[:, 2 * ox + oy, pl.ds(c * half[a], half[a]), :]
                sends.append(_remote(got, got, ssem.at[a, j], rsem.at[a, j], sib))
                theirs = outs[a].at[:, 2 * ox + oy, pl.ds((1 - c) * half[a], half[a]), :]
                waits.append(_remote(theirs, theirs, ssem.at[a, j], rsem.at[a, j], sib))
        for cp in sends:
            cp.start()
        for cp in waits:
            cp.wait_recv()
        for cp in sends:
            cp.wait_send()

    out_shape = [jax.ShapeDtypeStruct(a.shape, a.dtype) for a in arrays]
    scratch = [pltpu.SemaphoreType.DMA((nA, 3)), pltpu.SemaphoreType.DMA((nA, 3))]
    return _comm_call(body, "forward_cores", list(arrays), out_shape, scratch, {a: a for a in range(nA)})


def _exchange_halves(big):
    nA = len(big)
    half = [a.shape[2] // 2 for a in big]

    def body(*refs):
        ins, outs = refs[:nA], refs[nA:2 * nA]
        ssem, rsem = refs[2 * nA:]
        x, y, c = _mesh_pos()
        sib = (x, y, 1 - c)
        sends = [_remote(ins[a].at[:, :, pl.ds((1 - c) * half[a], half[a]), :], outs[a], ssem.at[a], rsem.at[a], sib)
                 for a in range(nA)]
        for cp in sends:
            cp.start()
        for a in range(nA):
            _remote(outs[a], outs[a], ssem.at[a], rsem.at[a], sib).wait_recv()
        for cp in sends:
            cp.wait_send()

    out_shape = [jax.ShapeDtypeStruct((a.shape[0], N_CHIPS, h, a.shape[3]), a.dtype) for a, h in zip(big, half)]
    scratch = [pltpu.SemaphoreType.DMA((nA,)), pltpu.SemaphoreType.DMA((nA,))]
    return _comm_call(body, "exchange_halves", list(big), out_shape, scratch)


def _exchange_cores(big, small):
    nA = len(big)
    half = [a.shape[2] // 2 for a in big]

    def body(*refs):
        ins, outs = refs[:nA + 1], refs[nA + 1:2 * nA + 2]
        ssem, rsem, ssem2, rsem2, lsem = refs[2 * nA + 2:]
        x, y, c = _mesh_pos()
        me = 4 * x + 2 * y + c
        sib = (x, y, 1 - c)
        peers = _other_devices(x, y, c)
        local = pltpu.make_async_copy(ins[nA], outs[nA].at[me], lsem.at[0])
        local.start()
        sends = []
        for a in range(nA):
            src = ins[a].at[:, :, pl.ds((1 - c) * half[a], half[a]), :]
            sends.append(_remote(src, outs[a], ssem.at[a], rsem.at[a], sib))
        for p, dev in enumerate(peers):
            sends.append(_remote(ins[nA], outs[nA].at[me], ssem2.at[p], rsem2.at[p], dev))
        for cp in sends:
            cp.start()
        for a in range(nA):
            _remote(outs[a], outs[a], ssem.at[a], rsem.at[a], sib).wait_recv()
        for p, (px, py, pc) in enumerate(peers):
            got = outs[nA].at[4 * px + 2 * py + pc]
            _remote(got, got, ssem2.at[p], rsem2.at[p], (px, py, pc)).wait_recv()
        for cp in sends:
            cp.wait_send()
        local.wait()

    out_shape = [jax.ShapeDtypeStruct((a.shape[0], N_CHIPS, h, a.shape[3]), a.dtype) for a, h in zip(big, half)]
    out_shape.append(jax.ShapeDtypeStruct((N_DEV,) + small.shape, small.dtype))
    scratch = [pltpu.SemaphoreType.DMA((nA,)), pltpu.SemaphoreType.DMA((nA,)), pltpu.SemaphoreType.DMA((N_DEV - 1,)),
               pltpu.SemaphoreType.DMA((N_DEV - 1,)), pltpu.SemaphoreType.DMA((1,))]
    return _comm_call(body, "exchange_cores", list(big) + [small], out_shape, scratch)


def _exchange_chips(parts):
    nA = len(parts)

    def body(*refs):
        ins, outs = refs[:nA], refs[nA:2 * nA]
        ssem, rsem = refs[2 * nA:]
        x, y, c = _mesh_pos()
        k = 2 * x + y
        chips = _other_chips(x, y)
        sends = []
        for j, (ox, oy) in enumerate(chips):
            for a in range(nA):
                sends.append(_remote(ins[a].at[:, 2 * ox + oy], outs[a].at[:, k], ssem.at[a, j], rsem.at[a, j],
                                     (ox, oy, c)))
        for cp in sends:
            cp.start()
        for j, (ox, oy) in enumerate(chips):
            for a in range(nA):
                got = outs[a].at[:, 2 * ox + oy]
                _remote(got, got, ssem.at[a, j], rsem.at[a, j], (ox, oy, c)).wait_recv()
        for cp in sends:
            cp.wait_send()

    out_shape = [jax.ShapeDtypeStruct(a.shape, a.dtype) for a in parts]
    scratch = [pltpu.SemaphoreType.DMA((nA, 3)), pltpu.SemaphoreType.DMA((nA, 3))]
    return _comm_call(body, "exchange_chips", list(parts), out_shape, scratch)


def _exchange_final(grads, everywhere):
    nA = len(grads)
    n_remote = sum(N_DEV - 1 if ev else 1 for ev in everywhere)

    def body(*refs):
        outs = refs[nA:2 * nA]
        ssem, rsem = refs[2 * nA:]
        x, y, c = _mesh_pos()
        k = 2 * x + y
        sib = (x, y, 1 - c)
        peers = _other_devices(x, y, c)
        sends, waits = [], []
        s = 0
        for a in range(nA):
            if everywhere[a]:
                r2 = grads[a].shape[1] // N_DEV
                mine = outs[a].at[:, pl.ds((2 * k + c) * r2, r2), :]
                for (px, py, pc) in peers:
                    sends.append(_remote(mine, mine, ssem.at[s], rsem.at[s], (px, py, pc)))
                    got = outs[a].at[:, pl.ds((2 * (2 * px + py) + pc) * r2, r2), :]
                    waits.append(_remote(got, got, ssem.at[s], rsem.at[s], (px, py, pc)))
                    s += 1
            else:
                r2 = grads[a].shape[1] // 2
                mine = outs[a].at[:, pl.ds(c * r2, r2), :]
                sends.append(_remote(mine, mine, ssem.at[s], rsem.at[s], sib))
                got = outs[a].at[:, pl.ds((1 - c) * r2, r2), :]
                waits.append(_remote(got, got, ssem.at[s], rsem.at[s], sib))
                s += 1
        for cp in sends:
            cp.start()
        for cp in waits:
            cp.wait_recv()
        for cp in sends:
            cp.wait_send()

    out_shape = [jax.ShapeDtypeStruct(g.shape, g.dtype) for g in grads]
    scratch = [pltpu.SemaphoreType.DMA((n_remote,)), pltpu.SemaphoreType.DMA((n_remote,))]
    return _comm_call(body, "exchange_final", list(grads), out_shape, scratch, {a: a for a in range(nA)})


BLOCK_BYTES = 4 << 20


def _row_tile(rows, cols, mult=16, limit=BLOCK_BYTES):
    best = mult
    for t in range(mult, rows + 1, mult):
        if rows % t == 0 and t * cols * 4 <= limit:
            best = t
    return best


def _add_cores(own, recv, pos):
    L, _, R, C = own.shape
    r2 = R // 2
    tr = _row_tile(r2, C)
    nb = r2 // tr

    def body(pos_ref, a_ref, r_ref, o_ref):
        o_ref[...] = (a_ref[...].astype(F32) + r_ref[...].astype(F32)).astype(BF16)

    blk = (None, None, tr, C)
    grid_spec = pltpu.PrefetchScalarGridSpec(
        num_scalar_prefetch=1, grid=(L, N_CHIPS, nb),
        in_specs=[pl.BlockSpec(blk, lambda l, s, i, pr: (l, s, pr[1] * nb + i, 0)),
                  pl.BlockSpec(blk, lambda l, s, i, pr: (l, s, i, 0))],
        out_specs=pl.BlockSpec(blk, lambda l, s, i, pr: (l, s, i, 0)))
    return _pallas(body, name="add_cores", grid_spec=grid_spec,
                   out_shape=jax.ShapeDtypeStruct((L, N_CHIPS, r2, C), BF16),
                   compiler_params=_params("parallel", "parallel", "parallel"))(pos, own, recv)


def _sum_chips(own, recv, pos, everywhere, layer, nlayers, prev):
    _, _, r2, C = own.shape
    tr = _row_tile(r2, 2 * C)
    nb = r2 // tr

    def body(pos_ref, a_ref, r_ref, *rest):
        acc = None
        for s in range(N_CHIPS):
            term = jnp.where(pos_ref[0] == s, a_ref[...], r_ref[s]).astype(F32)
            acc = term if acc is None else acc + term
        rest[-1][...] = acc

    if everywhere:
        def out_map(i, pr):
            return (layer, (2 * pr[0] + pr[1]) * nb + i, 0)
    else:
        def out_map(i, pr):
            return (layer, pr[1] * nb + i, 0)

    in_specs = [pl.BlockSpec((None, None, tr, C), lambda i, pr: (0, pr[0], i, 0)),
                pl.BlockSpec((None, N_CHIPS, tr, C), lambda i, pr: (0, 0, i, 0))]
    grid_spec = pltpu.PrefetchScalarGridSpec(
        num_scalar_prefetch=1, grid=(nb,), in_specs=in_specs + ([] if prev is None else [ANY]),
        out_specs=pl.BlockSpec((None, tr, C), out_map))
    rows = (N_DEV if everywhere else 2) * r2
    args = (pos, own, recv) if prev is None else (pos, own, recv, prev)
    return _pallas(body, name="sum_chips", grid_spec=grid_spec, out_shape=jax.ShapeDtypeStruct((nlayers, rows, C), F32),
                   input_output_aliases={} if prev is None else {3: 0},
                   compiler_params=_params("parallel"))(*args)


def _sum_devices(parts):
    n, R, C = parts.shape
    tr = _row_tile(R, C * n, 8)

    def body(p_ref, o_ref):
        acc = p_ref[0]
        for s in range(1, n):
            acc = acc + p_ref[s]
        o_ref[...] = acc

    return _pallas(body, name="sum_devices", grid=(R // tr,), in_specs=[pl.BlockSpec((n, tr, C), lambda i: (0, i, 0))],
                   out_specs=pl.BlockSpec((tr, C), lambda i: (i, 0)), out_shape=jax.ShapeDtypeStruct((R, C), F32),
                   compiler_params=_params("parallel"))(parts)


def _adamw(w, g, m, v, name):
    L, R, C = w.shape
    tr = _row_tile(R, C, 8, BLOCK_BYTES // 2)

    def body(w_ref, g_ref, m_ref, v_ref, d_ref, m2_ref, v2_ref):
        gg = g_ref[...]
        m2 = ADAM_B1 * m_ref[...] + (1.0 - ADAM_B1) * gg
        v2 = ADAM_B2 * v_ref[...] + (1.0 - ADAM_B2) * (gg * gg)
        m_hat = m2 / (1.0 - ADAM_B1 ** ADAM_STEP)
        v_hat = v2 / (1.0 - ADAM_B2 ** ADAM_STEP)
        d_ref[...] = -ADAM_LR * (m_hat / (jnp.sqrt(v_hat) + ADAM_EPS) + ADAM_WD * w_ref[...])
        m2_ref[...] = m2
        v2_ref[...] = v2

    blk = pl.BlockSpec((1, tr, C), lambda l, i: (l, i, 0))
    shp = jax.ShapeDtypeStruct((L, R, C), F32)
    return _pallas(body, name=name, grid=(L, R // tr), in_specs=[blk] * 4, out_specs=[blk] * 3, out_shape=[shp] * 3,
                   compiler_params=_params("parallel", "parallel"))(w, g, m, v)


WEIGHTS = ("norm_even", "w_in_even", "conv_a_w", "conv_a_b", "ln_a_g", "ln_a_b", "pool_w", "pool_b", "pool_scale",
           "w_out_even", "norm_odd", "w_in_odd", "conv_c_w", "conv_c_b", "w_rg", "b_rg", "w_ig", "b_ig", "lru_lambda",
           "w_out_odd", "final_norm")
BIG = ("w_in_even", "w_out_even", "pool_w", "w_in_odd", "w_out_odd", "w_rg", "w_ig")
SMALL = tuple(n for n in WEIGHTS if n not in BIG)
SMALL_SHARDED = ("conv_a_w", "pool_b", "norm_odd", "conv_c_w", "conv_c_b", "b_rg", "b_ig", "lru_lambda")


def _pack(arrs):
    flat = jnp.concatenate([a.reshape(-1) for a in arrs])
    rows = -(-flat.shape[0] // (64 * 128)) * 64
    return jnp.pad(flat, (0, rows * 128 - flat.shape[0])).reshape(rows, 128)


def _unpack(buf, shapes, lead=()):
    flat = buf.reshape(tuple(lead) + (-1,))
    out, o = [], 0
    for s in shapes:
        n = 1
        for d in s:
            n *= d
        out.append(flat[..., o:o + n].reshape(tuple(lead) + tuple(s)))
        o += n
    return out


def _shard(full, axis, k):
    n = full.shape[axis] // N_CHIPS
    return lax.dynamic_slice_in_dim(full, k * n, n, axis)


def kernel(x, norm_even, w_in_even, conv_a_w, conv_a_b, ln_a_g, ln_a_b, pool_w, pool_b, pool_scale, w_out_even, norm_odd, w_in_odd, conv_c_w, conv_c_b, w_rg, b_rg, w_ig, b_ig, lru_lambda, w_out_odd, final_norm, loss_target, m_norm_even, m_w_in_even, m_conv_a_w, m_conv_a_b, m_ln_a_g, m_ln_a_b, m_pool_w, m_pool_b, m_pool_scale, m_w_out_even, m_norm_odd, m_w_in_odd, m_conv_c_w, m_conv_c_b, m_w_rg, m_b_rg, m_w_ig, m_b_ig, m_lru_lambda, m_w_out_odd, m_final_norm, v_norm_even, v_w_in_even, v_conv_a_w, v_conv_a_b, v_ln_a_g, v_ln_a_b, v_pool_w, v_pool_b, v_pool_scale, v_w_out_even, v_norm_odd, v_w_in_odd, v_conv_c_w, v_conv_c_b, v_w_rg, v_b_rg, v_w_ig, v_b_ig, v_lru_lambda, v_w_out_odd, v_final_norm):
    P = dict(locals())
    xi, yi, ci = _mesh_pos()
    k = 2 * xi + yi
    L = w_in_even.shape[0]
    D = D_MODEL

    pos = jnp.stack([k, ci]).astype(jnp.int32)
    depth = 2 * L
    pool_w3 = pool_w.reshape(L, 4 * 64, POOL_GW)

    def cast_group(layer):
        j = layer // 2
        if layer % 2 == 0:
            return [_cast_shard(w_in_even, j, pos), _cast_shard(w_out_even, j, pos), _cast_shard(pool_w3, j, pos)]
        return [_cast_shard(w_in_odd, j, pos), _cast_shard(w_out_odd, j, pos)]

    *group, g_small = _gather_weights(cast_group(0), _pack([P[n] for n in SMALL_SHARDED]))
    full = {}
    for n, a in zip(SMALL_SHARDED, _unpack(g_small, [P[n].shape for n in SMALL_SHARDED], lead=(N_CHIPS,))):
        a = jnp.moveaxis(a, 0, -2)
        full[n] = a.reshape(a.shape[:-2] + (N_CHIPS * a.shape[-1],))

    def small_weights(layer, group):
        j = layer // 2
        if layer % 2 == 0:
            cw = full["conv_a_w"][j]
            pw = group[2].reshape(N_CHIPS, 4, 64, POOL_GW).transpose(1, 0, 2, 3).reshape(4, POOL_GW, POOL_GW)
            return dict(norm=norm_even[j][None], conv_w=_pad_rows(cw, 32), conv_w_rev=_pad_rows(cw[::-1], 32),
                        conv_b=conv_a_b[j][None], ln_g=ln_a_g[j][None], ln_b=ln_a_b[j][None], pool_w=pw,
                        pool_b=full["pool_b"][j].reshape(1, D), pool_scale=pool_scale[j][None])
        return dict(norm=full["norm_odd"][j][None], conv_w=_pad_rows(full["conv_c_w"][j], 8),
                    conv_b=full["conv_c_b"][j][None], w_rg=w_rg[j].astype(BF16), b_rg=full["b_rg"][j][None],
                    w_ig=w_ig[j].astype(BF16), b_ig=full["b_ig"][j][None], lam=full["lru_lambda"][j][None])

    no_token = jnp.zeros((8, 128), F32)
    h = x[0]
    saved, big_w, small_w = [], [], []
    for layer in range(depth):
        token = no_token
        if layer + 1 < depth:
            nxt = cast_group(layer + 1)
            copies = _gather_copies([a.shape for a in nxt])
            ssem, rsem, nxt, token = _split_start(nxt, copies, 3 * len(nxt), "gather_start%d" % (layer + 1))
        small_w.append(small_weights(layer, group))
        big_w.append((group[0], group[1].reshape(1, -1, D)))
        h, sv = _layer_fwd(layer % 2 == 0, h, small_w[layer], *big_w[layer], token)
        saved.append(sv)
        if layer + 1 < depth:
            group = _forward_cores(_split_wait(ssem, rsem, nxt, copies, h, "gather_wait%d" % (layer + 1)))

    dh, dhb, d_final, loss = _loss_head(h, final_norm[None], loss_target[0])
    loss = lax.psum(loss[0, 0], ("x", "y", "c"))
    everywhere = [False, False, False, False, False, True, True]
    final = [None] * len(everywhere)
    small_of = [None] * depth
    recv_small = None

    def finish(pair, land, slots, j):
        for a, r, s in zip(pair, land, slots):
            final[s] = _sum_chips(a, r, pos, everywhere[s], j, L, final[s])

    pending = None
    token = no_token
    for layer in reversed(range(depth)):
        j = layer // 2
        dh, dhb, dw_in, dw_out, sm = _layer_bwd(layer % 2 == 0, saved[layer], small_w[layer], *big_w[layer], dh, dhb,
                                                token)
        small_of[layer] = sm
        if pending is not None:
            ssem, rsem, arrs, copies, slots, pj, pl_ = pending
            arrs = _split_wait(ssem, rsem, arrs, copies, dh, "chips_wait%d" % pl_)
            finish(arrs[:len(slots)], arrs[len(slots):], slots, pj)
            pending = None
        if layer % 2 == 0:
            dpw = sm["pool_w"].reshape(4, N_CHIPS, 64, POOL_GW).transpose(1, 0, 2, 3)
            parts = [dw_in, dw_out.reshape(1, N_CHIPS, -1, D), dpw.reshape(1, N_CHIPS, 4 * 64, POOL_GW).astype(BF16)]
            slots = [0, 1, 2]
        else:
            parts = [dw_in, dw_out.reshape(1, N_CHIPS, -1, D),
                     sm["w_rg"].reshape(1, N_CHIPS, -1, LRU_HD).astype(BF16),
                     sm["w_ig"].reshape(1, N_CHIPS, -1, LRU_HD).astype(BF16)]
            slots = [3, 4, 5, 6]
        if layer > 0:
            recv = _exchange_halves(parts)
        else:
            small_g = []
            for jj in range(L):
                ge, go = small_of[2 * jj], small_of[2 * jj + 1]
                small_g += [ge["conv_w"].reshape(32, 8, D).sum(axis=1)[:CONV_K], ge["vec"][0:5], ge["norm"], go["vec"],
                            go["norm"]]
            small_g.append(d_final)
            small_shapes = [a.shape for a in small_g]
            *recv, recv_small = _exchange_cores(parts, _pack(small_g))
        pair = [_add_cores(a, r, pos) for a, r in zip(parts, recv)]
        if layer > 0:
            copies = _chips_copies(len(pair))
            land = [lax.empty(a.shape, a.dtype) for a in pair]
            ssem, rsem, arrs, token = _split_start(pair + land, copies, 3 * len(pair), "chips_start%d" % layer)
            pending = (ssem, rsem, arrs, copies, slots, j, layer)
        else:
            finish(pair, _exchange_chips(pair), slots, j)
    grad_x = dh
    gw = _exchange_final(final, everywhere)
    sg = _unpack(_sum_devices(recv_small), small_shapes)

    grads = dict(w_in_even=gw[0], w_out_even=gw[1], pool_w=gw[2].reshape(pool_w.shape), w_in_odd=gw[3], w_out_odd=gw[4],
                 w_rg=gw[5].reshape(w_rg.shape), w_ig=gw[6].reshape(w_ig.shape), final_norm=sg[-1][0])
    ev = [sg[5 * j + 1] for j in range(L)]
    ov = [sg[5 * j + 3] for j in range(L)]
    grads["conv_a_w"] = _shard(jnp.stack([sg[5 * j] for j in range(L)]), 2, k)
    grads["norm_even"] = jnp.stack([sg[5 * j + 2][0] for j in range(L)])
    grads["norm_odd"] = _shard(jnp.stack([sg[5 * j + 4][0] for j in range(L)]), 1, k)
    for r, n in enumerate(("conv_a_b", "ln_a_g", "ln_a_b", "pool_scale")):
        grads[n] = jnp.stack([e[r] for e in ev])
    grads["pool_b"] = _shard(jnp.stack([e[4].reshape(4, POOL_GW) for e in ev]), 2, k)
    grads["conv_c_w"] = _shard(jnp.stack([o[0:4] for o in ov]), 2, k)
    for r, n in zip((4, 5, 6, 7), ("conv_c_b", "b_rg", "b_ig", "lru_lambda")):
        grads[n] = _shard(jnp.stack([o[r] for o in ov]), 1, k)

    delta, new_m, new_v = {}, {}, {}
    for n in BIG:
        s3 = (L, -1, P[n].shape[-1])
        d, m2, v2 = _adamw(P[n].reshape(s3), grads[n].reshape(s3), P["m_" + n].reshape(s3), P["v_" + n].reshape(s3), "adamw")
        delta[n], new_m[n], new_v[n] = d.reshape(P[n].shape), m2.reshape(P[n].shape), v2.reshape(P[n].shape)
    shapes = [P[n].shape for n in SMALL]
    packed = [_pack([src[n] for n in SMALL])[None] for src in
              (P, grads, {n: P["m_" + n] for n in SMALL}, {n: P["v_" + n] for n in SMALL})]
    for res, out in zip(_adamw(*packed, "adamw_small"), (delta, new_m, new_v)):
        for n, a in zip(SMALL, _unpack(res[0], shapes)):
            out[n] = a

    return (loss, grad_x[None], *[grads[n] for n in WEIGHTS], *[delta[n] for n in WEIGHTS],
            *[new_m[n] for n in WEIGHTS], *[new_v[n] for n in WEIGHTS])
```

```python
import functools

import jax
import jax.numpy as jnp
from jax import lax
from jax.experimental import pallas as pl
from jax.experimental.pallas import tpu as pltpu

F32 = jnp.float32
BF16 = jnp.bfloat16
MESH = pl.DeviceIdType.MESH

D_MODEL = 1024
N_CHIPS = 4
N_DEV = 8
EPS_RMS = 1e-6
EPS_LN = 1e-5
CONV_K = 31
POOL_WINDOWS = (2, 4, 8, 16)
POOL_GW = 256
LRU_HEADS = 12
LRU_HD = 128
W_LRU = LRU_HEADS * LRU_HD
LRU_CONV_K = 4
LRU_C = 8.0
ADAM_LR = 0.001
ADAM_B1 = 0.9
ADAM_B2 = 0.999
ADAM_EPS = 1e-08
ADAM_WD = 0.01
ADAM_STEP = 10

VMEM_LIMIT_BYTES = 56 * 1024 * 1024
ROW_TILE = 512
MIX_TILE = 256
EVEN_HALO = 32
ODD_HALO = 8


def _pallas(body, **kw):
    return pl.pallas_call(body, **kw)


def _params(*sem):
    return pltpu.CompilerParams(dimension_semantics=sem if sem else None, vmem_limit_bytes=VMEM_LIMIT_BYTES)


def _sigmoid(x):
    return 0.5 * jnp.tanh(0.5 * x) + 0.5


def _dsilu(x, s):
    return s * (1.0 + x * (1.0 - s))


def _nt(a, b):
    return lax.dot_general(a, b, (((1,), (1,)), ((), ())), preferred_element_type=F32)


def _tn(a, b):
    return lax.dot_general(a, b, (((0,), (0,)), ((), ())), preferred_element_type=F32)


def _in_proj(h, g, wg, layer, after, name):
    T, D = h.shape
    _, nblk, _, nb = wg.shape

    nrow = T // ROW_TILE

    def body(h_ref, g_ref, w_ref, after_ref, p_ref, n_ref, n_all):
        j, i = pl.program_id(0), pl.program_id(1)

        @pl.when(j == 0)
        def _():
            x = h_ref[...]
            r = lax.rsqrt(jnp.mean(x * x, axis=-1, keepdims=True) + EPS_RMS)
            nn = (x * r * g_ref[...]).astype(BF16)
            n_ref[...] = nn
            n_all[i] = nn

        p_ref[...] = jnp.dot(n_all[i], w_ref[0], preferred_element_type=F32)

    def rows_once(j, i):
        return (jnp.where(j == 0, i, nrow - 1), 0)

    return _pallas(
        body, name=name, grid=(nblk, nrow),
        in_specs=[pl.BlockSpec((ROW_TILE, D), rows_once), pl.BlockSpec((1, D), lambda j, i: (0, 0)),
                  pl.BlockSpec((None, 1, D, nb), lambda j, i: (layer, j, 0, 0)),
                  pl.BlockSpec((8, 128), lambda j, i: (0, 0))],
        out_specs=[pl.BlockSpec((ROW_TILE, nb), lambda j, i: (i, j)), pl.BlockSpec((ROW_TILE, D), rows_once)],
        out_shape=[jax.ShapeDtypeStruct((T, nblk * nb), F32), jax.ShapeDtypeStruct((T, D), BF16)],
        scratch_shapes=[pltpu.VMEM((nrow, ROW_TILE, D), BF16)],
        compiler_params=_params("arbitrary", "arbitrary"))(h, g, wg, after)


def _out_proj(y, w, layer, hres, name):
    T, K = y.shape
    D = w.shape[2]

    def body(y_ref, w_ref, r_ref, o_ref):
        o_ref[...] = r_ref[...] + jnp.dot(y_ref[...], w_ref[...], preferred_element_type=F32)

    return _pallas(
        body, name=name, grid=(T // ROW_TILE,),
        in_specs=[pl.BlockSpec((ROW_TILE, K), lambda i: (i, 0)), pl.BlockSpec((None, K, D), lambda i: (layer, 0, 0)),
                  pl.BlockSpec((ROW_TILE, D), lambda i: (i, 0))],
        out_specs=pl.BlockSpec((ROW_TILE, D), lambda i: (i, 0)),
        out_shape=jax.ShapeDtypeStruct((T, D), F32),
        compiler_params=_params("parallel"))(y, w, hres)


def _dn_proj(dp, wg, layer, h, g, dres, name):
    T, D = h.shape
    _, nblk, _, nb = wg.shape

    nrow = T // ROW_TILE

    def body(dp_ref, w_ref, h_ref, g_ref, dres_ref, dh_ref, dhb_ref, dg_ref, acc_ref):
        j, i = pl.program_id(0), pl.program_id(1)
        part = _nt(dp_ref[...], w_ref[0])

        @pl.when(j == 0)
        def _():
            acc_ref[i] = part

        @pl.when(j > 0)
        def _():
            acc_ref[i] += part

        @pl.when(j == nblk - 1)
        def _():
            x = h_ref[...]
            r = lax.rsqrt(jnp.mean(x * x, axis=-1, keepdims=True) + EPS_RMS)
            dn = acc_ref[i]
            q = dn * g_ref[...]
            dh = dres_ref[...] + r * q - x * ((r * r * r) * jnp.mean(q * x, axis=-1, keepdims=True))
            dh_ref[...] = dh
            dhb_ref[...] = dh.astype(BF16)
            dgp = jnp.sum(dn * (x * r), axis=0, keepdims=True)

            @pl.when(i == 0)
            def _():
                dg_ref[...] = dgp

            @pl.when(i > 0)
            def _():
                dg_ref[...] += dgp

    def rows_last(j, i):
        return (jnp.where(j == nblk - 1, i, 0), 0)

    return _pallas(
        body, name=name, grid=(nblk, nrow),
        in_specs=[pl.BlockSpec((ROW_TILE, nb), lambda j, i: (i, j)),
                  pl.BlockSpec((None, 1, D, nb), lambda j, i: (layer, j, 0, 0)),
                  pl.BlockSpec((ROW_TILE, D), rows_last), pl.BlockSpec((1, D), lambda j, i: (0, 0)),
                  pl.BlockSpec((ROW_TILE, D), rows_last)],
        out_specs=[pl.BlockSpec((ROW_TILE, D), rows_last), pl.BlockSpec((ROW_TILE, D), rows_last),
                   pl.BlockSpec((1, D), lambda j, i: (0, 0))],
        out_shape=[jax.ShapeDtypeStruct((T, D), F32), jax.ShapeDtypeStruct((T, D), BF16),
                   jax.ShapeDtypeStruct((1, D), F32)],
        scratch_shapes=[pltpu.VMEM((nrow, ROW_TILE, D), F32)],
        compiler_params=_params("arbitrary", "arbitrary"))(dp, wg, h, g, dres)


def _dw_in(n, dp, nblk, layer, nlayers, prev, name):
    T, D = n.shape
    nb = dp.shape[1] // nblk
    ta = 512

    def body(n_ref, dp_ref, *rest):
        rest[-1][0] = _tn(n_ref[...], dp_ref[...]).astype(BF16)

    in_specs = [pl.BlockSpec((T, ta), lambda j, i: (0, i)), pl.BlockSpec((T, nb), lambda j, i: (0, j))]
    args = (n, dp) if prev is None else (n, dp, prev)
    return _pallas(
        body, name=name, grid=(nblk, D // ta), in_specs=in_specs + ([] if prev is None else [ANY]),
        out_specs=pl.BlockSpec((None, 1, ta, nb), lambda j, i: (layer, j, i, 0)),
        out_shape=jax.ShapeDtypeStruct((nlayers, nblk, D, nb), BF16),
        input_output_aliases={} if prev is None else {2: 0},
        compiler_params=_params("parallel", "parallel"))(*args)


def _dw_out(y, dout, layer, nlayers, prev, name):
    T, K = y.shape
    D = dout.shape[1]
    tk = 512

    def body(y_ref, d_ref, *rest):
        rest[-1][...] = _tn(y_ref[...], d_ref[...]).astype(BF16)

    in_specs = [pl.BlockSpec((T, tk), lambda i: (0, i)), pl.BlockSpec((T, D), lambda i: (0, 0))]
    args = (y, dout) if prev is None else (y, dout, prev)
    return _pallas(
        body, name=name, grid=(K // tk,), in_specs=in_specs + ([] if prev is None else [ANY]),
        out_specs=pl.BlockSpec((None, tk, D), lambda i: (layer, i, 0)),
        out_shape=jax.ShapeDtypeStruct((nlayers, K, D), BF16),
        input_output_aliases={} if prev is None else {2: 0},
        compiler_params=_params("parallel"))(*args)


def _loss_head(h, g, tgt):
    T, D = h.shape
    tm = MIX_TILE

    def body(h_ref, g_ref, t_ref, dh_ref, dhb_ref, dg_ref, loss_ref):
        i = pl.program_id(0)
        x = h_ref[...]
        gg = g_ref[...]
        r = lax.rsqrt(jnp.mean(x * x, axis=-1, keepdims=True) + EPS_RMS)
        xr = x * r
        e = xr * gg - t_ref[...]
        lp = 0.5 * jnp.sum(jnp.mean(e * e, axis=-1, keepdims=True), axis=0, keepdims=True)
        dn = e * (1.0 / D)
        q = dn * gg
        dh = r * q - x * ((r * r * r) * jnp.mean(q * x, axis=-1, keepdims=True))
        dh_ref[...] = dh
        dhb_ref[...] = dh.astype(BF16)
        dgp = jnp.sum(dn * xr, axis=0, keepdims=True)

        @pl.when(i == 0)
        def _():
            dg_ref[...] = dgp
            loss_ref[...] = lp

        @pl.when(i > 0)
        def _():
            dg_ref[...] += dgp
            loss_ref[...] += lp

    return _pallas(
        body, name="loss_head", grid=(T // tm,),
        in_specs=[pl.BlockSpec((tm, D), lambda i: (i, 0)), pl.BlockSpec((1, D), lambda i: (0, 0)),
                  pl.BlockSpec((tm, D), lambda i: (i, 0))],
        out_specs=[pl.BlockSpec((tm, D), lambda i: (i, 0)), pl.BlockSpec((tm, D), lambda i: (i, 0)),
                   pl.BlockSpec((1, D), lambda i: (0, 0)), pl.BlockSpec((1, 1), lambda i: (0, 0))],
        out_shape=[jax.ShapeDtypeStruct((T, D), F32), jax.ShapeDtypeStruct((T, D), BF16),
                   jax.ShapeDtypeStruct((1, D), F32), jax.ShapeDtypeStruct((1, 1), F32)],
        compiler_params=_params("arbitrary"))(h, g, tgt)


def _shift_up(x, j):
    return x if j == 0 else pltpu.roll(x, x.shape[0] - j, 0)


def _shift_down(x, j):
    return x if j == 0 else pltpu.roll(x, j, 0)


def _fill_shifted(dst_ref, src_ref):
    rows = dst_ref.shape[1]
    for s in range(8):
        dst_ref[s] = src_ref[pl.ds(s, rows), :]


def _fill_taps(wb_ref, w_ref):
    for k in range(w_ref.shape[0]):
        wb_ref[k] = jnp.broadcast_to(w_ref[k:k + 1, :], wb_ref.shape[1:])


def _tap_sum(sh_ref, wb_ref, r0, nrows, offsets):
    accs = [None] * (nrows // 8)
    for k, o in enumerate(offsets):
        wk = wb_ref[k]
        for u in range(nrows // 8):
            term = wk * sh_ref[o % 8, pl.ds(r0 + (o // 8) * 8 + 8 * u, 8), :]
            accs[u] = term if accs[u] is None else accs[u] + term
    return jnp.concatenate(accs, axis=0)


def _pool_sums(vx, up):
    sh = _shift_up if up else _shift_down
    outs = []
    for gi, w in enumerate(POOL_WINDOWS):
        s = vx[:, gi * POOL_GW:(gi + 1) * POOL_GW]
        j = 1
        while j < w:
            s = s + sh(s, j)
            j *= 2
        outs.append(s)
    return outs


def _inv_count(row0, nrows):
    pos = (row0 + 1 + lax.broadcasted_iota(jnp.int32, (nrows, 1), 0)).astype(F32)
    return [1.0 / jnp.minimum(pos, float(w)) for w in POOL_WINDOWS]


def _even_mixer_fwd(p, cw, cb, lg, lb, pw, pb, sc, name):
    T = p.shape[0]
    C = D_MODEL
    tT, HL = MIX_TILE, EVEN_HALO
    hb = tT // HL
    chunk = 32

    def body(pm_ref, ph_ref, cw_ref, cb_ref, lg_ref, lb_ref, pw_ref, pb_ref, sc_ref, y_ref, u1_ref, u0x_ref, sh_ref,
             wb_ref):
        i = pl.program_id(0)
        keep = (i > 0).astype(F32)

        @pl.when(i == 0)
        def _():
            _fill_taps(wb_ref, cw_ref)

        u0x_ref[0:HL] = ph_ref[:, 0:C] * _sigmoid(ph_ref[:, C:2 * C]) * keep
        u0x_ref[HL:HL + tT] = pm_ref[:, 0:C] * _sigmoid(pm_ref[:, C:2 * C])
        u0x_ref[HL + tT:HL + tT + 8] = jnp.zeros((8, C), F32)
        _fill_shifted(sh_ref, u0x_ref)
        offs = [HL - (CONV_K - 1) + k for k in range(CONV_K)]

        def conv_chunk(c, carry):
            r0 = pl.multiple_of(c * chunk, chunk)
            u1_ref[pl.ds(r0, chunk), :] = _tap_sum(sh_ref, wb_ref, r0, chunk, offs) + cb_ref[...]
            return carry

        lax.fori_loop(0, tT // chunk, conv_chunk, 0)
        u1 = u1_ref[...]
        mu = jnp.mean(u1, axis=-1, keepdims=True)
        xc = u1 - mu
        rs = lax.rsqrt(jnp.mean(xc * xc, axis=-1, keepdims=True) + EPS_LN)
        u2 = xc * rs * lg_ref[...] + lb_ref[...]
        u3 = u2 * _sigmoid(u2)
        ag = pm_ref[:, 2 * C:3 * C]
        y_ref[:, 0:C] = (u3 * (ag * _sigmoid(ag))).astype(BF16)
        vx = jnp.concatenate([ph_ref[:, 3 * C:4 * C] * keep, pm_ref[:, 3 * C:4 * C]], axis=0)
        sums = _pool_sums(vx, up=False)
        inv = _inv_count(i * tT, tT)
        for gi in range(len(POOL_WINDOWS)):
            cols = slice(gi * POOL_GW, (gi + 1) * POOL_GW)
            d0 = sums[gi][HL:] * inv[gi] - vx[HL:, cols]
            d1 = jnp.dot(d0.astype(BF16), pw_ref[gi], preferred_element_type=F32) + pb_ref[:, cols]
            bg = pm_ref[:, 4 * C + gi * POOL_GW:4 * C + (gi + 1) * POOL_GW]
            y_ref[:, C + gi * POOL_GW:C + (gi + 1) * POOL_GW] = (d1 * sc_ref[:, cols] * (bg * _sigmoid(bg))).astype(BF16)

    vec = pl.BlockSpec((1, C), lambda i: (0, 0))
    return _pallas(
        body, name=name, grid=(T // tT,),
        in_specs=[pl.BlockSpec((tT, 5 * C), lambda i: (i, 0)),
                  pl.BlockSpec((HL, 5 * C), lambda i: (jnp.maximum(i * hb - 1, 0), 0)),
                  pl.BlockSpec((32, C), lambda i: (0, 0)), vec, vec, vec,
                  pl.BlockSpec((4, POOL_GW, POOL_GW), lambda i: (0, 0, 0)), vec, vec],
        out_specs=[pl.BlockSpec((tT, 2 * C), lambda i: (i, 0)), pl.BlockSpec((tT, C), lambda i: (i, 0))],
        out_shape=[jax.ShapeDtypeStruct((T, 2 * C), BF16), jax.ShapeDtypeStruct((T, C), F32)],
        scratch_shapes=[pltpu.VMEM((HL + tT + 8, C), F32), pltpu.VMEM((8, HL + tT, C), F32),
                        pltpu.VMEM((32, 8, C), F32)],
        compiler_params=_params("arbitrary"))(p, p, cw, cb, lg, lb, pw, pb, sc)


def _even_mixer_bwd(p, u1, dout, w_out, after, cwr, lg, lb, pw, pb, sc, name):
    T = p.shape[0]
    C = D_MODEL
    tT, HL = MIX_TILE, EVEN_HALO
    hb = tT // HL
    nT = T // tT
    R1 = tT + HL
    chunk = 32

    def body(pm_ref, pp_ref, pn_ref, u1m_ref, u1n_ref, dom_ref, don_ref, wo_ref, after_ref, cwr_ref, lg_ref, lb_ref,
             pw_ref, pb_ref, sc_ref, dp_ref, dcw_ref, dvec_ref, dpw_ref, x_ref, sh_ref, du0_ref, wb_ref):
        i = pl.program_id(0)
        dy = _nt(jnp.concatenate([dom_ref[...], don_ref[...]], axis=0), wo_ref[...])

        @pl.when(i == 0)
        def _():
            _fill_taps(wb_ref, cwr_ref)

        keep_prev = (i > 0).astype(F32)
        keep_next = (i < nT - 1).astype(F32)
        row = lax.broadcasted_iota(jnp.int32, (R1, 1), 0)
        live = jnp.where(row < tT, 1.0, keep_next)

        def cat(m, n):
            return jnp.concatenate([m, n], axis=0)

        u1 = cat(u1m_ref[...], u1n_ref[...])
        mu = jnp.mean(u1, axis=-1, keepdims=True)
        xc = u1 - mu
        rs = lax.rsqrt(jnp.mean(xc * xc, axis=-1, keepdims=True) + EPS_LN)
        xh = xc * rs
        u2 = xh * lg_ref[...] + lb_ref[...]
        s2 = _sigmoid(u2)
        u3 = u2 * s2
        ag = cat(pm_ref[:, 2 * C:3 * C], pn_ref[:, 2 * C:3 * C])
        sa = _sigmoid(ag)
        dya = dy[:, 0:C]
        dp_ref[:, 2 * C:3 * C] = (dya * u3 * _dsilu(ag, sa))[0:tT].astype(BF16)
        du2 = dya * (ag * sa) * _dsilu(u2, s2)
        dlg = jnp.sum((du2 * xh)[0:tT], axis=0, keepdims=True)
        dlb = jnp.sum(du2[0:tT], axis=0, keepdims=True)
        dxh = du2 * lg_ref[...]
        du1 = rs * (dxh - jnp.mean(dxh, axis=-1, keepdims=True) - xh * jnp.mean(dxh * xh, axis=-1, keepdims=True))
        du1 = du1 * live
        dcb = jnp.sum(du1[0:tT], axis=0, keepdims=True)
        x_ref[0:R1] = du1
        x_ref[R1:R1 + 8] = jnp.zeros((8, C), F32)
        _fill_shifted(sh_ref, x_ref)

        def du0_chunk(c, carry):
            r0 = pl.multiple_of(c * chunk, chunk)
            du0_ref[pl.ds(r0, chunk), :] = _tap_sum(sh_ref, wb_ref, r0, chunk, list(range(CONV_K)))
            return carry

        lax.fori_loop(0, tT // chunk, du0_chunk, 0)
        av, agl = pm_ref[:, 0:C], pm_ref[:, C:2 * C]
        sg = _sigmoid(agl)
        du0 = du0_ref[...]
        dp_ref[:, 0:C] = (du0 * sg).astype(BF16)
        dp_ref[:, C:2 * C] = (du0 * av * sg * (1.0 - sg)).astype(BF16)
        du0_ref[...] = du1[0:tT]
        x_ref[0:HL] = pp_ref[:, 0:C] * _sigmoid(pp_ref[:, C:2 * C]) * keep_prev
        x_ref[HL:HL + tT] = av * sg
        x_ref[HL + tT:HL + tT + 8] = jnp.zeros((8, C), F32)
        _fill_shifted(sh_ref, x_ref)

        @pl.when(i == 0)
        def _():
            dcw_ref[...] = jnp.zeros_like(dcw_ref)

        for k in range(CONV_K):
            o = HL - (CONV_K - 1) + k

            def dw_chunk(c, acc, o=o):
                r0 = pl.multiple_of(c * 64, 64)
                for u in range(0, 64, 8):
                    acc = acc + du0_ref[pl.ds(r0 + u, 8), :] * sh_ref[o % 8, pl.ds(r0 + u + (o // 8) * 8, 8), :]
                return acc

            dcw_ref[8 * k:8 * k + 8, :] += lax.fori_loop(0, tT // 64, dw_chunk, jnp.zeros((8, C), F32))

        bg = cat(pm_ref[:, 4 * C:5 * C], pn_ref[:, 4 * C:5 * C])
        sb = _sigmoid(bg)
        dyb = dy[:, C:2 * C]
        dyb0 = dyb * (bg * sb)
        dd1 = dyb0 * sc_ref[...]
        dpb = jnp.sum(dd1[0:tT], axis=0, keepdims=True)
        inv1 = _inv_count(i * tT, R1)
        z_parts, dd0_parts = [], []
        for gi in range(len(POOL_WINDOWS)):
            cols = slice(gi * POOL_GW, (gi + 1) * POOL_GW)
            dd0 = _nt(dd1[:, cols].astype(BF16), pw_ref[gi])
            dd0_parts.append(dd0)
            z_parts.append(dd0 * inv1[gi] * live)
        fsum = _pool_sums(jnp.concatenate(z_parts, axis=1), up=True)
        vx = cat(pp_ref[:, 3 * C:4 * C] * keep_prev, pm_ref[:, 3 * C:4 * C])
        sums = _pool_sums(vx, up=False)
        inv0 = _inv_count(i * tT, tT)
        dsc_parts = []
        for gi in range(len(POOL_WINDOWS)):
            cols = slice(gi * POOL_GW, (gi + 1) * POOL_GW)
            dp_ref[:, 3 * C + gi * POOL_GW:3 * C + (gi + 1) * POOL_GW] = (fsum[gi][0:tT] - dd0_parts[gi][0:tT]).astype(BF16)
            d0 = (sums[gi][HL:] * inv0[gi] - vx[HL:, cols]).astype(BF16)
            d1 = jnp.dot(d0, pw_ref[gi], preferred_element_type=F32) + pb_ref[:, cols]
            bgm, sbm = bg[0:tT, cols], sb[0:tT, cols]
            dp_ref[:, 4 * C + gi * POOL_GW:4 * C + (gi + 1) * POOL_GW] = (
                dyb[0:tT, cols] * d1 * sc_ref[:, cols] * _dsilu(bgm, sbm)).astype(BF16)
            dsc_parts.append(jnp.sum(dyb0[0:tT, cols] * d1, axis=0, keepdims=True))
            dpw_g = _tn(d0, dd1[0:tT, cols].astype(BF16))

            @pl.when(i == 0)
            def _(gi=gi, dpw_g=dpw_g):
                dpw_ref[gi] = dpw_g

            @pl.when(i > 0)
            def _(gi=gi, dpw_g=dpw_g):
                dpw_ref[gi] += dpw_g

        dsc = jnp.concatenate(dsc_parts, axis=1)
        vecs = jnp.concatenate([dcb, dlg, dlb, dsc, dpb, jnp.zeros((3, C), F32)], axis=0)

        @pl.when(i == 0)
        def _():
            dvec_ref[...] = vecs

        @pl.when(i > 0)
        def _():
            dvec_ref[...] += vecs

    vec = pl.BlockSpec((1, C), lambda i: (0, 0))
    taps = pl.BlockSpec((32, C), lambda i: (0, 0))

    def prev_blk(i):
        return (jnp.maximum(i * hb - 1, 0), 0)

    def next_blk(i):
        return (jnp.minimum((i + 1) * hb, T // HL - 1), 0)

    return _pallas(
        body, name=name, grid=(nT,),
        in_specs=[pl.BlockSpec((tT, 5 * C), lambda i: (i, 0)), pl.BlockSpec((HL, 5 * C), prev_blk),
                  pl.BlockSpec((HL, 5 * C), next_blk),
                  pl.BlockSpec((tT, C), lambda i: (i, 0)), pl.BlockSpec((HL, C), next_blk),
                  pl.BlockSpec((tT, C), lambda i: (i, 0)), pl.BlockSpec((HL, C), next_blk),
                  pl.BlockSpec((None, 2 * C, C), lambda i: (0, 0, 0)), pl.BlockSpec((8, 128), lambda i: (0, 0)),
                  taps, vec, vec, pl.BlockSpec((4, POOL_GW, POOL_GW), lambda i: (0, 0, 0)), vec, vec],
        out_specs=[pl.BlockSpec((tT, 5 * C), lambda i: (i, 0)), pl.BlockSpec((32 * 8, C), lambda i: (0, 0)),
                   pl.BlockSpec((8, C), lambda i: (0, 0)), pl.BlockSpec((4, POOL_GW, POOL_GW), lambda i: (0, 0, 0))],
        out_shape=[jax.ShapeDtypeStruct((T, 5 * C), BF16), jax.ShapeDtypeStruct((32 * 8, C), F32),
                   jax.ShapeDtypeStruct((8, C), F32), jax.ShapeDtypeStruct((4, POOL_GW, POOL_GW), F32)],
        scratch_shapes=[pltpu.VMEM((R1 + 8, C), F32), pltpu.VMEM((8, R1, C), F32), pltpu.VMEM((tT, C), F32),
                        pltpu.VMEM((32, 8, C), F32)],
        compiler_params=_params("arbitrary"))(p, p, p, u1, u1, dout, dout, w_out, after, cwr, lg, lb, pw, pb, sc)


def _softplus(z):
    u = jnp.exp(-jnp.abs(z))
    w = 1.0 + u
    l1p = jnp.where(w == 1.0, u, u * jnp.log(w) / jnp.where(w == 1.0, 1.0, w - 1.0))
    return jnp.maximum(z, 0.0) + l1p


def _lru_gates(xrx, cw_ref, cb_ref, wr_ref, br_ref, wi_ref, bi_ref, lam_ref):
    HL = ODD_HALO
    xc = cb_ref[...] + cw_ref[LRU_CONV_K - 1:LRU_CONV_K, :] * xrx[HL:]
    for k in range(LRU_CONV_K - 1):
        xc = xc + cw_ref[k:k + 1, :] * _shift_down(xrx, LRU_CONV_K - 1 - k)[HL:]
    xcb = xc.astype(BF16)
    rp, ip = [], []
    for hd in range(LRU_HEADS):
        cols = slice(hd * LRU_HD, (hd + 1) * LRU_HD)
        rp.append(jnp.dot(xcb[:, cols], wr_ref[hd], preferred_element_type=F32))
        ip.append(jnp.dot(xcb[:, cols], wi_ref[hd], preferred_element_type=F32))
    r = _sigmoid(jnp.concatenate(rp, axis=1) + br_ref[...])
    ig = _sigmoid(jnp.concatenate(ip, axis=1) + bi_ref[...])
    sp = _softplus(-lam_ref[...])
    log_a = (-LRU_C) * r * sp
    a = jnp.exp(log_a)
    m2 = jnp.maximum(-jnp.tanh(log_a) * (a * a + 1.0), 1e-30)
    inv_mult = lax.rsqrt(m2)
    return xc, xcb, r, ig, sp, a, m2 * inv_mult, inv_mult


def _group_scan(a, b, reverse):
    n, w = a.shape
    a, b = a.reshape(n // 8, 8, w), b.reshape(n // 8, 8, w)
    pos = lax.broadcasted_iota(jnp.int32, (1, 8, 1), 1)
    s = 1
    while s < 8:
        ok = (pos < 8 - s) if reverse else (pos >= s)
        shift = (8 - s) if reverse else s
        a_sh = jnp.where(ok, pltpu.roll(a, shift, 1), 1.0)
        b_sh = jnp.where(ok, pltpu.roll(b, shift, 1), 0.0)
        b = a * b_sh + b
        a = a * a_sh
        s *= 2
    return a.reshape(n, w), b.reshape(n, w)


def _apply_carries(a_ref, b_ref, out_ref, c0, reverse):
    ng = a_ref.shape[0] // 8

    def step(t, c):
        r0 = pl.multiple_of(((ng - 1 - t) if reverse else t) * 8, 8)
        x = a_ref[pl.ds(r0, 8), :] * c + b_ref[pl.ds(r0, 8), :]
        out_ref[pl.ds(r0, 8), :] = x
        return x[0:1, :] if reverse else x[7:8, :]

    return lax.fori_loop(0, ng, step, c0)


def _odd_mixer_fwd(p, cw, cb, wr, br, wi, bi, lam, name):
    T = p.shape[0]
    W = W_LRU
    tT, HL = MIX_TILE, ODD_HALO
    hb = tT // HL

    def body(pm_ref, ph_ref, cw_ref, cb_ref, wr_ref, br_ref, wi_ref, bi_ref, lam_ref, y_ref, hs_ref, carry_ref,
             sa_ref, sb_ref):
        i = pl.program_id(0)
        keep = (i > 0).astype(F32)

        @pl.when(i == 0)
        def _():
            carry_ref[...] = jnp.zeros_like(carry_ref)

        xrx = jnp.concatenate([ph_ref[:, 0:W] * keep, pm_ref[:, 0:W]], axis=0)
        xc, _, _, ig, _, a, mult, _ = _lru_gates(xrx, cw_ref, cb_ref, wr_ref, br_ref, wi_ref, bi_ref, lam_ref)
        sa_ref[...], sb_ref[...] = _group_scan(a, mult * (ig * xc), reverse=False)
        last = _apply_carries(sa_ref, sb_ref, hs_ref, carry_ref[0:1, :], reverse=False)
        carry_ref[...] = jnp.broadcast_to(last, (8, W))
        hs = hs_ref[...]
        gt = pm_ref[:, W:2 * W]
        y_ref[...] = (hs * (gt * _sigmoid(gt))).astype(BF16)

    vec = pl.BlockSpec((1, W), lambda i: (0, 0))
    heads = pl.BlockSpec((LRU_HEADS, LRU_HD, LRU_HD), lambda i: (0, 0, 0))
    return _pallas(
        body, name=name, grid=(T // tT,),
        in_specs=[pl.BlockSpec((tT, 2 * W), lambda i: (i, 0)),
                  pl.BlockSpec((HL, 2 * W), lambda i: (jnp.maximum(i * hb - 1, 0), 0)),
                  pl.BlockSpec((8, W), lambda i: (0, 0)), vec, heads, vec, heads, vec, vec],
        out_specs=[pl.BlockSpec((tT, W), lambda i: (i, 0)), pl.BlockSpec((tT, W), lambda i: (i, 0))],
        out_shape=[jax.ShapeDtypeStruct((T, W), BF16), jax.ShapeDtypeStruct((T, W), F32)],
        scratch_shapes=[pltpu.VMEM((8, W), F32), pltpu.VMEM((tT, W), F32), pltpu.VMEM((tT, W), F32)],
        compiler_params=_params("arbitrary"))(p, p, cw, cb, wr, br, wi, bi, lam)


def _odd_mixer_bwd(p, hs, dout, w_out, after, cw, cb, wr, br, wi, bi, lam, name):
    T = p.shape[0]
    W = W_LRU
    D = dout.shape[1]
    tT, HL = MIX_TILE, ODD_HALO
    hb = tT // HL
    nT = T // tT

    def body(pm_ref, ph_ref, hsm_ref, hsh_ref, do_ref, wo_ref, after_ref, cw_ref, cb_ref, wr_ref, br_ref, wi_ref,
             bi_ref, lam_ref, dp_ref, dwr_ref, dwi_ref, dvec_ref, gcarry_ref, xcarry_ref, sa_ref, sb_ref, g_ref):
        i = pl.program_id(0)
        keep = (i < nT - 1).astype(F32)

        @pl.when(i == 0)
        def _():
            gcarry_ref[...] = jnp.zeros_like(gcarry_ref)
            xcarry_ref[...] = jnp.zeros_like(xcarry_ref)

        xrx = jnp.concatenate([ph_ref[:, 0:W] * keep, pm_ref[:, 0:W]], axis=0)
        xc, xcb, r, ig, sp, a, mult, inv_mult = _lru_gates(xrx, cw_ref, cb_ref, wr_ref, br_ref, wi_ref, bi_ref, lam_ref)
        hs = hsm_ref[...]
        gt = pm_ref[:, W:2 * W]
        sg = _sigmoid(gt)
        dyv = _nt(do_ref[...], wo_ref[...])
        dp_ref[:, W:2 * W] = (dyv * hs * _dsilu(gt, sg)).astype(BF16)
        row = lax.broadcasted_iota(jnp.int32, (tT, 1), 0)
        m = jnp.where(row == tT - 1, 1.0, _shift_up(a, 1))
        sa_ref[...], sb_ref[...] = _group_scan(m, dyv * (gt * sg), reverse=True)
        first = _apply_carries(sa_ref, sb_ref, g_ref, gcarry_ref[0:1, :], reverse=True)
        G = g_ref[...]
        gcarry_ref[...] = jnp.broadcast_to(a[0:1, :] * first, (8, W))
        hs_prev = jnp.where(row == 0, hsh_ref[HL - 1:HL, :] * keep, _shift_down(hs, 1))
        da = G * hs_prev
        dmult = G * (ig * xc)
        di = G * mult * xc
        dxc = G * mult * ig
        dlog_a = da * a - dmult * (a * a) * inv_mult
        drp = dlog_a * ((-LRU_C) * sp) * r * (1.0 - r)
        dip = di * ig * (1.0 - ig)
        dlam = jnp.sum(dlog_a * ((-LRU_C) * r), axis=0, keepdims=True) * (-_sigmoid(-lam_ref[...]))
        drb, dib = drp.astype(BF16), dip.astype(BF16)
        back = []
        for hd in range(LRU_HEADS):
            cols = slice(hd * LRU_HD, (hd + 1) * LRU_HD)
            back.append(_nt(drb[:, cols], wr_ref[hd]) + _nt(dib[:, cols], wi_ref[hd]))
            dwr_h = _tn(xcb[:, cols], drb[:, cols])
            dwi_h = _tn(xcb[:, cols], dib[:, cols])

            @pl.when(i == 0)
            def _(hd=hd, dwr_h=dwr_h, dwi_h=dwi_h):
                dwr_ref[hd] = dwr_h
                dwi_ref[hd] = dwi_h

            @pl.when(i > 0)
            def _(hd=hd, dwr_h=dwr_h, dwi_h=dwi_h):
                dwr_ref[hd] += dwr_h
                dwi_ref[hd] += dwi_h

        dxc = dxc + jnp.concatenate(back, axis=1)
        dxcx = jnp.concatenate([dxc, xcarry_ref[...]], axis=0)
        dxr = cw_ref[LRU_CONV_K - 1:LRU_CONV_K, :] * dxc
        rows = []
        for k in range(LRU_CONV_K - 1):
            j = LRU_CONV_K - 1 - k
            dxr = dxr + cw_ref[k:k + 1, :] * _shift_up(dxcx, j)[0:tT]
            rows.append(jnp.sum(dxc * _shift_down(xrx, j)[HL:], axis=0, keepdims=True))
        rows.append(jnp.sum(dxc * xrx[HL:], axis=0, keepdims=True))
        dp_ref[:, 0:W] = dxr.astype(BF16)
        xcarry_ref[...] = dxc[0:8]
        rows += [jnp.sum(dxc, axis=0, keepdims=True), jnp.sum(drp, axis=0, keepdims=True),
                 jnp.sum(dip, axis=0, keepdims=True), dlam]
        vecs = jnp.concatenate(rows, axis=0)

        @pl.when(i == 0)
        def _():
            dvec_ref[...] = vecs

        @pl.when(i > 0)
        def _():
            dvec_ref[...] += vecs

    vec = pl.BlockSpec((1, W), lambda i: (0, 0))
    heads = pl.BlockSpec((LRU_HEADS, LRU_HD, LRU_HD), lambda i: (0, 0, 0))

    def tile(i):
        return (nT - 1 - i, 0)

    def prev_blk(i):
        return (jnp.maximum((nT - 1 - i) * hb - 1, 0), 0)

    return _pallas(
        body, name=name, grid=(nT,),
        in_specs=[pl.BlockSpec((tT, 2 * W), tile), pl.BlockSpec((HL, 2 * W), prev_blk),
                  pl.BlockSpec((tT, W), tile), pl.BlockSpec((HL, W), prev_blk), pl.BlockSpec((tT, D), tile),
                  pl.BlockSpec((None, W, D), lambda i: (0, 0, 0)), pl.BlockSpec((8, 128), lambda i: (0, 0)),
                  pl.BlockSpec((8, W), lambda i: (0, 0)), vec, heads, vec, heads, vec, vec],
        out_specs=[pl.BlockSpec((tT, 2 * W), tile), heads, heads, pl.BlockSpec((8, W), lambda i: (0, 0))],
        out_shape=[jax.ShapeDtypeStruct((T, 2 * W), BF16), jax.ShapeDtypeStruct((LRU_HEADS, LRU_HD, LRU_HD), F32),
                   jax.ShapeDtypeStruct((LRU_HEADS, LRU_HD, LRU_HD), F32), jax.ShapeDtypeStruct((8, W), F32)],
        scratch_shapes=[pltpu.VMEM((8, W), F32), pltpu.VMEM((8, W), F32), pltpu.VMEM((tT, W), F32),
                        pltpu.VMEM((tT, W), F32), pltpu.VMEM((tT, W), F32)],
        compiler_params=_params("arbitrary"))(p, p, hs, hs, dout, w_out, after, cw, cb, wr, br, wi, bi, lam)


def _pad_rows(a, rows):
    return jnp.concatenate([a, jnp.zeros((rows - a.shape[0], a.shape[1]), a.dtype)], axis=0)


def _layer_fwd(even, h, w, w_in, w_out, after):
    if even:
        p, n = _in_proj(h, w["norm"], w_in, 0, after, "in_proj_even")
        y, aux = _even_mixer_fwd(p, w["conv_w"], w["conv_b"], w["ln_g"], w["ln_b"], w["pool_w"], w["pool_b"],
                                 w["pool_scale"], "even_mixer_fwd")
        h_next = _out_proj(y, w_out, 0, h, "out_proj_even")
    else:
        p, n = _in_proj(h, w["norm"], w_in, 0, after, "in_proj_odd")
        y, aux = _odd_mixer_fwd(p, w["conv_w"], w["conv_b"], w["w_rg"], w["b_rg"], w["w_ig"], w["b_ig"], w["lam"],
                                "odd_mixer_fwd")
        h_next = _out_proj(y, w_out, 0, h, "out_proj_odd")
    return h_next, (h, n, p, aux, y)


def _layer_bwd(even, saved, w, w_in, w_out, dh, dhb, after):
    h, n, p, aux, y = saved
    if even:
        dp, dcw, dvec, dpw = _even_mixer_bwd(p, aux, dhb, w_out, after, w["conv_w_rev"], w["ln_g"], w["ln_b"],
                                             w["pool_w"], w["pool_b"], w["pool_scale"], "even_mixer_bwd")
        dw_out = _dw_out(y, dhb, 0, 1, None, "dw_out_even")
        dw_in = _dw_in(n, dp, N_CHIPS, 0, 1, None, "dw_in_even")
        dh, dhb, dnorm = _dn_proj(dp, w_in, 0, h, w["norm"], dh, "dn_proj_even")
        return dh, dhb, dw_in, dw_out, dict(conv_w=dcw, vec=dvec, pool_w=dpw, norm=dnorm)
    dp, dwr, dwi, dvec = _odd_mixer_bwd(p, aux, dhb, w_out, after, w["conv_w"], w["conv_b"], w["w_rg"], w["b_rg"],
                                        w["w_ig"], w["b_ig"], w["lam"], "odd_mixer_bwd")
    dw_out = _dw_out(y, dhb, 0, 1, None, "dw_out_odd")
    dw_in = _dw_in(n, dp, N_CHIPS, 0, 1, None, "dw_in_odd")
    dh, dhb, dnorm = _dn_proj(dp, w_in, 0, h, w["norm"], dh, "dn_proj_odd")
    return dh, dhb, dw_in, dw_out, dict(w_rg=dwr, w_ig=dwi, vec=dvec, norm=dnorm)


ANY = pl.BlockSpec(memory_space=pl.ANY)


def _mesh_pos():
    return lax.axis_index("x"), lax.axis_index("y"), lax.axis_index("c")


def _other_chips(x, y):
    return [(1 - x, y), (x, 1 - y), (1 - x, 1 - y)]


def _other_devices(x, y, c):
    out = []
    for p in range(1, N_DEV):
        out.append((1 - x if p & 4 else x, 1 - y if p & 2 else y, 1 - c if p & 1 else c))
    return out


def _remote(src, dst, ssem, rsem, dev):
    return pltpu.make_async_remote_copy(src_ref=src, dst_ref=dst, send_sem=ssem, recv_sem=rsem, device_id=dev,
                                        device_id_type=MESH)


def _comm_call(body, name, ins, out_shape, scratch, aliases=None):
    return _pallas(body, name=name, in_specs=[ANY] * len(ins), out_specs=[ANY] * len(out_shape), out_shape=out_shape,
                   scratch_shapes=scratch, input_output_aliases=aliases or {},
                   compiler_params=pltpu.CompilerParams(has_side_effects=True))(*ins)


def _cast_shard(w, layer, pos):
    _, R, C = w.shape
    tr = _row_tile(R, C)

    def body(pos_ref, w_ref, o_ref):
        o_ref[...] = w_ref[...].astype(BF16)

    grid_spec = pltpu.PrefetchScalarGridSpec(
        num_scalar_prefetch=1, grid=(R // tr,),
        in_specs=[pl.BlockSpec((None, tr, C), lambda i, pr: (layer, i, 0))],
        out_specs=pl.BlockSpec((None, None, tr, C), lambda i, pr: (0, pr[0], i, 0)))
    return _pallas(body, name="cast_shard", grid_spec=grid_spec,
                   out_shape=jax.ShapeDtypeStruct((1, N_CHIPS, R, C), BF16),
                   compiler_params=_params("parallel"))(pos, w)


def _gather_weights(big, small):
    nA = len(big)
    half = [a.shape[2] // 2 for a in big]

    def body(*refs):
        ins, outs = refs[:nA + 1], refs[nA + 1:2 * nA + 2]
        ssem, rsem, fsem, frsem, lsem = refs[2 * nA + 2:]
        x, y, c = _mesh_pos()
        k = 2 * x + y
        chips = _other_chips(x, y)
        sib = (x, y, 1 - c)

        def slab(a, chip, core):
            return outs[a].at[:, chip, pl.ds(core * half[a], half[a]), :]

        local = [pltpu.make_async_copy(ins[nA], outs[nA].at[k], lsem.at[0])]
        for cp in local:
            cp.start()
        sends = []
        for j, (ox, oy) in enumerate(chips):
            for a in range(nA):
                sends.append(_remote(slab(a, k, c), slab(a, k, c), ssem.at[a, j], rsem.at[a, j], (ox, oy, c)))
            sends.append(_remote(ins[nA], outs[nA].at[k], ssem.at[nA, j], rsem.at[nA, j], (ox, oy, c)))
        for cp in sends:
            cp.start()
        for j, (ox, oy) in enumerate(chips):
            kj = 2 * ox + oy
            for a in range(nA):
                got = slab(a, kj, c)
                _remote(got, got, ssem.at[a, j], rsem.at[a, j], (ox, oy, c)).wait_recv()
                fw = _remote(got, got, fsem.at[a, j], frsem.at[a, j], sib)
                fw.start()
                sends.append(fw)
            gs = outs[nA].at[kj]
            _remote(gs, gs, ssem.at[nA, j], rsem.at[nA, j], (ox, oy, c)).wait_recv()
        for j, (ox, oy) in enumerate(chips):
            kj = 2 * ox + oy
            for a in range(nA):
                theirs = slab(a, kj, 1 - c)
                _remote(theirs, theirs, fsem.at[a, j], frsem.at[a, j], sib).wait_recv()
        for cp in sends:
            cp.wait_send()
        for cp in local:
            cp.wait()

    out_shape = [jax.ShapeDtypeStruct(a.shape, a.dtype) for a in big]
    out_shape.append(jax.ShapeDtypeStruct((N_CHIPS,) + small.shape, small.dtype))
    scratch = [pltpu.SemaphoreType.DMA((nA + 1, 3)), pltpu.SemaphoreType.DMA((nA + 1, 3)),
               pltpu.SemaphoreType.DMA((nA, 3)), pltpu.SemaphoreType.DMA((nA, 3)), pltpu.SemaphoreType.DMA((1,))]
    return _comm_call(body, "gather_weights", list(big) + [small], out_shape, scratch, {a: a for a in range(nA)})


HBM = pl.BlockSpec(memory_space=pltpu.HBM)
SEM = pl.BlockSpec(memory_space=pltpu.SEMAPHORE)
EFFECT = pltpu.SideEffectType.DATAFLOW_SIDE_EFFECTING


def _split_start(arrays, copies, n, name):
    k = len(arrays)

    def body(*refs):
        for cp in copies(refs[k + 2:2 * k + 2], refs[k], refs[k + 1]):
            cp.start()
        refs[2 * k + 2][...] = jnp.zeros((8, 128), F32)

    out = _pallas(
        body, name=name,
        out_shape=(pltpu.SemaphoreType.DMA((n,)), pltpu.SemaphoreType.DMA((n,)),
                   *[pltpu.HBM(a.shape, a.dtype) for a in arrays], jax.ShapeDtypeStruct((8, 128), F32)),
        in_specs=(HBM,) * k, out_specs=(SEM, SEM) + (HBM,) * k + (pl.BlockSpec(memory_space=pltpu.VMEM),),
        input_output_aliases={i: i + 2 for i in range(k)},
        compiler_params=pltpu.CompilerParams(has_side_effects=EFFECT),
    )(*[pltpu.with_memory_space_constraint(a, pltpu.HBM) for a in arrays])
    return out[0], out[1], list(out[2:2 + k]), out[2 + k]


def _split_wait(ssem, rsem, arrays, copies, after, name):
    k = len(arrays)

    def body(*refs):
        for cp in copies(refs[:k], refs[k], refs[k + 1]):
            cp.wait_send()
            cp.wait_recv()

    out = _pallas(
        body, name=name, out_shape=tuple(pltpu.HBM(a.shape, a.dtype) for a in arrays),
        in_specs=(HBM,) * k + (SEM, SEM, ANY), out_specs=(HBM,) * k, input_output_aliases={i: i for i in range(k)},
        compiler_params=pltpu.CompilerParams(has_side_effects=EFFECT),
    )(*arrays, ssem, rsem, after)
    return list(out)


def _gather_copies(shapes):
    half = [s[2] // 2 for s in shapes]

    def copies(refs, ssem, rsem):
        x, y, c = _mesh_pos()
        out = []
        for j, (ox, oy) in enumerate(_other_chips(x, y)):
            for a, ref in enumerate(refs):
                slab = ref.at[:, 2 * x + y, pl.ds(c * half[a], half[a]), :]
                out.append(_remote(slab, slab, ssem.at[3 * a + j], rsem.at[3 * a + j], (ox, oy, c)))
        return out

    return copies


def _chips_copies(n_arr):
    def copies(refs, ssem, rsem):
        x, y, c = _mesh_pos()
        out = []
        for j, (ox, oy) in enumerate(_other_chips(x, y)):
            for a in range(n_arr):
                out.append(_remote(refs[a].at[:, 2 * ox + oy], refs[n_arr + a].at[:, 2 * x + y], ssem.at[3 * a + j],
                                   rsem.at[3 * a + j], (ox, oy, c)))
        return out

    return copies


def _forward_cores(arrays):
    nA = len(arrays)
    half = [a.shape[2] // 2 for a in arrays]

    def body(*refs):
        outs = refs[nA:2 * nA]
        ssem, rsem = refs[2 * nA:]
        x, y, c = _mesh_pos()
        sib = (x, y, 1 - c)
        sends, waits = [], []
        for j, (ox, oy) in enumerate(_other_chips(x, y)):
            for a in range(nA):
                got = outs[a].at[:, 2 * ox + oy, pl.ds(c * half[a], half[a]), :]
                sends.append(_remote(got, got, ssem.at[a, j], rsem.at[a, j], sib))
                theirs = outs[a].at[:, 2 * ox + oy, pl.ds((1 - c) * half[a], half[a]), :]
                waits.append(_remote(theirs, theirs, ssem.at[a, j], rsem.at[a, j], sib))
        for cp in sends:
            cp.start()
        for cp in waits:
            cp.wait_recv()
        for cp in sends:
            cp.wait_send()

    out_shape = [jax.ShapeDtypeStruct(a.shape, a.dtype) for a in arrays]
    scratch = [pltpu.SemaphoreType.DMA((nA, 3)), pltpu.SemaphoreType.DMA((nA, 3))]
    return _comm_call(body, "forward_cores", list(arrays), out_shape, scratch, {a: a for a in range(nA)})


def _exchange_halves(big):
    nA = len(big)
    half = [a.shape[2] // 2 for a in big]

    def body(*refs):
        ins, outs = refs[:nA], refs[nA:2 * nA]
        ssem, rsem = refs[2 * nA:]
        x, y, c = _mesh_pos()
        sib = (x, y, 1 - c)
        sends = [_remote(ins[a].at[:, :, pl.ds((1 - c) * half[a], half[a]), :], outs[a], ssem.at[a], rsem.at[a], sib)
                 for a in range(nA)]
        for cp in sends:
            cp.start()
        for a in range(nA):
            _remote(outs[a], outs[a], ssem.at[a], rsem.at[a], sib).wait_recv()
        for cp in sends:
            cp.wait_send()

    out_shape = [jax.ShapeDtypeStruct((a.shape[0], N_CHIPS, h, a.shape[3]), a.dtype) for a, h in zip(big, half)]
    scratch = [pltpu.SemaphoreType.DMA((nA,)), pltpu.SemaphoreType.DMA((nA,))]
    return _comm_call(body, "exchange_halves", list(big), out_shape, scratch)


def _exchange_cores(big, small):
    nA = len(big)
    half = [a.shape[2] // 2 for a in big]

    def body(*refs):
        ins, outs = refs[:nA + 1], refs[nA + 1:2 * nA + 2]
        ssem, rsem, ssem2, rsem2, lsem = refs[2 * nA + 2:]
        x, y, c = _mesh_pos()
        me = 4 * x + 2 * y + c
        sib = (x, y, 1 - c)
        peers = _other_devices(x, y, c)
        local = pltpu.make_async_copy(ins[nA], outs[nA].at[me], lsem.at[0])
        local.start()
        sends = []
        for a in range(nA):
            src = ins[a].at[:, :, pl.ds((1 - c) * half[a], half[a]), :]
            sends.append(_remote(src, outs[a], ssem.at[a], rsem.at[a], sib))
        for p, dev in enumerate(peers):
            sends.append(_remote(ins[nA], outs[nA].at[me], ssem2.at[p], rsem2.at[p], dev))
        for cp in sends:
            cp.start()
        for a in range(nA):
            _remote(outs[a], outs[a], ssem.at[a], rsem.at[a], sib).wait_recv()
        for p, (px, py, pc) in enumerate(peers):
            got = outs[nA].at[4 * px + 2 * py + pc]
            _remote(got, got, ssem2.at[p], rsem2.at[p], (px, py, pc)).wait_recv()
        for cp in sends:
            cp.wait_send()
        local.wait()

    out_shape = [jax.ShapeDtypeStruct((a.shape[0], N_CHIPS, h, a.shape[3]), a.dtype) for a, h in zip(big, half)]
    out_shape.append(jax.ShapeDtypeStruct((N_DEV,) + small.shape, small.dtype))
    scratch = [pltpu.SemaphoreType.DMA((nA,)), pltpu.SemaphoreType.DMA((nA,)), pltpu.SemaphoreType.DMA((N_DEV - 1,)),
               pltpu.SemaphoreType.DMA((N_DEV - 1,)), pltpu.SemaphoreType.DMA((1,))]
    return _comm_call(body, "exchange_cores", list(big) + [small], out_shape, scratch)


def _exchange_chips(parts):
    nA = len(parts)

    def body(*refs):
        ins, outs = refs[:nA], refs[nA:2 * nA]
        ssem, rsem = refs[2 * nA:]
        x, y, c = _mesh_pos()
        k = 2 * x + y
        chips = _other_chips(x, y)
        sends = []
        for j, (ox, oy) in enumerate(chips):
            for a in range(nA):
                sends.append(_remote(ins[a].at[:, 2 * ox + oy], outs[a].at[:, k], ssem.at[a, j], rsem.at[a, j],
                                     (ox, oy, c)))
        for cp in sends:
            cp.start()
        for j, (ox, oy) in enumerate(chips):
            for a in range(nA):
                got = outs[a].at[:, 2 * ox + oy]
                _remote(got, got, ssem.at[a, j], rsem.at[a, j], (ox, oy, c)).wait_recv()
        for cp in sends:
            cp.wait_send()

    out_shape = [jax.ShapeDtypeStruct(a.shape, a.dtype) for a in parts]
    scratch = [pltpu.SemaphoreType.DMA((nA, 3)), pltpu.SemaphoreType.DMA((nA, 3))]
    return _comm_call(body, "exchange_chips", list(parts), out_shape, scratch)


def _exchange_final(grads, everywhere):
    nA = len(grads)
    n_remote = sum(N_DEV - 1 if ev else 1 for ev in everywhere)

    def body(*refs):
        outs = refs[nA:2 * nA]
        ssem, rsem = refs[2 * nA:]
        x, y, c = _mesh_pos()
        k = 2 * x + y
        sib = (x, y, 1 - c)
        peers = _other_devices(x, y, c)
        sends, waits = [], []
        s = 0
        for a in range(nA):
            if everywhere[a]:
                r2 = grads[a].shape[1] // N_DEV
                mine = outs[a].at[:, pl.ds((2 * k + c) * r2, r2), :]
                for (px, py, pc) in peers:
                    sends.append(_remote(mine, mine, ssem.at[s], rsem.at[s], (px, py, pc)))
                    got = outs[a].at[:, pl.ds((2 * (2 * px + py) + pc) * r2, r2), :]
                    waits.append(_remote(got, got, ssem.at[s], rsem.at[s], (px, py, pc)))
                    s += 1
            else:
                r2 = grads[a].shape[1] // 2
                mine = outs[a].at[:, pl.ds(c * r2, r2), :]
                sends.append(_remote(mine, mine, ssem.at[s], rsem.at[s], sib))
                got = outs[a].at[:, pl.ds((1 - c) * r2, r2), :]
                waits.append(_remote(got, got, ssem.at[s], rsem.at[s], sib))
                s += 1
        for cp in sends:
            cp.start()
        for cp in waits:
            cp.wait_recv()
        for cp in sends:
            cp.wait_send()

    out_shape = [jax.ShapeDtypeStruct(g.shape, g.dtype) for g in grads]
    scratch = [pltpu.SemaphoreType.DMA((n_remote,)), pltpu.SemaphoreType.DMA((n_remote,))]
    return _comm_call(body, "exchange_final", list(grads), out_shape, scratch, {a: a for a in range(nA)})


BLOCK_BYTES = 4 << 20


def _row_tile(rows, cols, mult=16, limit=BLOCK_BYTES):
    best = mult
    for t in range(mult, rows + 1, mult):
        if rows % t == 0 and t * cols * 4 <= limit:
            best = t
    return best


def _add_cores(own, recv, pos):
    L, _, R, C = own.shape
    r2 = R // 2
    tr = _row_tile(r2, C)
    nb = r2 // tr

    def body(pos_ref, a_ref, r_ref, o_ref):
        o_ref[...] = (a_ref[...].astype(F32) + r_ref[...].astype(F32)).astype(BF16)

    blk = (None, None, tr, C)
    grid_spec = pltpu.PrefetchScalarGridSpec(
        num_scalar_prefetch=1, grid=(L, N_CHIPS, nb),
        in_specs=[pl.BlockSpec(blk, lambda l, s, i, pr: (l, s, pr[1] * nb + i, 0)),
                  pl.BlockSpec(blk, lambda l, s, i, pr: (l, s, i, 0))],
        out_specs=pl.BlockSpec(blk, lambda l, s, i, pr: (l, s, i, 0)))
    return _pallas(body, name="add_cores", grid_spec=grid_spec,
                   out_shape=jax.ShapeDtypeStruct((L, N_CHIPS, r2, C), BF16),
                   compiler_params=_params("parallel", "parallel", "parallel"))(pos, own, recv)


def _sum_chips(own, recv, pos, everywhere, layer, nlayers, prev):
    _, _, r2, C = own.shape
    tr = _row_tile(r2, 2 * C)
    nb = r2 // tr

    def body(pos_ref, a_ref, r_ref, *rest):
        acc = None
        for s in range(N_CHIPS):
            term = jnp.where(pos_ref[0] == s, a_ref[...], r_ref[s]).astype(F32)
            acc = term if acc is None else acc + term
        rest[-1][...] = acc

    if everywhere:
        def out_map(i, pr):
            return (layer, (2 * pr[0] + pr[1]) * nb + i, 0)
    else:
        def out_map(i, pr):
            return (layer, pr[1] * nb + i, 0)

    in_specs = [pl.BlockSpec((None, None, tr, C), lambda i, pr: (0, pr[0], i, 0)),
                pl.BlockSpec((None, N_CHIPS, tr, C), lambda i, pr: (0, 0, i, 0))]
    grid_spec = pltpu.PrefetchScalarGridSpec(
        num_scalar_prefetch=1, grid=(nb,), in_specs=in_specs + ([] if prev is None else [ANY]),
        out_specs=pl.BlockSpec((None, tr, C), out_map))
    rows = (N_DEV if everywhere else 2) * r2
    args = (pos, own, recv) if prev is None else (pos, own, recv, prev)
    return _pallas(body, name="sum_chips", grid_spec=grid_spec, out_shape=jax.ShapeDtypeStruct((nlayers, rows, C), F32),
                   input_output_aliases={} if prev is None else {3: 0},
                   compiler_params=_params("parallel"))(*args)


def _sum_devices(parts):
    n, R, C = parts.shape
    tr = _row_tile(R, C * n, 8)

    def body(p_ref, o_ref):
        acc = p_ref[0]
        for s in range(1, n):
            acc = acc + p_ref[s]
        o_ref[...] = acc

    return _pallas(body, name="sum_devices", grid=(R // tr,), in_specs=[pl.BlockSpec((n, tr, C), lambda i: (0, i, 0))],
                   out_specs=pl.BlockSpec((tr, C), lambda i: (i, 0)), out_shape=jax.ShapeDtypeStruct((R, C), F32),
                   compiler_params=_params("parallel"))(parts)


def _adamw(w, g, m, v, name):
    L, R, C = w.shape
    tr = _row_tile(R, C, 8, BLOCK_BYTES // 2)

    def body(w_ref, g_ref, m_ref, v_ref, d_ref, m2_ref, v2_ref):
        gg = g_ref[...]
        m2 = ADAM_B1 * m_ref[...] + (1.0 - ADAM_B1) * gg
        v2 = ADAM_B2 * v_ref[...] + (1.0 - ADAM_B2) * (gg * gg)
        m_hat = m2 / (1.0 - ADAM_B1 ** ADAM_STEP)
        v_hat = v2 / (1.0 - ADAM_B2 ** ADAM_STEP)
        d_ref[...] = -ADAM_LR * (m_hat / (jnp.sqrt(v_hat) + ADAM_EPS) + ADAM_WD * w_ref[...])
        m2_ref[...] = m2
        v2_ref[...] = v2

    blk = pl.BlockSpec((1, tr, C), lambda l, i: (l, i, 0))
    shp = jax.ShapeDtypeStruct((L, R, C), F32)
    return _pallas(body, name=name, grid=(L, R // tr), in_specs=[blk] * 4, out_specs=[blk] * 3, out_shape=[shp] * 3,
                   compiler_params=_params("parallel", "parallel"))(w, g, m, v)


WEIGHTS = ("norm_even", "w_in_even", "conv_a_w", "conv_a_b", "ln_a_g", "ln_a_b", "pool_w", "pool_b", "pool_scale",
           "w_out_even", "norm_odd", "w_in_odd", "conv_c_w", "conv_c_b", "w_rg", "b_rg", "w_ig", "b_ig", "lru_lambda",
           "w_out_odd", "final_norm")
BIG = ("w_in_even", "w_out_even", "pool_w", "w_in_odd", "w_out_odd", "w_rg", "w_ig")
SMALL = tuple(n for n in WEIGHTS if n not in BIG)
SMALL_SHARDED = ("conv_a_w", "pool_b", "norm_odd", "conv_c_w", "conv_c_b", "b_rg", "b_ig", "lru_lambda")


def _pack(arrs):
    flat = jnp.concatenate([a.reshape(-1) for a in arrs])
    rows = -(-flat.shape[0] // (64 * 128)) * 64
    return jnp.pad(flat, (0, rows * 128 - flat.shape[0])).reshape(rows, 128)


def _unpack(buf, shapes, lead=()):
    flat = buf.reshape(tuple(lead) + (-1,))
    out, o = [], 0
    for s in shapes:
        n = 1
        for d in s:
            n *= d
        out.append(flat[..., o:o + n].reshape(tuple(lead) + tuple(s)))
        o += n
    return out


def _shard(full, axis, k):
    n = full.shape[axis] // N_CHIPS
    return lax.dynamic_slice_in_dim(full, k * n, n, axis)


def kernel(x, norm_even, w_in_even, conv_a_w, conv_a_b, ln_a_g, ln_a_b, pool_w, pool_b, pool_scale, w_out_even, norm_odd, w_in_odd, conv_c_w, conv_c_b, w_rg, b_rg, w_ig, b_ig, lru_lambda, w_out_odd, final_norm, loss_target, m_norm_even, m_w_in_even, m_conv_a_w, m_conv_a_b, m_ln_a_g, m_ln_a_b, m_pool_w, m_pool_b, m_pool_scale, m_w_out_even, m_norm_odd, m_w_in_odd, m_conv_c_w, m_conv_c_b, m_w_rg, m_b_rg, m_w_ig, m_b_ig, m_lru_lambda, m_w_out_odd, m_final_norm, v_norm_even, v_w_in_even, v_conv_a_w, v_conv_a_b, v_ln_a_g, v_ln_a_b, v_pool_w, v_pool_b, v_pool_scale, v_w_out_even, v_norm_odd, v_w_in_odd, v_conv_c_w, v_conv_c_b, v_w_rg, v_b_rg, v_w_ig, v_b_ig, v_lru_lambda, v_w_out_odd, v_final_norm):
    P = dict(locals())
    xi, yi, ci = _mesh_pos()
    k = 2 * xi + yi
    L = w_in_even.shape[0]
    D = D_MODEL

    pos = jnp.stack([k, ci]).astype(jnp.int32)
    depth = 2 * L
    pool_w3 = pool_w.reshape(L, 4 * 64, POOL_GW)

    def cast_group(layer):
        j = layer // 2
        if layer % 2 == 0:
            return [_cast_shard(w_in_even, j, pos), _cast_shard(w_out_even, j, pos), _cast_shard(pool_w3, j, pos)]
        return [_cast_shard(w_in_odd, j, pos), _cast_shard(w_out_odd, j, pos)]

    *group, g_small = _gather_weights(cast_group(0), _pack([P[n] for n in SMALL_SHARDED]))
    full = {}
    for n, a in zip(SMALL_SHARDED, _unpack(g_small, [P[n].shape for n in SMALL_SHARDED], lead=(N_CHIPS,))):
        a = jnp.moveaxis(a, 0, -2)
        full[n] = a.reshape(a.shape[:-2] + (N_CHIPS * a.shape[-1],))

    def small_weights(layer, group):
        j = layer // 2
        if layer % 2 == 0:
            cw = full["conv_a_w"][j]
            pw = group[2].reshape(N_CHIPS, 4, 64, POOL_GW).transpose(1, 0, 2, 3).reshape(4, POOL_GW, POOL_GW)
            return dict(norm=norm_even[j][None], conv_w=_pad_rows(cw, 32), conv_w_rev=_pad_rows(cw[::-1], 32),
                        conv_b=conv_a_b[j][None], ln_g=ln_a_g[j][None], ln_b=ln_a_b[j][None], pool_w=pw,
                        pool_b=full["pool_b"][j].reshape(1, D), pool_scale=pool_scale[j][None])
        return dict(norm=full["norm_odd"][j][None], conv_w=_pad_rows(full["conv_c_w"][j], 8),
                    conv_b=full["conv_c_b"][j][None], w_rg=w_rg[j].astype(BF16), b_rg=full["b_rg"][j][None],
                    w_ig=w_ig[j].astype(BF16), b_ig=full["b_ig"][j][None], lam=full["lru_lambda"][j][None])

    no_token = jnp.zeros((8, 128), F32)
    h = x[0]
    saved, big_w, small_w = [], [], []
    for layer in range(depth):
        token = no_token
        if layer + 1 < depth:
            nxt = cast_group(layer + 1)
            copies = _gather_copies([a.shape for a in nxt])
            ssem, rsem, nxt, token = _split_start(nxt, copies, 3 * len(nxt), "gather_start%d" % (layer + 1))
        small_w.append(small_weights(layer, group))
        big_w.append((group[0], group[1].reshape(1, -1, D)))
        h, sv = _layer_fwd(layer % 2 == 0, h, small_w[layer], *big_w[layer], token)
        saved.append(sv)
        if layer + 1 < depth:
            group = _forward_cores(_split_wait(ssem, rsem, nxt, copies, h, "gather_wait%d" % (layer + 1)))

    dh, dhb, d_final, loss = _loss_head(h, final_norm[None], loss_target[0])
    loss = lax.psum(loss[0, 0], ("x", "y", "c"))
    everywhere = [False, False, False, False, False, True, True]
    final = [None] * len(everywhere)
    small_of = [None] * depth
    recv_small = None

    def finish(pair, land, slots, j):
        for a, r, s in zip(pair, land, slots):
            final[s] = _sum_chips(a, r, pos, everywhere[s], j, L, final[s])

    pending = None
    token = no_token
    for layer in reversed(range(depth)):
        j = layer // 2
        dh, dhb, dw_in, dw_out, sm = _layer_bwd(layer % 2 == 0, saved[layer], small_w[layer], *big_w[layer], dh, dhb,
                                                token)
        small_of[layer] = sm
        if pending is not None:
            ssem, rsem, arrs, copies, slots, pj, pl_ = pending
            arrs = _split_wait(ssem, rsem, arrs, copies, dh, "chips_wait%d" % pl_)
            finish(arrs[:len(slots)], arrs[len(slots):], slots, pj)
            pending = None
        if layer % 2 == 0:
            dpw = sm["pool_w"].reshape(4, N_CHIPS, 64, POOL_GW).transpose(1, 0, 2, 3)
            parts = [dw_in, dw_out.reshape(1, N_CHIPS, -1, D), dpw.reshape(1, N_CHIPS, 4 * 64, POOL_GW).astype(BF16)]
            slots = [0, 1, 2]
        else:
            parts = [dw_in, dw_out.reshape(1, N_CHIPS, -1, D),
                     sm["w_rg"].reshape(1, N_CHIPS, -1, LRU_HD).astype(BF16),
                     sm["w_ig"].reshape(1, N_CHIPS, -1, LRU_HD).astype(BF16)]
            slots = [3, 4, 5, 6]
        if layer > 0:
            recv = _exchange_halves(parts)
        else:
            small_g = []
            for jj in range(L):
                ge, go = small_of[2 * jj], small_of[2 * jj + 1]
                small_g += [ge["conv_w"].reshape(32, 8, D).sum(axis=1)[:CONV_K], ge["vec"][0:5], ge["norm"], go["vec"],
                            go["norm"]]
            small_g.append(d_final)
            small_shapes = [a.shape for a in small_g]
            *recv, recv_small = _exchange_cores(parts, _pack(small_g))
        pair = [_add_cores(a, r, pos) for a, r in zip(parts, recv)]
        if layer > 0:
            copies = _chips_copies(len(pair))
            land = [lax.empty(a.shape, a.dtype) for a in pair]
            ssem, rsem, arrs, token = _split_start(pair + land, copies, 3 * len(pair), "chips_start%d" % layer)
            pending = (ssem, rsem, arrs, copies, slots, j, layer)
        else:
            finish(pair, _exchange_chips(pair), slots, j)
    grad_x = dh
    gw = _exchange_final(final, everywhere)
    sg = _unpack(_sum_devices(recv_small), small_shapes)

    grads = dict(w_in_even=gw[0], w_out_even=gw[1], pool_w=gw[2].reshape(pool_w.shape), w_in_odd=gw[3], w_out_odd=gw[4],
                 w_rg=gw[5].reshape(w_rg.shape), w_ig=gw[6].reshape(w_ig.shape), final_norm=sg[-1][0])
    ev = [sg[5 * j + 1] for j in range(L)]
    ov = [sg[5 * j + 3] for j in range(L)]
    grads["conv_a_w"] = _shard(jnp.stack([sg[5 * j] for j in range(L)]), 2, k)
    grads["norm_even"] = jnp.stack([sg[5 * j + 2][0] for j in range(L)])
    grads["norm_odd"] = _shard(jnp.stack([sg[5 * j + 4][0] for j in range(L)]), 1, k)
    for r, n in enumerate(("conv_a_b", "ln_a_g", "ln_a_b", "pool_scale")):
        grads[n] = jnp.stack([e[r] for e in ev])
    grads["pool_b"] = _shard(jnp.stack([e[4].reshape(4, POOL_GW) for e in ev]), 2, k)
    grads["conv_c_w"] = _shard(jnp.stack([o[0:4] for o in ov]), 2, k)
    for r, n in zip((4, 5, 6, 7), ("conv_c_b", "b_rg", "b_ig", "lru_lambda")):
        grads[n] = _shard(jnp.stack([o[r] for o in ov]), 1, k)

    delta, new_m, new_v = {}, {}, {}
    for n in BIG:
        s3 = (L, -1, P[n].shape[-1])
        d, m2, v2 = _adamw(P[n].reshape(s3), grads[n].reshape(s3), P["m_" + n].reshape(s3), P["v_" + n].reshape(s3), "adamw")
        delta[n], new_m[n], new_v[n] = d.reshape(P[n].shape), m2.reshape(P[n].shape), v2.reshape(P[n].shape)
    shapes = [P[n].shape for n in SMALL]
    packed = [_pack([src[n] for n in SMALL])[None] for src in
              (P, grads, {n: P["m_" + n] for n in SMALL}, {n: P["v_" + n] for n in SMALL})]
    for res, out in zip(_adamw(*packed, "adamw_small"), (delta, new_m, new_v)):
        for n, a in zip(SMALL, _unpack(res[0], shapes)):
            out[n] = a

    return (loss, grad_x[None], *[grads[n] for n in WEIGHTS], *[delta[n] for n in WEIGHTS],
            *[new_m[n] for n in WEIGHTS], *[new_v[n] for n in WEIGHTS])
```

```python
import functools

import jax
import jax.numpy as jnp
from jax import lax
from jax.experimental import pallas as pl
from jax.experimental.pallas import tpu as pltpu

F32 = jnp.float32
BF16 = jnp.bfloat16
MESH = pl.DeviceIdType.MESH

D_MODEL = 1024
N_CHIPS = 4
N_DEV = 8
EPS_RMS = 1e-6
EPS_LN = 1e-5
CONV_K = 31
POOL_WINDOWS = (2, 4, 8, 16)
POOL_GW = 256
LRU_HEADS = 12
LRU_HD = 128
W_LRU = LRU_HEADS * LRU_HD
LRU_CONV_K = 4
LRU_C = 8.0
ADAM_LR = 0.001
ADAM_B1 = 0.9
ADAM_B2 = 0.999
ADAM_EPS = 1e-08
ADAM_WD = 0.01
ADAM_STEP = 10

VMEM_LIMIT_BYTES = 56 * 1024 * 1024
ROW_TILE = 512
MIX_TILE = 256
EVEN_HALO = 32
ODD_HALO = 8


def _pallas(body, **kw):
    return pl.pallas_call(body, **kw)


def _params(*sem):
    return pltpu.CompilerParams(dimension_semantics=sem if sem else None, vmem_limit_bytes=VMEM_LIMIT_BYTES)


def _sigmoid(x):
    return 0.5 * jnp.tanh(0.5 * x) + 0.5


def _dsilu(x, s):
    return s * (1.0 + x * (1.0 - s))


def _nt(a, b):
    return lax.dot_general(a, b, (((1,), (1,)), ((), ())), preferred_element_type=F32)


def _tn(a, b):
    return lax.dot_general(a, b, (((0,), (0,)), ((), ())), preferred_element_type=F32)


def _in_proj(h, g, wg, layer, after, name):
    T, D = h.shape
    _, nblk, _, nb = wg.shape

    nrow = T // ROW_TILE

    def body(h_ref, g_ref, w_ref, after_ref, p_ref, n_ref, n_all):
        j, i = pl.program_id(0), pl.program_id(1)

        @pl.when(j == 0)
        def _():
            x = h_ref[...]
            r = lax.rsqrt(jnp.mean(x * x, axis=-1, keepdims=True) + EPS_RMS)
            nn = (x * r * g_ref[...]).astype(BF16)
            n_ref[...] = nn
            n_all[i] = nn

        p_ref[...] = jnp.dot(n_all[i], w_ref[0], preferred_element_type=F32)

    def rows_once(j, i):
        return (jnp.where(j == 0, i, nrow - 1), 0)

    return _pallas(
        body, name=name, grid=(nblk, nrow),
        in_specs=[pl.BlockSpec((ROW_TILE, D), rows_once), pl.BlockSpec((1, D), lambda j, i: (0, 0)),
                  pl.BlockSpec((None, 1, D, nb), lambda j, i: (layer, j, 0, 0)),
                  pl.BlockSpec((8, 128), lambda j, i: (0, 0))],
        out_specs=[pl.BlockSpec((ROW_TILE, nb), lambda j, i: (i, j)), pl.BlockSpec((ROW_TILE, D), rows_once)],
        out_shape=[jax.ShapeDtypeStruct((T, nblk * nb), F32), jax.ShapeDtypeStruct((T, D), BF16)],
        scratch_shapes=[pltpu.VMEM((nrow, ROW_TILE, D), BF16)],
        compiler_params=_params("arbitrary", "arbitrary"))(h, g, wg, after)


def _out_proj(y, w, layer, hres, name):
    T, K = y.shape
    D = w.shape[2]

    def body(y_ref, w_ref, r_ref, o_ref):
        o_ref[...] = r_ref[...] + jnp.dot(y_ref[...], w_ref[...], preferred_element_type=F32)

    return _pallas(
        body, name=name, grid=(T // ROW_TILE,),
        in_specs=[pl.BlockSpec((ROW_TILE, K), lambda i: (i, 0)), pl.BlockSpec((None, K, D), lambda i: (layer, 0, 0)),
                  pl.BlockSpec((ROW_TILE, D), lambda i: (i, 0))],
        out_specs=pl.BlockSpec((ROW_TILE, D), lambda i: (i, 0)),
        out_shape=jax.ShapeDtypeStruct((T, D), F32),
        compiler_params=_params("parallel"))(y, w, hres)


def _dn_proj(dp, wg, layer, h, g, dres, after, name):
    T, D = h.shape
    _, nblk, _, nb = wg.shape

    nrow = T // ROW_TILE

    def body(dp_ref, w_ref, h_ref, g_ref, dres_ref, after_ref, dh_ref, dhb_ref, dg_ref, acc_ref):
        j, i = pl.program_id(0), pl.program_id(1)
        part = _nt(dp_ref[...], w_ref[0])

        @pl.when(j == 0)
        def _():
            acc_ref[i] = part

        @pl.when(j > 0)
        def _():
            acc_ref[i] += part

        @pl.when(j == nblk - 1)
        def _():
            x = h_ref[...]
            r = lax.rsqrt(jnp.mean(x * x, axis=-1, keepdims=True) + EPS_RMS)
            dn = acc_ref[i]
            q = dn * g_ref[...]
            dh = dres_ref[...] + r * q - x * ((r * r * r) * jnp.mean(q * x, axis=-1, keepdims=True))
            dh_ref[...] = dh
            dhb_ref[...] = dh.astype(BF16)
            dgp = jnp.sum(dn * (x * r), axis=0, keepdims=True)

            @pl.when(i == 0)
            def _():
                dg_ref[...] = dgp

            @pl.when(i > 0)
            def _():
                dg_ref[...] += dgp

    def rows_last(j, i):
        return (jnp.where(j == nblk - 1, i, 0), 0)

    return _pallas(
        body, name=name, grid=(nblk, nrow),
        in_specs=[pl.BlockSpec((ROW_TILE, nb), lambda j, i: (i, j)),
                  pl.BlockSpec((None, 1, D, nb), lambda j, i: (layer, j, 0, 0)),
                  pl.BlockSpec((ROW_TILE, D), rows_last), pl.BlockSpec((1, D), lambda j, i: (0, 0)),
                  pl.BlockSpec((ROW_TILE, D), rows_last), pl.BlockSpec((8, 128), lambda j, i: (0, 0))],
        out_specs=[pl.BlockSpec((ROW_TILE, D), rows_last), pl.BlockSpec((ROW_TILE, D), rows_last),
                   pl.BlockSpec((1, D), lambda j, i: (0, 0))],
        out_shape=[jax.ShapeDtypeStruct((T, D), F32), jax.ShapeDtypeStruct((T, D), BF16),
                   jax.ShapeDtypeStruct((1, D), F32)],
        scratch_shapes=[pltpu.VMEM((nrow, ROW_TILE, D), F32)],
        compiler_params=_params("arbitrary", "arbitrary"))(dp, wg, h, g, dres, after)


def _dw_in(n, dp, nblk, layer, nlayers, prev, name):
    T, D = n.shape
    nb = dp.shape[1] // nblk
    ta = 512

    def body(n_ref, dp_ref, *rest):
        rest[-1][0] = _tn(n_ref[...], dp_ref[...]).astype(BF16)

    in_specs = [pl.BlockSpec((T, ta), lambda j, i: (0, i)), pl.BlockSpec((T, nb), lambda j, i: (0, j))]
    args = (n, dp) if prev is None else (n, dp, prev)
    return _pallas(
        body, name=name, grid=(nblk, D // ta), in_specs=in_specs + ([] if prev is None else [ANY]),
        out_specs=pl.BlockSpec((None, 1, ta, nb), lambda j, i: (layer, j, i, 0)),
        out_shape=jax.ShapeDtypeStruct((nlayers, nblk, D, nb), BF16),
        input_output_aliases={} if prev is None else {2: 0},
        compiler_params=_params("parallel", "parallel"))(*args)


def _dw_out(y, dout, layer, nlayers, prev, name):
    T, K = y.shape
    D = dout.shape[1]
    tk = 512

    def body(y_ref, d_ref, *rest):
        rest[-1][...] = _tn(y_ref[...], d_ref[...]).astype(BF16)

    in_specs = [pl.BlockSpec((T, tk), lambda i: (0, i)), pl.BlockSpec((T, D), lambda i: (0, 0))]
    args = (y, dout) if prev is None else (y, dout, prev)
    return _pallas(
        body, name=name, grid=(K // tk,), in_specs=in_specs + ([] if prev is None else [ANY]),
        out_specs=pl.BlockSpec((None, tk, D), lambda i: (layer, i, 0)),
        out_shape=jax.ShapeDtypeStruct((nlayers, K, D), BF16),
        input_output_aliases={} if prev is None else {2: 0},
        compiler_params=_params("parallel"))(*args)


def _loss_head(h, g, tgt):
    T, D = h.shape
    tm = MIX_TILE

    def body(h_ref, g_ref, t_ref, dh_ref, dhb_ref, dg_ref, loss_ref):
        i = pl.program_id(0)
        x = h_ref[...]
        gg = g_ref[...]
        r = lax.rsqrt(jnp.mean(x * x, axis=-1, keepdims=True) + EPS_RMS)
        xr = x * r
        e = xr * gg - t_ref[...]
        lp = 0.5 * jnp.sum(jnp.mean(e * e, axis=-1, keepdims=True), axis=0, keepdims=True)
        dn = e * (1.0 / D)
        q = dn * gg
        dh = r * q - x * ((r * r * r) * jnp.mean(q * x, axis=-1, keepdims=True))
        dh_ref[...] = dh
        dhb_ref[...] = dh.astype(BF16)
        dgp = jnp.sum(dn * xr, axis=0, keepdims=True)

        @pl.when(i == 0)
        def _():
            dg_ref[...] = dgp
            loss_ref[...] = lp

        @pl.when(i > 0)
        def _():
            dg_ref[...] += dgp
            loss_ref[...] += lp

    return _pallas(
        body, name="loss_head", grid=(T // tm,),
        in_specs=[pl.BlockSpec((tm, D), lambda i: (i, 0)), pl.BlockSpec((1, D), lambda i: (0, 0)),
                  pl.BlockSpec((tm, D), lambda i: (i, 0))],
        out_specs=[pl.BlockSpec((tm, D), lambda i: (i, 0)), pl.BlockSpec((tm, D), lambda i: (i, 0)),
                   pl.BlockSpec((1, D), lambda i: (0, 0)), pl.BlockSpec((1, 1), lambda i: (0, 0))],
        out_shape=[jax.ShapeDtypeStruct((T, D), F32), jax.ShapeDtypeStruct((T, D), BF16),
                   jax.ShapeDtypeStruct((1, D), F32), jax.ShapeDtypeStruct((1, 1), F32)],
        compiler_params=_params("arbitrary"))(h, g, tgt)


def _shift_up(x, j):
    return x if j == 0 else pltpu.roll(x, x.shape[0] - j, 0)


def _shift_down(x, j):
    return x if j == 0 else pltpu.roll(x, j, 0)


def _fill_shifted(dst_ref, src_ref):
    rows = dst_ref.shape[1]
    for s in range(8):
        dst_ref[s] = src_ref[pl.ds(s, rows), :]


def _fill_taps(wb_ref, w_ref):
    for k in range(w_ref.shape[0]):
        wb_ref[k] = jnp.broadcast_to(w_ref[k:k + 1, :], wb_ref.shape[1:])


def _tap_sum(sh_ref, wb_ref, r0, nrows, offsets):
    accs = [None] * (nrows // 8)
    for k, o in enumerate(offsets):
        wk = wb_ref[k]
        for u in range(nrows // 8):
            term = wk * sh_ref[o % 8, pl.ds(r0 + (o // 8) * 8 + 8 * u, 8), :]
            accs[u] = term if accs[u] is None else accs[u] + term
    return jnp.concatenate(accs, axis=0)


def _pool_sums(vx, up):
    sh = _shift_up if up else _shift_down
    outs = []
    for gi, w in enumerate(POOL_WINDOWS):
        s = vx[:, gi * POOL_GW:(gi + 1) * POOL_GW]
        j = 1
        while j < w:
            s = s + sh(s, j)
            j *= 2
        outs.append(s)
    return outs


def _inv_count(row0, nrows):
    pos = (row0 + 1 + lax.broadcasted_iota(jnp.int32, (nrows, 1), 0)).astype(F32)
    return [1.0 / jnp.minimum(pos, float(w)) for w in POOL_WINDOWS]


def _even_mixer_fwd(p, cw, cb, lg, lb, pw, pb, sc, name):
    T = p.shape[0]
    C = D_MODEL
    tT, HL = MIX_TILE, EVEN_HALO
    hb = tT // HL
    chunk = 32

    def body(pm_ref, ph_ref, cw_ref, cb_ref, lg_ref, lb_ref, pw_ref, pb_ref, sc_ref, y_ref, u1_ref, u0x_ref, sh_ref,
             wb_ref):
        i = pl.program_id(0)
        keep = (i > 0).astype(F32)

        @pl.when(i == 0)
        def _():
            _fill_taps(wb_ref, cw_ref)

        u0x_ref[0:HL] = ph_ref[:, 0:C] * _sigmoid(ph_ref[:, C:2 * C]) * keep
        u0x_ref[HL:HL + tT] = pm_ref[:, 0:C] * _sigmoid(pm_ref[:, C:2 * C])
        u0x_ref[HL + tT:HL + tT + 8] = jnp.zeros((8, C), F32)
        _fill_shifted(sh_ref, u0x_ref)
        offs = [HL - (CONV_K - 1) + k for k in range(CONV_K)]

        def conv_chunk(c, carry):
            r0 = pl.multiple_of(c * chunk, chunk)
            u1_ref[pl.ds(r0, chunk), :] = _tap_sum(sh_ref, wb_ref, r0, chunk, offs) + cb_ref[...]
            return carry

        lax.fori_loop(0, tT // chunk, conv_chunk, 0)
        u1 = u1_ref[...]
        mu = jnp.mean(u1, axis=-1, keepdims=True)
        xc = u1 - mu
        rs = lax.rsqrt(jnp.mean(xc * xc, axis=-1, keepdims=True) + EPS_LN)
        u2 = xc * rs * lg_ref[...] + lb_ref[...]
        u3 = u2 * _sigmoid(u2)
        ag = pm_ref[:, 2 * C:3 * C]
        y_ref[:, 0:C] = (u3 * (ag * _sigmoid(ag))).astype(BF16)
        vx = jnp.concatenate([ph_ref[:, 3 * C:4 * C] * keep, pm_ref[:, 3 * C:4 * C]], axis=0)
        sums = _pool_sums(vx, up=False)
        inv = _inv_count(i * tT, tT)
        for gi in range(len(POOL_WINDOWS)):
            cols = slice(gi * POOL_GW, (gi + 1) * POOL_GW)
            d0 = sums[gi][HL:] * inv[gi] - vx[HL:, cols]
            d1 = jnp.dot(d0.astype(BF16), pw_ref[gi], preferred_element_type=F32) + pb_ref[:, cols]
            bg = pm_ref[:, 4 * C + gi * POOL_GW:4 * C + (gi + 1) * POOL_GW]
            y_ref[:, C + gi * POOL_GW:C + (gi + 1) * POOL_GW] = (d1 * sc_ref[:, cols] * (bg * _sigmoid(bg))).astype(BF16)

    vec = pl.BlockSpec((1, C), lambda i: (0, 0))
    return _pallas(
        body, name=name, grid=(T // tT,),
        in_specs=[pl.BlockSpec((tT, 5 * C), lambda i: (i, 0)),
                  pl.BlockSpec((HL, 5 * C), lambda i: (jnp.maximum(i * hb - 1, 0), 0)),
                  pl.BlockSpec((32, C), lambda i: (0, 0)), vec, vec, vec,
                  pl.BlockSpec((4, POOL_GW, POOL_GW), lambda i: (0, 0, 0)), vec, vec],
        out_specs=[pl.BlockSpec((tT, 2 * C), lambda i: (i, 0)), pl.BlockSpec((tT, C), lambda i: (i, 0))],
        out_shape=[jax.ShapeDtypeStruct((T, 2 * C), BF16), jax.ShapeDtypeStruct((T, C), F32)],
        scratch_shapes=[pltpu.VMEM((HL + tT + 8, C), F32), pltpu.VMEM((8, HL + tT, C), F32),
                        pltpu.VMEM((32, 8, C), F32)],
        compiler_params=_params("arbitrary"))(p, p, cw, cb, lg, lb, pw, pb, sc)


def _even_mixer_bwd(p, u1, dout, w_out, after, cwr, lg, lb, pw, pb, sc, name):
    T = p.shape[0]
    C = D_MODEL
    tT, HL = MIX_TILE, EVEN_HALO
    hb = tT // HL
    nT = T // tT
    R1 = tT + HL
    chunk = 32

    def body(pm_ref, pp_ref, pn_ref, u1m_ref, u1n_ref, dom_ref, don_ref, wo_ref, after_ref, cwr_ref, lg_ref, lb_ref,
             pw_ref, pb_ref, sc_ref, dp_ref, dcw_ref, dvec_ref, dpw_ref, x_ref, sh_ref, du0_ref, wb_ref):
        i = pl.program_id(0)
        dy = _nt(jnp.concatenate([dom_ref[...], don_ref[...]], axis=0), wo_ref[...])

        @pl.when(i == 0)
        def _():
            _fill_taps(wb_ref, cwr_ref)

        keep_prev = (i > 0).astype(F32)
        keep_next = (i < nT - 1).astype(F32)
        row = lax.broadcasted_iota(jnp.int32, (R1, 1), 0)
        live = jnp.where(row < tT, 1.0, keep_next)

        def cat(m, n):
            return jnp.concatenate([m, n], axis=0)

        u1 = cat(u1m_ref[...], u1n_ref[...])
        mu = jnp.mean(u1, axis=-1, keepdims=True)
        xc = u1 - mu
        rs = lax.rsqrt(jnp.mean(xc * xc, axis=-1, keepdims=True) + EPS_LN)
        xh = xc * rs
        u2 = xh * lg_ref[...] + lb_ref[...]
        s2 = _sigmoid(u2)
        u3 = u2 * s2
        ag = cat(pm_ref[:, 2 * C:3 * C], pn_ref[:, 2 * C:3 * C])
        sa = _sigmoid(ag)
        dya = dy[:, 0:C]
        dp_ref[:, 2 * C:3 * C] = (dya * u3 * _dsilu(ag, sa))[0:tT].astype(BF16)
        du2 = dya * (ag * sa) * _dsilu(u2, s2)
        dlg = jnp.sum((du2 * xh)[0:tT], axis=0, keepdims=True)
        dlb = jnp.sum(du2[0:tT], axis=0, keepdims=True)
        dxh = du2 * lg_ref[...]
        du1 = rs * (dxh - jnp.mean(dxh, axis=-1, keepdims=True) - xh * jnp.mean(dxh * xh, axis=-1, keepdims=True))
        du1 = du1 * live
        dcb = jnp.sum(du1[0:tT], axis=0, keepdims=True)
        x_ref[0:R1] = du1
        x_ref[R1:R1 + 8] = jnp.zeros((8, C), F32)
        _fill_shifted(sh_ref, x_ref)

        def du0_chunk(c, carry):
            r0 = pl.multiple_of(c * chunk, chunk)
            du0_ref[pl.ds(r0, chunk), :] = _tap_sum(sh_ref, wb_ref, r0, chunk, list(range(CONV_K)))
            return carry

        lax.fori_loop(0, tT // chunk, du0_chunk, 0)
        av, agl = pm_ref[:, 0:C], pm_ref[:, C:2 * C]
        sg = _sigmoid(agl)
        du0 = du0_ref[...]
        dp_ref[:, 0:C] = (du0 * sg).astype(BF16)
        dp_ref[:, C:2 * C] = (du0 * av * sg * (1.0 - sg)).astype(BF16)
        du0_ref[...] = du1[0:tT]
        x_ref[0:HL] = pp_ref[:, 0:C] * _sigmoid(pp_ref[:, C:2 * C]) * keep_prev
        x_ref[HL:HL + tT] = av * sg
        x_ref[HL + tT:HL + tT + 8] = jnp.zeros((8, C), F32)
        _fill_shifted(sh_ref, x_ref)

        @pl.when(i == 0)
        def _():
            dcw_ref[...] = jnp.zeros_like(dcw_ref)

        for k in range(CONV_K):
            o = HL - (CONV_K - 1) + k

            def dw_chunk(c, acc, o=o):
                r0 = pl.multiple_of(c * 64, 64)
                for u in range(0, 64, 8):
                    acc = acc + du0_ref[pl.ds(r0 + u, 8), :] * sh_ref[o % 8, pl.ds(r0 + u + (o // 8) * 8, 8), :]
                return acc

            dcw_ref[8 * k:8 * k + 8, :] += lax.fori_loop(0, tT // 64, dw_chunk, jnp.zeros((8, C), F32))

        bg = cat(pm_ref[:, 4 * C:5 * C], pn_ref[:, 4 * C:5 * C])
        sb = _sigmoid(bg)
        dyb = dy[:, C:2 * C]
        dyb0 = dyb * (bg * sb)
        dd1 = dyb0 * sc_ref[...]
        dpb = jnp.sum(dd1[0:tT], axis=0, keepdims=True)
        inv1 = _inv_count(i * tT, R1)
        z_parts, dd0_parts = [], []
        for gi in range(len(POOL_WINDOWS)):
            cols = slice(gi * POOL_GW, (gi + 1) * POOL_GW)
            dd0 = _nt(dd1[:, cols].astype(BF16), pw_ref[gi])
            dd0_parts.append(dd0)
            z_parts.append(dd0 * inv1[gi] * live)
        fsum = _pool_sums(jnp.concatenate(z_parts, axis=1), up=True)
        vx = cat(pp_ref[:, 3 * C:4 * C] * keep_prev, pm_ref[:, 3 * C:4 * C])
        sums = _pool_sums(vx, up=False)
        inv0 = _inv_count(i * tT, tT)
        dsc_parts = []
        for gi in range(len(POOL_WINDOWS)):
            cols = slice(gi * POOL_GW, (gi + 1) * POOL_GW)
            dp_ref[:, 3 * C + gi * POOL_GW:3 * C + (gi + 1) * POOL_GW] = (fsum[gi][0:tT] - dd0_parts[gi][0:tT]).astype(BF16)
            d0 = (sums[gi][HL:] * inv0[gi] - vx[HL:, cols]).astype(BF16)
            d1 = jnp.dot(d0, pw_ref[gi], preferred_element_type=F32) + pb_ref[:, cols]
            bgm, sbm = bg[0:tT, cols], sb[0:tT, cols]
            dp_ref[:, 4 * C + gi * POOL_GW:4 * C + (gi + 1) * POOL_GW] = (
                dyb[0:tT, cols] * d1 * sc_ref[:, cols] * _dsilu(bgm, sbm)).astype(BF16)
            dsc_parts.append(jnp.sum(dyb0[0:tT, cols] * d1, axis=0, keepdims=True))
            dpw_g = _tn(d0, dd1[0:tT, cols].astype(BF16))

            @pl.when(i == 0)
            def _(gi=gi, dpw_g=dpw_g):
                dpw_ref[gi] = dpw_g

            @pl.when(i > 0)
            def _(gi=gi, dpw_g=dpw_g):
                dpw_ref[gi] += dpw_g

        dsc = jnp.concatenate(dsc_parts, axis=1)
        vecs = jnp.concatenate([dcb, dlg, dlb, dsc, dpb, jnp.zeros((3, C), F32)], axis=0)

        @pl.when(i == 0)
        def _():
            dvec_ref[...] = vecs

        @pl.when(i > 0)
        def _():
            dvec_ref[...] += vecs

    vec = pl.BlockSpec((1, C), lambda i: (0, 0))
    taps = pl.BlockSpec((32, C), lambda i: (0, 0))

    def prev_blk(i):
        return (jnp.maximum(i * hb - 1, 0), 0)

    def next_blk(i):
        return (jnp.minimum((i + 1) * hb, T // HL - 1), 0)

    return _pallas(
        body, name=name, grid=(nT,),
        in_specs=[pl.BlockSpec((tT, 5 * C), lambda i: (i, 0)), pl.BlockSpec((HL, 5 * C), prev_blk),
                  pl.BlockSpec((HL, 5 * C), next_blk),
                  pl.BlockSpec((tT, C), lambda i: (i, 0)), pl.BlockSpec((HL, C), next_blk),
                  pl.BlockSpec((tT, C), lambda i: (i, 0)), pl.BlockSpec((HL, C), next_blk),
                  pl.BlockSpec((None, 2 * C, C), lambda i: (0, 0, 0)), pl.BlockSpec((8, 128), lambda i: (0, 0)),
                  taps, vec, vec, pl.BlockSpec((4, POOL_GW, POOL_GW), lambda i: (0, 0, 0)), vec, vec],
        out_specs=[pl.BlockSpec((tT, 5 * C), lambda i: (i, 0)), pl.BlockSpec((32 * 8, C), lambda i: (0, 0)),
                   pl.BlockSpec((8, C), lambda i: (0, 0)), pl.BlockSpec((4, POOL_GW, POOL_GW), lambda i: (0, 0, 0))],
        out_shape=[jax.ShapeDtypeStruct((T, 5 * C), BF16), jax.ShapeDtypeStruct((32 * 8, C), F32),
                   jax.ShapeDtypeStruct((8, C), F32), jax.ShapeDtypeStruct((4, POOL_GW, POOL_GW), F32)],
        scratch_shapes=[pltpu.VMEM((R1 + 8, C), F32), pltpu.VMEM((8, R1, C), F32), pltpu.VMEM((tT, C), F32),
                        pltpu.VMEM((32, 8, C), F32)],
        compiler_params=_params("arbitrary"))(p, p, p, u1, u1, dout, dout, w_out, after, cwr, lg, lb, pw, pb, sc)


def _softplus(z):
    u = jnp.exp(-jnp.abs(z))
    w = 1.0 + u
    l1p = jnp.where(w == 1.0, u, u * jnp.log(w) / jnp.where(w == 1.0, 1.0, w - 1.0))
    return jnp.maximum(z, 0.0) + l1p


def _lru_gates(xrx, cw_ref, cb_ref, wr_ref, br_ref, wi_ref, bi_ref, lam_ref):
    HL = ODD_HALO
    xc = cb_ref[...] + cw_ref[LRU_CONV_K - 1:LRU_CONV_K, :] * xrx[HL:]
    for k in range(LRU_CONV_K - 1):
        xc = xc + cw_ref[k:k + 1, :] * _shift_down(xrx, LRU_CONV_K - 1 - k)[HL:]
    xcb = xc.astype(BF16)
    rp, ip = [], []
    for hd in range(LRU_HEADS):
        cols = slice(hd * LRU_HD, (hd + 1) * LRU_HD)
        rp.append(jnp.dot(xcb[:, cols], wr_ref[hd], preferred_element_type=F32))
        ip.append(jnp.dot(xcb[:, cols], wi_ref[hd], preferred_element_type=F32))
    r = _sigmoid(jnp.concatenate(rp, axis=1) + br_ref[...])
    ig = _sigmoid(jnp.concatenate(ip, axis=1) + bi_ref[...])
    sp = _softplus(-lam_ref[...])
    log_a = (-LRU_C) * r * sp
    a = jnp.exp(log_a)
    m2 = jnp.maximum(-jnp.tanh(log_a) * (a * a + 1.0), 1e-30)
    inv_mult = lax.rsqrt(m2)
    return xc, xcb, r, ig, sp, a, m2 * inv_mult, inv_mult


def _group_scan(a, b, reverse):
    n, w = a.shape
    a, b = a.reshape(n // 8, 8, w), b.reshape(n // 8, 8, w)
    pos = lax.broadcasted_iota(jnp.int32, (1, 8, 1), 1)
    s = 1
    while s < 8:
        ok = (pos < 8 - s) if reverse else (pos >= s)
        shift = (8 - s) if reverse else s
        a_sh = jnp.where(ok, pltpu.roll(a, shift, 1), 1.0)
        b_sh = jnp.where(ok, pltpu.roll(b, shift, 1), 0.0)
        b = a * b_sh + b
        a = a * a_sh
        s *= 2
    return a.reshape(n, w), b.reshape(n, w)


def _apply_carries(a_ref, b_ref, out_ref, c0, reverse):
    ng = a_ref.shape[0] // 8

    def step(t, c):
        r0 = pl.multiple_of(((ng - 1 - t) if reverse else t) * 8, 8)
        x = a_ref[pl.ds(r0, 8), :] * c + b_ref[pl.ds(r0, 8), :]
        out_ref[pl.ds(r0, 8), :] = x
        return x[0:1, :] if reverse else x[7:8, :]

    return lax.fori_loop(0, ng, step, c0)


def _odd_mixer_fwd(p, cw, cb, wr, br, wi, bi, lam, name):
    T = p.shape[0]
    W = W_LRU
    tT, HL = MIX_TILE, ODD_HALO
    hb = tT // HL

    def body(pm_ref, ph_ref, cw_ref, cb_ref, wr_ref, br_ref, wi_ref, bi_ref, lam_ref, y_ref, hs_ref, carry_ref,
             sa_ref, sb_ref):
        i = pl.program_id(0)
        keep = (i > 0).astype(F32)

        @pl.when(i == 0)
        def _():
            carry_ref[...] = jnp.zeros_like(carry_ref)

        xrx = jnp.concatenate([ph_ref[:, 0:W] * keep, pm_ref[:, 0:W]], axis=0)
        xc, _, _, ig, _, a, mult, _ = _lru_gates(xrx, cw_ref, cb_ref, wr_ref, br_ref, wi_ref, bi_ref, lam_ref)
        sa_ref[...], sb_ref[...] = _group_scan(a, mult * (ig * xc), reverse=False)
        last = _apply_carries(sa_ref, sb_ref, hs_ref, carry_ref[0:1, :], reverse=False)
        carry_ref[...] = jnp.broadcast_to(last, (8, W))
        hs = hs_ref[...]
        gt = pm_ref[:, W:2 * W]
        y_ref[...] = (hs * (gt * _sigmoid(gt))).astype(BF16)

    vec = pl.BlockSpec((1, W), lambda i: (0, 0))
    heads = pl.BlockSpec((LRU_HEADS, LRU_HD, LRU_HD), lambda i: (0, 0, 0))
    return _pallas(
        body, name=name, grid=(T // tT,),
        in_specs=[pl.BlockSpec((tT, 2 * W), lambda i: (i, 0)),
                  pl.BlockSpec((HL, 2 * W), lambda i: (jnp.maximum(i * hb - 1, 0), 0)),
                  pl.BlockSpec((8, W), lambda i: (0, 0)), vec, heads, vec, heads, vec, vec],
        out_specs=[pl.BlockSpec((tT, W), lambda i: (i, 0)), pl.BlockSpec((tT, W), lambda i: (i, 0))],
        out_shape=[jax.ShapeDtypeStruct((T, W), BF16), jax.ShapeDtypeStruct((T, W), F32)],
        scratch_shapes=[pltpu.VMEM((8, W), F32), pltpu.VMEM((tT, W), F32), pltpu.VMEM((tT, W), F32)],
        compiler_params=_params("arbitrary"))(p, p, cw, cb, wr, br, wi, bi, lam)


def _odd_mixer_bwd(p, hs, dout, w_out, after, cw, cb, wr, br, wi, bi, lam, name):
    T = p.shape[0]
    W = W_LRU
    D = dout.shape[1]
    tT, HL = MIX_TILE, ODD_HALO
    hb = tT // HL
    nT = T // tT

    def body(pm_ref, ph_ref, hsm_ref, hsh_ref, do_ref, wo_ref, after_ref, cw_ref, cb_ref, wr_ref, br_ref, wi_ref,
             bi_ref, lam_ref, dp_ref, dwr_ref, dwi_ref, dvec_ref, gcarry_ref, xcarry_ref, sa_ref, sb_ref, g_ref):
        i = pl.program_id(0)
        keep = (i < nT - 1).astype(F32)

        @pl.when(i == 0)
        def _():
            gcarry_ref[...] = jnp.zeros_like(gcarry_ref)
            xcarry_ref[...] = jnp.zeros_like(xcarry_ref)

        xrx = jnp.concatenate([ph_ref[:, 0:W] * keep, pm_ref[:, 0:W]], axis=0)
        xc, xcb, r, ig, sp, a, mult, inv_mult = _lru_gates(xrx, cw_ref, cb_ref, wr_ref, br_ref, wi_ref, bi_ref, lam_ref)
        hs = hsm_ref[...]
        gt = pm_ref[:, W:2 * W]
        sg = _sigmoid(gt)
        dyv = _nt(do_ref[...], wo_ref[...])
        dp_ref[:, W:2 * W] = (dyv * hs * _dsilu(gt, sg)).astype(BF16)
        row = lax.broadcasted_iota(jnp.int32, (tT, 1), 0)
        m = jnp.where(row == tT - 1, 1.0, _shift_up(a, 1))
        sa_ref[...], sb_ref[...] = _group_scan(m, dyv * (gt * sg), reverse=True)
        first = _apply_carries(sa_ref, sb_ref, g_ref, gcarry_ref[0:1, :], reverse=True)
        G = g_ref[...]
        gcarry_ref[...] = jnp.broadcast_to(a[0:1, :] * first, (8, W))
        hs_prev = jnp.where(row == 0, hsh_ref[HL - 1:HL, :] * keep, _shift_down(hs, 1))
        da = G * hs_prev
        dmult = G * (ig * xc)
        di = G * mult * xc
        dxc = G * mult * ig
        dlog_a = da * a - dmult * (a * a) * inv_mult
        drp = dlog_a * ((-LRU_C) * sp) * r * (1.0 - r)
        dip = di * ig * (1.0 - ig)
        dlam = jnp.sum(dlog_a * ((-LRU_C) * r), axis=0, keepdims=True) * (-_sigmoid(-lam_ref[...]))
        drb, dib = drp.astype(BF16), dip.astype(BF16)
        back = []
        for hd in range(LRU_HEADS):
            cols = slice(hd * LRU_HD, (hd + 1) * LRU_HD)
            back.append(_nt(drb[:, cols], wr_ref[hd]) + _nt(dib[:, cols], wi_ref[hd]))
            dwr_h = _tn(xcb[:, cols], drb[:, cols])
            dwi_h = _tn(xcb[:, cols], dib[:, cols])

            @pl.when(i == 0)
            def _(hd=hd, dwr_h=dwr_h, dwi_h=dwi_h):
                dwr_ref[hd] = dwr_h
                dwi_ref[hd] = dwi_h

            @pl.when(i > 0)
            def _(hd=hd, dwr_h=dwr_h, dwi_h=dwi_h):
                dwr_ref[hd] += dwr_h
                dwi_ref[hd] += dwi_h

        dxc = dxc + jnp.concatenate(back, axis=1)
        dxcx = jnp.concatenate([dxc, xcarry_ref[...]], axis=0)
        dxr = cw_ref[LRU_CONV_K - 1:LRU_CONV_K, :] * dxc
        rows = []
        for k in range(LRU_CONV_K - 1):
            j = LRU_CONV_K - 1 - k
            dxr = dxr + cw_ref[k:k + 1, :] * _shift_up(dxcx, j)[0:tT]
            rows.append(jnp.sum(dxc * _shift_down(xrx, j)[HL:], axis=0, keepdims=True))
        rows.append(jnp.sum(dxc * xrx[HL:], axis=0, keepdims=True))
        dp_ref[:, 0:W] = dxr.astype(BF16)
        xcarry_ref[...] = dxc[0:8]
        rows += [jnp.sum(dxc, axis=0, keepdims=True), jnp.sum(drp, axis=0, keepdims=True),
                 jnp.sum(dip, axis=0, keepdims=True), dlam]
        vecs = jnp.concatenate(rows, axis=0)

        @pl.when(i == 0)
        def _():
            dvec_ref[...] = vecs

        @pl.when(i > 0)
        def _():
            dvec_ref[...] += vecs

    vec = pl.BlockSpec((1, W), lambda i: (0, 0))
    heads = pl.BlockSpec((LRU_HEADS, LRU_HD, LRU_HD), lambda i: (0, 0, 0))

    def tile(i):
        return (nT - 1 - i, 0)

    def prev_blk(i):
        return (jnp.maximum((nT - 1 - i) * hb - 1, 0), 0)

    return _pallas(
        body, name=name, grid=(nT,),
        in_specs=[pl.BlockSpec((tT, 2 * W), tile), pl.BlockSpec((HL, 2 * W), prev_blk),
                  pl.BlockSpec((tT, W), tile), pl.BlockSpec((HL, W), prev_blk), pl.BlockSpec((tT, D), tile),
                  pl.BlockSpec((None, W, D), lambda i: (0, 0, 0)), pl.BlockSpec((8, 128), lambda i: (0, 0)),
                  pl.BlockSpec((8, W), lambda i: (0, 0)), vec, heads, vec, heads, vec, vec],
        out_specs=[pl.BlockSpec((tT, 2 * W), tile), heads, heads, pl.BlockSpec((8, W), lambda i: (0, 0))],
        out_shape=[jax.ShapeDtypeStruct((T, 2 * W), BF16), jax.ShapeDtypeStruct((LRU_HEADS, LRU_HD, LRU_HD), F32),
                   jax.ShapeDtypeStruct((LRU_HEADS, LRU_HD, LRU_HD), F32), jax.ShapeDtypeStruct((8, W), F32)],
        scratch_shapes=[pltpu.VMEM((8, W), F32), pltpu.VMEM((8, W), F32), pltpu.VMEM((tT, W), F32),
                        pltpu.VMEM((tT, W), F32), pltpu.VMEM((tT, W), F32)],
        compiler_params=_params("arbitrary"))(p, p, hs, hs, dout, w_out, after, cw, cb, wr, br, wi, bi, lam)


def _pad_rows(a, rows):
    return jnp.concatenate([a, jnp.zeros((rows - a.shape[0], a.shape[1]), a.dtype)], axis=0)


def _layer_fwd(even, h, w, w_in, w_out, after):
    if even:
        p, n = _in_proj(h, w["norm"], w_in, 0, after, "in_proj_even")
        y, aux = _even_mixer_fwd(p, w["conv_w"], w["conv_b"], w["ln_g"], w["ln_b"], w["pool_w"], w["pool_b"],
                                 w["pool_scale"], "even_mixer_fwd")
        h_next = _out_proj(y, w_out, 0, h, "out_proj_even")
    else:
        p, n = _in_proj(h, w["norm"], w_in, 0, after, "in_proj_odd")
        y, aux = _odd_mixer_fwd(p, w["conv_w"], w["conv_b"], w["w_rg"], w["b_rg"], w["w_ig"], w["b_ig"], w["lam"],
                                "odd_mixer_fwd")
        h_next = _out_proj(y, w_out, 0, h, "out_proj_odd")
    return h_next, (h, n, p, aux, y)


def _layer_bwd_weights(even, saved, w, w_out, dhb, after):
    h, n, p, aux, y = saved
    if even:
        dp, dcw, dvec, dpw = _even_mixer_bwd(p, aux, dhb, w_out, after, w["conv_w_rev"], w["ln_g"], w["ln_b"],
                                             w["pool_w"], w["pool_b"], w["pool_scale"], "even_mixer_bwd")
        dw_out = _dw_out(y, dhb, 0, 1, None, "dw_out_even")
        dw_in = _dw_in(n, dp, N_CHIPS, 0, 1, None, "dw_in_even")
        return dp, dw_in, dw_out, dict(conv_w=dcw, vec=dvec, pool_w=dpw)
    dp, dwr, dwi, dvec = _odd_mixer_bwd(p, aux, dhb, w_out, after, w["conv_w"], w["conv_b"], w["w_rg"], w["b_rg"],
                                        w["w_ig"], w["b_ig"], w["lam"], "odd_mixer_bwd")
    dw_out = _dw_out(y, dhb, 0, 1, None, "dw_out_odd")
    dw_in = _dw_in(n, dp, N_CHIPS, 0, 1, None, "dw_in_odd")
    return dp, dw_in, dw_out, dict(w_rg=dwr, w_ig=dwi, vec=dvec)


def _layer_bwd_input(even, saved, w, w_in, dp, dh, after):
    return _dn_proj(dp, w_in, 0, saved[0], w["norm"], dh, after, "dn_proj_even" if even else "dn_proj_odd")


ANY = pl.BlockSpec(memory_space=pl.ANY)


def _mesh_pos():
    return lax.axis_index("x"), lax.axis_index("y"), lax.axis_index("c")


def _other_chips(x, y):
    return [(1 - x, y), (x, 1 - y), (1 - x, 1 - y)]


def _other_devices(x, y, c):
    out = []
    for p in range(1, N_DEV):
        out.append((1 - x if p & 4 else x, 1 - y if p & 2 else y, 1 - c if p & 1 else c))
    return out


def _remote(src, dst, ssem, rsem, dev):
    return pltpu.make_async_remote_copy(src_ref=src, dst_ref=dst, send_sem=ssem, recv_sem=rsem, device_id=dev,
                                        device_id_type=MESH)


def _comm_call(body, name, ins, out_shape, scratch, aliases=None):
    return _pallas(body, name=name, in_specs=[ANY] * len(ins), out_specs=[ANY] * len(out_shape), out_shape=out_shape,
                   scratch_shapes=scratch, input_output_aliases=aliases or {},
                   compiler_params=pltpu.CompilerParams(has_side_effects=True))(*ins)


def _cast_shard(w, layer, pos):
    _, R, C = w.shape
    tr = _row_tile(R, C)

    def body(pos_ref, w_ref, o_ref):
        o_ref[...] = w_ref[...].astype(BF16)

    grid_spec = pltpu.PrefetchScalarGridSpec(
        num_scalar_prefetch=1, grid=(R // tr,),
        in_specs=[pl.BlockSpec((None, tr, C), lambda i, pr: (layer, i, 0))],
        out_specs=pl.BlockSpec((None, None, tr, C), lambda i, pr: (0, pr[0], i, 0)))
    return _pallas(body, name="cast_shard", grid_spec=grid_spec,
                   out_shape=jax.ShapeDtypeStruct((1, N_CHIPS, R, C), BF16),
                   compiler_params=_params("parallel"))(pos, w)


def _gather_weights(big, small):
    nA = len(big)
    half = [a.shape[2] // 2 for a in big]

    def body(*refs):
        ins, outs = refs[:nA + 1], refs[nA + 1:2 * nA + 2]
        ssem, rsem, fsem, frsem, lsem = refs[2 * nA + 2:]
        x, y, c = _mesh_pos()
        k = 2 * x + y
        chips = _other_chips(x, y)
        sib = (x, y, 1 - c)

        def slab(a, chip, core):
            return outs[a].at[:, chip, pl.ds(core * half[a], half[a]), :]

        local = [pltpu.make_async_copy(ins[nA], outs[nA].at[k], lsem.at[0])]
        for cp in local:
            cp.start()
        sends = []
        for j, (ox, oy) in enumerate(chips):
            for a in range(nA):
                sends.append(_remote(slab(a, k, c), slab(a, k, c), ssem.at[a, j], rsem.at[a, j], (ox, oy, c)))
            sends.append(_remote(ins[nA], outs[nA].at[k], ssem.at[nA, j], rsem.at[nA, j], (ox, oy, c)))
        for cp in sends:
            cp.start()
        for j, (ox, oy) in enumerate(chips):
            kj = 2 * ox + oy
            for a in range(nA):
                got = slab(a, kj, c)
                _remote(got, got, ssem.at[a, j], rsem.at[a, j], (ox, oy, c)).wait_recv()
                fw = _remote(got, got, fsem.at[a, j], frsem.at[a, j], sib)
                fw.start()
                sends.append(fw)
            gs = outs[nA].at[kj]
            _remote(gs, gs, ssem.at[nA, j], rsem.at[nA, j], (ox, oy, c)).wait_recv()
        for j, (ox, oy) in enumerate(chips):
            kj = 2 * ox + oy
            for a in range(nA):
                theirs = slab(a, kj, 1 - c)
                _remote(theirs, theirs, fsem.at[a, j], frsem.at[a, j], sib).wait_recv()
        for cp in sends:
            cp.wait_send()
        for cp in local:
            cp.wait()

    out_shape = [jax.ShapeDtypeStruct(a.shape, a.dtype) for a in big]
    out_shape.append(jax.ShapeDtypeStruct((N_CHIPS,) + small.shape, small.dtype))
    scratch = [pltpu.SemaphoreType.DMA((nA + 1, 3)), pltpu.SemaphoreType.DMA((nA + 1, 3)),
               pltpu.SemaphoreType.DMA((nA, 3)), pltpu.SemaphoreType.DMA((nA, 3)), pltpu.SemaphoreType.DMA((1,))]
    return _comm_call(body, "gather_weights", list(big) + [small], out_shape, scratch, {a: a for a in range(nA)})


HBM = pl.BlockSpec(memory_space=pltpu.HBM)
SEM = pl.BlockSpec(memory_space=pltpu.SEMAPHORE)
EFFECT = pltpu.SideEffectType.DATAFLOW_SIDE_EFFECTING


def _split_start(arrays, copies, n, name):
    k = len(arrays)

    def body(*refs):
        for cp in copies(refs[k + 2:2 * k + 2], refs[k], refs[k + 1]):
            cp.start()
        refs[2 * k + 2][...] = jnp.zeros((8, 128), F32)

    out = _pallas(
        body, name=name,
        out_shape=(pltpu.SemaphoreType.DMA((n,)), pltpu.SemaphoreType.DMA((n,)),
                   *[pltpu.HBM(a.shape, a.dtype) for a in arrays], jax.ShapeDtypeStruct((8, 128), F32)),
        in_specs=(HBM,) * k, out_specs=(SEM, SEM) + (HBM,) * k + (pl.BlockSpec(memory_space=pltpu.VMEM),),
        input_output_aliases={i: i + 2 for i in range(k)},
        compiler_params=pltpu.CompilerParams(has_side_effects=EFFECT),
    )(*[pltpu.with_memory_space_constraint(a, pltpu.HBM) for a in arrays])
    return out[0], out[1], list(out[2:2 + k]), out[2 + k]


def _split_wait(ssem, rsem, arrays, copies, after, name):
    k = len(arrays)

    def body(*refs):
        for cp in copies(refs[:k], refs[k], refs[k + 1]):
            cp.wait_send()
            cp.wait_recv()

    out = _pallas(
        body, name=name, out_shape=tuple(pltpu.HBM(a.shape, a.dtype) for a in arrays),
        in_specs=(HBM,) * k + (SEM, SEM, ANY), out_specs=(HBM,) * k, input_output_aliases={i: i for i in range(k)},
        compiler_params=pltpu.CompilerParams(has_side_effects=EFFECT),
    )(*arrays, ssem, rsem, after)
    return list(out)


def _gather_copies(shapes):
    half = [s[2] // 2 for s in shapes]

    def copies(refs, ssem, rsem):
        x, y, c = _mesh_pos()
        out = []
        for j, (ox, oy) in enumerate(_other_chips(x, y)):
            for a, ref in enumerate(refs):
                slab = ref.at[:, 2 * x + y, pl.ds(c * half[a], half[a]), :]
                out.append(_remote(slab, slab, ssem.at[3 * a + j], rsem.at[3 * a + j], (ox, oy, c)))
        return out

    return copies


def _chips_copies(n_arr):
    def copies(refs, ssem, rsem):
        x, y, c = _mesh_pos()
        out = []
        for j, (ox, oy) in enumerate(_other_chips(x, y)):
            for a in range(n_arr):
                out.append(_remote(refs[a].at[:, 2 * ox + oy], refs[n_arr + a].at[:, 2 * x + y], ssem.at[3 * a + j],
                                   rsem.at[3 * a + j], (ox, oy, c)))
        return out

    return copies


def _forward_cores(arrays):
    nA = len(arrays)
    half = [a.shape[2] // 2 for a in arrays]

    def body(*refs):
        outs = refs[nA:2 * nA]
        ssem, rsem = refs[2 * nA:]
        x, y, c = _mesh_pos()
        sib = (x, y, 1 - c)
        sends, waits = [], []
        for j, (ox, oy) in enumerate(_other_chips(x, y)):
            for a in range(nA):
                got = outs[a].at[:, 2 * ox + oy, pl.ds(c * half[a], half[a]), :]
                sends.append(_remote(got, got, ssem.at[a, j], rsem.at[a, j], sib))
                theirs = outs[a].at[:, 2 * ox + oy, pl.ds((1 - c) * half[a], half[a]), :]
                waits.append(_remote(theirs, theirs, ssem.at[a, j], rsem.at[a, j], sib))
        for cp in sends:
            cp.start()
        for cp in waits:
            cp.wait_recv()
        for cp in sends:
            cp.wait_send()

    out_shape = [jax.ShapeDtypeStruct(a.shape, a.dtype) for a in arrays]
    scratch = [pltpu.SemaphoreType.DMA((nA, 3)), pltpu.SemaphoreType.DMA((nA, 3))]
    return _comm_call(body, "forward_cores", list(arrays), out_shape, scratch, {a: a for a in range(nA)})


def _exchange_halves(big):
    nA = len(big)
    half = [a.shape[2] // 2 for a in big]

    def body(*refs):
        ins, outs = refs[:nA], refs[nA:2 * nA]
        ssem, rsem = refs[2 * nA:]
        x, y, c = _mesh_pos()
        sib = (x, y, 1 - c)
        sends = [_remote(ins[a].at[:, :, pl.ds((1 - c) * half[a], half[a]), :], outs[a], ssem.at[a], rsem.at[a], sib)
                 for a in range(nA)]
        for cp in sends:
            cp.start()
        for a in range(nA):
            _remote(outs[a], outs[a], ssem.at[a], rsem.at[a], sib).wait_recv()
        for cp in sends:
            cp.wait_send()

    out_shape = [jax.ShapeDtypeStruct((a.shape[0], N_CHIPS, h, a.shape[3]), a.dtype) for a, h in zip(big, half)]
    scratch = [pltpu.SemaphoreType.DMA((nA,)), pltpu.SemaphoreType.DMA((nA,))]
    return _comm_call(body, "exchange_halves", list(big), out_shape, scratch)


def _exchange_final(grads, everywhere, small):
    nA = len(grads)
    n_remote = sum(N_DEV - 1 if ev else 1 for ev in everywhere) + N_DEV - 1

    def body(*refs):
        small_ref, outs, gathered = refs[nA], refs[nA + 1:2 * nA + 1], refs[2 * nA + 1]
        ssem, rsem, lsem = refs[2 * nA + 2:]
        x, y, c = _mesh_pos()
        k = 2 * x + y
        me = 2 * k + c
        sib = (x, y, 1 - c)
        peers = _other_devices(x, y, c)
        local = pltpu.make_async_copy(small_ref, gathered.at[me], lsem.at[0])
        local.start()
        sends, waits = [], []
        s = 0
        for (px, py, pc) in peers:
            sends.append(_remote(small_ref, gathered.at[me], ssem.at[s], rsem.at[s], (px, py, pc)))
            got = gathered.at[4 * px + 2 * py + pc]
            waits.append(_remote(got, got, ssem.at[s], rsem.at[s], (px, py, pc)))
            s += 1
        for a in range(nA):
            if everywhere[a]:
                r2 = grads[a].shape[1] // N_DEV
                mine = outs[a].at[:, pl.ds((2 * k + c) * r2, r2), :]
                for (px, py, pc) in peers:
                    sends.append(_remote(mine, mine, ssem.at[s], rsem.at[s], (px, py, pc)))
                    got = outs[a].at[:, pl.ds((2 * (2 * px + py) + pc) * r2, r2), :]
                    waits.append(_remote(got, got, ssem.at[s], rsem.at[s], (px, py, pc)))
                    s += 1
            else:
                r2 = grads[a].shape[1] // 2
                mine = outs[a].at[:, pl.ds(c * r2, r2), :]
                sends.append(_remote(mine, mine, ssem.at[s], rsem.at[s], sib))
                got = outs[a].at[:, pl.ds((1 - c) * r2, r2), :]
                waits.append(_remote(got, got, ssem.at[s], rsem.at[s], sib))
                s += 1
        for cp in sends:
            cp.start()
        for cp in waits:
            cp.wait_recv()
        for cp in sends:
            cp.wait_send()
        local.wait()

    out_shape = [jax.ShapeDtypeStruct(g.shape, g.dtype) for g in grads]
    out_shape.append(jax.ShapeDtypeStruct((N_DEV,) + small.shape, small.dtype))
    scratch = [pltpu.SemaphoreType.DMA((n_remote,)), pltpu.SemaphoreType.DMA((n_remote,)), pltpu.SemaphoreType.DMA((1,))]
    return _comm_call(body, "exchange_final", list(grads) + [small], out_shape, scratch, {a: a for a in range(nA)})


BLOCK_BYTES = 4 << 20


def _row_tile(rows, cols, mult=16, limit=BLOCK_BYTES):
    best = mult
    for t in range(mult, rows + 1, mult):
        if rows % t == 0 and t * cols * 4 <= limit:
            best = t
    return best


def _add_cores(own, recv, pos):
    L, _, R, C = own.shape
    r2 = R // 2
    tr = _row_tile(r2, C)
    nb = r2 // tr

    def body(pos_ref, a_ref, r_ref, o_ref):
        o_ref[...] = (a_ref[...].astype(F32) + r_ref[...].astype(F32)).astype(BF16)

    blk = (None, None, tr, C)
    grid_spec = pltpu.PrefetchScalarGridSpec(
        num_scalar_prefetch=1, grid=(L, N_CHIPS, nb),
        in_specs=[pl.BlockSpec(blk, lambda l, s, i, pr: (l, s, pr[1] * nb + i, 0)),
                  pl.BlockSpec(blk, lambda l, s, i, pr: (l, s, i, 0))],
        out_specs=pl.BlockSpec(blk, lambda l, s, i, pr: (l, s, i, 0)))
    return _pallas(body, name="add_cores", grid_spec=grid_spec,
                   out_shape=jax.ShapeDtypeStruct((L, N_CHIPS, r2, C), BF16),
                   compiler_params=_params("parallel", "parallel", "parallel"))(pos, own, recv)


def _sum_chips(own, recv, pos, everywhere, layer, nlayers, prev):
    _, _, r2, C = own.shape
    tr = _row_tile(r2, 2 * C)
    nb = r2 // tr

    def body(pos_ref, a_ref, r_ref, *rest):
        acc = None
        for s in range(N_CHIPS):
            term = jnp.where(pos_ref[0] == s, a_ref[...], r_ref[s]).astype(F32)
            acc = term if acc is None else acc + term
        rest[-1][...] = acc

    if everywhere:
        def out_map(i, pr):
            return (layer, (2 * pr[0] + pr[1]) * nb + i, 0)
    else:
        def out_map(i, pr):
            return (layer, pr[1] * nb + i, 0)

    in_specs = [pl.BlockSpec((None, None, tr, C), lambda i, pr: (0, pr[0], i, 0)),
                pl.BlockSpec((None, N_CHIPS, tr, C), lambda i, pr: (0, 0, i, 0))]
    grid_spec = pltpu.PrefetchScalarGridSpec(
        num_scalar_prefetch=1, grid=(nb,), in_specs=in_specs + ([] if prev is None else [ANY]),
        out_specs=pl.BlockSpec((None, tr, C), out_map))
    rows = (N_DEV if everywhere else 2) * r2
    args = (pos, own, recv) if prev is None else (pos, own, recv, prev)
    return _pallas(body, name="sum_chips", grid_spec=grid_spec, out_shape=jax.ShapeDtypeStruct((nlayers, rows, C), F32),
                   input_output_aliases={} if prev is None else {3: 0},
                   compiler_params=_params("parallel"))(*args)


def _sum_devices(parts):
    n, R, C = parts.shape
    tr = _row_tile(R, C * n, 8)

    def body(p_ref, o_ref):
        acc = p_ref[0]
        for s in range(1, n):
            acc = acc + p_ref[s]
        o_ref[...] = acc

    return _pallas(body, name="sum_devices", grid=(R // tr,), in_specs=[pl.BlockSpec((n, tr, C), lambda i: (0, i, 0))],
                   out_specs=pl.BlockSpec((tr, C), lambda i: (i, 0)), out_shape=jax.ShapeDtypeStruct((R, C), F32),
                   compiler_params=_params("parallel"))(parts)


def _adamw(w, g, m, v, name):
    L, R, C = w.shape
    tr = _row_tile(R, C, 8, BLOCK_BYTES // 2)

    def body(w_ref, g_ref, m_ref, v_ref, d_ref, m2_ref, v2_ref):
        gg = g_ref[...]
        m2 = ADAM_B1 * m_ref[...] + (1.0 - ADAM_B1) * gg
        v2 = ADAM_B2 * v_ref[...] + (1.0 - ADAM_B2) * (gg * gg)
        m_hat = m2 / (1.0 - ADAM_B1 ** ADAM_STEP)
        v_hat = v2 / (1.0 - ADAM_B2 ** ADAM_STEP)
        d_ref[...] = -ADAM_LR * (m_hat / (jnp.sqrt(v_hat) + ADAM_EPS) + ADAM_WD * w_ref[...])
        m2_ref[...] = m2
        v2_ref[...] = v2

    blk = pl.BlockSpec((1, tr, C), lambda l, i: (l, i, 0))
    shp = jax.ShapeDtypeStruct((L, R, C), F32)
    return _pallas(body, name=name, grid=(L, R // tr), in_specs=[blk] * 4, out_specs=[blk] * 3, out_shape=[shp] * 3,
                   compiler_params=_params("parallel", "parallel"))(w, g, m, v)


WEIGHTS = ("norm_even", "w_in_even", "conv_a_w", "conv_a_b", "ln_a_g", "ln_a_b", "pool_w", "pool_b", "pool_scale",
           "w_out_even", "norm_odd", "w_in_odd", "conv_c_w", "conv_c_b", "w_rg", "b_rg", "w_ig", "b_ig", "lru_lambda",
           "w_out_odd", "final_norm")
BIG = ("w_in_even", "w_out_even", "pool_w", "w_in_odd", "w_out_odd", "w_rg", "w_ig")
SMALL = tuple(n for n in WEIGHTS if n not in BIG)
SMALL_SHARDED = ("conv_a_w", "pool_b", "norm_odd", "conv_c_w", "conv_c_b", "b_rg", "b_ig", "lru_lambda")


def _pack(arrs):
    flat = jnp.concatenate([a.reshape(-1) for a in arrs])
    rows = -(-flat.shape[0] // (64 * 128)) * 64
    return jnp.pad(flat, (0, rows * 128 - flat.shape[0])).reshape(rows, 128)


def _unpack(buf, shapes, lead=()):
    flat = buf.reshape(tuple(lead) + (-1,))
    out, o = [], 0
    for s in shapes:
        n = 1
        for d in s:
            n *= d
        out.append(flat[..., o:o + n].reshape(tuple(lead) + tuple(s)))
        o += n
    return out


def _shard(full, axis, k):
    n = full.shape[axis] // N_CHIPS
    return lax.dynamic_slice_in_dim(full, k * n, n, axis)


def kernel(x, norm_even, w_in_even, conv_a_w, conv_a_b, ln_a_g, ln_a_b, pool_w, pool_b, pool_scale, w_out_even, norm_odd, w_in_odd, conv_c_w, conv_c_b, w_rg, b_rg, w_ig, b_ig, lru_lambda, w_out_odd, final_norm, loss_target, m_norm_even, m_w_in_even, m_conv_a_w, m_conv_a_b, m_ln_a_g, m_ln_a_b, m_pool_w, m_pool_b, m_pool_scale, m_w_out_even, m_norm_odd, m_w_in_odd, m_conv_c_w, m_conv_c_b, m_w_rg, m_b_rg, m_w_ig, m_b_ig, m_lru_lambda, m_w_out_odd, m_final_norm, v_norm_even, v_w_in_even, v_conv_a_w, v_conv_a_b, v_ln_a_g, v_ln_a_b, v_pool_w, v_pool_b, v_pool_scale, v_w_out_even, v_norm_odd, v_w_in_odd, v_conv_c_w, v_conv_c_b, v_w_rg, v_b_rg, v_w_ig, v_b_ig, v_lru_lambda, v_w_out_odd, v_final_norm):
    P = dict(locals())
    xi, yi, ci = _mesh_pos()
    k = 2 * xi + yi
    L = w_in_even.shape[0]
    D = D_MODEL

    pos = jnp.stack([k, ci]).astype(jnp.int32)
    depth = 2 * L
    pool_w3 = pool_w.reshape(L, 4 * 64, POOL_GW)

    def cast_group(layer):
        j = layer // 2
        if layer % 2 == 0:
            return [_cast_shard(w_in_even, j, pos), _cast_shard(w_out_even, j, pos), _cast_shard(pool_w3, j, pos)]
        return [_cast_shard(w_in_odd, j, pos), _cast_shard(w_out_odd, j, pos)]

    *group, g_small = _gather_weights(cast_group(0), _pack([P[n] for n in SMALL_SHARDED]))
    full = {}
    for n, a in zip(SMALL_SHARDED, _unpack(g_small, [P[n].shape for n in SMALL_SHARDED], lead=(N_CHIPS,))):
        a = jnp.moveaxis(a, 0, -2)
        full[n] = a.reshape(a.shape[:-2] + (N_CHIPS * a.shape[-1],))

    def small_weights(layer, group):
        j = layer // 2
        if layer % 2 == 0:
            cw = full["conv_a_w"][j]
            pw = group[2].reshape(N_CHIPS, 4, 64, POOL_GW).transpose(1, 0, 2, 3).reshape(4, POOL_GW, POOL_GW)
            return dict(norm=norm_even[j][None], conv_w=_pad_rows(cw, 32), conv_w_rev=_pad_rows(cw[::-1], 32),
                        conv_b=conv_a_b[j][None], ln_g=ln_a_g[j][None], ln_b=ln_a_b[j][None], pool_w=pw,
                        pool_b=full["pool_b"][j].reshape(1, D), pool_scale=pool_scale[j][None])
        return dict(norm=full["norm_odd"][j][None], conv_w=_pad_rows(full["conv_c_w"][j], 8),
                    conv_b=full["conv_c_b"][j][None], w_rg=w_rg[j].astype(BF16), b_rg=full["b_rg"][j][None],
                    w_ig=w_ig[j].astype(BF16), b_ig=full["b_ig"][j][None], lam=full["lru_lambda"][j][None])

    no_token = jnp.zeros((8, 128), F32)
    h = x[0]
    saved, big_w, small_w = [], [], []
    for layer in range(depth):
        token = no_token
        if layer + 1 < depth:
            nxt = cast_group(layer + 1)
            copies = _gather_copies([a.shape for a in nxt])
            ssem, rsem, nxt, token = _split_start(nxt, copies, 3 * len(nxt), "gather_start%d" % (layer + 1))
        small_w.append(small_weights(layer, group))
        big_w.append((group[0], group[1].reshape(1, -1, D)))
        h, sv = _layer_fwd(layer % 2 == 0, h, small_w[layer], *big_w[layer], token)
        saved.append(sv)
        if layer + 1 < depth:
            group = _forward_cores(_split_wait(ssem, rsem, nxt, copies, h, "gather_wait%d" % (layer + 1)))

    dh, dhb, d_final, loss = _loss_head(h, final_norm[None], loss_target[0])
    loss = lax.psum(loss[0, 0], ("x", "y", "c"))
    everywhere = [False, False, False, False, False, True, True]
    final = [None] * len(everywhere)
    small_of = [None] * depth

    def finish(pending, after):
        ssem, rsem, arrs, copies, slots, pj, pl_ = pending
        arrs = _split_wait(ssem, rsem, arrs, copies, after, "chips_wait%d" % pl_)
        for a, r, s in zip(arrs[:len(slots)], arrs[len(slots):], slots):
            final[s] = _sum_chips(a, r, pos, everywhere[s], pj, L, final[s])

    pending = None
    token = no_token
    for layer in reversed(range(depth)):
        j = layer // 2
        even_layer = layer % 2 == 0
        dp, dw_in, dw_out, sm = _layer_bwd_weights(even_layer, saved[layer], small_w[layer], big_w[layer][1], dhb, token)
        if even_layer:
            dpw = sm["pool_w"].reshape(4, N_CHIPS, 64, POOL_GW).transpose(1, 0, 2, 3)
            parts = [dw_in, dw_out.reshape(1, N_CHIPS, -1, D), dpw.reshape(1, N_CHIPS, 4 * 64, POOL_GW).astype(BF16)]
            slots = [0, 1, 2]
        else:
            parts = [dw_in, dw_out.reshape(1, N_CHIPS, -1, D),
                     sm["w_rg"].reshape(1, N_CHIPS, -1, LRU_HD).astype(BF16),
                     sm["w_ig"].reshape(1, N_CHIPS, -1, LRU_HD).astype(BF16)]
            slots = [3, 4, 5, 6]
        pair = [_add_cores(a, r, pos) for a, r in zip(parts, _exchange_halves(parts))]
        copies = _chips_copies(len(pair))
        land = [lax.empty(a.shape, a.dtype) for a in pair]
        ssem, rsem, arrs, token = _split_start(pair + land, copies, 3 * len(pair), "chips_start%d" % layer)
        dh, dhb, sm["norm"] = _layer_bwd_input(even_layer, saved[layer], small_w[layer], big_w[layer][0], dp, dh, token)
        small_of[layer] = sm
        if pending is not None:
            finish(pending, dh)
        pending = (ssem, rsem, arrs, copies, slots, j, layer)
    grad_x = dh
    small_g = []
    for jj in range(L):
        ge, go = small_of[2 * jj], small_of[2 * jj + 1]
        small_g += [ge["conv_w"].reshape(32, 8, D).sum(axis=1)[:CONV_K], ge["vec"][0:5], ge["norm"], go["vec"], go["norm"]]
    small_g.append(d_final)
    small_shapes = [a.shape for a in small_g]
    packed_small = _pack(small_g)
    finish(pending, packed_small)
    *gw, recv_small = _exchange_final(final, everywhere, packed_small)
    sg = _unpack(_sum_devices(recv_small), small_shapes)

    grads = dict(w_in_even=gw[0], w_out_even=gw[1], pool_w=gw[2].reshape(pool_w.shape), w_in_odd=gw[3], w_out_odd=gw[4],
                 w_rg=gw[5].reshape(w_rg.shape), w_ig=gw[6].reshape(w_ig.shape), final_norm=sg[-1][0])
    ev = [sg[5 * j + 1] for j in range(L)]
    ov = [sg[5 * j + 3] for j in range(L)]
    grads["conv_a_w"] = _shard(jnp.stack([sg[5 * j] for j in range(L)]), 2, k)
    grads["norm_even"] = jnp.stack([sg[5 * j + 2][0] for j in range(L)])
    grads["norm_odd"] = _shard(jnp.stack([sg[5 * j + 4][0] for j in range(L)]), 1, k)
    for r, n in enumerate(("conv_a_b", "ln_a_g", "ln_a_b", "pool_scale")):
        grads[n] = jnp.stack([e[r] for e in ev])
    grads["pool_b"] = _shard(jnp.stack([e[4].reshape(4, POOL_GW) for e in ev]), 2, k)
    grads["conv_c_w"] = _shard(jnp.stack([o[0:4] for o in ov]), 2, k)
    for r, n in zip((4, 5, 6, 7), ("conv_c_b", "b_rg", "b_ig", "lru_lambda")):
        grads[n] = _shard(jnp.stack([o[r] for o in ov]), 1, k)

    delta, new_m, new_v = {}, {}, {}
    for n in BIG:
        s3 = (L, -1, P[n].shape[-1])
        d, m2, v2 = _adamw(P[n].reshape(s3), grads[n].reshape(s3), P["m_" + n].reshape(s3), P["v_" + n].reshape(s3), "adamw")
        delta[n], new_m[n], new_v[n] = d.reshape(P[n].shape), m2.reshape(P[n].shape), v2.reshape(P[n].shape)
    shapes = [P[n].shape for n in SMALL]
    packed = [_pack([src[n] for n in SMALL])[None] for src in
              (P, grads, {n: P["m_" + n] for n in SMALL}, {n: P["v_" + n] for n in SMALL})]
    for res, out in zip(_adamw(*packed, "adamw_small"), (delta, new_m, new_v)):
        for n, a in zip(SMALL, _unpack(res[0], shapes)):
            out[n] = a

    return (loss, grad_x[None], *[grads[n] for n in WEIGHTS], *[delta[n] for n in WEIGHTS],
            *[new_m[n] for n in WEIGHTS], *[new_v[n] for n in WEIGHTS])
```

```python
import functools

import jax
import jax.numpy as jnp
from jax import lax
from jax.experimental import pallas as pl
from jax.experimental.pallas import tpu as pltpu

F32 = jnp.float32
BF16 = jnp.bfloat16
MESH = pl.DeviceIdType.MESH

D_MODEL = 1024
N_CHIPS = 4
N_DEV = 8
EPS_RMS = 1e-6
EPS_LN = 1e-5
CONV_K = 31
POOL_WINDOWS = (2, 4, 8, 16)
POOL_GW = 256
LRU_HEADS = 12
LRU_HD = 128
W_LRU = LRU_HEADS * LRU_HD
LRU_CONV_K = 4
LRU_C = 8.0
ADAM_LR = 0.001
ADAM_B1 = 0.9
ADAM_B2 = 0.999
ADAM_EPS = 1e-08
ADAM_WD = 0.01
ADAM_STEP = 10

VMEM_LIMIT_BYTES = 56 * 1024 * 1024
ROW_TILE = 512
MIX_TILE = 256
EVEN_HALO = 32
ODD_HALO = 8


def _pallas(body, **kw):
    return pl.pallas_call(body, **kw)


def _params(*sem):
    return pltpu.CompilerParams(dimension_semantics=sem if sem else None, vmem_limit_bytes=VMEM_LIMIT_BYTES)


def _sigmoid(x):
    return 0.5 * jnp.tanh(0.5 * x) + 0.5


def _dsilu(x, s):
    return s * (1.0 + x * (1.0 - s))


def _nt(a, b):
    return lax.dot_general(a, b, (((1,), (1,)), ((), ())), preferred_element_type=F32)


def _tn(a, b):
    return lax.dot_general(a, b, (((0,), (0,)), ((), ())), preferred_element_type=F32)


def _in_proj(h, g, wg, layer, after, name):
    T, D = h.shape
    _, nblk, _, nb = wg.shape

    nrow = T // ROW_TILE

    def body(h_ref, g_ref, w_ref, after_ref, p_ref, n_ref, n_all):
        j, i = pl.program_id(0), pl.program_id(1)

        @pl.when(j == 0)
        def _():
            x = h_ref[...]
            r = lax.rsqrt(jnp.mean(x * x, axis=-1, keepdims=True) + EPS_RMS)
            nn = (x * r * g_ref[...]).astype(BF16)
            n_ref[...] = nn
            n_all[i] = nn

        p_ref[...] = jnp.dot(n_all[i], w_ref[0], preferred_element_type=F32)

    def rows_once(j, i):
        return (jnp.where(j == 0, i, nrow - 1), 0)

    return _pallas(
        body, name=name, grid=(nblk, nrow),
        in_specs=[pl.BlockSpec((ROW_TILE, D), rows_once), pl.BlockSpec((1, D), lambda j, i: (0, 0)),
                  pl.BlockSpec((None, 1, D, nb), lambda j, i: (layer, j, 0, 0)),
                  pl.BlockSpec((8, 128), lambda j, i: (0, 0))],
        out_specs=[pl.BlockSpec((ROW_TILE, nb), lambda j, i: (i, j)), pl.BlockSpec((ROW_TILE, D), rows_once)],
        out_shape=[jax.ShapeDtypeStruct((T, nblk * nb), F32), jax.ShapeDtypeStruct((T, D), BF16)],
        scratch_shapes=[pltpu.VMEM((nrow, ROW_TILE, D), BF16)],
        compiler_params=_params("arbitrary", "arbitrary"))(h, g, wg, after)


def _out_proj(y, w, layer, hres, name):
    T, K = y.shape
    D = w.shape[2]

    def body(y_ref, w_ref, r_ref, o_ref):
        o_ref[...] = r_ref[...] + jnp.dot(y_ref[...], w_ref[...], preferred_element_type=F32)

    return _pallas(
        body, name=name, grid=(T // ROW_TILE,),
        in_specs=[pl.BlockSpec((ROW_TILE, K), lambda i: (i, 0)), pl.BlockSpec((None, K, D), lambda i: (layer, 0, 0)),
                  pl.BlockSpec((ROW_TILE, D), lambda i: (i, 0))],
        out_specs=pl.BlockSpec((ROW_TILE, D), lambda i: (i, 0)),
        out_shape=jax.ShapeDtypeStruct((T, D), F32),
        compiler_params=_params("parallel"))(y, w, hres)


def _dn_proj(dp, wg, layer, h, g, dres, after, name):
    T, D = h.shape
    _, nblk, _, nb = wg.shape

    nrow = T // ROW_TILE

    def body(dp_ref, w_ref, h_ref, g_ref, dres_ref, after_ref, dh_ref, dhb_ref, dg_ref, acc_ref):
        j, i = pl.program_id(0), pl.program_id(1)
        part = _nt(dp_ref[...], w_ref[0])

        @pl.when(j == 0)
        def _():
            acc_ref[i] = part

        @pl.when(j > 0)
        def _():
            acc_ref[i] += part

        @pl.when(j == nblk - 1)
        def _():
            x = h_ref[...]
            r = lax.rsqrt(jnp.mean(x * x, axis=-1, keepdims=True) + EPS_RMS)
            dn = acc_ref[i]
            q = dn * g_ref[...]
            dh = dres_ref[...] + r * q - x * ((r * r * r) * jnp.mean(q * x, axis=-1, keepdims=True))
            dh_ref[...] = dh
            dhb_ref[...] = dh.astype(BF16)
            dgp = jnp.sum(dn * (x * r), axis=0, keepdims=True)

            @pl.when(i == 0)
            def _():
                dg_ref[...] = dgp

            @pl.when(i > 0)
            def _():
                dg_ref[...] += dgp

    def rows_last(j, i):
        return (jnp.where(j == nblk - 1, i, 0), 0)

    return _pallas(
        body, name=name, grid=(nblk, nrow),
        in_specs=[pl.BlockSpec((ROW_TILE, nb), lambda j, i: (i, j)),
                  pl.BlockSpec((None, 1, D, nb), lambda j, i: (layer, j, 0, 0)),
                  pl.BlockSpec((ROW_TILE, D), rows_last), pl.BlockSpec((1, D), lambda j, i: (0, 0)),
                  pl.BlockSpec((ROW_TILE, D), rows_last), pl.BlockSpec((8, 128), lambda j, i: (0, 0))],
        out_specs=[pl.BlockSpec((ROW_TILE, D), rows_last), pl.BlockSpec((ROW_TILE, D), rows_last),
                   pl.BlockSpec((1, D), lambda j, i: (0, 0))],
        out_shape=[jax.ShapeDtypeStruct((T, D), F32), jax.ShapeDtypeStruct((T, D), BF16),
                   jax.ShapeDtypeStruct((1, D), F32)],
        scratch_shapes=[pltpu.VMEM((nrow, ROW_TILE, D), F32)],
        compiler_params=_params("arbitrary", "arbitrary"))(dp, wg, h, g, dres, after)


def _dw_in(n, dp, nblk, layer, nlayers, prev, name):
    T, D = n.shape
    nb = dp.shape[1] // nblk
    ta = D

    def body(n_ref, dp_ref, *rest):
        rest[-1][0] = _tn(n_ref[...], dp_ref[...]).astype(BF16)

    in_specs = [pl.BlockSpec((T, ta), lambda j, i: (0, i)), pl.BlockSpec((T, nb), lambda j, i: (0, j))]
    args = (n, dp) if prev is None else (n, dp, prev)
    return _pallas(
        body, name=name, grid=(nblk, D // ta), in_specs=in_specs + ([] if prev is None else [ANY]),
        out_specs=pl.BlockSpec((None, 1, ta, nb), lambda j, i: (layer, j, i, 0)),
        out_shape=jax.ShapeDtypeStruct((nlayers, nblk, D, nb), BF16),
        input_output_aliases={} if prev is None else {2: 0},
        compiler_params=_params("parallel", "parallel"))(*args)


def _dw_out(y, dout, layer, nlayers, prev, name):
    T, K = y.shape
    D = dout.shape[1]
    tk = 512

    def body(y_ref, d_ref, *rest):
        rest[-1][...] = _tn(y_ref[...], d_ref[...]).astype(BF16)

    in_specs = [pl.BlockSpec((T, tk), lambda i: (0, i)), pl.BlockSpec((T, D), lambda i: (0, 0))]
    args = (y, dout) if prev is None else (y, dout, prev)
    return _pallas(
        body, name=name, grid=(K // tk,), in_specs=in_specs + ([] if prev is None else [ANY]),
        out_specs=pl.BlockSpec((None, tk, D), lambda i: (layer, i, 0)),
        out_shape=jax.ShapeDtypeStruct((nlayers, K, D), BF16),
        input_output_aliases={} if prev is None else {2: 0},
        compiler_params=_params("parallel"))(*args)


def _loss_head(h, g, tgt):
    T, D = h.shape
    tm = MIX_TILE

    def body(h_ref, g_ref, t_ref, dh_ref, dhb_ref, dg_ref, loss_ref):
        i = pl.program_id(0)
        x = h_ref[...]
        gg = g_ref[...]
        r = lax.rsqrt(jnp.mean(x * x, axis=-1, keepdims=True) + EPS_RMS)
        xr = x * r
        e = xr * gg - t_ref[...]
        lp = 0.5 * jnp.sum(jnp.mean(e * e, axis=-1, keepdims=True), axis=0, keepdims=True)
        dn = e * (1.0 / D)
        q = dn * gg
        dh = r * q - x * ((r * r * r) * jnp.mean(q * x, axis=-1, keepdims=True))
        dh_ref[...] = dh
        dhb_ref[...] = dh.astype(BF16)
        dgp = jnp.sum(dn * xr, axis=0, keepdims=True)

        @pl.when(i == 0)
        def _():
            dg_ref[...] = dgp
            loss_ref[...] = lp

        @pl.when(i > 0)
        def _():
            dg_ref[...] += dgp
            loss_ref[...] += lp

    return _pallas(
        body, name="loss_head", grid=(T // tm,),
        in_specs=[pl.BlockSpec((tm, D), lambda i: (i, 0)), pl.BlockSpec((1, D), lambda i: (0, 0)),
                  pl.BlockSpec((tm, D), lambda i: (i, 0))],
        out_specs=[pl.BlockSpec((tm, D), lambda i: (i, 0)), pl.BlockSpec((tm, D), lambda i: (i, 0)),
                   pl.BlockSpec((1, D), lambda i: (0, 0)), pl.BlockSpec((1, 1), lambda i: (0, 0))],
        out_shape=[jax.ShapeDtypeStruct((T, D), F32), jax.ShapeDtypeStruct((T, D), BF16),
                   jax.ShapeDtypeStruct((1, D), F32), jax.ShapeDtypeStruct((1, 1), F32)],
        compiler_params=_params("arbitrary"))(h, g, tgt)


def _shift_up(x, j):
    return x if j == 0 else pltpu.roll(x, x.shape[0] - j, 0)


def _shift_down(x, j):
    return x if j == 0 else pltpu.roll(x, j, 0)


def _fill_shifted(dst_ref, src_ref):
    rows = dst_ref.shape[1]
    for s in range(8):
        dst_ref[s] = src_ref[pl.ds(s, rows), :]


def _fill_taps(wb_ref, w_ref):
    for k in range(w_ref.shape[0]):
        wb_ref[k] = jnp.broadcast_to(w_ref[k:k + 1, :], wb_ref.shape[1:])


def _tap_sum(sh_ref, wb_ref, r0, nrows, offsets):
    accs = [None] * (nrows // 8)
    for k, o in enumerate(offsets):
        wk = wb_ref[k]
        for u in range(nrows // 8):
            term = wk * sh_ref[o % 8, pl.ds(r0 + (o // 8) * 8 + 8 * u, 8), :]
            accs[u] = term if accs[u] is None else accs[u] + term
    return jnp.concatenate(accs, axis=0)


def _pool_sums(vx, up):
    sh = _shift_up if up else _shift_down
    outs = []
    for gi, w in enumerate(POOL_WINDOWS):
        s = vx[:, gi * POOL_GW:(gi + 1) * POOL_GW]
        j = 1
        while j < w:
            s = s + sh(s, j)
            j *= 2
        outs.append(s)
    return outs


def _inv_count(row0, nrows):
    pos = (row0 + 1 + lax.broadcasted_iota(jnp.int32, (nrows, 1), 0)).astype(F32)
    return [1.0 / jnp.minimum(pos, float(w)) for w in POOL_WINDOWS]


def _even_mixer_fwd(p, cw, cb, lg, lb, pw, pb, sc, name):
    T = p.shape[0]
    C = D_MODEL
    tT, HL = MIX_TILE, EVEN_HALO
    hb = tT // HL
    chunk = 32

    def body(pm_ref, ph_ref, cw_ref, cb_ref, lg_ref, lb_ref, pw_ref, pb_ref, sc_ref, y_ref, u1_ref, u0x_ref, sh_ref,
             wb_ref):
        i = pl.program_id(0)
        keep = (i > 0).astype(F32)

        @pl.when(i == 0)
        def _():
            _fill_taps(wb_ref, cw_ref)

        u0x_ref[0:HL] = ph_ref[:, 0:C] * _sigmoid(ph_ref[:, C:2 * C]) * keep
        u0x_ref[HL:HL + tT] = pm_ref[:, 0:C] * _sigmoid(pm_ref[:, C:2 * C])
        u0x_ref[HL + tT:HL + tT + 8] = jnp.zeros((8, C), F32)
        _fill_shifted(sh_ref, u0x_ref)
        offs = [HL - (CONV_K - 1) + k for k in range(CONV_K)]

        def conv_chunk(c, carry):
            r0 = pl.multiple_of(c * chunk, chunk)
            u1_ref[pl.ds(r0, chunk), :] = _tap_sum(sh_ref, wb_ref, r0, chunk, offs) + cb_ref[...]
            return carry

        lax.fori_loop(0, tT // chunk, conv_chunk, 0)
        u1 = u1_ref[...]
        mu = jnp.mean(u1, axis=-1, keepdims=True)
        xc = u1 - mu
        rs = lax.rsqrt(jnp.mean(xc * xc, axis=-1, keepdims=True) + EPS_LN)
        u2 = xc * rs * lg_ref[...] + lb_ref[...]
        u3 = u2 * _sigmoid(u2)
        ag = pm_ref[:, 2 * C:3 * C]
        y_ref[:, 0:C] = (u3 * (ag * _sigmoid(ag))).astype(BF16)
        vx = jnp.concatenate([ph_ref[:, 3 * C:4 * C] * keep, pm_ref[:, 3 * C:4 * C]], axis=0)
        sums = _pool_sums(vx, up=False)
        inv = _inv_count(i * tT, tT)
        for gi in range(len(POOL_WINDOWS)):
            cols = slice(gi * POOL_GW, (gi + 1) * POOL_GW)
            d0 = sums[gi][HL:] * inv[gi] - vx[HL:, cols]
            d1 = jnp.dot(d0.astype(BF16), pw_ref[gi], preferred_element_type=F32) + pb_ref[:, cols]
            bg = pm_ref[:, 4 * C + gi * POOL_GW:4 * C + (gi + 1) * POOL_GW]
            y_ref[:, C + gi * POOL_GW:C + (gi + 1) * POOL_GW] = (d1 * sc_ref[:, cols] * (bg * _sigmoid(bg))).astype(BF16)

    vec = pl.BlockSpec((1, C), lambda i: (0, 0))
    return _pallas(
        body, name=name, grid=(T // tT,),
        in_specs=[pl.BlockSpec((tT, 5 * C), lambda i: (i, 0)),
                  pl.BlockSpec((HL, 5 * C), lambda i: (jnp.maximum(i * hb - 1, 0), 0)),
                  pl.BlockSpec((32, C), lambda i: (0, 0)), vec, vec, vec,
                  pl.BlockSpec((4, POOL_GW, POOL_GW), lambda i: (0, 0, 0)), vec, vec],
        out_specs=[pl.BlockSpec((tT, 2 * C), lambda i: (i, 0)), pl.BlockSpec((tT, C), lambda i: (i, 0))],
        out_shape=[jax.ShapeDtypeStruct((T, 2 * C), BF16), jax.ShapeDtypeStruct((T, C), F32)],
        scratch_shapes=[pltpu.VMEM((HL + tT + 8, C), F32), pltpu.VMEM((8, HL + tT, C), F32),
                        pltpu.VMEM((32, 8, C), F32)],
        compiler_params=_params("arbitrary"))(p, p, cw, cb, lg, lb, pw, pb, sc)


def _even_mixer_bwd(p, u1, dout, w_out, after, cwr, lg, lb, pw, pb, sc, name):
    T = p.shape[0]
    C = D_MODEL
    tT, HL = MIX_TILE, EVEN_HALO
    hb = tT // HL
    nT = T // tT
    R1 = tT + HL
    chunk = 32

    def body(pm_ref, pp_ref, pn_ref, u1m_ref, u1n_ref, dom_ref, don_ref, wo_ref, after_ref, cwr_ref, lg_ref, lb_ref,
             pw_ref, pb_ref, sc_ref, dp_ref, dcw_ref, dvec_ref, dpw_ref, x_ref, sh_ref, du0_ref, wb_ref):
        i = pl.program_id(0)
        dy = _nt(jnp.concatenate([dom_ref[...], don_ref[...]], axis=0), wo_ref[...])

        @pl.when(i == 0)
        def _():
            _fill_taps(wb_ref, cwr_ref)

        keep_prev = (i > 0).astype(F32)
        keep_next = (i < nT - 1).astype(F32)
        row = lax.broadcasted_iota(jnp.int32, (R1, 1), 0)
        live = jnp.where(row < tT, 1.0, keep_next)

        def cat(m, n):
            return jnp.concatenate([m, n], axis=0)

        u1 = cat(u1m_ref[...], u1n_ref[...])
        mu = jnp.mean(u1, axis=-1, keepdims=True)
        xc = u1 - mu
        rs = lax.rsqrt(jnp.mean(xc * xc, axis=-1, keepdims=True) + EPS_LN)
        xh = xc * rs
        u2 = xh * lg_ref[...] + lb_ref[...]
        s2 = _sigmoid(u2)
        u3 = u2 * s2
        ag = cat(pm_ref[:, 2 * C:3 * C], pn_ref[:, 2 * C:3 * C])
        sa = _sigmoid(ag)
        dya = dy[:, 0:C]
        dp_ref[:, 2 * C:3 * C] = (dya * u3 * _dsilu(ag, sa))[0:tT].astype(BF16)
        du2 = dya * (ag * sa) * _dsilu(u2, s2)
        dlg = jnp.sum((du2 * xh)[0:tT], axis=0, keepdims=True)
        dlb = jnp.sum(du2[0:tT], axis=0, keepdims=True)
        dxh = du2 * lg_ref[...]
        du1 = rs * (dxh - jnp.mean(dxh, axis=-1, keepdims=True) - xh * jnp.mean(dxh * xh, axis=-1, keepdims=True))
        du1 = du1 * live
        dcb = jnp.sum(du1[0:tT], axis=0, keepdims=True)
        x_ref[0:R1] = du1
        x_ref[R1:R1 + 8] = jnp.zeros((8, C), F32)
        _fill_shifted(sh_ref, x_ref)

        def du0_chunk(c, carry):
            r0 = pl.multiple_of(c * chunk, chunk)
            du0_ref[pl.ds(r0, chunk), :] = _tap_sum(sh_ref, wb_ref, r0, chunk, list(range(CONV_K)))
            return carry

        lax.fori_loop(0, tT // chunk, du0_chunk, 0)
        av, agl = pm_ref[:, 0:C], pm_ref[:, C:2 * C]
        sg = _sigmoid(agl)
        du0 = du0_ref[...]
        dp_ref[:, 0:C] = (du0 * sg).astype(BF16)
        dp_ref[:, C:2 * C] = (du0 * av * sg * (1.0 - sg)).astype(BF16)
        du0_ref[...] = du1[0:tT]
        x_ref[0:HL] = pp_ref[:, 0:C] * _sigmoid(pp_ref[:, C:2 * C]) * keep_prev
        x_ref[HL:HL + tT] = av * sg
        x_ref[HL + tT:HL + tT + 8] = jnp.zeros((8, C), F32)
        _fill_shifted(sh_ref, x_ref)

        @pl.when(i == 0)
        def _():
            dcw_ref[...] = jnp.zeros_like(dcw_ref)

        for k in range(CONV_K):
            o = HL - (CONV_K - 1) + k

            def dw_chunk(c, acc, o=o):
                r0 = pl.multiple_of(c * 64, 64)
                for u in range(0, 64, 8):
                    acc = acc + du0_ref[pl.ds(r0 + u, 8), :] * sh_ref[o % 8, pl.ds(r0 + u + (o // 8) * 8, 8), :]
                return acc

            dcw_ref[8 * k:8 * k + 8, :] += lax.fori_loop(0, tT // 64, dw_chunk, jnp.zeros((8, C), F32))

        bg = cat(pm_ref[:, 4 * C:5 * C], pn_ref[:, 4 * C:5 * C])
        sb = _sigmoid(bg)
        dyb = dy[:, C:2 * C]
        dyb0 = dyb * (bg * sb)
        dd1 = dyb0 * sc_ref[...]
        dpb = jnp.sum(dd1[0:tT], axis=0, keepdims=True)
        inv1 = _inv_count(i * tT, R1)
        z_parts, dd0_parts = [], []
        for gi in range(len(POOL_WINDOWS)):
            cols = slice(gi * POOL_GW, (gi + 1) * POOL_GW)
            dd0 = _nt(dd1[:, cols].astype(BF16), pw_ref[gi])
            dd0_parts.append(dd0)
            z_parts.append(dd0 * inv1[gi] * live)
        fsum = _pool_sums(jnp.concatenate(z_parts, axis=1), up=True)
        vx = cat(pp_ref[:, 3 * C:4 * C] * keep_prev, pm_ref[:, 3 * C:4 * C])
        sums = _pool_sums(vx, up=False)
        inv0 = _inv_count(i * tT, tT)
        dsc_parts = []
        for gi in range(len(POOL_WINDOWS)):
            cols = slice(gi * POOL_GW, (gi + 1) * POOL_GW)
            dp_ref[:, 3 * C + gi * POOL_GW:3 * C + (gi + 1) * POOL_GW] = (fsum[gi][0:tT] - dd0_parts[gi][0:tT]).astype(BF16)
            d0 = (sums[gi][HL:] * inv0[gi] - vx[HL:, cols]).astype(BF16)
            d1 = jnp.dot(d0, pw_ref[gi], preferred_element_type=F32) + pb_ref[:, cols]
            bgm, sbm = bg[0:tT, cols], sb[0:tT, cols]
            dp_ref[:, 4 * C + gi * POOL_GW:4 * C + (gi + 1) * POOL_GW] = (
                dyb[0:tT, cols] * d1 * sc_ref[:, cols] * _dsilu(bgm, sbm)).astype(BF16)
            dsc_parts.append(jnp.sum(dyb0[0:tT, cols] * d1, axis=0, keepdims=True))
            dpw_g = _tn(d0, dd1[0:tT, cols].astype(BF16))

            @pl.when(i == 0)
            def _(gi=gi, dpw_g=dpw_g):
                dpw_ref[gi] = dpw_g

            @pl.when(i > 0)
            def _(gi=gi, dpw_g=dpw_g):
                dpw_ref[gi] += dpw_g

        dsc = jnp.concatenate(dsc_parts, axis=1)
        vecs = jnp.concatenate([dcb, dlg, dlb, dsc, dpb, jnp.zeros((3, C), F32)], axis=0)

        @pl.when(i == 0)
        def _():
            dvec_ref[...] = vecs

        @pl.when(i > 0)
        def _():
            dvec_ref[...] += vecs

    vec = pl.BlockSpec((1, C), lambda i: (0, 0))
    taps = pl.BlockSpec((32, C), lambda i: (0, 0))

    def prev_blk(i):
        return (jnp.maximum(i * hb - 1, 0), 0)

    def next_blk(i):
        return (jnp.minimum((i + 1) * hb, T // HL - 1), 0)

    return _pallas(
        body, name=name, grid=(nT,),
        in_specs=[pl.BlockSpec((tT, 5 * C), lambda i: (i, 0)), pl.BlockSpec((HL, 5 * C), prev_blk),
                  pl.BlockSpec((HL, 5 * C), next_blk),
                  pl.BlockSpec((tT, C), lambda i: (i, 0)), pl.BlockSpec((HL, C), next_blk),
                  pl.BlockSpec((tT, C), lambda i: (i, 0)), pl.BlockSpec((HL, C), next_blk),
                  pl.BlockSpec((None, 2 * C, C), lambda i: (0, 0, 0)), pl.BlockSpec((8, 128), lambda i: (0, 0)),
                  taps, vec, vec, pl.BlockSpec((4, POOL_GW, POOL_GW), lambda i: (0, 0, 0)), vec, vec],
        out_specs=[pl.BlockSpec((tT, 5 * C), lambda i: (i, 0)), pl.BlockSpec((32 * 8, C), lambda i: (0, 0)),
                   pl.BlockSpec((8, C), lambda i: (0, 0)), pl.BlockSpec((4, POOL_GW, POOL_GW), lambda i: (0, 0, 0))],
        out_shape=[jax.ShapeDtypeStruct((T, 5 * C), BF16), jax.ShapeDtypeStruct((32 * 8, C), F32),
                   jax.ShapeDtypeStruct((8, C), F32), jax.ShapeDtypeStruct((4, POOL_GW, POOL_GW), F32)],
        scratch_shapes=[pltpu.VMEM((R1 + 8, C), F32), pltpu.VMEM((8, R1, C), F32), pltpu.VMEM((tT, C), F32),
                        pltpu.VMEM((32, 8, C), F32)],
        compiler_params=_params("arbitrary"))(p, p, p, u1, u1, dout, dout, w_out, after, cwr, lg, lb, pw, pb, sc)


def _softplus(z):
    u = jnp.exp(-jnp.abs(z))
    w = 1.0 + u
    l1p = jnp.where(w == 1.0, u, u * jnp.log(w) / jnp.where(w == 1.0, 1.0, w - 1.0))
    return jnp.maximum(z, 0.0) + l1p


def _lru_gates(xrx, cw_ref, cb_ref, wr_ref, br_ref, wi_ref, bi_ref, lam_ref):
    HL = ODD_HALO
    xc = cb_ref[...] + cw_ref[LRU_CONV_K - 1:LRU_CONV_K, :] * xrx[HL:]
    for k in range(LRU_CONV_K - 1):
        xc = xc + cw_ref[k:k + 1, :] * _shift_down(xrx, LRU_CONV_K - 1 - k)[HL:]
    xcb = xc.astype(BF16)
    rp, ip = [], []
    for hd in range(LRU_HEADS):
        cols = slice(hd * LRU_HD, (hd + 1) * LRU_HD)
        rp.append(jnp.dot(xcb[:, cols], wr_ref[hd], preferred_element_type=F32))
        ip.append(jnp.dot(xcb[:, cols], wi_ref[hd], preferred_element_type=F32))
    r = _sigmoid(jnp.concatenate(rp, axis=1) + br_ref[...])
    ig = _sigmoid(jnp.concatenate(ip, axis=1) + bi_ref[...])
    sp = _softplus(-lam_ref[...])
    log_a = (-LRU_C) * r * sp
    a = jnp.exp(log_a)
    m2 = jnp.maximum(-jnp.tanh(log_a) * (a * a + 1.0), 1e-30)
    inv_mult = lax.rsqrt(m2)
    return xc, xcb, r, ig, sp, a, m2 * inv_mult, inv_mult


def _group_scan(a, b, reverse):
    n, w = a.shape
    a, b = a.reshape(n // 8, 8, w), b.reshape(n // 8, 8, w)
    pos = lax.broadcasted_iota(jnp.int32, (1, 8, 1), 1)
    s = 1
    while s < 8:
        ok = (pos < 8 - s) if reverse else (pos >= s)
        shift = (8 - s) if reverse else s
        a_sh = jnp.where(ok, pltpu.roll(a, shift, 1), 1.0)
        b_sh = jnp.where(ok, pltpu.roll(b, shift, 1), 0.0)
        b = a * b_sh + b
        a = a * a_sh
        s *= 2
    return a.reshape(n, w), b.reshape(n, w)


def _apply_carries(a_ref, b_ref, out_ref, c0, reverse):
    ng = a_ref.shape[0] // 8

    def step(t, c):
        r0 = pl.multiple_of(((ng - 1 - t) if reverse else t) * 8, 8)
        x = a_ref[pl.ds(r0, 8), :] * c + b_ref[pl.ds(r0, 8), :]
        out_ref[pl.ds(r0, 8), :] = x
        return x[0:1, :] if reverse else x[7:8, :]

    return lax.fori_loop(0, ng, step, c0)


def _odd_mixer_fwd(p, cw, cb, wr, br, wi, bi, lam, name):
    T = p.shape[0]
    W = W_LRU
    tT, HL = MIX_TILE, ODD_HALO
    hb = tT // HL

    def body(pm_ref, ph_ref, cw_ref, cb_ref, wr_ref, br_ref, wi_ref, bi_ref, lam_ref, y_ref, hs_ref, carry_ref,
             sa_ref, sb_ref):
        i = pl.program_id(0)
        keep = (i > 0).astype(F32)

        @pl.when(i == 0)
        def _():
            carry_ref[...] = jnp.zeros_like(carry_ref)

        xrx = jnp.concatenate([ph_ref[:, 0:W] * keep, pm_ref[:, 0:W]], axis=0)
        xc, _, _, ig, _, a, mult, _ = _lru_gates(xrx, cw_ref, cb_ref, wr_ref, br_ref, wi_ref, bi_ref, lam_ref)
        sa_ref[...], sb_ref[...] = _group_scan(a, mult * (ig * xc), reverse=False)
        last = _apply_carries(sa_ref, sb_ref, hs_ref, carry_ref[0:1, :], reverse=False)
        carry_ref[...] = jnp.broadcast_to(last, (8, W))
        hs = hs_ref[...]
        gt = pm_ref[:, W:2 * W]
        y_ref[...] = (hs * (gt * _sigmoid(gt))).astype(BF16)

    vec = pl.BlockSpec((1, W), lambda i: (0, 0))
    heads = pl.BlockSpec((LRU_HEADS, LRU_HD, LRU_HD), lambda i: (0, 0, 0))
    return _pallas(
        body, name=name, grid=(T // tT,),
        in_specs=[pl.BlockSpec((tT, 2 * W), lambda i: (i, 0)),
                  pl.BlockSpec((HL, 2 * W), lambda i: (jnp.maximum(i * hb - 1, 0), 0)),
                  pl.BlockSpec((8, W), lambda i: (0, 0)), vec, heads, vec, heads, vec, vec],
        out_specs=[pl.BlockSpec((tT, W), lambda i: (i, 0)), pl.BlockSpec((tT, W), lambda i: (i, 0))],
        out_shape=[jax.ShapeDtypeStruct((T, W), BF16), jax.ShapeDtypeStruct((T, W), F32)],
        scratch_shapes=[pltpu.VMEM((8, W), F32), pltpu.VMEM((tT, W), F32), pltpu.VMEM((tT, W), F32)],
        compiler_params=_params("arbitrary"))(p, p, cw, cb, wr, br, wi, bi, lam)


def _odd_mixer_bwd(p, hs, dout, w_out, after, cw, cb, wr, br, wi, bi, lam, name):
    T = p.shape[0]
    W = W_LRU
    D = dout.shape[1]
    tT, HL = MIX_TILE, ODD_HALO
    hb = tT // HL
    nT = T // tT

    def body(pm_ref, ph_ref, hsm_ref, hsh_ref, do_ref, wo_ref, after_ref, cw_ref, cb_ref, wr_ref, br_ref, wi_ref,
             bi_ref, lam_ref, dp_ref, dwr_ref, dwi_ref, dvec_ref, gcarry_ref, xcarry_ref, sa_ref, sb_ref, g_ref):
        i = pl.program_id(0)
        keep = (i < nT - 1).astype(F32)

        @pl.when(i == 0)
        def _():
            gcarry_ref[...] = jnp.zeros_like(gcarry_ref)
            xcarry_ref[...] = jnp.zeros_like(xcarry_ref)

        xrx = jnp.concatenate([ph_ref[:, 0:W] * keep, pm_ref[:, 0:W]], axis=0)
        xc, xcb, r, ig, sp, a, mult, inv_mult = _lru_gates(xrx, cw_ref, cb_ref, wr_ref, br_ref, wi_ref, bi_ref, lam_ref)
        hs = hsm_ref[...]
        gt = pm_ref[:, W:2 * W]
        sg = _sigmoid(gt)
        dyv = _nt(do_ref[...], wo_ref[...])
        dp_ref[:, W:2 * W] = (dyv * hs * _dsilu(gt, sg)).astype(BF16)
        row = lax.broadcasted_iota(jnp.int32, (tT, 1), 0)
        m = jnp.where(row == tT - 1, 1.0, _shift_up(a, 1))
        sa_ref[...], sb_ref[...] = _group_scan(m, dyv * (gt * sg), reverse=True)
        first = _apply_carries(sa_ref, sb_ref, g_ref, gcarry_ref[0:1, :], reverse=True)
        G = g_ref[...]
        gcarry_ref[...] = jnp.broadcast_to(a[0:1, :] * first, (8, W))
        hs_prev = jnp.where(row == 0, hsh_ref[HL - 1:HL, :] * keep, _shift_down(hs, 1))
        da = G * hs_prev
        dmult = G * (ig * xc)
        di = G * mult * xc
        dxc = G * mult * ig
        dlog_a = da * a - dmult * (a * a) * inv_mult
        drp = dlog_a * ((-LRU_C) * sp) * r * (1.0 - r)
        dip = di * ig * (1.0 - ig)
        dlam = jnp.sum(dlog_a * ((-LRU_C) * r), axis=0, keepdims=True) * (-_sigmoid(-lam_ref[...]))
        drb, dib = drp.astype(BF16), dip.astype(BF16)
        back = []
        for hd in range(LRU_HEADS):
            cols = slice(hd * LRU_HD, (hd + 1) * LRU_HD)
            back.append(_nt(drb[:, cols], wr_ref[hd]) + _nt(dib[:, cols], wi_ref[hd]))
            dwr_h = _tn(xcb[:, cols], drb[:, cols])
            dwi_h = _tn(xcb[:, cols], dib[:, cols])

            @pl.when(i == 0)
            def _(hd=hd, dwr_h=dwr_h, dwi_h=dwi_h):
                dwr_ref[hd] = dwr_h
                dwi_ref[hd] = dwi_h

            @pl.when(i > 0)
            def _(hd=hd, dwr_h=dwr_h, dwi_h=dwi_h):
                dwr_ref[hd] += dwr_h
                dwi_ref[hd] += dwi_h

        dxc = dxc + jnp.concatenate(back, axis=1)
        dxcx = jnp.concatenate([dxc, xcarry_ref[...]], axis=0)
        dxr = cw_ref[LRU_CONV_K - 1:LRU_CONV_K, :] * dxc
        rows = []
        for k in range(LRU_CONV_K - 1):
            j = LRU_CONV_K - 1 - k
            dxr = dxr + cw_ref[k:k + 1, :] * _shift_up(dxcx, j)[0:tT]
            rows.append(jnp.sum(dxc * _shift_down(xrx, j)[HL:], axis=0, keepdims=True))
        rows.append(jnp.sum(dxc * xrx[HL:], axis=0, keepdims=True))
        dp_ref[:, 0:W] = dxr.astype(BF16)
        xcarry_ref[...] = dxc[0:8]
        rows += [jnp.sum(dxc, axis=0, keepdims=True), jnp.sum(drp, axis=0, keepdims=True),
                 jnp.sum(dip, axis=0, keepdims=True), dlam]
        vecs = jnp.concatenate(rows, axis=0)

        @pl.when(i == 0)
        def _():
            dvec_ref[...] = vecs

        @pl.when(i > 0)
        def _():
            dvec_ref[...] += vecs

    vec = pl.BlockSpec((1, W), lambda i: (0, 0))
    heads = pl.BlockSpec((LRU_HEADS, LRU_HD, LRU_HD), lambda i: (0, 0, 0))

    def tile(i):
        return (nT - 1 - i, 0)

    def prev_blk(i):
        return (jnp.maximum((nT - 1 - i) * hb - 1, 0), 0)

    return _pallas(
        body, name=name, grid=(nT,),
        in_specs=[pl.BlockSpec((tT, 2 * W), tile), pl.BlockSpec((HL, 2 * W), prev_blk),
                  pl.BlockSpec((tT, W), tile), pl.BlockSpec((HL, W), prev_blk), pl.BlockSpec((tT, D), tile),
                  pl.BlockSpec((None, W, D), lambda i: (0, 0, 0)), pl.BlockSpec((8, 128), lambda i: (0, 0)),
                  pl.BlockSpec((8, W), lambda i: (0, 0)), vec, heads, vec, heads, vec, vec],
        out_specs=[pl.BlockSpec((tT, 2 * W), tile), heads, heads, pl.BlockSpec((8, W), lambda i: (0, 0))],
        out_shape=[jax.ShapeDtypeStruct((T, 2 * W), BF16), jax.ShapeDtypeStruct((LRU_HEADS, LRU_HD, LRU_HD), F32),
                   jax.ShapeDtypeStruct((LRU_HEADS, LRU_HD, LRU_HD), F32), jax.ShapeDtypeStruct((8, W), F32)],
        scratch_shapes=[pltpu.VMEM((8, W), F32), pltpu.VMEM((8, W), F32), pltpu.VMEM((tT, W), F32),
                        pltpu.VMEM((tT, W), F32), pltpu.VMEM((tT, W), F32)],
        compiler_params=_params("arbitrary"))(p, p, hs, hs, dout, w_out, after, cw, cb, wr, br, wi, bi, lam)


def _pad_rows(a, rows):
    return jnp.concatenate([a, jnp.zeros((rows - a.shape[0], a.shape[1]), a.dtype)], axis=0)


def _layer_fwd(even, h, w, w_in, w_out, after):
    if even:
        p, n = _in_proj(h, w["norm"], w_in, 0, after, "in_proj_even")
        y, aux = _even_mixer_fwd(p, w["conv_w"], w["conv_b"], w["ln_g"], w["ln_b"], w["pool_w"], w["pool_b"],
                                 w["pool_scale"], "even_mixer_fwd")
    else:
        p, n = _in_proj(h, w["norm"], w_in, 0, after, "in_proj_odd")
        y, aux = _odd_mixer_fwd(p, w["conv_w"], w["conv_b"], w["w_rg"], w["b_rg"], w["w_ig"], w["b_ig"], w["lam"],
                                "odd_mixer_fwd")
    if callable(w_out):
        w_out = w_out(y)
    h_next = _out_proj(y, w_out, 0, h, "out_proj_even" if even else "out_proj_odd")
    return h_next, (h, n, p, aux, y), w_out


def _layer_bwd_weights(even, saved, w, w_out, dhb, after):
    h, n, p, aux, y = saved
    if even:
        dp, dcw, dvec, dpw = _even_mixer_bwd(p, aux, dhb, w_out, after, w["conv_w_rev"], w["ln_g"], w["ln_b"],
                                             w["pool_w"], w["pool_b"], w["pool_scale"], "even_mixer_bwd")
        dw_out = _dw_out(y, dhb, 0, 1, None, "dw_out_even")
        dw_in = _dw_in(n, dp, N_CHIPS, 0, 1, None, "dw_in_even")
        return dp, dw_in, dw_out, dict(conv_w=dcw, vec=dvec, pool_w=dpw)
    dp, dwr, dwi, dvec = _odd_mixer_bwd(p, aux, dhb, w_out, after, w["conv_w"], w["conv_b"], w["w_rg"], w["b_rg"],
                                        w["w_ig"], w["b_ig"], w["lam"], "odd_mixer_bwd")
    dw_out = _dw_out(y, dhb, 0, 1, None, "dw_out_odd")
    dw_in = _dw_in(n, dp, N_CHIPS, 0, 1, None, "dw_in_odd")
    return dp, dw_in, dw_out, dict(w_rg=dwr, w_ig=dwi, vec=dvec)


def _layer_bwd_input(even, saved, w, w_in, dp, dh, after):
    return _dn_proj(dp, w_in, 0, saved[0], w["norm"], dh, after, "dn_proj_even" if even else "dn_proj_odd")


ANY = pl.BlockSpec(memory_space=pl.ANY)


def _mesh_pos():
    return lax.axis_index("x"), lax.axis_index("y"), lax.axis_index("c")


def _other_chips(x, y):
    return [(1 - x, y), (x, 1 - y), (1 - x, 1 - y)]


def _other_devices(x, y, c):
    out = []
    for p in range(1, N_DEV):
        out.append((1 - x if p & 4 else x, 1 - y if p & 2 else y, 1 - c if p & 1 else c))
    return out


def _remote(src, dst, ssem, rsem, dev):
    return pltpu.make_async_remote_copy(src_ref=src, dst_ref=dst, send_sem=ssem, recv_sem=rsem, device_id=dev,
                                        device_id_type=MESH)


def _comm_call(body, name, ins, out_shape, scratch, aliases=None):
    return _pallas(body, name=name, in_specs=[ANY] * len(ins), out_specs=[ANY] * len(out_shape), out_shape=out_shape,
                   scratch_shapes=scratch, input_output_aliases=aliases or {},
                   compiler_params=pltpu.CompilerParams(has_side_effects=True))(*ins)


def _cast_shard(w, layer, pos):
    _, R, C = w.shape
    tr = _row_tile(R, C)

    def body(pos_ref, w_ref, o_ref):
        o_ref[...] = w_ref[...].astype(BF16)

    grid_spec = pltpu.PrefetchScalarGridSpec(
        num_scalar_prefetch=1, grid=(R // tr,),
        in_specs=[pl.BlockSpec((None, tr, C), lambda i, pr: (layer, i, 0))],
        out_specs=pl.BlockSpec((None, None, tr, C), lambda i, pr: (0, pr[0], i, 0)))
    return _pallas(body, name="cast_shard", grid_spec=grid_spec,
                   out_shape=jax.ShapeDtypeStruct((1, N_CHIPS, R, C), BF16),
                   compiler_params=_params("parallel"))(pos, w)


def _gather_weights(big, small):
    nA = len(big)
    half = [a.shape[2] // 2 for a in big]

    def body(*refs):
        ins, outs = refs[:nA + 1], refs[nA + 1:2 * nA + 2]
        ssem, rsem, fsem, frsem, lsem = refs[2 * nA + 2:]
        x, y, c = _mesh_pos()
        k = 2 * x + y
        chips = _other_chips(x, y)
        sib = (x, y, 1 - c)

        def slab(a, chip, core):
            return outs[a].at[:, chip, pl.ds(core * half[a], half[a]), :]

        local = [pltpu.make_async_copy(ins[nA], outs[nA].at[k], lsem.at[0])]
        for cp in local:
            cp.start()
        sends = []
        for j, (ox, oy) in enumerate(chips):
            for a in range(nA):
                sends.append(_remote(slab(a, k, c), slab(a, k, c), ssem.at[a, j], rsem.at[a, j], (ox, oy, c)))
            sends.append(_remote(ins[nA], outs[nA].at[k], ssem.at[nA, j], rsem.at[nA, j], (ox, oy, c)))
        for cp in sends:
            cp.start()
        for j, (ox, oy) in enumerate(chips):
            kj = 2 * ox + oy
            for a in range(nA):
                got = slab(a, kj, c)
                _remote(got, got, ssem.at[a, j], rsem.at[a, j], (ox, oy, c)).wait_recv()
                fw = _remote(got, got, fsem.at[a, j], frsem.at[a, j], sib)
                fw.start()
                sends.append(fw)
            gs = outs[nA].at[kj]
            _remote(gs, gs, ssem.at[nA, j], rsem.at[nA, j], (ox, oy, c)).wait_recv()
        for j, (ox, oy) in enumerate(chips):
            kj = 2 * ox + oy
            for a in range(nA):
                theirs = slab(a, kj, 1 - c)
                _remote(theirs, theirs, fsem.at[a, j], frsem.at[a, j], sib).wait_recv()
        for cp in sends:
            cp.wait_send()
        for cp in local:
            cp.wait()

    out_shape = [jax.ShapeDtypeStruct(a.shape, a.dtype) for a in big]
    out_shape.append(jax.ShapeDtypeStruct((N_CHIPS,) + small.shape, small.dtype))
    scratch = [pltpu.SemaphoreType.DMA((nA + 1, 3)), pltpu.SemaphoreType.DMA((nA + 1, 3)),
               pltpu.SemaphoreType.DMA((nA, 3)), pltpu.SemaphoreType.DMA((nA, 3)), pltpu.SemaphoreType.DMA((1,))]
    return _comm_call(body, "gather_weights", list(big) + [small], out_shape, scratch, {a: a for a in range(nA)})


HBM = pl.BlockSpec(memory_space=pltpu.HBM)
SEM = pl.BlockSpec(memory_space=pltpu.SEMAPHORE)
EFFECT = pltpu.SideEffectType.DATAFLOW_SIDE_EFFECTING


def _split_start(arrays, copies, n, name):
    k = len(arrays)

    def body(*refs):
        for cp in copies(refs[k + 2:2 * k + 2], refs[k], refs[k + 1]):
            cp.start()
        refs[2 * k + 2][...] = jnp.zeros((8, 128), F32)

    out = _pallas(
        body, name=name,
        out_shape=(pltpu.SemaphoreType.DMA((n,)), pltpu.SemaphoreType.DMA((n,)),
                   *[pltpu.HBM(a.shape, a.dtype) for a in arrays], jax.ShapeDtypeStruct((8, 128), F32)),
        in_specs=(HBM,) * k, out_specs=(SEM, SEM) + (HBM,) * k + (pl.BlockSpec(memory_space=pltpu.VMEM),),
        input_output_aliases={i: i + 2 for i in range(k)},
        compiler_params=pltpu.CompilerParams(has_side_effects=EFFECT),
    )(*[pltpu.with_memory_space_constraint(a, pltpu.HBM) for a in arrays])
    return out[0], out[1], list(out[2:2 + k]), out[2 + k]


def _split_wait(ssem, rsem, arrays, copies, after, name):
    k = len(arrays)

    def body(*refs):
        for cp in copies(refs[:k], refs[k], refs[k + 1]):
            cp.wait_send()
            cp.wait_recv()

    out = _pallas(
        body, name=name, out_shape=tuple(pltpu.HBM(a.shape, a.dtype) for a in arrays),
        in_specs=(HBM,) * k + (SEM, SEM, ANY), out_specs=(HBM,) * k, input_output_aliases={i: i for i in range(k)},
        compiler_params=pltpu.CompilerParams(has_side_effects=EFFECT),
    )(*arrays, ssem, rsem, after)
    return list(out)


def _gather_copies(shapes):
    half = [s[2] // 2 for s in shapes]

    def copies(refs, ssem, rsem):
        x, y, c = _mesh_pos()
        out = []
        for j, (ox, oy) in enumerate(_other_chips(x, y)):
            for a, ref in enumerate(refs):
                slab = ref.at[:, 2 * x + y, pl.ds(c * half[a], half[a]), :]
                out.append(_remote(slab, slab, ssem.at[3 * a + j], rsem.at[3 * a + j], (ox, oy, c)))
        return out

    return copies


def _chips_copies(n_arr):
    def copies(refs, ssem, rsem):
        x, y, c = _mesh_pos()
        out = []
        for j, (ox, oy) in enumerate(_other_chips(x, y)):
            for a in range(n_arr):
                out.append(_remote(refs[a].at[:, 2 * ox + oy], refs[n_arr + a].at[:, 2 * x + y], ssem.at[3 * a + j],
                                   rsem.at[3 * a + j], (ox, oy, c)))
        return out

    return copies


def _forward_cores(arrays):
    nA = len(arrays)
    half = [a.shape[2] // 2 for a in arrays]

    def body(*refs):
        outs = refs[nA:2 * nA]
        ssem, rsem = refs[2 * nA:]
        x, y, c = _mesh_pos()
        sib = (x, y, 1 - c)
        sends, waits = [], []
        for j, (ox, oy) in enumerate(_other_chips(x, y)):
            for a in range(nA):
                got = outs[a].at[:, 2 * ox + oy, pl.ds(c * half[a], half[a]), :]
                sends.append(_remote(got, got, ssem.at[a, j], rsem.at[a, j], sib))
                theirs = outs[a].at[:, 2 * ox + oy, pl.ds((1 - c) * half[a], half[a]), :]
                waits.append(_remote(theirs, theirs, ssem.at[a, j], rsem.at[a, j], sib))
        for cp in sends:
            cp.start()
        for cp in waits:
            cp.wait_recv()
        for cp in sends:
            cp.wait_send()

    out_shape = [jax.ShapeDtypeStruct(a.shape, a.dtype) for a in arrays]
    scratch = [pltpu.SemaphoreType.DMA((nA, 3)), pltpu.SemaphoreType.DMA((nA, 3))]
    return _comm_call(body, "forward_cores", list(arrays), out_shape, scratch, {a: a for a in range(nA)})


def _exchange_halves(big):
    nA = len(big)
    half = [a.shape[2] // 2 for a in big]

    def body(*refs):
        ins, outs = refs[:nA], refs[nA:2 * nA]
        ssem, rsem = refs[2 * nA:]
        x, y, c = _mesh_pos()
        sib = (x, y, 1 - c)
        sends = [_remote(ins[a].at[:, :, pl.ds((1 - c) * half[a], half[a]), :], outs[a], ssem.at[a], rsem.at[a], sib)
                 for a in range(nA)]
        for cp in sends:
            cp.start()
        for a in range(nA):
            _remote(outs[a], outs[a], ssem.at[a], rsem.at[a], sib).wait_recv()
        for cp in sends:
            cp.wait_send()

    out_shape = [jax.ShapeDtypeStruct((a.shape[0], N_CHIPS, h, a.shape[3]), a.dtype) for a, h in zip(big, half)]
    scratch = [pltpu.SemaphoreType.DMA((nA,)), pltpu.SemaphoreType.DMA((nA,))]
    return _comm_call(body, "exchange_halves", list(big), out_shape, scratch)


def _exchange_final(grads, everywhere, small):
    nA = len(grads)
    n_remote = sum(N_DEV - 1 if ev else 1 for ev in everywhere) + N_DEV - 1

    def body(*refs):
        small_ref, outs, gathered = refs[nA], refs[nA + 1:2 * nA + 1], refs[2 * nA + 1]
        ssem, rsem, lsem = refs[2 * nA + 2:]
        x, y, c = _mesh_pos()
        k = 2 * x + y
        me = 2 * k + c
        sib = (x, y, 1 - c)
        peers = _other_devices(x, y, c)
        local = pltpu.make_async_copy(small_ref, gathered.at[me], lsem.at[0])
        local.start()
        sends, waits = [], []
        s = 0
        for (px, py, pc) in peers:
            sends.append(_remote(small_ref, gathered.at[me], ssem.at[s], rsem.at[s], (px, py, pc)))
            got = gathered.at[4 * px + 2 * py + pc]
            waits.append(_remote(got, got, ssem.at[s], rsem.at[s], (px, py, pc)))
            s += 1
        for a in range(nA):
            if everywhere[a]:
                r2 = grads[a].shape[1] // N_DEV
                mine = outs[a].at[:, pl.ds((2 * k + c) * r2, r2), :]
                for (px, py, pc) in peers:
                    sends.append(_remote(mine, mine, ssem.at[s], rsem.at[s], (px, py, pc)))
                    got = outs[a].at[:, pl.ds((2 * (2 * px + py) + pc) * r2, r2), :]
                    waits.append(_remote(got, got, ssem.at[s], rsem.at[s], (px, py, pc)))
                    s += 1
            else:
                r2 = grads[a].shape[1] // 2
                mine = outs[a].at[:, pl.ds(c * r2, r2), :]
                sends.append(_remote(mine, mine, ssem.at[s], rsem.at[s], sib))
                got = outs[a].at[:, pl.ds((1 - c) * r2, r2), :]
                waits.append(_remote(got, got, ssem.at[s], rsem.at[s], sib))
                s += 1
        for cp in sends:
            cp.start()
        for cp in waits:
            cp.wait_recv()
        for cp in sends:
            cp.wait_send()
        local.wait()

    out_shape = [jax.ShapeDtypeStruct(g.shape, g.dtype) for g in grads]
    out_shape.append(jax.ShapeDtypeStruct((N_DEV,) + small.shape, small.dtype))
    scratch = [pltpu.SemaphoreType.DMA((n_remote,)), pltpu.SemaphoreType.DMA((n_remote,)), pltpu.SemaphoreType.DMA((1,))]
    return _comm_call(body, "exchange_final", list(grads) + [small], out_shape, scratch, {a: a for a in range(nA)})


BLOCK_BYTES = 4 << 20


def _row_tile(rows, cols, mult=16, limit=BLOCK_BYTES):
    best = mult
    for t in range(mult, rows + 1, mult):
        if rows % t == 0 and t * cols * 4 <= limit:
            best = t
    return best


def _add_cores(own, recv, pos):
    L, _, R, C = own.shape
    r2 = R // 2
    tr = _row_tile(r2, C)
    nb = r2 // tr

    def body(pos_ref, a_ref, r_ref, o_ref):
        o_ref[...] = (a_ref[...].astype(F32) + r_ref[...].astype(F32)).astype(BF16)

    blk = (None, None, tr, C)
    grid_spec = pltpu.PrefetchScalarGridSpec(
        num_scalar_prefetch=1, grid=(L, N_CHIPS, nb),
        in_specs=[pl.BlockSpec(blk, lambda l, s, i, pr: (l, s, pr[1] * nb + i, 0)),
                  pl.BlockSpec(blk, lambda l, s, i, pr: (l, s, i, 0))],
        out_specs=pl.BlockSpec(blk, lambda l, s, i, pr: (l, s, i, 0)))
    return _pallas(body, name="add_cores", grid_spec=grid_spec,
                   out_shape=jax.ShapeDtypeStruct((L, N_CHIPS, r2, C), BF16),
                   compiler_params=_params("parallel", "parallel", "parallel"))(pos, own, recv)


def _sum_chips(own, recv, pos, everywhere, layer, nlayers, prev):
    _, _, r2, C = own.shape
    tr = _row_tile(r2, 2 * C)
    nb = r2 // tr

    def body(pos_ref, a_ref, r_ref, *rest):
        acc = None
        for s in range(N_CHIPS):
            term = jnp.where(pos_ref[0] == s, a_ref[...], r_ref[s]).astype(F32)
            acc = term if acc is None else acc + term
        rest[-1][...] = acc

    if everywhere:
        def out_map(i, pr):
            return (layer, (2 * pr[0] + pr[1]) * nb + i, 0)
    else:
        def out_map(i, pr):
            return (layer, pr[1] * nb + i, 0)

    in_specs = [pl.BlockSpec((None, None, tr, C), lambda i, pr: (0, pr[0], i, 0)),
                pl.BlockSpec((None, N_CHIPS, tr, C), lambda i, pr: (0, 0, i, 0))]
    grid_spec = pltpu.PrefetchScalarGridSpec(
        num_scalar_prefetch=1, grid=(nb,), in_specs=in_specs + ([] if prev is None else [ANY]),
        out_specs=pl.BlockSpec((None, tr, C), out_map))
    rows = (N_DEV if everywhere else 2) * r2
    args = (pos, own, recv) if prev is None else (pos, own, recv, prev)
    return _pallas(body, name="sum_chips", grid_spec=grid_spec, out_shape=jax.ShapeDtypeStruct((nlayers, rows, C), F32),
                   input_output_aliases={} if prev is None else {3: 0},
                   compiler_params=_params("parallel"))(*args)


def _sum_devices(parts):
    n, R, C = parts.shape
    tr = _row_tile(R, C * n, 8)

    def body(p_ref, o_ref):
        acc = p_ref[0]
        for s in range(1, n):
            acc = acc + p_ref[s]
        o_ref[...] = acc

    return _pallas(body, name="sum_devices", grid=(R // tr,), in_specs=[pl.BlockSpec((n, tr, C), lambda i: (0, i, 0))],
                   out_specs=pl.BlockSpec((tr, C), lambda i: (i, 0)), out_shape=jax.ShapeDtypeStruct((R, C), F32),
                   compiler_params=_params("parallel"))(parts)


def _adamw(w, g, m, v, name):
    L, R, C = w.shape
    tr = _row_tile(R, C, 8, BLOCK_BYTES // 2)

    def body(w_ref, g_ref, m_ref, v_ref, d_ref, m2_ref, v2_ref):
        gg = g_ref[...]
        m2 = ADAM_B1 * m_ref[...] + (1.0 - ADAM_B1) * gg
        v2 = ADAM_B2 * v_ref[...] + (1.0 - ADAM_B2) * (gg * gg)
        m_hat = m2 / (1.0 - ADAM_B1 ** ADAM_STEP)
        v_hat = v2 / (1.0 - ADAM_B2 ** ADAM_STEP)
        d_ref[...] = -ADAM_LR * (m_hat / (jnp.sqrt(v_hat) + ADAM_EPS) + ADAM_WD * w_ref[...])
        m2_ref[...] = m2
        v2_ref[...] = v2

    blk = pl.BlockSpec((1, tr, C), lambda l, i: (l, i, 0))
    shp = jax.ShapeDtypeStruct((L, R, C), F32)
    return _pallas(body, name=name, grid=(L, R // tr), in_specs=[blk] * 4, out_specs=[blk] * 3, out_shape=[shp] * 3,
                   compiler_params=_params("parallel", "parallel"))(w, g, m, v)


WEIGHTS = ("norm_even", "w_in_even", "conv_a_w", "conv_a_b", "ln_a_g", "ln_a_b", "pool_w", "pool_b", "pool_scale",
           "w_out_even", "norm_odd", "w_in_odd", "conv_c_w", "conv_c_b", "w_rg", "b_rg", "w_ig", "b_ig", "lru_lambda",
           "w_out_odd", "final_norm")
BIG = ("w_in_even", "w_out_even", "pool_w", "w_in_odd", "w_out_odd", "w_rg", "w_ig")
SMALL = tuple(n for n in WEIGHTS if n not in BIG)
SMALL_SHARDED = ("conv_a_w", "pool_b", "norm_odd", "conv_c_w", "conv_c_b", "b_rg", "b_ig", "lru_lambda")


def _pack(arrs):
    flat = jnp.concatenate([a.reshape(-1) for a in arrs])
    rows = -(-flat.shape[0] // (64 * 128)) * 64
    return jnp.pad(flat, (0, rows * 128 - flat.shape[0])).reshape(rows, 128)


def _unpack(buf, shapes, lead=()):
    flat = buf.reshape(tuple(lead) + (-1,))
    out, o = [], 0
    for s in shapes:
        n = 1
        for d in s:
            n *= d
        out.append(flat[..., o:o + n].reshape(tuple(lead) + tuple(s)))
        o += n
    return out


def _shard(full, axis, k):
    n = full.shape[axis] // N_CHIPS
    return lax.dynamic_slice_in_dim(full, k * n, n, axis)


def kernel(x, norm_even, w_in_even, conv_a_w, conv_a_b, ln_a_g, ln_a_b, pool_w, pool_b, pool_scale, w_out_even, norm_odd, w_in_odd, conv_c_w, conv_c_b, w_rg, b_rg, w_ig, b_ig, lru_lambda, w_out_odd, final_norm, loss_target, m_norm_even, m_w_in_even, m_conv_a_w, m_conv_a_b, m_ln_a_g, m_ln_a_b, m_pool_w, m_pool_b, m_pool_scale, m_w_out_even, m_norm_odd, m_w_in_odd, m_conv_c_w, m_conv_c_b, m_w_rg, m_b_rg, m_w_ig, m_b_ig, m_lru_lambda, m_w_out_odd, m_final_norm, v_norm_even, v_w_in_even, v_conv_a_w, v_conv_a_b, v_ln_a_g, v_ln_a_b, v_pool_w, v_pool_b, v_pool_scale, v_w_out_even, v_norm_odd, v_w_in_odd, v_conv_c_w, v_conv_c_b, v_w_rg, v_b_rg, v_w_ig, v_b_ig, v_lru_lambda, v_w_out_odd, v_final_norm):
    P = dict(locals())
    xi, yi, ci = _mesh_pos()
    k = 2 * xi + yi
    L = w_in_even.shape[0]
    D = D_MODEL

    pos = jnp.stack([k, ci]).astype(jnp.int32)
    depth = 2 * L
    pool_w3 = pool_w.reshape(L, 4 * 64, POOL_GW)

    def cast_group(layer):
        j = layer // 2
        if layer % 2 == 0:
            return [_cast_shard(w_in_even, j, pos), _cast_shard(w_out_even, j, pos), _cast_shard(pool_w3, j, pos)]
        return [_cast_shard(w_in_odd, j, pos), _cast_shard(w_out_odd, j, pos)]

    first = cast_group(0)
    g_in, g_pool, g_small = _gather_weights([first[0], first[2]], _pack([P[n] for n in SMALL_SHARDED]))
    copies0 = _gather_copies([first[1].shape])
    ssem0, rsem0, late, token0 = _split_start([first[1]], copies0, 3, "gather_start0")

    def late_w_out(y):
        return _forward_cores(_split_wait(ssem0, rsem0, late, copies0, y, "gather_wait0"))[0].reshape(1, -1, D)

    group = [g_in, late_w_out, g_pool]
    full = {}
    for n, a in zip(SMALL_SHARDED, _unpack(g_small, [P[n].shape for n in SMALL_SHARDED], lead=(N_CHIPS,))):
        a = jnp.moveaxis(a, 0, -2)
        full[n] = a.reshape(a.shape[:-2] + (N_CHIPS * a.shape[-1],))

    def small_weights(layer, group):
        j = layer // 2
        if layer % 2 == 0:
            cw = full["conv_a_w"][j]
            pw = group[2].reshape(N_CHIPS, 4, 64, POOL_GW).transpose(1, 0, 2, 3).reshape(4, POOL_GW, POOL_GW)
            return dict(norm=norm_even[j][None], conv_w=_pad_rows(cw, 32), conv_w_rev=_pad_rows(cw[::-1], 32),
                        conv_b=conv_a_b[j][None], ln_g=ln_a_g[j][None], ln_b=ln_a_b[j][None], pool_w=pw,
                        pool_b=full["pool_b"][j].reshape(1, D), pool_scale=pool_scale[j][None])
        return dict(norm=full["norm_odd"][j][None], conv_w=_pad_rows(full["conv_c_w"][j], 8),
                    conv_b=full["conv_c_b"][j][None], w_rg=w_rg[j].astype(BF16), b_rg=full["b_rg"][j][None],
                    w_ig=w_ig[j].astype(BF16), b_ig=full["b_ig"][j][None], lam=full["lru_lambda"][j][None])

    no_token = jnp.zeros((8, 128), F32)
    h = x[0]
    saved, big_w, small_w = [], [], []
    for layer in range(depth):
        token = token0 if layer == 0 else no_token
        if layer + 1 < depth:
            nxt = cast_group(layer + 1)
            copies = _gather_copies([a.shape for a in nxt])
            ssem, rsem, nxt, token = _split_start(nxt, copies, 3 * len(nxt), "gather_start%d" % (layer + 1))
        small_w.append(small_weights(layer, group))
        w_out = group[1] if callable(group[1]) else group[1].reshape(1, -1, D)
        h, sv, w_out = _layer_fwd(layer % 2 == 0, h, small_w[layer], group[0], w_out, token)
        big_w.append((group[0], w_out))
        saved.append(sv)
        if layer + 1 < depth:
            group = _forward_cores(_split_wait(ssem, rsem, nxt, copies, h, "gather_wait%d" % (layer + 1)))

    dh, dhb, d_final, loss = _loss_head(h, final_norm[None], loss_target[0])
    loss = lax.psum(loss[0, 0], ("x", "y", "c"))
    everywhere = [False, False, False, False, False, True, True]
    final = [None] * len(everywhere)
    small_of = [None] * depth

    def finish(pending, after):
        ssem, rsem, arrs, copies, slots, pj, pl_ = pending
        arrs = _split_wait(ssem, rsem, arrs, copies, after, "chips_wait%d" % pl_)
        for a, r, s in zip(arrs[:len(slots)], arrs[len(slots):], slots):
            final[s] = _sum_chips(a, r, pos, everywhere[s], pj, L, final[s])

    pending = None
    token = no_token
    for layer in reversed(range(depth)):
        j = layer // 2
        even_layer = layer % 2 == 0
        dp, dw_in, dw_out, sm = _layer_bwd_weights(even_layer, saved[layer], small_w[layer], big_w[layer][1], dhb, token)
        if even_layer:
            dpw = sm["pool_w"].reshape(4, N_CHIPS, 64, POOL_GW).transpose(1, 0, 2, 3)
            parts = [dw_in, dw_out.reshape(1, N_CHIPS, -1, D), dpw.reshape(1, N_CHIPS, 4 * 64, POOL_GW).astype(BF16)]
            slots = [0, 1, 2]
        else:
            parts = [dw_in, dw_out.reshape(1, N_CHIPS, -1, D),
                     sm["w_rg"].reshape(1, N_CHIPS, -1, LRU_HD).astype(BF16),
                     sm["w_ig"].reshape(1, N_CHIPS, -1, LRU_HD).astype(BF16)]
            slots = [3, 4, 5, 6]
        pair = [_add_cores(a, r, pos) for a, r in zip(parts, _exchange_halves(parts))]
        copies = _chips_copies(len(pair))
        land = [lax.empty(a.shape, a.dtype) for a in pair]
        ssem, rsem, arrs, token = _split_start(pair + land, copies, 3 * len(pair), "chips_start%d" % layer)
        dh, dhb, sm["norm"] = _layer_bwd_input(even_layer, saved[layer], small_w[layer], big_w[layer][0], dp, dh, token)
        small_of[layer] = sm
        if pending is not None:
            finish(pending, dh)
        pending = (ssem, rsem, arrs, copies, slots, j, layer)
    grad_x = dh
    small_g = []
    for jj in range(L):
        ge, go = small_of[2 * jj], small_of[2 * jj + 1]
        small_g += [ge["conv_w"].reshape(32, 8, D).sum(axis=1)[:CONV_K], ge["vec"][0:5], ge["norm"], go["vec"], go["norm"]]
    small_g.append(d_final)
    small_shapes = [a.shape for a in small_g]
    packed_small = _pack(small_g)
    finish(pending, packed_small)
    *gw, recv_small = _exchange_final(final, everywhere, packed_small)
    sg = _unpack(_sum_devices(recv_small), small_shapes)

    grads = dict(w_in_even=gw[0], w_out_even=gw[1], pool_w=gw[2].reshape(pool_w.shape), w_in_odd=gw[3], w_out_odd=gw[4],
                 w_rg=gw[5].reshape(w_rg.shape), w_ig=gw[6].reshape(w_ig.shape), final_norm=sg[-1][0])
    ev = [sg[5 * j + 1] for j in range(L)]
    ov = [sg[5 * j + 3] for j in range(L)]
    grads["conv_a_w"] = _shard(jnp.stack([sg[5 * j] for j in range(L)]), 2, k)
    grads["norm_even"] = jnp.stack([sg[5 * j + 2][0] for j in range(L)])
    grads["norm_odd"] = _shard(jnp.stack([sg[5 * j + 4][0] for j in range(L)]), 1, k)
    for r, n in enumerate(("conv_a_b", "ln_a_g", "ln_a_b", "pool_scale")):
        grads[n] = jnp.stack([e[r] for e in ev])
    grads["pool_b"] = _shard(jnp.stack([e[4].reshape(4, POOL_GW) for e in ev]), 2, k)
    grads["conv_c_w"] = _shard(jnp.stack([o[0:4] for o in ov]), 2, k)
    for r, n in zip((4, 5, 6, 7), ("conv_c_b", "b_rg", "b_ig", "lru_lambda")):
        grads[n] = _shard(jnp.stack([o[r] for o in ov]), 1, k)

    delta, new_m, new_v = {}, {}, {}
    for n in BIG:
        s3 = (L, -1, P[n].shape[-1])
        d, m2, v2 = _adamw(P[n].reshape(s3), grads[n].reshape(s3), P["m_" + n].reshape(s3), P["v_" + n].reshape(s3), "adamw")
        delta[n], new_m[n], new_v[n] = d.reshape(P[n].shape), m2.reshape(P[n].shape), v2.reshape(P[n].shape)
    shapes = [P[n].shape for n in SMALL]
    packed = [_pack([src[n] for n in SMALL])[None] for src in
              (P, grads, {n: P["m_" + n] for n in SMALL}, {n: P["v_" + n] for n in SMALL})]
    for res, out in zip(_adamw(*packed, "adamw_small"), (delta, new_m, new_v)):
        for n, a in zip(SMALL, _unpack(res[0], shapes)):
            out[n] = a

    return (loss, grad_x[None], *[grads[n] for n in WEIGHTS], *[delta[n] for n in WEIGHTS],
            *[new_m[n] for n in WEIGHTS], *[new_v[n] for n in WEIGHTS])
```

```python
import functools

import jax
import jax.numpy as jnp
from jax import lax
from jax.experimental import pallas as pl
from jax.experimental.pallas import tpu as pltpu

F32 = jnp.float32
BF16 = jnp.bfloat16
MESH = pl.DeviceIdType.MESH

D_MODEL = 1024
N_CHIPS = 4
N_DEV = 8
EPS_RMS = 1e-6
EPS_LN = 1e-5
CONV_K = 31
POOL_WINDOWS = (2, 4, 8, 16)
POOL_GW = 256
LRU_HEADS = 12
LRU_HD = 128
W_LRU = LRU_HEADS * LRU_HD
LRU_CONV_K = 4
LRU_C = 8.0
ADAM_LR = 0.001
ADAM_B1 = 0.9
ADAM_B2 = 0.999
ADAM_EPS = 1e-08
ADAM_WD = 0.01
ADAM_STEP = 10

VMEM_LIMIT_BYTES = 56 * 1024 * 1024
ROW_TILE = 512
MIX_TILE = 256
EVEN_HALO = 32
ODD_HALO = 8


def _pallas(body, **kw):
    return pl.pallas_call(body, **kw)


def _params(*sem):
    return pltpu.CompilerParams(dimension_semantics=sem if sem else None, vmem_limit_bytes=VMEM_LIMIT_BYTES)


def _sigmoid(x):
    return 0.5 * jnp.tanh(0.5 * x) + 0.5


def _dsilu(x, s):
    return s * (1.0 + x * (1.0 - s))


def _nt(a, b):
    return lax.dot_general(a, b, (((1,), (1,)), ((), ())), preferred_element_type=F32)


def _tn(a, b):
    return lax.dot_general(a, b, (((0,), (0,)), ((), ())), preferred_element_type=F32)


def _in_proj(h, g, wg, layer, after, name):
    T, D = h.shape
    _, nblk, _, nb = wg.shape

    nrow = T // ROW_TILE

    def body(h_ref, g_ref, w_ref, after_ref, p_ref, n_ref, n_all):
        j, i = pl.program_id(0), pl.program_id(1)

        @pl.when(j == 0)
        def _():
            x = h_ref[...]
            r = lax.rsqrt(jnp.mean(x * x, axis=-1, keepdims=True) + EPS_RMS)
            nn = (x * r * g_ref[...]).astype(BF16)
            n_ref[...] = nn
            n_all[i] = nn

        p_ref[...] = jnp.dot(n_all[i], w_ref[0], preferred_element_type=F32)

    def rows_once(j, i):
        return (jnp.where(j == 0, i, nrow - 1), 0)

    return _pallas(
        body, name=name, grid=(nblk, nrow),
        in_specs=[pl.BlockSpec((ROW_TILE, D), rows_once), pl.BlockSpec((1, D), lambda j, i: (0, 0)),
                  pl.BlockSpec((None, 1, D, nb), lambda j, i: (layer, j, 0, 0)),
                  pl.BlockSpec((8, 128), lambda j, i: (0, 0))],
        out_specs=[pl.BlockSpec((ROW_TILE, nb), lambda j, i: (i, j)), pl.BlockSpec((ROW_TILE, D), rows_once)],
        out_shape=[jax.ShapeDtypeStruct((T, nblk * nb), F32), jax.ShapeDtypeStruct((T, D), BF16)],
        scratch_shapes=[pltpu.VMEM((nrow, ROW_TILE, D), BF16)],
        compiler_params=_params("arbitrary", "arbitrary"))(h, g, wg, after)


def _out_proj(y, w, layer, hres, name):
    T, K = y.shape
    D = w.shape[2]

    def body(y_ref, w_ref, r_ref, o_ref):
        o_ref[...] = r_ref[...] + jnp.dot(y_ref[...], w_ref[...], preferred_element_type=F32)

    return _pallas(
        body, name=name, grid=(T // ROW_TILE,),
        in_specs=[pl.BlockSpec((ROW_TILE, K), lambda i: (i, 0)), pl.BlockSpec((None, K, D), lambda i: (layer, 0, 0)),
                  pl.BlockSpec((ROW_TILE, D), lambda i: (i, 0))],
        out_specs=pl.BlockSpec((ROW_TILE, D), lambda i: (i, 0)),
        out_shape=jax.ShapeDtypeStruct((T, D), F32),
        compiler_params=_params("parallel"))(y, w, hres)


def _dn_proj(dp, wg, layer, h, g, dres, after, name):
    T, D = h.shape
    _, nblk, _, nb = wg.shape

    nrow = T // ROW_TILE

    def body(dp_ref, w_ref, h_ref, g_ref, dres_ref, after_ref, dh_ref, dhb_ref, dg_ref, acc_ref):
        j, i = pl.program_id(0), pl.program_id(1)
        part = _nt(dp_ref[...], w_ref[0])

        @pl.when(j == 0)
        def _():
            acc_ref[i] = part

        @pl.when(j > 0)
        def _():
            acc_ref[i] += part

        @pl.when(j == nblk - 1)
        def _():
            x = h_ref[...]
            r = lax.rsqrt(jnp.mean(x * x, axis=-1, keepdims=True) + EPS_RMS)
            dn = acc_ref[i]
            q = dn * g_ref[...]
            dh = dres_ref[...] + r * q - x * ((r * r * r) * jnp.mean(q * x, axis=-1, keepdims=True))
            dh_ref[...] = dh
            dhb_ref[...] = dh.astype(BF16)
            dgp = jnp.sum(dn * (x * r), axis=0, keepdims=True)

            @pl.when(i == 0)
            def _():
                dg_ref[...] = dgp

            @pl.when(i > 0)
            def _():
                dg_ref[...] += dgp

    def rows_last(j, i):
        return (jnp.where(j == nblk - 1, i, 0), 0)

    return _pallas(
        body, name=name, grid=(nblk, nrow),
        in_specs=[pl.BlockSpec((ROW_TILE, nb), lambda j, i: (i, j)),
                  pl.BlockSpec((None, 1, D, nb), lambda j, i: (layer, j, 0, 0)),
                  pl.BlockSpec((ROW_TILE, D), rows_last), pl.BlockSpec((1, D), lambda j, i: (0, 0)),
                  pl.BlockSpec((ROW_TILE, D), rows_last), pl.BlockSpec((8, 128), lambda j, i: (0, 0))],
        out_specs=[pl.BlockSpec((ROW_TILE, D), rows_last), pl.BlockSpec((ROW_TILE, D), rows_last),
                   pl.BlockSpec((1, D), lambda j, i: (0, 0))],
        out_shape=[jax.ShapeDtypeStruct((T, D), F32), jax.ShapeDtypeStruct((T, D), BF16),
                   jax.ShapeDtypeStruct((1, D), F32)],
        scratch_shapes=[pltpu.VMEM((nrow, ROW_TILE, D), F32)],
        compiler_params=_params("arbitrary", "arbitrary"))(dp, wg, h, g, dres, after)


def _dw_in(n, dp, nblk, layer, nlayers, prev, name):
    T, D = n.shape
    nb = dp.shape[1] // nblk
    ta = D

    def body(n_ref, dp_ref, *rest):
        rest[-1][0] = _tn(n_ref[...], dp_ref[...]).astype(BF16)

    in_specs = [pl.BlockSpec((T, ta), lambda j, i: (0, i)), pl.BlockSpec((T, nb), lambda j, i: (0, j))]
    args = (n, dp) if prev is None else (n, dp, prev)
    return _pallas(
        body, name=name, grid=(nblk, D // ta), in_specs=in_specs + ([] if prev is None else [ANY]),
        out_specs=pl.BlockSpec((None, 1, ta, nb), lambda j, i: (layer, j, i, 0)),
        out_shape=jax.ShapeDtypeStruct((nlayers, nblk, D, nb), BF16),
        input_output_aliases={} if prev is None else {2: 0},
        compiler_params=_params("parallel", "parallel"))(*args)


def _dw_out(y, dout, layer, nlayers, prev, name):
    T, K = y.shape
    D = dout.shape[1]
    tk = 512

    def body(y_ref, d_ref, *rest):
        rest[-1][...] = _tn(y_ref[...], d_ref[...]).astype(BF16)

    in_specs = [pl.BlockSpec((T, tk), lambda i: (0, i)), pl.BlockSpec((T, D), lambda i: (0, 0))]
    args = (y, dout) if prev is None else (y, dout, prev)
    return _pallas(
        body, name=name, grid=(K // tk,), in_specs=in_specs + ([] if prev is None else [ANY]),
        out_specs=pl.BlockSpec((None, tk, D), lambda i: (layer, i, 0)),
        out_shape=jax.ShapeDtypeStruct((nlayers, K, D), BF16),
        input_output_aliases={} if prev is None else {2: 0},
        compiler_params=_params("parallel"))(*args)


def _loss_head(h, g, tgt):
    T, D = h.shape
    tm = MIX_TILE

    def body(h_ref, g_ref, t_ref, dh_ref, dhb_ref, dg_ref, loss_ref):
        i = pl.program_id(0)
        x = h_ref[...]
        gg = g_ref[...]
        r = lax.rsqrt(jnp.mean(x * x, axis=-1, keepdims=True) + EPS_RMS)
        xr = x * r
        e = xr * gg - t_ref[...]
        lp = 0.5 * jnp.sum(jnp.mean(e * e, axis=-1, keepdims=True), axis=0, keepdims=True)
        dn = e * (1.0 / D)
        q = dn * gg
        dh = r * q - x * ((r * r * r) * jnp.mean(q * x, axis=-1, keepdims=True))
        dh_ref[...] = dh
        dhb_ref[...] = dh.astype(BF16)
        dgp = jnp.sum(dn * xr, axis=0, keepdims=True)

        @pl.when(i == 0)
        def _():
            dg_ref[...] = dgp
            loss_ref[...] = lp

        @pl.when(i > 0)
        def _():
            dg_ref[...] += dgp
            loss_ref[...] += lp

    return _pallas(
        body, name="loss_head", grid=(T // tm,),
        in_specs=[pl.BlockSpec((tm, D), lambda i: (i, 0)), pl.BlockSpec((1, D), lambda i: (0, 0)),
                  pl.BlockSpec((tm, D), lambda i: (i, 0))],
        out_specs=[pl.BlockSpec((tm, D), lambda i: (i, 0)), pl.BlockSpec((tm, D), lambda i: (i, 0)),
                   pl.BlockSpec((1, D), lambda i: (0, 0)), pl.BlockSpec((1, 1), lambda i: (0, 0))],
        out_shape=[jax.ShapeDtypeStruct((T, D), F32), jax.ShapeDtypeStruct((T, D), BF16),
                   jax.ShapeDtypeStruct((1, D), F32), jax.ShapeDtypeStruct((1, 1), F32)],
        compiler_params=_params("arbitrary"))(h, g, tgt)


def _shift_up(x, j):
    return x if j == 0 else pltpu.roll(x, x.shape[0] - j, 0)


def _shift_down(x, j):
    return x if j == 0 else pltpu.roll(x, j, 0)


def _fill_shifted(dst_ref, src_ref):
    rows = dst_ref.shape[1]
    for s in range(8):
        dst_ref[s] = src_ref[pl.ds(s, rows), :]


def _fill_taps(wb_ref, w_ref):
    for k in range(w_ref.shape[0]):
        wb_ref[k] = jnp.broadcast_to(w_ref[k:k + 1, :], wb_ref.shape[1:])


def _tap_sum(sh_ref, wb_ref, r0, nrows, offsets):
    accs = [None] * (nrows // 8)
    for k, o in enumerate(offsets):
        wk = wb_ref[k]
        for u in range(nrows // 8):
            term = wk * sh_ref[o % 8, pl.ds(r0 + (o // 8) * 8 + 8 * u, 8), :]
            accs[u] = term if accs[u] is None else accs[u] + term
    return jnp.concatenate(accs, axis=0)


def _pool_sums(vx, up):
    sh = _shift_up if up else _shift_down
    outs = []
    for gi, w in enumerate(POOL_WINDOWS):
        s = vx[:, gi * POOL_GW:(gi + 1) * POOL_GW]
        j = 1
        while j < w:
            s = s + sh(s, j)
            j *= 2
        outs.append(s)
    return outs


def _inv_count(row0, nrows):
    pos = (row0 + 1 + lax.broadcasted_iota(jnp.int32, (nrows, 1), 0)).astype(F32)
    return [1.0 / jnp.minimum(pos, float(w)) for w in POOL_WINDOWS]


def _even_mixer_fwd(p, cw, cb, lg, lb, pw, pb, sc, name):
    T = p.shape[0]
    C = D_MODEL
    tT, HL = MIX_TILE, EVEN_HALO
    hb = tT // HL
    chunk = 32

    def body(pm_ref, ph_ref, cw_ref, cb_ref, lg_ref, lb_ref, pw_ref, pb_ref, sc_ref, y_ref, u1_ref, u0x_ref, sh_ref,
             wb_ref):
        i = pl.program_id(0)
        keep = (i > 0).astype(F32)

        @pl.when(i == 0)
        def _():
            _fill_taps(wb_ref, cw_ref)

        u0x_ref[0:HL] = ph_ref[:, 0:C] * _sigmoid(ph_ref[:, C:2 * C]) * keep
        u0x_ref[HL:HL + tT] = pm_ref[:, 0:C] * _sigmoid(pm_ref[:, C:2 * C])
        u0x_ref[HL + tT:HL + tT + 8] = jnp.zeros((8, C), F32)
        _fill_shifted(sh_ref, u0x_ref)
        offs = [HL - (CONV_K - 1) + k for k in range(CONV_K)]

        def conv_chunk(c, carry):
            r0 = pl.multiple_of(c * chunk, chunk)
            u1_ref[pl.ds(r0, chunk), :] = _tap_sum(sh_ref, wb_ref, r0, chunk, offs) + cb_ref[...]
            return carry

        lax.fori_loop(0, tT // chunk, conv_chunk, 0)
        u1 = u1_ref[...]
        mu = jnp.mean(u1, axis=-1, keepdims=True)
        xc = u1 - mu
        rs = lax.rsqrt(jnp.mean(xc * xc, axis=-1, keepdims=True) + EPS_LN)
        u2 = xc * rs * lg_ref[...] + lb_ref[...]
        u3 = u2 * _sigmoid(u2)
        ag = pm_ref[:, 2 * C:3 * C]
        y_ref[:, 0:C] = (u3 * (ag * _sigmoid(ag))).astype(BF16)
        vx = jnp.concatenate([ph_ref[:, 3 * C:4 * C] * keep, pm_ref[:, 3 * C:4 * C]], axis=0)
        sums = _pool_sums(vx, up=False)
        inv = _inv_count(i * tT, tT)
        for gi in range(len(POOL_WINDOWS)):
            cols = slice(gi * POOL_GW, (gi + 1) * POOL_GW)
            d0 = sums[gi][HL:] * inv[gi] - vx[HL:, cols]
            d1 = jnp.dot(d0.astype(BF16), pw_ref[gi], preferred_element_type=F32) + pb_ref[:, cols]
            bg = pm_ref[:, 4 * C + gi * POOL_GW:4 * C + (gi + 1) * POOL_GW]
            y_ref[:, C + gi * POOL_GW:C + (gi + 1) * POOL_GW] = (d1 * sc_ref[:, cols] * (bg * _sigmoid(bg))).astype(BF16)

    vec = pl.BlockSpec((1, C), lambda i: (0, 0))
    return _pallas(
        body, name=name, grid=(T // tT,),
        in_specs=[pl.BlockSpec((tT, 5 * C), lambda i: (i, 0)),
                  pl.BlockSpec((HL, 5 * C), lambda i: (jnp.maximum(i * hb - 1, 0), 0)),
                  pl.BlockSpec((32, C), lambda i: (0, 0)), vec, vec, vec,
                  pl.BlockSpec((4, POOL_GW, POOL_GW), lambda i: (0, 0, 0)), vec, vec],
        out_specs=[pl.BlockSpec((tT, 2 * C), lambda i: (i, 0)), pl.BlockSpec((tT, C), lambda i: (i, 0))],
        out_shape=[jax.ShapeDtypeStruct((T, 2 * C), BF16), jax.ShapeDtypeStruct((T, C), F32)],
        scratch_shapes=[pltpu.VMEM((HL + tT + 8, C), F32), pltpu.VMEM((8, HL + tT, C), F32),
                        pltpu.VMEM((32, 8, C), F32)],
        compiler_params=_params("arbitrary"))(p, p, cw, cb, lg, lb, pw, pb, sc)


def _even_mixer_bwd(p, u1, dout, w_out, after, cwr, lg, lb, pw, pb, sc, name):
    T = p.shape[0]
    C = D_MODEL
    tT, HL = MIX_TILE, EVEN_HALO
    hb = tT // HL
    nT = T // tT
    R1 = tT + HL
    chunk = 32

    def body(pm_ref, pp_ref, pn_ref, u1m_ref, u1n_ref, dom_ref, don_ref, wo_ref, after_ref, cwr_ref, lg_ref, lb_ref,
             pw_ref, pb_ref, sc_ref, dp_ref, dcw_ref, dvec_ref, dpw_ref, x_ref, sh_ref, du0_ref, wb_ref):
        i = pl.program_id(0)
        dy = _nt(jnp.concatenate([dom_ref[...], don_ref[...]], axis=0), wo_ref[...])

        @pl.when(i == 0)
        def _():
            _fill_taps(wb_ref, cwr_ref)

        keep_prev = (i > 0).astype(F32)
        keep_next = (i < nT - 1).astype(F32)
        row = lax.broadcasted_iota(jnp.int32, (R1, 1), 0)
        live = jnp.where(row < tT, 1.0, keep_next)

        def cat(m, n):
            return jnp.concatenate([m, n], axis=0)

        u1 = cat(u1m_ref[...], u1n_ref[...])
        mu = jnp.mean(u1, axis=-1, keepdims=True)
        xc = u1 - mu
        rs = lax.rsqrt(jnp.mean(xc * xc, axis=-1, keepdims=True) + EPS_LN)
        xh = xc * rs
        u2 = xh * lg_ref[...] + lb_ref[...]
        s2 = _sigmoid(u2)
        u3 = u2 * s2
        ag = cat(pm_ref[:, 2 * C:3 * C], pn_ref[:, 2 * C:3 * C])
        sa = _sigmoid(ag)
        dya = dy[:, 0:C]
        dp_ref[:, 2 * C:3 * C] = (dya * u3 * _dsilu(ag, sa))[0:tT].astype(BF16)
        du2 = dya * (ag * sa) * _dsilu(u2, s2)
        dlg = jnp.sum((du2 * xh)[0:tT], axis=0, keepdims=True)
        dlb = jnp.sum(du2[0:tT], axis=0, keepdims=True)
        dxh = du2 * lg_ref[...]
        du1 = rs * (dxh - jnp.mean(dxh, axis=-1, keepdims=True) - xh * jnp.mean(dxh * xh, axis=-1, keepdims=True))
        du1 = du1 * live
        dcb = jnp.sum(du1[0:tT], axis=0, keepdims=True)
        x_ref[0:R1] = du1
        x_ref[R1:R1 + 8] = jnp.zeros((8, C), F32)
        _fill_shifted(sh_ref, x_ref)

        def du0_chunk(c, carry):
            r0 = pl.multiple_of(c * chunk, chunk)
            du0_ref[pl.ds(r0, chunk), :] = _tap_sum(sh_ref, wb_ref, r0, chunk, list(range(CONV_K)))
            return carry

        lax.fori_loop(0, tT // chunk, du0_chunk, 0)
        av, agl = pm_ref[:, 0:C], pm_ref[:, C:2 * C]
        sg = _sigmoid(agl)
        du0 = du0_ref[...]
        dp_ref[:, 0:C] = (du0 * sg).astype(BF16)
        dp_ref[:, C:2 * C] = (du0 * av * sg * (1.0 - sg)).astype(BF16)
        du0_ref[...] = du1[0:tT]
        x_ref[0:HL] = pp_ref[:, 0:C] * _sigmoid(pp_ref[:, C:2 * C]) * keep_prev
        x_ref[HL:HL + tT] = av * sg
        x_ref[HL + tT:HL + tT + 8] = jnp.zeros((8, C), F32)
        _fill_shifted(sh_ref, x_ref)

        @pl.when(i == 0)
        def _():
            dcw_ref[...] = jnp.zeros_like(dcw_ref)

        for k0 in range(0, CONV_K, 2):
            taps = [k for k in (k0, k0 + 1) if k < CONV_K]
            offs = [HL - (CONV_K - 1) + k for k in taps]

            def dw_chunk(c, accs, offs=offs):
                r0 = pl.multiple_of(c * 64, 64)
                accs = list(accs)
                for u in range(0, 64, 8):
                    d = du0_ref[pl.ds(r0 + u, 8), :]
                    for t, o in enumerate(offs):
                        accs[t] = accs[t] + d * sh_ref[o % 8, pl.ds(r0 + u + (o // 8) * 8, 8), :]
                return tuple(accs)

            sums = lax.fori_loop(0, tT // 64, dw_chunk, tuple(jnp.zeros((8, C), F32) for _ in taps))
            for k, acc in zip(taps, sums):
                dcw_ref[8 * k:8 * k + 8, :] += acc

        bg = cat(pm_ref[:, 4 * C:5 * C], pn_ref[:, 4 * C:5 * C])
        sb = _sigmoid(bg)
        dyb = dy[:, C:2 * C]
        dyb0 = dyb * (bg * sb)
        dd1 = dyb0 * sc_ref[...]
        dpb = jnp.sum(dd1[0:tT], axis=0, keepdims=True)
        inv1 = _inv_count(i * tT, R1)
        z_parts, dd0_parts = [], []
        for gi in range(len(POOL_WINDOWS)):
            cols = slice(gi * POOL_GW, (gi + 1) * POOL_GW)
            dd0 = _nt(dd1[:, cols].astype(BF16), pw_ref[gi])
            dd0_parts.append(dd0)
            z_parts.append(dd0 * inv1[gi] * live)
        fsum = _pool_sums(jnp.concatenate(z_parts, axis=1), up=True)
        vx = cat(pp_ref[:, 3 * C:4 * C] * keep_prev, pm_ref[:, 3 * C:4 * C])
        sums = _pool_sums(vx, up=False)
        inv0 = _inv_count(i * tT, tT)
        dsc_parts = []
        for gi in range(len(POOL_WINDOWS)):
            cols = slice(gi * POOL_GW, (gi + 1) * POOL_GW)
            dp_ref[:, 3 * C + gi * POOL_GW:3 * C + (gi + 1) * POOL_GW] = (fsum[gi][0:tT] - dd0_parts[gi][0:tT]).astype(BF16)
            d0 = (sums[gi][HL:] * inv0[gi] - vx[HL:, cols]).astype(BF16)
            d1 = jnp.dot(d0, pw_ref[gi], preferred_element_type=F32) + pb_ref[:, cols]
            bgm, sbm = bg[0:tT, cols], sb[0:tT, cols]
            dp_ref[:, 4 * C + gi * POOL_GW:4 * C + (gi + 1) * POOL_GW] = (
                dyb[0:tT, cols] * d1 * sc_ref[:, cols] * _dsilu(bgm, sbm)).astype(BF16)
            dsc_parts.append(jnp.sum(dyb0[0:tT, cols] * d1, axis=0, keepdims=True))
            dpw_g = _tn(d0, dd1[0:tT, cols].astype(BF16))

            @pl.when(i == 0)
            def _(gi=gi, dpw_g=dpw_g):
                dpw_ref[gi] = dpw_g

            @pl.when(i > 0)
            def _(gi=gi, dpw_g=dpw_g):
                dpw_ref[gi] += dpw_g

        dsc = jnp.concatenate(dsc_parts, axis=1)
        vecs = jnp.concatenate([dcb, dlg, dlb, dsc, dpb, jnp.zeros((3, C), F32)], axis=0)

        @pl.when(i == 0)
        def _():
            dvec_ref[...] = vecs

        @pl.when(i > 0)
        def _():
            dvec_ref[...] += vecs

    vec = pl.BlockSpec((1, C), lambda i: (0, 0))
    taps = pl.BlockSpec((32, C), lambda i: (0, 0))

    def prev_blk(i):
        return (jnp.maximum(i * hb - 1, 0), 0)

    def next_blk(i):
        return (jnp.minimum((i + 1) * hb, T // HL - 1), 0)

    return _pallas(
        body, name=name, grid=(nT,),
        in_specs=[pl.BlockSpec((tT, 5 * C), lambda i: (i, 0)), pl.BlockSpec((HL, 5 * C), prev_blk),
                  pl.BlockSpec((HL, 5 * C), next_blk),
                  pl.BlockSpec((tT, C), lambda i: (i, 0)), pl.BlockSpec((HL, C), next_blk),
                  pl.BlockSpec((tT, C), lambda i: (i, 0)), pl.BlockSpec((HL, C), next_blk),
                  pl.BlockSpec((None, 2 * C, C), lambda i: (0, 0, 0)), pl.BlockSpec((8, 128), lambda i: (0, 0)),
                  taps, vec, vec, pl.BlockSpec((4, POOL_GW, POOL_GW), lambda i: (0, 0, 0)), vec, vec],
        out_specs=[pl.BlockSpec((tT, 5 * C), lambda i: (i, 0)), pl.BlockSpec((32 * 8, C), lambda i: (0, 0)),
                   pl.BlockSpec((8, C), lambda i: (0, 0)), pl.BlockSpec((4, POOL_GW, POOL_GW), lambda i: (0, 0, 0))],
        out_shape=[jax.ShapeDtypeStruct((T, 5 * C), BF16), jax.ShapeDtypeStruct((32 * 8, C), F32),
                   jax.ShapeDtypeStruct((8, C), F32), jax.ShapeDtypeStruct((4, POOL_GW, POOL_GW), F32)],
        scratch_shapes=[pltpu.VMEM((R1 + 8, C), F32), pltpu.VMEM((8, R1, C), F32), pltpu.VMEM((tT, C), F32),
                        pltpu.VMEM((32, 8, C), F32)],
        compiler_params=_params("arbitrary"))(p, p, p, u1, u1, dout, dout, w_out, after, cwr, lg, lb, pw, pb, sc)


def _softplus(z):
    u = jnp.exp(-jnp.abs(z))
    w = 1.0 + u
    l1p = jnp.where(w == 1.0, u, u * jnp.log(w) / jnp.where(w == 1.0, 1.0, w - 1.0))
    return jnp.maximum(z, 0.0) + l1p


def _lru_gates(xrx, cw_ref, cb_ref, wr_ref, br_ref, wi_ref, bi_ref, lam_ref):
    HL = ODD_HALO
    xc = cb_ref[...] + cw_ref[LRU_CONV_K - 1:LRU_CONV_K, :] * xrx[HL:]
    for k in range(LRU_CONV_K - 1):
        xc = xc + cw_ref[k:k + 1, :] * _shift_down(xrx, LRU_CONV_K - 1 - k)[HL:]
    xcb = xc.astype(BF16)
    rp, ip = [], []
    for hd in range(LRU_HEADS):
        cols = slice(hd * LRU_HD, (hd + 1) * LRU_HD)
        rp.append(jnp.dot(xcb[:, cols], wr_ref[hd], preferred_element_type=F32))
        ip.append(jnp.dot(xcb[:, cols], wi_ref[hd], preferred_element_type=F32))
    r = _sigmoid(jnp.concatenate(rp, axis=1) + br_ref[...])
    ig = _sigmoid(jnp.concatenate(ip, axis=1) + bi_ref[...])
    sp = _softplus(-lam_ref[...])
    log_a = (-LRU_C) * r * sp
    a = jnp.exp(log_a)
    m2 = jnp.maximum(-jnp.tanh(log_a) * (a * a + 1.0), 1e-30)
    inv_mult = lax.rsqrt(m2)
    return xc, xcb, r, ig, sp, a, m2 * inv_mult, inv_mult


def _group_scan(a, b, reverse):
    n, w = a.shape
    a, b = a.reshape(n // 8, 8, w), b.reshape(n // 8, 8, w)
    pos = lax.broadcasted_iota(jnp.int32, (1, 8, 1), 1)
    s = 1
    while s < 8:
        ok = (pos < 8 - s) if reverse else (pos >= s)
        shift = (8 - s) if reverse else s
        a_sh = jnp.where(ok, pltpu.roll(a, shift, 1), 1.0)
        b_sh = jnp.where(ok, pltpu.roll(b, shift, 1), 0.0)
        b = a * b_sh + b
        a = a * a_sh
        s *= 2
    return a.reshape(n, w), b.reshape(n, w)


def _apply_carries(a_ref, b_ref, out_ref, c0, reverse):
    ng = a_ref.shape[0] // 8

    def step(t, c):
        r0 = pl.multiple_of(((ng - 1 - t) if reverse else t) * 8, 8)
        x = a_ref[pl.ds(r0, 8), :] * c + b_ref[pl.ds(r0, 8), :]
        out_ref[pl.ds(r0, 8), :] = x
        return x[0:1, :] if reverse else x[7:8, :]

    return lax.fori_loop(0, ng, step, c0)


def _odd_mixer_fwd(p, cw, cb, wr, br, wi, bi, lam, name):
    T = p.shape[0]
    W = W_LRU
    tT, HL = MIX_TILE, ODD_HALO
    hb = tT // HL

    def body(pm_ref, ph_ref, cw_ref, cb_ref, wr_ref, br_ref, wi_ref, bi_ref, lam_ref, y_ref, hs_ref, carry_ref,
             sa_ref, sb_ref):
        i = pl.program_id(0)
        keep = (i > 0).astype(F32)

        @pl.when(i == 0)
        def _():
            carry_ref[...] = jnp.zeros_like(carry_ref)

        xrx = jnp.concatenate([ph_ref[:, 0:W] * keep, pm_ref[:, 0:W]], axis=0)
        xc, _, _, ig, _, a, mult, _ = _lru_gates(xrx, cw_ref, cb_ref, wr_ref, br_ref, wi_ref, bi_ref, lam_ref)
        sa_ref[...], sb_ref[...] = _group_scan(a, mult * (ig * xc), reverse=False)
        last = _apply_carries(sa_ref, sb_ref, hs_ref, carry_ref[0:1, :], reverse=False)
        carry_ref[...] = jnp.broadcast_to(last, (8, W))
        hs = hs_ref[...]
        gt = pm_ref[:, W:2 * W]
        y_ref[...] = (hs * (gt * _sigmoid(gt))).astype(BF16)

    vec = pl.BlockSpec((1, W), lambda i: (0, 0))
    heads = pl.BlockSpec((LRU_HEADS, LRU_HD, LRU_HD), lambda i: (0, 0, 0))
    return _pallas(
        body, name=name, grid=(T // tT,),
        in_specs=[pl.BlockSpec((tT, 2 * W), lambda i: (i, 0)),
                  pl.BlockSpec((HL, 2 * W), lambda i: (jnp.maximum(i * hb - 1, 0), 0)),
                  pl.BlockSpec((8, W), lambda i: (0, 0)), vec, heads, vec, heads, vec, vec],
        out_specs=[pl.BlockSpec((tT, W), lambda i: (i, 0)), pl.BlockSpec((tT, W), lambda i: (i, 0))],
        out_shape=[jax.ShapeDtypeStruct((T, W), BF16), jax.ShapeDtypeStruct((T, W), F32)],
        scratch_shapes=[pltpu.VMEM((8, W), F32), pltpu.VMEM((tT, W), F32), pltpu.VMEM((tT, W), F32)],
        compiler_params=_params("arbitrary"))(p, p, cw, cb, wr, br, wi, bi, lam)


def _odd_mixer_bwd(p, hs, dout, w_out, after, cw, cb, wr, br, wi, bi, lam, name):
    T = p.shape[0]
    W = W_LRU
    D = dout.shape[1]
    tT, HL = MIX_TILE, ODD_HALO
    hb = tT // HL
    nT = T // tT

    def body(pm_ref, ph_ref, hsm_ref, hsh_ref, do_ref, wo_ref, after_ref, cw_ref, cb_ref, wr_ref, br_ref, wi_ref,
             bi_ref, lam_ref, dp_ref, dwr_ref, dwi_ref, dvec_ref, gcarry_ref, xcarry_ref, sa_ref, sb_ref, g_ref):
        i = pl.program_id(0)
        keep = (i < nT - 1).astype(F32)

        @pl.when(i == 0)
        def _():
            gcarry_ref[...] = jnp.zeros_like(gcarry_ref)
            xcarry_ref[...] = jnp.zeros_like(xcarry_ref)

        xrx = jnp.concatenate([ph_ref[:, 0:W] * keep, pm_ref[:, 0:W]], axis=0)
        xc, xcb, r, ig, sp, a, mult, inv_mult = _lru_gates(xrx, cw_ref, cb_ref, wr_ref, br_ref, wi_ref, bi_ref, lam_ref)
        hs = hsm_ref[...]
        gt = pm_ref[:, W:2 * W]
        sg = _sigmoid(gt)
        dyv = _nt(do_ref[...], wo_ref[...])
        dp_ref[:, W:2 * W] = (dyv * hs * _dsilu(gt, sg)).astype(BF16)
        row = lax.broadcasted_iota(jnp.int32, (tT, 1), 0)
        m = jnp.where(row == tT - 1, 1.0, _shift_up(a, 1))
        sa_ref[...], sb_ref[...] = _group_scan(m, dyv * (gt * sg), reverse=True)
        first = _apply_carries(sa_ref, sb_ref, g_ref, gcarry_ref[0:1, :], reverse=True)
        G = g_ref[...]
        gcarry_ref[...] = jnp.broadcast_to(a[0:1, :] * first, (8, W))
        hs_prev = jnp.where(row == 0, hsh_ref[HL - 1:HL, :] * keep, _shift_down(hs, 1))
        da = G * hs_prev
        dmult = G * (ig * xc)
        di = G * mult * xc
        dxc = G * mult * ig
        dlog_a = da * a - dmult * (a * a) * inv_mult
        drp = dlog_a * ((-LRU_C) * sp) * r * (1.0 - r)
        dip = di * ig * (1.0 - ig)
        dlam = jnp.sum(dlog_a * ((-LRU_C) * r), axis=0, keepdims=True) * (-_sigmoid(-lam_ref[...]))
        drb, dib = drp.astype(BF16), dip.astype(BF16)
        back = []
        for hd in range(LRU_HEADS):
            cols = slice(hd * LRU_HD, (hd + 1) * LRU_HD)
            back.append(_nt(drb[:, cols], wr_ref[hd]) + _nt(dib[:, cols], wi_ref[hd]))
            dwr_h = _tn(xcb[:, cols], drb[:, cols])
            dwi_h = _tn(xcb[:, cols], dib[:, cols])

            @pl.when(i == 0)
            def _(hd=hd, dwr_h=dwr_h, dwi_h=dwi_h):
                dwr_ref[hd] = dwr_h
                dwi_ref[hd] = dwi_h

            @pl.when(i > 0)
            def _(hd=hd, dwr_h=dwr_h, dwi_h=dwi_h):
                dwr_ref[hd] += dwr_h
                dwi_ref[hd] += dwi_h

        dxc = dxc + jnp.concatenate(back, axis=1)
        dxcx = jnp.concatenate([dxc, xcarry_ref[...]], axis=0)
        dxr = cw_ref[LRU_CONV_K - 1:LRU_CONV_K, :] * dxc
        rows = []
        for k in range(LRU_CONV_K - 1):
            j = LRU_CONV_K - 1 - k
            dxr = dxr + cw_ref[k:k + 1, :] * _shift_up(dxcx, j)[0:tT]
            rows.append(jnp.sum(dxc * _shift_down(xrx, j)[HL:], axis=0, keepdims=True))
        rows.append(jnp.sum(dxc * xrx[HL:], axis=0, keepdims=True))
        dp_ref[:, 0:W] = dxr.astype(BF16)
        xcarry_ref[...] = dxc[0:8]
        rows += [jnp.sum(dxc, axis=0, keepdims=True), jnp.sum(drp, axis=0, keepdims=True),
                 jnp.sum(dip, axis=0, keepdims=True), dlam]
        vecs = jnp.concatenate(rows, axis=0)

        @pl.when(i == 0)
        def _():
            dvec_ref[...] = vecs

        @pl.when(i > 0)
        def _():
            dvec_ref[...] += vecs

    vec = pl.BlockSpec((1, W), lambda i: (0, 0))
    heads = pl.BlockSpec((LRU_HEADS, LRU_HD, LRU_HD), lambda i: (0, 0, 0))

    def tile(i):
        return (nT - 1 - i, 0)

    def prev_blk(i):
        return (jnp.maximum((nT - 1 - i) * hb - 1, 0), 0)

    return _pallas(
        body, name=name, grid=(nT,),
        in_specs=[pl.BlockSpec((tT, 2 * W), tile), pl.BlockSpec((HL, 2 * W), prev_blk),
                  pl.BlockSpec((tT, W), tile), pl.BlockSpec((HL, W), prev_blk), pl.BlockSpec((tT, D), tile),
                  pl.BlockSpec((None, W, D), lambda i: (0, 0, 0)), pl.BlockSpec((8, 128), lambda i: (0, 0)),
                  pl.BlockSpec((8, W), lambda i: (0, 0)), vec, heads, vec, heads, vec, vec],
        out_specs=[pl.BlockSpec((tT, 2 * W), tile), heads, heads, pl.BlockSpec((8, W), lambda i: (0, 0))],
        out_shape=[jax.ShapeDtypeStruct((T, 2 * W), BF16), jax.ShapeDtypeStruct((LRU_HEADS, LRU_HD, LRU_HD), F32),
                   jax.ShapeDtypeStruct((LRU_HEADS, LRU_HD, LRU_HD), F32), jax.ShapeDtypeStruct((8, W), F32)],
        scratch_shapes=[pltpu.VMEM((8, W), F32), pltpu.VMEM((8, W), F32), pltpu.VMEM((tT, W), F32),
                        pltpu.VMEM((tT, W), F32), pltpu.VMEM((tT, W), F32)],
        compiler_params=_params("arbitrary"))(p, p, hs, hs, dout, w_out, after, cw, cb, wr, br, wi, bi, lam)


def _pad_rows(a, rows):
    return jnp.concatenate([a, jnp.zeros((rows - a.shape[0], a.shape[1]), a.dtype)], axis=0)


def _layer_fwd(even, h, w, w_in, w_out, after):
    if even:
        p, n = _in_proj(h, w["norm"], w_in, 0, after, "in_proj_even")
        y, aux = _even_mixer_fwd(p, w["conv_w"], w["conv_b"], w["ln_g"], w["ln_b"], w["pool_w"], w["pool_b"],
                                 w["pool_scale"], "even_mixer_fwd")
    else:
        p, n = _in_proj(h, w["norm"], w_in, 0, after, "in_proj_odd")
        y, aux = _odd_mixer_fwd(p, w["conv_w"], w["conv_b"], w["w_rg"], w["b_rg"], w["w_ig"], w["b_ig"], w["lam"],
                                "odd_mixer_fwd")
    if callable(w_out):
        w_out = w_out(y)
    h_next = _out_proj(y, w_out, 0, h, "out_proj_even" if even else "out_proj_odd")
    return h_next, (h, n, p, aux, y), w_out


def _layer_bwd_weights(even, saved, w, w_out, dhb, after):
    h, n, p, aux, y = saved
    if even:
        dp, dcw, dvec, dpw = _even_mixer_bwd(p, aux, dhb, w_out, after, w["conv_w_rev"], w["ln_g"], w["ln_b"],
                                             w["pool_w"], w["pool_b"], w["pool_scale"], "even_mixer_bwd")
        dw_out = _dw_out(y, dhb, 0, 1, None, "dw_out_even")
        dw_in = _dw_in(n, dp, N_CHIPS, 0, 1, None, "dw_in_even")
        return dp, dw_in, dw_out, dict(conv_w=dcw, vec=dvec, pool_w=dpw)
    dp, dwr, dwi, dvec = _odd_mixer_bwd(p, aux, dhb, w_out, after, w["conv_w"], w["conv_b"], w["w_rg"], w["b_rg"],
                                        w["w_ig"], w["b_ig"], w["lam"], "odd_mixer_bwd")
    dw_out = _dw_out(y, dhb, 0, 1, None, "dw_out_odd")
    dw_in = _dw_in(n, dp, N_CHIPS, 0, 1, None, "dw_in_odd")
    return dp, dw_in, dw_out, dict(w_rg=dwr, w_ig=dwi, vec=dvec)


def _layer_bwd_input(even, saved, w, w_in, dp, dh, after):
    return _dn_proj(dp, w_in, 0, saved[0], w["norm"], dh, after, "dn_proj_even" if even else "dn_proj_odd")


ANY = pl.BlockSpec(memory_space=pl.ANY)


def _mesh_pos():
    return lax.axis_index("x"), lax.axis_index("y"), lax.axis_index("c")


def _other_chips(x, y):
    return [(1 - x, y), (x, 1 - y), (1 - x, 1 - y)]


def _other_devices(x, y, c):
    out = []
    for p in range(1, N_DEV):
        out.append((1 - x if p & 4 else x, 1 - y if p & 2 else y, 1 - c if p & 1 else c))
    return out


def _remote(src, dst, ssem, rsem, dev):
    return pltpu.make_async_remote_copy(src_ref=src, dst_ref=dst, send_sem=ssem, recv_sem=rsem, device_id=dev,
                                        device_id_type=MESH)


def _comm_call(body, name, ins, out_shape, scratch, aliases=None):
    return _pallas(body, name=name, in_specs=[ANY] * len(ins), out_specs=[ANY] * len(out_shape), out_shape=out_shape,
                   scratch_shapes=scratch, input_output_aliases=aliases or {},
                   compiler_params=pltpu.CompilerParams(has_side_effects=True))(*ins)


def _cast_shard(w, layer, pos):
    _, R, C = w.shape
    tr = _row_tile(R, C)

    def body(pos_ref, w_ref, o_ref):
        o_ref[...] = w_ref[...].astype(BF16)

    grid_spec = pltpu.PrefetchScalarGridSpec(
        num_scalar_prefetch=1, grid=(R // tr,),
        in_specs=[pl.BlockSpec((None, tr, C), lambda i, pr: (layer, i, 0))],
        out_specs=pl.BlockSpec((None, None, tr, C), lambda i, pr: (0, pr[0], i, 0)))
    return _pallas(body, name="cast_shard", grid_spec=grid_spec,
                   out_shape=jax.ShapeDtypeStruct((1, N_CHIPS, R, C), BF16),
                   compiler_params=_params("parallel"))(pos, w)


def _gather_weights(big, small):
    nA = len(big)
    half = [a.shape[2] // 2 for a in big]

    def body(*refs):
        ins, outs = refs[:nA + 1], refs[nA + 1:2 * nA + 2]
        ssem, rsem, fsem, frsem, lsem = refs[2 * nA + 2:]
        x, y, c = _mesh_pos()
        k = 2 * x + y
        chips = _other_chips(x, y)
        sib = (x, y, 1 - c)

        def slab(a, chip, core):
            return outs[a].at[:, chip, pl.ds(core * half[a], half[a]), :]

        local = [pltpu.make_async_copy(ins[nA], outs[nA].at[k], lsem.at[0])]
        for cp in local:
            cp.start()
        sends = []
        for j, (ox, oy) in enumerate(chips):
            for a in range(nA):
                sends.append(_remote(slab(a, k, c), slab(a, k, c), ssem.at[a, j], rsem.at[a, j], (ox, oy, c)))
            sends.append(_remote(ins[nA], outs[nA].at[k], ssem.at[nA, j], rsem.at[nA, j], (ox, oy, c)))
        for cp in sends:
            cp.start()
        for j, (ox, oy) in enumerate(chips):
            kj = 2 * ox + oy
            for a in range(nA):
                got = slab(a, kj, c)
                _remote(got, got, ssem.at[a, j], rsem.at[a, j], (ox, oy, c)).wait_recv()
                fw = _remote(got, got, fsem.at[a, j], frsem.at[a, j], sib)
                fw.start()
                sends.append(fw)
            gs = outs[nA].at[kj]
            _remote(gs, gs, ssem.at[nA, j], rsem.at[nA, j], (ox, oy, c)).wait_recv()
        for j, (ox, oy) in enumerate(chips):
            kj = 2 * ox + oy
            for a in range(nA):
                theirs = slab(a, kj, 1 - c)
                _remote(theirs, theirs, fsem.at[a, j], frsem.at[a, j], sib).wait_recv()
        for cp in sends:
            cp.wait_send()
        for cp in local:
            cp.wait()

    out_shape = [jax.ShapeDtypeStruct(a.shape, a.dtype) for a in big]
    out_shape.append(jax.ShapeDtypeStruct((N_CHIPS,) + small.shape, small.dtype))
    scratch = [pltpu.SemaphoreType.DMA((nA + 1, 3)), pltpu.SemaphoreType.DMA((nA + 1, 3)),
               pltpu.SemaphoreType.DMA((nA, 3)), pltpu.SemaphoreType.DMA((nA, 3)), pltpu.SemaphoreType.DMA((1,))]
    return _comm_call(body, "gather_weights", list(big) + [small], out_shape, scratch, {a: a for a in range(nA)})


HBM = pl.BlockSpec(memory_space=pltpu.HBM)
SEM = pl.BlockSpec(memory_space=pltpu.SEMAPHORE)
EFFECT = pltpu.SideEffectType.DATAFLOW_SIDE_EFFECTING


def _split_start(arrays, copies, n, name):
    k = len(arrays)

    def body(*refs):
        for cp in copies(refs[k + 2:2 * k + 2], refs[k], refs[k + 1]):
            cp.start()
        refs[2 * k + 2][...] = jnp.zeros((8, 128), F32)

    out = _pallas(
        body, name=name,
        out_shape=(pltpu.SemaphoreType.DMA((n,)), pltpu.SemaphoreType.DMA((n,)),
                   *[pltpu.HBM(a.shape, a.dtype) for a in arrays], jax.ShapeDtypeStruct((8, 128), F32)),
        in_specs=(HBM,) * k, out_specs=(SEM, SEM) + (HBM,) * k + (pl.BlockSpec(memory_space=pltpu.VMEM),),
        input_output_aliases={i: i + 2 for i in range(k)},
        compiler_params=pltpu.CompilerParams(has_side_effects=EFFECT),
    )(*[pltpu.with_memory_space_constraint(a, pltpu.HBM) for a in arrays])
    return out[0], out[1], list(out[2:2 + k]), out[2 + k]


def _split_wait(ssem, rsem, arrays, copies, after, name):
    k = len(arrays)

    def body(*refs):
        for cp in copies(refs[:k], refs[k], refs[k + 1]):
            cp.wait_send()
            cp.wait_recv()

    out = _pallas(
        body, name=name, out_shape=tuple(pltpu.HBM(a.shape, a.dtype) for a in arrays),
        in_specs=(HBM,) * k + (SEM, SEM, ANY), out_specs=(HBM,) * k, input_output_aliases={i: i for i in range(k)},
        compiler_params=pltpu.CompilerParams(has_side_effects=EFFECT),
    )(*arrays, ssem, rsem, after)
    return list(out)


def _gather_copies(shapes):
    half = [s[2] // 2 for s in shapes]

    def copies(refs, ssem, rsem):
        x, y, c = _mesh_pos()
        out = []
        for j, (ox, oy) in enumerate(_other_chips(x, y)):
            for a, ref in enumerate(refs):
                slab = ref.at[:, 2 * x + y, pl.ds(c * half[a], half[a]), :]
                out.append(_remote(slab, slab, ssem.at[3 * a + j], rsem.at[3 * a + j], (ox, oy, c)))
        return out

    return copies


def _chips_copies(n_arr):
    def copies(refs, ssem, rsem):
        x, y, c = _mesh_pos()
        out = []
        for j, (ox, oy) in enumerate(_other_chips(x, y)):
            for a in range(n_arr):
                out.append(_remote(refs[a].at[:, 2 * ox + oy], refs[n_arr + a].at[:, 2 * x + y], ssem.at[3 * a + j],
                                   rsem.at[3 * a + j], (ox, oy, c)))
        return out

    return copies


def _halves_copies(shapes):
    n = len(shapes)
    half = [s[2] // 2 for s in shapes]

    def copies(refs, ssem, rsem):
        x, y, c = _mesh_pos()
        return [_remote(refs[a].at[:, :, pl.ds((1 - c) * half[a], half[a]), :], refs[n + a], ssem.at[a], rsem.at[a],
                        (x, y, 1 - c)) for a in range(n)]

    return copies


def _forward_cores(arrays):
    nA = len(arrays)
    half = [a.shape[2] // 2 for a in arrays]

    def body(*refs):
        outs = refs[nA:2 * nA]
        ssem, rsem = refs[2 * nA:]
        x, y, c = _mesh_pos()
        sib = (x, y, 1 - c)
        sends, waits = [], []
        for j, (ox, oy) in enumerate(_other_chips(x, y)):
            for a in range(nA):
                got = outs[a].at[:, 2 * ox + oy, pl.ds(c * half[a], half[a]), :]
                sends.append(_remote(got, got, ssem.at[a, j], rsem.at[a, j], sib))
                theirs = outs[a].at[:, 2 * ox + oy, pl.ds((1 - c) * half[a], half[a]), :]
                waits.append(_remote(theirs, theirs, ssem.at[a, j], rsem.at[a, j], sib))
        for cp in sends:
            cp.start()
        for cp in waits:
            cp.wait_recv()
        for cp in sends:
            cp.wait_send()

    out_shape = [jax.ShapeDtypeStruct(a.shape, a.dtype) for a in arrays]
    scratch = [pltpu.SemaphoreType.DMA((nA, 3)), pltpu.SemaphoreType.DMA((nA, 3))]
    return _comm_call(body, "forward_cores", list(arrays), out_shape, scratch, {a: a for a in range(nA)})


def _exchange_halves(big):
    nA = len(big)
    half = [a.shape[2] // 2 for a in big]

    def body(*refs):
        ins, outs = refs[:nA], refs[nA:2 * nA]
        ssem, rsem = refs[2 * nA:]
        x, y, c = _mesh_pos()
        sib = (x, y, 1 - c)
        sends = [_remote(ins[a].at[:, :, pl.ds((1 - c) * half[a], half[a]), :], outs[a], ssem.at[a], rsem.at[a], sib)
                 for a in range(nA)]
        for cp in sends:
            cp.start()
        for a in range(nA):
            _remote(outs[a], outs[a], ssem.at[a], rsem.at[a], sib).wait_recv()
        for cp in sends:
            cp.wait_send()

    out_shape = [jax.ShapeDtypeStruct((a.shape[0], N_CHIPS, h, a.shape[3]), a.dtype) for a, h in zip(big, half)]
    scratch = [pltpu.SemaphoreType.DMA((nA,)), pltpu.SemaphoreType.DMA((nA,))]
    return _comm_call(body, "exchange_halves", list(big), out_shape, scratch)


def _exchange_final(grads, everywhere, small):
    nA = len(grads)
    n_remote = sum(N_DEV - 1 if ev else 1 for ev in everywhere) + N_DEV - 1

    def body(*refs):
        small_ref, outs, gathered = refs[nA], refs[nA + 1:2 * nA + 1], refs[2 * nA + 1]
        ssem, rsem, lsem = refs[2 * nA + 2:]
        x, y, c = _mesh_pos()
        k = 2 * x + y
        me = 2 * k + c
        sib = (x, y, 1 - c)
        peers = _other_devices(x, y, c)
        local = pltpu.make_async_copy(small_ref, gathered.at[me], lsem.at[0])
        local.start()
        sends, waits = [], []
        s = 0
        for (px, py, pc) in peers:
            sends.append(_remote(small_ref, gathered.at[me], ssem.at[s], rsem.at[s], (px, py, pc)))
            got = gathered.at[4 * px + 2 * py + pc]
            waits.append(_remote(got, got, ssem.at[s], rsem.at[s], (px, py, pc)))
            s += 1
        for a in range(nA):
            if everywhere[a]:
                r2 = grads[a].shape[1] // N_DEV
                mine = outs[a].at[:, pl.ds((2 * k + c) * r2, r2), :]
                for (px, py, pc) in peers:
                    sends.append(_remote(mine, mine, ssem.at[s], rsem.at[s], (px, py, pc)))
                    got = outs[a].at[:, pl.ds((2 * (2 * px + py) + pc) * r2, r2), :]
                    waits.append(_remote(got, got, ssem.at[s], rsem.at[s], (px, py, pc)))
                    s += 1
            else:
                r2 = grads[a].shape[1] // 2
                mine = outs[a].at[:, pl.ds(c * r2, r2), :]
                sends.append(_remote(mine, mine, ssem.at[s], rsem.at[s], sib))
                got = outs[a].at[:, pl.ds((1 - c) * r2, r2), :]
                waits.append(_remote(got, got, ssem.at[s], rsem.at[s], sib))
                s += 1
        for cp in sends:
            cp.start()
        for cp in waits:
            cp.wait_recv()
        for cp in sends:
            cp.wait_send()
        local.wait()

    out_shape = [jax.ShapeDtypeStruct(g.shape, g.dtype) for g in grads]
    out_shape.append(jax.ShapeDtypeStruct((N_DEV,) + small.shape, small.dtype))
    scratch = [pltpu.SemaphoreType.DMA((n_remote,)), pltpu.SemaphoreType.DMA((n_remote,)), pltpu.SemaphoreType.DMA((1,))]
    return _comm_call(body, "exchange_final", list(grads) + [small], out_shape, scratch, {a: a for a in range(nA)})


BLOCK_BYTES = 4 << 20


def _row_tile(rows, cols, mult=16, limit=BLOCK_BYTES):
    best = mult
    for t in range(mult, rows + 1, mult):
        if rows % t == 0 and t * cols * 4 <= limit:
            best = t
    return best


def _add_cores(own, recv, pos):
    L, _, R, C = own.shape
    r2 = R // 2
    tr = _row_tile(r2, C)
    nb = r2 // tr

    def body(pos_ref, a_ref, r_ref, o_ref):
        o_ref[...] = (a_ref[...].astype(F32) + r_ref[...].astype(F32)).astype(BF16)

    blk = (None, None, tr, C)
    grid_spec = pltpu.PrefetchScalarGridSpec(
        num_scalar_prefetch=1, grid=(L, N_CHIPS, nb),
        in_specs=[pl.BlockSpec(blk, lambda l, s, i, pr: (l, s, pr[1] * nb + i, 0)),
                  pl.BlockSpec(blk, lambda l, s, i, pr: (l, s, i, 0))],
        out_specs=pl.BlockSpec(blk, lambda l, s, i, pr: (l, s, i, 0)))
    return _pallas(body, name="add_cores", grid_spec=grid_spec,
                   out_shape=jax.ShapeDtypeStruct((L, N_CHIPS, r2, C), BF16),
                   compiler_params=_params("parallel", "parallel", "parallel"))(pos, own, recv)


def _sum_chips(own, recv, pos, everywhere, layer, nlayers, prev):
    _, _, r2, C = own.shape
    tr = _row_tile(r2, 2 * C)
    nb = r2 // tr

    def body(pos_ref, a_ref, r_ref, *rest):
        acc = None
        for s in range(N_CHIPS):
            term = jnp.where(pos_ref[0] == s, a_ref[...], r_ref[s]).astype(F32)
            acc = term if acc is None else acc + term
        rest[-1][...] = acc

    if everywhere:
        def out_map(i, pr):
            return (layer, (2 * pr[0] + pr[1]) * nb + i, 0)
    else:
        def out_map(i, pr):
            return (layer, pr[1] * nb + i, 0)

    in_specs = [pl.BlockSpec((None, None, tr, C), lambda i, pr: (0, pr[0], i, 0)),
                pl.BlockSpec((None, N_CHIPS, tr, C), lambda i, pr: (0, 0, i, 0))]
    grid_spec = pltpu.PrefetchScalarGridSpec(
        num_scalar_prefetch=1, grid=(nb,), in_specs=in_specs + ([] if prev is None else [ANY]),
        out_specs=pl.BlockSpec((None, tr, C), out_map))
    rows = (N_DEV if everywhere else 2) * r2
    args = (pos, own, recv) if prev is None else (pos, own, recv, prev)
    return _pallas(body, name="sum_chips", grid_spec=grid_spec, out_shape=jax.ShapeDtypeStruct((nlayers, rows, C), F32),
                   input_output_aliases={} if prev is None else {3: 0},
                   compiler_params=_params("parallel"))(*args)


def _sum_devices(parts):
    n, R, C = parts.shape
    tr = _row_tile(R, C * n, 8)

    def body(p_ref, o_ref):
        acc = p_ref[0]
        for s in range(1, n):
            acc = acc + p_ref[s]
        o_ref[...] = acc

    return _pallas(body, name="sum_devices", grid=(R // tr,), in_specs=[pl.BlockSpec((n, tr, C), lambda i: (0, i, 0))],
                   out_specs=pl.BlockSpec((tr, C), lambda i: (i, 0)), out_shape=jax.ShapeDtypeStruct((R, C), F32),
                   compiler_params=_params("parallel"))(parts)


def _adamw(w, g, m, v, name):
    L, R, C = w.shape
    tr = _row_tile(R, C, 8, BLOCK_BYTES // 2)

    def body(w_ref, g_ref, m_ref, v_ref, d_ref, m2_ref, v2_ref):
        gg = g_ref[...]
        m2 = ADAM_B1 * m_ref[...] + (1.0 - ADAM_B1) * gg
        v2 = ADAM_B2 * v_ref[...] + (1.0 - ADAM_B2) * (gg * gg)
        m_hat = m2 / (1.0 - ADAM_B1 ** ADAM_STEP)
        v_hat = v2 / (1.0 - ADAM_B2 ** ADAM_STEP)
        d_ref[...] = -ADAM_LR * (m_hat / (jnp.sqrt(v_hat) + ADAM_EPS) + ADAM_WD * w_ref[...])
        m2_ref[...] = m2
        v2_ref[...] = v2

    blk = pl.BlockSpec((1, tr, C), lambda l, i: (l, i, 0))
    shp = jax.ShapeDtypeStruct((L, R, C), F32)
    return _pallas(body, name=name, grid=(L, R // tr), in_specs=[blk] * 4, out_specs=[blk] * 3, out_shape=[shp] * 3,
                   compiler_params=_params("parallel", "parallel"))(w, g, m, v)


WEIGHTS = ("norm_even", "w_in_even", "conv_a_w", "conv_a_b", "ln_a_g", "ln_a_b", "pool_w", "pool_b", "pool_scale",
           "w_out_even", "norm_odd", "w_in_odd", "conv_c_w", "conv_c_b", "w_rg", "b_rg", "w_ig", "b_ig", "lru_lambda",
           "w_out_odd", "final_norm")
BIG = ("w_in_even", "w_out_even", "pool_w", "w_in_odd", "w_out_odd", "w_rg", "w_ig")
SMALL = tuple(n for n in WEIGHTS if n not in BIG)
SMALL_SHARDED = ("conv_a_w", "pool_b", "norm_odd", "conv_c_w", "conv_c_b", "b_rg", "b_ig", "lru_lambda")


def _pack(arrs):
    flat = jnp.concatenate([a.reshape(-1) for a in arrs])
    rows = -(-flat.shape[0] // (64 * 128)) * 64
    return jnp.pad(flat, (0, rows * 128 - flat.shape[0])).reshape(rows, 128)


def _unpack(buf, shapes, lead=()):
    flat = buf.reshape(tuple(lead) + (-1,))
    out, o = [], 0
    for s in shapes:
        n = 1
        for d in s:
            n *= d
        out.append(flat[..., o:o + n].reshape(tuple(lead) + tuple(s)))
        o += n
    return out


def _shard(full, axis, k):
    n = full.shape[axis] // N_CHIPS
    return lax.dynamic_slice_in_dim(full, k * n, n, axis)


def kernel(x, norm_even, w_in_even, conv_a_w, conv_a_b, ln_a_g, ln_a_b, pool_w, pool_b, pool_scale, w_out_even, norm_odd, w_in_odd, conv_c_w, conv_c_b, w_rg, b_rg, w_ig, b_ig, lru_lambda, w_out_odd, final_norm, loss_target, m_norm_even, m_w_in_even, m_conv_a_w, m_conv_a_b, m_ln_a_g, m_ln_a_b, m_pool_w, m_pool_b, m_pool_scale, m_w_out_even, m_norm_odd, m_w_in_odd, m_conv_c_w, m_conv_c_b, m_w_rg, m_b_rg, m_w_ig, m_b_ig, m_lru_lambda, m_w_out_odd, m_final_norm, v_norm_even, v_w_in_even, v_conv_a_w, v_conv_a_b, v_ln_a_g, v_ln_a_b, v_pool_w, v_pool_b, v_pool_scale, v_w_out_even, v_norm_odd, v_w_in_odd, v_conv_c_w, v_conv_c_b, v_w_rg, v_b_rg, v_w_ig, v_b_ig, v_lru_lambda, v_w_out_odd, v_final_norm):
    P = dict(locals())
    xi, yi, ci = _mesh_pos()
    k = 2 * xi + yi
    L = w_in_even.shape[0]
    D = D_MODEL

    pos = jnp.stack([k, ci]).astype(jnp.int32)
    depth = 2 * L
    pool_w3 = pool_w.reshape(L, 4 * 64, POOL_GW)

    def cast_group(layer):
        j = layer // 2
        if layer % 2 == 0:
            return [_cast_shard(w_in_even, j, pos), _cast_shard(w_out_even, j, pos), _cast_shard(pool_w3, j, pos)]
        return [_cast_shard(w_in_odd, j, pos), _cast_shard(w_out_odd, j, pos)]

    first = cast_group(0)
    g_in, g_pool, g_small = _gather_weights([first[0], first[2]], _pack([P[n] for n in SMALL_SHARDED]))
    copies0 = _gather_copies([first[1].shape])
    ssem0, rsem0, late, token0 = _split_start([first[1]], copies0, 3, "gather_start0")

    def late_w_out(y):
        return _forward_cores(_split_wait(ssem0, rsem0, late, copies0, y, "gather_wait0"))[0].reshape(1, -1, D)

    group = [g_in, late_w_out, g_pool]
    full = {}
    for n, a in zip(SMALL_SHARDED, _unpack(g_small, [P[n].shape for n in SMALL_SHARDED], lead=(N_CHIPS,))):
        a = jnp.moveaxis(a, 0, -2)
        full[n] = a.reshape(a.shape[:-2] + (N_CHIPS * a.shape[-1],))

    def small_weights(layer, group):
        j = layer // 2
        if layer % 2 == 0:
            cw = full["conv_a_w"][j]
            pw = group[2].reshape(N_CHIPS, 4, 64, POOL_GW).transpose(1, 0, 2, 3).reshape(4, POOL_GW, POOL_GW)
            return dict(norm=norm_even[j][None], conv_w=_pad_rows(cw, 32), conv_w_rev=_pad_rows(cw[::-1], 32),
                        conv_b=conv_a_b[j][None], ln_g=ln_a_g[j][None], ln_b=ln_a_b[j][None], pool_w=pw,
                        pool_b=full["pool_b"][j].reshape(1, D), pool_scale=pool_scale[j][None])
        return dict(norm=full["norm_odd"][j][None], conv_w=_pad_rows(full["conv_c_w"][j], 8),
                    conv_b=full["conv_c_b"][j][None], w_rg=w_rg[j].astype(BF16), b_rg=full["b_rg"][j][None],
                    w_ig=w_ig[j].astype(BF16), b_ig=full["b_ig"][j][None], lam=full["lru_lambda"][j][None])

    no_token = jnp.zeros((8, 128), F32)
    h = x[0]
    saved, big_w, small_w = [], [], []
    for layer in range(depth):
        token = token0 if layer == 0 else no_token
        if layer + 1 < depth:
            nxt = cast_group(layer + 1)
            copies = _gather_copies([a.shape for a in nxt])
            ssem, rsem, nxt, token = _split_start(nxt, copies, 3 * len(nxt), "gather_start%d" % (layer + 1))
        small_w.append(small_weights(layer, group))
        w_out = group[1] if callable(group[1]) else group[1].reshape(1, -1, D)
        h, sv, w_out = _layer_fwd(layer % 2 == 0, h, small_w[layer], group[0], w_out, token)
        big_w.append((group[0], w_out))
        saved.append(sv)
        if layer + 1 < depth:
            group = _forward_cores(_split_wait(ssem, rsem, nxt, copies, h, "gather_wait%d" % (layer + 1)))

    dh, dhb, d_final, loss = _loss_head(h, final_norm[None], loss_target[0])
    loss = lax.psum(loss[0, 0], ("x", "y", "c"))
    everywhere = [False, False, False, False, False, True, True]
    final = [None] * len(everywhere)
    small_of = [None] * depth

    def finish(pending, after):
        ssem, rsem, arrs, copies, slots, pj, pl_ = pending
        arrs = _split_wait(ssem, rsem, arrs, copies, after, "chips_wait%d" % pl_)
        for a, r, s in zip(arrs[:len(slots)], arrs[len(slots):], slots):
            final[s] = _sum_chips(a, r, pos, everywhere[s], pj, L, final[s])

    pending = None
    token = no_token
    for layer in reversed(range(depth)):
        j = layer // 2
        even_layer = layer % 2 == 0
        dp, dw_in, dw_out, sm = _layer_bwd_weights(even_layer, saved[layer], small_w[layer], big_w[layer][1], dhb, token)
        if even_layer:
            dpw = sm["pool_w"].reshape(4, N_CHIPS, 64, POOL_GW).transpose(1, 0, 2, 3)
            parts = [dw_in, dw_out.reshape(1, N_CHIPS, -1, D), dpw.reshape(1, N_CHIPS, 4 * 64, POOL_GW).astype(BF16)]
            slots = [0, 1, 2]
        else:
            parts = [dw_in, dw_out.reshape(1, N_CHIPS, -1, D),
                     sm["w_rg"].reshape(1, N_CHIPS, -1, LRU_HD).astype(BF16),
                     sm["w_ig"].reshape(1, N_CHIPS, -1, LRU_HD).astype(BF16)]
            slots = [3, 4, 5, 6]
        n = len(parts)
        if layer > 0:
            hcopies = _halves_copies([a.shape for a in parts])
            hland = [lax.empty((1, N_CHIPS, a.shape[2] // 2, a.shape[3]), a.dtype) for a in parts]
            hs, hr, harrs, htoken = _split_start(parts + hland, hcopies, n, "halves_start%d" % layer)
            dh, dhb, sm["norm"] = _layer_bwd_input(even_layer, saved[layer], small_w[layer], big_w[layer][0], dp, dh,
                                                   htoken)
            harrs = _split_wait(hs, hr, harrs, hcopies, dh, "halves_wait%d" % layer)
            parts, recv = harrs[:n], harrs[n:]
        else:
            recv = _exchange_halves(parts)
        pair = [_add_cores(a, r, pos) for a, r in zip(parts, recv)]
        copies = _chips_copies(n)
        land = [lax.empty(a.shape, a.dtype) for a in pair]
        ssem, rsem, arrs, token = _split_start(pair + land, copies, 3 * n, "chips_start%d" % layer)
        if layer == 0:
            dh, dhb, sm["norm"] = _layer_bwd_input(even_layer, saved[layer], small_w[layer], big_w[layer][0], dp, dh, token)
        small_of[layer] = sm
        if pending is not None:
            finish(pending, dh)
        pending = (ssem, rsem, arrs, copies, slots, j, layer)
    grad_x = dh
    small_g = []
    for jj in range(L):
        ge, go = small_of[2 * jj], small_of[2 * jj + 1]
        small_g += [ge["conv_w"].reshape(32, 8, D).sum(axis=1)[:CONV_K], ge["vec"][0:5], ge["norm"], go["vec"], go["norm"]]
    small_g.append(d_final)
    small_shapes = [a.shape for a in small_g]
    packed_small = _pack(small_g)
    finish(pending, packed_small)
    *gw, recv_small = _exchange_final(final, everywhere, packed_small)
    sg = _unpack(_sum_devices(recv_small), small_shapes)

    grads = dict(w_in_even=gw[0], w_out_even=gw[1], pool_w=gw[2].reshape(pool_w.shape), w_in_odd=gw[3], w_out_odd=gw[4],
                 w_rg=gw[5].reshape(w_rg.shape), w_ig=gw[6].reshape(w_ig.shape), final_norm=sg[-1][0])
    ev = [sg[5 * j + 1] for j in range(L)]
    ov = [sg[5 * j + 3] for j in range(L)]
    grads["conv_a_w"] = _shard(jnp.stack([sg[5 * j] for j in range(L)]), 2, k)
    grads["norm_even"] = jnp.stack([sg[5 * j + 2][0] for j in range(L)])
    grads["norm_odd"] = _shard(jnp.stack([sg[5 * j + 4][0] for j in range(L)]), 1, k)
    for r, n in enumerate(("conv_a_b", "ln_a_g", "ln_a_b", "pool_scale")):
        grads[n] = jnp.stack([e[r] for e in ev])
    grads["pool_b"] = _shard(jnp.stack([e[4].reshape(4, POOL_GW) for e in ev]), 2, k)
    grads["conv_c_w"] = _shard(jnp.stack([o[0:4] for o in ov]), 2, k)
    for r, n in zip((4, 5, 6, 7), ("conv_c_b", "b_rg", "b_ig", "lru_lambda")):
        grads[n] = _shard(jnp.stack([o[r] for o in ov]), 1, k)

    delta, new_m, new_v = {}, {}, {}
    for n in BIG:
        s3 = (L, -1, P[n].shape[-1])
        d, m2, v2 = _adamw(P[n].reshape(s3), grads[n].reshape(s3), P["m_" + n].reshape(s3), P["v_" + n].reshape(s3), "adamw")
        delta[n], new_m[n], new_v[n] = d.reshape(P[n].shape), m2.reshape(P[n].shape), v2.reshape(P[n].shape)
    shapes = [P[n].shape for n in SMALL]
    packed = [_pack([src[n] for n in SMALL])[None] for src in
              (P, grads, {n: P["m_" + n] for n in SMALL}, {n: P["v_" + n] for n in SMALL})]
    for res, out in zip(_adamw(*packed, "adamw_small"), (delta, new_m, new_v)):
        for n, a in zip(SMALL, _unpack(res[0], shapes)):
            out[n] = a

    return (loss, grad_x[None], *[grads[n] for n in WEIGHTS], *[delta[n] for n in WEIGHTS],
            *[new_m[n] for n in WEIGHTS], *[new_v[n] for n in WEIGHTS])
```

```python
import functools

import jax
import jax.numpy as jnp
from jax import lax
from jax.experimental import pallas as pl
from jax.experimental.pallas import tpu as pltpu

F32 = jnp.float32
BF16 = jnp.bfloat16
MESH = pl.DeviceIdType.MESH

D_MODEL = 1024
N_CHIPS = 4
N_DEV = 8
EPS_RMS = 1e-6
EPS_LN = 1e-5
CONV_K = 31
POOL_WINDOWS = (2, 4, 8, 16)
POOL_GW = 256
LRU_HEADS = 12
LRU_HD = 128
W_LRU = LRU_HEADS * LRU_HD
LRU_CONV_K = 4
LRU_C = 8.0
ADAM_LR = 0.001
ADAM_B1 = 0.9
ADAM_B2 = 0.999
ADAM_EPS = 1e-08
ADAM_WD = 0.01
ADAM_STEP = 10

VMEM_LIMIT_BYTES = 56 * 1024 * 1024
ROW_TILE = 512
MIX_TILE = 256
EVEN_HALO = 32
ODD_HALO = 8


def _pallas(body, **kw):
    return pl.pallas_call(body, **kw)


def _params(*sem):
    return pltpu.CompilerParams(dimension_semantics=sem if sem else None, vmem_limit_bytes=VMEM_LIMIT_BYTES)


def _sigmoid(x):
    return 0.5 * jnp.tanh(0.5 * x) + 0.5


def _dsilu(x, s):
    return s * (1.0 + x * (1.0 - s))


def _nt(a, b):
    return lax.dot_general(a, b, (((1,), (1,)), ((), ())), preferred_element_type=F32)


def _tn(a, b):
    return lax.dot_general(a, b, (((0,), (0,)), ((), ())), preferred_element_type=F32)


def _in_proj(h, g, glayer, wg, layer, after, name):
    T, D = h.shape
    _, nblk, _, nb = wg.shape

    nrow = T // ROW_TILE

    def body(h_ref, g_ref, w_ref, after_ref, p_ref, n_ref, n_all):
        j, i = pl.program_id(0), pl.program_id(1)

        @pl.when(j == 0)
        def _():
            x = h_ref[...]
            r = lax.rsqrt(jnp.mean(x * x, axis=-1, keepdims=True) + EPS_RMS)
            nn = (x * r * g_ref[...]).astype(BF16)
            n_ref[...] = nn
            n_all[i] = nn

        p_ref[...] = jnp.dot(n_all[i], w_ref[0], preferred_element_type=F32)

    def rows_once(j, i):
        return (jnp.where(j == 0, i, nrow - 1), 0)

    return _pallas(
        body, name=name, grid=(nblk, nrow),
        in_specs=[pl.BlockSpec((ROW_TILE, D), rows_once), pl.BlockSpec((None, 1, D), lambda j, i: (glayer, 0, 0)),
                  pl.BlockSpec((None, 1, D, nb), lambda j, i: (layer, j, 0, 0)),
                  pl.BlockSpec((8, 128), lambda j, i: (0, 0))],
        out_specs=[pl.BlockSpec((ROW_TILE, nb), lambda j, i: (i, j)), pl.BlockSpec((ROW_TILE, D), rows_once)],
        out_shape=[jax.ShapeDtypeStruct((T, nblk * nb), F32), jax.ShapeDtypeStruct((T, D), BF16)],
        scratch_shapes=[pltpu.VMEM((nrow, ROW_TILE, D), BF16)],
        compiler_params=_params("arbitrary", "arbitrary"))(h, g, wg, after)


def _out_proj(y, w, layer, hres, name):
    T, K = y.shape
    D = w.shape[2]

    def body(y_ref, w_ref, r_ref, o_ref):
        o_ref[...] = r_ref[...] + jnp.dot(y_ref[...], w_ref[...], preferred_element_type=F32)

    return _pallas(
        body, name=name, grid=(T // ROW_TILE,),
        in_specs=[pl.BlockSpec((ROW_TILE, K), lambda i: (i, 0)), pl.BlockSpec((None, K, D), lambda i: (layer, 0, 0)),
                  pl.BlockSpec((ROW_TILE, D), lambda i: (i, 0))],
        out_specs=pl.BlockSpec((ROW_TILE, D), lambda i: (i, 0)),
        out_shape=jax.ShapeDtypeStruct((T, D), F32),
        compiler_params=_params("parallel"))(y, w, hres)


def _dn_proj(dp, wg, layer, h, g, glayer, dres, after, name):
    T, D = h.shape
    _, nblk, _, nb = wg.shape

    nrow = T // ROW_TILE

    def body(dp_ref, w_ref, h_ref, g_ref, dres_ref, after_ref, dh_ref, dhb_ref, dg_ref, acc_ref):
        j, i = pl.program_id(0), pl.program_id(1)
        part = _nt(dp_ref[...], w_ref[0])

        @pl.when(j == 0)
        def _():
            acc_ref[i] = part

        @pl.when(j > 0)
        def _():
            acc_ref[i] += part

        @pl.when(j == nblk - 1)
        def _():
            x = h_ref[...]
            r = lax.rsqrt(jnp.mean(x * x, axis=-1, keepdims=True) + EPS_RMS)
            dn = acc_ref[i]
            q = dn * g_ref[...]
            dh = dres_ref[...] + r * q - x * ((r * r * r) * jnp.mean(q * x, axis=-1, keepdims=True))
            dh_ref[...] = dh
            dhb_ref[...] = dh.astype(BF16)
            dgp = jnp.sum(dn * (x * r), axis=0, keepdims=True)

            @pl.when(i == 0)
            def _():
                dg_ref[...] = dgp

            @pl.when(i > 0)
            def _():
                dg_ref[...] += dgp

    def rows_last(j, i):
        return (jnp.where(j == nblk - 1, i, 0), 0)

    return _pallas(
        body, name=name, grid=(nblk, nrow),
        in_specs=[pl.BlockSpec((ROW_TILE, nb), lambda j, i: (i, j)),
                  pl.BlockSpec((None, 1, D, nb), lambda j, i: (layer, j, 0, 0)),
                  pl.BlockSpec((ROW_TILE, D), rows_last), pl.BlockSpec((None, 1, D), lambda j, i: (glayer, 0, 0)),
                  pl.BlockSpec((ROW_TILE, D), rows_last), pl.BlockSpec((8, 128), lambda j, i: (0, 0))],
        out_specs=[pl.BlockSpec((ROW_TILE, D), rows_last), pl.BlockSpec((ROW_TILE, D), rows_last),
                   pl.BlockSpec((1, D), lambda j, i: (0, 0))],
        out_shape=[jax.ShapeDtypeStruct((T, D), F32), jax.ShapeDtypeStruct((T, D), BF16),
                   jax.ShapeDtypeStruct((1, D), F32)],
        scratch_shapes=[pltpu.VMEM((nrow, ROW_TILE, D), F32)],
        compiler_params=_params("arbitrary", "arbitrary"))(dp, wg, h, g, dres, after)


def _dw_in(n, dp, nblk, layer, nlayers, prev, name):
    T, D = n.shape
    nb = dp.shape[1] // nblk
    ta = D

    def body(n_ref, dp_ref, *rest):
        rest[-1][0] = _tn(n_ref[...], dp_ref[...]).astype(BF16)

    in_specs = [pl.BlockSpec((T, ta), lambda j, i: (0, i)), pl.BlockSpec((T, nb), lambda j, i: (0, j))]
    args = (n, dp) if prev is None else (n, dp, prev)
    return _pallas(
        body, name=name, grid=(nblk, D // ta), in_specs=in_specs + ([] if prev is None else [ANY]),
        out_specs=pl.BlockSpec((None, 1, ta, nb), lambda j, i: (layer, j, i, 0)),
        out_shape=jax.ShapeDtypeStruct((nlayers, nblk, D, nb), BF16),
        input_output_aliases={} if prev is None else {2: 0},
        compiler_params=_params("parallel", "parallel"))(*args)


def _dw_out(y, dout, layer, nlayers, prev, name):
    T, K = y.shape
    D = dout.shape[1]
    tk = 512

    def body(y_ref, d_ref, *rest):
        rest[-1][...] = _tn(y_ref[...], d_ref[...]).astype(BF16)

    in_specs = [pl.BlockSpec((T, tk), lambda i: (0, i)), pl.BlockSpec((T, D), lambda i: (0, 0))]
    args = (y, dout) if prev is None else (y, dout, prev)
    return _pallas(
        body, name=name, grid=(K // tk,), in_specs=in_specs + ([] if prev is None else [ANY]),
        out_specs=pl.BlockSpec((None, tk, D), lambda i: (layer, i, 0)),
        out_shape=jax.ShapeDtypeStruct((nlayers, K, D), BF16),
        input_output_aliases={} if prev is None else {2: 0},
        compiler_params=_params("parallel"))(*args)


def _loss_head(h, g, tgt):
    T, D = h.shape
    tm = MIX_TILE

    def body(h_ref, g_ref, t_ref, dh_ref, dhb_ref, dg_ref, loss_ref):
        i = pl.program_id(0)
        x = h_ref[...]
        gg = g_ref[...]
        r = lax.rsqrt(jnp.mean(x * x, axis=-1, keepdims=True) + EPS_RMS)
        xr = x * r
        e = xr * gg - t_ref[...]
        lp = 0.5 * jnp.sum(jnp.mean(e * e, axis=-1, keepdims=True), axis=0, keepdims=True)
        dn = e * (1.0 / D)
        q = dn * gg
        dh = r * q - x * ((r * r * r) * jnp.mean(q * x, axis=-1, keepdims=True))
        dh_ref[...] = dh
        dhb_ref[...] = dh.astype(BF16)
        dgp = jnp.sum(dn * xr, axis=0, keepdims=True)

        @pl.when(i == 0)
        def _():
            dg_ref[...] = dgp
            loss_ref[...] = lp

        @pl.when(i > 0)
        def _():
            dg_ref[...] += dgp
            loss_ref[...] += lp

    return _pallas(
        body, name="loss_head", grid=(T // tm,),
        in_specs=[pl.BlockSpec((tm, D), lambda i: (i, 0)), pl.BlockSpec((1, D), lambda i: (0, 0)),
                  pl.BlockSpec((tm, D), lambda i: (i, 0))],
        out_specs=[pl.BlockSpec((tm, D), lambda i: (i, 0)), pl.BlockSpec((tm, D), lambda i: (i, 0)),
                   pl.BlockSpec((1, D), lambda i: (0, 0)), pl.BlockSpec((1, 1), lambda i: (0, 0))],
        out_shape=[jax.ShapeDtypeStruct((T, D), F32), jax.ShapeDtypeStruct((T, D), BF16),
                   jax.ShapeDtypeStruct((1, D), F32), jax.ShapeDtypeStruct((1, 1), F32)],
        compiler_params=_params("arbitrary"))(h, g, tgt)


def _shift_up(x, j):
    return x if j == 0 else pltpu.roll(x, x.shape[0] - j, 0)


def _shift_down(x, j):
    return x if j == 0 else pltpu.roll(x, j, 0)


def _fill_shifted(dst_ref, src_ref):
    rows = dst_ref.shape[1]
    for s in range(8):
        dst_ref[s] = src_ref[pl.ds(s, rows), :]


def _fill_taps(wb_ref, w_ref):
    for k in range(w_ref.shape[0]):
        wb_ref[k] = jnp.broadcast_to(w_ref[k:k + 1, :], wb_ref.shape[1:])


def _tap_sum(sh_ref, wb_ref, r0, nrows, offsets):
    accs = [None] * (nrows // 8)
    for k, o in enumerate(offsets):
        wk = wb_ref[k]
        for u in range(nrows // 8):
            term = wk * sh_ref[o % 8, pl.ds(r0 + (o // 8) * 8 + 8 * u, 8), :]
            accs[u] = term if accs[u] is None else accs[u] + term
    return jnp.concatenate(accs, axis=0)


def _pool_sums(vx, up):
    sh = _shift_up if up else _shift_down
    outs = []
    for gi, w in enumerate(POOL_WINDOWS):
        s = vx[:, gi * POOL_GW:(gi + 1) * POOL_GW]
        j = 1
        while j < w:
            s = s + sh(s, j)
            j *= 2
        outs.append(s)
    return outs


def _inv_count(row0, nrows):
    pos = (row0 + 1 + lax.broadcasted_iota(jnp.int32, (nrows, 1), 0)).astype(F32)
    return [1.0 / jnp.minimum(pos, float(w)) for w in POOL_WINDOWS]


def _even_mixer_fwd(p, sl, cw, cb, lg, lb, pw, pb, sc, name):
    T = p.shape[0]
    C = D_MODEL
    tT, HL = MIX_TILE, EVEN_HALO
    hb = tT // HL
    chunk = 32

    def body(pm_ref, ph_ref, cw_ref, cb_ref, lg_ref, lb_ref, pw_ref, pb_ref, sc_ref, y_ref, u1_ref, u0x_ref, sh_ref,
             wb_ref):
        i = pl.program_id(0)
        keep = (i > 0).astype(F32)

        @pl.when(i == 0)
        def _():
            _fill_taps(wb_ref, cw_ref)

        u0x_ref[0:HL] = ph_ref[:, 0:C] * _sigmoid(ph_ref[:, C:2 * C]) * keep
        u0x_ref[HL:HL + tT] = pm_ref[:, 0:C] * _sigmoid(pm_ref[:, C:2 * C])
        u0x_ref[HL + tT:HL + tT + 8] = jnp.zeros((8, C), F32)
        _fill_shifted(sh_ref, u0x_ref)
        offs = [HL - (CONV_K - 1) + k for k in range(CONV_K)]

        def conv_chunk(c, carry):
            r0 = pl.multiple_of(c * chunk, chunk)
            u1_ref[pl.ds(r0, chunk), :] = _tap_sum(sh_ref, wb_ref, r0, chunk, offs) + cb_ref[...]
            return carry

        lax.fori_loop(0, tT // chunk, conv_chunk, 0)
        u1 = u1_ref[...]
        mu = jnp.mean(u1, axis=-1, keepdims=True)
        xc = u1 - mu
        rs = lax.rsqrt(jnp.mean(xc * xc, axis=-1, keepdims=True) + EPS_LN)
        u2 = xc * rs * lg_ref[...] + lb_ref[...]
        u3 = u2 * _sigmoid(u2)
        ag = pm_ref[:, 2 * C:3 * C]
        y_ref[:, 0:C] = (u3 * (ag * _sigmoid(ag))).astype(BF16)
        vx = jnp.concatenate([ph_ref[:, 3 * C:4 * C] * keep, pm_ref[:, 3 * C:4 * C]], axis=0)
        sums = _pool_sums(vx, up=False)
        inv = _inv_count(i * tT, tT)
        for gi in range(len(POOL_WINDOWS)):
            cols = slice(gi * POOL_GW, (gi + 1) * POOL_GW)
            d0 = sums[gi][HL:] * inv[gi] - vx[HL:, cols]
            d1 = jnp.dot(d0.astype(BF16), pw_ref[gi], preferred_element_type=F32) + pb_ref[:, cols]
            bg = pm_ref[:, 4 * C + gi * POOL_GW:4 * C + (gi + 1) * POOL_GW]
            y_ref[:, C + gi * POOL_GW:C + (gi + 1) * POOL_GW] = (d1 * sc_ref[:, cols] * (bg * _sigmoid(bg))).astype(BF16)

    vec = pl.BlockSpec((None, 1, C), lambda i: (sl, 0, 0))
    return _pallas(
        body, name=name, grid=(T // tT,),
        in_specs=[pl.BlockSpec((tT, 5 * C), lambda i: (i, 0)),
                  pl.BlockSpec((HL, 5 * C), lambda i: (jnp.maximum(i * hb - 1, 0), 0)),
                  pl.BlockSpec((None, 32, C), lambda i: (sl, 0, 0)), vec, vec, vec,
                  pl.BlockSpec((4, POOL_GW, POOL_GW), lambda i: (0, 0, 0)), vec, vec],
        out_specs=[pl.BlockSpec((tT, 2 * C), lambda i: (i, 0)), pl.BlockSpec((tT, C), lambda i: (i, 0))],
        out_shape=[jax.ShapeDtypeStruct((T, 2 * C), BF16), jax.ShapeDtypeStruct((T, C), F32)],
        scratch_shapes=[pltpu.VMEM((HL + tT + 8, C), F32), pltpu.VMEM((8, HL + tT, C), F32),
                        pltpu.VMEM((32, 8, C), F32)],
        compiler_params=_params("arbitrary"))(p, p, cw, cb, lg, lb, pw, pb, sc)


def _even_mixer_bwd(p, u1, dout, w_out, after, sl, cwr, lg, lb, pw, pb, sc, name):
    T = p.shape[0]
    C = D_MODEL
    tT, HL = MIX_TILE, EVEN_HALO
    hb = tT // HL
    nT = T // tT
    R1 = tT + HL
    chunk = 32

    def body(pm_ref, pp_ref, pn_ref, u1m_ref, u1n_ref, dom_ref, don_ref, wo_ref, after_ref, cwr_ref, lg_ref, lb_ref,
             pw_ref, pb_ref, sc_ref, dp_ref, dcw_ref, dvec_ref, dpw_ref, x_ref, sh_ref, du0_ref, wb_ref):
        i = pl.program_id(0)
        dy = _nt(jnp.concatenate([dom_ref[...], don_ref[...]], axis=0), wo_ref[...])

        @pl.when(i == 0)
        def _():
            _fill_taps(wb_ref, cwr_ref)

        keep_prev = (i > 0).astype(F32)
        keep_next = (i < nT - 1).astype(F32)
        row = lax.broadcasted_iota(jnp.int32, (R1, 1), 0)
        live = jnp.where(row < tT, 1.0, keep_next)

        def cat(m, n):
            return jnp.concatenate([m, n], axis=0)

        u1 = cat(u1m_ref[...], u1n_ref[...])
        mu = jnp.mean(u1, axis=-1, keepdims=True)
        xc = u1 - mu
        rs = lax.rsqrt(jnp.mean(xc * xc, axis=-1, keepdims=True) + EPS_LN)
        xh = xc * rs
        u2 = xh * lg_ref[...] + lb_ref[...]
        s2 = _sigmoid(u2)
        u3 = u2 * s2
        ag = cat(pm_ref[:, 2 * C:3 * C], pn_ref[:, 2 * C:3 * C])
        sa = _sigmoid(ag)
        dya = dy[:, 0:C]
        dp_ref[:, 2 * C:3 * C] = (dya * u3 * _dsilu(ag, sa))[0:tT].astype(BF16)
        du2 = dya * (ag * sa) * _dsilu(u2, s2)
        dlg = jnp.sum((du2 * xh)[0:tT], axis=0, keepdims=True)
        dlb = jnp.sum(du2[0:tT], axis=0, keepdims=True)
        dxh = du2 * lg_ref[...]
        du1 = rs * (dxh - jnp.mean(dxh, axis=-1, keepdims=True) - xh * jnp.mean(dxh * xh, axis=-1, keepdims=True))
        du1 = du1 * live
        dcb = jnp.sum(du1[0:tT], axis=0, keepdims=True)
        x_ref[0:R1] = du1
        x_ref[R1:R1 + 8] = jnp.zeros((8, C), F32)
        _fill_shifted(sh_ref, x_ref)

        def du0_chunk(c, carry):
            r0 = pl.multiple_of(c * chunk, chunk)
            du0_ref[pl.ds(r0, chunk), :] = _tap_sum(sh_ref, wb_ref, r0, chunk, list(range(CONV_K)))
            return carry

        lax.fori_loop(0, tT // chunk, du0_chunk, 0)
        av, agl = pm_ref[:, 0:C], pm_ref[:, C:2 * C]
        sg = _sigmoid(agl)
        du0 = du0_ref[...]
        dp_ref[:, 0:C] = (du0 * sg).astype(BF16)
        dp_ref[:, C:2 * C] = (du0 * av * sg * (1.0 - sg)).astype(BF16)
        du0_ref[...] = du1[0:tT]
        x_ref[0:HL] = pp_ref[:, 0:C] * _sigmoid(pp_ref[:, C:2 * C]) * keep_prev
        x_ref[HL:HL + tT] = av * sg
        x_ref[HL + tT:HL + tT + 8] = jnp.zeros((8, C), F32)
        _fill_shifted(sh_ref, x_ref)

        @pl.when(i == 0)
        def _():
            dcw_ref[...] = jnp.zeros_like(dcw_ref)

        for k0 in range(0, CONV_K, 2):
            taps = [k for k in (k0, k0 + 1) if k < CONV_K]
            offs = [HL - (CONV_K - 1) + k for k in taps]

            def dw_chunk(c, accs, offs=offs):
                r0 = pl.multiple_of(c * 64, 64)
                accs = list(accs)
                for u in range(0, 64, 8):
                    d = du0_ref[pl.ds(r0 + u, 8), :]
                    for t, o in enumerate(offs):
                        accs[t] = accs[t] + d * sh_ref[o % 8, pl.ds(r0 + u + (o // 8) * 8, 8), :]
                return tuple(accs)

            sums = lax.fori_loop(0, tT // 64, dw_chunk, tuple(jnp.zeros((8, C), F32) for _ in taps))
            for k, acc in zip(taps, sums):
                dcw_ref[8 * k:8 * k + 8, :] += acc

        bg = cat(pm_ref[:, 4 * C:5 * C], pn_ref[:, 4 * C:5 * C])
        sb = _sigmoid(bg)
        dyb = dy[:, C:2 * C]
        dyb0 = dyb * (bg * sb)
        dd1 = dyb0 * sc_ref[...]
        dpb = jnp.sum(dd1[0:tT], axis=0, keepdims=True)
        inv1 = _inv_count(i * tT, R1)
        z_parts, dd0_parts = [], []
        for gi in range(len(POOL_WINDOWS)):
            cols = slice(gi * POOL_GW, (gi + 1) * POOL_GW)
            dd0 = _nt(dd1[:, cols].astype(BF16), pw_ref[gi])
            dd0_parts.append(dd0)
            z_parts.append(dd0 * inv1[gi] * live)
        fsum = _pool_sums(jnp.concatenate(z_parts, axis=1), up=True)
        vx = cat(pp_ref[:, 3 * C:4 * C] * keep_prev, pm_ref[:, 3 * C:4 * C])
        sums = _pool_sums(vx, up=False)
        inv0 = _inv_count(i * tT, tT)
        dsc_parts = []
        for gi in range(len(POOL_WINDOWS)):
            cols = slice(gi * POOL_GW, (gi + 1) * POOL_GW)
            dp_ref[:, 3 * C + gi * POOL_GW:3 * C + (gi + 1) * POOL_GW] = (fsum[gi][0:tT] - dd0_parts[gi][0:tT]).astype(BF16)
            d0 = (sums[gi][HL:] * inv0[gi] - vx[HL:, cols]).astype(BF16)
            d1 = jnp.dot(d0, pw_ref[gi], preferred_element_type=F32) + pb_ref[:, cols]
            bgm, sbm = bg[0:tT, cols], sb[0:tT, cols]
            dp_ref[:, 4 * C + gi * POOL_GW:4 * C + (gi + 1) * POOL_GW] = (
                dyb[0:tT, cols] * d1 * sc_ref[:, cols] * _dsilu(bgm, sbm)).astype(BF16)
            dsc_parts.append(jnp.sum(dyb0[0:tT, cols] * d1, axis=0, keepdims=True))
            dpw_g = _tn(d0, dd1[0:tT, cols].astype(BF16))

            @pl.when(i == 0)
            def _(gi=gi, dpw_g=dpw_g):
                dpw_ref[gi] = dpw_g

            @pl.when(i > 0)
            def _(gi=gi, dpw_g=dpw_g):
                dpw_ref[gi] += dpw_g

        dsc = jnp.concatenate(dsc_parts, axis=1)
        vecs = jnp.concatenate([dcb, dlg, dlb, dsc, dpb, jnp.zeros((3, C), F32)], axis=0)

        @pl.when(i == 0)
        def _():
            dvec_ref[...] = vecs

        @pl.when(i > 0)
        def _():
            dvec_ref[...] += vecs

    vec = pl.BlockSpec((None, 1, C), lambda i: (sl, 0, 0))
    taps = pl.BlockSpec((None, 32, C), lambda i: (sl, 0, 0))

    def prev_blk(i):
        return (jnp.maximum(i * hb - 1, 0), 0)

    def next_blk(i):
        return (jnp.minimum((i + 1) * hb, T // HL - 1), 0)

    return _pallas(
        body, name=name, grid=(nT,),
        in_specs=[pl.BlockSpec((tT, 5 * C), lambda i: (i, 0)), pl.BlockSpec((HL, 5 * C), prev_blk),
                  pl.BlockSpec((HL, 5 * C), next_blk),
                  pl.BlockSpec((tT, C), lambda i: (i, 0)), pl.BlockSpec((HL, C), next_blk),
                  pl.BlockSpec((tT, C), lambda i: (i, 0)), pl.BlockSpec((HL, C), next_blk),
                  pl.BlockSpec((None, 2 * C, C), lambda i: (0, 0, 0)), pl.BlockSpec((8, 128), lambda i: (0, 0)),
                  taps, vec, vec, pl.BlockSpec((4, POOL_GW, POOL_GW), lambda i: (0, 0, 0)), vec, vec],
        out_specs=[pl.BlockSpec((tT, 5 * C), lambda i: (i, 0)), pl.BlockSpec((32 * 8, C), lambda i: (0, 0)),
                   pl.BlockSpec((8, C), lambda i: (0, 0)), pl.BlockSpec((4, POOL_GW, POOL_GW), lambda i: (0, 0, 0))],
        out_shape=[jax.ShapeDtypeStruct((T, 5 * C), BF16), jax.ShapeDtypeStruct((32 * 8, C), F32),
                   jax.ShapeDtypeStruct((8, C), F32), jax.ShapeDtypeStruct((4, POOL_GW, POOL_GW), F32)],
        scratch_shapes=[pltpu.VMEM((R1 + 8, C), F32), pltpu.VMEM((8, R1, C), F32), pltpu.VMEM((tT, C), F32),
                        pltpu.VMEM((32, 8, C), F32)],
        compiler_params=_params("arbitrary"))(p, p, p, u1, u1, dout, dout, w_out, after, cwr, lg, lb, pw, pb, sc)


def _softplus(z):
    u = jnp.exp(-jnp.abs(z))
    w = 1.0 + u
    l1p = jnp.where(w == 1.0, u, u * jnp.log(w) / jnp.where(w == 1.0, 1.0, w - 1.0))
    return jnp.maximum(z, 0.0) + l1p


def _lru_gates(xrx, cw_ref, cb_ref, wr_ref, br_ref, wi_ref, bi_ref, lam_ref):
    HL = ODD_HALO
    xc = cb_ref[...] + cw_ref[LRU_CONV_K - 1:LRU_CONV_K, :] * xrx[HL:]
    for k in range(LRU_CONV_K - 1):
        xc = xc + cw_ref[k:k + 1, :] * _shift_down(xrx, LRU_CONV_K - 1 - k)[HL:]
    xcb = xc.astype(BF16)
    rp, ip = [], []
    for hd in range(LRU_HEADS):
        cols = slice(hd * LRU_HD, (hd + 1) * LRU_HD)
        rp.append(jnp.dot(xcb[:, cols], wr_ref[hd], preferred_element_type=F32))
        ip.append(jnp.dot(xcb[:, cols], wi_ref[hd], preferred_element_type=F32))
    r = _sigmoid(jnp.concatenate(rp, axis=1) + br_ref[...])
    ig = _sigmoid(jnp.concatenate(ip, axis=1) + bi_ref[...])
    sp = _softplus(-lam_ref[...])
    log_a = (-LRU_C) * r * sp
    a = jnp.exp(log_a)
    m2 = jnp.maximum(-jnp.tanh(log_a) * (a * a + 1.0), 1e-30)
    inv_mult = lax.rsqrt(m2)
    return xc, xcb, r, ig, sp, a, m2 * inv_mult, inv_mult


def _group_scan(a, b, reverse):
    n, w = a.shape
    a, b = a.reshape(n // 8, 8, w), b.reshape(n // 8, 8, w)
    pos = lax.broadcasted_iota(jnp.int32, (1, 8, 1), 1)
    s = 1
    while s < 8:
        ok = (pos < 8 - s) if reverse else (pos >= s)
        shift = (8 - s) if reverse else s
        a_sh = jnp.where(ok, pltpu.roll(a, shift, 1), 1.0)
        b_sh = jnp.where(ok, pltpu.roll(b, shift, 1), 0.0)
        b = a * b_sh + b
        a = a * a_sh
        s *= 2
    return a.reshape(n, w), b.reshape(n, w)


def _apply_carries(a_ref, b_ref, out_ref, c0, reverse):
    ng = a_ref.shape[0] // 8

    def step(t, c):
        r0 = pl.multiple_of(((ng - 1 - t) if reverse else t) * 8, 8)
        x = a_ref[pl.ds(r0, 8), :] * c + b_ref[pl.ds(r0, 8), :]
        out_ref[pl.ds(r0, 8), :] = x
        return x[0:1, :] if reverse else x[7:8, :]

    return lax.fori_loop(0, ng, step, c0)


def _odd_mixer_fwd(p, sl, cw, cb, wr, br, wi, bi, lam, name):
    T = p.shape[0]
    W = W_LRU
    tT, HL = MIX_TILE, ODD_HALO
    hb = tT // HL

    def body(pm_ref, ph_ref, cw_ref, cb_ref, wr_ref, br_ref, wi_ref, bi_ref, lam_ref, y_ref, hs_ref, carry_ref,
             sa_ref, sb_ref):
        i = pl.program_id(0)
        keep = (i > 0).astype(F32)

        @pl.when(i == 0)
        def _():
            carry_ref[...] = jnp.zeros_like(carry_ref)

        xrx = jnp.concatenate([ph_ref[:, 0:W] * keep, pm_ref[:, 0:W]], axis=0)
        xc, _, _, ig, _, a, mult, _ = _lru_gates(xrx, cw_ref, cb_ref, wr_ref, br_ref, wi_ref, bi_ref, lam_ref)
        sa_ref[...], sb_ref[...] = _group_scan(a, mult * (ig * xc), reverse=False)
        last = _apply_carries(sa_ref, sb_ref, hs_ref, carry_ref[0:1, :], reverse=False)
        carry_ref[...] = jnp.broadcast_to(last, (8, W))
        hs = hs_ref[...]
        gt = pm_ref[:, W:2 * W]
        y_ref[...] = (hs * (gt * _sigmoid(gt))).astype(BF16)

    vec = pl.BlockSpec((None, 1, W), lambda i: (sl, 0, 0))
    heads = pl.BlockSpec((None, LRU_HEADS, LRU_HD, LRU_HD), lambda i: (sl, 0, 0, 0))
    return _pallas(
        body, name=name, grid=(T // tT,),
        in_specs=[pl.BlockSpec((tT, 2 * W), lambda i: (i, 0)),
                  pl.BlockSpec((HL, 2 * W), lambda i: (jnp.maximum(i * hb - 1, 0), 0)),
                  pl.BlockSpec((None, 8, W), lambda i: (sl, 0, 0)), vec, heads, vec, heads, vec, vec],
        out_specs=[pl.BlockSpec((tT, W), lambda i: (i, 0)), pl.BlockSpec((tT, W), lambda i: (i, 0))],
        out_shape=[jax.ShapeDtypeStruct((T, W), BF16), jax.ShapeDtypeStruct((T, W), F32)],
        scratch_shapes=[pltpu.VMEM((8, W), F32), pltpu.VMEM((tT, W), F32), pltpu.VMEM((tT, W), F32)],
        compiler_params=_params("arbitrary"))(p, p, cw, cb, wr, br, wi, bi, lam)


def _odd_mixer_bwd(p, hs, dout, w_out, after, sl, cw, cb, wr, br, wi, bi, lam, name):
    T = p.shape[0]
    W = W_LRU
    D = dout.shape[1]
    tT, HL = MIX_TILE, ODD_HALO
    hb = tT // HL
    nT = T // tT

    def body(pm_ref, ph_ref, hsm_ref, hsh_ref, do_ref, wo_ref, after_ref, cw_ref, cb_ref, wr_ref, br_ref, wi_ref,
             bi_ref, lam_ref, dp_ref, dwr_ref, dwi_ref, dvec_ref, gcarry_ref, xcarry_ref, sa_ref, sb_ref, g_ref):
        i = pl.program_id(0)
        keep = (i < nT - 1).astype(F32)

        @pl.when(i == 0)
        def _():
            gcarry_ref[...] = jnp.zeros_like(gcarry_ref)
            xcarry_ref[...] = jnp.zeros_like(xcarry_ref)

        xrx = jnp.concatenate([ph_ref[:, 0:W] * keep, pm_ref[:, 0:W]], axis=0)
        xc, xcb, r, ig, sp, a, mult, inv_mult = _lru_gates(xrx, cw_ref, cb_ref, wr_ref, br_ref, wi_ref, bi_ref, lam_ref)
        hs = hsm_ref[...]
        gt = pm_ref[:, W:2 * W]
        sg = _sigmoid(gt)
        dyv = _nt(do_ref[...], wo_ref[...])
        dp_ref[:, W:2 * W] = (dyv * hs * _dsilu(gt, sg)).astype(BF16)
        row = lax.broadcasted_iota(jnp.int32, (tT, 1), 0)
        m = jnp.where(row == tT - 1, 1.0, _shift_up(a, 1))
        sa_ref[...], sb_ref[...] = _group_scan(m, dyv * (gt * sg), reverse=True)
        first = _apply_carries(sa_ref, sb_ref, g_ref, gcarry_ref[0:1, :], reverse=True)
        G = g_ref[...]
        gcarry_ref[...] = jnp.broadcast_to(a[0:1, :] * first, (8, W))
        hs_prev = jnp.where(row == 0, hsh_ref[HL - 1:HL, :] * keep, _shift_down(hs, 1))
        da = G * hs_prev
        dmult = G * (ig * xc)
        di = G * mult * xc
        dxc = G * mult * ig
        dlog_a = da * a - dmult * (a * a) * inv_mult
        drp = dlog_a * ((-LRU_C) * sp) * r * (1.0 - r)
        dip = di * ig * (1.0 - ig)
        dlam = jnp.sum(dlog_a * ((-LRU_C) * r), axis=0, keepdims=True) * (-_sigmoid(-lam_ref[...]))
        drb, dib = drp.astype(BF16), dip.astype(BF16)
        back = []
        for hd in range(LRU_HEADS):
            cols = slice(hd * LRU_HD, (hd + 1) * LRU_HD)
            back.append(_nt(drb[:, cols], wr_ref[hd]) + _nt(dib[:, cols], wi_ref[hd]))
            dwr_h = _tn(xcb[:, cols], drb[:, cols])
            dwi_h = _tn(xcb[:, cols], dib[:, cols])

            @pl.when(i == 0)
            def _(hd=hd, dwr_h=dwr_h, dwi_h=dwi_h):
                dwr_ref[hd] = dwr_h
                dwi_ref[hd] = dwi_h

            @pl.when(i > 0)
            def _(hd=hd, dwr_h=dwr_h, dwi_h=dwi_h):
                dwr_ref[hd] += dwr_h
                dwi_ref[hd] += dwi_h

        dxc = dxc + jnp.concatenate(back, axis=1)
        dxcx = jnp.concatenate([dxc, xcarry_ref[...]], axis=0)
        dxr = cw_ref[LRU_CONV_K - 1:LRU_CONV_K, :] * dxc
        rows = []
        for k in range(LRU_CONV_K - 1):
            j = LRU_CONV_K - 1 - k
            dxr = dxr + cw_ref[k:k + 1, :] * _shift_up(dxcx, j)[0:tT]
            rows.append(jnp.sum(dxc * _shift_down(xrx, j)[HL:], axis=0, keepdims=True))
        rows.append(jnp.sum(dxc * xrx[HL:], axis=0, keepdims=True))
        dp_ref[:, 0:W] = dxr.astype(BF16)
        xcarry_ref[...] = dxc[0:8]
        rows += [jnp.sum(dxc, axis=0, keepdims=True), jnp.sum(drp, axis=0, keepdims=True),
                 jnp.sum(dip, axis=0, keepdims=True), dlam]
        vecs = jnp.concatenate(rows, axis=0)

        @pl.when(i == 0)
        def _():
            dvec_ref[...] = vecs

        @pl.when(i > 0)
        def _():
            dvec_ref[...] += vecs

    vec = pl.BlockSpec((None, 1, W), lambda i: (sl, 0, 0))
    heads = pl.BlockSpec((None, LRU_HEADS, LRU_HD, LRU_HD), lambda i: (sl, 0, 0, 0))
    dheads = pl.BlockSpec((LRU_HEADS, LRU_HD, LRU_HD), lambda i: (0, 0, 0))

    def tile(i):
        return (nT - 1 - i, 0)

    def prev_blk(i):
        return (jnp.maximum((nT - 1 - i) * hb - 1, 0), 0)

    return _pallas(
        body, name=name, grid=(nT,),
        in_specs=[pl.BlockSpec((tT, 2 * W), tile), pl.BlockSpec((HL, 2 * W), prev_blk),
                  pl.BlockSpec((tT, W), tile), pl.BlockSpec((HL, W), prev_blk), pl.BlockSpec((tT, D), tile),
                  pl.BlockSpec((None, W, D), lambda i: (0, 0, 0)), pl.BlockSpec((8, 128), lambda i: (0, 0)),
                  pl.BlockSpec((None, 8, W), lambda i: (sl, 0, 0)), vec, heads, vec, heads, vec, vec],
        out_specs=[pl.BlockSpec((tT, 2 * W), tile), dheads, dheads, pl.BlockSpec((8, W), lambda i: (0, 0))],
        out_shape=[jax.ShapeDtypeStruct((T, 2 * W), BF16), jax.ShapeDtypeStruct((LRU_HEADS, LRU_HD, LRU_HD), F32),
                   jax.ShapeDtypeStruct((LRU_HEADS, LRU_HD, LRU_HD), F32), jax.ShapeDtypeStruct((8, W), F32)],
        scratch_shapes=[pltpu.VMEM((8, W), F32), pltpu.VMEM((8, W), F32), pltpu.VMEM((tT, W), F32),
                        pltpu.VMEM((tT, W), F32), pltpu.VMEM((tT, W), F32)],
        compiler_params=_params("arbitrary"))(p, p, hs, hs, dout, w_out, after, cw, cb, wr, br, wi, bi, lam)


def _pad_rows(a, rows):
    return jnp.pad(a, ((0, 0), (0, rows - a.shape[1]), (0, 0)))


def _layer_fwd(even, h, w, w_in, w_out, after, mid=None):
    sl = w["sl"]
    p, n = _in_proj(h, w["norm"], sl, w_in, 0, after, "in_proj_even" if even else "in_proj_odd")
    if mid is not None:
        mid()
    if even:
        y, aux = _even_mixer_fwd(p, sl, w["conv_w"], w["conv_b"], w["ln_g"], w["ln_b"], w["pool_w"], w["pool_b"],
                                 w["pool_scale"], "even_mixer_fwd")
    else:
        y, aux = _odd_mixer_fwd(p, sl, w["conv_w"], w["conv_b"], w["w_rg"], w["b_rg"], w["w_ig"], w["b_ig"], w["lam"],
                                "odd_mixer_fwd")
    if callable(w_out):
        w_out = w_out(y)
    h_next = _out_proj(y, w_out, 0, h, "out_proj_even" if even else "out_proj_odd")
    return h_next, (h, n, p, aux, y), w_out


def _layer_bwd_weights(even, saved, w, w_out, dhb, after):
    h, n, p, aux, y = saved
    if even:
        dp, dcw, dvec, dpw = _even_mixer_bwd(p, aux, dhb, w_out, after, w["sl"], w["conv_w_rev"], w["ln_g"], w["ln_b"],
                                             w["pool_w"], w["pool_b"], w["pool_scale"], "even_mixer_bwd")
        dw_out = _dw_out(y, dhb, 0, 1, None, "dw_out_even")
        dw_in = _dw_in(n, dp, N_CHIPS, 0, 1, None, "dw_in_even")
        return dp, dw_in, dw_out, dict(conv_w=dcw, vec=dvec, pool_w=dpw)
    dp, dwr, dwi, dvec = _odd_mixer_bwd(p, aux, dhb, w_out, after, w["sl"], w["conv_w"], w["conv_b"], w["w_rg"],
                                        w["b_rg"], w["w_ig"], w["b_ig"], w["lam"], "odd_mixer_bwd")
    dw_out = _dw_out(y, dhb, 0, 1, None, "dw_out_odd")
    dw_in = _dw_in(n, dp, N_CHIPS, 0, 1, None, "dw_in_odd")
    return dp, dw_in, dw_out, dict(w_rg=dwr, w_ig=dwi, vec=dvec)


def _layer_bwd_input(even, saved, w, w_in, dp, dh, after):
    return _dn_proj(dp, w_in, 0, saved[0], w["norm"], w["sl"], dh, after, "dn_proj_even" if even else "dn_proj_odd")


ANY = pl.BlockSpec(memory_space=pl.ANY)


def _mesh_pos():
    return lax.axis_index("x"), lax.axis_index("y"), lax.axis_index("c")


def _other_chips(x, y):
    return [(1 - x, y), (x, 1 - y), (1 - x, 1 - y)]


def _other_devices(x, y, c):
    out = []
    for p in range(1, N_DEV):
        out.append((1 - x if p & 4 else x, 1 - y if p & 2 else y, 1 - c if p & 1 else c))
    return out


def _remote(src, dst, ssem, rsem, dev):
    return pltpu.make_async_remote_copy(src_ref=src, dst_ref=dst, send_sem=ssem, recv_sem=rsem, device_id=dev,
                                        device_id_type=MESH)


def _comm_call(body, name, ins, out_shape, scratch, aliases=None):
    return _pallas(body, name=name, in_specs=[ANY] * len(ins), out_specs=[ANY] * len(out_shape), out_shape=out_shape,
                   scratch_shapes=scratch, input_output_aliases=aliases or {},
                   compiler_params=pltpu.CompilerParams(has_side_effects=True))(*ins)


def _cast_shard(w, layer, pos):
    _, R, C = w.shape
    tr = _row_tile(R, C)

    def body(pos_ref, w_ref, o_ref):
        o_ref[...] = w_ref[...].astype(BF16)

    grid_spec = pltpu.PrefetchScalarGridSpec(
        num_scalar_prefetch=1, grid=(R // tr,),
        in_specs=[pl.BlockSpec((None, tr, C), lambda i, pr: (layer, i, 0))],
        out_specs=pl.BlockSpec((None, None, tr, C), lambda i, pr: (0, pr[0], i, 0)))
    return _pallas(body, name="cast_shard", grid_spec=grid_spec,
                   out_shape=jax.ShapeDtypeStruct((1, N_CHIPS, R, C), BF16),
                   compiler_params=_params("parallel"))(pos, w)


def _gather_weights(big, small):
    nA = len(big)
    half = [a.shape[2] // 2 for a in big]

    def body(*refs):
        ins, outs = refs[:nA + 1], refs[nA + 1:2 * nA + 2]
        ssem, rsem, fsem, frsem, lsem = refs[2 * nA + 2:]
        x, y, c = _mesh_pos()
        k = 2 * x + y
        chips = _other_chips(x, y)
        sib = (x, y, 1 - c)

        def slab(a, chip, core):
            return outs[a].at[:, chip, pl.ds(core * half[a], half[a]), :]

        local = [pltpu.make_async_copy(ins[nA], outs[nA].at[k], lsem.at[0])]
        for cp in local:
            cp.start()
        sends = []
        for j, (ox, oy) in enumerate(chips):
            for a in range(nA):
                sends.append(_remote(slab(a, k, c), slab(a, k, c), ssem.at[a, j], rsem.at[a, j], (ox, oy, c)))
            sends.append(_remote(ins[nA], outs[nA].at[k], ssem.at[nA, j], rsem.at[nA, j], (ox, oy, c)))
        for cp in sends:
            cp.start()
        for j, (ox, oy) in enumerate(chips):
            kj = 2 * ox + oy
            for a in range(nA):
                got = slab(a, kj, c)
                _remote(got, got, ssem.at[a, j], rsem.at[a, j], (ox, oy, c)).wait_recv()
                fw = _remote(got, got, fsem.at[a, j], frsem.at[a, j], sib)
                fw.start()
                sends.append(fw)
            gs = outs[nA].at[kj]
            _remote(gs, gs, ssem.at[nA, j], rsem.at[nA, j], (ox, oy, c)).wait_recv()
        for j, (ox, oy) in enumerate(chips):
            kj = 2 * ox + oy
            for a in range(nA):
                theirs = slab(a, kj, 1 - c)
                _remote(theirs, theirs, fsem.at[a, j], frsem.at[a, j], sib).wait_recv()
        for cp in sends:
            cp.wait_send()
        for cp in local:
            cp.wait()

    out_shape = [jax.ShapeDtypeStruct(a.shape, a.dtype) for a in big]
    out_shape.append(jax.ShapeDtypeStruct((N_CHIPS,) + small.shape, small.dtype))
    scratch = [pltpu.SemaphoreType.DMA((nA + 1, 3)), pltpu.SemaphoreType.DMA((nA + 1, 3)),
               pltpu.SemaphoreType.DMA((nA, 3)), pltpu.SemaphoreType.DMA((nA, 3)), pltpu.SemaphoreType.DMA((1,))]
    return _comm_call(body, "gather_weights", list(big) + [small], out_shape, scratch, {a: a for a in range(nA)})


HBM = pl.BlockSpec(memory_space=pltpu.HBM)
SEM = pl.BlockSpec(memory_space=pltpu.SEMAPHORE)
EFFECT = pltpu.SideEffectType.DATAFLOW_SIDE_EFFECTING


def _split_start(arrays, copies, n, name):
    k = len(arrays)

    def body(*refs):
        for cp in copies(refs[k + 2:2 * k + 2], refs[k], refs[k + 1]):
            cp.start()
        refs[2 * k + 2][...] = jnp.zeros((8, 128), F32)

    out = _pallas(
        body, name=name,
        out_shape=(pltpu.SemaphoreType.DMA((n,)), pltpu.SemaphoreType.DMA((n,)),
                   *[pltpu.HBM(a.shape, a.dtype) for a in arrays], jax.ShapeDtypeStruct((8, 128), F32)),
        in_specs=(HBM,) * k, out_specs=(SEM, SEM) + (HBM,) * k + (pl.BlockSpec(memory_space=pltpu.VMEM),),
        input_output_aliases={i: i + 2 for i in range(k)},
        compiler_params=pltpu.CompilerParams(has_side_effects=EFFECT),
    )(*[pltpu.with_memory_space_constraint(a, pltpu.HBM) for a in arrays])
    return out[0], out[1], list(out[2:2 + k]), out[2 + k]


def _split_wait(ssem, rsem, arrays, copies, after, name):
    k = len(arrays)

    def body(*refs):
        for cp in copies(refs[:k], refs[k], refs[k + 1]):
            cp.wait_send()
            cp.wait_recv()

    out = _pallas(
        body, name=name, out_shape=tuple(pltpu.HBM(a.shape, a.dtype) for a in arrays),
        in_specs=(HBM,) * k + (SEM, SEM, ANY), out_specs=(HBM,) * k, input_output_aliases={i: i for i in range(k)},
        compiler_params=pltpu.CompilerParams(has_side_effects=EFFECT),
    )(*arrays, ssem, rsem, after)
    return list(out)


def _gather_copies(shapes):
    half = [s[2] // 2 for s in shapes]

    def copies(refs, ssem, rsem):
        x, y, c = _mesh_pos()
        out = []
        for j, (ox, oy) in enumerate(_other_chips(x, y)):
            for a, ref in enumerate(refs):
                slab = ref.at[:, 2 * x + y, pl.ds(c * half[a], half[a]), :]
                out.append(_remote(slab, slab, ssem.at[3 * a + j], rsem.at[3 * a + j], (ox, oy, c)))
        return out

    return copies


def _chips_copies(n_arr):
    def copies(refs, ssem, rsem):
        x, y, c = _mesh_pos()
        out = []
        for j, (ox, oy) in enumerate(_other_chips(x, y)):
            for a in range(n_arr):
                out.append(_remote(refs[a].at[:, 2 * ox + oy], refs[n_arr + a].at[:, 2 * x + y], ssem.at[3 * a + j],
                                   rsem.at[3 * a + j], (ox, oy, c)))
        return out

    return copies


def _halves_copies(shapes):
    n = len(shapes)
    half = [s[2] // 2 for s in shapes]

    def copies(refs, ssem, rsem):
        x, y, c = _mesh_pos()
        return [_remote(refs[a].at[:, :, pl.ds((1 - c) * half[a], half[a]), :], refs[n + a], ssem.at[a], rsem.at[a],
                        (x, y, 1 - c)) for a in range(n)]

    return copies


def _forward_cores(arrays):
    nA = len(arrays)
    half = [a.shape[2] // 2 for a in arrays]

    def body(*refs):
        outs = refs[nA:2 * nA]
        ssem, rsem = refs[2 * nA:]
        x, y, c = _mesh_pos()
        sib = (x, y, 1 - c)
        sends, waits = [], []
        for j, (ox, oy) in enumerate(_other_chips(x, y)):
            for a in range(nA):
                got = outs[a].at[:, 2 * ox + oy, pl.ds(c * half[a], half[a]), :]
                sends.append(_remote(got, got, ssem.at[a, j], rsem.at[a, j], sib))
                theirs = outs[a].at[:, 2 * ox + oy, pl.ds((1 - c) * half[a], half[a]), :]
                waits.append(_remote(theirs, theirs, ssem.at[a, j], rsem.at[a, j], sib))
        for cp in sends:
            cp.start()
        for cp in waits:
            cp.wait_recv()
        for cp in sends:
            cp.wait_send()

    out_shape = [jax.ShapeDtypeStruct(a.shape, a.dtype) for a in arrays]
    scratch = [pltpu.SemaphoreType.DMA((nA, 3)), pltpu.SemaphoreType.DMA((nA, 3))]
    return _comm_call(body, "forward_cores", list(arrays), out_shape, scratch, {a: a for a in range(nA)})


def _exchange_halves(big):
    nA = len(big)
    half = [a.shape[2] // 2 for a in big]

    def body(*refs):
        ins, outs = refs[:nA], refs[nA:2 * nA]
        ssem, rsem = refs[2 * nA:]
        x, y, c = _mesh_pos()
        sib = (x, y, 1 - c)
        sends = [_remote(ins[a].at[:, :, pl.ds((1 - c) * half[a], half[a]), :], outs[a], ssem.at[a], rsem.at[a], sib)
                 for a in range(nA)]
        for cp in sends:
            cp.start()
        for a in range(nA):
            _remote(outs[a], outs[a], ssem.at[a], rsem.at[a], sib).wait_recv()
        for cp in sends:
            cp.wait_send()

    out_shape = [jax.ShapeDtypeStruct((a.shape[0], N_CHIPS, h, a.shape[3]), a.dtype) for a, h in zip(big, half)]
    scratch = [pltpu.SemaphoreType.DMA((nA,)), pltpu.SemaphoreType.DMA((nA,))]
    return _comm_call(body, "exchange_halves", list(big), out_shape, scratch)


def _exchange_final(grads, everywhere, small):
    nA = len(grads)
    n_remote = sum(N_DEV - 1 if ev else 1 for ev in everywhere) + N_DEV - 1

    def body(*refs):
        small_ref, outs, gathered = refs[nA], refs[nA + 1:2 * nA + 1], refs[2 * nA + 1]
        ssem, rsem, lsem = refs[2 * nA + 2:]
        x, y, c = _mesh_pos()
        k = 2 * x + y
        me = 2 * k + c
        sib = (x, y, 1 - c)
        peers = _other_devices(x, y, c)
        local = pltpu.make_async_copy(small_ref, gathered.at[me], lsem.at[0])
        local.start()
        sends, waits = [], []
        s = 0
        for (px, py, pc) in peers:
            sends.append(_remote(small_ref, gathered.at[me], ssem.at[s], rsem.at[s], (px, py, pc)))
            got = gathered.at[4 * px + 2 * py + pc]
            waits.append(_remote(got, got, ssem.at[s], rsem.at[s], (px, py, pc)))
            s += 1
        for a in range(nA):
            if everywhere[a]:
                r2 = grads[a].shape[1] // N_DEV
                mine = outs[a].at[:, pl.ds((2 * k + c) * r2, r2), :]
                for (px, py, pc) in peers:
                    sends.append(_remote(mine, mine, ssem.at[s], rsem.at[s], (px, py, pc)))
                    got = outs[a].at[:, pl.ds((2 * (2 * px + py) + pc) * r2, r2), :]
                    waits.append(_remote(got, got, ssem.at[s], rsem.at[s], (px, py, pc)))
                    s += 1
            else:
                r2 = grads[a].shape[1] // 2
                mine = outs[a].at[:, pl.ds(c * r2, r2), :]
                sends.append(_remote(mine, mine, ssem.at[s], rsem.at[s], sib))
                got = outs[a].at[:, pl.ds((1 - c) * r2, r2), :]
                waits.append(_remote(got, got, ssem.at[s], rsem.at[s], sib))
                s += 1
        for cp in sends:
            cp.start()
        for cp in waits:
            cp.wait_recv()
        for cp in sends:
            cp.wait_send()
        local.wait()

    out_shape = [jax.ShapeDtypeStruct(g.shape, g.dtype) for g in grads]
    out_shape.append(jax.ShapeDtypeStruct((N_DEV,) + small.shape, small.dtype))
    scratch = [pltpu.SemaphoreType.DMA((n_remote,)), pltpu.SemaphoreType.DMA((n_remote,)), pltpu.SemaphoreType.DMA((1,))]
    return _comm_call(body, "exchange_final", list(grads) + [small], out_shape, scratch, {a: a for a in range(nA)})


BLOCK_BYTES = 4 << 20


def _row_tile(rows, cols, mult=16, limit=BLOCK_BYTES):
    best = mult
    for t in range(mult, rows + 1, mult):
        if rows % t == 0 and t * cols * 4 <= limit:
            best = t
    return best


def _add_cores(own, recv, pos):
    L, _, R, C = own.shape
    r2 = R // 2
    tr = _row_tile(r2, C)
    nb = r2 // tr

    def body(pos_ref, a_ref, r_ref, o_ref):
        o_ref[...] = (a_ref[...].astype(F32) + r_ref[...].astype(F32)).astype(BF16)

    blk = (None, None, tr, C)
    grid_spec = pltpu.PrefetchScalarGridSpec(
        num_scalar_prefetch=1, grid=(L, N_CHIPS, nb),
        in_specs=[pl.BlockSpec(blk, lambda l, s, i, pr: (l, s, pr[1] * nb + i, 0)),
                  pl.BlockSpec(blk, lambda l, s, i, pr: (l, s, i, 0))],
        out_specs=pl.BlockSpec(blk, lambda l, s, i, pr: (l, s, i, 0)))
    return _pallas(body, name="add_cores", grid_spec=grid_spec,
                   out_shape=jax.ShapeDtypeStruct((L, N_CHIPS, r2, C), BF16),
                   compiler_params=_params("parallel", "parallel", "parallel"))(pos, own, recv)


def _sum_chips(own, recv, pos, everywhere, layer, nlayers, prev):
    _, _, r2, C = own.shape
    tr = _row_tile(r2, 2 * C)
    nb = r2 // tr

    def body(pos_ref, a_ref, r_ref, *rest):
        acc = None
        for s in range(N_CHIPS):
            term = jnp.where(pos_ref[0] == s, a_ref[...], r_ref[s]).astype(F32)
            acc = term if acc is None else acc + term
        rest[-1][...] = acc

    if everywhere:
        def out_map(i, pr):
            return (layer, (2 * pr[0] + pr[1]) * nb + i, 0)
    else:
        def out_map(i, pr):
            return (layer, pr[1] * nb + i, 0)

    in_specs = [pl.BlockSpec((None, None, tr, C), lambda i, pr: (0, pr[0], i, 0)),
                pl.BlockSpec((None, N_CHIPS, tr, C), lambda i, pr: (0, 0, i, 0))]
    grid_spec = pltpu.PrefetchScalarGridSpec(
        num_scalar_prefetch=1, grid=(nb,), in_specs=in_specs + ([] if prev is None else [ANY]),
        out_specs=pl.BlockSpec((None, tr, C), out_map))
    rows = (N_DEV if everywhere else 2) * r2
    args = (pos, own, recv) if prev is None else (pos, own, recv, prev)
    return _pallas(body, name="sum_chips", grid_spec=grid_spec, out_shape=jax.ShapeDtypeStruct((nlayers, rows, C), F32),
                   input_output_aliases={} if prev is None else {3: 0},
                   compiler_params=_params("parallel"))(*args)


def _sum_devices(parts):
    n, R, C = parts.shape
    tr = _row_tile(R, C * n, 8)

    def body(p_ref, o_ref):
        acc = p_ref[0]
        for s in range(1, n):
            acc = acc + p_ref[s]
        o_ref[...] = acc

    return _pallas(body, name="sum_devices", grid=(R // tr,), in_specs=[pl.BlockSpec((n, tr, C), lambda i: (0, i, 0))],
                   out_specs=pl.BlockSpec((tr, C), lambda i: (i, 0)), out_shape=jax.ShapeDtypeStruct((R, C), F32),
                   compiler_params=_params("parallel"))(parts)


def _adamw(w, g, m, v, name):
    L, R, C = w.shape
    tr = _row_tile(R, C, 8, BLOCK_BYTES // 2)

    def body(w_ref, g_ref, m_ref, v_ref, d_ref, m2_ref, v2_ref):
        gg = g_ref[...]
        m2 = ADAM_B1 * m_ref[...] + (1.0 - ADAM_B1) * gg
        v2 = ADAM_B2 * v_ref[...] + (1.0 - ADAM_B2) * (gg * gg)
        m_hat = m2 / (1.0 - ADAM_B1 ** ADAM_STEP)
        v_hat = v2 / (1.0 - ADAM_B2 ** ADAM_STEP)
        d_ref[...] = -ADAM_LR * (m_hat / (jnp.sqrt(v_hat) + ADAM_EPS) + ADAM_WD * w_ref[...])
        m2_ref[...] = m2
        v2_ref[...] = v2

    blk = pl.BlockSpec((1, tr, C), lambda l, i: (l, i, 0))
    shp = jax.ShapeDtypeStruct((L, R, C), F32)
    return _pallas(body, name=name, grid=(L, R // tr), in_specs=[blk] * 4, out_specs=[blk] * 3, out_shape=[shp] * 3,
                   compiler_params=_params("parallel", "parallel"))(w, g, m, v)


WEIGHTS = ("norm_even", "w_in_even", "conv_a_w", "conv_a_b", "ln_a_g", "ln_a_b", "pool_w", "pool_b", "pool_scale",
           "w_out_even", "norm_odd", "w_in_odd", "conv_c_w", "conv_c_b", "w_rg", "b_rg", "w_ig", "b_ig", "lru_lambda",
           "w_out_odd", "final_norm")
BIG = ("w_in_even", "w_out_even", "pool_w", "w_in_odd", "w_out_odd", "w_rg", "w_ig")
SMALL = tuple(n for n in WEIGHTS if n not in BIG)
SMALL_SHARDED = ("conv_a_w", "pool_b", "norm_odd", "conv_c_w", "conv_c_b", "b_rg", "b_ig", "lru_lambda")


def _pack(arrs):
    flat = jnp.concatenate([a.reshape(-1) for a in arrs])
    rows = -(-flat.shape[0] // (64 * 128)) * 64
    return jnp.pad(flat, (0, rows * 128 - flat.shape[0])).reshape(rows, 128)


def _unpack(buf, shapes, lead=()):
    flat = buf.reshape(tuple(lead) + (-1,))
    out, o = [], 0
    for s in shapes:
        n = 1
        for d in s:
            n *= d
        out.append(flat[..., o:o + n].reshape(tuple(lead) + tuple(s)))
        o += n
    return out


def _shard(full, axis, k):
    n = full.shape[axis] // N_CHIPS
    return lax.dynamic_slice_in_dim(full, k * n, n, axis)


def kernel(x, norm_even, w_in_even, conv_a_w, conv_a_b, ln_a_g, ln_a_b, pool_w, pool_b, pool_scale, w_out_even, norm_odd, w_in_odd, conv_c_w, conv_c_b, w_rg, b_rg, w_ig, b_ig, lru_lambda, w_out_odd, final_norm, loss_target, m_norm_even, m_w_in_even, m_conv_a_w, m_conv_a_b, m_ln_a_g, m_ln_a_b, m_pool_w, m_pool_b, m_pool_scale, m_w_out_even, m_norm_odd, m_w_in_odd, m_conv_c_w, m_conv_c_b, m_w_rg, m_b_rg, m_w_ig, m_b_ig, m_lru_lambda, m_w_out_odd, m_final_norm, v_norm_even, v_w_in_even, v_conv_a_w, v_conv_a_b, v_ln_a_g, v_ln_a_b, v_pool_w, v_pool_b, v_pool_scale, v_w_out_even, v_norm_odd, v_w_in_odd, v_conv_c_w, v_conv_c_b, v_w_rg, v_b_rg, v_w_ig, v_b_ig, v_lru_lambda, v_w_out_odd, v_final_norm):
    P = dict(locals())
    xi, yi, ci = _mesh_pos()
    k = 2 * xi + yi
    L = w_in_even.shape[0]
    D = D_MODEL

    pos = jnp.stack([k, ci]).astype(jnp.int32)
    depth = 2 * L
    pool_w3 = pool_w.reshape(L, 4 * 64, POOL_GW)

    def cast_group(layer):
        j = layer // 2
        if layer % 2 == 0:
            return [_cast_shard(w_in_even, j, pos), _cast_shard(w_out_even, j, pos), _cast_shard(pool_w3, j, pos)]
        return [_cast_shard(w_in_odd, j, pos), _cast_shard(w_out_odd, j, pos)]

    first = cast_group(0)
    g_in, g_pool, g_small = _gather_weights([first[0], first[2]], _pack([P[n] for n in SMALL_SHARDED]))
    copies0 = _gather_copies([first[1].shape])
    ssem0, rsem0, late, token0 = _split_start([first[1]], copies0, 3, "gather_start0")

    def late_w_out(y):
        return _forward_cores(_split_wait(ssem0, rsem0, late, copies0, y, "gather_wait0"))[0].reshape(1, -1, D)

    group = [g_in, late_w_out, g_pool]
    full = {}
    for n, a in zip(SMALL_SHARDED, _unpack(g_small, [P[n].shape for n in SMALL_SHARDED], lead=(N_CHIPS,))):
        a = jnp.moveaxis(a, 0, -2)
        full[n] = a.reshape(a.shape[:-2] + (N_CHIPS * a.shape[-1],))

    small_even = dict(norm=norm_even[:, None], conv_w=_pad_rows(full["conv_a_w"], 32),
                      conv_w_rev=_pad_rows(full["conv_a_w"][:, ::-1], 32), conv_b=conv_a_b[:, None], ln_g=ln_a_g[:, None],
                      ln_b=ln_a_b[:, None], pool_b=full["pool_b"].reshape(L, 1, D), pool_scale=pool_scale[:, None])
    small_odd = dict(norm=full["norm_odd"][:, None], conv_w=_pad_rows(full["conv_c_w"], 8),
                     conv_b=full["conv_c_b"][:, None], w_rg=w_rg.astype(BF16), b_rg=full["b_rg"][:, None],
                     w_ig=w_ig.astype(BF16), b_ig=full["b_ig"][:, None], lam=full["lru_lambda"][:, None])

    def small_weights(layer, group):
        if layer % 2 == 0:
            pw = group[2].reshape(N_CHIPS, 4, 64, POOL_GW).transpose(1, 0, 2, 3).reshape(4, POOL_GW, POOL_GW)
            return dict(small_even, sl=layer // 2, pool_w=pw)
        return dict(small_odd, sl=layer // 2)

    no_token = jnp.zeros((8, 128), F32)
    h = x[0]
    saved, big_w, small_w = [], [], []
    for layer in range(depth):
        token, mid, started = no_token, None, []

        def start_next(layer=layer, started=started):
            nxt = cast_group(layer + 1)
            copies = _gather_copies([a.shape for a in nxt])
            started.append((copies,) + _split_start(nxt, copies, 3 * len(nxt), "gather_start%d" % (layer + 1)))

        if layer == 0:
            token, mid = token0, start_next
        elif layer + 1 < depth:
            start_next()
            token = started[0][4]
        small_w.append(small_weights(layer, group))
        w_out = group[1] if callable(group[1]) else group[1].reshape(1, -1, D)
        h, sv, w_out = _layer_fwd(layer % 2 == 0, h, small_w[layer], group[0], w_out, token, mid)
        big_w.append((group[0], w_out))
        saved.append(sv)
        if layer + 1 < depth:
            copies, ssem, rsem, nxt, _ = started[0]
            group = _forward_cores(_split_wait(ssem, rsem, nxt, copies, h, "gather_wait%d" % (layer + 1)))

    dh, dhb, d_final, loss = _loss_head(h, final_norm[None], loss_target[0])
    loss = lax.psum(loss[0, 0], ("x", "y", "c"))
    everywhere = [False, False, False, False, False, True, True]
    final = [None] * len(everywhere)
    small_of = [None] * depth

    def finish(pending, after):
        ssem, rsem, arrs, copies, slots, pj, pl_ = pending
        arrs = _split_wait(ssem, rsem, arrs, copies, after, "chips_wait%d" % pl_)
        for a, r, s in zip(arrs[:len(slots)], arrs[len(slots):], slots):
            final[s] = _sum_chips(a, r, pos, everywhere[s], pj, L, final[s])

    pending = None
    token = no_token
    for layer in reversed(range(depth)):
        j = layer // 2
        even_layer = layer % 2 == 0
        dp, dw_in, dw_out, sm = _layer_bwd_weights(even_layer, saved[layer], small_w[layer], big_w[layer][1], dhb, token)
        if even_layer:
            dpw = sm["pool_w"].reshape(4, N_CHIPS, 64, POOL_GW).transpose(1, 0, 2, 3)
            parts = [dw_in, dw_out.reshape(1, N_CHIPS, -1, D), dpw.reshape(1, N_CHIPS, 4 * 64, POOL_GW).astype(BF16)]
            slots = [0, 1, 2]
        else:
            parts = [dw_in, dw_out.reshape(1, N_CHIPS, -1, D),
                     sm["w_rg"].reshape(1, N_CHIPS, -1, LRU_HD).astype(BF16),
                     sm["w_ig"].reshape(1, N_CHIPS, -1, LRU_HD).astype(BF16)]
            slots = [3, 4, 5, 6]
        n = len(parts)
        if layer > 0:
            hcopies = _halves_copies([a.shape for a in parts])
            hland = [lax.empty((1, N_CHIPS, a.shape[2] // 2, a.shape[3]), a.dtype) for a in parts]
            hs, hr, harrs, htoken = _split_start(parts + hland, hcopies, n, "halves_start%d" % layer)
            dh, dhb, sm["norm"] = _layer_bwd_input(even_layer, saved[layer], small_w[layer], big_w[layer][0], dp, dh,
                                                   htoken)
            harrs = _split_wait(hs, hr, harrs, hcopies, dh, "halves_wait%d" % layer)
            parts, recv = harrs[:n], harrs[n:]
        else:
            recv = _exchange_halves(parts)
        pair = [_add_cores(a, r, pos) for a, r in zip(parts, recv)]
        copies = _chips_copies(n)
        land = [lax.empty(a.shape, a.dtype) for a in pair]
        ssem, rsem, arrs, token = _split_start(pair + land, copies, 3 * n, "chips_start%d" % layer)
        if layer == 0:
            dh, dhb, sm["norm"] = _layer_bwd_input(even_layer, saved[layer], small_w[layer], big_w[layer][0], dp, dh, token)
        small_of[layer] = sm
        if pending is not None:
            finish(pending, dh)
        pending = (ssem, rsem, arrs, copies, slots, j, layer)
    grad_x = dh
    small_g = []
    for jj in range(L):
        ge, go = small_of[2 * jj], small_of[2 * jj + 1]
        small_g += [ge["conv_w"].reshape(32, 8, D).sum(axis=1)[:CONV_K], ge["vec"][0:5], ge["norm"], go["vec"], go["norm"]]
    small_g.append(d_final)
    small_shapes = [a.shape for a in small_g]
    packed_small = _pack(small_g)
    finish(pending, packed_small)
    *gw, recv_small = _exchange_final(final, everywhere, packed_small)
    sg = _unpack(_sum_devices(recv_small), small_shapes)

    grads = dict(w_in_even=gw[0], w_out_even=gw[1], pool_w=gw[2].reshape(pool_w.shape), w_in_odd=gw[3], w_out_odd=gw[4],
                 w_rg=gw[5].reshape(w_rg.shape), w_ig=gw[6].reshape(w_ig.shape), final_norm=sg[-1][0])
    ev = [sg[5 * j + 1] for j in range(L)]
    ov = [sg[5 * j + 3] for j in range(L)]
    grads["conv_a_w"] = _shard(jnp.stack([sg[5 * j] for j in range(L)]), 2, k)
    grads["norm_even"] = jnp.stack([sg[5 * j + 2][0] for j in range(L)])
    grads["norm_odd"] = _shard(jnp.stack([sg[5 * j + 4][0] for j in range(L)]), 1, k)
    for r, n in enumerate(("conv_a_b", "ln_a_g", "ln_a_b", "pool_scale")):
        grads[n] = jnp.stack([e[r] for e in ev])
    grads["pool_b"] = _shard(jnp.stack([e[4].reshape(4, POOL_GW) for e in ev]), 2, k)
    grads["conv_c_w"] = _shard(jnp.stack([o[0:4] for o in ov]), 2, k)
    for r, n in zip((4, 5, 6, 7), ("conv_c_b", "b_rg", "b_ig", "lru_lambda")):
        grads[n] = _shard(jnp.stack([o[r] for o in ov]), 1, k)

    delta, new_m, new_v = {}, {}, {}
    for n in BIG:
        s3 = (L, -1, P[n].shape[-1])
        d, m2, v2 = _adamw(P[n].reshape(s3), grads[n].reshape(s3), P["m_" + n].reshape(s3), P["v_" + n].reshape(s3), "adamw")
        delta[n], new_m[n], new_v[n] = d.reshape(P[n].shape), m2.reshape(P[n].shape), v2.reshape(P[n].shape)
    shapes = [P[n].shape for n in SMALL]
    packed = [_pack([src[n] for n in SMALL])[None] for src in
              (P, grads, {n: P["m_" + n] for n in SMALL}, {n: P["v_" + n] for n in SMALL})]
    for res, out in zip(_adamw(*packed, "adamw_small"), (delta, new_m, new_v)):
        for n, a in zip(SMALL, _unpack(res[0], shapes)):
            out[n] = a

    return (loss, grad_x[None], *[grads[n] for n in WEIGHTS], *[delta[n] for n in WEIGHTS],
            *[new_m[n] for n in WEIGHTS], *[new_v[n] for n in WEIGHTS])
```

```python
import functools

import jax
import jax.numpy as jnp
from jax import lax
from jax.experimental import pallas as pl
from jax.experimental.pallas import tpu as pltpu

F32 = jnp.float32
BF16 = jnp.bfloat16
MESH = pl.DeviceIdType.MESH

D_MODEL = 1024
N_CHIPS = 4
N_DEV = 8
EPS_RMS = 1e-6
EPS_LN = 1e-5
CONV_K = 31
POOL_WINDOWS = (2, 4, 8, 16)
POOL_GW = 256
LRU_HEADS = 12
LRU_HD = 128
W_LRU = LRU_HEADS * LRU_HD
LRU_CONV_K = 4
LRU_C = 8.0
ADAM_LR = 0.001
ADAM_B1 = 0.9
ADAM_B2 = 0.999
ADAM_EPS = 1e-08
ADAM_WD = 0.01
ADAM_STEP = 10

VMEM_LIMIT_BYTES = 56 * 1024 * 1024
ROW_TILE = 512
MIX_TILE = 256
EVEN_HALO = 32
ODD_HALO = 8


def _pallas(body, **kw):
    return pl.pallas_call(body, **kw)


def _params(*sem):
    return pltpu.CompilerParams(dimension_semantics=sem if sem else None, vmem_limit_bytes=VMEM_LIMIT_BYTES)


def _sigmoid(x):
    return 0.5 * jnp.tanh(0.5 * x) + 0.5


def _dsilu(x, s):
    return s * (1.0 + x * (1.0 - s))


def _nt(a, b):
    return lax.dot_general(a, b, (((1,), (1,)), ((), ())), preferred_element_type=F32)


def _tn(a, b):
    return lax.dot_general(a, b, (((0,), (0,)), ((), ())), preferred_element_type=F32)


def _in_proj(h, g, glayer, wg, layer, after, name):
    T, D = h.shape
    _, nblk, _, nb = wg.shape

    nrow = T // ROW_TILE

    def body(h_ref, g_ref, w_ref, after_ref, p_ref, n_ref, n_all):
        j, i = pl.program_id(0), pl.program_id(1)

        @pl.when(j == 0)
        def _():
            x = h_ref[...]
            r = lax.rsqrt(jnp.mean(x * x, axis=-1, keepdims=True) + EPS_RMS)
            nn = (x * r * g_ref[...]).astype(BF16)
            n_ref[...] = nn
            n_all[i] = nn

        p_ref[...] = jnp.dot(n_all[i], w_ref[0], preferred_element_type=F32)

    def rows_once(j, i):
        return (jnp.where(j == 0, i, nrow - 1), 0)

    return _pallas(
        body, name=name, grid=(nblk, nrow),
        in_specs=[pl.BlockSpec((ROW_TILE, D), rows_once), pl.BlockSpec((None, 1, D), lambda j, i: (glayer, 0, 0)),
                  pl.BlockSpec((None, 1, D, nb), lambda j, i: (layer, j, 0, 0)),
                  pl.BlockSpec((8, 128), lambda j, i: (0, 0))],
        out_specs=[pl.BlockSpec((ROW_TILE, nb), lambda j, i: (i, j)), pl.BlockSpec((ROW_TILE, D), rows_once)],
        out_shape=[jax.ShapeDtypeStruct((T, nblk * nb), F32), jax.ShapeDtypeStruct((T, D), BF16)],
        scratch_shapes=[pltpu.VMEM((nrow, ROW_TILE, D), BF16)],
        compiler_params=_params("arbitrary", "arbitrary"))(h, g, wg, after)


def _out_proj(y, w, layer, hres, name):
    T, K = y.shape
    D = w.shape[2]

    def body(y_ref, w_ref, r_ref, o_ref):
        o_ref[...] = r_ref[...] + jnp.dot(y_ref[...], w_ref[...], preferred_element_type=F32)

    return _pallas(
        body, name=name, grid=(T // ROW_TILE,),
        in_specs=[pl.BlockSpec((ROW_TILE, K), lambda i: (i, 0)), pl.BlockSpec((None, K, D), lambda i: (layer, 0, 0)),
                  pl.BlockSpec((ROW_TILE, D), lambda i: (i, 0))],
        out_specs=pl.BlockSpec((ROW_TILE, D), lambda i: (i, 0)),
        out_shape=jax.ShapeDtypeStruct((T, D), F32),
        compiler_params=_params("parallel"))(y, w, hres)


def _dn_proj(dp, wg, layer, h, g, glayer, dres, after, name):
    T, D = h.shape
    _, nblk, _, nb = wg.shape

    nrow = T // ROW_TILE

    def body(dp_ref, w_ref, h_ref, g_ref, dres_ref, after_ref, dh_ref, dhb_ref, dg_ref, acc_ref):
        j, i = pl.program_id(0), pl.program_id(1)
        part = _nt(dp_ref[...], w_ref[0])

        @pl.when(j == 0)
        def _():
            acc_ref[i] = part

        @pl.when(j > 0)
        def _():
            acc_ref[i] += part

        @pl.when(j == nblk - 1)
        def _():
            x = h_ref[...]
            r = lax.rsqrt(jnp.mean(x * x, axis=-1, keepdims=True) + EPS_RMS)
            dn = acc_ref[i]
            q = dn * g_ref[...]
            dh = dres_ref[...] + r * q - x * ((r * r * r) * jnp.mean(q * x, axis=-1, keepdims=True))
            dh_ref[...] = dh
            dhb_ref[...] = dh.astype(BF16)
            dgp = jnp.sum(dn * (x * r), axis=0, keepdims=True)

            @pl.when(i == 0)
            def _():
                dg_ref[...] = dgp

            @pl.when(i > 0)
            def _():
                dg_ref[...] += dgp

    def rows_last(j, i):
        return (jnp.where(j == nblk - 1, i, 0), 0)

    return _pallas(
        body, name=name, grid=(nblk, nrow),
        in_specs=[pl.BlockSpec((ROW_TILE, nb), lambda j, i: (i, j)),
                  pl.BlockSpec((None, 1, D, nb), lambda j, i: (layer, j, 0, 0)),
                  pl.BlockSpec((ROW_TILE, D), rows_last), pl.BlockSpec((None, 1, D), lambda j, i: (glayer, 0, 0)),
                  pl.BlockSpec((ROW_TILE, D), rows_last), pl.BlockSpec((8, 128), lambda j, i: (0, 0))],
        out_specs=[pl.BlockSpec((ROW_TILE, D), rows_last), pl.BlockSpec((ROW_TILE, D), rows_last),
                   pl.BlockSpec((1, D), lambda j, i: (0, 0))],
        out_shape=[jax.ShapeDtypeStruct((T, D), F32), jax.ShapeDtypeStruct((T, D), BF16),
                   jax.ShapeDtypeStruct((1, D), F32)],
        scratch_shapes=[pltpu.VMEM((nrow, ROW_TILE, D), F32)],
        compiler_params=_params("arbitrary", "arbitrary"))(dp, wg, h, g, dres, after)


def _dw_in(n, dp, nblk, layer, nlayers, prev, name):
    T, D = n.shape
    nb = dp.shape[1] // nblk
    ta = D

    def body(n_ref, dp_ref, *rest):
        rest[-1][0] = _tn(n_ref[...], dp_ref[...]).astype(BF16)

    in_specs = [pl.BlockSpec((T, ta), lambda j, i: (0, i)), pl.BlockSpec((T, nb), lambda j, i: (0, j))]
    args = (n, dp) if prev is None else (n, dp, prev)
    return _pallas(
        body, name=name, grid=(nblk, D // ta), in_specs=in_specs + ([] if prev is None else [ANY]),
        out_specs=pl.BlockSpec((None, 1, ta, nb), lambda j, i: (layer, j, i, 0)),
        out_shape=jax.ShapeDtypeStruct((nlayers, nblk, D, nb), BF16),
        input_output_aliases={} if prev is None else {2: 0},
        compiler_params=_params("parallel", "parallel"))(*args)


def _dw_out(y, dout, layer, nlayers, prev, name):
    T, K = y.shape
    D = dout.shape[1]
    tk = 512

    def body(y_ref, d_ref, *rest):
        rest[-1][...] = _tn(y_ref[...], d_ref[...]).astype(BF16)

    in_specs = [pl.BlockSpec((T, tk), lambda i: (0, i)), pl.BlockSpec((T, D), lambda i: (0, 0))]
    args = (y, dout) if prev is None else (y, dout, prev)
    return _pallas(
        body, name=name, grid=(K // tk,), in_specs=in_specs + ([] if prev is None else [ANY]),
        out_specs=pl.BlockSpec((None, tk, D), lambda i: (layer, i, 0)),
        out_shape=jax.ShapeDtypeStruct((nlayers, K, D), BF16),
        input_output_aliases={} if prev is None else {2: 0},
        compiler_params=_params("parallel"))(*args)


def _loss_head(h, g, tgt):
    T, D = h.shape
    tm = MIX_TILE

    def body(h_ref, g_ref, t_ref, dh_ref, dhb_ref, dg_ref, loss_ref):
        i = pl.program_id(0)
        x = h_ref[...]
        gg = g_ref[...]
        r = lax.rsqrt(jnp.mean(x * x, axis=-1, keepdims=True) + EPS_RMS)
        xr = x * r
        e = xr * gg - t_ref[...]
        lp = 0.5 * jnp.sum(jnp.mean(e * e, axis=-1, keepdims=True), axis=0, keepdims=True)
        dn = e * (1.0 / D)
        q = dn * gg
        dh = r * q - x * ((r * r * r) * jnp.mean(q * x, axis=-1, keepdims=True))
        dh_ref[...] = dh
        dhb_ref[...] = dh.astype(BF16)
        dgp = jnp.sum(dn * xr, axis=0, keepdims=True)

        @pl.when(i == 0)
        def _():
            dg_ref[...] = dgp
            loss_ref[...] = lp

        @pl.when(i > 0)
        def _():
            dg_ref[...] += dgp
            loss_ref[...] += lp

    return _pallas(
        body, name="loss_head", grid=(T // tm,),
        in_specs=[pl.BlockSpec((tm, D), lambda i: (i, 0)), pl.BlockSpec((1, D), lambda i: (0, 0)),
                  pl.BlockSpec((tm, D), lambda i: (i, 0))],
        out_specs=[pl.BlockSpec((tm, D), lambda i: (i, 0)), pl.BlockSpec((tm, D), lambda i: (i, 0)),
                   pl.BlockSpec((1, D), lambda i: (0, 0)), pl.BlockSpec((1, 1), lambda i: (0, 0))],
        out_shape=[jax.ShapeDtypeStruct((T, D), F32), jax.ShapeDtypeStruct((T, D), BF16),
                   jax.ShapeDtypeStruct((1, D), F32), jax.ShapeDtypeStruct((1, 1), F32)],
        compiler_params=_params("arbitrary"))(h, g, tgt)


def _shift_up(x, j):
    return x if j == 0 else pltpu.roll(x, x.shape[0] - j, 0)


def _shift_down(x, j):
    return x if j == 0 else pltpu.roll(x, j, 0)


def _fill_shifted(dst_ref, src_ref):
    rows = dst_ref.shape[1]
    for s in range(8):
        dst_ref[s] = src_ref[pl.ds(s, rows), :]


def _fill_taps(wb_ref, w_ref):
    for k in range(w_ref.shape[0]):
        wb_ref[k] = jnp.broadcast_to(w_ref[k:k + 1, :], wb_ref.shape[1:])


def _tap_sum(sh_ref, wb_ref, r0, nrows, offsets):
    accs = [None] * (nrows // 8)
    for k, o in enumerate(offsets):
        wk = wb_ref[k]
        for u in range(nrows // 8):
            term = wk * sh_ref[o % 8, pl.ds(r0 + (o // 8) * 8 + 8 * u, 8), :]
            accs[u] = term if accs[u] is None else accs[u] + term
    return jnp.concatenate(accs, axis=0)


def _pool_sums(vx, up):
    sh = _shift_up if up else _shift_down
    outs = []
    for gi, w in enumerate(POOL_WINDOWS):
        s = vx[:, gi * POOL_GW:(gi + 1) * POOL_GW]
        j = 1
        while j < w:
            s = s + sh(s, j)
            j *= 2
        outs.append(s)
    return outs


def _inv_count(row0, nrows):
    pos = (row0 + 1 + lax.broadcasted_iota(jnp.int32, (nrows, 1), 0)).astype(F32)
    return [1.0 / jnp.minimum(pos, float(w)) for w in POOL_WINDOWS]


def _even_mixer_fwd(p, sl, cw, cb, lg, lb, pw, pb, sc, name):
    T = p.shape[0]
    C = D_MODEL
    tT, HL = MIX_TILE, EVEN_HALO
    hb = tT // HL
    chunk = 32

    def body(pm_ref, ph_ref, cw_ref, cb_ref, lg_ref, lb_ref, pw_ref, pb_ref, sc_ref, y_ref, u1_ref, u0x_ref, sh_ref,
             wb_ref):
        i = pl.program_id(0)
        keep = (i > 0).astype(F32)

        @pl.when(i == 0)
        def _():
            _fill_taps(wb_ref, cw_ref)

        u0x_ref[0:HL] = ph_ref[:, 0:C] * _sigmoid(ph_ref[:, C:2 * C]) * keep
        u0x_ref[HL:HL + tT] = pm_ref[:, 0:C] * _sigmoid(pm_ref[:, C:2 * C])
        u0x_ref[HL + tT:HL + tT + 8] = jnp.zeros((8, C), F32)
        _fill_shifted(sh_ref, u0x_ref)
        offs = [HL - (CONV_K - 1) + k for k in range(CONV_K)]

        def conv_chunk(c, carry):
            r0 = pl.multiple_of(c * chunk, chunk)
            u1_ref[pl.ds(r0, chunk), :] = _tap_sum(sh_ref, wb_ref, r0, chunk, offs) + cb_ref[...]
            return carry

        lax.fori_loop(0, tT // chunk, conv_chunk, 0)
        u1 = u1_ref[...]
        mu = jnp.mean(u1, axis=-1, keepdims=True)
        xc = u1 - mu
        rs = lax.rsqrt(jnp.mean(xc * xc, axis=-1, keepdims=True) + EPS_LN)
        u2 = xc * rs * lg_ref[...] + lb_ref[...]
        u3 = u2 * _sigmoid(u2)
        ag = pm_ref[:, 2 * C:3 * C]
        y_ref[:, 0:C] = (u3 * (ag * _sigmoid(ag))).astype(BF16)
        vx = jnp.concatenate([ph_ref[:, 3 * C:4 * C] * keep, pm_ref[:, 3 * C:4 * C]], axis=0)
        sums = _pool_sums(vx, up=False)
        inv = _inv_count(i * tT, tT)
        for gi in range(len(POOL_WINDOWS)):
            cols = slice(gi * POOL_GW, (gi + 1) * POOL_GW)
            d0 = sums[gi][HL:] * inv[gi] - vx[HL:, cols]
            d1 = jnp.dot(d0.astype(BF16), pw_ref[gi], preferred_element_type=F32) + pb_ref[:, cols]
            bg = pm_ref[:, 4 * C + gi * POOL_GW:4 * C + (gi + 1) * POOL_GW]
            y_ref[:, C + gi * POOL_GW:C + (gi + 1) * POOL_GW] = (d1 * sc_ref[:, cols] * (bg * _sigmoid(bg))).astype(BF16)

    vec = pl.BlockSpec((None, 1, C), lambda i: (sl, 0, 0))
    return _pallas(
        body, name=name, grid=(T // tT,),
        in_specs=[pl.BlockSpec((tT, 5 * C), lambda i: (i, 0)),
                  pl.BlockSpec((HL, 5 * C), lambda i: (jnp.maximum(i * hb - 1, 0), 0)),
                  pl.BlockSpec((None, 32, C), lambda i: (sl, 0, 0)), vec, vec, vec,
                  pl.BlockSpec((4, POOL_GW, POOL_GW), lambda i: (0, 0, 0)), vec, vec],
        out_specs=[pl.BlockSpec((tT, 2 * C), lambda i: (i, 0)), pl.BlockSpec((tT, C), lambda i: (i, 0))],
        out_shape=[jax.ShapeDtypeStruct((T, 2 * C), BF16), jax.ShapeDtypeStruct((T, C), F32)],
        scratch_shapes=[pltpu.VMEM((HL + tT + 8, C), F32), pltpu.VMEM((8, HL + tT, C), F32),
                        pltpu.VMEM((32, 8, C), F32)],
        compiler_params=_params("arbitrary"))(p, p, cw, cb, lg, lb, pw, pb, sc)


def _even_mixer_bwd(p, u1, dout, w_out, after, sl, cwr, lg, lb, pw, pb, sc, name):
    T = p.shape[0]
    C = D_MODEL
    tT, HL = MIX_TILE, EVEN_HALO
    hb = tT // HL
    nT = T // tT
    R1 = tT + HL
    chunk = 32

    def body(pm_ref, pp_ref, pn_ref, u1m_ref, u1n_ref, dom_ref, don_ref, wo_ref, after_ref, cwr_ref, lg_ref, lb_ref,
             pw_ref, pb_ref, sc_ref, dp_ref, dcw_ref, dvec_ref, dpw_ref, x_ref, sh_ref, du0_ref, wb_ref):
        i = pl.program_id(0)
        dy = _nt(jnp.concatenate([dom_ref[...], don_ref[...]], axis=0), wo_ref[...])

        @pl.when(i == 0)
        def _():
            _fill_taps(wb_ref, cwr_ref)

        keep_prev = (i > 0).astype(F32)
        keep_next = (i < nT - 1).astype(F32)
        row = lax.broadcasted_iota(jnp.int32, (R1, 1), 0)
        live = jnp.where(row < tT, 1.0, keep_next)

        def cat(m, n):
            return jnp.concatenate([m, n], axis=0)

        u1 = cat(u1m_ref[...], u1n_ref[...])
        mu = jnp.mean(u1, axis=-1, keepdims=True)
        xc = u1 - mu
        rs = lax.rsqrt(jnp.mean(xc * xc, axis=-1, keepdims=True) + EPS_LN)
        xh = xc * rs
        u2 = xh * lg_ref[...] + lb_ref[...]
        s2 = _sigmoid(u2)
        u3 = u2 * s2
        ag = cat(pm_ref[:, 2 * C:3 * C], pn_ref[:, 2 * C:3 * C])
        sa = _sigmoid(ag)
        dya = dy[:, 0:C]
        dp_ref[:, 2 * C:3 * C] = (dya * u3 * _dsilu(ag, sa))[0:tT].astype(BF16)
        du2 = dya * (ag * sa) * _dsilu(u2, s2)
        dlg = jnp.sum((du2 * xh)[0:tT], axis=0, keepdims=True)
        dlb = jnp.sum(du2[0:tT], axis=0, keepdims=True)
        dxh = du2 * lg_ref[...]
        du1 = rs * (dxh - jnp.mean(dxh, axis=-1, keepdims=True) - xh * jnp.mean(dxh * xh, axis=-1, keepdims=True))
        du1 = du1 * live
        dcb = jnp.sum(du1[0:tT], axis=0, keepdims=True)
        x_ref[0:R1] = du1
        x_ref[R1:R1 + 8] = jnp.zeros((8, C), F32)
        _fill_shifted(sh_ref, x_ref)

        def du0_chunk(c, carry):
            r0 = pl.multiple_of(c * chunk, chunk)
            du0_ref[pl.ds(r0, chunk), :] = _tap_sum(sh_ref, wb_ref, r0, chunk, list(range(CONV_K)))
            return carry

        lax.fori_loop(0, tT // chunk, du0_chunk, 0)
        av, agl = pm_ref[:, 0:C], pm_ref[:, C:2 * C]
        sg = _sigmoid(agl)
        du0 = du0_ref[...]
        dp_ref[:, 0:C] = (du0 * sg).astype(BF16)
        dp_ref[:, C:2 * C] = (du0 * av * sg * (1.0 - sg)).astype(BF16)
        du0_ref[...] = du1[0:tT]
        x_ref[0:HL] = pp_ref[:, 0:C] * _sigmoid(pp_ref[:, C:2 * C]) * keep_prev
        x_ref[HL:HL + tT] = av * sg
        x_ref[HL + tT:HL + tT + 8] = jnp.zeros((8, C), F32)
        _fill_shifted(sh_ref, x_ref)

        @pl.when(i == 0)
        def _():
            dcw_ref[...] = jnp.zeros_like(dcw_ref)

        for k0 in range(0, CONV_K, 2):
            taps = [k for k in (k0, k0 + 1) if k < CONV_K]
            offs = [HL - (CONV_K - 1) + k for k in taps]

            def dw_chunk(c, accs, offs=offs):
                r0 = pl.multiple_of(c * 64, 64)
                accs = list(accs)
                for u in range(0, 64, 8):
                    d = du0_ref[pl.ds(r0 + u, 8), :]
                    for t, o in enumerate(offs):
                        accs[t] = accs[t] + d * sh_ref[o % 8, pl.ds(r0 + u + (o // 8) * 8, 8), :]
                return tuple(accs)

            sums = lax.fori_loop(0, tT // 64, dw_chunk, tuple(jnp.zeros((8, C), F32) for _ in taps))
            for k, acc in zip(taps, sums):
                dcw_ref[8 * k:8 * k + 8, :] += acc

        bg = cat(pm_ref[:, 4 * C:5 * C], pn_ref[:, 4 * C:5 * C])
        sb = _sigmoid(bg)
        dyb = dy[:, C:2 * C]
        dyb0 = dyb * (bg * sb)
        dd1 = dyb0 * sc_ref[...]
        dpb = jnp.sum(dd1[0:tT], axis=0, keepdims=True)
        inv1 = _inv_count(i * tT, R1)
        z_parts, dd0_parts = [], []
        for gi in range(len(POOL_WINDOWS)):
            cols = slice(gi * POOL_GW, (gi + 1) * POOL_GW)
            dd0 = _nt(dd1[:, cols].astype(BF16), pw_ref[gi])
            dd0_parts.append(dd0)
            z_parts.append(dd0 * inv1[gi] * live)
        fsum = _pool_sums(jnp.concatenate(z_parts, axis=1), up=True)
        vx = cat(pp_ref[:, 3 * C:4 * C] * keep_prev, pm_ref[:, 3 * C:4 * C])
        sums = _pool_sums(vx, up=False)
        inv0 = _inv_count(i * tT, tT)
        dsc_parts = []
        for gi in range(len(POOL_WINDOWS)):
            cols = slice(gi * POOL_GW, (gi + 1) * POOL_GW)
            dp_ref[:, 3 * C + gi * POOL_GW:3 * C + (gi + 1) * POOL_GW] = (fsum[gi][0:tT] - dd0_parts[gi][0:tT]).astype(BF16)
            d0 = (sums[gi][HL:] * inv0[gi] - vx[HL:, cols]).astype(BF16)
            d1 = jnp.dot(d0, pw_ref[gi], preferred_element_type=F32) + pb_ref[:, cols]
            bgm, sbm = bg[0:tT, cols], sb[0:tT, cols]
            dp_ref[:, 4 * C + gi * POOL_GW:4 * C + (gi + 1) * POOL_GW] = (
                dyb[0:tT, cols] * d1 * sc_ref[:, cols] * _dsilu(bgm, sbm)).astype(BF16)
            dsc_parts.append(jnp.sum(dyb0[0:tT, cols] * d1, axis=0, keepdims=True))
            dpw_g = _tn(d0, dd1[0:tT, cols].astype(BF16))

            @pl.when(i == 0)
            def _(gi=gi, dpw_g=dpw_g):
                dpw_ref[gi] = dpw_g

            @pl.when(i > 0)
            def _(gi=gi, dpw_g=dpw_g):
                dpw_ref[gi] += dpw_g

        dsc = jnp.concatenate(dsc_parts, axis=1)
        vecs = jnp.concatenate([dcb, dlg, dlb, dsc, dpb, jnp.zeros((3, C), F32)], axis=0)

        @pl.when(i == 0)
        def _():
            dvec_ref[...] = vecs

        @pl.when(i > 0)
        def _():
            dvec_ref[...] += vecs

    vec = pl.BlockSpec((None, 1, C), lambda i: (sl, 0, 0))
    taps = pl.BlockSpec((None, 32, C), lambda i: (sl, 0, 0))

    def prev_blk(i):
        return (jnp.maximum(i * hb - 1, 0), 0)

    def next_blk(i):
        return (jnp.minimum((i + 1) * hb, T // HL - 1), 0)

    return _pallas(
        body, name=name, grid=(nT,),
        in_specs=[pl.BlockSpec((tT, 5 * C), lambda i: (i, 0)), pl.BlockSpec((HL, 5 * C), prev_blk),
                  pl.BlockSpec((HL, 5 * C), next_blk),
                  pl.BlockSpec((tT, C), lambda i: (i, 0)), pl.BlockSpec((HL, C), next_blk),
                  pl.BlockSpec((tT, C), lambda i: (i, 0)), pl.BlockSpec((HL, C), next_blk),
                  pl.BlockSpec((None, 2 * C, C), lambda i: (0, 0, 0)), pl.BlockSpec((8, 128), lambda i: (0, 0)),
                  taps, vec, vec, pl.BlockSpec((4, POOL_GW, POOL_GW), lambda i: (0, 0, 0)), vec, vec],
        out_specs=[pl.BlockSpec((tT, 5 * C), lambda i: (i, 0)), pl.BlockSpec((32 * 8, C), lambda i: (0, 0)),
                   pl.BlockSpec((8, C), lambda i: (0, 0)), pl.BlockSpec((4, POOL_GW, POOL_GW), lambda i: (0, 0, 0))],
        out_shape=[jax.ShapeDtypeStruct((T, 5 * C), BF16), jax.ShapeDtypeStruct((32 * 8, C), F32),
                   jax.ShapeDtypeStruct((8, C), F32), jax.ShapeDtypeStruct((4, POOL_GW, POOL_GW), F32)],
        scratch_shapes=[pltpu.VMEM((R1 + 8, C), F32), pltpu.VMEM((8, R1, C), F32), pltpu.VMEM((tT, C), F32),
                        pltpu.VMEM((32, 8, C), F32)],
        compiler_params=_params("arbitrary"))(p, p, p, u1, u1, dout, dout, w_out, after, cwr, lg, lb, pw, pb, sc)


def _softplus(z):
    u = jnp.exp(-jnp.abs(z))
    w = 1.0 + u
    l1p = jnp.where(w == 1.0, u, u * jnp.log(w) / jnp.where(w == 1.0, 1.0, w - 1.0))
    return jnp.maximum(z, 0.0) + l1p


def _lru_gates(xrx, cw_ref, cb_ref, wr_ref, br_ref, wi_ref, bi_ref, lam_ref):
    HL = ODD_HALO
    xc = cb_ref[...] + cw_ref[LRU_CONV_K - 1:LRU_CONV_K, :] * xrx[HL:]
    for k in range(LRU_CONV_K - 1):
        xc = xc + cw_ref[k:k + 1, :] * _shift_down(xrx, LRU_CONV_K - 1 - k)[HL:]
    xcb = xc.astype(BF16)
    rp, ip = [], []
    for hd in range(LRU_HEADS):
        cols = slice(hd * LRU_HD, (hd + 1) * LRU_HD)
        rp.append(jnp.dot(xcb[:, cols], wr_ref[hd], preferred_element_type=F32))
        ip.append(jnp.dot(xcb[:, cols], wi_ref[hd], preferred_element_type=F32))
    r = _sigmoid(jnp.concatenate(rp, axis=1) + br_ref[...])
    ig = _sigmoid(jnp.concatenate(ip, axis=1) + bi_ref[...])
    sp = _softplus(-lam_ref[...])
    log_a = (-LRU_C) * r * sp
    a = jnp.exp(log_a)
    m2 = jnp.maximum(-jnp.tanh(log_a) * (a * a + 1.0), 1e-30)
    inv_mult = lax.rsqrt(m2)
    return xc, xcb, r, ig, sp, a, m2 * inv_mult, inv_mult


def _group_scan(a, b, reverse):
    n, w = a.shape
    a, b = a.reshape(n // 8, 8, w), b.reshape(n // 8, 8, w)
    pos = lax.broadcasted_iota(jnp.int32, (1, 8, 1), 1)
    s = 1
    while s < 8:
        ok = (pos < 8 - s) if reverse else (pos >= s)
        shift = (8 - s) if reverse else s
        a_sh = jnp.where(ok, pltpu.roll(a, shift, 1), 1.0)
        b_sh = jnp.where(ok, pltpu.roll(b, shift, 1), 0.0)
        b = a * b_sh + b
        a = a * a_sh
        s *= 2
    return a.reshape(n, w), b.reshape(n, w)


def _apply_carries(a_ref, b_ref, out_ref, c0, reverse):
    ng = a_ref.shape[0] // 8

    def step(t, c):
        r0 = pl.multiple_of(((ng - 1 - t) if reverse else t) * 8, 8)
        x = a_ref[pl.ds(r0, 8), :] * c + b_ref[pl.ds(r0, 8), :]
        out_ref[pl.ds(r0, 8), :] = x
        return x[0:1, :] if reverse else x[7:8, :]

    return lax.fori_loop(0, ng, step, c0)


def _odd_mixer_fwd(p, sl, cw, cb, wr, br, wi, bi, lam, name):
    T = p.shape[0]
    W = W_LRU
    tT, HL = MIX_TILE, ODD_HALO
    hb = tT // HL

    def body(pm_ref, ph_ref, cw_ref, cb_ref, wr_ref, br_ref, wi_ref, bi_ref, lam_ref, y_ref, hs_ref, carry_ref,
             sa_ref, sb_ref):
        i = pl.program_id(0)
        keep = (i > 0).astype(F32)

        @pl.when(i == 0)
        def _():
            carry_ref[...] = jnp.zeros_like(carry_ref)

        xrx = jnp.concatenate([ph_ref[:, 0:W] * keep, pm_ref[:, 0:W]], axis=0)
        xc, _, _, ig, _, a, mult, _ = _lru_gates(xrx, cw_ref, cb_ref, wr_ref, br_ref, wi_ref, bi_ref, lam_ref)
        sa_ref[...], sb_ref[...] = _group_scan(a, mult * (ig * xc), reverse=False)
        last = _apply_carries(sa_ref, sb_ref, hs_ref, carry_ref[0:1, :], reverse=False)
        carry_ref[...] = jnp.broadcast_to(last, (8, W))
        hs = hs_ref[...]
        gt = pm_ref[:, W:2 * W]
        y_ref[...] = (hs * (gt * _sigmoid(gt))).astype(BF16)

    vec = pl.BlockSpec((None, 1, W), lambda i: (sl, 0, 0))
    heads = pl.BlockSpec((None, LRU_HEADS, LRU_HD, LRU_HD), lambda i: (sl, 0, 0, 0))
    return _pallas(
        body, name=name, grid=(T // tT,),
        in_specs=[pl.BlockSpec((tT, 2 * W), lambda i: (i, 0)),
                  pl.BlockSpec((HL, 2 * W), lambda i: (jnp.maximum(i * hb - 1, 0), 0)),
                  pl.BlockSpec((None, 8, W), lambda i: (sl, 0, 0)), vec, heads, vec, heads, vec, vec],
        out_specs=[pl.BlockSpec((tT, W), lambda i: (i, 0)), pl.BlockSpec((tT, W), lambda i: (i, 0))],
        out_shape=[jax.ShapeDtypeStruct((T, W), BF16), jax.ShapeDtypeStruct((T, W), F32)],
        scratch_shapes=[pltpu.VMEM((8, W), F32), pltpu.VMEM((tT, W), F32), pltpu.VMEM((tT, W), F32)],
        compiler_params=_params("arbitrary"))(p, p, cw, cb, wr, br, wi, bi, lam)


def _odd_mixer_bwd(p, hs, dout, w_out, after, sl, cw, cb, wr, br, wi, bi, lam, name):
    T = p.shape[0]
    W = W_LRU
    D = dout.shape[1]
    tT, HL = MIX_TILE, ODD_HALO
    hb = tT // HL
    nT = T // tT

    def body(pm_ref, ph_ref, hsm_ref, hsh_ref, do_ref, wo_ref, after_ref, cw_ref, cb_ref, wr_ref, br_ref, wi_ref,
             bi_ref, lam_ref, dp_ref, dwr_ref, dwi_ref, dvec_ref, gcarry_ref, xcarry_ref, sa_ref, sb_ref, g_ref):
        i = pl.program_id(0)
        keep = (i < nT - 1).astype(F32)

        @pl.when(i == 0)
        def _():
            gcarry_ref[...] = jnp.zeros_like(gcarry_ref)
            xcarry_ref[...] = jnp.zeros_like(xcarry_ref)

        xrx = jnp.concatenate([ph_ref[:, 0:W] * keep, pm_ref[:, 0:W]], axis=0)
        xc, xcb, r, ig, sp, a, mult, inv_mult = _lru_gates(xrx, cw_ref, cb_ref, wr_ref, br_ref, wi_ref, bi_ref, lam_ref)
        hs = hsm_ref[...]
        gt = pm_ref[:, W:2 * W]
        sg = _sigmoid(gt)
        dyv = _nt(do_ref[...], wo_ref[...])
        dp_ref[:, W:2 * W] = (dyv * hs * _dsilu(gt, sg)).astype(BF16)
        row = lax.broadcasted_iota(jnp.int32, (tT, 1), 0)
        m = jnp.where(row == tT - 1, 1.0, _shift_up(a, 1))
        sa_ref[...], sb_ref[...] = _group_scan(m, dyv * (gt * sg), reverse=True)
        first = _apply_carries(sa_ref, sb_ref, g_ref, gcarry_ref[0:1, :], reverse=True)
        G = g_ref[...]
        gcarry_ref[...] = jnp.broadcast_to(a[0:1, :] * first, (8, W))
        hs_prev = jnp.where(row == 0, hsh_ref[HL - 1:HL, :] * keep, _shift_down(hs, 1))
        da = G * hs_prev
        dmult = G * (ig * xc)
        di = G * mult * xc
        dxc = G * mult * ig
        dlog_a = da * a - dmult * (a * a) * inv_mult
        drp = dlog_a * ((-LRU_C) * sp) * r * (1.0 - r)
        dip = di * ig * (1.0 - ig)
        dlam = jnp.sum(dlog_a * ((-LRU_C) * r), axis=0, keepdims=True) * (-_sigmoid(-lam_ref[...]))
        drb, dib = drp.astype(BF16), dip.astype(BF16)
        back = []
        for hd in range(LRU_HEADS):
            cols = slice(hd * LRU_HD, (hd + 1) * LRU_HD)
            back.append(_nt(drb[:, cols], wr_ref[hd]) + _nt(dib[:, cols], wi_ref[hd]))
            dwr_h = _tn(xcb[:, cols], drb[:, cols])
            dwi_h = _tn(xcb[:, cols], dib[:, cols])

            @pl.when(i == 0)
            def _(hd=hd, dwr_h=dwr_h, dwi_h=dwi_h):
                dwr_ref[hd] = dwr_h
                dwi_ref[hd] = dwi_h

            @pl.when(i > 0)
            def _(hd=hd, dwr_h=dwr_h, dwi_h=dwi_h):
                dwr_ref[hd] += dwr_h
                dwi_ref[hd] += dwi_h

        dxc = dxc + jnp.concatenate(back, axis=1)
        dxcx = jnp.concatenate([dxc, xcarry_ref[...]], axis=0)
        dxr = cw_ref[LRU_CONV_K - 1:LRU_CONV_K, :] * dxc
        rows = []
        for k in range(LRU_CONV_K - 1):
            j = LRU_CONV_K - 1 - k
            dxr = dxr + cw_ref[k:k + 1, :] * _shift_up(dxcx, j)[0:tT]
            rows.append(jnp.sum(dxc * _shift_down(xrx, j)[HL:], axis=0, keepdims=True))
        rows.append(jnp.sum(dxc * xrx[HL:], axis=0, keepdims=True))
        dp_ref[:, 0:W] = dxr.astype(BF16)
        xcarry_ref[...] = dxc[0:8]
        rows += [jnp.sum(dxc, axis=0, keepdims=True), jnp.sum(drp, axis=0, keepdims=True),
                 jnp.sum(dip, axis=0, keepdims=True), dlam]
        vecs = jnp.concatenate(rows, axis=0)

        @pl.when(i == 0)
        def _():
            dvec_ref[...] = vecs

        @pl.when(i > 0)
        def _():
            dvec_ref[...] += vecs

    vec = pl.BlockSpec((None, 1, W), lambda i: (sl, 0, 0))
    heads = pl.BlockSpec((None, LRU_HEADS, LRU_HD, LRU_HD), lambda i: (sl, 0, 0, 0))
    dheads = pl.BlockSpec((LRU_HEADS, LRU_HD, LRU_HD), lambda i: (0, 0, 0))

    def tile(i):
        return (nT - 1 - i, 0)

    def prev_blk(i):
        return (jnp.maximum((nT - 1 - i) * hb - 1, 0), 0)

    return _pallas(
        body, name=name, grid=(nT,),
        in_specs=[pl.BlockSpec((tT, 2 * W), tile), pl.BlockSpec((HL, 2 * W), prev_blk),
                  pl.BlockSpec((tT, W), tile), pl.BlockSpec((HL, W), prev_blk), pl.BlockSpec((tT, D), tile),
                  pl.BlockSpec((None, W, D), lambda i: (0, 0, 0)), pl.BlockSpec((8, 128), lambda i: (0, 0)),
                  pl.BlockSpec((None, 8, W), lambda i: (sl, 0, 0)), vec, heads, vec, heads, vec, vec],
        out_specs=[pl.BlockSpec((tT, 2 * W), tile), dheads, dheads, pl.BlockSpec((8, W), lambda i: (0, 0))],
        out_shape=[jax.ShapeDtypeStruct((T, 2 * W), BF16), jax.ShapeDtypeStruct((LRU_HEADS, LRU_HD, LRU_HD), F32),
                   jax.ShapeDtypeStruct((LRU_HEADS, LRU_HD, LRU_HD), F32), jax.ShapeDtypeStruct((8, W), F32)],
        scratch_shapes=[pltpu.VMEM((8, W), F32), pltpu.VMEM((8, W), F32), pltpu.VMEM((tT, W), F32),
                        pltpu.VMEM((tT, W), F32), pltpu.VMEM((tT, W), F32)],
        compiler_params=_params("arbitrary"))(p, p, hs, hs, dout, w_out, after, cw, cb, wr, br, wi, bi, lam)


def _pad_rows(a, rows):
    return jnp.pad(a, ((0, 0), (0, rows - a.shape[1]), (0, 0)))


def _layer_fwd(even, h, w, w_in, w_out, after):
    sl = w["sl"]
    p, n = _in_proj(h, w["norm"], sl, w_in, 0, after, "in_proj_even" if even else "in_proj_odd")
    if even:
        y, aux = _even_mixer_fwd(p, sl, w["conv_w"], w["conv_b"], w["ln_g"], w["ln_b"], w["pool_w"], w["pool_b"],
                                 w["pool_scale"], "even_mixer_fwd")
    else:
        y, aux = _odd_mixer_fwd(p, sl, w["conv_w"], w["conv_b"], w["w_rg"], w["b_rg"], w["w_ig"], w["b_ig"], w["lam"],
                                "odd_mixer_fwd")
    if callable(w_out):
        w_out = w_out(y)
    h_next = _out_proj(y, w_out, 0, h, "out_proj_even" if even else "out_proj_odd")
    return h_next, (h, n, p, aux, y), w_out


def _layer_bwd_weights(even, saved, w, w_out, dhb, after):
    h, n, p, aux, y = saved
    if even:
        dp, dcw, dvec, dpw = _even_mixer_bwd(p, aux, dhb, w_out, after, w["sl"], w["conv_w_rev"], w["ln_g"], w["ln_b"],
                                             w["pool_w"], w["pool_b"], w["pool_scale"], "even_mixer_bwd")
        dw_out = _dw_out(y, dhb, 0, 1, None, "dw_out_even")
        dw_in = _dw_in(n, dp, N_CHIPS, 0, 1, None, "dw_in_even")
        return dp, dw_in, dw_out, dict(conv_w=dcw, vec=dvec, pool_w=dpw)
    dp, dwr, dwi, dvec = _odd_mixer_bwd(p, aux, dhb, w_out, after, w["sl"], w["conv_w"], w["conv_b"], w["w_rg"],
                                        w["b_rg"], w["w_ig"], w["b_ig"], w["lam"], "odd_mixer_bwd")
    dw_out = _dw_out(y, dhb, 0, 1, None, "dw_out_odd")
    dw_in = _dw_in(n, dp, N_CHIPS, 0, 1, None, "dw_in_odd")
    return dp, dw_in, dw_out, dict(w_rg=dwr, w_ig=dwi, vec=dvec)


def _layer_bwd_input(even, saved, w, w_in, dp, dh, after):
    return _dn_proj(dp, w_in, 0, saved[0], w["norm"], w["sl"], dh, after, "dn_proj_even" if even else "dn_proj_odd")


ANY = pl.BlockSpec(memory_space=pl.ANY)


def _mesh_pos():
    return lax.axis_index("x"), lax.axis_index("y"), lax.axis_index("c")


def _other_chips(x, y):
    return [(1 - x, y), (x, 1 - y), (1 - x, 1 - y)]


def _other_devices(x, y, c):
    out = []
    for p in range(1, N_DEV):
        out.append((1 - x if p & 4 else x, 1 - y if p & 2 else y, 1 - c if p & 1 else c))
    return out


def _remote(src, dst, ssem, rsem, dev):
    return pltpu.make_async_remote_copy(src_ref=src, dst_ref=dst, send_sem=ssem, recv_sem=rsem, device_id=dev,
                                        device_id_type=MESH)


def _comm_call(body, name, ins, out_shape, scratch, aliases=None):
    return _pallas(body, name=name, in_specs=[ANY] * len(ins), out_specs=[ANY] * len(out_shape), out_shape=out_shape,
                   scratch_shapes=scratch, input_output_aliases=aliases or {},
                   compiler_params=pltpu.CompilerParams(has_side_effects=True))(*ins)


def _cast_shard(w, layer, pos):
    _, R, C = w.shape
    tr = _row_tile(R, C)

    def body(pos_ref, w_ref, o_ref):
        o_ref[...] = w_ref[...].astype(BF16)

    grid_spec = pltpu.PrefetchScalarGridSpec(
        num_scalar_prefetch=1, grid=(R // tr,),
        in_specs=[pl.BlockSpec((None, tr, C), lambda i, pr: (layer, i, 0))],
        out_specs=pl.BlockSpec((None, None, tr, C), lambda i, pr: (0, pr[0], i, 0)))
    return _pallas(body, name="cast_shard", grid_spec=grid_spec,
                   out_shape=jax.ShapeDtypeStruct((1, N_CHIPS, R, C), BF16),
                   compiler_params=_params("parallel"))(pos, w)


def _gather_weights(big, small):
    nA = len(big)
    half = [a.shape[2] // 2 for a in big]

    def body(*refs):
        ins, outs = refs[:nA + 1], refs[nA + 1:2 * nA + 2]
        ssem, rsem, fsem, frsem, lsem = refs[2 * nA + 2:]
        x, y, c = _mesh_pos()
        k = 2 * x + y
        chips = _other_chips(x, y)
        sib = (x, y, 1 - c)

        def slab(a, chip, core):
            return outs[a].at[:, chip, pl.ds(core * half[a], half[a]), :]

        local = [pltpu.make_async_copy(ins[nA], outs[nA].at[k], lsem.at[0])]
        for cp in local:
            cp.start()
        sends = []
        for j, (ox, oy) in enumerate(chips):
            for a in range(nA):
                sends.append(_remote(slab(a, k, c), slab(a, k, c), ssem.at[a, j], rsem.at[a, j], (ox, oy, c)))
            sends.append(_remote(ins[nA], outs[nA].at[k], ssem.at[nA, j], rsem.at[nA, j], (ox, oy, c)))
        for cp in sends:
            cp.start()
        for j, (ox, oy) in enumerate(chips):
            kj = 2 * ox + oy
            for a in range(nA):
                got = slab(a, kj, c)
                _remote(got, got, ssem.at[a, j], rsem.at[a, j], (ox, oy, c)).wait_recv()
                fw = _remote(got, got, fsem.at[a, j], frsem.at[a, j], sib)
                fw.start()
                sends.append(fw)
            gs = outs[nA].at[kj]
            _remote(gs, gs, ssem.at[nA, j], rsem.at[nA, j], (ox, oy, c)).wait_recv()
        for j, (ox, oy) in enumerate(chips):
            kj = 2 * ox + oy
            for a in range(nA):
                theirs = slab(a, kj, 1 - c)
                _remote(theirs, theirs, fsem.at[a, j], frsem.at[a, j], sib).wait_recv()
        for cp in sends:
            cp.wait_send()
        for cp in local:
            cp.wait()

    out_shape = [jax.ShapeDtypeStruct(a.shape, a.dtype) for a in big]
    out_shape.append(jax.ShapeDtypeStruct((N_CHIPS,) + small.shape, small.dtype))
    scratch = [pltpu.SemaphoreType.DMA((nA + 1, 3)), pltpu.SemaphoreType.DMA((nA + 1, 3)),
               pltpu.SemaphoreType.DMA((nA, 3)), pltpu.SemaphoreType.DMA((nA, 3)), pltpu.SemaphoreType.DMA((1,))]
    return _comm_call(body, "gather_weights", list(big) + [small], out_shape, scratch, {a: a for a in range(nA)})


HBM = pl.BlockSpec(memory_space=pltpu.HBM)
SEM = pl.BlockSpec(memory_space=pltpu.SEMAPHORE)
EFFECT = pltpu.SideEffectType.DATAFLOW_SIDE_EFFECTING


def _split_start(arrays, copies, n, name):
    k = len(arrays)

    def body(*refs):
        for cp in copies(refs[k + 2:2 * k + 2], refs[k], refs[k + 1]):
            cp.start()
        refs[2 * k + 2][...] = jnp.zeros((8, 128), F32)

    out = _pallas(
        body, name=name,
        out_shape=(pltpu.SemaphoreType.DMA((n,)), pltpu.SemaphoreType.DMA((n,)),
                   *[pltpu.HBM(a.shape, a.dtype) for a in arrays], jax.ShapeDtypeStruct((8, 128), F32)),
        in_specs=(HBM,) * k, out_specs=(SEM, SEM) + (HBM,) * k + (pl.BlockSpec(memory_space=pltpu.VMEM),),
        input_output_aliases={i: i + 2 for i in range(k)},
        compiler_params=pltpu.CompilerParams(has_side_effects=EFFECT),
    )(*[pltpu.with_memory_space_constraint(a, pltpu.HBM) for a in arrays])
    return out[0], out[1], list(out[2:2 + k]), out[2 + k]


def _split_wait(ssem, rsem, arrays, copies, after, name):
    k = len(arrays)

    def body(*refs):
        for cp in copies(refs[:k], refs[k], refs[k + 1]):
            cp.wait_send()
            cp.wait_recv()

    out = _pallas(
        body, name=name, out_shape=tuple(pltpu.HBM(a.shape, a.dtype) for a in arrays),
        in_specs=(HBM,) * k + (SEM, SEM, ANY), out_specs=(HBM,) * k, input_output_aliases={i: i for i in range(k)},
        compiler_params=pltpu.CompilerParams(has_side_effects=EFFECT),
    )(*arrays, ssem, rsem, after)
    return list(out)


def _gather_copies(shapes):
    half = [s[2] // 2 for s in shapes]

    def copies(refs, ssem, rsem):
        x, y, c = _mesh_pos()
        out = []
        for j, (ox, oy) in enumerate(_other_chips(x, y)):
            for a, ref in enumerate(refs):
                slab = ref.at[:, 2 * x + y, pl.ds(c * half[a], half[a]), :]
                out.append(_remote(slab, slab, ssem.at[3 * a + j], rsem.at[3 * a + j], (ox, oy, c)))
        return out

    return copies


def _chips_copies(n_arr):
    def copies(refs, ssem, rsem):
        x, y, c = _mesh_pos()
        out = []
        for j, (ox, oy) in enumerate(_other_chips(x, y)):
            for a in range(n_arr):
                out.append(_remote(refs[a].at[:, 2 * ox + oy], refs[n_arr + a].at[:, 2 * x + y], ssem.at[3 * a + j],
                                   rsem.at[3 * a + j], (ox, oy, c)))
        return out

    return copies


def _halves_copies(shapes):
    n = len(shapes)
    half = [s[2] // 2 for s in shapes]

    def copies(refs, ssem, rsem):
        x, y, c = _mesh_pos()
        return [_remote(refs[a].at[:, :, pl.ds((1 - c) * half[a], half[a]), :], refs[n + a], ssem.at[a], rsem.at[a],
                        (x, y, 1 - c)) for a in range(n)]

    return copies


def _forward_cores(arrays):
    nA = len(arrays)
    half = [a.shape[2] // 2 for a in arrays]

    def body(*refs):
        outs = refs[nA:2 * nA]
        ssem, rsem = refs[2 * nA:]
        x, y, c = _mesh_pos()
        sib = (x, y, 1 - c)
        sends, waits = [], []
        for j, (ox, oy) in enumerate(_other_chips(x, y)):
            for a in range(nA):
                got = outs[a].at[:, 2 * ox + oy, pl.ds(c * half[a], half[a]), :]
                sends.append(_remote(got, got, ssem.at[a, j], rsem.at[a, j], sib))
                theirs = outs[a].at[:, 2 * ox + oy, pl.ds((1 - c) * half[a], half[a]), :]
                waits.append(_remote(theirs, theirs, ssem.at[a, j], rsem.at[a, j], sib))
        for cp in sends:
            cp.start()
        for cp in waits:
            cp.wait_recv()
        for cp in sends:
            cp.wait_send()

    out_shape = [jax.ShapeDtypeStruct(a.shape, a.dtype) for a in arrays]
    scratch = [pltpu.SemaphoreType.DMA((nA, 3)), pltpu.SemaphoreType.DMA((nA, 3))]
    return _comm_call(body, "forward_cores", list(arrays), out_shape, scratch, {a: a for a in range(nA)})


def _exchange_halves(big):
    nA = len(big)
    half = [a.shape[2] // 2 for a in big]

    def body(*refs):
        ins, outs = refs[:nA], refs[nA:2 * nA]
        ssem, rsem = refs[2 * nA:]
        x, y, c = _mesh_pos()
        sib = (x, y, 1 - c)
        sends = [_remote(ins[a].at[:, :, pl.ds((1 - c) * half[a], half[a]), :], outs[a], ssem.at[a], rsem.at[a], sib)
                 for a in range(nA)]
        for cp in sends:
            cp.start()
        for a in range(nA):
            _remote(outs[a], outs[a], ssem.at[a], rsem.at[a], sib).wait_recv()
        for cp in sends:
            cp.wait_send()

    out_shape = [jax.ShapeDtypeStruct((a.shape[0], N_CHIPS, h, a.shape[3]), a.dtype) for a, h in zip(big, half)]
    scratch = [pltpu.SemaphoreType.DMA((nA,)), pltpu.SemaphoreType.DMA((nA,))]
    return _comm_call(body, "exchange_halves", list(big), out_shape, scratch)


def _exchange_final(grads, everywhere, small):
    nA = len(grads)
    n_remote = sum(N_DEV - 1 if ev else 1 for ev in everywhere) + N_DEV - 1

    def body(*refs):
        small_ref, outs, gathered = refs[nA], refs[nA + 1:2 * nA + 1], refs[2 * nA + 1]
        ssem, rsem, lsem = refs[2 * nA + 2:]
        x, y, c = _mesh_pos()
        k = 2 * x + y
        me = 2 * k + c
        sib = (x, y, 1 - c)
        peers = _other_devices(x, y, c)
        local = pltpu.make_async_copy(small_ref, gathered.at[me], lsem.at[0])
        local.start()
        sends, waits = [], []
        s = 0
        for (px, py, pc) in peers:
            sends.append(_remote(small_ref, gathered.at[me], ssem.at[s], rsem.at[s], (px, py, pc)))
            got = gathered.at[4 * px + 2 * py + pc]
            waits.append(_remote(got, got, ssem.at[s], rsem.at[s], (px, py, pc)))
            s += 1
        for a in range(nA):
            if everywhere[a]:
                r2 = grads[a].shape[1] // N_DEV
                mine = outs[a].at[:, pl.ds((2 * k + c) * r2, r2), :]
                for (px, py, pc) in peers:
                    sends.append(_remote(mine, mine, ssem.at[s], rsem.at[s], (px, py, pc)))
                    got = outs[a].at[:, pl.ds((2 * (2 * px + py) + pc) * r2, r2), :]
                    waits.append(_remote(got, got, ssem.at[s], rsem.at[s], (px, py, pc)))
                    s += 1
            else:
                r2 = grads[a].shape[1] // 2
                mine = outs[a].at[:, pl.ds(c * r2, r2), :]
                sends.append(_remote(mine, mine, ssem.at[s], rsem.at[s], sib))
                got = outs[a].at[:, pl.ds((1 - c) * r2, r2), :]
                waits.append(_remote(got, got, ssem.at[s], rsem.at[s], sib))
                s += 1
        for cp in sends:
            cp.start()
        for cp in waits:
            cp.wait_recv()
        for cp in sends:
            cp.wait_send()
        local.wait()

    out_shape = [jax.ShapeDtypeStruct(g.shape, g.dtype) for g in grads]
    out_shape.append(jax.ShapeDtypeStruct((N_DEV,) + small.shape, small.dtype))
    scratch = [pltpu.SemaphoreType.DMA((n_remote,)), pltpu.SemaphoreType.DMA((n_remote,)), pltpu.SemaphoreType.DMA((1,))]
    return _comm_call(body, "exchange_final", list(grads) + [small], out_shape, scratch, {a: a for a in range(nA)})


BLOCK_BYTES = 4 << 20


def _row_tile(rows, cols, mult=16, limit=BLOCK_BYTES):
    best = mult
    for t in range(mult, rows + 1, mult):
        if rows % t == 0 and t * cols * 4 <= limit:
            best = t
    return best


def _add_cores(own, recv, pos):
    L, _, R, C = own.shape
    r2 = R // 2
    tr = _row_tile(r2, C)
    nb = r2 // tr

    def body(pos_ref, a_ref, r_ref, o_ref):
        o_ref[...] = (a_ref[...].astype(F32) + r_ref[...].astype(F32)).astype(BF16)

    blk = (None, None, tr, C)
    grid_spec = pltpu.PrefetchScalarGridSpec(
        num_scalar_prefetch=1, grid=(L, N_CHIPS, nb),
        in_specs=[pl.BlockSpec(blk, lambda l, s, i, pr: (l, s, pr[1] * nb + i, 0)),
                  pl.BlockSpec(blk, lambda l, s, i, pr: (l, s, i, 0))],
        out_specs=pl.BlockSpec(blk, lambda l, s, i, pr: (l, s, i, 0)))
    return _pallas(body, name="add_cores", grid_spec=grid_spec,
                   out_shape=jax.ShapeDtypeStruct((L, N_CHIPS, r2, C), BF16),
                   compiler_params=_params("parallel", "parallel", "parallel"))(pos, own, recv)


def _sum_chips(own, recv, pos, everywhere, layer, nlayers, prev):
    _, _, r2, C = own.shape
    tr = _row_tile(r2, 2 * C)
    nb = r2 // tr

    def body(pos_ref, a_ref, r_ref, *rest):
        acc = None
        for s in range(N_CHIPS):
            term = jnp.where(pos_ref[0] == s, a_ref[...], r_ref[s]).astype(F32)
            acc = term if acc is None else acc + term
        rest[-1][...] = acc

    if everywhere:
        def out_map(i, pr):
            return (layer, (2 * pr[0] + pr[1]) * nb + i, 0)
    else:
        def out_map(i, pr):
            return (layer, pr[1] * nb + i, 0)

    in_specs = [pl.BlockSpec((None, None, tr, C), lambda i, pr: (0, pr[0], i, 0)),
                pl.BlockSpec((None, N_CHIPS, tr, C), lambda i, pr: (0, 0, i, 0))]
    grid_spec = pltpu.PrefetchScalarGridSpec(
        num_scalar_prefetch=1, grid=(nb,), in_specs=in_specs + ([] if prev is None else [ANY]),
        out_specs=pl.BlockSpec((None, tr, C), out_map))
    rows = (N_DEV if everywhere else 2) * r2
    args = (pos, own, recv) if prev is None else (pos, own, recv, prev)
    return _pallas(body, name="sum_chips", grid_spec=grid_spec, out_shape=jax.ShapeDtypeStruct((nlayers, rows, C), F32),
                   input_output_aliases={} if prev is None else {3: 0},
                   compiler_params=_params("parallel"))(*args)


def _sum_devices(parts):
    n, R, C = parts.shape
    tr = _row_tile(R, C * n, 8)

    def body(p_ref, o_ref):
        acc = p_ref[0]
        for s in range(1, n):
            acc = acc + p_ref[s]
        o_ref[...] = acc

    return _pallas(body, name="sum_devices", grid=(R // tr,), in_specs=[pl.BlockSpec((n, tr, C), lambda i: (0, i, 0))],
                   out_specs=pl.BlockSpec((tr, C), lambda i: (i, 0)), out_shape=jax.ShapeDtypeStruct((R, C), F32),
                   compiler_params=_params("parallel"))(parts)


def _adamw(w, g, m, v, name):
    L, R, C = w.shape
    tr = _row_tile(R, C, 8, BLOCK_BYTES // 2)

    def body(w_ref, g_ref, m_ref, v_ref, d_ref, m2_ref, v2_ref):
        gg = g_ref[...]
        m2 = ADAM_B1 * m_ref[...] + (1.0 - ADAM_B1) * gg
        v2 = ADAM_B2 * v_ref[...] + (1.0 - ADAM_B2) * (gg * gg)
        m_hat = m2 / (1.0 - ADAM_B1 ** ADAM_STEP)
        v_hat = v2 / (1.0 - ADAM_B2 ** ADAM_STEP)
        d_ref[...] = -ADAM_LR * (m_hat / (jnp.sqrt(v_hat) + ADAM_EPS) + ADAM_WD * w_ref[...])
        m2_ref[...] = m2
        v2_ref[...] = v2

    blk = pl.BlockSpec((1, tr, C), lambda l, i: (l, i, 0))
    shp = jax.ShapeDtypeStruct((L, R, C), F32)
    return _pallas(body, name=name, grid=(L, R // tr), in_specs=[blk] * 4, out_specs=[blk] * 3, out_shape=[shp] * 3,
                   compiler_params=_params("parallel", "parallel"))(w, g, m, v)


WEIGHTS = ("norm_even", "w_in_even", "conv_a_w", "conv_a_b", "ln_a_g", "ln_a_b", "pool_w", "pool_b", "pool_scale",
           "w_out_even", "norm_odd", "w_in_odd", "conv_c_w", "conv_c_b", "w_rg", "b_rg", "w_ig", "b_ig", "lru_lambda",
           "w_out_odd", "final_norm")
BIG = ("w_in_even", "w_out_even", "pool_w", "w_in_odd", "w_out_odd", "w_rg", "w_ig")
SMALL = tuple(n for n in WEIGHTS if n not in BIG)
SMALL_SHARDED = ("conv_a_w", "pool_b", "norm_odd", "conv_c_w", "conv_c_b", "b_rg", "b_ig", "lru_lambda")


def _pack(arrs):
    flat = jnp.concatenate([a.reshape(-1) for a in arrs])
    rows = -(-flat.shape[0] // (64 * 128)) * 64
    return jnp.pad(flat, (0, rows * 128 - flat.shape[0])).reshape(rows, 128)


def _unpack(buf, shapes, lead=()):
    flat = buf.reshape(tuple(lead) + (-1,))
    out, o = [], 0
    for s in shapes:
        n = 1
        for d in s:
            n *= d
        out.append(flat[..., o:o + n].reshape(tuple(lead) + tuple(s)))
        o += n
    return out


def _shard(full, axis, k):
    n = full.shape[axis] // N_CHIPS
    return lax.dynamic_slice_in_dim(full, k * n, n, axis)


def kernel(x, norm_even, w_in_even, conv_a_w, conv_a_b, ln_a_g, ln_a_b, pool_w, pool_b, pool_scale, w_out_even, norm_odd, w_in_odd, conv_c_w, conv_c_b, w_rg, b_rg, w_ig, b_ig, lru_lambda, w_out_odd, final_norm, loss_target, m_norm_even, m_w_in_even, m_conv_a_w, m_conv_a_b, m_ln_a_g, m_ln_a_b, m_pool_w, m_pool_b, m_pool_scale, m_w_out_even, m_norm_odd, m_w_in_odd, m_conv_c_w, m_conv_c_b, m_w_rg, m_b_rg, m_w_ig, m_b_ig, m_lru_lambda, m_w_out_odd, m_final_norm, v_norm_even, v_w_in_even, v_conv_a_w, v_conv_a_b, v_ln_a_g, v_ln_a_b, v_pool_w, v_pool_b, v_pool_scale, v_w_out_even, v_norm_odd, v_w_in_odd, v_conv_c_w, v_conv_c_b, v_w_rg, v_b_rg, v_w_ig, v_b_ig, v_lru_lambda, v_w_out_odd, v_final_norm):
    P = dict(locals())
    xi, yi, ci = _mesh_pos()
    k = 2 * xi + yi
    L = w_in_even.shape[0]
    D = D_MODEL

    pos = jnp.stack([k, ci]).astype(jnp.int32)
    depth = 2 * L
    pool_w3 = pool_w.reshape(L, 4 * 64, POOL_GW)

    def cast_group(layer):
        j = layer // 2
        if layer % 2 == 0:
            return [_cast_shard(w_in_even, j, pos), _cast_shard(w_out_even, j, pos), _cast_shard(pool_w3, j, pos)]
        return [_cast_shard(w_in_odd, j, pos), _cast_shard(w_out_odd, j, pos)]

    first = cast_group(0)
    g_in, g_pool, g_small = _gather_weights([first[0], first[2]], _pack([P[n] for n in SMALL_SHARDED]))
    copies0 = _gather_copies([first[1].shape])
    ssem0, rsem0, late, token0 = _split_start([first[1]], copies0, 3, "gather_start0")

    def late_w_out(y):
        return _forward_cores(_split_wait(ssem0, rsem0, late, copies0, y, "gather_wait0"))[0].reshape(1, -1, D)

    group = [g_in, late_w_out, g_pool]
    full = {}
    for n, a in zip(SMALL_SHARDED, _unpack(g_small, [P[n].shape for n in SMALL_SHARDED], lead=(N_CHIPS,))):
        a = jnp.moveaxis(a, 0, -2)
        full[n] = a.reshape(a.shape[:-2] + (N_CHIPS * a.shape[-1],))

    small_even = dict(norm=norm_even[:, None], conv_w=_pad_rows(full["conv_a_w"], 32),
                      conv_w_rev=_pad_rows(full["conv_a_w"][:, ::-1], 32), conv_b=conv_a_b[:, None], ln_g=ln_a_g[:, None],
                      ln_b=ln_a_b[:, None], pool_b=full["pool_b"].reshape(L, 1, D), pool_scale=pool_scale[:, None])
    small_odd = dict(norm=full["norm_odd"][:, None], conv_w=_pad_rows(full["conv_c_w"], 8),
                     conv_b=full["conv_c_b"][:, None], w_rg=w_rg.astype(BF16), b_rg=full["b_rg"][:, None],
                     w_ig=w_ig.astype(BF16), b_ig=full["b_ig"][:, None], lam=full["lru_lambda"][:, None])

    def small_weights(layer, group):
        if layer % 2 == 0:
            pw = group[2].reshape(N_CHIPS, 4, 64, POOL_GW).transpose(1, 0, 2, 3).reshape(4, POOL_GW, POOL_GW)
            return dict(small_even, sl=layer // 2, pool_w=pw)
        return dict(small_odd, sl=layer // 2)

    no_token = jnp.zeros((8, 128), F32)
    h = x[0]
    saved, big_w, small_w = [], [], []
    for layer in range(depth):
        token = token0 if layer == 0 else no_token
        if layer + 1 < depth:
            nxt = cast_group(layer + 1)
            copies = _gather_copies([a.shape for a in nxt])
            ssem, rsem, nxt, token = _split_start(nxt, copies, 3 * len(nxt), "gather_start%d" % (layer + 1))
        small_w.append(small_weights(layer, group))
        w_out = group[1] if callable(group[1]) else group[1].reshape(1, -1, D)
        h, sv, w_out = _layer_fwd(layer % 2 == 0, h, small_w[layer], group[0], w_out, token)
        big_w.append((group[0], w_out))
        saved.append(sv)
        if layer + 1 < depth:
            group = _forward_cores(_split_wait(ssem, rsem, nxt, copies, h, "gather_wait%d" % (layer + 1)))

    dh, dhb, d_final, loss = _loss_head(h, final_norm[None], loss_target[0])
    loss = lax.psum(loss[0, 0], ("x", "y", "c"))
    everywhere = [False, False, False, False, False, True, True]
    final = [None] * len(everywhere)
    small_of = [None] * depth

    def finish(pending, after):
        ssem, rsem, arrs, copies, slots, pj, pl_ = pending
        arrs = _split_wait(ssem, rsem, arrs, copies, after, "chips_wait%d" % pl_)
        for a, r, s in zip(arrs[:len(slots)], arrs[len(slots):], slots):
            final[s] = _sum_chips(a, r, pos, everywhere[s], pj, L, final[s])

    pending = None
    token = no_token
    for layer in reversed(range(depth)):
        j = layer // 2
        even_layer = layer % 2 == 0
        dp, dw_in, dw_out, sm = _layer_bwd_weights(even_layer, saved[layer], small_w[layer], big_w[layer][1], dhb, token)
        if even_layer:
            dpw = sm["pool_w"].reshape(4, N_CHIPS, 64, POOL_GW).transpose(1, 0, 2, 3)
            parts = [dw_in, dw_out.reshape(1, N_CHIPS, -1, D), dpw.reshape(1, N_CHIPS, 4 * 64, POOL_GW).astype(BF16)]
            slots = [0, 1, 2]
        else:
            parts = [dw_in, dw_out.reshape(1, N_CHIPS, -1, D),
                     sm["w_rg"].reshape(1, N_CHIPS, -1, LRU_HD).astype(BF16),
                     sm["w_ig"].reshape(1, N_CHIPS, -1, LRU_HD).astype(BF16)]
            slots = [3, 4, 5, 6]
        n = len(parts)
        if layer > 0:
            hcopies = _halves_copies([a.shape for a in parts])
            hland = [lax.empty((1, N_CHIPS, a.shape[2] // 2, a.shape[3]), a.dtype) for a in parts]
            hs, hr, harrs, htoken = _split_start(parts + hland, hcopies, n, "halves_start%d" % layer)
            dh, dhb, sm["norm"] = _layer_bwd_input(even_layer, saved[layer], small_w[layer], big_w[layer][0], dp, dh,
                                                   htoken)
            harrs = _split_wait(hs, hr, harrs, hcopies, dh, "halves_wait%d" % layer)
            parts, recv = harrs[:n], harrs[n:]
        else:
            recv = _exchange_halves(parts)
        pair = [_add_cores(a, r, pos) for a, r in zip(parts, recv)]
        copies = _chips_copies(n)
        land = [lax.empty(a.shape, a.dtype) for a in pair]
        ssem, rsem, arrs, token = _split_start(pair + land, copies, 3 * n, "chips_start%d" % layer)
        if layer == 0:
            dh, dhb, sm["norm"] = _layer_bwd_input(even_layer, saved[layer], small_w[layer], big_w[layer][0], dp, dh, token)
        small_of[layer] = sm
        if pending is not None:
            finish(pending, dh)
        pending = (ssem, rsem, arrs, copies, slots, j, layer)
    grad_x = dh
    small_g = []
    for jj in range(L):
        ge, go = small_of[2 * jj], small_of[2 * jj + 1]
        small_g += [ge["conv_w"].reshape(32, 8, D).sum(axis=1)[:CONV_K], ge["vec"][0:5], ge["norm"], go["vec"], go["norm"]]
    small_g.append(d_final)
    small_shapes = [a.shape for a in small_g]
    packed_small = _pack(small_g)
    finish(pending, packed_small)
    *gw, recv_small = _exchange_final(final, everywhere, packed_small)
    sg = _unpack(_sum_devices(recv_small), small_shapes)

    grads = dict(w_in_even=gw[0], w_out_even=gw[1], pool_w=gw[2].reshape(pool_w.shape), w_in_odd=gw[3], w_out_odd=gw[4],
                 w_rg=gw[5].reshape(w_rg.shape), w_ig=gw[6].reshape(w_ig.shape), final_norm=sg[-1][0])
    ev = [sg[5 * j + 1] for j in range(L)]
    ov = [sg[5 * j + 3] for j in range(L)]
    grads["conv_a_w"] = _shard(jnp.stack([sg[5 * j] for j in range(L)]), 2, k)
    grads["norm_even"] = jnp.stack([sg[5 * j + 2][0] for j in range(L)])
    grads["norm_odd"] = _shard(jnp.stack([sg[5 * j + 4][0] for j in range(L)]), 1, k)
    for r, n in enumerate(("conv_a_b", "ln_a_g", "ln_a_b", "pool_scale")):
        grads[n] = jnp.stack([e[r] for e in ev])
    grads["pool_b"] = _shard(jnp.stack([e[4].reshape(4, POOL_GW) for e in ev]), 2, k)
    grads["conv_c_w"] = _shard(jnp.stack([o[0:4] for o in ov]), 2, k)
    for r, n in zip((4, 5, 6, 7), ("conv_c_b", "b_rg", "b_ig", "lru_lambda")):
        grads[n] = _shard(jnp.stack([o[r] for o in ov]), 1, k)

    delta, new_m, new_v = {}, {}, {}
    for n in BIG:
        s3 = (L, -1, P[n].shape[-1])
        d, m2, v2 = _adamw(P[n].reshape(s3), grads[n].reshape(s3), P["m_" + n].reshape(s3), P["v_" + n].reshape(s3), "adamw")
        delta[n], new_m[n], new_v[n] = d.reshape(P[n].shape), m2.reshape(P[n].shape), v2.reshape(P[n].shape)
    shapes = [P[n].shape for n in SMALL]
    packed = [_pack([src[n] for n in SMALL])[None] for src in
              (P, grads, {n: P["m_" + n] for n in SMALL}, {n: P["v_" + n] for n in SMALL})]
    for res, out in zip(_adamw(*packed, "adamw_small"), (delta, new_m, new_v)):
        for n, a in zip(SMALL, _unpack(res[0], shapes)):
            out[n] = a

    return (loss, grad_x[None], *[grads[n] for n in WEIGHTS], *[delta[n] for n in WEIGHTS],
            *[new_m[n] for n in WEIGHTS], *[new_v[n] for n in WEIGHTS])
```

```python
import functools

import jax
import jax.numpy as jnp
from jax import lax
from jax.experimental import pallas as pl
from jax.experimental.pallas import tpu as pltpu

F32 = jnp.float32
BF16 = jnp.bfloat16
MESH = pl.DeviceIdType.MESH

D_MODEL = 1024
N_CHIPS = 4
N_DEV = 8
EPS_RMS = 1e-6
EPS_LN = 1e-5
CONV_K = 31
POOL_WINDOWS = (2, 4, 8, 16)
POOL_GW = 256
LRU_HEADS = 12
LRU_HD = 128
W_LRU = LRU_HEADS * LRU_HD
LRU_CONV_K = 4
LRU_C = 8.0
ADAM_LR = 0.001
ADAM_B1 = 0.9
ADAM_B2 = 0.999
ADAM_EPS = 1e-08
ADAM_WD = 0.01
ADAM_STEP = 10

VMEM_LIMIT_BYTES = 56 * 1024 * 1024
ROW_TILE = 512
MIX_TILE = 256
EVEN_HALO = 32
ODD_HALO = 8


def _pallas(body, **kw):
    return pl.pallas_call(body, **kw)


def _params(*sem):
    return pltpu.CompilerParams(dimension_semantics=sem if sem else None, vmem_limit_bytes=VMEM_LIMIT_BYTES)


def _sigmoid(x):
    return 0.5 * jnp.tanh(0.5 * x) + 0.5


def _dsilu(x, s):
    return s * (1.0 + x * (1.0 - s))


def _nt(a, b):
    return lax.dot_general(a, b, (((1,), (1,)), ((), ())), preferred_element_type=F32)


def _tn(a, b):
    return lax.dot_general(a, b, (((0,), (0,)), ((), ())), preferred_element_type=F32)


def _in_proj(h, g, glayer, wg, layer, after, name):
    T, D = h.shape
    _, nblk, _, nb = wg.shape

    nrow = T // ROW_TILE

    def body(h_ref, g_ref, w_ref, after_ref, p_ref, n_ref, n_all):
        j, i = pl.program_id(0), pl.program_id(1)

        @pl.when(j == 0)
        def _():
            x = h_ref[...]
            r = lax.rsqrt(jnp.mean(x * x, axis=-1, keepdims=True) + EPS_RMS)
            nn = (x * r * g_ref[...]).astype(BF16)
            n_ref[...] = nn
            n_all[i] = nn

        p_ref[...] = jnp.dot(n_all[i], w_ref[0], preferred_element_type=F32)

    def rows_once(j, i):
        return (jnp.where(j == 0, i, nrow - 1), 0)

    return _pallas(
        body, name=name, grid=(nblk, nrow),
        in_specs=[pl.BlockSpec((ROW_TILE, D), rows_once), pl.BlockSpec((None, 1, D), lambda j, i: (glayer, 0, 0)),
                  pl.BlockSpec((None, 1, D, nb), lambda j, i: (layer, j, 0, 0)),
                  pl.BlockSpec((8, 128), lambda j, i: (0, 0))],
        out_specs=[pl.BlockSpec((ROW_TILE, nb), lambda j, i: (i, j)), pl.BlockSpec((ROW_TILE, D), rows_once)],
        out_shape=[jax.ShapeDtypeStruct((T, nblk * nb), F32), jax.ShapeDtypeStruct((T, D), BF16)],
        scratch_shapes=[pltpu.VMEM((nrow, ROW_TILE, D), BF16)],
        compiler_params=_params("arbitrary", "arbitrary"))(h, g, wg, after)


def _out_proj(y, w, layer, hres, name):
    T, K = y.shape
    D = w.shape[2]

    def body(y_ref, w_ref, r_ref, o_ref):
        o_ref[...] = r_ref[...] + jnp.dot(y_ref[...], w_ref[...], preferred_element_type=F32)

    return _pallas(
        body, name=name, grid=(T // ROW_TILE,),
        in_specs=[pl.BlockSpec((ROW_TILE, K), lambda i: (i, 0)), pl.BlockSpec((None, K, D), lambda i: (layer, 0, 0)),
                  pl.BlockSpec((ROW_TILE, D), lambda i: (i, 0))],
        out_specs=pl.BlockSpec((ROW_TILE, D), lambda i: (i, 0)),
        out_shape=jax.ShapeDtypeStruct((T, D), F32),
        compiler_params=_params("parallel"))(y, w, hres)


def _dn_proj(dp, wg, layer, h, g, glayer, dres, after, name):
    T, D = h.shape
    _, nblk, _, nb = wg.shape

    nrow = T // ROW_TILE

    def body(dp_ref, w_ref, h_ref, g_ref, dres_ref, after_ref, dh_ref, dhb_ref, dg_ref, acc_ref):
        j, i = pl.program_id(0), pl.program_id(1)
        part = _nt(dp_ref[...], w_ref[0])

        @pl.when(j == 0)
        def _():
            acc_ref[i] = part

        @pl.when(j > 0)
        def _():
            acc_ref[i] += part

        @pl.when(j == nblk - 1)
        def _():
            x = h_ref[...]
            r = lax.rsqrt(jnp.mean(x * x, axis=-1, keepdims=True) + EPS_RMS)
            dn = acc_ref[i]
            q = dn * g_ref[...]
            dh = dres_ref[...] + r * q - x * ((r * r * r) * jnp.mean(q * x, axis=-1, keepdims=True))
            dh_ref[...] = dh
            dhb_ref[...] = dh.astype(BF16)
            dgp = jnp.sum(dn * (x * r), axis=0, keepdims=True)

            @pl.when(i == 0)
            def _():
                dg_ref[...] = dgp

            @pl.when(i > 0)
            def _():
                dg_ref[...] += dgp

    def rows_last(j, i):
        return (jnp.where(j == nblk - 1, i, 0), 0)

    return _pallas(
        body, name=name, grid=(nblk, nrow),
        in_specs=[pl.BlockSpec((ROW_TILE, nb), lambda j, i: (i, j)),
                  pl.BlockSpec((None, 1, D, nb), lambda j, i: (layer, j, 0, 0)),
                  pl.BlockSpec((ROW_TILE, D), rows_last), pl.BlockSpec((None, 1, D), lambda j, i: (glayer, 0, 0)),
                  pl.BlockSpec((ROW_TILE, D), rows_last), pl.BlockSpec((8, 128), lambda j, i: (0, 0))],
        out_specs=[pl.BlockSpec((ROW_TILE, D), rows_last), pl.BlockSpec((ROW_TILE, D), rows_last),
                   pl.BlockSpec((1, D), lambda j, i: (0, 0))],
        out_shape=[jax.ShapeDtypeStruct((T, D), F32), jax.ShapeDtypeStruct((T, D), BF16),
                   jax.ShapeDtypeStruct((1, D), F32)],
        scratch_shapes=[pltpu.VMEM((nrow, ROW_TILE, D), F32)],
        compiler_params=_params("arbitrary", "arbitrary"))(dp, wg, h, g, dres, after)


def _dw_in(n, dp, nblk, layer, nlayers, prev, name):
    T, D = n.shape
    nb = dp.shape[1] // nblk
    ta = D

    def body(n_ref, dp_ref, *rest):
        rest[-1][0] = _tn(n_ref[...], dp_ref[...]).astype(BF16)

    in_specs = [pl.BlockSpec((T, ta), lambda j, i: (0, i)), pl.BlockSpec((T, nb), lambda j, i: (0, j))]
    args = (n, dp) if prev is None else (n, dp, prev)
    return _pallas(
        body, name=name, grid=(nblk, D // ta), in_specs=in_specs + ([] if prev is None else [ANY]),
        out_specs=pl.BlockSpec((None, 1, ta, nb), lambda j, i: (layer, j, i, 0)),
        out_shape=jax.ShapeDtypeStruct((nlayers, nblk, D, nb), BF16),
        input_output_aliases={} if prev is None else {2: 0},
        compiler_params=_params("parallel", "parallel"))(*args)


def _dw_out(y, dout, layer, nlayers, prev, name):
    T, K = y.shape
    D = dout.shape[1]
    tk = 512

    def body(y_ref, d_ref, *rest):
        rest[-1][...] = _tn(y_ref[...], d_ref[...]).astype(BF16)

    in_specs = [pl.BlockSpec((T, tk), lambda i: (0, i)), pl.BlockSpec((T, D), lambda i: (0, 0))]
    args = (y, dout) if prev is None else (y, dout, prev)
    return _pallas(
        body, name=name, grid=(K // tk,), in_specs=in_specs + ([] if prev is None else [ANY]),
        out_specs=pl.BlockSpec((None, tk, D), lambda i: (layer, i, 0)),
        out_shape=jax.ShapeDtypeStruct((nlayers, K, D), BF16),
        input_output_aliases={} if prev is None else {2: 0},
        compiler_params=_params("parallel"))(*args)


def _loss_head(h, g, tgt):
    T, D = h.shape
    tm = MIX_TILE

    def body(h_ref, g_ref, t_ref, dh_ref, dhb_ref, dg_ref, loss_ref):
        i = pl.program_id(0)
        x = h_ref[...]
        gg = g_ref[...]
        r = lax.rsqrt(jnp.mean(x * x, axis=-1, keepdims=True) + EPS_RMS)
        xr = x * r
        e = xr * gg - t_ref[...]
        lp = 0.5 * jnp.sum(jnp.mean(e * e, axis=-1, keepdims=True), axis=0, keepdims=True)
        dn = e * (1.0 / D)
        q = dn * gg
        dh = r * q - x * ((r * r * r) * jnp.mean(q * x, axis=-1, keepdims=True))
        dh_ref[...] = dh
        dhb_ref[...] = dh.astype(BF16)
        dgp = jnp.sum(dn * xr, axis=0, keepdims=True)

        @pl.when(i == 0)
        def _():
            dg_ref[...] = dgp
            loss_ref[...] = lp

        @pl.when(i > 0)
        def _():
            dg_ref[...] += dgp
            loss_ref[...] += lp

    return _pallas(
        body, name="loss_head", grid=(T // tm,),
        in_specs=[pl.BlockSpec((tm, D), lambda i: (i, 0)), pl.BlockSpec((1, D), lambda i: (0, 0)),
                  pl.BlockSpec((tm, D), lambda i: (i, 0))],
        out_specs=[pl.BlockSpec((tm, D), lambda i: (i, 0)), pl.BlockSpec((tm, D), lambda i: (i, 0)),
                   pl.BlockSpec((1, D), lambda i: (0, 0)), pl.BlockSpec((1, 1), lambda i: (0, 0))],
        out_shape=[jax.ShapeDtypeStruct((T, D), F32), jax.ShapeDtypeStruct((T, D), BF16),
                   jax.ShapeDtypeStruct((1, D), F32), jax.ShapeDtypeStruct((1, 1), F32)],
        compiler_params=_params("arbitrary"))(h, g, tgt)


def _shift_up(x, j):
    return x if j == 0 else pltpu.roll(x, x.shape[0] - j, 0)


def _shift_down(x, j):
    return x if j == 0 else pltpu.roll(x, j, 0)


def _fill_shifted(dst_ref, src_ref):
    rows = dst_ref.shape[1]
    for s in range(8):
        dst_ref[s] = src_ref[pl.ds(s, rows), :]


def _fill_taps(wb_ref, w_ref):
    for k in range(w_ref.shape[0]):
        wb_ref[k] = jnp.broadcast_to(w_ref[k:k + 1, :], wb_ref.shape[1:])


def _tap_sum(sh_ref, wb_ref, r0, nrows, offsets):
    accs = [None] * (nrows // 8)
    for k, o in enumerate(offsets):
        wk = wb_ref[k]
        for u in range(nrows // 8):
            term = wk * sh_ref[o % 8, pl.ds(r0 + (o // 8) * 8 + 8 * u, 8), :]
            accs[u] = term if accs[u] is None else accs[u] + term
    return jnp.concatenate(accs, axis=0)


def _pool_sums(vx, up):
    sh = _shift_up if up else _shift_down
    outs = []
    for gi, w in enumerate(POOL_WINDOWS):
        s = vx[:, gi * POOL_GW:(gi + 1) * POOL_GW]
        j = 1
        while j < w:
            s = s + sh(s, j)
            j *= 2
        outs.append(s)
    return outs


def _inv_count(row0, nrows):
    pos = (row0 + 1 + lax.broadcasted_iota(jnp.int32, (nrows, 1), 0)).astype(F32)
    return [1.0 / jnp.minimum(pos, float(w)) for w in POOL_WINDOWS]


def _even_mixer_fwd(p, sl, cw, cb, lg, lb, pw, pb, sc, name):
    T = p.shape[0]
    C = D_MODEL
    tT, HL = MIX_TILE, EVEN_HALO
    hb = tT // HL
    chunk = 32

    def body(pm_ref, ph_ref, cw_ref, cb_ref, lg_ref, lb_ref, pw_ref, pb_ref, sc_ref, y_ref, u1_ref, u0x_ref, sh_ref,
             wb_ref):
        i = pl.program_id(0)
        keep = (i > 0).astype(F32)

        @pl.when(i == 0)
        def _():
            _fill_taps(wb_ref, cw_ref)

        u0x_ref[0:HL] = ph_ref[:, 0:C] * _sigmoid(ph_ref[:, C:2 * C]) * keep
        u0x_ref[HL:HL + tT] = pm_ref[:, 0:C] * _sigmoid(pm_ref[:, C:2 * C])
        u0x_ref[HL + tT:HL + tT + 8] = jnp.zeros((8, C), F32)
        _fill_shifted(sh_ref, u0x_ref)
        offs = [HL - (CONV_K - 1) + k for k in range(CONV_K)]

        def conv_chunk(c, carry):
            r0 = pl.multiple_of(c * chunk, chunk)
            u1_ref[pl.ds(r0, chunk), :] = _tap_sum(sh_ref, wb_ref, r0, chunk, offs) + cb_ref[...]
            return carry

        lax.fori_loop(0, tT // chunk, conv_chunk, 0)
        u1 = u1_ref[...]
        mu = jnp.mean(u1, axis=-1, keepdims=True)
        xc = u1 - mu
        rs = lax.rsqrt(jnp.mean(xc * xc, axis=-1, keepdims=True) + EPS_LN)
        u2 = xc * rs * lg_ref[...] + lb_ref[...]
        u3 = u2 * _sigmoid(u2)
        ag = pm_ref[:, 2 * C:3 * C]
        y_ref[:, 0:C] = (u3 * (ag * _sigmoid(ag))).astype(BF16)
        vx = jnp.concatenate([ph_ref[:, 3 * C:4 * C] * keep, pm_ref[:, 3 * C:4 * C]], axis=0)
        sums = _pool_sums(vx, up=False)
        inv = _inv_count(i * tT, tT)
        for gi in range(len(POOL_WINDOWS)):
            cols = slice(gi * POOL_GW, (gi + 1) * POOL_GW)
            d0 = sums[gi][HL:] * inv[gi] - vx[HL:, cols]
            d1 = jnp.dot(d0.astype(BF16), pw_ref[gi], preferred_element_type=F32) + pb_ref[:, cols]
            bg = pm_ref[:, 4 * C + gi * POOL_GW:4 * C + (gi + 1) * POOL_GW]
            y_ref[:, C + gi * POOL_GW:C + (gi + 1) * POOL_GW] = (d1 * sc_ref[:, cols] * (bg * _sigmoid(bg))).astype(BF16)

    vec = pl.BlockSpec((None, 1, C), lambda i: (sl, 0, 0))
    return _pallas(
        body, name=name, grid=(T // tT,),
        in_specs=[pl.BlockSpec((tT, 5 * C), lambda i: (i, 0)),
                  pl.BlockSpec((HL, 5 * C), lambda i: (jnp.maximum(i * hb - 1, 0), 0)),
                  pl.BlockSpec((None, 32, C), lambda i: (sl, 0, 0)), vec, vec, vec,
                  pl.BlockSpec((4, POOL_GW, POOL_GW), lambda i: (0, 0, 0)), vec, vec],
        out_specs=[pl.BlockSpec((tT, 2 * C), lambda i: (i, 0)), pl.BlockSpec((tT, C), lambda i: (i, 0))],
        out_shape=[jax.ShapeDtypeStruct((T, 2 * C), BF16), jax.ShapeDtypeStruct((T, C), F32)],
        scratch_shapes=[pltpu.VMEM((HL + tT + 8, C), F32), pltpu.VMEM((8, HL + tT, C), F32),
                        pltpu.VMEM((32, 8, C), F32)],
        compiler_params=_params("arbitrary"))(p, p, cw, cb, lg, lb, pw, pb, sc)


def _even_mixer_bwd(p, u1, dout, w_out, after, sl, cwr, lg, lb, pw, pb, sc, name):
    T = p.shape[0]
    C = D_MODEL
    tT, HL = MIX_TILE, EVEN_HALO
    hb = tT // HL
    nT = T // tT
    R1 = tT + HL
    chunk = 32

    def body(pm_ref, pp_ref, pn_ref, u1m_ref, u1n_ref, dom_ref, don_ref, wo_ref, after_ref, cwr_ref, lg_ref, lb_ref,
             pw_ref, pb_ref, sc_ref, dp_ref, dcw_ref, dvec_ref, dpw_ref, x_ref, sh_ref, du0_ref, wb_ref):
        i = pl.program_id(0)
        dy = _nt(jnp.concatenate([dom_ref[...], don_ref[...]], axis=0), wo_ref[...])

        @pl.when(i == 0)
        def _():
            _fill_taps(wb_ref, cwr_ref)

        keep_prev = (i > 0).astype(F32)
        keep_next = (i < nT - 1).astype(F32)
        row = lax.broadcasted_iota(jnp.int32, (R1, 1), 0)
        live = jnp.where(row < tT, 1.0, keep_next)

        def cat(m, n):
            return jnp.concatenate([m, n], axis=0)

        u1 = cat(u1m_ref[...], u1n_ref[...])
        mu = jnp.mean(u1, axis=-1, keepdims=True)
        xc = u1 - mu
        rs = lax.rsqrt(jnp.mean(xc * xc, axis=-1, keepdims=True) + EPS_LN)
        xh = xc * rs
        u2 = xh * lg_ref[...] + lb_ref[...]
        s2 = _sigmoid(u2)
        u3 = u2 * s2
        ag = cat(pm_ref[:, 2 * C:3 * C], pn_ref[:, 2 * C:3 * C])
        sa = _sigmoid(ag)
        dya = dy[:, 0:C]
        dp_ref[:, 2 * C:3 * C] = (dya * u3 * _dsilu(ag, sa))[0:tT].astype(BF16)
        du2 = dya * (ag * sa) * _dsilu(u2, s2)
        dlg = jnp.sum((du2 * xh)[0:tT], axis=0, keepdims=True)
        dlb = jnp.sum(du2[0:tT], axis=0, keepdims=True)
        dxh = du2 * lg_ref[...]
        du1 = rs * (dxh - jnp.mean(dxh, axis=-1, keepdims=True) - xh * jnp.mean(dxh * xh, axis=-1, keepdims=True))
        du1 = du1 * live
        dcb = jnp.sum(du1[0:tT], axis=0, keepdims=True)
        x_ref[0:R1] = du1
        x_ref[R1:R1 + 8] = jnp.zeros((8, C), F32)
        _fill_shifted(sh_ref, x_ref)

        def du0_chunk(c, carry):
            r0 = pl.multiple_of(c * chunk, chunk)
            du0_ref[pl.ds(r0, chunk), :] = _tap_sum(sh_ref, wb_ref, r0, chunk, list(range(CONV_K)))
            return carry

        lax.fori_loop(0, tT // chunk, du0_chunk, 0)
        av, agl = pm_ref[:, 0:C], pm_ref[:, C:2 * C]
        sg = _sigmoid(agl)
        du0 = du0_ref[...]
        dp_ref[:, 0:C] = (du0 * sg).astype(BF16)
        dp_ref[:, C:2 * C] = (du0 * av * sg * (1.0 - sg)).astype(BF16)
        du0_ref[...] = du1[0:tT]
        x_ref[0:HL] = pp_ref[:, 0:C] * _sigmoid(pp_ref[:, C:2 * C]) * keep_prev
        x_ref[HL:HL + tT] = av * sg
        x_ref[HL + tT:HL + tT + 8] = jnp.zeros((8, C), F32)
        _fill_shifted(sh_ref, x_ref)

        @pl.when(i == 0)
        def _():
            dcw_ref[...] = jnp.zeros_like(dcw_ref)

        for k0 in range(0, CONV_K, 2):
            taps = [k for k in (k0, k0 + 1) if k < CONV_K]
            offs = [HL - (CONV_K - 1) + k for k in taps]

            def dw_chunk(c, accs, offs=offs):
                r0 = pl.multiple_of(c * 64, 64)
                accs = list(accs)
                for u in range(0, 64, 8):
                    d = du0_ref[pl.ds(r0 + u, 8), :]
                    for t, o in enumerate(offs):
                        accs[t] = accs[t] + d * sh_ref[o % 8, pl.ds(r0 + u + (o // 8) * 8, 8), :]
                return tuple(accs)

            sums = lax.fori_loop(0, tT // 64, dw_chunk, tuple(jnp.zeros((8, C), F32) for _ in taps))
            for k, acc in zip(taps, sums):
                dcw_ref[8 * k:8 * k + 8, :] += acc

        bg = cat(pm_ref[:, 4 * C:5 * C], pn_ref[:, 4 * C:5 * C])
        sb = _sigmoid(bg)
        dyb = dy[:, C:2 * C]
        dyb0 = dyb * (bg * sb)
        dd1 = dyb0 * sc_ref[...]
        dpb = jnp.sum(dd1[0:tT], axis=0, keepdims=True)
        inv1 = _inv_count(i * tT, R1)
        z_parts, dd0_parts = [], []
        for gi in range(len(POOL_WINDOWS)):
            cols = slice(gi * POOL_GW, (gi + 1) * POOL_GW)
            dd0 = _nt(dd1[:, cols].astype(BF16), pw_ref[gi])
            dd0_parts.append(dd0)
            z_parts.append(dd0 * inv1[gi] * live)
        fsum = _pool_sums(jnp.concatenate(z_parts, axis=1), up=True)
        vx = cat(pp_ref[:, 3 * C:4 * C] * keep_prev, pm_ref[:, 3 * C:4 * C])
        sums = _pool_sums(vx, up=False)
        inv0 = _inv_count(i * tT, tT)
        dsc_parts = []
        for gi in range(len(POOL_WINDOWS)):
            cols = slice(gi * POOL_GW, (gi + 1) * POOL_GW)
            dp_ref[:, 3 * C + gi * POOL_GW:3 * C + (gi + 1) * POOL_GW] = (fsum[gi][0:tT] - dd0_parts[gi][0:tT]).astype(BF16)
            d0 = (sums[gi][HL:] * inv0[gi] - vx[HL:, cols]).astype(BF16)
            d1 = jnp.dot(d0, pw_ref[gi], preferred_element_type=F32) + pb_ref[:, cols]
            bgm, sbm = bg[0:tT, cols], sb[0:tT, cols]
            dp_ref[:, 4 * C + gi * POOL_GW:4 * C + (gi + 1) * POOL_GW] = (
                dyb[0:tT, cols] * d1 * sc_ref[:, cols] * _dsilu(bgm, sbm)).astype(BF16)
            dsc_parts.append(jnp.sum(dyb0[0:tT, cols] * d1, axis=0, keepdims=True))
            dpw_g = _tn(d0, dd1[0:tT, cols].astype(BF16))

            @pl.when(i == 0)
            def _(gi=gi, dpw_g=dpw_g):
                dpw_ref[gi] = dpw_g

            @pl.when(i > 0)
            def _(gi=gi, dpw_g=dpw_g):
                dpw_ref[gi] += dpw_g

        dsc = jnp.concatenate(dsc_parts, axis=1)
        vecs = jnp.concatenate([dcb, dlg, dlb, dsc, dpb, jnp.zeros((3, C), F32)], axis=0)

        @pl.when(i == 0)
        def _():
            dvec_ref[...] = vecs

        @pl.when(i > 0)
        def _():
            dvec_ref[...] += vecs

    vec = pl.BlockSpec((None, 1, C), lambda i: (sl, 0, 0))
    taps = pl.BlockSpec((None, 32, C), lambda i: (sl, 0, 0))

    def prev_blk(i):
        return (jnp.maximum(i * hb - 1, 0), 0)

    def next_blk(i):
        return (jnp.minimum((i + 1) * hb, T // HL - 1), 0)

    return _pallas(
        body, name=name, grid=(nT,),
        in_specs=[pl.BlockSpec((tT, 5 * C), lambda i: (i, 0)), pl.BlockSpec((HL, 5 * C), prev_blk),
                  pl.BlockSpec((HL, 5 * C), next_blk),
                  pl.BlockSpec((tT, C), lambda i: (i, 0)), pl.BlockSpec((HL, C), next_blk),
                  pl.BlockSpec((tT, C), lambda i: (i, 0)), pl.BlockSpec((HL, C), next_blk),
                  pl.BlockSpec((None, 2 * C, C), lambda i: (0, 0, 0)), pl.BlockSpec((8, 128), lambda i: (0, 0)),
                  taps, vec, vec, pl.BlockSpec((4, POOL_GW, POOL_GW), lambda i: (0, 0, 0)), vec, vec],
        out_specs=[pl.BlockSpec((tT, 5 * C), lambda i: (i, 0)), pl.BlockSpec((32 * 8, C), lambda i: (0, 0)),
                   pl.BlockSpec((8, C), lambda i: (0, 0)), pl.BlockSpec((4, POOL_GW, POOL_GW), lambda i: (0, 0, 0))],
        out_shape=[jax.ShapeDtypeStruct((T, 5 * C), BF16), jax.ShapeDtypeStruct((32 * 8, C), F32),
                   jax.ShapeDtypeStruct((8, C), F32), jax.ShapeDtypeStruct((4, POOL_GW, POOL_GW), F32)],
        scratch_shapes=[pltpu.VMEM((R1 + 8, C), F32), pltpu.VMEM((8, R1, C), F32), pltpu.VMEM((tT, C), F32),
                        pltpu.VMEM((32, 8, C), F32)],
        compiler_params=_params("arbitrary"))(p, p, p, u1, u1, dout, dout, w_out, after, cwr, lg, lb, pw, pb, sc)


def _softplus(z):
    u = jnp.exp(-jnp.abs(z))
    w = 1.0 + u
    l1p = jnp.where(w == 1.0, u, u * jnp.log(w) / jnp.where(w == 1.0, 1.0, w - 1.0))
    return jnp.maximum(z, 0.0) + l1p


def _lru_gates(xrx, cw_ref, cb_ref, wr_ref, br_ref, wi_ref, bi_ref, lam_ref):
    HL = ODD_HALO
    xc = cb_ref[...] + cw_ref[LRU_CONV_K - 1:LRU_CONV_K, :] * xrx[HL:]
    for k in range(LRU_CONV_K - 1):
        xc = xc + cw_ref[k:k + 1, :] * _shift_down(xrx, LRU_CONV_K - 1 - k)[HL:]
    xcb = xc.astype(BF16)
    rp, ip = [], []
    for hd in range(LRU_HEADS):
        cols = slice(hd * LRU_HD, (hd + 1) * LRU_HD)
        rp.append(jnp.dot(xcb[:, cols], wr_ref[hd], preferred_element_type=F32))
        ip.append(jnp.dot(xcb[:, cols], wi_ref[hd], preferred_element_type=F32))
    r = _sigmoid(jnp.concatenate(rp, axis=1) + br_ref[...])
    ig = _sigmoid(jnp.concatenate(ip, axis=1) + bi_ref[...])
    sp = _softplus(-lam_ref[...])
    log_a = (-LRU_C) * r * sp
    a = jnp.exp(log_a)
    m2 = jnp.maximum(-jnp.tanh(log_a) * (a * a + 1.0), 1e-30)
    inv_mult = lax.rsqrt(m2)
    return xc, xcb, r, ig, sp, a, m2 * inv_mult, inv_mult


def _group_scan(a, b, reverse):
    n, w = a.shape
    a, b = a.reshape(n // 8, 8, w), b.reshape(n // 8, 8, w)
    pos = lax.broadcasted_iota(jnp.int32, (1, 8, 1), 1)
    s = 1
    while s < 8:
        ok = (pos < 8 - s) if reverse else (pos >= s)
        shift = (8 - s) if reverse else s
        a_sh = jnp.where(ok, pltpu.roll(a, shift, 1), 1.0)
        b_sh = jnp.where(ok, pltpu.roll(b, shift, 1), 0.0)
        b = a * b_sh + b
        a = a * a_sh
        s *= 2
    return a.reshape(n, w), b.reshape(n, w)


def _apply_carries(a_ref, b_ref, out_ref, c0, reverse):
    ng = a_ref.shape[0] // 8

    def step(t, c):
        r0 = pl.multiple_of(((ng - 1 - t) if reverse else t) * 8, 8)
        x = a_ref[pl.ds(r0, 8), :] * c + b_ref[pl.ds(r0, 8), :]
        out_ref[pl.ds(r0, 8), :] = x
        return x[0:1, :] if reverse else x[7:8, :]

    return lax.fori_loop(0, ng, step, c0)


def _odd_mixer_fwd(p, sl, cw, cb, wr, br, wi, bi, lam, name):
    T = p.shape[0]
    W = W_LRU
    tT, HL = MIX_TILE, ODD_HALO
    hb = tT // HL

    def body(pm_ref, ph_ref, cw_ref, cb_ref, wr_ref, br_ref, wi_ref, bi_ref, lam_ref, y_ref, hs_ref, carry_ref,
             sa_ref, sb_ref):
        i = pl.program_id(0)
        keep = (i > 0).astype(F32)

        @pl.when(i == 0)
        def _():
            carry_ref[...] = jnp.zeros_like(carry_ref)

        xrx = jnp.concatenate([ph_ref[:, 0:W] * keep, pm_ref[:, 0:W]], axis=0)
        xc, _, _, ig, _, a, mult, _ = _lru_gates(xrx, cw_ref, cb_ref, wr_ref, br_ref, wi_ref, bi_ref, lam_ref)
        sa_ref[...], sb_ref[...] = _group_scan(a, mult * (ig * xc), reverse=False)
        last = _apply_carries(sa_ref, sb_ref, hs_ref, carry_ref[0:1, :], reverse=False)
        carry_ref[...] = jnp.broadcast_to(last, (8, W))
        hs = hs_ref[...]
        gt = pm_ref[:, W:2 * W]
        y_ref[...] = (hs * (gt * _sigmoid(gt))).astype(BF16)

    vec = pl.BlockSpec((None, 1, W), lambda i: (sl, 0, 0))
    heads = pl.BlockSpec((None, LRU_HEADS, LRU_HD, LRU_HD), lambda i: (sl, 0, 0, 0))
    return _pallas(
        body, name=name, grid=(T // tT,),
        in_specs=[pl.BlockSpec((tT, 2 * W), lambda i: (i, 0)),
                  pl.BlockSpec((HL, 2 * W), lambda i: (jnp.maximum(i * hb - 1, 0), 0)),
                  pl.BlockSpec((None, 8, W), lambda i: (sl, 0, 0)), vec, heads, vec, heads, vec, vec],
        out_specs=[pl.BlockSpec((tT, W), lambda i: (i, 0)), pl.BlockSpec((tT, W), lambda i: (i, 0))],
        out_shape=[jax.ShapeDtypeStruct((T, W), BF16), jax.ShapeDtypeStruct((T, W), F32)],
        scratch_shapes=[pltpu.VMEM((8, W), F32), pltpu.VMEM((tT, W), F32), pltpu.VMEM((tT, W), F32)],
        compiler_params=_params("arbitrary"))(p, p, cw, cb, wr, br, wi, bi, lam)


def _odd_mixer_bwd(p, hs, dout, w_out, after, sl, cw, cb, wr, br, wi, bi, lam, name):
    T = p.shape[0]
    W = W_LRU
    D = dout.shape[1]
    tT, HL = MIX_TILE, ODD_HALO
    hb = tT // HL
    nT = T // tT

    def body(pm_ref, ph_ref, hsm_ref, hsh_ref, do_ref, wo_ref, after_ref, cw_ref, cb_ref, wr_ref, br_ref, wi_ref,
             bi_ref, lam_ref, dp_ref, dwr_ref, dwi_ref, dvec_ref, gcarry_ref, xcarry_ref, sa_ref, sb_ref, g_ref):
        i = pl.program_id(0)
        keep = (i < nT - 1).astype(F32)

        @pl.when(i == 0)
        def _():
            gcarry_ref[...] = jnp.zeros_like(gcarry_ref)
            xcarry_ref[...] = jnp.zeros_like(xcarry_ref)

        xrx = jnp.concatenate([ph_ref[:, 0:W] * keep, pm_ref[:, 0:W]], axis=0)
        xc, xcb, r, ig, sp, a, mult, inv_mult = _lru_gates(xrx, cw_ref, cb_ref, wr_ref, br_ref, wi_ref, bi_ref, lam_ref)
        hs = hsm_ref[...]
        gt = pm_ref[:, W:2 * W]
        sg = _sigmoid(gt)
        dyv = _nt(do_ref[...], wo_ref[...])
        dp_ref[:, W:2 * W] = (dyv * hs * _dsilu(gt, sg)).astype(BF16)
        row = lax.broadcasted_iota(jnp.int32, (tT, 1), 0)
        m = jnp.where(row == tT - 1, 1.0, _shift_up(a, 1))
        sa_ref[...], sb_ref[...] = _group_scan(m, dyv * (gt * sg), reverse=True)
        first = _apply_carries(sa_ref, sb_ref, g_ref, gcarry_ref[0:1, :], reverse=True)
        G = g_ref[...]
        gcarry_ref[...] = jnp.broadcast_to(a[0:1, :] * first, (8, W))
        hs_prev = jnp.where(row == 0, hsh_ref[HL - 1:HL, :] * keep, _shift_down(hs, 1))
        da = G * hs_prev
        dmult = G * (ig * xc)
        di = G * mult * xc
        dxc = G * mult * ig
        dlog_a = da * a - dmult * (a * a) * inv_mult
        drp = dlog_a * ((-LRU_C) * sp) * r * (1.0 - r)
        dip = di * ig * (1.0 - ig)
        dlam = jnp.sum(dlog_a * ((-LRU_C) * r), axis=0, keepdims=True) * (-_sigmoid(-lam_ref[...]))
        drb, dib = drp.astype(BF16), dip.astype(BF16)
        back = []
        for hd in range(LRU_HEADS):
            cols = slice(hd * LRU_HD, (hd + 1) * LRU_HD)
            back.append(_nt(drb[:, cols], wr_ref[hd]) + _nt(dib[:, cols], wi_ref[hd]))
            dwr_h = _tn(xcb[:, cols], drb[:, cols])
            dwi_h = _tn(xcb[:, cols], dib[:, cols])

            @pl.when(i == 0)
            def _(hd=hd, dwr_h=dwr_h, dwi_h=dwi_h):
                dwr_ref[hd] = dwr_h
                dwi_ref[hd] = dwi_h

            @pl.when(i > 0)
            def _(hd=hd, dwr_h=dwr_h, dwi_h=dwi_h):
                dwr_ref[hd] += dwr_h
                dwi_ref[hd] += dwi_h

        dxc = dxc + jnp.concatenate(back, axis=1)
        dxcx = jnp.concatenate([dxc, xcarry_ref[...]], axis=0)
        dxr = cw_ref[LRU_CONV_K - 1:LRU_CONV_K, :] * dxc
        rows = []
        for k in range(LRU_CONV_K - 1):
            j = LRU_CONV_K - 1 - k
            dxr = dxr + cw_ref[k:k + 1, :] * _shift_up(dxcx, j)[0:tT]
            rows.append(jnp.sum(dxc * _shift_down(xrx, j)[HL:], axis=0, keepdims=True))
        rows.append(jnp.sum(dxc * xrx[HL:], axis=0, keepdims=True))
        dp_ref[:, 0:W] = dxr.astype(BF16)
        xcarry_ref[...] = dxc[0:8]
        rows += [jnp.sum(dxc, axis=0, keepdims=True), jnp.sum(drp, axis=0, keepdims=True),
                 jnp.sum(dip, axis=0, keepdims=True), dlam]
        vecs = jnp.concatenate(rows, axis=0)

        @pl.when(i == 0)
        def _():
            dvec_ref[...] = vecs

        @pl.when(i > 0)
        def _():
            dvec_ref[...] += vecs

    vec = pl.BlockSpec((None, 1, W), lambda i: (sl, 0, 0))
    heads = pl.BlockSpec((None, LRU_HEADS, LRU_HD, LRU_HD), lambda i: (sl, 0, 0, 0))
    dheads = pl.BlockSpec((LRU_HEADS, LRU_HD, LRU_HD), lambda i: (0, 0, 0))

    def tile(i):
        return (nT - 1 - i, 0)

    def prev_blk(i):
        return (jnp.maximum((nT - 1 - i) * hb - 1, 0), 0)

    return _pallas(
        body, name=name, grid=(nT,),
        in_specs=[pl.BlockSpec((tT, 2 * W), tile), pl.BlockSpec((HL, 2 * W), prev_blk),
                  pl.BlockSpec((tT, W), tile), pl.BlockSpec((HL, W), prev_blk), pl.BlockSpec((tT, D), tile),
                  pl.BlockSpec((None, W, D), lambda i: (0, 0, 0)), pl.BlockSpec((8, 128), lambda i: (0, 0)),
                  pl.BlockSpec((None, 8, W), lambda i: (sl, 0, 0)), vec, heads, vec, heads, vec, vec],
        out_specs=[pl.BlockSpec((tT, 2 * W), tile), dheads, dheads, pl.BlockSpec((8, W), lambda i: (0, 0))],
        out_shape=[jax.ShapeDtypeStruct((T, 2 * W), BF16), jax.ShapeDtypeStruct((LRU_HEADS, LRU_HD, LRU_HD), F32),
                   jax.ShapeDtypeStruct((LRU_HEADS, LRU_HD, LRU_HD), F32), jax.ShapeDtypeStruct((8, W), F32)],
        scratch_shapes=[pltpu.VMEM((8, W), F32), pltpu.VMEM((8, W), F32), pltpu.VMEM((tT, W), F32),
                        pltpu.VMEM((tT, W), F32), pltpu.VMEM((tT, W), F32)],
        compiler_params=_params("arbitrary"))(p, p, hs, hs, dout, w_out, after, cw, cb, wr, br, wi, bi, lam)


def _pad_rows(a, rows):
    return jnp.pad(a, ((0, 0), (0, rows - a.shape[1]), (0, 0)))


def _layer_fwd(even, h, w, w_in, w_out, after):
    sl = w["sl"]
    p, n = _in_proj(h, w["norm"], sl, w_in, 0, after, "in_proj_even" if even else "in_proj_odd")
    if even:
        y, aux = _even_mixer_fwd(p, sl, w["conv_w"], w["conv_b"], w["ln_g"], w["ln_b"], w["pool_w"], w["pool_b"],
                                 w["pool_scale"], "even_mixer_fwd")
    else:
        y, aux = _odd_mixer_fwd(p, sl, w["conv_w"], w["conv_b"], w["w_rg"], w["b_rg"], w["w_ig"], w["b_ig"], w["lam"],
                                "odd_mixer_fwd")
    if callable(w_out):
        w_out = w_out(y)
    h_next = _out_proj(y, w_out, 0, h, "out_proj_even" if even else "out_proj_odd")
    return h_next, (h, n, p, aux, y), w_out


def _layer_bwd_weights(even, saved, w, w_out, dhb, after):
    h, n, p, aux, y = saved
    if even:
        dp, dcw, dvec, dpw = _even_mixer_bwd(p, aux, dhb, w_out, after, w["sl"], w["conv_w_rev"], w["ln_g"], w["ln_b"],
                                             w["pool_w"], w["pool_b"], w["pool_scale"], "even_mixer_bwd")
        dw_out = _dw_out(y, dhb, 0, 1, None, "dw_out_even")
        dw_in = _dw_in(n, dp, N_CHIPS, 0, 1, None, "dw_in_even")
        return dp, dw_in, dw_out, dict(conv_w=dcw, vec=dvec, pool_w=dpw)
    dp, dwr, dwi, dvec = _odd_mixer_bwd(p, aux, dhb, w_out, after, w["sl"], w["conv_w"], w["conv_b"], w["w_rg"],
                                        w["b_rg"], w["w_ig"], w["b_ig"], w["lam"], "odd_mixer_bwd")
    dw_out = _dw_out(y, dhb, 0, 1, None, "dw_out_odd")
    dw_in = _dw_in(n, dp, N_CHIPS, 0, 1, None, "dw_in_odd")
    return dp, dw_in, dw_out, dict(w_rg=dwr, w_ig=dwi, vec=dvec)


def _layer_bwd_input(even, saved, w, w_in, dp, dh, after):
    return _dn_proj(dp, w_in, 0, saved[0], w["norm"], w["sl"], dh, after, "dn_proj_even" if even else "dn_proj_odd")


ANY = pl.BlockSpec(memory_space=pl.ANY)


def _mesh_pos():
    return lax.axis_index("x"), lax.axis_index("y"), lax.axis_index("c")


def _other_chips(x, y):
    return [(1 - x, y), (x, 1 - y), (1 - x, 1 - y)]


def _other_devices(x, y, c):
    out = []
    for p in range(1, N_DEV):
        out.append((1 - x if p & 4 else x, 1 - y if p & 2 else y, 1 - c if p & 1 else c))
    return out


def _remote(src, dst, ssem, rsem, dev):
    return pltpu.make_async_remote_copy(src_ref=src, dst_ref=dst, send_sem=ssem, recv_sem=rsem, device_id=dev,
                                        device_id_type=MESH)


def _comm_call(body, name, ins, out_shape, scratch, aliases=None):
    return _pallas(body, name=name, in_specs=[ANY] * len(ins), out_specs=[ANY] * len(out_shape), out_shape=out_shape,
                   scratch_shapes=scratch, input_output_aliases=aliases or {},
                   compiler_params=pltpu.CompilerParams(has_side_effects=True))(*ins)


def _cast_shard(w, layer, pos):
    _, R, C = w.shape
    tr = _row_tile(R, C)

    def body(pos_ref, w_ref, o_ref):
        o_ref[...] = w_ref[...].astype(BF16)

    grid_spec = pltpu.PrefetchScalarGridSpec(
        num_scalar_prefetch=1, grid=(R // tr,),
        in_specs=[pl.BlockSpec((None, tr, C), lambda i, pr: (layer, i, 0))],
        out_specs=pl.BlockSpec((None, None, tr, C), lambda i, pr: (0, pr[0], i, 0)))
    return _pallas(body, name="cast_shard", grid_spec=grid_spec,
                   out_shape=jax.ShapeDtypeStruct((1, N_CHIPS, R, C), BF16),
                   compiler_params=_params("parallel"))(pos, w)


def _gather_weights(big, small):
    nA = len(big)
    half = [a.shape[2] // 2 for a in big]

    def body(*refs):
        ins, outs = refs[:nA + 1], refs[nA + 1:2 * nA + 2]
        ssem, rsem, fsem, frsem, lsem = refs[2 * nA + 2:]
        x, y, c = _mesh_pos()
        k = 2 * x + y
        chips = _other_chips(x, y)
        sib = (x, y, 1 - c)

        def slab(a, chip, core):
            return outs[a].at[:, chip, pl.ds(core * half[a], half[a]), :]

        local = [pltpu.make_async_copy(ins[nA], outs[nA].at[k], lsem.at[0])]
        for cp in local:
            cp.start()
        sends = []
        for j, (ox, oy) in enumerate(chips):
            for a in range(nA):
                sends.append(_remote(slab(a, k, c), slab(a, k, c), ssem.at[a, j], rsem.at[a, j], (ox, oy, c)))
            sends.append(_remote(ins[nA], outs[nA].at[k], ssem.at[nA, j], rsem.at[nA, j], (ox, oy, c)))
        for cp in sends:
            cp.start()
        for j, (ox, oy) in enumerate(chips):
            kj = 2 * ox + oy
            for a in range(nA):
                got = slab(a, kj, c)
                _remote(got, got, ssem.at[a, j], rsem.at[a, j], (ox, oy, c)).wait_recv()
                fw = _remote(got, got, fsem.at[a, j], frsem.at[a, j], sib)
                fw.start()
                sends.append(fw)
            gs = outs[nA].at[kj]
            _remote(gs, gs, ssem.at[nA, j], rsem.at[nA, j], (ox, oy, c)).wait_recv()
        for j, (ox, oy) in enumerate(chips):
            kj = 2 * ox + oy
            for a in range(nA):
                theirs = slab(a, kj, 1 - c)
                _remote(theirs, theirs, fsem.at[a, j], frsem.at[a, j], sib).wait_recv()
        for cp in sends:
            cp.wait_send()
        for cp in local:
            cp.wait()

    out_shape = [jax.ShapeDtypeStruct(a.shape, a.dtype) for a in big]
    out_shape.append(jax.ShapeDtypeStruct((N_CHIPS,) + small.shape, small.dtype))
    scratch = [pltpu.SemaphoreType.DMA((nA + 1, 3)), pltpu.SemaphoreType.DMA((nA + 1, 3)),
               pltpu.SemaphoreType.DMA((nA, 3)), pltpu.SemaphoreType.DMA((nA, 3)), pltpu.SemaphoreType.DMA((1,))]
    return _comm_call(body, "gather_weights", list(big) + [small], out_shape, scratch, {a: a for a in range(nA)})


HBM = pl.BlockSpec(memory_space=pltpu.HBM)
SEM = pl.BlockSpec(memory_space=pltpu.SEMAPHORE)
EFFECT = pltpu.SideEffectType.DATAFLOW_SIDE_EFFECTING


def _split_start(arrays, copies, n, name):
    k = len(arrays)

    def body(*refs):
        for cp in copies(refs[k + 2:2 * k + 2], refs[k], refs[k + 1]):
            cp.start()
        refs[2 * k + 2][...] = jnp.zeros((8, 128), F32)

    out = _pallas(
        body, name=name,
        out_shape=(pltpu.SemaphoreType.DMA((n,)), pltpu.SemaphoreType.DMA((n,)),
                   *[pltpu.HBM(a.shape, a.dtype) for a in arrays], jax.ShapeDtypeStruct((8, 128), F32)),
        in_specs=(HBM,) * k, out_specs=(SEM, SEM) + (HBM,) * k + (pl.BlockSpec(memory_space=pltpu.VMEM),),
        input_output_aliases={i: i + 2 for i in range(k)},
        compiler_params=pltpu.CompilerParams(has_side_effects=EFFECT),
    )(*[pltpu.with_memory_space_constraint(a, pltpu.HBM) for a in arrays])
    return out[0], out[1], list(out[2:2 + k]), out[2 + k]


def _split_wait(ssem, rsem, arrays, copies, after, name):
    k = len(arrays)

    def body(*refs):
        for cp in copies(refs[:k], refs[k], refs[k + 1]):
            cp.wait_send()
            cp.wait_recv()

    out = _pallas(
        body, name=name, out_shape=tuple(pltpu.HBM(a.shape, a.dtype) for a in arrays),
        in_specs=(HBM,) * k + (SEM, SEM, ANY), out_specs=(HBM,) * k, input_output_aliases={i: i for i in range(k)},
        compiler_params=pltpu.CompilerParams(has_side_effects=EFFECT),
    )(*arrays, ssem, rsem, after)
    return list(out)


def _gather_copies(shapes):
    half = [s[2] // 2 for s in shapes]

    def copies(refs, ssem, rsem):
        x, y, c = _mesh_pos()
        out = []
        for j, (ox, oy) in enumerate(_other_chips(x, y)):
            for a, ref in enumerate(refs):
                slab = ref.at[:, 2 * x + y, pl.ds(c * half[a], half[a]), :]
                out.append(_remote(slab, slab, ssem.at[3 * a + j], rsem.at[3 * a + j], (ox, oy, c)))
        return out

    return copies


def _chips_copies(n_arr):
    def copies(refs, ssem, rsem):
        x, y, c = _mesh_pos()
        out = []
        for j, (ox, oy) in enumerate(_other_chips(x, y)):
            for a in range(n_arr):
                out.append(_remote(refs[a].at[:, 2 * ox + oy], refs[n_arr + a].at[:, 2 * x + y], ssem.at[3 * a + j],
                                   rsem.at[3 * a + j], (ox, oy, c)))
        return out

    return copies


def _halves_copies(shapes):
    n = len(shapes)
    half = [s[2] // 2 for s in shapes]

    def copies(refs, ssem, rsem):
        x, y, c = _mesh_pos()
        return [_remote(refs[a].at[:, :, pl.ds((1 - c) * half[a], half[a]), :], refs[n + a], ssem.at[a], rsem.at[a],
                        (x, y, 1 - c)) for a in range(n)]

    return copies


def _forward_cores(arrays):
    nA = len(arrays)
    half = [a.shape[2] // 2 for a in arrays]

    def body(*refs):
        outs = refs[nA:2 * nA]
        ssem, rsem = refs[2 * nA:]
        x, y, c = _mesh_pos()
        sib = (x, y, 1 - c)
        sends, waits = [], []
        for j, (ox, oy) in enumerate(_other_chips(x, y)):
            for a in range(nA):
                got = outs[a].at[:, 2 * ox + oy, pl.ds(c * half[a], half[a]), :]
                sends.append(_remote(got, got, ssem.at[a, j], rsem.at[a, j], sib))
                theirs = outs[a].at[:, 2 * ox + oy, pl.ds((1 - c) * half[a], half[a]), :]
                waits.append(_remote(theirs, theirs, ssem.at[a, j], rsem.at[a, j], sib))
        for cp in sends:
            cp.start()
        for cp in waits:
            cp.wait_recv()
        for cp in sends:
            cp.wait_send()

    out_shape = [jax.ShapeDtypeStruct(a.shape, a.dtype) for a in arrays]
    scratch = [pltpu.SemaphoreType.DMA((nA, 3)), pltpu.SemaphoreType.DMA((nA, 3))]
    return _comm_call(body, "forward_cores", list(arrays), out_shape, scratch, {a: a for a in range(nA)})


def _exchange_halves(big):
    nA = len(big)
    half = [a.shape[2] // 2 for a in big]

    def body(*refs):
        ins, outs = refs[:nA], refs[nA:2 * nA]
        ssem, rsem = refs[2 * nA:]
        x, y, c = _mesh_pos()
        sib = (x, y, 1 - c)
        sends = [_remote(ins[a].at[:, :, pl.ds((1 - c) * half[a], half[a]), :], outs[a], ssem.at[a], rsem.at[a], sib)
                 for a in range(nA)]
        for cp in sends:
            cp.start()
        for a in range(nA):
            _remote(outs[a], outs[a], ssem.at[a], rsem.at[a], sib).wait_recv()
        for cp in sends:
            cp.wait_send()

    out_shape = [jax.ShapeDtypeStruct((a.shape[0], N_CHIPS, h, a.shape[3]), a.dtype) for a, h in zip(big, half)]
    scratch = [pltpu.SemaphoreType.DMA((nA,)), pltpu.SemaphoreType.DMA((nA,))]
    return _comm_call(body, "exchange_halves", list(big), out_shape, scratch)


def _exchange_final(grads, everywhere, small):
    nA = len(grads)
    n_remote = sum(7 if ev else 1 for ev in everywhere) + 7

    def body(*refs):
        small_ref, outs, gathered = refs[nA], refs[nA + 1:2 * nA + 1], refs[2 * nA + 1]
        ssem, rsem, lsem = refs[2 * nA + 2:]
        x, y, c = _mesh_pos()
        k = 2 * x + y
        sib = (x, y, 1 - c)
        local = pltpu.make_async_copy(small_ref, gathered.at[2 * k + c], lsem.at[0])
        local.start()
        sends, arrivals, waits = [], [], []
        count = [0]

        def sems():
            count[0] += 1
            return ssem.at[count[0] - 1], rsem.at[count[0] - 1]

        def to_sibling(src, mine, theirs):
            sm = sems()
            sends.append(_remote(src, mine, *sm, sib))
            waits.append(_remote(theirs, theirs, *sm, sib))

        def to_everyone(src, place):
            to_sibling(src, place(k, c), place(k, 1 - c))
            for (ox, oy) in _other_chips(x, y):
                ici, d2d = sems(), sems()
                got = place(2 * ox + oy, c)
                sends.append(_remote(src, place(k, c), *ici, (ox, oy, c)))
                arrivals.append((_remote(got, got, *ici, (ox, oy, c)), _remote(got, got, *d2d, sib)))
                theirs = place(2 * ox + oy, 1 - c)
                waits.append(_remote(theirs, theirs, *d2d, sib))

        to_everyone(small_ref, lambda chip, core: gathered.at[2 * chip + core])
        for a in range(nA):
            if everywhere[a]:
                r2 = grads[a].shape[1] // N_DEV

                def place(chip, core, a=a, r2=r2):
                    return outs[a].at[:, pl.ds((2 * chip + core) * r2, r2), :]

                to_everyone(place(k, c), place)
            else:
                r2 = grads[a].shape[1] // 2
                mine = outs[a].at[:, pl.ds(c * r2, r2), :]
                to_sibling(mine, mine, outs[a].at[:, pl.ds((1 - c) * r2, r2), :])
        for cp in sends:
            cp.start()
        for arrived, onward in arrivals:
            arrived.wait_recv()
            onward.start()
        for cp in waits:
            cp.wait_recv()
        for cp in sends + [onward for _, onward in arrivals]:
            cp.wait_send()
        local.wait()

    out_shape = [jax.ShapeDtypeStruct(g.shape, g.dtype) for g in grads]
    out_shape.append(jax.ShapeDtypeStruct((N_DEV,) + small.shape, small.dtype))
    scratch = [pltpu.SemaphoreType.DMA((n_remote,)), pltpu.SemaphoreType.DMA((n_remote,)), pltpu.SemaphoreType.DMA((1,))]
    return _comm_call(body, "exchange_final", list(grads) + [small], out_shape, scratch, {a: a for a in range(nA)})


BLOCK_BYTES = 4 << 20


def _row_tile(rows, cols, mult=16, limit=BLOCK_BYTES):
    best = mult
    for t in range(mult, rows + 1, mult):
        if rows % t == 0 and t * cols * 4 <= limit:
            best = t
    return best


def _add_cores(own, recv, pos):
    L, _, R, C = own.shape
    r2 = R // 2
    tr = _row_tile(r2, C)
    nb = r2 // tr

    def body(pos_ref, a_ref, r_ref, o_ref):
        o_ref[...] = (a_ref[...].astype(F32) + r_ref[...].astype(F32)).astype(BF16)

    blk = (None, None, tr, C)
    grid_spec = pltpu.PrefetchScalarGridSpec(
        num_scalar_prefetch=1, grid=(L, N_CHIPS, nb),
        in_specs=[pl.BlockSpec(blk, lambda l, s, i, pr: (l, s, pr[1] * nb + i, 0)),
                  pl.BlockSpec(blk, lambda l, s, i, pr: (l, s, i, 0))],
        out_specs=pl.BlockSpec(blk, lambda l, s, i, pr: (l, s, i, 0)))
    return _pallas(body, name="add_cores", grid_spec=grid_spec,
                   out_shape=jax.ShapeDtypeStruct((L, N_CHIPS, r2, C), BF16),
                   compiler_params=_params("parallel", "parallel", "parallel"))(pos, own, recv)


def _sum_chips(own, recv, pos, everywhere, layer, nlayers, prev):
    _, _, r2, C = own.shape
    tr = _row_tile(r2, 2 * C)
    nb = r2 // tr

    def body(pos_ref, a_ref, r_ref, *rest):
        acc = None
        for s in range(N_CHIPS):
            term = jnp.where(pos_ref[0] == s, a_ref[...], r_ref[s]).astype(F32)
            acc = term if acc is None else acc + term
        rest[-1][...] = acc

    if everywhere:
        def out_map(i, pr):
            return (layer, (2 * pr[0] + pr[1]) * nb + i, 0)
    else:
        def out_map(i, pr):
            return (layer, pr[1] * nb + i, 0)

    in_specs = [pl.BlockSpec((None, None, tr, C), lambda i, pr: (0, pr[0], i, 0)),
                pl.BlockSpec((None, N_CHIPS, tr, C), lambda i, pr: (0, 0, i, 0))]
    grid_spec = pltpu.PrefetchScalarGridSpec(
        num_scalar_prefetch=1, grid=(nb,), in_specs=in_specs + ([] if prev is None else [ANY]),
        out_specs=pl.BlockSpec((None, tr, C), out_map))
    rows = (N_DEV if everywhere else 2) * r2
    args = (pos, own, recv) if prev is None else (pos, own, recv, prev)
    return _pallas(body, name="sum_chips", grid_spec=grid_spec, out_shape=jax.ShapeDtypeStruct((nlayers, rows, C), F32),
                   input_output_aliases={} if prev is None else {3: 0},
                   compiler_params=_params("parallel"))(*args)


def _sum_devices(parts):
    n, R, C = parts.shape
    tr = _row_tile(R, C * n, 8)

    def body(p_ref, o_ref):
        acc = p_ref[0]
        for s in range(1, n):
            acc = acc + p_ref[s]
        o_ref[...] = acc

    return _pallas(body, name="sum_devices", grid=(R // tr,), in_specs=[pl.BlockSpec((n, tr, C), lambda i: (0, i, 0))],
                   out_specs=pl.BlockSpec((tr, C), lambda i: (i, 0)), out_shape=jax.ShapeDtypeStruct((R, C), F32),
                   compiler_params=_params("parallel"))(parts)


def _adamw(w, g, m, v, name):
    L, R, C = w.shape
    tr = _row_tile(R, C, 8, BLOCK_BYTES // 2)

    def body(w_ref, g_ref, m_ref, v_ref, d_ref, m2_ref, v2_ref):
        gg = g_ref[...]
        m2 = ADAM_B1 * m_ref[...] + (1.0 - ADAM_B1) * gg
        v2 = ADAM_B2 * v_ref[...] + (1.0 - ADAM_B2) * (gg * gg)
        m_hat = m2 / (1.0 - ADAM_B1 ** ADAM_STEP)
        v_hat = v2 / (1.0 - ADAM_B2 ** ADAM_STEP)
        d_ref[...] = -ADAM_LR * (m_hat / (jnp.sqrt(v_hat) + ADAM_EPS) + ADAM_WD * w_ref[...])
        m2_ref[...] = m2
        v2_ref[...] = v2

    blk = pl.BlockSpec((1, tr, C), lambda l, i: (l, i, 0))
    shp = jax.ShapeDtypeStruct((L, R, C), F32)
    return _pallas(body, name=name, grid=(L, R // tr), in_specs=[blk] * 4, out_specs=[blk] * 3, out_shape=[shp] * 3,
                   compiler_params=_params("parallel", "parallel"))(w, g, m, v)


WEIGHTS = ("norm_even", "w_in_even", "conv_a_w", "conv_a_b", "ln_a_g", "ln_a_b", "pool_w", "pool_b", "pool_scale",
           "w_out_even", "norm_odd", "w_in_odd", "conv_c_w", "conv_c_b", "w_rg", "b_rg", "w_ig", "b_ig", "lru_lambda",
           "w_out_odd", "final_norm")
BIG = ("w_in_even", "w_out_even", "pool_w", "w_in_odd", "w_out_odd", "w_rg", "w_ig")
SMALL = tuple(n for n in WEIGHTS if n not in BIG)
SMALL_SHARDED = ("conv_a_w", "pool_b", "norm_odd", "conv_c_w", "conv_c_b", "b_rg", "b_ig", "lru_lambda")


def _pack(arrs):
    flat = jnp.concatenate([a.reshape(-1) for a in arrs])
    rows = -(-flat.shape[0] // (64 * 128)) * 64
    return jnp.pad(flat, (0, rows * 128 - flat.shape[0])).reshape(rows, 128)


def _unpack(buf, shapes, lead=()):
    flat = buf.reshape(tuple(lead) + (-1,))
    out, o = [], 0
    for s in shapes:
        n = 1
        for d in s:
            n *= d
        out.append(flat[..., o:o + n].reshape(tuple(lead) + tuple(s)))
        o += n
    return out


def _shard(full, axis, k):
    n = full.shape[axis] // N_CHIPS
    return lax.dynamic_slice_in_dim(full, k * n, n, axis)


def kernel(x, norm_even, w_in_even, conv_a_w, conv_a_b, ln_a_g, ln_a_b, pool_w, pool_b, pool_scale, w_out_even, norm_odd, w_in_odd, conv_c_w, conv_c_b, w_rg, b_rg, w_ig, b_ig, lru_lambda, w_out_odd, final_norm, loss_target, m_norm_even, m_w_in_even, m_conv_a_w, m_conv_a_b, m_ln_a_g, m_ln_a_b, m_pool_w, m_pool_b, m_pool_scale, m_w_out_even, m_norm_odd, m_w_in_odd, m_conv_c_w, m_conv_c_b, m_w_rg, m_b_rg, m_w_ig, m_b_ig, m_lru_lambda, m_w_out_odd, m_final_norm, v_norm_even, v_w_in_even, v_conv_a_w, v_conv_a_b, v_ln_a_g, v_ln_a_b, v_pool_w, v_pool_b, v_pool_scale, v_w_out_even, v_norm_odd, v_w_in_odd, v_conv_c_w, v_conv_c_b, v_w_rg, v_b_rg, v_w_ig, v_b_ig, v_lru_lambda, v_w_out_odd, v_final_norm):
    P = dict(locals())
    xi, yi, ci = _mesh_pos()
    k = 2 * xi + yi
    L = w_in_even.shape[0]
    D = D_MODEL

    pos = jnp.stack([k, ci]).astype(jnp.int32)
    depth = 2 * L
    pool_w3 = pool_w.reshape(L, 4 * 64, POOL_GW)

    def cast_group(layer):
        j = layer // 2
        if layer % 2 == 0:
            return [_cast_shard(w_in_even, j, pos), _cast_shard(w_out_even, j, pos), _cast_shard(pool_w3, j, pos)]
        return [_cast_shard(w_in_odd, j, pos), _cast_shard(w_out_odd, j, pos)]

    first = cast_group(0)
    g_in, g_pool, g_small = _gather_weights([first[0], first[2]], _pack([P[n] for n in SMALL_SHARDED]))
    copies0 = _gather_copies([first[1].shape])
    ssem0, rsem0, late, token0 = _split_start([first[1]], copies0, 3, "gather_start0")

    def late_w_out(y):
        return _forward_cores(_split_wait(ssem0, rsem0, late, copies0, y, "gather_wait0"))[0].reshape(1, -1, D)

    group = [g_in, late_w_out, g_pool]
    full = {}
    for n, a in zip(SMALL_SHARDED, _unpack(g_small, [P[n].shape for n in SMALL_SHARDED], lead=(N_CHIPS,))):
        a = jnp.moveaxis(a, 0, -2)
        full[n] = a.reshape(a.shape[:-2] + (N_CHIPS * a.shape[-1],))

    small_even = dict(norm=norm_even[:, None], conv_w=_pad_rows(full["conv_a_w"], 32),
                      conv_w_rev=_pad_rows(full["conv_a_w"][:, ::-1], 32), conv_b=conv_a_b[:, None], ln_g=ln_a_g[:, None],
                      ln_b=ln_a_b[:, None], pool_b=full["pool_b"].reshape(L, 1, D), pool_scale=pool_scale[:, None])
    small_odd = dict(norm=full["norm_odd"][:, None], conv_w=_pad_rows(full["conv_c_w"], 8),
                     conv_b=full["conv_c_b"][:, None], w_rg=w_rg.astype(BF16), b_rg=full["b_rg"][:, None],
                     w_ig=w_ig.astype(BF16), b_ig=full["b_ig"][:, None], lam=full["lru_lambda"][:, None])

    def small_weights(layer, group):
        if layer % 2 == 0:
            pw = group[2].reshape(N_CHIPS, 4, 64, POOL_GW).transpose(1, 0, 2, 3).reshape(4, POOL_GW, POOL_GW)
            return dict(small_even, sl=layer // 2, pool_w=pw)
        return dict(small_odd, sl=layer // 2)

    no_token = jnp.zeros((8, 128), F32)
    h = x[0]
    saved, big_w, small_w = [], [], []
    for layer in range(depth):
        token = token0 if layer == 0 else no_token
        if layer + 1 < depth:
            nxt = cast_group(layer + 1)
            copies = _gather_copies([a.shape for a in nxt])
            ssem, rsem, nxt, token = _split_start(nxt, copies, 3 * len(nxt), "gather_start%d" % (layer + 1))
        small_w.append(small_weights(layer, group))
        w_out = group[1] if callable(group[1]) else group[1].reshape(1, -1, D)
        h, sv, w_out = _layer_fwd(layer % 2 == 0, h, small_w[layer], group[0], w_out, token)
        big_w.append((group[0], w_out))
        saved.append(sv)
        if layer + 1 < depth:
            group = _forward_cores(_split_wait(ssem, rsem, nxt, copies, h, "gather_wait%d" % (layer + 1)))

    dh, dhb, d_final, loss = _loss_head(h, final_norm[None], loss_target[0])
    loss = lax.psum(loss[0, 0], ("x", "y", "c"))
    everywhere = [False, False, False, False, False, True, True]
    final = [None] * len(everywhere)
    small_of = [None] * depth

    def finish(pending, after):
        ssem, rsem, arrs, copies, slots, pj, pl_ = pending
        arrs = _split_wait(ssem, rsem, arrs, copies, after, "chips_wait%d" % pl_)
        for a, r, s in zip(arrs[:len(slots)], arrs[len(slots):], slots):
            final[s] = _sum_chips(a, r, pos, everywhere[s], pj, L, final[s])

    pending = None
    token = no_token
    for layer in reversed(range(depth)):
        j = layer // 2
        even_layer = layer % 2 == 0
        dp, dw_in, dw_out, sm = _layer_bwd_weights(even_layer, saved[layer], small_w[layer], big_w[layer][1], dhb, token)
        if even_layer:
            dpw = sm["pool_w"].reshape(4, N_CHIPS, 64, POOL_GW).transpose(1, 0, 2, 3)
            parts = [dw_in, dw_out.reshape(1, N_CHIPS, -1, D), dpw.reshape(1, N_CHIPS, 4 * 64, POOL_GW).astype(BF16)]
            slots = [0, 1, 2]
        else:
            parts = [dw_in, dw_out.reshape(1, N_CHIPS, -1, D),
                     sm["w_rg"].reshape(1, N_CHIPS, -1, LRU_HD).astype(BF16),
                     sm["w_ig"].reshape(1, N_CHIPS, -1, LRU_HD).astype(BF16)]
            slots = [3, 4, 5, 6]
        n = len(parts)
        if layer > 0:
            hcopies = _halves_copies([a.shape for a in parts])
            hland = [lax.empty((1, N_CHIPS, a.shape[2] // 2, a.shape[3]), a.dtype) for a in parts]
            hs, hr, harrs, htoken = _split_start(parts + hland, hcopies, n, "halves_start%d" % layer)
            dh, dhb, sm["norm"] = _layer_bwd_input(even_layer, saved[layer], small_w[layer], big_w[layer][0], dp, dh,
                                                   htoken)
            harrs = _split_wait(hs, hr, harrs, hcopies, dh, "halves_wait%d" % layer)
            parts, recv = harrs[:n], harrs[n:]
        else:
            recv = _exchange_halves(parts)
        pair = [_add_cores(a, r, pos) for a, r in zip(parts, recv)]
        copies = _chips_copies(n)
        land = [lax.empty(a.shape, a.dtype) for a in pair]
        ssem, rsem, arrs, token = _split_start(pair + land, copies, 3 * n, "chips_start%d" % layer)
        if layer == 0:
            dh, dhb, sm["norm"] = _layer_bwd_input(even_layer, saved[layer], small_w[layer], big_w[layer][0], dp, dh, token)
        small_of[layer] = sm
        if pending is not None:
            finish(pending, dh)
        pending = (ssem, rsem, arrs, copies, slots, j, layer)
    grad_x = dh
    small_g = []
    for jj in range(L):
        ge, go = small_of[2 * jj], small_of[2 * jj + 1]
        small_g += [ge["conv_w"].reshape(32, 8, D).sum(axis=1)[:CONV_K], ge["vec"][0:5], ge["norm"], go["vec"], go["norm"]]
    small_g.append(d_final)
    small_shapes = [a.shape for a in small_g]
    packed_small = _pack(small_g)
    finish(pending, packed_small)
    *gw, recv_small = _exchange_final(final, everywhere, packed_small)
    sg = _unpack(_sum_devices(recv_small), small_shapes)

    grads = dict(w_in_even=gw[0], w_out_even=gw[1], pool_w=gw[2].reshape(pool_w.shape), w_in_odd=gw[3], w_out_odd=gw[4],
                 w_rg=gw[5].reshape(w_rg.shape), w_ig=gw[6].reshape(w_ig.shape), final_norm=sg[-1][0])
    ev = [sg[5 * j + 1] for j in range(L)]
    ov = [sg[5 * j + 3] for j in range(L)]
    grads["conv_a_w"] = _shard(jnp.stack([sg[5 * j] for j in range(L)]), 2, k)
    grads["norm_even"] = jnp.stack([sg[5 * j + 2][0] for j in range(L)])
    grads["norm_odd"] = _shard(jnp.stack([sg[5 * j + 4][0] for j in range(L)]), 1, k)
    for r, n in enumerate(("conv_a_b", "ln_a_g", "ln_a_b", "pool_scale")):
        grads[n] = jnp.stack([e[r] for e in ev])
    grads["pool_b"] = _shard(jnp.stack([e[4].reshape(4, POOL_GW) for e in ev]), 2, k)
    grads["conv_c_w"] = _shard(jnp.stack([o[0:4] for o in ov]), 2, k)
    for r, n in zip((4, 5, 6, 7), ("conv_c_b", "b_rg", "b_ig", "lru_lambda")):
        grads[n] = _shard(jnp.stack([o[r] for o in ov]), 1, k)

    delta, new_m, new_v = {}, {}, {}
    for n in BIG:
        s3 = (L, -1, P[n].shape[-1])
        d, m2, v2 = _adamw(P[n].reshape(s3), grads[n].reshape(s3), P["m_" + n].reshape(s3), P["v_" + n].reshape(s3), "adamw")
        delta[n], new_m[n], new_v[n] = d.reshape(P[n].shape), m2.reshape(P[n].shape), v2.reshape(P[n].shape)
    shapes = [P[n].shape for n in SMALL]
    packed = [_pack([src[n] for n in SMALL])[None] for src in
              (P, grads, {n: P["m_" + n] for n in SMALL}, {n: P["v_" + n] for n in SMALL})]
    for res, out in zip(_adamw(*packed, "adamw_small"), (delta, new_m, new_v)):
        for n, a in zip(SMALL, _unpack(res[0], shapes)):
            out[n] = a

    return (loss, grad_x[None], *[grads[n] for n in WEIGHTS], *[delta[n] for n in WEIGHTS],
            *[new_m[n] for n in WEIGHTS], *[new_v[n] for n in WEIGHTS])
```

```python
import jax
import jax.numpy as jnp
from jax import lax
from jax.experimental import pallas as pl
from jax.experimental.pallas import tpu as pltpu

F32 = jnp.float32
BF16 = jnp.bfloat16
MESH = pl.DeviceIdType.MESH

D_MODEL = 1024
N_CHIPS = 4
N_DEV = 8
EPS_RMS = 1e-6
EPS_LN = 1e-5
CONV_K = 31
POOL_WINDOWS = (2, 4, 8, 16)
POOL_GW = 256
LRU_HEADS = 12
LRU_HD = 128
W_LRU = LRU_HEADS * LRU_HD
LRU_CONV_K = 4
LRU_C = 8.0
ADAM_LR = 0.001
ADAM_B1 = 0.9
ADAM_B2 = 0.999
ADAM_EPS = 1e-08
ADAM_WD = 0.01
ADAM_STEP = 10

VMEM_LIMIT_BYTES = 56 * 1024 * 1024
ROW_TILE = 512
MIX_TILE = 256
EVEN_HALO = 32
ODD_HALO = 8


def _pallas(body, **kw):
    return pl.pallas_call(body, **kw)


def _params(*sem):
    return pltpu.CompilerParams(dimension_semantics=sem if sem else None, vmem_limit_bytes=VMEM_LIMIT_BYTES)


def _sigmoid(x):
    return 0.5 * jnp.tanh(0.5 * x) + 0.5


def _dsilu(x, s):
    return s * (1.0 + x * (1.0 - s))


def _nt(a, b):
    return lax.dot_general(a, b, (((1,), (1,)), ((), ())), preferred_element_type=F32)


def _tn(a, b):
    return lax.dot_general(a, b, (((0,), (0,)), ((), ())), preferred_element_type=F32)


def _in_proj(h, g, glayer, wg, layer, after, name):
    T, D = h.shape
    _, nblk, _, nb = wg.shape

    nrow = T // ROW_TILE

    def body(h_ref, g_ref, w_ref, after_ref, p_ref, n_ref, n_all):
        j, i = pl.program_id(0), pl.program_id(1)

        @pl.when(j == 0)
        def _():
            x = h_ref[...]
            r = lax.rsqrt(jnp.mean(x * x, axis=-1, keepdims=True) + EPS_RMS)
            nn = (x * r * g_ref[...]).astype(BF16)
            n_ref[...] = nn
            n_all[i] = nn

        p_ref[...] = jnp.dot(n_all[i], w_ref[0], preferred_element_type=F32)

    def rows_once(j, i):
        return (jnp.where(j == 0, i, nrow - 1), 0)

    return _pallas(
        body, name=name, grid=(nblk, nrow),
        in_specs=[pl.BlockSpec((ROW_TILE, D), rows_once), pl.BlockSpec((None, 1, D), lambda j, i: (glayer, 0, 0)),
                  pl.BlockSpec((None, 1, D, nb), lambda j, i: (layer, j, 0, 0)),
                  pl.BlockSpec((8, 128), lambda j, i: (0, 0))],
        out_specs=[pl.BlockSpec((ROW_TILE, nb), lambda j, i: (i, j)), pl.BlockSpec((ROW_TILE, D), rows_once)],
        out_shape=[jax.ShapeDtypeStruct((T, nblk * nb), F32), jax.ShapeDtypeStruct((T, D), BF16)],
        scratch_shapes=[pltpu.VMEM((nrow, ROW_TILE, D), BF16)],
        compiler_params=_params("arbitrary", "arbitrary"))(h, g, wg, after)


def _out_proj(y, w, layer, hres, name):
    T, K = y.shape
    D = w.shape[2]

    def body(y_ref, w_ref, r_ref, o_ref):
        o_ref[...] = r_ref[...] + jnp.dot(y_ref[...], w_ref[...], preferred_element_type=F32)

    return _pallas(
        body, name=name, grid=(T // ROW_TILE,),
        in_specs=[pl.BlockSpec((ROW_TILE, K), lambda i: (i, 0)), pl.BlockSpec((None, K, D), lambda i: (layer, 0, 0)),
                  pl.BlockSpec((ROW_TILE, D), lambda i: (i, 0))],
        out_specs=pl.BlockSpec((ROW_TILE, D), lambda i: (i, 0)),
        out_shape=jax.ShapeDtypeStruct((T, D), F32),
        compiler_params=_params("parallel"))(y, w, hres)


def _dn_proj(dp, wg, layer, h, g, glayer, dres, after, name):
    T, D = h.shape
    _, nblk, _, nb = wg.shape

    nrow = T // ROW_TILE

    def body(dp_ref, w_ref, h_ref, g_ref, dres_ref, after_ref, dh_ref, dhb_ref, dg_ref, acc_ref):
        j, i = pl.program_id(0), pl.program_id(1)
        part = _nt(dp_ref[...], w_ref[0])

        @pl.when(j == 0)
        def _():
            acc_ref[i] = part

        @pl.when(j > 0)
        def _():
            acc_ref[i] += part

        @pl.when(j == nblk - 1)
        def _():
            x = h_ref[...]
            r = lax.rsqrt(jnp.mean(x * x, axis=-1, keepdims=True) + EPS_RMS)
            dn = acc_ref[i]
            q = dn * g_ref[...]
            dh = dres_ref[...] + r * q - x * ((r * r * r) * jnp.mean(q * x, axis=-1, keepdims=True))
            dh_ref[...] = dh
            dhb_ref[...] = dh.astype(BF16)
            dgp = jnp.sum(dn * (x * r), axis=0, keepdims=True)

            @pl.when(i == 0)
            def _():
                dg_ref[...] = dgp

            @pl.when(i > 0)
            def _():
                dg_ref[...] += dgp

    def rows_last(j, i):
        return (jnp.where(j == nblk - 1, i, 0), 0)

    return _pallas(
        body, name=name, grid=(nblk, nrow),
        in_specs=[pl.BlockSpec((ROW_TILE, nb), lambda j, i: (i, j)),
                  pl.BlockSpec((None, 1, D, nb), lambda j, i: (layer, j, 0, 0)),
                  pl.BlockSpec((ROW_TILE, D), rows_last), pl.BlockSpec((None, 1, D), lambda j, i: (glayer, 0, 0)),
                  pl.BlockSpec((ROW_TILE, D), rows_last), pl.BlockSpec((8, 128), lambda j, i: (0, 0))],
        out_specs=[pl.BlockSpec((ROW_TILE, D), rows_last), pl.BlockSpec((ROW_TILE, D), rows_last),
                   pl.BlockSpec((1, D), lambda j, i: (0, 0))],
        out_shape=[jax.ShapeDtypeStruct((T, D), F32), jax.ShapeDtypeStruct((T, D), BF16),
                   jax.ShapeDtypeStruct((1, D), F32)],
        scratch_shapes=[pltpu.VMEM((nrow, ROW_TILE, D), F32)],
        compiler_params=_params("arbitrary", "arbitrary"))(dp, wg, h, g, dres, after)


def _dw_in(n, dp, nblk, layer, nlayers, prev, name):
    T, D = n.shape
    nb = dp.shape[1] // nblk
    ta = D

    def body(n_ref, dp_ref, *rest):
        rest[-1][0] = _tn(n_ref[...], dp_ref[...]).astype(BF16)

    in_specs = [pl.BlockSpec((T, ta), lambda j, i: (0, i)), pl.BlockSpec((T, nb), lambda j, i: (0, j))]
    args = (n, dp) if prev is None else (n, dp, prev)
    return _pallas(
        body, name=name, grid=(nblk, D // ta), in_specs=in_specs + ([] if prev is None else [ANY]),
        out_specs=pl.BlockSpec((None, 1, ta, nb), lambda j, i: (layer, j, i, 0)),
        out_shape=jax.ShapeDtypeStruct((nlayers, nblk, D, nb), BF16),
        input_output_aliases={} if prev is None else {2: 0},
        compiler_params=_params("parallel", "parallel"))(*args)


def _dw_out(y, dout, layer, nlayers, prev, name):
    T, K = y.shape
    D = dout.shape[1]
    tk = 512

    def body(y_ref, d_ref, *rest):
        rest[-1][...] = _tn(y_ref[...], d_ref[...]).astype(BF16)

    in_specs = [pl.BlockSpec((T, tk), lambda i: (0, i)), pl.BlockSpec((T, D), lambda i: (0, 0))]
    args = (y, dout) if prev is None else (y, dout, prev)
    return _pallas(
        body, name=name, grid=(K // tk,), in_specs=in_specs + ([] if prev is None else [ANY]),
        out_specs=pl.BlockSpec((None, tk, D), lambda i: (layer, i, 0)),
        out_shape=jax.ShapeDtypeStruct((nlayers, K, D), BF16),
        input_output_aliases={} if prev is None else {2: 0},
        compiler_params=_params("parallel"))(*args)


def _loss_head(h, g, tgt):
    T, D = h.shape
    tm = MIX_TILE

    def body(h_ref, g_ref, t_ref, dh_ref, dhb_ref, dg_ref, loss_ref):
        i = pl.program_id(0)
        x = h_ref[...]
        gg = g_ref[...]
        r = lax.rsqrt(jnp.mean(x * x, axis=-1, keepdims=True) + EPS_RMS)
        xr = x * r
        e = xr * gg - t_ref[...]
        lp = 0.5 * jnp.sum(jnp.mean(e * e, axis=-1, keepdims=True), axis=0, keepdims=True)
        dn = e * (1.0 / D)
        q = dn * gg
        dh = r * q - x * ((r * r * r) * jnp.mean(q * x, axis=-1, keepdims=True))
        dh_ref[...] = dh
        dhb_ref[...] = dh.astype(BF16)
        dgp = jnp.sum(dn * xr, axis=0, keepdims=True)

        @pl.when(i == 0)
        def _():
            dg_ref[...] = dgp
            loss_ref[...] = lp

        @pl.when(i > 0)
        def _():
            dg_ref[...] += dgp
            loss_ref[...] += lp

    return _pallas(
        body, name="loss_head", grid=(T // tm,),
        in_specs=[pl.BlockSpec((tm, D), lambda i: (i, 0)), pl.BlockSpec((1, D), lambda i: (0, 0)),
                  pl.BlockSpec((tm, D), lambda i: (i, 0))],
        out_specs=[pl.BlockSpec((tm, D), lambda i: (i, 0)), pl.BlockSpec((tm, D), lambda i: (i, 0)),
                   pl.BlockSpec((1, D), lambda i: (0, 0)), pl.BlockSpec((1, 1), lambda i: (0, 0))],
        out_shape=[jax.ShapeDtypeStruct((T, D), F32), jax.ShapeDtypeStruct((T, D), BF16),
                   jax.ShapeDtypeStruct((1, D), F32), jax.ShapeDtypeStruct((1, 1), F32)],
        compiler_params=_params("arbitrary"))(h, g, tgt)


def _shift_up(x, j):
    return x if j == 0 else pltpu.roll(x, x.shape[0] - j, 0)


def _shift_down(x, j):
    return x if j == 0 else pltpu.roll(x, j, 0)


def _fill_shifted(dst_ref, src_ref):
    rows = dst_ref.shape[1]
    for s in range(8):
        dst_ref[s] = src_ref[pl.ds(s, rows), :]


def _fill_taps(wb_ref, w_ref):
    for k in range(w_ref.shape[0]):
        wb_ref[k] = jnp.broadcast_to(w_ref[k:k + 1, :], wb_ref.shape[1:])


def _tap_sum(sh_ref, wb_ref, r0, nrows, offsets):
    accs = [None] * (nrows // 8)
    for k, o in enumerate(offsets):
        wk = wb_ref[k]
        for u in range(nrows // 8):
            term = wk * sh_ref[o % 8, pl.ds(r0 + (o // 8) * 8 + 8 * u, 8), :]
            accs[u] = term if accs[u] is None else accs[u] + term
    return jnp.concatenate(accs, axis=0)


def _pool_sums(vx, up):
    sh = _shift_up if up else _shift_down
    outs = []
    for gi, w in enumerate(POOL_WINDOWS):
        s = vx[:, gi * POOL_GW:(gi + 1) * POOL_GW]
        j = 1
        while j < w:
            s = s + sh(s, j)
            j *= 2
        outs.append(s)
    return outs


def _inv_count(row0, nrows):
    pos = (row0 + 1 + lax.broadcasted_iota(jnp.int32, (nrows, 1), 0)).astype(F32)
    return [1.0 / jnp.minimum(pos, float(w)) for w in POOL_WINDOWS]


def _even_mixer_fwd(p, sl, cw, cb, lg, lb, pw, pb, sc, name, h=None, w_out=None):
    T = p.shape[0]
    C = D_MODEL
    tT, HL = MIX_TILE, EVEN_HALO
    hb = tT // HL
    chunk = 32
    fused = w_out is not None

    def body(pm_ref, ph_ref, cw_ref, cb_ref, lg_ref, lb_ref, pw_ref, pb_ref, sc_ref, *rest):
        if fused:
            h_ref, wo_ref, y_ref, u1_ref, hn_ref, u0x_ref, sh_ref, wb_ref = rest
        else:
            y_ref, u1_ref, u0x_ref, sh_ref, wb_ref = rest
        i = pl.program_id(0)
        keep = (i > 0).astype(F32)

        @pl.when(i == 0)
        def _():
            _fill_taps(wb_ref, cw_ref)

        u0x_ref[0:HL] = ph_ref[:, 0:C] * _sigmoid(ph_ref[:, C:2 * C]) * keep
        u0x_ref[HL:HL + tT] = pm_ref[:, 0:C] * _sigmoid(pm_ref[:, C:2 * C])
        u0x_ref[HL + tT:HL + tT + 8] = jnp.zeros((8, C), F32)
        _fill_shifted(sh_ref, u0x_ref)
        offs = [HL - (CONV_K - 1) + k for k in range(CONV_K)]

        def conv_chunk(c, carry):
            r0 = pl.multiple_of(c * chunk, chunk)
            u1_ref[pl.ds(r0, chunk), :] = _tap_sum(sh_ref, wb_ref, r0, chunk, offs) + cb_ref[...]
            return carry

        lax.fori_loop(0, tT // chunk, conv_chunk, 0)
        u1 = u1_ref[...]
        mu = jnp.mean(u1, axis=-1, keepdims=True)
        xc = u1 - mu
        rs = lax.rsqrt(jnp.mean(xc * xc, axis=-1, keepdims=True) + EPS_LN)
        u2 = xc * rs * lg_ref[...] + lb_ref[...]
        u3 = u2 * _sigmoid(u2)
        ag = pm_ref[:, 2 * C:3 * C]
        y_ref[:, 0:C] = (u3 * (ag * _sigmoid(ag))).astype(BF16)
        vx = jnp.concatenate([ph_ref[:, 3 * C:4 * C] * keep, pm_ref[:, 3 * C:4 * C]], axis=0)
        sums = _pool_sums(vx, up=False)
        inv = _inv_count(i * tT, tT)
        for gi in range(len(POOL_WINDOWS)):
            cols = slice(gi * POOL_GW, (gi + 1) * POOL_GW)
            d0 = sums[gi][HL:] * inv[gi] - vx[HL:, cols]
            d1 = jnp.dot(d0.astype(BF16), pw_ref[gi], preferred_element_type=F32) + pb_ref[:, cols]
            bg = pm_ref[:, 4 * C + gi * POOL_GW:4 * C + (gi + 1) * POOL_GW]
            y_ref[:, C + gi * POOL_GW:C + (gi + 1) * POOL_GW] = (d1 * sc_ref[:, cols] * (bg * _sigmoid(bg))).astype(BF16)
        if fused:
            hn_ref[...] = h_ref[...] + jnp.dot(y_ref[...], wo_ref[...], preferred_element_type=F32)

    vec = pl.BlockSpec((None, 1, C), lambda i: (sl, 0, 0))
    rows = pl.BlockSpec((tT, C), lambda i: (i, 0))
    in_specs = [pl.BlockSpec((tT, 5 * C), lambda i: (i, 0)),
                pl.BlockSpec((HL, 5 * C), lambda i: (jnp.maximum(i * hb - 1, 0), 0)),
                pl.BlockSpec((None, 32, C), lambda i: (sl, 0, 0)), vec, vec, vec,
                pl.BlockSpec((4, POOL_GW, POOL_GW), lambda i: (0, 0, 0)), vec, vec]
    out_specs = [pl.BlockSpec((tT, 2 * C), lambda i: (i, 0)), rows]
    out_shape = [jax.ShapeDtypeStruct((T, 2 * C), BF16), jax.ShapeDtypeStruct((T, C), F32)]
    args = (p, p, cw, cb, lg, lb, pw, pb, sc)
    if fused:
        in_specs += [rows, pl.BlockSpec((None, 2 * C, C), lambda i: (0, 0, 0))]
        out_specs.append(rows)
        out_shape.append(jax.ShapeDtypeStruct((T, C), F32))
        args += (h, w_out)
    return _pallas(
        body, name=name, grid=(T // tT,), in_specs=in_specs, out_specs=out_specs, out_shape=out_shape,
        scratch_shapes=[pltpu.VMEM((HL + tT + 8, C), F32), pltpu.VMEM((8, HL + tT, C), F32),
                        pltpu.VMEM((32, 8, C), F32)],
        compiler_params=_params("arbitrary"))(*args)


def _even_mixer_bwd(p, u1, dout, w_out, after, sl, cwr, lg, lb, pw, pb, sc, name):
    T = p.shape[0]
    C = D_MODEL
    tT, HL = MIX_TILE, EVEN_HALO
    hb = tT // HL
    nT = T // tT
    R1 = tT + HL
    chunk = 32

    def body(pm_ref, pp_ref, pn_ref, u1m_ref, u1n_ref, dom_ref, don_ref, wo_ref, after_ref, cwr_ref, lg_ref, lb_ref,
             pw_ref, pb_ref, sc_ref, dp_ref, dcw_ref, dvec_ref, dpw_ref, x_ref, sh_ref, du0_ref, wb_ref):
        i = pl.program_id(0)
        dy = _nt(jnp.concatenate([dom_ref[...], don_ref[...]], axis=0), wo_ref[...])

        @pl.when(i == 0)
        def _():
            _fill_taps(wb_ref, cwr_ref)

        keep_prev = (i > 0).astype(F32)
        keep_next = (i < nT - 1).astype(F32)
        row = lax.broadcasted_iota(jnp.int32, (R1, 1), 0)
        live = jnp.where(row < tT, 1.0, keep_next)

        def cat(m, n):
            return jnp.concatenate([m, n], axis=0)

        u1 = cat(u1m_ref[...], u1n_ref[...])
        mu = jnp.mean(u1, axis=-1, keepdims=True)
        xc = u1 - mu
        rs = lax.rsqrt(jnp.mean(xc * xc, axis=-1, keepdims=True) + EPS_LN)
        xh = xc * rs
        u2 = xh * lg_ref[...] + lb_ref[...]
        s2 = _sigmoid(u2)
        u3 = u2 * s2
        ag = cat(pm_ref[:, 2 * C:3 * C], pn_ref[:, 2 * C:3 * C])
        sa = _sigmoid(ag)
        dya = dy[:, 0:C]
        dp_ref[:, 2 * C:3 * C] = (dya * u3 * _dsilu(ag, sa))[0:tT].astype(BF16)
        du2 = dya * (ag * sa) * _dsilu(u2, s2)
        dlg = jnp.sum((du2 * xh)[0:tT], axis=0, keepdims=True)
        dlb = jnp.sum(du2[0:tT], axis=0, keepdims=True)
        dxh = du2 * lg_ref[...]
        du1 = rs * (dxh - jnp.mean(dxh, axis=-1, keepdims=True) - xh * jnp.mean(dxh * xh, axis=-1, keepdims=True))
        du1 = du1 * live
        dcb = jnp.sum(du1[0:tT], axis=0, keepdims=True)
        x_ref[0:R1] = du1
        x_ref[R1:R1 + 8] = jnp.zeros((8, C), F32)
        _fill_shifted(sh_ref, x_ref)

        def du0_chunk(c, carry):
            r0 = pl.multiple_of(c * chunk, chunk)
            du0_ref[pl.ds(r0, chunk), :] = _tap_sum(sh_ref, wb_ref, r0, chunk, list(range(CONV_K)))
            return carry

        lax.fori_loop(0, tT // chunk, du0_chunk, 0)
        av, agl = pm_ref[:, 0:C], pm_ref[:, C:2 * C]
        sg = _sigmoid(agl)
        du0 = du0_ref[...]
        dp_ref[:, 0:C] = (du0 * sg).astype(BF16)
        dp_ref[:, C:2 * C] = (du0 * av * sg * (1.0 - sg)).astype(BF16)
        du0_ref[...] = du1[0:tT]
        x_ref[0:HL] = pp_ref[:, 0:C] * _sigmoid(pp_ref[:, C:2 * C]) * keep_prev
        x_ref[HL:HL + tT] = av * sg
        x_ref[HL + tT:HL + tT + 8] = jnp.zeros((8, C), F32)
        _fill_shifted(sh_ref, x_ref)

        @pl.when(i == 0)
        def _():
            dcw_ref[...] = jnp.zeros_like(dcw_ref)

        for k0 in range(0, CONV_K, 2):
            taps = [k for k in (k0, k0 + 1) if k < CONV_K]
            offs = [HL - (CONV_K - 1) + k for k in taps]

            def dw_chunk(c, accs, offs=offs):
                r0 = pl.multiple_of(c * 64, 64)
                accs = list(accs)
                for u in range(0, 64, 8):
                    d = du0_ref[pl.ds(r0 + u, 8), :]
                    for t, o in enumerate(offs):
                        accs[t] = accs[t] + d * sh_ref[o % 8, pl.ds(r0 + u + (o // 8) * 8, 8), :]
                return tuple(accs)

            sums = lax.fori_loop(0, tT // 64, dw_chunk, tuple(jnp.zeros((8, C), F32) for _ in taps))
            for k, acc in zip(taps, sums):
                dcw_ref[8 * k:8 * k + 8, :] += acc

        bg = cat(pm_ref[:, 4 * C:5 * C], pn_ref[:, 4 * C:5 * C])
        sb = _sigmoid(bg)
        dyb = dy[:, C:2 * C]
        dyb0 = dyb * (bg * sb)
        dd1 = dyb0 * sc_ref[...]
        dpb = jnp.sum(dd1[0:tT], axis=0, keepdims=True)
        inv1 = _inv_count(i * tT, R1)
        z_parts, dd0_parts = [], []
        for gi in range(len(POOL_WINDOWS)):
            cols = slice(gi * POOL_GW, (gi + 1) * POOL_GW)
            dd0 = _nt(dd1[:, cols].astype(BF16), pw_ref[gi])
            dd0_parts.append(dd0)
            z_parts.append(dd0 * inv1[gi] * live)
        fsum = _pool_sums(jnp.concatenate(z_parts, axis=1), up=True)
        vx = cat(pp_ref[:, 3 * C:4 * C] * keep_prev, pm_ref[:, 3 * C:4 * C])
        sums = _pool_sums(vx, up=False)
        inv0 = _inv_count(i * tT, tT)
        dsc_parts = []
        for gi in range(len(POOL_WINDOWS)):
            cols = slice(gi * POOL_GW, (gi + 1) * POOL_GW)
            dp_ref[:, 3 * C + gi * POOL_GW:3 * C + (gi + 1) * POOL_GW] = (fsum[gi][0:tT] - dd0_parts[gi][0:tT]).astype(BF16)
            d0 = (sums[gi][HL:] * inv0[gi] - vx[HL:, cols]).astype(BF16)
            d1 = jnp.dot(d0, pw_ref[gi], preferred_element_type=F32) + pb_ref[:, cols]
            bgm, sbm = bg[0:tT, cols], sb[0:tT, cols]
            dp_ref[:, 4 * C + gi * POOL_GW:4 * C + (gi + 1) * POOL_GW] = (
                dyb[0:tT, cols] * d1 * sc_ref[:, cols] * _dsilu(bgm, sbm)).astype(BF16)
            dsc_parts.append(jnp.sum(dyb0[0:tT, cols] * d1, axis=0, keepdims=True))
            dpw_g = _tn(d0, dd1[0:tT, cols].astype(BF16))

            @pl.when(i == 0)
            def _(gi=gi, dpw_g=dpw_g):
                dpw_ref[gi] = dpw_g

            @pl.when(i > 0)
            def _(gi=gi, dpw_g=dpw_g):
                dpw_ref[gi] += dpw_g

        dsc = jnp.concatenate(dsc_parts, axis=1)
        vecs = jnp.concatenate([dcb, dlg, dlb, dsc, dpb, jnp.zeros((3, C), F32)], axis=0)

        @pl.when(i == 0)
        def _():
            dvec_ref[...] = vecs

        @pl.when(i > 0)
        def _():
            dvec_ref[...] += vecs

    vec = pl.BlockSpec((None, 1, C), lambda i: (sl, 0, 0))
    taps = pl.BlockSpec((None, 32, C), lambda i: (sl, 0, 0))

    def prev_blk(i):
        return (jnp.maximum(i * hb - 1, 0), 0)

    def next_blk(i):
        return (jnp.minimum((i + 1) * hb, T // HL - 1), 0)

    return _pallas(
        body, name=name, grid=(nT,),
        in_specs=[pl.BlockSpec((tT, 5 * C), lambda i: (i, 0)), pl.BlockSpec((HL, 5 * C), prev_blk),
                  pl.BlockSpec((HL, 5 * C), next_blk),
                  pl.BlockSpec((tT, C), lambda i: (i, 0)), pl.BlockSpec((HL, C), next_blk),
                  pl.BlockSpec((tT, C), lambda i: (i, 0)), pl.BlockSpec((HL, C), next_blk),
                  pl.BlockSpec((None, 2 * C, C), lambda i: (0, 0, 0)), pl.BlockSpec((8, 128), lambda i: (0, 0)),
                  taps, vec, vec, pl.BlockSpec((4, POOL_GW, POOL_GW), lambda i: (0, 0, 0)), vec, vec],
        out_specs=[pl.BlockSpec((tT, 5 * C), lambda i: (i, 0)), pl.BlockSpec((32 * 8, C), lambda i: (0, 0)),
                   pl.BlockSpec((8, C), lambda i: (0, 0)), pl.BlockSpec((4, POOL_GW, POOL_GW), lambda i: (0, 0, 0))],
        out_shape=[jax.ShapeDtypeStruct((T, 5 * C), BF16), jax.ShapeDtypeStruct((32 * 8, C), F32),
                   jax.ShapeDtypeStruct((8, C), F32), jax.ShapeDtypeStruct((4, POOL_GW, POOL_GW), F32)],
        scratch_shapes=[pltpu.VMEM((R1 + 8, C), F32), pltpu.VMEM((8, R1, C), F32), pltpu.VMEM((tT, C), F32),
                        pltpu.VMEM((32, 8, C), F32)],
        compiler_params=_params("arbitrary"))(p, p, p, u1, u1, dout, dout, w_out, after, cwr, lg, lb, pw, pb, sc)


def _softplus(z):
    u = jnp.exp(-jnp.abs(z))
    w = 1.0 + u
    l1p = jnp.where(w == 1.0, u, u * jnp.log(w) / jnp.where(w == 1.0, 1.0, w - 1.0))
    return jnp.maximum(z, 0.0) + l1p


def _lru_gates(xrx, cw_ref, cb_ref, wr_ref, br_ref, wi_ref, bi_ref, lam_ref):
    HL = ODD_HALO
    xc = cb_ref[...] + cw_ref[LRU_CONV_K - 1:LRU_CONV_K, :] * xrx[HL:]
    for k in range(LRU_CONV_K - 1):
        xc = xc + cw_ref[k:k + 1, :] * _shift_down(xrx, LRU_CONV_K - 1 - k)[HL:]
    xcb = xc.astype(BF16)
    rp, ip = [], []
    for hd in range(LRU_HEADS):
        cols = slice(hd * LRU_HD, (hd + 1) * LRU_HD)
        rp.append(jnp.dot(xcb[:, cols], wr_ref[hd], preferred_element_type=F32))
        ip.append(jnp.dot(xcb[:, cols], wi_ref[hd], preferred_element_type=F32))
    r = _sigmoid(jnp.concatenate(rp, axis=1) + br_ref[...])
    ig = _sigmoid(jnp.concatenate(ip, axis=1) + bi_ref[...])
    sp = _softplus(-lam_ref[...])
    log_a = (-LRU_C) * r * sp
    a = jnp.exp(log_a)
    m2 = jnp.maximum(-jnp.tanh(log_a) * (a * a + 1.0), 1e-30)
    inv_mult = lax.rsqrt(m2)
    return xc, xcb, r, ig, sp, a, m2 * inv_mult, inv_mult


def _group_scan(a, b, reverse):
    n, w = a.shape
    a, b = a.reshape(n // 8, 8, w), b.reshape(n // 8, 8, w)
    pos = lax.broadcasted_iota(jnp.int32, (1, 8, 1), 1)
    s = 1
    while s < 8:
        ok = (pos < 8 - s) if reverse else (pos >= s)
        shift = (8 - s) if reverse else s
        a_sh = jnp.where(ok, pltpu.roll(a, shift, 1), 1.0)
        b_sh = jnp.where(ok, pltpu.roll(b, shift, 1), 0.0)
        b = a * b_sh + b
        a = a * a_sh
        s *= 2
    return a.reshape(n, w), b.reshape(n, w)


def _apply_carries(a_ref, b_ref, out_ref, c0, reverse):
    ng = a_ref.shape[0] // 8

    def step(t, c):
        r0 = pl.multiple_of(((ng - 1 - t) if reverse else t) * 8, 8)
        x = a_ref[pl.ds(r0, 8), :] * c + b_ref[pl.ds(r0, 8), :]
        out_ref[pl.ds(r0, 8), :] = x
        return x[0:1, :] if reverse else x[7:8, :]

    return lax.fori_loop(0, ng, step, c0)


def _odd_mixer_fwd(p, sl, cw, cb, wr, br, wi, bi, lam, name, h=None, w_out=None):
    T = p.shape[0]
    W = W_LRU
    D = D_MODEL
    tT, HL = MIX_TILE, ODD_HALO
    hb = tT // HL
    fused = w_out is not None

    def body(pm_ref, ph_ref, cw_ref, cb_ref, wr_ref, br_ref, wi_ref, bi_ref, lam_ref, *rest):
        if fused:
            h_ref, wo_ref, y_ref, hs_ref, hn_ref, carry_ref, sa_ref, sb_ref = rest
        else:
            y_ref, hs_ref, carry_ref, sa_ref, sb_ref = rest
        i = pl.program_id(0)
        keep = (i > 0).astype(F32)

        @pl.when(i == 0)
        def _():
            carry_ref[...] = jnp.zeros_like(carry_ref)

        xrx = jnp.concatenate([ph_ref[:, 0:W] * keep, pm_ref[:, 0:W]], axis=0)
        xc, _, _, ig, _, a, mult, _ = _lru_gates(xrx, cw_ref, cb_ref, wr_ref, br_ref, wi_ref, bi_ref, lam_ref)
        sa_ref[...], sb_ref[...] = _group_scan(a, mult * (ig * xc), reverse=False)
        last = _apply_carries(sa_ref, sb_ref, hs_ref, carry_ref[0:1, :], reverse=False)
        carry_ref[...] = jnp.broadcast_to(last, (8, W))
        hs = hs_ref[...]
        gt = pm_ref[:, W:2 * W]
        y_ref[...] = (hs * (gt * _sigmoid(gt))).astype(BF16)
        if fused:
            hn_ref[...] = h_ref[...] + jnp.dot(y_ref[...], wo_ref[...], preferred_element_type=F32)

    vec = pl.BlockSpec((None, 1, W), lambda i: (sl, 0, 0))
    heads = pl.BlockSpec((None, LRU_HEADS, LRU_HD, LRU_HD), lambda i: (sl, 0, 0, 0))
    rows = pl.BlockSpec((tT, D), lambda i: (i, 0))
    in_specs = [pl.BlockSpec((tT, 2 * W), lambda i: (i, 0)),
                pl.BlockSpec((HL, 2 * W), lambda i: (jnp.maximum(i * hb - 1, 0), 0)),
                pl.BlockSpec((None, 8, W), lambda i: (sl, 0, 0)), vec, heads, vec, heads, vec, vec]
    out_specs = [pl.BlockSpec((tT, W), lambda i: (i, 0)), pl.BlockSpec((tT, W), lambda i: (i, 0))]
    out_shape = [jax.ShapeDtypeStruct((T, W), BF16), jax.ShapeDtypeStruct((T, W), F32)]
    args = (p, p, cw, cb, wr, br, wi, bi, lam)
    if fused:
        in_specs += [rows, pl.BlockSpec((None, W, D), lambda i: (0, 0, 0))]
        out_specs.append(rows)
        out_shape.append(jax.ShapeDtypeStruct((T, D), F32))
        args += (h, w_out)
    return _pallas(
        body, name=name, grid=(T // tT,), in_specs=in_specs, out_specs=out_specs, out_shape=out_shape,
        scratch_shapes=[pltpu.VMEM((8, W), F32), pltpu.VMEM((tT, W), F32), pltpu.VMEM((tT, W), F32)],
        compiler_params=_params("arbitrary"))(*args)


def _odd_mixer_bwd(p, hs, dout, w_out, after, sl, cw, cb, wr, br, wi, bi, lam, name):
    T = p.shape[0]
    W = W_LRU
    D = dout.shape[1]
    tT, HL = MIX_TILE, ODD_HALO
    hb = tT // HL
    nT = T // tT

    def body(pm_ref, ph_ref, hsm_ref, hsh_ref, do_ref, wo_ref, after_ref, cw_ref, cb_ref, wr_ref, br_ref, wi_ref,
             bi_ref, lam_ref, dp_ref, dwr_ref, dwi_ref, dvec_ref, gcarry_ref, xcarry_ref, sa_ref, sb_ref, g_ref):
        i = pl.program_id(0)
        keep = (i < nT - 1).astype(F32)

        @pl.when(i == 0)
        def _():
            gcarry_ref[...] = jnp.zeros_like(gcarry_ref)
            xcarry_ref[...] = jnp.zeros_like(xcarry_ref)

        xrx = jnp.concatenate([ph_ref[:, 0:W] * keep, pm_ref[:, 0:W]], axis=0)
        xc, xcb, r, ig, sp, a, mult, inv_mult = _lru_gates(xrx, cw_ref, cb_ref, wr_ref, br_ref, wi_ref, bi_ref, lam_ref)
        hs = hsm_ref[...]
        gt = pm_ref[:, W:2 * W]
        sg = _sigmoid(gt)
        dyv = _nt(do_ref[...], wo_ref[...])
        dp_ref[:, W:2 * W] = (dyv * hs * _dsilu(gt, sg)).astype(BF16)
        row = lax.broadcasted_iota(jnp.int32, (tT, 1), 0)
        m = jnp.where(row == tT - 1, 1.0, _shift_up(a, 1))
        sa_ref[...], sb_ref[...] = _group_scan(m, dyv * (gt * sg), reverse=True)
        first = _apply_carries(sa_ref, sb_ref, g_ref, gcarry_ref[0:1, :], reverse=True)
        G = g_ref[...]
        gcarry_ref[...] = jnp.broadcast_to(a[0:1, :] * first, (8, W))
        hs_prev = jnp.where(row == 0, hsh_ref[HL - 1:HL, :] * keep, _shift_down(hs, 1))
        da = G * hs_prev
        dmult = G * (ig * xc)
        di = G * mult * xc
        dxc = G * mult * ig
        dlog_a = da * a - dmult * (a * a) * inv_mult
        drp = dlog_a * ((-LRU_C) * sp) * r * (1.0 - r)
        dip = di * ig * (1.0 - ig)
        dlam = jnp.sum(dlog_a * ((-LRU_C) * r), axis=0, keepdims=True) * (-_sigmoid(-lam_ref[...]))
        drb, dib = drp.astype(BF16), dip.astype(BF16)
        back = []
        for hd in range(LRU_HEADS):
            cols = slice(hd * LRU_HD, (hd + 1) * LRU_HD)
            back.append(_nt(drb[:, cols], wr_ref[hd]) + _nt(dib[:, cols], wi_ref[hd]))
            dwr_h = _tn(xcb[:, cols], drb[:, cols])
            dwi_h = _tn(xcb[:, cols], dib[:, cols])

            @pl.when(i == 0)
            def _(hd=hd, dwr_h=dwr_h, dwi_h=dwi_h):
                dwr_ref[hd] = dwr_h
                dwi_ref[hd] = dwi_h

            @pl.when(i > 0)
            def _(hd=hd, dwr_h=dwr_h, dwi_h=dwi_h):
                dwr_ref[hd] += dwr_h
                dwi_ref[hd] += dwi_h

        dxc = dxc + jnp.concatenate(back, axis=1)
        dxcx = jnp.concatenate([dxc, xcarry_ref[...]], axis=0)
        dxr = cw_ref[LRU_CONV_K - 1:LRU_CONV_K, :] * dxc
        rows = []
        for k in range(LRU_CONV_K - 1):
            j = LRU_CONV_K - 1 - k
            dxr = dxr + cw_ref[k:k + 1, :] * _shift_up(dxcx, j)[0:tT]
            rows.append(jnp.sum(dxc * _shift_down(xrx, j)[HL:], axis=0, keepdims=True))
        rows.append(jnp.sum(dxc * xrx[HL:], axis=0, keepdims=True))
        dp_ref[:, 0:W] = dxr.astype(BF16)
        xcarry_ref[...] = dxc[0:8]
        rows += [jnp.sum(dxc, axis=0, keepdims=True), jnp.sum(drp, axis=0, keepdims=True),
                 jnp.sum(dip, axis=0, keepdims=True), dlam]
        vecs = jnp.concatenate(rows, axis=0)

        @pl.when(i == 0)
        def _():
            dvec_ref[...] = vecs

        @pl.when(i > 0)
        def _():
            dvec_ref[...] += vecs

    vec = pl.BlockSpec((None, 1, W), lambda i: (sl, 0, 0))
    heads = pl.BlockSpec((None, LRU_HEADS, LRU_HD, LRU_HD), lambda i: (sl, 0, 0, 0))
    dheads = pl.BlockSpec((LRU_HEADS, LRU_HD, LRU_HD), lambda i: (0, 0, 0))

    def tile(i):
        return (nT - 1 - i, 0)

    def prev_blk(i):
        return (jnp.maximum((nT - 1 - i) * hb - 1, 0), 0)

    return _pallas(
        body, name=name, grid=(nT,),
        in_specs=[pl.BlockSpec((tT, 2 * W), tile), pl.BlockSpec((HL, 2 * W), prev_blk),
                  pl.BlockSpec((tT, W), tile), pl.BlockSpec((HL, W), prev_blk), pl.BlockSpec((tT, D), tile),
                  pl.BlockSpec((None, W, D), lambda i: (0, 0, 0)), pl.BlockSpec((8, 128), lambda i: (0, 0)),
                  pl.BlockSpec((None, 8, W), lambda i: (sl, 0, 0)), vec, heads, vec, heads, vec, vec],
        out_specs=[pl.BlockSpec((tT, 2 * W), tile), dheads, dheads, pl.BlockSpec((8, W), lambda i: (0, 0))],
        out_shape=[jax.ShapeDtypeStruct((T, 2 * W), BF16), jax.ShapeDtypeStruct((LRU_HEADS, LRU_HD, LRU_HD), F32),
                   jax.ShapeDtypeStruct((LRU_HEADS, LRU_HD, LRU_HD), F32), jax.ShapeDtypeStruct((8, W), F32)],
        scratch_shapes=[pltpu.VMEM((8, W), F32), pltpu.VMEM((8, W), F32), pltpu.VMEM((tT, W), F32),
                        pltpu.VMEM((tT, W), F32), pltpu.VMEM((tT, W), F32)],
        compiler_params=_params("arbitrary"))(p, p, hs, hs, dout, w_out, after, cw, cb, wr, br, wi, bi, lam)


def _pad_rows(a, rows):
    return jnp.pad(a, ((0, 0), (0, rows - a.shape[1]), (0, 0)))


def _layer_fwd(even, h, w, w_in, w_out, after):
    sl = w["sl"]
    p, n = _in_proj(h, w["norm"], sl, w_in, 0, after, "in_proj_even" if even else "in_proj_odd")
    late = callable(w_out)
    fuse = {} if late else dict(h=h, w_out=w_out)
    if even:
        y, aux, *rest = _even_mixer_fwd(p, sl, w["conv_w"], w["conv_b"], w["ln_g"], w["ln_b"], w["pool_w"], w["pool_b"],
                                        w["pool_scale"], "even_mixer_fwd", **fuse)
    else:
        y, aux, *rest = _odd_mixer_fwd(p, sl, w["conv_w"], w["conv_b"], w["w_rg"], w["b_rg"], w["w_ig"], w["b_ig"],
                                       w["lam"], "odd_mixer_fwd", **fuse)
    if late:
        w_out = w_out(y)
        rest = [_out_proj(y, w_out, 0, h, "out_proj_even" if even else "out_proj_odd")]
    return rest[0], (h, n, p, aux, y), w_out


def _layer_bwd_weights(even, saved, w, w_out, dhb, after):
    h, n, p, aux, y = saved
    if even:
        dp, dcw, dvec, dpw = _even_mixer_bwd(p, aux, dhb, w_out, after, w["sl"], w["conv_w_rev"], w["ln_g"], w["ln_b"],
                                             w["pool_w"], w["pool_b"], w["pool_scale"], "even_mixer_bwd")
        dw_out = _dw_out(y, dhb, 0, 1, None, "dw_out_even")
        dw_in = _dw_in(n, dp, N_CHIPS, 0, 1, None, "dw_in_even")
        return dp, dw_in, dw_out, dict(conv_w=dcw, vec=dvec, pool_w=dpw)
    dp, dwr, dwi, dvec = _odd_mixer_bwd(p, aux, dhb, w_out, after, w["sl"], w["conv_w"], w["conv_b"], w["w_rg"],
                                        w["b_rg"], w["w_ig"], w["b_ig"], w["lam"], "odd_mixer_bwd")
    dw_out = _dw_out(y, dhb, 0, 1, None, "dw_out_odd")
    dw_in = _dw_in(n, dp, N_CHIPS, 0, 1, None, "dw_in_odd")
    return dp, dw_in, dw_out, dict(w_rg=dwr, w_ig=dwi, vec=dvec)


def _layer_bwd_input(even, saved, w, w_in, dp, dh, after):
    return _dn_proj(dp, w_in, 0, saved[0], w["norm"], w["sl"], dh, after, "dn_proj_even" if even else "dn_proj_odd")


ANY = pl.BlockSpec(memory_space=pl.ANY)


def _mesh_pos():
    return lax.axis_index("x"), lax.axis_index("y"), lax.axis_index("c")


def _other_chips(x, y):
    return [(1 - x, y), (x, 1 - y), (1 - x, 1 - y)]


def _remote(src, dst, ssem, rsem, dev):
    return pltpu.make_async_remote_copy(src_ref=src, dst_ref=dst, send_sem=ssem, recv_sem=rsem, device_id=dev,
                                        device_id_type=MESH)


def _comm_call(body, name, ins, out_shape, scratch, aliases=None):
    return _pallas(body, name=name, in_specs=[ANY] * len(ins), out_specs=[ANY] * len(out_shape), out_shape=out_shape,
                   scratch_shapes=scratch, input_output_aliases=aliases or {},
                   compiler_params=pltpu.CompilerParams(has_side_effects=True))(*ins)


def _cast_shard(w, layer, pos):
    _, R, C = w.shape
    tr = _row_tile(R, C)

    def body(pos_ref, w_ref, o_ref):
        o_ref[...] = w_ref[...].astype(BF16)

    grid_spec = pltpu.PrefetchScalarGridSpec(
        num_scalar_prefetch=1, grid=(R // tr,),
        in_specs=[pl.BlockSpec((None, tr, C), lambda i, pr: (layer, i, 0))],
        out_specs=pl.BlockSpec((None, None, tr, C), lambda i, pr: (0, pr[0], i, 0)))
    return _pallas(body, name="cast_shard", grid_spec=grid_spec,
                   out_shape=jax.ShapeDtypeStruct((1, N_CHIPS, R, C), BF16),
                   compiler_params=_params("parallel"))(pos, w)


def _gather_weights(big, small):
    nA = len(big)
    half = [a.shape[2] // 2 for a in big]

    def body(*refs):
        ins, outs = refs[:nA + 1], refs[nA + 1:2 * nA + 2]
        ssem, rsem, fsem, frsem, lsem = refs[2 * nA + 2:]
        x, y, c = _mesh_pos()
        k = 2 * x + y
        chips = _other_chips(x, y)
        sib = (x, y, 1 - c)

        def slab(a, chip, core):
            return outs[a].at[:, chip, pl.ds(core * half[a], half[a]), :]

        local = [pltpu.make_async_copy(ins[nA], outs[nA].at[k], lsem.at[0])]
        for cp in local:
            cp.start()
        sends = []
        for j, (ox, oy) in enumerate(chips):
            for a in range(nA):
                sends.append(_remote(slab(a, k, c), slab(a, k, c), ssem.at[a, j], rsem.at[a, j], (ox, oy, c)))
            sends.append(_remote(ins[nA], outs[nA].at[k], ssem.at[nA, j], rsem.at[nA, j], (ox, oy, c)))
        for cp in sends:
            cp.start()
        for j, (ox, oy) in enumerate(chips):
            kj = 2 * ox + oy
            for a in range(nA):
                got = slab(a, kj, c)
                _remote(got, got, ssem.at[a, j], rsem.at[a, j], (ox, oy, c)).wait_recv()
                fw = _remote(got, got, fsem.at[a, j], frsem.at[a, j], sib)
                fw.start()
                sends.append(fw)
            gs = outs[nA].at[kj]
            _remote(gs, gs, ssem.at[nA, j], rsem.at[nA, j], (ox, oy, c)).wait_recv()
        for j, (ox, oy) in enumerate(chips):
            kj = 2 * ox + oy
            for a in range(nA):
                theirs = slab(a, kj, 1 - c)
                _remote(theirs, theirs, fsem.at[a, j], frsem.at[a, j], sib).wait_recv()
        for cp in sends:
            cp.wait_send()
        for cp in local:
            cp.wait()

    out_shape = [jax.ShapeDtypeStruct(a.shape, a.dtype) for a in big]
    out_shape.append(jax.ShapeDtypeStruct((N_CHIPS,) + small.shape, small.dtype))
    scratch = [pltpu.SemaphoreType.DMA((nA + 1, 3)), pltpu.SemaphoreType.DMA((nA + 1, 3)),
               pltpu.SemaphoreType.DMA((nA, 3)), pltpu.SemaphoreType.DMA((nA, 3)), pltpu.SemaphoreType.DMA((1,))]
    return _comm_call(body, "gather_weights", list(big) + [small], out_shape, scratch, {a: a for a in range(nA)})


HBM = pl.BlockSpec(memory_space=pltpu.HBM)
SEM = pl.BlockSpec(memory_space=pltpu.SEMAPHORE)
EFFECT = pltpu.SideEffectType.DATAFLOW_SIDE_EFFECTING


def _split_start(arrays, copies, n, name):
    k = len(arrays)

    def body(*refs):
        for cp in copies(refs[k + 2:2 * k + 2], refs[k], refs[k + 1]):
            cp.start()
        refs[2 * k + 2][...] = jnp.zeros((8, 128), F32)

    out = _pallas(
        body, name=name,
        out_shape=(pltpu.SemaphoreType.DMA((n,)), pltpu.SemaphoreType.DMA((n,)),
                   *[pltpu.HBM(a.shape, a.dtype) for a in arrays], jax.ShapeDtypeStruct((8, 128), F32)),
        in_specs=(HBM,) * k, out_specs=(SEM, SEM) + (HBM,) * k + (pl.BlockSpec(memory_space=pltpu.VMEM),),
        input_output_aliases={i: i + 2 for i in range(k)},
        compiler_params=pltpu.CompilerParams(has_side_effects=EFFECT),
    )(*[pltpu.with_memory_space_constraint(a, pltpu.HBM) for a in arrays])
    return out[0], out[1], list(out[2:2 + k]), out[2 + k]


def _split_wait(ssem, rsem, arrays, copies, after, name):
    k = len(arrays)

    def body(*refs):
        for cp in copies(refs[:k], refs[k], refs[k + 1]):
            cp.wait_send()
            cp.wait_recv()

    out = _pallas(
        body, name=name, out_shape=tuple(pltpu.HBM(a.shape, a.dtype) for a in arrays),
        in_specs=(HBM,) * k + (SEM, SEM, ANY), out_specs=(HBM,) * k, input_output_aliases={i: i for i in range(k)},
        compiler_params=pltpu.CompilerParams(has_side_effects=EFFECT),
    )(*arrays, ssem, rsem, after)
    return list(out)


def _gather_copies(shapes):
    half = [s[2] // 2 for s in shapes]

    def copies(refs, ssem, rsem):
        x, y, c = _mesh_pos()
        out = []
        for j, (ox, oy) in enumerate(_other_chips(x, y)):
            for a, ref in enumerate(refs):
                slab = ref.at[:, 2 * x + y, pl.ds(c * half[a], half[a]), :]
                out.append(_remote(slab, slab, ssem.at[3 * a + j], rsem.at[3 * a + j], (ox, oy, c)))
        return out

    return copies


def _chips_copies(n_arr):
    def copies(refs, ssem, rsem):
        x, y, c = _mesh_pos()
        out = []
        for j, (ox, oy) in enumerate(_other_chips(x, y)):
            for a in range(n_arr):
                out.append(_remote(refs[a].at[:, 2 * ox + oy], refs[n_arr + a].at[:, 2 * x + y], ssem.at[3 * a + j],
                                   rsem.at[3 * a + j], (ox, oy, c)))
        return out

    return copies


def _halves_copies(shapes):
    n = len(shapes)
    half = [s[2] // 2 for s in shapes]

    def copies(refs, ssem, rsem):
        x, y, c = _mesh_pos()
        return [_remote(refs[a].at[:, :, pl.ds((1 - c) * half[a], half[a]), :], refs[n + a], ssem.at[a], rsem.at[a],
                        (x, y, 1 - c)) for a in range(n)]

    return copies


def _forward_cores(arrays):
    nA = len(arrays)
    half = [a.shape[2] // 2 for a in arrays]

    def body(*refs):
        outs = refs[nA:2 * nA]
        ssem, rsem = refs[2 * nA:]
        x, y, c = _mesh_pos()
        sib = (x, y, 1 - c)
        sends, waits = [], []
        for j, (ox, oy) in enumerate(_other_chips(x, y)):
            for a in range(nA):
                got = outs[a].at[:, 2 * ox + oy, pl.ds(c * half[a], half[a]), :]
                sends.append(_remote(got, got, ssem.at[a, j], rsem.at[a, j], sib))
                theirs = outs[a].at[:, 2 * ox + oy, pl.ds((1 - c) * half[a], half[a]), :]
                waits.append(_remote(theirs, theirs, ssem.at[a, j], rsem.at[a, j], sib))
        for cp in sends:
            cp.start()
        for cp in waits:
            cp.wait_recv()
        for cp in sends:
            cp.wait_send()

    out_shape = [jax.ShapeDtypeStruct(a.shape, a.dtype) for a in arrays]
    scratch = [pltpu.SemaphoreType.DMA((nA, 3)), pltpu.SemaphoreType.DMA((nA, 3))]
    return _comm_call(body, "forward_cores", list(arrays), out_shape, scratch, {a: a for a in range(nA)})


def _exchange_halves(big):
    nA = len(big)
    half = [a.shape[2] // 2 for a in big]

    def body(*refs):
        ins, outs = refs[:nA], refs[nA:2 * nA]
        ssem, rsem = refs[2 * nA:]
        x, y, c = _mesh_pos()
        sib = (x, y, 1 - c)
        sends = [_remote(ins[a].at[:, :, pl.ds((1 - c) * half[a], half[a]), :], outs[a], ssem.at[a], rsem.at[a], sib)
                 for a in range(nA)]
        for cp in sends:
            cp.start()
        for a in range(nA):
            _remote(outs[a], outs[a], ssem.at[a], rsem.at[a], sib).wait_recv()
        for cp in sends:
            cp.wait_send()

    out_shape = [jax.ShapeDtypeStruct((a.shape[0], N_CHIPS, h, a.shape[3]), a.dtype) for a, h in zip(big, half)]
    scratch = [pltpu.SemaphoreType.DMA((nA,)), pltpu.SemaphoreType.DMA((nA,))]
    return _comm_call(body, "exchange_halves", list(big), out_shape, scratch)


def _exchange_final(grads, everywhere, small):
    nA = len(grads)
    n_remote = sum(7 if ev else 1 for ev in everywhere) + 7

    def body(*refs):
        small_ref, outs, gathered = refs[nA], refs[nA + 1:2 * nA + 1], refs[2 * nA + 1]
        ssem, rsem, lsem = refs[2 * nA + 2:]
        x, y, c = _mesh_pos()
        k = 2 * x + y
        sib = (x, y, 1 - c)
        local = pltpu.make_async_copy(small_ref, gathered.at[2 * k + c], lsem.at[0])
        local.start()
        sends, arrivals, waits = [], [], []
        count = [0]

        def sems():
            count[0] += 1
            return ssem.at[count[0] - 1], rsem.at[count[0] - 1]

        def to_sibling(src, mine, theirs):
            sm = sems()
            sends.append(_remote(src, mine, *sm, sib))
            waits.append(_remote(theirs, theirs, *sm, sib))

        def to_everyone(src, place):
            to_sibling(src, place(k, c), place(k, 1 - c))
            for (ox, oy) in _other_chips(x, y):
                ici, d2d = sems(), sems()
                got = place(2 * ox + oy, c)
                sends.append(_remote(src, place(k, c), *ici, (ox, oy, c)))
                arrivals.append((_remote(got, got, *ici, (ox, oy, c)), _remote(got, got, *d2d, sib)))
                theirs = place(2 * ox + oy, 1 - c)
                waits.append(_remote(theirs, theirs, *d2d, sib))

        to_everyone(small_ref, lambda chip, core: gathered.at[2 * chip + core])
        for a in range(nA):
            if everywhere[a]:
                r2 = grads[a].shape[1] // N_DEV

                def place(chip, core, a=a, r2=r2):
                    return outs[a].at[:, pl.ds((2 * chip + core) * r2, r2), :]

                to_everyone(place(k, c), place)
            else:
                r2 = grads[a].shape[1] // 2
                mine = outs[a].at[:, pl.ds(c * r2, r2), :]
                to_sibling(mine, mine, outs[a].at[:, pl.ds((1 - c) * r2, r2), :])
        for cp in sends:
            cp.start()
        for arrived, onward in arrivals:
            arrived.wait_recv()
            onward.start()
        for cp in waits:
            cp.wait_recv()
        for cp in sends + [onward for _, onward in arrivals]:
            cp.wait_send()
        local.wait()

    out_shape = [jax.ShapeDtypeStruct(g.shape, g.dtype) for g in grads]
    out_shape.append(jax.ShapeDtypeStruct((N_DEV,) + small.shape, small.dtype))
    scratch = [pltpu.SemaphoreType.DMA((n_remote,)), pltpu.SemaphoreType.DMA((n_remote,)), pltpu.SemaphoreType.DMA((1,))]
    return _comm_call(body, "exchange_final", list(grads) + [small], out_shape, scratch, {a: a for a in range(nA)})


BLOCK_BYTES = 4 << 20


def _row_tile(rows, cols, mult=16, limit=BLOCK_BYTES):
    best = mult
    for t in range(mult, rows + 1, mult):
        if rows % t == 0 and t * cols * 4 <= limit:
            best = t
    return best


def _add_cores(own, recv, pos):
    L, _, R, C = own.shape
    r2 = R // 2
    tr = _row_tile(r2, C)
    nb = r2 // tr

    def body(pos_ref, a_ref, r_ref, o_ref):
        o_ref[...] = (a_ref[...].astype(F32) + r_ref[...].astype(F32)).astype(BF16)

    blk = (None, None, tr, C)
    grid_spec = pltpu.PrefetchScalarGridSpec(
        num_scalar_prefetch=1, grid=(L, N_CHIPS, nb),
        in_specs=[pl.BlockSpec(blk, lambda l, s, i, pr: (l, s, pr[1] * nb + i, 0)),
                  pl.BlockSpec(blk, lambda l, s, i, pr: (l, s, i, 0))],
        out_specs=pl.BlockSpec(blk, lambda l, s, i, pr: (l, s, i, 0)))
    return _pallas(body, name="add_cores", grid_spec=grid_spec,
                   out_shape=jax.ShapeDtypeStruct((L, N_CHIPS, r2, C), BF16),
                   compiler_params=_params("parallel", "parallel", "parallel"))(pos, own, recv)


def _sum_chips(own, recv, pos, everywhere, layer, nlayers, prev):
    _, _, r2, C = own.shape
    tr = _row_tile(r2, 2 * C)
    nb = r2 // tr

    def body(pos_ref, a_ref, r_ref, *rest):
        acc = None
        for s in range(N_CHIPS):
            term = jnp.where(pos_ref[0] == s, a_ref[...], r_ref[s]).astype(F32)
            acc = term if acc is None else acc + term
        rest[-1][...] = acc

    if everywhere:
        def out_map(i, pr):
            return (layer, (2 * pr[0] + pr[1]) * nb + i, 0)
    else:
        def out_map(i, pr):
            return (layer, pr[1] * nb + i, 0)

    in_specs = [pl.BlockSpec((None, None, tr, C), lambda i, pr: (0, pr[0], i, 0)),
                pl.BlockSpec((None, N_CHIPS, tr, C), lambda i, pr: (0, 0, i, 0))]
    grid_spec = pltpu.PrefetchScalarGridSpec(
        num_scalar_prefetch=1, grid=(nb,), in_specs=in_specs + ([] if prev is None else [ANY]),
        out_specs=pl.BlockSpec((None, tr, C), out_map))
    rows = (N_DEV if everywhere else 2) * r2
    args = (pos, own, recv) if prev is None else (pos, own, recv, prev)
    return _pallas(body, name="sum_chips", grid_spec=grid_spec, out_shape=jax.ShapeDtypeStruct((nlayers, rows, C), F32),
                   input_output_aliases={} if prev is None else {3: 0},
                   compiler_params=_params("parallel"))(*args)


def _sum_devices(parts):
    n, R, C = parts.shape
    tr = _row_tile(R, C * n, 8)

    def body(p_ref, o_ref):
        acc = p_ref[0]
        for s in range(1, n):
            acc = acc + p_ref[s]
        o_ref[...] = acc

    return _pallas(body, name="sum_devices", grid=(R // tr,), in_specs=[pl.BlockSpec((n, tr, C), lambda i: (0, i, 0))],
                   out_specs=pl.BlockSpec((tr, C), lambda i: (i, 0)), out_shape=jax.ShapeDtypeStruct((R, C), F32),
                   compiler_params=_params("parallel"))(parts)


def _adamw(w, g, m, v, name):
    L, R, C = w.shape
    tr = _row_tile(R, C, 8, BLOCK_BYTES // 2)

    def body(w_ref, g_ref, m_ref, v_ref, d_ref, m2_ref, v2_ref):
        gg = g_ref[...]
        m2 = ADAM_B1 * m_ref[...] + (1.0 - ADAM_B1) * gg
        v2 = ADAM_B2 * v_ref[...] + (1.0 - ADAM_B2) * (gg * gg)
        m_hat = m2 / (1.0 - ADAM_B1 ** ADAM_STEP)
        v_hat = v2 / (1.0 - ADAM_B2 ** ADAM_STEP)
        d_ref[...] = -ADAM_LR * (m_hat / (jnp.sqrt(v_hat) + ADAM_EPS) + ADAM_WD * w_ref[...])
        m2_ref[...] = m2
        v2_ref[...] = v2

    blk = pl.BlockSpec((1, tr, C), lambda l, i: (l, i, 0))
    shp = jax.ShapeDtypeStruct((L, R, C), F32)
    return _pallas(body, name=name, grid=(L, R // tr), in_specs=[blk] * 4, out_specs=[blk] * 3, out_shape=[shp] * 3,
                   compiler_params=_params("parallel", "parallel"))(w, g, m, v)


WEIGHTS = ("norm_even", "w_in_even", "conv_a_w", "conv_a_b", "ln_a_g", "ln_a_b", "pool_w", "pool_b", "pool_scale",
           "w_out_even", "norm_odd", "w_in_odd", "conv_c_w", "conv_c_b", "w_rg", "b_rg", "w_ig", "b_ig", "lru_lambda",
           "w_out_odd", "final_norm")
BIG = ("w_in_even", "w_out_even", "pool_w", "w_in_odd", "w_out_odd", "w_rg", "w_ig")
SMALL = tuple(n for n in WEIGHTS if n not in BIG)
SMALL_SHARDED = ("conv_a_w", "pool_b", "norm_odd", "conv_c_w", "conv_c_b", "b_rg", "b_ig", "lru_lambda")


def _pack(arrs):
    flat = jnp.concatenate([a.reshape(-1) for a in arrs])
    rows = -(-flat.shape[0] // (64 * 128)) * 64
    return jnp.pad(flat, (0, rows * 128 - flat.shape[0])).reshape(rows, 128)


def _unpack(buf, shapes, lead=()):
    flat = buf.reshape(tuple(lead) + (-1,))
    out, o = [], 0
    for s in shapes:
        n = 1
        for d in s:
            n *= d
        out.append(flat[..., o:o + n].reshape(tuple(lead) + tuple(s)))
        o += n
    return out


def _shard(full, axis, k):
    n = full.shape[axis] // N_CHIPS
    return lax.dynamic_slice_in_dim(full, k * n, n, axis)


def kernel(x, norm_even, w_in_even, conv_a_w, conv_a_b, ln_a_g, ln_a_b, pool_w, pool_b, pool_scale, w_out_even, norm_odd, w_in_odd, conv_c_w, conv_c_b, w_rg, b_rg, w_ig, b_ig, lru_lambda, w_out_odd, final_norm, loss_target, m_norm_even, m_w_in_even, m_conv_a_w, m_conv_a_b, m_ln_a_g, m_ln_a_b, m_pool_w, m_pool_b, m_pool_scale, m_w_out_even, m_norm_odd, m_w_in_odd, m_conv_c_w, m_conv_c_b, m_w_rg, m_b_rg, m_w_ig, m_b_ig, m_lru_lambda, m_w_out_odd, m_final_norm, v_norm_even, v_w_in_even, v_conv_a_w, v_conv_a_b, v_ln_a_g, v_ln_a_b, v_pool_w, v_pool_b, v_pool_scale, v_w_out_even, v_norm_odd, v_w_in_odd, v_conv_c_w, v_conv_c_b, v_w_rg, v_b_rg, v_w_ig, v_b_ig, v_lru_lambda, v_w_out_odd, v_final_norm):
    P = dict(locals())
    xi, yi, ci = _mesh_pos()
    k = 2 * xi + yi
    L = w_in_even.shape[0]
    D = D_MODEL

    pos = jnp.stack([k, ci]).astype(jnp.int32)
    depth = 2 * L
    pool_w3 = pool_w.reshape(L, 4 * 64, POOL_GW)

    def cast_group(layer):
        j = layer // 2
        if layer % 2 == 0:
            return [_cast_shard(w_in_even, j, pos), _cast_shard(w_out_even, j, pos), _cast_shard(pool_w3, j, pos)]
        return [_cast_shard(w_in_odd, j, pos), _cast_shard(w_out_odd, j, pos)]

    first = cast_group(0)
    g_in, g_pool, g_small = _gather_weights([first[0], first[2]], _pack([P[n] for n in SMALL_SHARDED]))
    copies0 = _gather_copies([first[1].shape])
    ssem0, rsem0, late, token0 = _split_start([first[1]], copies0, 3, "gather_start0")

    def late_w_out(y):
        return _forward_cores(_split_wait(ssem0, rsem0, late, copies0, y, "gather_wait0"))[0].reshape(1, -1, D)

    group = [g_in, late_w_out, g_pool]
    full = {}
    for n, a in zip(SMALL_SHARDED, _unpack(g_small, [P[n].shape for n in SMALL_SHARDED], lead=(N_CHIPS,))):
        a = jnp.moveaxis(a, 0, -2)
        full[n] = a.reshape(a.shape[:-2] + (N_CHIPS * a.shape[-1],))

    small_even = dict(norm=norm_even[:, None], conv_w=_pad_rows(full["conv_a_w"], 32),
                      conv_w_rev=_pad_rows(full["conv_a_w"][:, ::-1], 32), conv_b=conv_a_b[:, None], ln_g=ln_a_g[:, None],
                      ln_b=ln_a_b[:, None], pool_b=full["pool_b"].reshape(L, 1, D), pool_scale=pool_scale[:, None])
    small_odd = dict(norm=full["norm_odd"][:, None], conv_w=_pad_rows(full["conv_c_w"], 8),
                     conv_b=full["conv_c_b"][:, None], w_rg=w_rg.astype(BF16), b_rg=full["b_rg"][:, None],
                     w_ig=w_ig.astype(BF16), b_ig=full["b_ig"][:, None], lam=full["lru_lambda"][:, None])

    def small_weights(layer, group):
        if layer % 2 == 0:
            pw = group[2].reshape(N_CHIPS, 4, 64, POOL_GW).transpose(1, 0, 2, 3).reshape(4, POOL_GW, POOL_GW)
            return dict(small_even, sl=layer // 2, pool_w=pw)
        return dict(small_odd, sl=layer // 2)

    no_token = jnp.zeros((8, 128), F32)
    h = x[0]
    saved, big_w, small_w = [], [], []
    for layer in range(depth):
        token = token0 if layer == 0 else no_token
        if layer + 1 < depth:
            nxt = cast_group(layer + 1)
            copies = _gather_copies([a.shape for a in nxt])
            ssem, rsem, nxt, token = _split_start(nxt, copies, 3 * len(nxt), "gather_start%d" % (layer + 1))
        small_w.append(small_weights(layer, group))
        w_out = group[1] if callable(group[1]) else group[1].reshape(1, -1, D)
        h, sv, w_out = _layer_fwd(layer % 2 == 0, h, small_w[layer], group[0], w_out, token)
        big_w.append((group[0], w_out))
        saved.append(sv)
        if layer + 1 < depth:
            group = _forward_cores(_split_wait(ssem, rsem, nxt, copies, h, "gather_wait%d" % (layer + 1)))

    dh, dhb, d_final, loss = _loss_head(h, final_norm[None], loss_target[0])
    loss = lax.psum(loss[0, 0], ("x", "y", "c"))
    everywhere = [False, False, False, False, False, True, True]
    final = [None] * len(everywhere)
    small_of = [None] * depth

    def finish(pending, after):
        ssem, rsem, arrs, copies, slots, pj, pl_ = pending
        arrs = _split_wait(ssem, rsem, arrs, copies, after, "chips_wait%d" % pl_)
        for a, r, s in zip(arrs[:len(slots)], arrs[len(slots):], slots):
            final[s] = _sum_chips(a, r, pos, everywhere[s], pj, L, final[s])

    pending = None
    token = no_token
    for layer in reversed(range(depth)):
        j = layer // 2
        even_layer = layer % 2 == 0
        dp, dw_in, dw_out, sm = _layer_bwd_weights(even_layer, saved[layer], small_w[layer], big_w[layer][1], dhb, token)
        if even_layer:
            dpw = sm["pool_w"].reshape(4, N_CHIPS, 64, POOL_GW).transpose(1, 0, 2, 3)
            parts = [dw_in, dw_out.reshape(1, N_CHIPS, -1, D), dpw.reshape(1, N_CHIPS, 4 * 64, POOL_GW).astype(BF16)]
            slots = [0, 1, 2]
        else:
            parts = [dw_in, dw_out.reshape(1, N_CHIPS, -1, D),
                     sm["w_rg"].reshape(1, N_CHIPS, -1, LRU_HD).astype(BF16),
                     sm["w_ig"].reshape(1, N_CHIPS, -1, LRU_HD).astype(BF16)]
            slots = [3, 4, 5, 6]
        n = len(parts)
        if layer > 0:
            hcopies = _halves_copies([a.shape for a in parts])
            hland = [lax.empty((1, N_CHIPS, a.shape[2] // 2, a.shape[3]), a.dtype) for a in parts]
            hs, hr, harrs, htoken = _split_start(parts + hland, hcopies, n, "halves_start%d" % layer)
            dh, dhb, sm["norm"] = _layer_bwd_input(even_layer, saved[layer], small_w[layer], big_w[layer][0], dp, dh,
                                                   htoken)
            harrs = _split_wait(hs, hr, harrs, hcopies, dh, "halves_wait%d" % layer)
            parts, recv = harrs[:n], harrs[n:]
        else:
            recv = _exchange_halves(parts)
        pair = [_add_cores(a, r, pos) for a, r in zip(parts, recv)]
        copies = _chips_copies(n)
        land = [lax.empty(a.shape, a.dtype) for a in pair]
        ssem, rsem, arrs, token = _split_start(pair + land, copies, 3 * n, "chips_start%d" % layer)
        if layer == 0:
            dh, dhb, sm["norm"] = _layer_bwd_input(even_layer, saved[layer], small_w[layer], big_w[layer][0], dp, dh, token)
        small_of[layer] = sm
        if pending is not None:
            finish(pending, dh)
        pending = (ssem, rsem, arrs, copies, slots, j, layer)
    grad_x = dh
    small_g = []
    for jj in range(L):
        ge, go = small_of[2 * jj], small_of[2 * jj + 1]
        small_g += [ge["conv_w"].reshape(32, 8, D).sum(axis=1)[:CONV_K], ge["vec"][0:5], ge["norm"], go["vec"], go["norm"]]
    small_g.append(d_final)
    small_shapes = [a.shape for a in small_g]
    packed_small = _pack(small_g)
    finish(pending, packed_small)
    *gw, recv_small = _exchange_final(final, everywhere, packed_small)
    sg = _unpack(_sum_devices(recv_small), small_shapes)

    grads = dict(w_in_even=gw[0], w_out_even=gw[1], pool_w=gw[2].reshape(pool_w.shape), w_in_odd=gw[3], w_out_odd=gw[4],
                 w_rg=gw[5].reshape(w_rg.shape), w_ig=gw[6].reshape(w_ig.shape), final_norm=sg[-1][0])
    ev = [sg[5 * j + 1] for j in range(L)]
    ov = [sg[5 * j + 3] for j in range(L)]
    grads["conv_a_w"] = _shard(jnp.stack([sg[5 * j] for j in range(L)]), 2, k)
    grads["norm_even"] = jnp.stack([sg[5 * j + 2][0] for j in range(L)])
    grads["norm_odd"] = _shard(jnp.stack([sg[5 * j + 4][0] for j in range(L)]), 1, k)
    for r, n in enumerate(("conv_a_b", "ln_a_g", "ln_a_b", "pool_scale")):
        grads[n] = jnp.stack([e[r] for e in ev])
    grads["pool_b"] = _shard(jnp.stack([e[4].reshape(4, POOL_GW) for e in ev]), 2, k)
    grads["conv_c_w"] = _shard(jnp.stack([o[0:4] for o in ov]), 2, k)
    for r, n in zip((4, 5, 6, 7), ("conv_c_b", "b_rg", "b_ig", "lru_lambda")):
        grads[n] = _shard(jnp.stack([o[r] for o in ov]), 1, k)

    delta, new_m, new_v = {}, {}, {}
    for n in BIG:
        s3 = (L, -1, P[n].shape[-1])
        d, m2, v2 = _adamw(P[n].reshape(s3), grads[n].reshape(s3), P["m_" + n].reshape(s3), P["v_" + n].reshape(s3), "adamw")
        delta[n], new_m[n], new_v[n] = d.reshape(P[n].shape), m2.reshape(P[n].shape), v2.reshape(P[n].shape)
    shapes = [P[n].shape for n in SMALL]
    packed = [_pack([src[n] for n in SMALL])[None] for src in
              (P, grads, {n: P["m_" + n] for n in SMALL}, {n: P["v_" + n] for n in SMALL})]
    for res, out in zip(_adamw(*packed, "adamw_small"), (delta, new_m, new_v)):
        for n, a in zip(SMALL, _unpack(res[0], shapes)):
            out[n] = a

    return (loss, grad_x[None], *[grads[n] for n in WEIGHTS], *[delta[n] for n in WEIGHTS],
            *[new_m[n] for n in WEIGHTS], *[new_v[n] for n in WEIGHTS])
```

```python
import jax
import jax.numpy as jnp
from jax import lax
from jax.experimental import pallas as pl
from jax.experimental.pallas import tpu as pltpu

F32 = jnp.float32
BF16 = jnp.bfloat16
MESH = pl.DeviceIdType.MESH

D_MODEL = 1024
N_CHIPS = 4
N_DEV = 8
EPS_RMS = 1e-6
EPS_LN = 1e-5
CONV_K = 31
POOL_WINDOWS = (2, 4, 8, 16)
POOL_GW = 256
LRU_HEADS = 12
LRU_HD = 128
W_LRU = LRU_HEADS * LRU_HD
LRU_CONV_K = 4
LRU_C = 8.0
ADAM_LR = 0.001
ADAM_B1 = 0.9
ADAM_B2 = 0.999
ADAM_EPS = 1e-08
ADAM_WD = 0.01
ADAM_STEP = 10

VMEM_LIMIT_BYTES = 56 * 1024 * 1024
ROW_TILE = 512
MIX_TILE = 256
EVEN_HALO = 32
ODD_HALO = 8


def _pallas(body, **kw):
    return pl.pallas_call(body, **kw)


def _params(*sem):
    return pltpu.CompilerParams(dimension_semantics=sem if sem else None, vmem_limit_bytes=VMEM_LIMIT_BYTES)


def _sigmoid(x):
    return 0.5 * jnp.tanh(0.5 * x) + 0.5


def _dsilu(x, s):
    return s * (1.0 + x * (1.0 - s))


def _nt(a, b):
    return lax.dot_general(a, b, (((1,), (1,)), ((), ())), preferred_element_type=F32)


def _tn(a, b):
    return lax.dot_general(a, b, (((0,), (0,)), ((), ())), preferred_element_type=F32)


def _in_proj(h, g, glayer, wg, layer, after, name):
    T, D = h.shape
    _, nblk, _, nb = wg.shape

    nrow = T // ROW_TILE

    def body(h_ref, g_ref, w_ref, after_ref, p_ref, n_ref, n_all):
        j, i = pl.program_id(0), pl.program_id(1)

        @pl.when(j == 0)
        def _():
            x = h_ref[...]
            r = lax.rsqrt(jnp.mean(x * x, axis=-1, keepdims=True) + EPS_RMS)
            nn = (x * r * g_ref[...]).astype(BF16)
            n_ref[...] = nn
            n_all[i] = nn

        p_ref[...] = jnp.dot(n_all[i], w_ref[0], preferred_element_type=F32)

    def rows_once(j, i):
        return (jnp.where(j == 0, i, nrow - 1), 0)

    return _pallas(
        body, name=name, grid=(nblk, nrow),
        in_specs=[pl.BlockSpec((ROW_TILE, D), rows_once), pl.BlockSpec((None, 1, D), lambda j, i: (glayer, 0, 0)),
                  pl.BlockSpec((None, 1, D, nb), lambda j, i: (layer, j, 0, 0)),
                  pl.BlockSpec((8, 128), lambda j, i: (0, 0))],
        out_specs=[pl.BlockSpec((ROW_TILE, nb), lambda j, i: (i, j)), pl.BlockSpec((ROW_TILE, D), rows_once)],
        out_shape=[jax.ShapeDtypeStruct((T, nblk * nb), F32), jax.ShapeDtypeStruct((T, D), BF16)],
        scratch_shapes=[pltpu.VMEM((nrow, ROW_TILE, D), BF16)],
        compiler_params=_params("arbitrary", "arbitrary"))(h, g, wg, after)


def _dn_proj(dp, wg, layer, h, g, glayer, dres, after, name):
    T, D = h.shape
    _, nblk, _, nb = wg.shape

    nrow = T // ROW_TILE

    def body(dp_ref, w_ref, h_ref, g_ref, dres_ref, after_ref, dh_ref, dhb_ref, dg_ref, acc_ref):
        j, i = pl.program_id(0), pl.program_id(1)
        part = _nt(dp_ref[...], w_ref[0])

        @pl.when(j == 0)
        def _():
            acc_ref[i] = part

        @pl.when(j > 0)
        def _():
            acc_ref[i] += part

        @pl.when(j == nblk - 1)
        def _():
            x = h_ref[...]
            r = lax.rsqrt(jnp.mean(x * x, axis=-1, keepdims=True) + EPS_RMS)
            dn = acc_ref[i]
            q = dn * g_ref[...]
            dh = dres_ref[...] + r * q - x * ((r * r * r) * jnp.mean(q * x, axis=-1, keepdims=True))
            dh_ref[...] = dh
            dhb_ref[...] = dh.astype(BF16)
            dgp = jnp.sum(dn * (x * r), axis=0, keepdims=True)

            @pl.when(i == 0)
            def _():
                dg_ref[...] = dgp

            @pl.when(i > 0)
            def _():
                dg_ref[...] += dgp

    def rows_last(j, i):
        return (jnp.where(j == nblk - 1, i, 0), 0)

    return _pallas(
        body, name=name, grid=(nblk, nrow),
        in_specs=[pl.BlockSpec((ROW_TILE, nb), lambda j, i: (i, j)),
                  pl.BlockSpec((None, 1, D, nb), lambda j, i: (layer, j, 0, 0)),
                  pl.BlockSpec((ROW_TILE, D), rows_last), pl.BlockSpec((None, 1, D), lambda j, i: (glayer, 0, 0)),
                  pl.BlockSpec((ROW_TILE, D), rows_last), pl.BlockSpec((8, 128), lambda j, i: (0, 0))],
        out_specs=[pl.BlockSpec((ROW_TILE, D), rows_last), pl.BlockSpec((ROW_TILE, D), rows_last),
                   pl.BlockSpec((1, D), lambda j, i: (0, 0))],
        out_shape=[jax.ShapeDtypeStruct((T, D), F32), jax.ShapeDtypeStruct((T, D), BF16),
                   jax.ShapeDtypeStruct((1, D), F32)],
        scratch_shapes=[pltpu.VMEM((nrow, ROW_TILE, D), F32)],
        compiler_params=_params("arbitrary", "arbitrary"))(dp, wg, h, g, dres, after)


def _dw_in(n, dp, nblk, layer, nlayers, prev, name):
    T, D = n.shape
    nb = dp.shape[1] // nblk
    ta = D

    def body(n_ref, dp_ref, *rest):
        rest[-1][0] = _tn(n_ref[...], dp_ref[...]).astype(BF16)

    in_specs = [pl.BlockSpec((T, ta), lambda j, i: (0, i)), pl.BlockSpec((T, nb), lambda j, i: (0, j))]
    args = (n, dp) if prev is None else (n, dp, prev)
    return _pallas(
        body, name=name, grid=(nblk, D // ta), in_specs=in_specs + ([] if prev is None else [ANY]),
        out_specs=pl.BlockSpec((None, 1, ta, nb), lambda j, i: (layer, j, i, 0)),
        out_shape=jax.ShapeDtypeStruct((nlayers, nblk, D, nb), BF16),
        input_output_aliases={} if prev is None else {2: 0},
        compiler_params=_params("parallel", "parallel"))(*args)


def _dw_out(y, dout, layer, nlayers, prev, name):
    T, K = y.shape
    D = dout.shape[1]
    tk = 512

    def body(y_ref, d_ref, *rest):
        rest[-1][...] = _tn(y_ref[...], d_ref[...]).astype(BF16)

    in_specs = [pl.BlockSpec((T, tk), lambda i: (0, i)), pl.BlockSpec((T, D), lambda i: (0, 0))]
    args = (y, dout) if prev is None else (y, dout, prev)
    return _pallas(
        body, name=name, grid=(K // tk,), in_specs=in_specs + ([] if prev is None else [ANY]),
        out_specs=pl.BlockSpec((None, tk, D), lambda i: (layer, i, 0)),
        out_shape=jax.ShapeDtypeStruct((nlayers, K, D), BF16),
        input_output_aliases={} if prev is None else {2: 0},
        compiler_params=_params("parallel"))(*args)


def _loss_head(h, g, tgt):
    T, D = h.shape
    tm = MIX_TILE

    def body(h_ref, g_ref, t_ref, dh_ref, dhb_ref, dg_ref, loss_ref):
        i = pl.program_id(0)
        x = h_ref[...]
        gg = g_ref[...]
        r = lax.rsqrt(jnp.mean(x * x, axis=-1, keepdims=True) + EPS_RMS)
        xr = x * r
        e = xr * gg - t_ref[...]
        lp = 0.5 * jnp.sum(jnp.mean(e * e, axis=-1, keepdims=True), axis=0, keepdims=True)
        dn = e * (1.0 / D)
        q = dn * gg
        dh = r * q - x * ((r * r * r) * jnp.mean(q * x, axis=-1, keepdims=True))
        dh_ref[...] = dh
        dhb_ref[...] = dh.astype(BF16)
        dgp = jnp.sum(dn * xr, axis=0, keepdims=True)

        @pl.when(i == 0)
        def _():
            dg_ref[...] = dgp
            loss_ref[...] = lp

        @pl.when(i > 0)
        def _():
            dg_ref[...] += dgp
            loss_ref[...] += lp

    return _pallas(
        body, name="loss_head", grid=(T // tm,),
        in_specs=[pl.BlockSpec((tm, D), lambda i: (i, 0)), pl.BlockSpec((1, D), lambda i: (0, 0)),
                  pl.BlockSpec((tm, D), lambda i: (i, 0))],
        out_specs=[pl.BlockSpec((tm, D), lambda i: (i, 0)), pl.BlockSpec((tm, D), lambda i: (i, 0)),
                   pl.BlockSpec((1, D), lambda i: (0, 0)), pl.BlockSpec((1, 1), lambda i: (0, 0))],
        out_shape=[jax.ShapeDtypeStruct((T, D), F32), jax.ShapeDtypeStruct((T, D), BF16),
                   jax.ShapeDtypeStruct((1, D), F32), jax.ShapeDtypeStruct((1, 1), F32)],
        compiler_params=_params("arbitrary"))(h, g, tgt)


def _shift_up(x, j):
    return x if j == 0 else pltpu.roll(x, x.shape[0] - j, 0)


def _shift_down(x, j):
    return x if j == 0 else pltpu.roll(x, j, 0)


def _fill_shifted(dst_ref, src_ref):
    rows = dst_ref.shape[1]
    for s in range(8):
        dst_ref[s] = src_ref[pl.ds(s, rows), :]


def _fill_taps(wb_ref, w_ref):
    for k in range(w_ref.shape[0]):
        wb_ref[k] = jnp.broadcast_to(w_ref[k:k + 1, :], wb_ref.shape[1:])


def _tap_sum(sh_ref, wb_ref, r0, nrows, offsets):
    accs = [None] * (nrows // 8)
    for k, o in enumerate(offsets):
        wk = wb_ref[k]
        for u in range(nrows // 8):
            term = wk * sh_ref[o % 8, pl.ds(r0 + (o // 8) * 8 + 8 * u, 8), :]
            accs[u] = term if accs[u] is None else accs[u] + term
    return jnp.concatenate(accs, axis=0)


def _pool_sums(vx, up):
    sh = _shift_up if up else _shift_down
    outs = []
    for gi, w in enumerate(POOL_WINDOWS):
        s = vx[:, gi * POOL_GW:(gi + 1) * POOL_GW]
        j = 1
        while j < w:
            s = s + sh(s, j)
            j *= 2
        outs.append(s)
    return outs


def _inv_count(row0, nrows):
    pos = (row0 + 1 + lax.broadcasted_iota(jnp.int32, (nrows, 1), 0)).astype(F32)
    return [1.0 / jnp.minimum(pos, float(w)) for w in POOL_WINDOWS]


def _even_mixer_fwd(p, h, w_out, sl, cw, cb, lg, lb, pw, pb, sc, name):
    T = p.shape[0]
    C = D_MODEL
    tT, HL = MIX_TILE, EVEN_HALO
    hb = tT // HL
    chunk = 32

    def body(pm_ref, ph_ref, cw_ref, cb_ref, lg_ref, lb_ref, pw_ref, pb_ref, sc_ref, h_ref, wo_ref, y_ref, u1_ref,
             hn_ref, u0x_ref, sh_ref, wb_ref):
        i = pl.program_id(0)
        keep = (i > 0).astype(F32)

        @pl.when(i == 0)
        def _():
            _fill_taps(wb_ref, cw_ref)

        u0x_ref[0:HL] = ph_ref[:, 0:C] * _sigmoid(ph_ref[:, C:2 * C]) * keep
        u0x_ref[HL:HL + tT] = pm_ref[:, 0:C] * _sigmoid(pm_ref[:, C:2 * C])
        u0x_ref[HL + tT:HL + tT + 8] = jnp.zeros((8, C), F32)
        _fill_shifted(sh_ref, u0x_ref)
        offs = [HL - (CONV_K - 1) + k for k in range(CONV_K)]

        def conv_chunk(c, carry):
            r0 = pl.multiple_of(c * chunk, chunk)
            u1_ref[pl.ds(r0, chunk), :] = _tap_sum(sh_ref, wb_ref, r0, chunk, offs) + cb_ref[...]
            return carry

        lax.fori_loop(0, tT // chunk, conv_chunk, 0)
        u1 = u1_ref[...]
        mu = jnp.mean(u1, axis=-1, keepdims=True)
        xc = u1 - mu
        rs = lax.rsqrt(jnp.mean(xc * xc, axis=-1, keepdims=True) + EPS_LN)
        u2 = xc * rs * lg_ref[...] + lb_ref[...]
        u3 = u2 * _sigmoid(u2)
        ag = pm_ref[:, 2 * C:3 * C]
        y_ref[:, 0:C] = (u3 * (ag * _sigmoid(ag))).astype(BF16)
        vx = jnp.concatenate([ph_ref[:, 3 * C:4 * C] * keep, pm_ref[:, 3 * C:4 * C]], axis=0)
        sums = _pool_sums(vx, up=False)
        inv = _inv_count(i * tT, tT)
        for gi in range(len(POOL_WINDOWS)):
            cols = slice(gi * POOL_GW, (gi + 1) * POOL_GW)
            d0 = sums[gi][HL:] * inv[gi] - vx[HL:, cols]
            d1 = jnp.dot(d0.astype(BF16), pw_ref[gi], preferred_element_type=F32) + pb_ref[:, cols]
            bg = pm_ref[:, 4 * C + gi * POOL_GW:4 * C + (gi + 1) * POOL_GW]
            y_ref[:, C + gi * POOL_GW:C + (gi + 1) * POOL_GW] = (d1 * sc_ref[:, cols] * (bg * _sigmoid(bg))).astype(BF16)
        hn_ref[...] = h_ref[...] + jnp.dot(y_ref[...], wo_ref[...], preferred_element_type=F32)

    vec = pl.BlockSpec((None, 1, C), lambda i: (sl, 0, 0))
    rows = pl.BlockSpec((tT, C), lambda i: (i, 0))
    return _pallas(
        body, name=name, grid=(T // tT,),
        in_specs=[pl.BlockSpec((tT, 5 * C), lambda i: (i, 0)),
                  pl.BlockSpec((HL, 5 * C), lambda i: (jnp.maximum(i * hb - 1, 0), 0)),
                  pl.BlockSpec((None, 32, C), lambda i: (sl, 0, 0)), vec, vec, vec,
                  pl.BlockSpec((4, POOL_GW, POOL_GW), lambda i: (0, 0, 0)), vec, vec,
                  rows, pl.BlockSpec((None, 2 * C, C), lambda i: (0, 0, 0))],
        out_specs=[pl.BlockSpec((tT, 2 * C), lambda i: (i, 0)), rows, rows],
        out_shape=[jax.ShapeDtypeStruct((T, 2 * C), BF16), jax.ShapeDtypeStruct((T, C), F32),
                   jax.ShapeDtypeStruct((T, C), F32)],
        scratch_shapes=[pltpu.VMEM((HL + tT + 8, C), F32), pltpu.VMEM((8, HL + tT, C), F32),
                        pltpu.VMEM((32, 8, C), F32)],
        compiler_params=_params("arbitrary"))(p, p, cw, cb, lg, lb, pw, pb, sc, h, w_out)


def _even_mixer_bwd(p, u1, dout, w_out, after, sl, cwr, lg, lb, pw, pb, sc, name):
    T = p.shape[0]
    C = D_MODEL
    tT, HL = MIX_TILE, EVEN_HALO
    hb = tT // HL
    nT = T // tT
    R1 = tT + HL
    chunk = 32

    def body(pm_ref, pp_ref, pn_ref, u1m_ref, u1n_ref, dom_ref, don_ref, wo_ref, after_ref, cwr_ref, lg_ref, lb_ref,
             pw_ref, pb_ref, sc_ref, dp_ref, dcw_ref, dvec_ref, dpw_ref, x_ref, sh_ref, du0_ref, wb_ref):
        i = pl.program_id(0)
        dy = _nt(jnp.concatenate([dom_ref[...], don_ref[...]], axis=0), wo_ref[...])

        @pl.when(i == 0)
        def _():
            _fill_taps(wb_ref, cwr_ref)

        keep_prev = (i > 0).astype(F32)
        keep_next = (i < nT - 1).astype(F32)
        row = lax.broadcasted_iota(jnp.int32, (R1, 1), 0)
        live = jnp.where(row < tT, 1.0, keep_next)

        def cat(m, n):
            return jnp.concatenate([m, n], axis=0)

        u1 = cat(u1m_ref[...], u1n_ref[...])
        mu = jnp.mean(u1, axis=-1, keepdims=True)
        xc = u1 - mu
        rs = lax.rsqrt(jnp.mean(xc * xc, axis=-1, keepdims=True) + EPS_LN)
        xh = xc * rs
        u2 = xh * lg_ref[...] + lb_ref[...]
        s2 = _sigmoid(u2)
        u3 = u2 * s2
        ag = cat(pm_ref[:, 2 * C:3 * C], pn_ref[:, 2 * C:3 * C])
        sa = _sigmoid(ag)
        dya = dy[:, 0:C]
        dp_ref[:, 2 * C:3 * C] = (dya * u3 * _dsilu(ag, sa))[0:tT].astype(BF16)
        du2 = dya * (ag * sa) * _dsilu(u2, s2)
        dlg = jnp.sum((du2 * xh)[0:tT], axis=0, keepdims=True)
        dlb = jnp.sum(du2[0:tT], axis=0, keepdims=True)
        dxh = du2 * lg_ref[...]
        du1 = rs * (dxh - jnp.mean(dxh, axis=-1, keepdims=True) - xh * jnp.mean(dxh * xh, axis=-1, keepdims=True))
        du1 = du1 * live
        dcb = jnp.sum(du1[0:tT], axis=0, keepdims=True)
        x_ref[0:R1] = du1
        x_ref[R1:R1 + 8] = jnp.zeros((8, C), F32)
        _fill_shifted(sh_ref, x_ref)

        def du0_chunk(c, carry):
            r0 = pl.multiple_of(c * chunk, chunk)
            du0_ref[pl.ds(r0, chunk), :] = _tap_sum(sh_ref, wb_ref, r0, chunk, list(range(CONV_K)))
            return carry

        lax.fori_loop(0, tT // chunk, du0_chunk, 0)
        av, agl = pm_ref[:, 0:C], pm_ref[:, C:2 * C]
        sg = _sigmoid(agl)
        du0 = du0_ref[...]
        dp_ref[:, 0:C] = (du0 * sg).astype(BF16)
        dp_ref[:, C:2 * C] = (du0 * av * sg * (1.0 - sg)).astype(BF16)
        du0_ref[...] = du1[0:tT]
        x_ref[0:HL] = pp_ref[:, 0:C] * _sigmoid(pp_ref[:, C:2 * C]) * keep_prev
        x_ref[HL:HL + tT] = av * sg
        x_ref[HL + tT:HL + tT + 8] = jnp.zeros((8, C), F32)
        _fill_shifted(sh_ref, x_ref)

        @pl.when(i == 0)
        def _():
            dcw_ref[...] = jnp.zeros_like(dcw_ref)

        for k0 in range(0, CONV_K, 2):
            taps = [k for k in (k0, k0 + 1) if k < CONV_K]
            offs = [HL - (CONV_K - 1) + k for k in taps]

            def dw_chunk(c, accs, offs=offs):
                r0 = pl.multiple_of(c * 64, 64)
                accs = list(accs)
                for u in range(0, 64, 8):
                    d = du0_ref[pl.ds(r0 + u, 8), :]
                    for t, o in enumerate(offs):
                        accs[t] = accs[t] + d * sh_ref[o % 8, pl.ds(r0 + u + (o // 8) * 8, 8), :]
                return tuple(accs)

            sums = lax.fori_loop(0, tT // 64, dw_chunk, tuple(jnp.zeros((8, C), F32) for _ in taps))
            for k, acc in zip(taps, sums):
                dcw_ref[8 * k:8 * k + 8, :] += acc

        bg = cat(pm_ref[:, 4 * C:5 * C], pn_ref[:, 4 * C:5 * C])
        sb = _sigmoid(bg)
        dyb = dy[:, C:2 * C]
        dyb0 = dyb * (bg * sb)
        dd1 = dyb0 * sc_ref[...]
        dpb = jnp.sum(dd1[0:tT], axis=0, keepdims=True)
        inv1 = _inv_count(i * tT, R1)
        z_parts, dd0_parts = [], []
        for gi in range(len(POOL_WINDOWS)):
            cols = slice(gi * POOL_GW, (gi + 1) * POOL_GW)
            dd0 = _nt(dd1[:, cols].astype(BF16), pw_ref[gi])
            dd0_parts.append(dd0)
            z_parts.append(dd0 * inv1[gi] * live)
        fsum = _pool_sums(jnp.concatenate(z_parts, axis=1), up=True)
        vx = cat(pp_ref[:, 3 * C:4 * C] * keep_prev, pm_ref[:, 3 * C:4 * C])
        sums = _pool_sums(vx, up=False)
        inv0 = _inv_count(i * tT, tT)
        dsc_parts = []
        for gi in range(len(POOL_WINDOWS)):
            cols = slice(gi * POOL_GW, (gi + 1) * POOL_GW)
            dp_ref[:, 3 * C + gi * POOL_GW:3 * C + (gi + 1) * POOL_GW] = (fsum[gi][0:tT] - dd0_parts[gi][0:tT]).astype(BF16)
            d0 = (sums[gi][HL:] * inv0[gi] - vx[HL:, cols]).astype(BF16)
            d1 = jnp.dot(d0, pw_ref[gi], preferred_element_type=F32) + pb_ref[:, cols]
            bgm, sbm = bg[0:tT, cols], sb[0:tT, cols]
            dp_ref[:, 4 * C + gi * POOL_GW:4 * C + (gi + 1) * POOL_GW] = (
                dyb[0:tT, cols] * d1 * sc_ref[:, cols] * _dsilu(bgm, sbm)).astype(BF16)
            dsc_parts.append(jnp.sum(dyb0[0:tT, cols] * d1, axis=0, keepdims=True))
            dpw_g = _tn(d0, dd1[0:tT, cols].astype(BF16))

            @pl.when(i == 0)
            def _(gi=gi, dpw_g=dpw_g):
                dpw_ref[gi] = dpw_g

            @pl.when(i > 0)
            def _(gi=gi, dpw_g=dpw_g):
                dpw_ref[gi] += dpw_g

        dsc = jnp.concatenate(dsc_parts, axis=1)
        vecs = jnp.concatenate([dcb, dlg, dlb, dsc, dpb, jnp.zeros((3, C), F32)], axis=0)

        @pl.when(i == 0)
        def _():
            dvec_ref[...] = vecs

        @pl.when(i > 0)
        def _():
            dvec_ref[...] += vecs

    vec = pl.BlockSpec((None, 1, C), lambda i: (sl, 0, 0))
    taps = pl.BlockSpec((None, 32, C), lambda i: (sl, 0, 0))

    def prev_blk(i):
        return (jnp.maximum(i * hb - 1, 0), 0)

    def next_blk(i):
        return (jnp.minimum((i + 1) * hb, T // HL - 1), 0)

    return _pallas(
        body, name=name, grid=(nT,),
        in_specs=[pl.BlockSpec((tT, 5 * C), lambda i: (i, 0)), pl.BlockSpec((HL, 5 * C), prev_blk),
                  pl.BlockSpec((HL, 5 * C), next_blk),
                  pl.BlockSpec((tT, C), lambda i: (i, 0)), pl.BlockSpec((HL, C), next_blk),
                  pl.BlockSpec((tT, C), lambda i: (i, 0)), pl.BlockSpec((HL, C), next_blk),
                  pl.BlockSpec((None, 2 * C, C), lambda i: (0, 0, 0)), pl.BlockSpec((8, 128), lambda i: (0, 0)),
                  taps, vec, vec, pl.BlockSpec((4, POOL_GW, POOL_GW), lambda i: (0, 0, 0)), vec, vec],
        out_specs=[pl.BlockSpec((tT, 5 * C), lambda i: (i, 0)), pl.BlockSpec((32 * 8, C), lambda i: (0, 0)),
                   pl.BlockSpec((8, C), lambda i: (0, 0)), pl.BlockSpec((4, POOL_GW, POOL_GW), lambda i: (0, 0, 0))],
        out_shape=[jax.ShapeDtypeStruct((T, 5 * C), BF16), jax.ShapeDtypeStruct((32 * 8, C), F32),
                   jax.ShapeDtypeStruct((8, C), F32), jax.ShapeDtypeStruct((4, POOL_GW, POOL_GW), F32)],
        scratch_shapes=[pltpu.VMEM((R1 + 8, C), F32), pltpu.VMEM((8, R1, C), F32), pltpu.VMEM((tT, C), F32),
                        pltpu.VMEM((32, 8, C), F32)],
        compiler_params=_params("arbitrary"))(p, p, p, u1, u1, dout, dout, w_out, after, cwr, lg, lb, pw, pb, sc)


def _softplus(z):
    u = jnp.exp(-jnp.abs(z))
    w = 1.0 + u
    l1p = jnp.where(w == 1.0, u, u * jnp.log(w) / jnp.where(w == 1.0, 1.0, w - 1.0))
    return jnp.maximum(z, 0.0) + l1p


def _lru_gates(xrx, cw_ref, cb_ref, wr_ref, br_ref, wi_ref, bi_ref, lam_ref):
    HL = ODD_HALO
    xc = cb_ref[...] + cw_ref[LRU_CONV_K - 1:LRU_CONV_K, :] * xrx[HL:]
    for k in range(LRU_CONV_K - 1):
        xc = xc + cw_ref[k:k + 1, :] * _shift_down(xrx, LRU_CONV_K - 1 - k)[HL:]
    xcb = xc.astype(BF16)
    rp, ip = [], []
    for hd in range(LRU_HEADS):
        cols = slice(hd * LRU_HD, (hd + 1) * LRU_HD)
        rp.append(jnp.dot(xcb[:, cols], wr_ref[hd], preferred_element_type=F32))
        ip.append(jnp.dot(xcb[:, cols], wi_ref[hd], preferred_element_type=F32))
    r = _sigmoid(jnp.concatenate(rp, axis=1) + br_ref[...])
    ig = _sigmoid(jnp.concatenate(ip, axis=1) + bi_ref[...])
    sp = _softplus(-lam_ref[...])
    log_a = (-LRU_C) * r * sp
    a = jnp.exp(log_a)
    m2 = jnp.maximum(-jnp.tanh(log_a) * (a * a + 1.0), 1e-30)
    inv_mult = lax.rsqrt(m2)
    return xc, xcb, r, ig, sp, a, m2 * inv_mult, inv_mult


def _group_scan(a, b, reverse):
    n, w = a.shape
    a, b = a.reshape(n // 8, 8, w), b.reshape(n // 8, 8, w)
    pos = lax.broadcasted_iota(jnp.int32, (1, 8, 1), 1)
    s = 1
    while s < 8:
        ok = (pos < 8 - s) if reverse else (pos >= s)
        shift = (8 - s) if reverse else s
        a_sh = jnp.where(ok, pltpu.roll(a, shift, 1), 1.0)
        b_sh = jnp.where(ok, pltpu.roll(b, shift, 1), 0.0)
        b = a * b_sh + b
        a = a * a_sh
        s *= 2
    return a.reshape(n, w), b.reshape(n, w)


def _apply_carries(a_ref, b_ref, out_ref, c0, reverse):
    ng = a_ref.shape[0] // 8

    def step(t, c):
        r0 = pl.multiple_of(((ng - 1 - t) if reverse else t) * 8, 8)
        x = a_ref[pl.ds(r0, 8), :] * c + b_ref[pl.ds(r0, 8), :]
        out_ref[pl.ds(r0, 8), :] = x
        return x[0:1, :] if reverse else x[7:8, :]

    return lax.fori_loop(0, ng, step, c0)


def _odd_mixer_fwd(p, h, w_out, sl, cw, cb, wr, br, wi, bi, lam, name):
    T = p.shape[0]
    W = W_LRU
    D = D_MODEL
    tT, HL = MIX_TILE, ODD_HALO
    hb = tT // HL

    def body(pm_ref, ph_ref, cw_ref, cb_ref, wr_ref, br_ref, wi_ref, bi_ref, lam_ref, h_ref, wo_ref, y_ref, hs_ref,
             hn_ref, carry_ref, sa_ref, sb_ref):
        i = pl.program_id(0)
        keep = (i > 0).astype(F32)

        @pl.when(i == 0)
        def _():
            carry_ref[...] = jnp.zeros_like(carry_ref)

        xrx = jnp.concatenate([ph_ref[:, 0:W] * keep, pm_ref[:, 0:W]], axis=0)
        xc, _, _, ig, _, a, mult, _ = _lru_gates(xrx, cw_ref, cb_ref, wr_ref, br_ref, wi_ref, bi_ref, lam_ref)
        sa_ref[...], sb_ref[...] = _group_scan(a, mult * (ig * xc), reverse=False)
        last = _apply_carries(sa_ref, sb_ref, hs_ref, carry_ref[0:1, :], reverse=False)
        carry_ref[...] = jnp.broadcast_to(last, (8, W))
        hs = hs_ref[...]
        gt = pm_ref[:, W:2 * W]
        y_ref[...] = (hs * (gt * _sigmoid(gt))).astype(BF16)
        hn_ref[...] = h_ref[...] + jnp.dot(y_ref[...], wo_ref[...], preferred_element_type=F32)

    vec = pl.BlockSpec((None, 1, W), lambda i: (sl, 0, 0))
    heads = pl.BlockSpec((None, LRU_HEADS, LRU_HD, LRU_HD), lambda i: (sl, 0, 0, 0))
    rows = pl.BlockSpec((tT, D), lambda i: (i, 0))
    wide = pl.BlockSpec((tT, W), lambda i: (i, 0))
    return _pallas(
        body, name=name, grid=(T // tT,),
        in_specs=[pl.BlockSpec((tT, 2 * W), lambda i: (i, 0)),
                  pl.BlockSpec((HL, 2 * W), lambda i: (jnp.maximum(i * hb - 1, 0), 0)),
                  pl.BlockSpec((None, 8, W), lambda i: (sl, 0, 0)), vec, heads, vec, heads, vec, vec,
                  rows, pl.BlockSpec((None, W, D), lambda i: (0, 0, 0))],
        out_specs=[wide, wide, rows],
        out_shape=[jax.ShapeDtypeStruct((T, W), BF16), jax.ShapeDtypeStruct((T, W), F32),
                   jax.ShapeDtypeStruct((T, D), F32)],
        scratch_shapes=[pltpu.VMEM((8, W), F32), pltpu.VMEM((tT, W), F32), pltpu.VMEM((tT, W), F32)],
        compiler_params=_params("arbitrary"))(p, p, cw, cb, wr, br, wi, bi, lam, h, w_out)


def _odd_mixer_bwd(p, hs, dout, w_out, after, sl, cw, cb, wr, br, wi, bi, lam, name):
    T = p.shape[0]
    W = W_LRU
    D = dout.shape[1]
    tT, HL = MIX_TILE, ODD_HALO
    hb = tT // HL
    nT = T // tT

    def body(pm_ref, ph_ref, hsm_ref, hsh_ref, do_ref, wo_ref, after_ref, cw_ref, cb_ref, wr_ref, br_ref, wi_ref,
             bi_ref, lam_ref, dp_ref, dwr_ref, dwi_ref, dvec_ref, gcarry_ref, xcarry_ref, sa_ref, sb_ref, g_ref):
        i = pl.program_id(0)
        keep = (i < nT - 1).astype(F32)

        @pl.when(i == 0)
        def _():
            gcarry_ref[...] = jnp.zeros_like(gcarry_ref)
            xcarry_ref[...] = jnp.zeros_like(xcarry_ref)

        xrx = jnp.concatenate([ph_ref[:, 0:W] * keep, pm_ref[:, 0:W]], axis=0)
        xc, xcb, r, ig, sp, a, mult, inv_mult = _lru_gates(xrx, cw_ref, cb_ref, wr_ref, br_ref, wi_ref, bi_ref, lam_ref)
        hs = hsm_ref[...]
        gt = pm_ref[:, W:2 * W]
        sg = _sigmoid(gt)
        dyv = _nt(do_ref[...], wo_ref[...])
        dp_ref[:, W:2 * W] = (dyv * hs * _dsilu(gt, sg)).astype(BF16)
        row = lax.broadcasted_iota(jnp.int32, (tT, 1), 0)
        m = jnp.where(row == tT - 1, 1.0, _shift_up(a, 1))
        sa_ref[...], sb_ref[...] = _group_scan(m, dyv * (gt * sg), reverse=True)
        first = _apply_carries(sa_ref, sb_ref, g_ref, gcarry_ref[0:1, :], reverse=True)
        G = g_ref[...]
        gcarry_ref[...] = jnp.broadcast_to(a[0:1, :] * first, (8, W))
        hs_prev = jnp.where(row == 0, hsh_ref[HL - 1:HL, :] * keep, _shift_down(hs, 1))
        da = G * hs_prev
        dmult = G * (ig * xc)
        di = G * mult * xc
        dxc = G * mult * ig
        dlog_a = da * a - dmult * (a * a) * inv_mult
        drp = dlog_a * ((-LRU_C) * sp) * r * (1.0 - r)
        dip = di * ig * (1.0 - ig)
        dlam = jnp.sum(dlog_a * ((-LRU_C) * r), axis=0, keepdims=True) * (-_sigmoid(-lam_ref[...]))
        drb, dib = drp.astype(BF16), dip.astype(BF16)
        back = []
        for hd in range(LRU_HEADS):
            cols = slice(hd * LRU_HD, (hd + 1) * LRU_HD)
            back.append(_nt(drb[:, cols], wr_ref[hd]) + _nt(dib[:, cols], wi_ref[hd]))
            dwr_h = _tn(xcb[:, cols], drb[:, cols])
            dwi_h = _tn(xcb[:, cols], dib[:, cols])

            @pl.when(i == 0)
            def _(hd=hd, dwr_h=dwr_h, dwi_h=dwi_h):
                dwr_ref[hd] = dwr_h
                dwi_ref[hd] = dwi_h

            @pl.when(i > 0)
            def _(hd=hd, dwr_h=dwr_h, dwi_h=dwi_h):
                dwr_ref[hd] += dwr_h
                dwi_ref[hd] += dwi_h

        dxc = dxc + jnp.concatenate(back, axis=1)
        dxcx = jnp.concatenate([dxc, xcarry_ref[...]], axis=0)
        dxr = cw_ref[LRU_CONV_K - 1:LRU_CONV_K, :] * dxc
        rows = []
        for k in range(LRU_CONV_K - 1):
            j = LRU_CONV_K - 1 - k
            dxr = dxr + cw_ref[k:k + 1, :] * _shift_up(dxcx, j)[0:tT]
            rows.append(jnp.sum(dxc * _shift_down(xrx, j)[HL:], axis=0, keepdims=True))
        rows.append(jnp.sum(dxc * xrx[HL:], axis=0, keepdims=True))
        dp_ref[:, 0:W] = dxr.astype(BF16)
        xcarry_ref[...] = dxc[0:8]
        rows += [jnp.sum(dxc, axis=0, keepdims=True), jnp.sum(drp, axis=0, keepdims=True),
                 jnp.sum(dip, axis=0, keepdims=True), dlam]
        vecs = jnp.concatenate(rows, axis=0)

        @pl.when(i == 0)
        def _():
            dvec_ref[...] = vecs

        @pl.when(i > 0)
        def _():
            dvec_ref[...] += vecs

    vec = pl.BlockSpec((None, 1, W), lambda i: (sl, 0, 0))
    heads = pl.BlockSpec((None, LRU_HEADS, LRU_HD, LRU_HD), lambda i: (sl, 0, 0, 0))
    dheads = pl.BlockSpec((LRU_HEADS, LRU_HD, LRU_HD), lambda i: (0, 0, 0))

    def tile(i):
        return (nT - 1 - i, 0)

    def prev_blk(i):
        return (jnp.maximum((nT - 1 - i) * hb - 1, 0), 0)

    return _pallas(
        body, name=name, grid=(nT,),
        in_specs=[pl.BlockSpec((tT, 2 * W), tile), pl.BlockSpec((HL, 2 * W), prev_blk),
                  pl.BlockSpec((tT, W), tile), pl.BlockSpec((HL, W), prev_blk), pl.BlockSpec((tT, D), tile),
                  pl.BlockSpec((None, W, D), lambda i: (0, 0, 0)), pl.BlockSpec((8, 128), lambda i: (0, 0)),
                  pl.BlockSpec((None, 8, W), lambda i: (sl, 0, 0)), vec, heads, vec, heads, vec, vec],
        out_specs=[pl.BlockSpec((tT, 2 * W), tile), dheads, dheads, pl.BlockSpec((8, W), lambda i: (0, 0))],
        out_shape=[jax.ShapeDtypeStruct((T, 2 * W), BF16), jax.ShapeDtypeStruct((LRU_HEADS, LRU_HD, LRU_HD), F32),
                   jax.ShapeDtypeStruct((LRU_HEADS, LRU_HD, LRU_HD), F32), jax.ShapeDtypeStruct((8, W), F32)],
        scratch_shapes=[pltpu.VMEM((8, W), F32), pltpu.VMEM((8, W), F32), pltpu.VMEM((tT, W), F32),
                        pltpu.VMEM((tT, W), F32), pltpu.VMEM((tT, W), F32)],
        compiler_params=_params("arbitrary"))(p, p, hs, hs, dout, w_out, after, cw, cb, wr, br, wi, bi, lam)


def _pad_rows(a, rows):
    return jnp.pad(a, ((0, 0), (0, rows - a.shape[1]), (0, 0)))


def _layer_fwd(even, h, w, w_in, w_out, after):
    sl = w["sl"]
    p, n = _in_proj(h, w["norm"], sl, w_in, 0, after, "in_proj_even" if even else "in_proj_odd")
    if even:
        y, aux, h_next = _even_mixer_fwd(p, h, w_out, sl, w["conv_w"], w["conv_b"], w["ln_g"], w["ln_b"], w["pool_w"],
                                         w["pool_b"], w["pool_scale"], "even_mixer_fwd")
    else:
        y, aux, h_next = _odd_mixer_fwd(p, h, w_out, sl, w["conv_w"], w["conv_b"], w["w_rg"], w["b_rg"], w["w_ig"],
                                        w["b_ig"], w["lam"], "odd_mixer_fwd")
    return h_next, (h, n, p, aux, y)


def _layer_bwd_weights(even, saved, w, w_out, dhb, after):
    h, n, p, aux, y = saved
    if even:
        dp, dcw, dvec, dpw = _even_mixer_bwd(p, aux, dhb, w_out, after, w["sl"], w["conv_w_rev"], w["ln_g"], w["ln_b"],
                                             w["pool_w"], w["pool_b"], w["pool_scale"], "even_mixer_bwd")
        dw_out = _dw_out(y, dhb, 0, 1, None, "dw_out_even")
        dw_in = _dw_in(n, dp, N_CHIPS, 0, 1, None, "dw_in_even")
        return dp, dw_in, dw_out, dict(conv_w=dcw, vec=dvec, pool_w=dpw)
    dp, dwr, dwi, dvec = _odd_mixer_bwd(p, aux, dhb, w_out, after, w["sl"], w["conv_w"], w["conv_b"], w["w_rg"],
                                        w["b_rg"], w["w_ig"], w["b_ig"], w["lam"], "odd_mixer_bwd")
    dw_out = _dw_out(y, dhb, 0, 1, None, "dw_out_odd")
    dw_in = _dw_in(n, dp, N_CHIPS, 0, 1, None, "dw_in_odd")
    return dp, dw_in, dw_out, dict(w_rg=dwr, w_ig=dwi, vec=dvec)


def _layer_bwd_input(even, saved, w, w_in, dp, dh, after):
    return _dn_proj(dp, w_in, 0, saved[0], w["norm"], w["sl"], dh, after, "dn_proj_even" if even else "dn_proj_odd")


ANY = pl.BlockSpec(memory_space=pl.ANY)


def _mesh_pos():
    return lax.axis_index("x"), lax.axis_index("y"), lax.axis_index("c")


def _other_chips(x, y):
    return [(1 - x, y), (x, 1 - y), (1 - x, 1 - y)]


def _remote(src, dst, ssem, rsem, dev):
    return pltpu.make_async_remote_copy(src_ref=src, dst_ref=dst, send_sem=ssem, recv_sem=rsem, device_id=dev,
                                        device_id_type=MESH)


def _comm_call(body, name, ins, out_shape, scratch, aliases=None):
    return _pallas(body, name=name, in_specs=[ANY] * len(ins), out_specs=[ANY] * len(out_shape), out_shape=out_shape,
                   scratch_shapes=scratch, input_output_aliases=aliases or {},
                   compiler_params=pltpu.CompilerParams(has_side_effects=True))(*ins)


def _cast_shard(w, layer, pos):
    _, R, C = w.shape
    tr = _row_tile(R, C)

    def body(pos_ref, w_ref, o_ref):
        o_ref[...] = w_ref[...].astype(BF16)

    grid_spec = pltpu.PrefetchScalarGridSpec(
        num_scalar_prefetch=1, grid=(R // tr,),
        in_specs=[pl.BlockSpec((None, tr, C), lambda i, pr: (layer, i, 0))],
        out_specs=pl.BlockSpec((None, None, tr, C), lambda i, pr: (0, pr[0], i, 0)))
    return _pallas(body, name="cast_shard", grid_spec=grid_spec,
                   out_shape=jax.ShapeDtypeStruct((1, N_CHIPS, R, C), BF16),
                   compiler_params=_params("parallel"))(pos, w)


def _gather_weights(big, small):
    nA = len(big)
    half = [a.shape[2] // 2 for a in big]

    def body(*refs):
        ins, outs = refs[:nA + 1], refs[nA + 1:2 * nA + 2]
        ssem, rsem, fsem, frsem, lsem = refs[2 * nA + 2:]
        x, y, c = _mesh_pos()
        k = 2 * x + y
        chips = _other_chips(x, y)
        sib = (x, y, 1 - c)

        def slab(a, chip, core):
            return outs[a].at[:, chip, pl.ds(core * half[a], half[a]), :]

        local = [pltpu.make_async_copy(ins[nA], outs[nA].at[k], lsem.at[0])]
        for cp in local:
            cp.start()
        sends = []
        for j, (ox, oy) in enumerate(chips):
            for a in range(nA):
                sends.append(_remote(slab(a, k, c), slab(a, k, c), ssem.at[a, j], rsem.at[a, j], (ox, oy, c)))
            sends.append(_remote(ins[nA], outs[nA].at[k], ssem.at[nA, j], rsem.at[nA, j], (ox, oy, c)))
        for cp in sends:
            cp.start()
        for j, (ox, oy) in enumerate(chips):
            kj = 2 * ox + oy
            for a in range(nA):
                got = slab(a, kj, c)
                _remote(got, got, ssem.at[a, j], rsem.at[a, j], (ox, oy, c)).wait_recv()
                fw = _remote(got, got, fsem.at[a, j], frsem.at[a, j], sib)
                fw.start()
                sends.append(fw)
            gs = outs[nA].at[kj]
            _remote(gs, gs, ssem.at[nA, j], rsem.at[nA, j], (ox, oy, c)).wait_recv()
        for j, (ox, oy) in enumerate(chips):
            kj = 2 * ox + oy
            for a in range(nA):
                theirs = slab(a, kj, 1 - c)
                _remote(theirs, theirs, fsem.at[a, j], frsem.at[a, j], sib).wait_recv()
        for cp in sends:
            cp.wait_send()
        for cp in local:
            cp.wait()

    out_shape = [jax.ShapeDtypeStruct(a.shape, a.dtype) for a in big]
    out_shape.append(jax.ShapeDtypeStruct((N_CHIPS,) + small.shape, small.dtype))
    scratch = [pltpu.SemaphoreType.DMA((nA + 1, 3)), pltpu.SemaphoreType.DMA((nA + 1, 3)),
               pltpu.SemaphoreType.DMA((nA, 3)), pltpu.SemaphoreType.DMA((nA, 3)), pltpu.SemaphoreType.DMA((1,))]
    return _comm_call(body, "gather_weights", list(big) + [small], out_shape, scratch, {a: a for a in range(nA)})


HBM = pl.BlockSpec(memory_space=pltpu.HBM)
SEM = pl.BlockSpec(memory_space=pltpu.SEMAPHORE)
EFFECT = pltpu.SideEffectType.DATAFLOW_SIDE_EFFECTING


def _split_start(arrays, copies, n, name):
    k = len(arrays)

    def body(*refs):
        for cp in copies(refs[k + 2:2 * k + 2], refs[k], refs[k + 1]):
            cp.start()
        refs[2 * k + 2][...] = jnp.zeros((8, 128), F32)

    out = _pallas(
        body, name=name,
        out_shape=(pltpu.SemaphoreType.DMA((n,)), pltpu.SemaphoreType.DMA((n,)),
                   *[pltpu.HBM(a.shape, a.dtype) for a in arrays], jax.ShapeDtypeStruct((8, 128), F32)),
        in_specs=(HBM,) * k, out_specs=(SEM, SEM) + (HBM,) * k + (pl.BlockSpec(memory_space=pltpu.VMEM),),
        input_output_aliases={i: i + 2 for i in range(k)},
        compiler_params=pltpu.CompilerParams(has_side_effects=EFFECT),
    )(*[pltpu.with_memory_space_constraint(a, pltpu.HBM) for a in arrays])
    return out[0], out[1], list(out[2:2 + k]), out[2 + k]


def _split_wait(ssem, rsem, arrays, copies, after, name):
    k = len(arrays)

    def body(*refs):
        for cp in copies(refs[:k], refs[k], refs[k + 1]):
            cp.wait_send()
            cp.wait_recv()

    out = _pallas(
        body, name=name, out_shape=tuple(pltpu.HBM(a.shape, a.dtype) for a in arrays),
        in_specs=(HBM,) * k + (SEM, SEM, ANY), out_specs=(HBM,) * k, input_output_aliases={i: i for i in range(k)},
        compiler_params=pltpu.CompilerParams(has_side_effects=EFFECT),
    )(*arrays, ssem, rsem, after)
    return list(out)


def _gather_copies(shapes):
    half = [s[2] // 2 for s in shapes]

    def copies(refs, ssem, rsem):
        x, y, c = _mesh_pos()
        out = []
        for j, (ox, oy) in enumerate(_other_chips(x, y)):
            for a, ref in enumerate(refs):
                slab = ref.at[:, 2 * x + y, pl.ds(c * half[a], half[a]), :]
                out.append(_remote(slab, slab, ssem.at[3 * a + j], rsem.at[3 * a + j], (ox, oy, c)))
        return out

    return copies


def _chips_copies(n_arr):
    def copies(refs, ssem, rsem):
        x, y, c = _mesh_pos()
        out = []
        for j, (ox, oy) in enumerate(_other_chips(x, y)):
            for a in range(n_arr):
                out.append(_remote(refs[a].at[:, 2 * ox + oy], refs[n_arr + a].at[:, 2 * x + y], ssem.at[3 * a + j],
                                   rsem.at[3 * a + j], (ox, oy, c)))
        return out

    return copies


def _halves_copies(shapes):
    n = len(shapes)
    half = [s[2] // 2 for s in shapes]

    def copies(refs, ssem, rsem):
        x, y, c = _mesh_pos()
        return [_remote(refs[a].at[:, :, pl.ds((1 - c) * half[a], half[a]), :], refs[n + a], ssem.at[a], rsem.at[a],
                        (x, y, 1 - c)) for a in range(n)]

    return copies


def _forward_cores(arrays):
    nA = len(arrays)
    half = [a.shape[2] // 2 for a in arrays]

    def body(*refs):
        outs = refs[nA:2 * nA]
        ssem, rsem = refs[2 * nA:]
        x, y, c = _mesh_pos()
        sib = (x, y, 1 - c)
        sends, waits = [], []
        for j, (ox, oy) in enumerate(_other_chips(x, y)):
            for a in range(nA):
                got = outs[a].at[:, 2 * ox + oy, pl.ds(c * half[a], half[a]), :]
                sends.append(_remote(got, got, ssem.at[a, j], rsem.at[a, j], sib))
                theirs = outs[a].at[:, 2 * ox + oy, pl.ds((1 - c) * half[a], half[a]), :]
                waits.append(_remote(theirs, theirs, ssem.at[a, j], rsem.at[a, j], sib))
        for cp in sends:
            cp.start()
        for cp in waits:
            cp.wait_recv()
        for cp in sends:
            cp.wait_send()

    out_shape = [jax.ShapeDtypeStruct(a.shape, a.dtype) for a in arrays]
    scratch = [pltpu.SemaphoreType.DMA((nA, 3)), pltpu.SemaphoreType.DMA((nA, 3))]
    return _comm_call(body, "forward_cores", list(arrays), out_shape, scratch, {a: a for a in range(nA)})


def _exchange_halves(big):
    nA = len(big)
    half = [a.shape[2] // 2 for a in big]

    def body(*refs):
        ins, outs = refs[:nA], refs[nA:2 * nA]
        ssem, rsem = refs[2 * nA:]
        x, y, c = _mesh_pos()
        sib = (x, y, 1 - c)
        sends = [_remote(ins[a].at[:, :, pl.ds((1 - c) * half[a], half[a]), :], outs[a], ssem.at[a], rsem.at[a], sib)
                 for a in range(nA)]
        for cp in sends:
            cp.start()
        for a in range(nA):
            _remote(outs[a], outs[a], ssem.at[a], rsem.at[a], sib).wait_recv()
        for cp in sends:
            cp.wait_send()

    out_shape = [jax.ShapeDtypeStruct((a.shape[0], N_CHIPS, h, a.shape[3]), a.dtype) for a, h in zip(big, half)]
    scratch = [pltpu.SemaphoreType.DMA((nA,)), pltpu.SemaphoreType.DMA((nA,))]
    return _comm_call(body, "exchange_halves", list(big), out_shape, scratch)


def _exchange_final(grads, everywhere, small):
    nA = len(grads)
    n_remote = sum(7 if ev else 1 for ev in everywhere) + 7

    def body(*refs):
        small_ref, outs, gathered = refs[nA], refs[nA + 1:2 * nA + 1], refs[2 * nA + 1]
        ssem, rsem, lsem = refs[2 * nA + 2:]
        x, y, c = _mesh_pos()
        k = 2 * x + y
        sib = (x, y, 1 - c)
        local = pltpu.make_async_copy(small_ref, gathered.at[2 * k + c], lsem.at[0])
        local.start()
        sends, arrivals, waits = [], [], []
        count = [0]

        def sems():
            count[0] += 1
            return ssem.at[count[0] - 1], rsem.at[count[0] - 1]

        def to_sibling(src, mine, theirs):
            sm = sems()
            sends.append(_remote(src, mine, *sm, sib))
            waits.append(_remote(theirs, theirs, *sm, sib))

        def to_everyone(src, place):
            to_sibling(src, place(k, c), place(k, 1 - c))
            for (ox, oy) in _other_chips(x, y):
                ici, d2d = sems(), sems()
                got = place(2 * ox + oy, c)
                sends.append(_remote(src, place(k, c), *ici, (ox, oy, c)))
                arrivals.append((_remote(got, got, *ici, (ox, oy, c)), _remote(got, got, *d2d, sib)))
                theirs = place(2 * ox + oy, 1 - c)
                waits.append(_remote(theirs, theirs, *d2d, sib))

        to_everyone(small_ref, lambda chip, core: gathered.at[2 * chip + core])
        for a in range(nA):
            if everywhere[a]:
                r2 = grads[a].shape[1] // N_DEV

                def place(chip, core, a=a, r2=r2):
                    return outs[a].at[:, pl.ds((2 * chip + core) * r2, r2), :]

                to_everyone(place(k, c), place)
            else:
                r2 = grads[a].shape[1] // 2
                mine = outs[a].at[:, pl.ds(c * r2, r2), :]
                to_sibling(mine, mine, outs[a].at[:, pl.ds((1 - c) * r2, r2), :])
        for cp in sends:
            cp.start()
        for arrived, onward in arrivals:
            arrived.wait_recv()
            onward.start()
        for cp in waits:
            cp.wait_recv()
        for cp in sends + [onward for _, onward in arrivals]:
            cp.wait_send()
        local.wait()

    out_shape = [jax.ShapeDtypeStruct(g.shape, g.dtype) for g in grads]
    out_shape.append(jax.ShapeDtypeStruct((N_DEV,) + small.shape, small.dtype))
    scratch = [pltpu.SemaphoreType.DMA((n_remote,)), pltpu.SemaphoreType.DMA((n_remote,)), pltpu.SemaphoreType.DMA((1,))]
    return _comm_call(body, "exchange_final", list(grads) + [small], out_shape, scratch, {a: a for a in range(nA)})


BLOCK_BYTES = 4 << 20


def _row_tile(rows, cols, mult=16, limit=BLOCK_BYTES):
    best = mult
    for t in range(mult, rows + 1, mult):
        if rows % t == 0 and t * cols * 4 <= limit:
            best = t
    return best


def _add_cores(own, recv, pos):
    L, _, R, C = own.shape
    r2 = R // 2
    tr = _row_tile(r2, C)
    nb = r2 // tr

    def body(pos_ref, a_ref, r_ref, o_ref):
        o_ref[...] = (a_ref[...].astype(F32) + r_ref[...].astype(F32)).astype(BF16)

    blk = (None, None, tr, C)
    grid_spec = pltpu.PrefetchScalarGridSpec(
        num_scalar_prefetch=1, grid=(L, N_CHIPS, nb),
        in_specs=[pl.BlockSpec(blk, lambda l, s, i, pr: (l, s, pr[1] * nb + i, 0)),
                  pl.BlockSpec(blk, lambda l, s, i, pr: (l, s, i, 0))],
        out_specs=pl.BlockSpec(blk, lambda l, s, i, pr: (l, s, i, 0)))
    return _pallas(body, name="add_cores", grid_spec=grid_spec,
                   out_shape=jax.ShapeDtypeStruct((L, N_CHIPS, r2, C), BF16),
                   compiler_params=_params("parallel", "parallel", "parallel"))(pos, own, recv)


def _sum_chips(own, recv, pos, everywhere, layer, nlayers, prev):
    _, _, r2, C = own.shape
    tr = _row_tile(r2, 2 * C)
    nb = r2 // tr

    def body(pos_ref, a_ref, r_ref, *rest):
        acc = None
        for s in range(N_CHIPS):
            term = jnp.where(pos_ref[0] == s, a_ref[...], r_ref[s]).astype(F32)
            acc = term if acc is None else acc + term
        rest[-1][...] = acc

    if everywhere:
        def out_map(i, pr):
            return (layer, (2 * pr[0] + pr[1]) * nb + i, 0)
    else:
        def out_map(i, pr):
            return (layer, pr[1] * nb + i, 0)

    in_specs = [pl.BlockSpec((None, None, tr, C), lambda i, pr: (0, pr[0], i, 0)),
                pl.BlockSpec((None, N_CHIPS, tr, C), lambda i, pr: (0, 0, i, 0))]
    grid_spec = pltpu.PrefetchScalarGridSpec(
        num_scalar_prefetch=1, grid=(nb,), in_specs=in_specs + ([] if prev is None else [ANY]),
        out_specs=pl.BlockSpec((None, tr, C), out_map))
    rows = (N_DEV if everywhere else 2) * r2
    args = (pos, own, recv) if prev is None else (pos, own, recv, prev)
    return _pallas(body, name="sum_chips", grid_spec=grid_spec, out_shape=jax.ShapeDtypeStruct((nlayers, rows, C), F32),
                   input_output_aliases={} if prev is None else {3: 0},
                   compiler_params=_params("parallel"))(*args)


def _sum_devices(parts):
    n, R, C = parts.shape
    tr = _row_tile(R, C * n, 8)

    def body(p_ref, o_ref):
        acc = p_ref[0]
        for s in range(1, n):
            acc = acc + p_ref[s]
        o_ref[...] = acc

    return _pallas(body, name="sum_devices", grid=(R // tr,), in_specs=[pl.BlockSpec((n, tr, C), lambda i: (0, i, 0))],
                   out_specs=pl.BlockSpec((tr, C), lambda i: (i, 0)), out_shape=jax.ShapeDtypeStruct((R, C), F32),
                   compiler_params=_params("parallel"))(parts)


def _adamw(w, g, m, v, name):
    L, R, C = w.shape
    tr = _row_tile(R, C, 8, BLOCK_BYTES // 2)

    def body(w_ref, g_ref, m_ref, v_ref, d_ref, m2_ref, v2_ref):
        gg = g_ref[...]
        m2 = ADAM_B1 * m_ref[...] + (1.0 - ADAM_B1) * gg
        v2 = ADAM_B2 * v_ref[...] + (1.0 - ADAM_B2) * (gg * gg)
        m_hat = m2 / (1.0 - ADAM_B1 ** ADAM_STEP)
        v_hat = v2 / (1.0 - ADAM_B2 ** ADAM_STEP)
        d_ref[...] = -ADAM_LR * (m_hat / (jnp.sqrt(v_hat) + ADAM_EPS) + ADAM_WD * w_ref[...])
        m2_ref[...] = m2
        v2_ref[...] = v2

    blk = pl.BlockSpec((1, tr, C), lambda l, i: (l, i, 0))
    shp = jax.ShapeDtypeStruct((L, R, C), F32)
    return _pallas(body, name=name, grid=(L, R // tr), in_specs=[blk] * 4, out_specs=[blk] * 3, out_shape=[shp] * 3,
                   compiler_params=_params("parallel", "parallel"))(w, g, m, v)


WEIGHTS = ("norm_even", "w_in_even", "conv_a_w", "conv_a_b", "ln_a_g", "ln_a_b", "pool_w", "pool_b", "pool_scale",
           "w_out_even", "norm_odd", "w_in_odd", "conv_c_w", "conv_c_b", "w_rg", "b_rg", "w_ig", "b_ig", "lru_lambda",
           "w_out_odd", "final_norm")
BIG = ("w_in_even", "w_out_even", "pool_w", "w_in_odd", "w_out_odd", "w_rg", "w_ig")
SMALL = tuple(n for n in WEIGHTS if n not in BIG)
SMALL_SHARDED = ("conv_a_w", "pool_b", "norm_odd", "conv_c_w", "conv_c_b", "b_rg", "b_ig", "lru_lambda")


def _pack(arrs):
    flat = jnp.concatenate([a.reshape(-1) for a in arrs])
    rows = -(-flat.shape[0] // (64 * 128)) * 64
    return jnp.pad(flat, (0, rows * 128 - flat.shape[0])).reshape(rows, 128)


def _unpack(buf, shapes, lead=()):
    flat = buf.reshape(tuple(lead) + (-1,))
    out, o = [], 0
    for s in shapes:
        n = 1
        for d in s:
            n *= d
        out.append(flat[..., o:o + n].reshape(tuple(lead) + tuple(s)))
        o += n
    return out


def _shard(full, axis, k):
    n = full.shape[axis] // N_CHIPS
    return lax.dynamic_slice_in_dim(full, k * n, n, axis)


def kernel(x, norm_even, w_in_even, conv_a_w, conv_a_b, ln_a_g, ln_a_b, pool_w, pool_b, pool_scale, w_out_even, norm_odd, w_in_odd, conv_c_w, conv_c_b, w_rg, b_rg, w_ig, b_ig, lru_lambda, w_out_odd, final_norm, loss_target, m_norm_even, m_w_in_even, m_conv_a_w, m_conv_a_b, m_ln_a_g, m_ln_a_b, m_pool_w, m_pool_b, m_pool_scale, m_w_out_even, m_norm_odd, m_w_in_odd, m_conv_c_w, m_conv_c_b, m_w_rg, m_b_rg, m_w_ig, m_b_ig, m_lru_lambda, m_w_out_odd, m_final_norm, v_norm_even, v_w_in_even, v_conv_a_w, v_conv_a_b, v_ln_a_g, v_ln_a_b, v_pool_w, v_pool_b, v_pool_scale, v_w_out_even, v_norm_odd, v_w_in_odd, v_conv_c_w, v_conv_c_b, v_w_rg, v_b_rg, v_w_ig, v_b_ig, v_lru_lambda, v_w_out_odd, v_final_norm):
    P = dict(locals())
    xi, yi, ci = _mesh_pos()
    k = 2 * xi + yi
    L = w_in_even.shape[0]
    D = D_MODEL

    pos = jnp.stack([k, ci]).astype(jnp.int32)
    depth = 2 * L
    pool_w3 = pool_w.reshape(L, 4 * 64, POOL_GW)

    def cast_group(layer):
        j = layer // 2
        if layer % 2 == 0:
            return [_cast_shard(w_in_even, j, pos), _cast_shard(w_out_even, j, pos), _cast_shard(pool_w3, j, pos)]
        return [_cast_shard(w_in_odd, j, pos), _cast_shard(w_out_odd, j, pos)]

    *group, g_small = _gather_weights(cast_group(0), _pack([P[n] for n in SMALL_SHARDED]))
    full = {}
    for n, a in zip(SMALL_SHARDED, _unpack(g_small, [P[n].shape for n in SMALL_SHARDED], lead=(N_CHIPS,))):
        a = jnp.moveaxis(a, 0, -2)
        full[n] = a.reshape(a.shape[:-2] + (N_CHIPS * a.shape[-1],))

    small_even = dict(norm=norm_even[:, None], conv_w=_pad_rows(full["conv_a_w"], 32),
                      conv_w_rev=_pad_rows(full["conv_a_w"][:, ::-1], 32), conv_b=conv_a_b[:, None], ln_g=ln_a_g[:, None],
                      ln_b=ln_a_b[:, None], pool_b=full["pool_b"].reshape(L, 1, D), pool_scale=pool_scale[:, None])
    small_odd = dict(norm=full["norm_odd"][:, None], conv_w=_pad_rows(full["conv_c_w"], 8),
                     conv_b=full["conv_c_b"][:, None], w_rg=w_rg.astype(BF16), b_rg=full["b_rg"][:, None],
                     w_ig=w_ig.astype(BF16), b_ig=full["b_ig"][:, None], lam=full["lru_lambda"][:, None])

    def small_weights(layer, group):
        if layer % 2 == 0:
            pw = group[2].reshape(N_CHIPS, 4, 64, POOL_GW).transpose(1, 0, 2, 3).reshape(4, POOL_GW, POOL_GW)
            return dict(small_even, sl=layer // 2, pool_w=pw)
        return dict(small_odd, sl=layer // 2)

    no_token = jnp.zeros((8, 128), F32)
    h = x[0]
    saved, big_w, small_w = [], [], []
    for layer in range(depth):
        token = no_token
        if layer + 1 < depth:
            nxt = cast_group(layer + 1)
            copies = _gather_copies([a.shape for a in nxt])
            ssem, rsem, nxt, token = _split_start(nxt, copies, 3 * len(nxt), "gather_start%d" % (layer + 1))
        small_w.append(small_weights(layer, group))
        big_w.append((group[0], group[1].reshape(1, -1, D)))
        h, sv = _layer_fwd(layer % 2 == 0, h, small_w[layer], *big_w[layer], token)
        saved.append(sv)
        if layer + 1 < depth:
            group = _forward_cores(_split_wait(ssem, rsem, nxt, copies, h, "gather_wait%d" % (layer + 1)))

    dh, dhb, d_final, loss = _loss_head(h, final_norm[None], loss_target[0])
    loss = lax.psum(loss[0, 0], ("x", "y", "c"))
    everywhere = [False, False, False, False, False, True, True]
    final = [None] * len(everywhere)
    small_of = [None] * depth

    def finish(pending, after):
        ssem, rsem, arrs, copies, slots, pj, pl_ = pending
        arrs = _split_wait(ssem, rsem, arrs, copies, after, "chips_wait%d" % pl_)
        for a, r, s in zip(arrs[:len(slots)], arrs[len(slots):], slots):
            final[s] = _sum_chips(a, r, pos, everywhere[s], pj, L, final[s])

    pending = None
    token = no_token
    for layer in reversed(range(depth)):
        j = layer // 2
        even_layer = layer % 2 == 0
        dp, dw_in, dw_out, sm = _layer_bwd_weights(even_layer, saved[layer], small_w[layer], big_w[layer][1], dhb, token)
        if even_layer:
            dpw = sm["pool_w"].reshape(4, N_CHIPS, 64, POOL_GW).transpose(1, 0, 2, 3)
            parts = [dw_in, dw_out.reshape(1, N_CHIPS, -1, D), dpw.reshape(1, N_CHIPS, 4 * 64, POOL_GW).astype(BF16)]
            slots = [0, 1, 2]
        else:
            parts = [dw_in, dw_out.reshape(1, N_CHIPS, -1, D),
                     sm["w_rg"].reshape(1, N_CHIPS, -1, LRU_HD).astype(BF16),
                     sm["w_ig"].reshape(1, N_CHIPS, -1, LRU_HD).astype(BF16)]
            slots = [3, 4, 5, 6]
        n = len(parts)
        if layer > 0:
            hcopies = _halves_copies([a.shape for a in parts])
            hland = [lax.empty((1, N_CHIPS, a.shape[2] // 2, a.shape[3]), a.dtype) for a in parts]
            hs, hr, harrs, htoken = _split_start(parts + hland, hcopies, n, "halves_start%d" % layer)
            dh, dhb, sm["norm"] = _layer_bwd_input(even_layer, saved[layer], small_w[layer], big_w[layer][0], dp, dh,
                                                   htoken)
            harrs = _split_wait(hs, hr, harrs, hcopies, dh, "halves_wait%d" % layer)
            parts, recv = harrs[:n], harrs[n:]
        else:
            recv = _exchange_halves(parts)
        pair = [_add_cores(a, r, pos) for a, r in zip(parts, recv)]
        copies = _chips_copies(n)
        land = [lax.empty(a.shape, a.dtype) for a in pair]
        ssem, rsem, arrs, token = _split_start(pair + land, copies, 3 * n, "chips_start%d" % layer)
        if layer == 0:
            dh, dhb, sm["norm"] = _layer_bwd_input(even_layer, saved[layer], small_w[layer], big_w[layer][0], dp, dh, token)
        small_of[layer] = sm
        if pending is not None:
            finish(pending, dh)
        pending = (ssem, rsem, arrs, copies, slots, j, layer)
    grad_x = dh
    small_g = []
    for jj in range(L):
        ge, go = small_of[2 * jj], small_of[2 * jj + 1]
        small_g += [ge["conv_w"].reshape(32, 8, D).sum(axis=1)[:CONV_K], ge["vec"][0:5], ge["norm"], go["vec"], go["norm"]]
    small_g.append(d_final)
    small_shapes = [a.shape for a in small_g]
    packed_small = _pack(small_g)
    finish(pending, packed_small)
    *gw, recv_small = _exchange_final(final, everywhere, packed_small)
    sg = _unpack(_sum_devices(recv_small), small_shapes)

    grads = dict(w_in_even=gw[0], w_out_even=gw[1], pool_w=gw[2].reshape(pool_w.shape), w_in_odd=gw[3], w_out_odd=gw[4],
                 w_rg=gw[5].reshape(w_rg.shape), w_ig=gw[6].reshape(w_ig.shape), final_norm=sg[-1][0])
    ev = [sg[5 * j + 1] for j in range(L)]
    ov = [sg[5 * j + 3] for j in range(L)]
    grads["conv_a_w"] = _shard(jnp.stack([sg[5 * j] for j in range(L)]), 2, k)
    grads["norm_even"] = jnp.stack([sg[5 * j + 2][0] for j in range(L)])
    grads["norm_odd"] = _shard(jnp.stack([sg[5 * j + 4][0] for j in range(L)]), 1, k)
    for r, n in enumerate(("conv_a_b", "ln_a_g", "ln_a_b", "pool_scale")):
        grads[n] = jnp.stack([e[r] for e in ev])
    grads["pool_b"] = _shard(jnp.stack([e[4].reshape(4, POOL_GW) for e in ev]), 2, k)
    grads["conv_c_w"] = _shard(jnp.stack([o[0:4] for o in ov]), 2, k)
    for r, n in zip((4, 5, 6, 7), ("conv_c_b", "b_rg", "b_ig", "lru_lambda")):
        grads[n] = _shard(jnp.stack([o[r] for o in ov]), 1, k)

    delta, new_m, new_v = {}, {}, {}
    for n in BIG:
        s3 = (L, -1, P[n].shape[-1])
        d, m2, v2 = _adamw(P[n].reshape(s3), grads[n].reshape(s3), P["m_" + n].reshape(s3), P["v_" + n].reshape(s3), "adamw")
        delta[n], new_m[n], new_v[n] = d.reshape(P[n].shape), m2.reshape(P[n].shape), v2.reshape(P[n].shape)
    shapes = [P[n].shape for n in SMALL]
    packed = [_pack([src[n] for n in SMALL])[None] for src in
              (P, grads, {n: P["m_" + n] for n in SMALL}, {n: P["v_" + n] for n in SMALL})]
    for res, out in zip(_adamw(*packed, "adamw_small"), (delta, new_m, new_v)):
        for n, a in zip(SMALL, _unpack(res[0], shapes)):
            out[n] = a

    return (loss, grad_x[None], *[grads[n] for n in WEIGHTS], *[delta[n] for n in WEIGHTS],
            *[new_m[n] for n in WEIGHTS], *[new_v[n] for n in WEIGHTS])
```

```python
import jax
import jax.numpy as jnp
from jax import lax
from jax.experimental import pallas as pl
from jax.experimental.pallas import tpu as pltpu

F32 = jnp.float32
BF16 = jnp.bfloat16
MESH = pl.DeviceIdType.MESH

D_MODEL = 1024
N_CHIPS = 4
N_DEV = 8
EPS_RMS = 1e-6
EPS_LN = 1e-5
CONV_K = 31
POOL_WINDOWS = (2, 4, 8, 16)
POOL_GW = 256
LRU_HEADS = 12
LRU_HD = 128
W_LRU = LRU_HEADS * LRU_HD
LRU_CONV_K = 4
LRU_C = 8.0
ADAM_LR = 0.001
ADAM_B1 = 0.9
ADAM_B2 = 0.999
ADAM_EPS = 1e-08
ADAM_WD = 0.01
ADAM_STEP = 10

VMEM_LIMIT_BYTES = 56 * 1024 * 1024
ROW_TILE = 512
MIX_TILE = 256
EVEN_HALO = 32
ODD_HALO = 8


def _pallas(body, **kw):
    return pl.pallas_call(body, **kw)


def _params(*sem):
    return pltpu.CompilerParams(dimension_semantics=sem if sem else None, vmem_limit_bytes=VMEM_LIMIT_BYTES)


def _sigmoid(x):
    return 0.5 * jnp.tanh(0.5 * x) + 0.5


def _dsilu(x, s):
    return s * (1.0 + x * (1.0 - s))


def _nt(a, b):
    return lax.dot_general(a, b, (((1,), (1,)), ((), ())), preferred_element_type=F32)


def _tn(a, b):
    return lax.dot_general(a, b, (((0,), (0,)), ((), ())), preferred_element_type=F32)


def _in_proj(h, g, glayer, wg, layer, after, name):
    T, D = h.shape
    _, nblk, _, nb = wg.shape

    nrow = T // ROW_TILE

    def body(h_ref, g_ref, w_ref, after_ref, p_ref, n_ref, n_all):
        j, i = pl.program_id(0), pl.program_id(1)

        @pl.when(j == 0)
        def _():
            x = h_ref[...]
            r = lax.rsqrt(jnp.mean(x * x, axis=-1, keepdims=True) + EPS_RMS)
            nn = (x * r * g_ref[...]).astype(BF16)
            n_ref[...] = nn
            n_all[i] = nn

        p_ref[...] = jnp.dot(n_all[i], w_ref[0], preferred_element_type=F32)

    def rows_once(j, i):
        return (jnp.where(j == 0, i, nrow - 1), 0)

    return _pallas(
        body, name=name, grid=(nblk, nrow),
        in_specs=[pl.BlockSpec((ROW_TILE, D), rows_once), pl.BlockSpec((None, 1, D), lambda j, i: (glayer, 0, 0)),
                  pl.BlockSpec((None, 1, D, nb), lambda j, i: (layer, j, 0, 0)),
                  pl.BlockSpec((8, 128), lambda j, i: (0, 0))],
        out_specs=[pl.BlockSpec((ROW_TILE, nb), lambda j, i: (i, j)), pl.BlockSpec((ROW_TILE, D), rows_once)],
        out_shape=[jax.ShapeDtypeStruct((T, nblk * nb), F32), jax.ShapeDtypeStruct((T, D), BF16)],
        scratch_shapes=[pltpu.VMEM((nrow, ROW_TILE, D), BF16)],
        compiler_params=_params("arbitrary", "arbitrary"))(h, g, wg, after)


def _dn_proj(dp, wg, layer, h, g, glayer, dres, after, name):
    T, D = h.shape
    _, nblk, _, nb = wg.shape

    nrow = T // ROW_TILE

    def body(dp_ref, w_ref, h_ref, g_ref, dres_ref, after_ref, dh_ref, dhb_ref, dg_ref, acc_ref):
        j, i = pl.program_id(0), pl.program_id(1)
        part = _nt(dp_ref[...], w_ref[0])

        @pl.when(j == 0)
        def _():
            acc_ref[i] = part

        @pl.when(j > 0)
        def _():
            acc_ref[i] += part

        @pl.when(j == nblk - 1)
        def _():
            x = h_ref[...]
            r = lax.rsqrt(jnp.mean(x * x, axis=-1, keepdims=True) + EPS_RMS)
            dn = acc_ref[i]
            q = dn * g_ref[...]
            dh = dres_ref[...] + r * q - x * ((r * r * r) * jnp.mean(q * x, axis=-1, keepdims=True))
            dh_ref[...] = dh
            dhb_ref[...] = dh.astype(BF16)
            dgp = jnp.sum(dn * (x * r), axis=0, keepdims=True)

            @pl.when(i == 0)
            def _():
                dg_ref[...] = dgp

            @pl.when(i > 0)
            def _():
                dg_ref[...] += dgp

    def rows_last(j, i):
        return (jnp.where(j == nblk - 1, i, 0), 0)

    return _pallas(
        body, name=name, grid=(nblk, nrow),
        in_specs=[pl.BlockSpec((ROW_TILE, nb), lambda j, i: (i, j)),
                  pl.BlockSpec((None, 1, D, nb), lambda j, i: (layer, j, 0, 0)),
                  pl.BlockSpec((ROW_TILE, D), rows_last), pl.BlockSpec((None, 1, D), lambda j, i: (glayer, 0, 0)),
                  pl.BlockSpec((ROW_TILE, D), rows_last), pl.BlockSpec((8, 128), lambda j, i: (0, 0))],
        out_specs=[pl.BlockSpec((ROW_TILE, D), rows_last), pl.BlockSpec((ROW_TILE, D), rows_last),
                   pl.BlockSpec((1, D), lambda j, i: (0, 0))],
        out_shape=[jax.ShapeDtypeStruct((T, D), F32), jax.ShapeDtypeStruct((T, D), BF16),
                   jax.ShapeDtypeStruct((1, D), F32)],
        scratch_shapes=[pltpu.VMEM((nrow, ROW_TILE, D), F32)],
        compiler_params=_params("arbitrary", "arbitrary"))(dp, wg, h, g, dres, after)


def _dw_in(n, dp, nblk, layer, nlayers, prev, name):
    T, D = n.shape
    nb = dp.shape[1] // nblk
    ta = D

    def body(n_ref, dp_ref, *rest):
        rest[-1][0] = _tn(n_ref[...], dp_ref[...]).astype(BF16)

    in_specs = [pl.BlockSpec((T, ta), lambda j, i: (0, i)), pl.BlockSpec((T, nb), lambda j, i: (0, j))]
    args = (n, dp) if prev is None else (n, dp, prev)
    return _pallas(
        body, name=name, grid=(nblk, D // ta), in_specs=in_specs + ([] if prev is None else [ANY]),
        out_specs=pl.BlockSpec((None, 1, ta, nb), lambda j, i: (layer, j, i, 0)),
        out_shape=jax.ShapeDtypeStruct((nlayers, nblk, D, nb), BF16),
        input_output_aliases={} if prev is None else {2: 0},
        compiler_params=_params("parallel", "parallel"))(*args)


def _dw_out(y, dout, layer, nlayers, prev, name):
    T, K = y.shape
    D = dout.shape[1]
    tk = 512

    def body(y_ref, d_ref, *rest):
        rest[-1][...] = _tn(y_ref[...], d_ref[...]).astype(BF16)

    in_specs = [pl.BlockSpec((T, tk), lambda i: (0, i)), pl.BlockSpec((T, D), lambda i: (0, 0))]
    args = (y, dout) if prev is None else (y, dout, prev)
    return _pallas(
        body, name=name, grid=(K // tk,), in_specs=in_specs + ([] if prev is None else [ANY]),
        out_specs=pl.BlockSpec((None, tk, D), lambda i: (layer, i, 0)),
        out_shape=jax.ShapeDtypeStruct((nlayers, K, D), BF16),
        input_output_aliases={} if prev is None else {2: 0},
        compiler_params=_params("parallel"))(*args)


def _loss_head(h, g, tgt):
    T, D = h.shape
    tm = MIX_TILE

    def body(h_ref, g_ref, t_ref, dh_ref, dhb_ref, dg_ref, loss_ref):
        i = pl.program_id(0)
        x = h_ref[...]
        gg = g_ref[...]
        r = lax.rsqrt(jnp.mean(x * x, axis=-1, keepdims=True) + EPS_RMS)
        xr = x * r
        e = xr * gg - t_ref[...]
        lp = 0.5 * jnp.sum(jnp.mean(e * e, axis=-1, keepdims=True), axis=0, keepdims=True)
        dn = e * (1.0 / D)
        q = dn * gg
        dh = r * q - x * ((r * r * r) * jnp.mean(q * x, axis=-1, keepdims=True))
        dh_ref[...] = dh
        dhb_ref[...] = dh.astype(BF16)
        dgp = jnp.sum(dn * xr, axis=0, keepdims=True)

        @pl.when(i == 0)
        def _():
            dg_ref[...] = dgp
            loss_ref[...] = lp

        @pl.when(i > 0)
        def _():
            dg_ref[...] += dgp
            loss_ref[...] += lp

    return _pallas(
        body, name="loss_head", grid=(T // tm,),
        in_specs=[pl.BlockSpec((tm, D), lambda i: (i, 0)), pl.BlockSpec((1, D), lambda i: (0, 0)),
                  pl.BlockSpec((tm, D), lambda i: (i, 0))],
        out_specs=[pl.BlockSpec((tm, D), lambda i: (i, 0)), pl.BlockSpec((tm, D), lambda i: (i, 0)),
                   pl.BlockSpec((1, D), lambda i: (0, 0)), pl.BlockSpec((1, 1), lambda i: (0, 0))],
        out_shape=[jax.ShapeDtypeStruct((T, D), F32), jax.ShapeDtypeStruct((T, D), BF16),
                   jax.ShapeDtypeStruct((1, D), F32), jax.ShapeDtypeStruct((1, 1), F32)],
        compiler_params=_params("arbitrary"))(h, g, tgt)


def _shift_up(x, j):
    return x if j == 0 else pltpu.roll(x, x.shape[0] - j, 0)


def _shift_down(x, j):
    return x if j == 0 else pltpu.roll(x, j, 0)


def _fill_shifted(dst_ref, src_ref):
    rows = dst_ref.shape[1]
    for s in range(8):
        dst_ref[s] = src_ref[pl.ds(s, rows), :]


def _fill_taps(wb_ref, w_ref):
    for k in range(w_ref.shape[0]):
        wb_ref[k] = jnp.broadcast_to(w_ref[k:k + 1, :], wb_ref.shape[1:])


def _tap_sum(sh_ref, wb_ref, r0, nrows, offsets):
    accs = [None] * (nrows // 8)
    for k, o in enumerate(offsets):
        wk = wb_ref[k]
        for u in range(nrows // 8):
            term = wk * sh_ref[o % 8, pl.ds(r0 + (o // 8) * 8 + 8 * u, 8), :]
            accs[u] = term if accs[u] is None else accs[u] + term
    return jnp.concatenate(accs, axis=0)


def _pool_sums(vx, up):
    sh = _shift_up if up else _shift_down
    outs = []
    for gi, w in enumerate(POOL_WINDOWS):
        s = vx[:, gi * POOL_GW:(gi + 1) * POOL_GW]
        j = 1
        while j < w:
            s = s + sh(s, j)
            j *= 2
        outs.append(s)
    return outs


def _inv_count(row0, nrows):
    pos = (row0 + 1 + lax.broadcasted_iota(jnp.int32, (nrows, 1), 0)).astype(F32)
    return [1.0 / jnp.minimum(pos, float(w)) for w in POOL_WINDOWS]


def _even_mixer_fwd(p, h, w_out, sl, cw, cb, lg, lb, pw, pb, sc, name):
    T = p.shape[0]
    C = D_MODEL
    tT, HL = MIX_TILE, EVEN_HALO
    hb = tT // HL
    chunk = 32

    def body(pm_ref, ph_ref, cw_ref, cb_ref, lg_ref, lb_ref, pw_ref, pb_ref, sc_ref, h_ref, wo_ref, y_ref, u1_ref,
             hn_ref, u0x_ref, sh_ref, wb_ref):
        i = pl.program_id(0)
        keep = (i > 0).astype(F32)

        @pl.when(i == 0)
        def _():
            _fill_taps(wb_ref, cw_ref)

        u0x_ref[0:HL] = ph_ref[:, 0:C] * _sigmoid(ph_ref[:, C:2 * C]) * keep
        u0x_ref[HL:HL + tT] = pm_ref[:, 0:C] * _sigmoid(pm_ref[:, C:2 * C])
        u0x_ref[HL + tT:HL + tT + 8] = jnp.zeros((8, C), F32)
        _fill_shifted(sh_ref, u0x_ref)
        offs = [HL - (CONV_K - 1) + k for k in range(CONV_K)]

        def conv_chunk(c, carry):
            r0 = pl.multiple_of(c * chunk, chunk)
            u1_ref[pl.ds(r0, chunk), :] = _tap_sum(sh_ref, wb_ref, r0, chunk, offs) + cb_ref[...]
            return carry

        lax.fori_loop(0, tT // chunk, conv_chunk, 0)
        u1 = u1_ref[...]
        mu = jnp.mean(u1, axis=-1, keepdims=True)
        xc = u1 - mu
        rs = lax.rsqrt(jnp.mean(xc * xc, axis=-1, keepdims=True) + EPS_LN)
        u2 = xc * rs * lg_ref[...] + lb_ref[...]
        u3 = u2 * _sigmoid(u2)
        ag = pm_ref[:, 2 * C:3 * C]
        y_ref[:, 0:C] = (u3 * (ag * _sigmoid(ag))).astype(BF16)
        vx = jnp.concatenate([ph_ref[:, 3 * C:4 * C] * keep, pm_ref[:, 3 * C:4 * C]], axis=0)
        sums = _pool_sums(vx, up=False)
        inv = _inv_count(i * tT, tT)
        for gi in range(len(POOL_WINDOWS)):
            cols = slice(gi * POOL_GW, (gi + 1) * POOL_GW)
            d0 = sums[gi][HL:] * inv[gi] - vx[HL:, cols]
            d1 = jnp.dot(d0.astype(BF16), pw_ref[gi], preferred_element_type=F32) + pb_ref[:, cols]
            bg = pm_ref[:, 4 * C + gi * POOL_GW:4 * C + (gi + 1) * POOL_GW]
            y_ref[:, C + gi * POOL_GW:C + (gi + 1) * POOL_GW] = (d1 * sc_ref[:, cols] * (bg * _sigmoid(bg))).astype(BF16)
        hn_ref[...] = h_ref[...] + jnp.dot(y_ref[...], wo_ref[...], preferred_element_type=F32)

    vec = pl.BlockSpec((None, 1, C), lambda i: (sl, 0, 0))
    rows = pl.BlockSpec((tT, C), lambda i: (i, 0))
    return _pallas(
        body, name=name, grid=(T // tT,),
        in_specs=[pl.BlockSpec((tT, 5 * C), lambda i: (i, 0)),
                  pl.BlockSpec((HL, 5 * C), lambda i: (jnp.maximum(i * hb - 1, 0), 0)),
                  pl.BlockSpec((None, 32, C), lambda i: (sl, 0, 0)), vec, vec, vec,
                  pl.BlockSpec((4, POOL_GW, POOL_GW), lambda i: (0, 0, 0)), vec, vec,
                  rows, pl.BlockSpec((None, 2 * C, C), lambda i: (0, 0, 0))],
        out_specs=[pl.BlockSpec((tT, 2 * C), lambda i: (i, 0)), rows, rows],
        out_shape=[jax.ShapeDtypeStruct((T, 2 * C), BF16), jax.ShapeDtypeStruct((T, C), F32),
                   jax.ShapeDtypeStruct((T, C), F32)],
        scratch_shapes=[pltpu.VMEM((HL + tT + 8, C), F32), pltpu.VMEM((8, HL + tT, C), F32),
                        pltpu.VMEM((32, 8, C), F32)],
        compiler_params=_params("arbitrary"))(p, p, cw, cb, lg, lb, pw, pb, sc, h, w_out)


def _even_mixer_bwd(p, u1, dout, w_out, after, sl, cwr, lg, lb, pw, pb, sc, name):
    T = p.shape[0]
    C = D_MODEL
    tT, HL = MIX_TILE, EVEN_HALO
    hb = tT // HL
    nT = T // tT
    R1 = tT + HL
    chunk = 32

    def body(pm_ref, pp_ref, pn_ref, u1m_ref, u1n_ref, dom_ref, don_ref, wo_ref, after_ref, cwr_ref, lg_ref, lb_ref,
             pw_ref, pb_ref, sc_ref, dp_ref, dcw_ref, dvec_ref, dpw_ref, x_ref, sh_ref, du0_ref, wb_ref):
        i = pl.program_id(0)
        dy = _nt(jnp.concatenate([dom_ref[...], don_ref[...]], axis=0), wo_ref[...])

        @pl.when(i == 0)
        def _():
            _fill_taps(wb_ref, cwr_ref)

        keep_prev = (i > 0).astype(F32)
        keep_next = (i < nT - 1).astype(F32)
        row = lax.broadcasted_iota(jnp.int32, (R1, 1), 0)
        live = jnp.where(row < tT, 1.0, keep_next)

        def cat(m, n):
            return jnp.concatenate([m, n], axis=0)

        u1 = cat(u1m_ref[...], u1n_ref[...])
        mu = jnp.mean(u1, axis=-1, keepdims=True)
        xc = u1 - mu
        rs = lax.rsqrt(jnp.mean(xc * xc, axis=-1, keepdims=True) + EPS_LN)
        xh = xc * rs
        u2 = xh * lg_ref[...] + lb_ref[...]
        s2 = _sigmoid(u2)
        u3 = u2 * s2
        ag = cat(pm_ref[:, 2 * C:3 * C], pn_ref[:, 2 * C:3 * C])
        sa = _sigmoid(ag)
        dya = dy[:, 0:C]
        dp_ref[:, 2 * C:3 * C] = (dya * u3 * _dsilu(ag, sa))[0:tT].astype(BF16)
        du2 = dya * (ag * sa) * _dsilu(u2, s2)
        dlg = jnp.sum((du2 * xh)[0:tT], axis=0, keepdims=True)
        dlb = jnp.sum(du2[0:tT], axis=0, keepdims=True)
        dxh = du2 * lg_ref[...]
        du1 = rs * (dxh - jnp.mean(dxh, axis=-1, keepdims=True) - xh * jnp.mean(dxh * xh, axis=-1, keepdims=True))
        du1 = du1 * live
        dcb = jnp.sum(du1[0:tT], axis=0, keepdims=True)
        x_ref[0:R1] = du1
        x_ref[R1:R1 + 8] = jnp.zeros((8, C), F32)
        _fill_shifted(sh_ref, x_ref)

        def du0_chunk(c, carry):
            r0 = pl.multiple_of(c * chunk, chunk)
            du0_ref[pl.ds(r0, chunk), :] = _tap_sum(sh_ref, wb_ref, r0, chunk, list(range(CONV_K)))
            return carry

        lax.fori_loop(0, tT // chunk, du0_chunk, 0)
        av, agl = pm_ref[:, 0:C], pm_ref[:, C:2 * C]
        sg = _sigmoid(agl)
        du0 = du0_ref[...]
        dp_ref[:, 0:C] = (du0 * sg).astype(BF16)
        dp_ref[:, C:2 * C] = (du0 * av * sg * (1.0 - sg)).astype(BF16)
        du0_ref[...] = du1[0:tT]
        x_ref[0:HL] = pp_ref[:, 0:C] * _sigmoid(pp_ref[:, C:2 * C]) * keep_prev
        x_ref[HL:HL + tT] = av * sg
        x_ref[HL + tT:HL + tT + 8] = jnp.zeros((8, C), F32)
        _fill_shifted(sh_ref, x_ref)

        @pl.when(i == 0)
        def _():
            dcw_ref[...] = jnp.zeros_like(dcw_ref)

        for k0 in range(0, CONV_K, 2):
            taps = [k for k in (k0, k0 + 1) if k < CONV_K]
            offs = [HL - (CONV_K - 1) + k for k in taps]

            def dw_chunk(c, accs, offs=offs):
                r0 = pl.multiple_of(c * 64, 64)
                accs = list(accs)
                for u in range(0, 64, 8):
                    d = du0_ref[pl.ds(r0 + u, 8), :]
                    for t, o in enumerate(offs):
                        accs[t] = accs[t] + d * sh_ref[o % 8, pl.ds(r0 + u + (o // 8) * 8, 8), :]
                return tuple(accs)

            sums = lax.fori_loop(0, tT // 64, dw_chunk, tuple(jnp.zeros((8, C), F32) for _ in taps))
            for k, acc in zip(taps, sums):
                dcw_ref[8 * k:8 * k + 8, :] += acc

        bg = cat(pm_ref[:, 4 * C:5 * C], pn_ref[:, 4 * C:5 * C])
        sb = _sigmoid(bg)
        dyb = dy[:, C:2 * C]
        dyb0 = dyb * (bg * sb)
        dd1 = dyb0 * sc_ref[...]
        dpb = jnp.sum(dd1[0:tT], axis=0, keepdims=True)
        inv1 = _inv_count(i * tT, R1)
        z_parts, dd0_parts = [], []
        for gi in range(len(POOL_WINDOWS)):
            cols = slice(gi * POOL_GW, (gi + 1) * POOL_GW)
            dd0 = _nt(dd1[:, cols].astype(BF16), pw_ref[gi])
            dd0_parts.append(dd0)
            z_parts.append(dd0 * inv1[gi] * live)
        fsum = _pool_sums(jnp.concatenate(z_parts, axis=1), up=True)
        vx = cat(pp_ref[:, 3 * C:4 * C] * keep_prev, pm_ref[:, 3 * C:4 * C])
        sums = _pool_sums(vx, up=False)
        inv0 = _inv_count(i * tT, tT)
        dsc_parts = []
        for gi in range(len(POOL_WINDOWS)):
            cols = slice(gi * POOL_GW, (gi + 1) * POOL_GW)
            dp_ref[:, 3 * C + gi * POOL_GW:3 * C + (gi + 1) * POOL_GW] = (fsum[gi][0:tT] - dd0_parts[gi][0:tT]).astype(BF16)
            d0 = (sums[gi][HL:] * inv0[gi] - vx[HL:, cols]).astype(BF16)
            d1 = jnp.dot(d0, pw_ref[gi], preferred_element_type=F32) + pb_ref[:, cols]
            bgm, sbm = bg[0:tT, cols], sb[0:tT, cols]
            dp_ref[:, 4 * C + gi * POOL_GW:4 * C + (gi + 1) * POOL_GW] = (
                dyb[0:tT, cols] * d1 * sc_ref[:, cols] * _dsilu(bgm, sbm)).astype(BF16)
            dsc_parts.append(jnp.sum(dyb0[0:tT, cols] * d1, axis=0, keepdims=True))
            dpw_g = _tn(d0, dd1[0:tT, cols].astype(BF16))

            @pl.when(i == 0)
            def _(gi=gi, dpw_g=dpw_g):
                dpw_ref[gi] = dpw_g

            @pl.when(i > 0)
            def _(gi=gi, dpw_g=dpw_g):
                dpw_ref[gi] += dpw_g

        dsc = jnp.concatenate(dsc_parts, axis=1)
        vecs = jnp.concatenate([dcb, dlg, dlb, dsc, dpb, jnp.zeros((3, C), F32)], axis=0)

        @pl.when(i == 0)
        def _():
            dvec_ref[...] = vecs

        @pl.when(i > 0)
        def _():
            dvec_ref[...] += vecs

    vec = pl.BlockSpec((None, 1, C), lambda i: (sl, 0, 0))
    taps = pl.BlockSpec((None, 32, C), lambda i: (sl, 0, 0))

    def prev_blk(i):
        return (jnp.maximum(i * hb - 1, 0), 0)

    def next_blk(i):
        return (jnp.minimum((i + 1) * hb, T // HL - 1), 0)

    return _pallas(
        body, name=name, grid=(nT,),
        in_specs=[pl.BlockSpec((tT, 5 * C), lambda i: (i, 0)), pl.BlockSpec((HL, 5 * C), prev_blk),
                  pl.BlockSpec((HL, 5 * C), next_blk),
                  pl.BlockSpec((tT, C), lambda i: (i, 0)), pl.BlockSpec((HL, C), next_blk),
                  pl.BlockSpec((tT, C), lambda i: (i, 0)), pl.BlockSpec((HL, C), next_blk),
                  pl.BlockSpec((None, 2 * C, C), lambda i: (0, 0, 0)), pl.BlockSpec((8, 128), lambda i: (0, 0)),
                  taps, vec, vec, pl.BlockSpec((4, POOL_GW, POOL_GW), lambda i: (0, 0, 0)), vec, vec],
        out_specs=[pl.BlockSpec((tT, 5 * C), lambda i: (i, 0)), pl.BlockSpec((32 * 8, C), lambda i: (0, 0)),
                   pl.BlockSpec((8, C), lambda i: (0, 0)), pl.BlockSpec((4, POOL_GW, POOL_GW), lambda i: (0, 0, 0))],
        out_shape=[jax.ShapeDtypeStruct((T, 5 * C), BF16), jax.ShapeDtypeStruct((32 * 8, C), F32),
                   jax.ShapeDtypeStruct((8, C), F32), jax.ShapeDtypeStruct((4, POOL_GW, POOL_GW), F32)],
        scratch_shapes=[pltpu.VMEM((R1 + 8, C), F32), pltpu.VMEM((8, R1, C), F32), pltpu.VMEM((tT, C), F32),
                        pltpu.VMEM((32, 8, C), F32)],
        compiler_params=_params("arbitrary"))(p, p, p, u1, u1, dout, dout, w_out, after, cwr, lg, lb, pw, pb, sc)


def _softplus(z):
    u = jnp.exp(-jnp.abs(z))
    w = 1.0 + u
    l1p = jnp.where(w == 1.0, u, u * jnp.log(w) / jnp.where(w == 1.0, 1.0, w - 1.0))
    return jnp.maximum(z, 0.0) + l1p


def _lru_gates(xrx, cw_ref, cb_ref, wr_ref, br_ref, wi_ref, bi_ref, lam_ref):
    HL = ODD_HALO
    xc = cb_ref[...] + cw_ref[LRU_CONV_K - 1:LRU_CONV_K, :] * xrx[HL:]
    for k in range(LRU_CONV_K - 1):
        xc = xc + cw_ref[k:k + 1, :] * _shift_down(xrx, LRU_CONV_K - 1 - k)[HL:]
    xcb = xc.astype(BF16)
    rp, ip = [], []
    for hd in range(LRU_HEADS):
        cols = slice(hd * LRU_HD, (hd + 1) * LRU_HD)
        rp.append(jnp.dot(xcb[:, cols], wr_ref[hd], preferred_element_type=F32))
        ip.append(jnp.dot(xcb[:, cols], wi_ref[hd], preferred_element_type=F32))
    r = _sigmoid(jnp.concatenate(rp, axis=1) + br_ref[...])
    ig = _sigmoid(jnp.concatenate(ip, axis=1) + bi_ref[...])
    sp = _softplus(-lam_ref[...])
    log_a = (-LRU_C) * r * sp
    a = jnp.exp(log_a)
    m2 = jnp.maximum(-jnp.tanh(log_a) * (a * a + 1.0), 1e-30)
    inv_mult = lax.rsqrt(m2)
    return xc, xcb, r, ig, sp, a, m2 * inv_mult, inv_mult


def _group_scan(a, b, reverse):
    n, w = a.shape
    a, b = a.reshape(n // 8, 8, w), b.reshape(n // 8, 8, w)
    pos = lax.broadcasted_iota(jnp.int32, (1, 8, 1), 1)
    s = 1
    while s < 8:
        ok = (pos < 8 - s) if reverse else (pos >= s)
        shift = (8 - s) if reverse else s
        a_sh = jnp.where(ok, pltpu.roll(a, shift, 1), 1.0)
        b_sh = jnp.where(ok, pltpu.roll(b, shift, 1), 0.0)
        b = a * b_sh + b
        a = a * a_sh
        s *= 2
    return a.reshape(n, w), b.reshape(n, w)


def _apply_carries(a_ref, b_ref, out_ref, c0, reverse):
    ng = a_ref.shape[0] // 8

    def step(t, c):
        r0 = pl.multiple_of(((ng - 1 - t) if reverse else t) * 8, 8)
        x = a_ref[pl.ds(r0, 8), :] * c + b_ref[pl.ds(r0, 8), :]
        out_ref[pl.ds(r0, 8), :] = x
        return x[0:1, :] if reverse else x[7:8, :]

    return lax.fori_loop(0, ng, step, c0)


def _odd_mixer_fwd(p, h, w_out, sl, cw, cb, wr, br, wi, bi, lam, name):
    T = p.shape[0]
    W = W_LRU
    D = D_MODEL
    tT, HL = MIX_TILE, ODD_HALO
    hb = tT // HL

    def body(pm_ref, ph_ref, cw_ref, cb_ref, wr_ref, br_ref, wi_ref, bi_ref, lam_ref, h_ref, wo_ref, y_ref, hs_ref,
             hn_ref, carry_ref, sa_ref, sb_ref):
        i = pl.program_id(0)
        keep = (i > 0).astype(F32)

        @pl.when(i == 0)
        def _():
            carry_ref[...] = jnp.zeros_like(carry_ref)

        xrx = jnp.concatenate([ph_ref[:, 0:W] * keep, pm_ref[:, 0:W]], axis=0)
        xc, _, _, ig, _, a, mult, _ = _lru_gates(xrx, cw_ref, cb_ref, wr_ref, br_ref, wi_ref, bi_ref, lam_ref)
        sa_ref[...], sb_ref[...] = _group_scan(a, mult * (ig * xc), reverse=False)
        last = _apply_carries(sa_ref, sb_ref, hs_ref, carry_ref[0:1, :], reverse=False)
        carry_ref[...] = jnp.broadcast_to(last, (8, W))
        hs = hs_ref[...]
        gt = pm_ref[:, W:2 * W]
        y_ref[...] = (hs * (gt * _sigmoid(gt))).astype(BF16)
        hn_ref[...] = h_ref[...] + jnp.dot(y_ref[...], wo_ref[...], preferred_element_type=F32)

    vec = pl.BlockSpec((None, 1, W), lambda i: (sl, 0, 0))
    heads = pl.BlockSpec((None, LRU_HEADS, LRU_HD, LRU_HD), lambda i: (sl, 0, 0, 0))
    rows = pl.BlockSpec((tT, D), lambda i: (i, 0))
    wide = pl.BlockSpec((tT, W), lambda i: (i, 0))
    return _pallas(
        body, name=name, grid=(T // tT,),
        in_specs=[pl.BlockSpec((tT, 2 * W), lambda i: (i, 0)),
                  pl.BlockSpec((HL, 2 * W), lambda i: (jnp.maximum(i * hb - 1, 0), 0)),
                  pl.BlockSpec((None, 8, W), lambda i: (sl, 0, 0)), vec, heads, vec, heads, vec, vec,
                  rows, pl.BlockSpec((None, W, D), lambda i: (0, 0, 0))],
        out_specs=[wide, wide, rows],
        out_shape=[jax.ShapeDtypeStruct((T, W), BF16), jax.ShapeDtypeStruct((T, W), F32),
                   jax.ShapeDtypeStruct((T, D), F32)],
        scratch_shapes=[pltpu.VMEM((8, W), F32), pltpu.VMEM((tT, W), F32), pltpu.VMEM((tT, W), F32)],
        compiler_params=_params("arbitrary"))(p, p, cw, cb, wr, br, wi, bi, lam, h, w_out)


def _odd_mixer_bwd(p, hs, dout, w_out, after, sl, cw, cb, wr, br, wi, bi, lam, name):
    T = p.shape[0]
    W = W_LRU
    D = dout.shape[1]
    tT, HL = MIX_TILE, ODD_HALO
    hb = tT // HL
    nT = T // tT

    def body(pm_ref, ph_ref, hsm_ref, hsh_ref, do_ref, wo_ref, after_ref, cw_ref, cb_ref, wr_ref, br_ref, wi_ref,
             bi_ref, lam_ref, dp_ref, dwr_ref, dwi_ref, dvec_ref, gcarry_ref, xcarry_ref, sa_ref, sb_ref, g_ref):
        i = pl.program_id(0)
        keep = (i < nT - 1).astype(F32)

        @pl.when(i == 0)
        def _():
            gcarry_ref[...] = jnp.zeros_like(gcarry_ref)
            xcarry_ref[...] = jnp.zeros_like(xcarry_ref)

        xrx = jnp.concatenate([ph_ref[:, 0:W] * keep, pm_ref[:, 0:W]], axis=0)
        xc, xcb, r, ig, sp, a, mult, inv_mult = _lru_gates(xrx, cw_ref, cb_ref, wr_ref, br_ref, wi_ref, bi_ref, lam_ref)
        hs = hsm_ref[...]
        gt = pm_ref[:, W:2 * W]
        sg = _sigmoid(gt)
        dyv = _nt(do_ref[...], wo_ref[...])
        dp_ref[:, W:2 * W] = (dyv * hs * _dsilu(gt, sg)).astype(BF16)
        row = lax.broadcasted_iota(jnp.int32, (tT, 1), 0)
        m = jnp.where(row == tT - 1, 1.0, _shift_up(a, 1))
        sa_ref[...], sb_ref[...] = _group_scan(m, dyv * (gt * sg), reverse=True)
        first = _apply_carries(sa_ref, sb_ref, g_ref, gcarry_ref[0:1, :], reverse=True)
        G = g_ref[...]
        gcarry_ref[...] = jnp.broadcast_to(a[0:1, :] * first, (8, W))
        hs_prev = jnp.where(row == 0, hsh_ref[HL - 1:HL, :] * keep, _shift_down(hs, 1))
        da = G * hs_prev
        dmult = G * (ig * xc)
        di = G * mult * xc
        dxc = G * mult * ig
        dlog_a = da * a - dmult * (a * a) * inv_mult
        drp = dlog_a * ((-LRU_C) * sp) * r * (1.0 - r)
        dip = di * ig * (1.0 - ig)
        dlam = jnp.sum(dlog_a * ((-LRU_C) * r), axis=0, keepdims=True) * (-_sigmoid(-lam_ref[...]))
        drb, dib = drp.astype(BF16), dip.astype(BF16)
        back = []
        for hd in range(LRU_HEADS):
            cols = slice(hd * LRU_HD, (hd + 1) * LRU_HD)
            back.append(_nt(drb[:, cols], wr_ref[hd]) + _nt(dib[:, cols], wi_ref[hd]))
            dwr_h = _tn(xcb[:, cols], drb[:, cols])
            dwi_h = _tn(xcb[:, cols], dib[:, cols])

            @pl.when(i == 0)
            def _(hd=hd, dwr_h=dwr_h, dwi_h=dwi_h):
                dwr_ref[hd] = dwr_h
                dwi_ref[hd] = dwi_h

            @pl.when(i > 0)
            def _(hd=hd, dwr_h=dwr_h, dwi_h=dwi_h):
                dwr_ref[hd] += dwr_h
                dwi_ref[hd] += dwi_h

        dxc = dxc + jnp.concatenate(back, axis=1)
        dxcx = jnp.concatenate([dxc, xcarry_ref[...]], axis=0)
        dxr = cw_ref[LRU_CONV_K - 1:LRU_CONV_K, :] * dxc
        rows = []
        for k in range(LRU_CONV_K - 1):
            j = LRU_CONV_K - 1 - k
            dxr = dxr + cw_ref[k:k + 1, :] * _shift_up(dxcx, j)[0:tT]
            rows.append(jnp.sum(dxc * _shift_down(xrx, j)[HL:], axis=0, keepdims=True))
        rows.append(jnp.sum(dxc * xrx[HL:], axis=0, keepdims=True))
        dp_ref[:, 0:W] = dxr.astype(BF16)
        xcarry_ref[...] = dxc[0:8]
        rows += [jnp.sum(dxc, axis=0, keepdims=True), jnp.sum(drp, axis=0, keepdims=True),
                 jnp.sum(dip, axis=0, keepdims=True), dlam]
        vecs = jnp.concatenate(rows, axis=0)

        @pl.when(i == 0)
        def _():
            dvec_ref[...] = vecs

        @pl.when(i > 0)
        def _():
            dvec_ref[...] += vecs

    vec = pl.BlockSpec((None, 1, W), lambda i: (sl, 0, 0))
    heads = pl.BlockSpec((None, LRU_HEADS, LRU_HD, LRU_HD), lambda i: (sl, 0, 0, 0))
    dheads = pl.BlockSpec((LRU_HEADS, LRU_HD, LRU_HD), lambda i: (0, 0, 0))

    def tile(i):
        return (nT - 1 - i, 0)

    def prev_blk(i):
        return (jnp.maximum((nT - 1 - i) * hb - 1, 0), 0)

    return _pallas(
        body, name=name, grid=(nT,),
        in_specs=[pl.BlockSpec((tT, 2 * W), tile), pl.BlockSpec((HL, 2 * W), prev_blk),
                  pl.BlockSpec((tT, W), tile), pl.BlockSpec((HL, W), prev_blk), pl.BlockSpec((tT, D), tile),
                  pl.BlockSpec((None, W, D), lambda i: (0, 0, 0)), pl.BlockSpec((8, 128), lambda i: (0, 0)),
                  pl.BlockSpec((None, 8, W), lambda i: (sl, 0, 0)), vec, heads, vec, heads, vec, vec],
        out_specs=[pl.BlockSpec((tT, 2 * W), tile), dheads, dheads, pl.BlockSpec((8, W), lambda i: (0, 0))],
        out_shape=[jax.ShapeDtypeStruct((T, 2 * W), BF16), jax.ShapeDtypeStruct((LRU_HEADS, LRU_HD, LRU_HD), F32),
                   jax.ShapeDtypeStruct((LRU_HEADS, LRU_HD, LRU_HD), F32), jax.ShapeDtypeStruct((8, W), F32)],
        scratch_shapes=[pltpu.VMEM((8, W), F32), pltpu.VMEM((8, W), F32), pltpu.VMEM((tT, W), F32),
                        pltpu.VMEM((tT, W), F32), pltpu.VMEM((tT, W), F32)],
        compiler_params=_params("arbitrary"))(p, p, hs, hs, dout, w_out, after, cw, cb, wr, br, wi, bi, lam)


def _pad_rows(a, rows):
    return jnp.pad(a, ((0, 0), (0, rows - a.shape[1]), (0, 0)))


def _layer_fwd(even, h, w, w_in, w_out, after):
    sl = w["sl"]
    p, n = _in_proj(h, w["norm"], sl, w_in, 0, after, "in_proj_even" if even else "in_proj_odd")
    if callable(w_out):
        w_out = w_out(p)
    if even:
        y, aux, h_next = _even_mixer_fwd(p, h, w_out, sl, w["conv_w"], w["conv_b"], w["ln_g"], w["ln_b"], w["pool_w"],
                                         w["pool_b"], w["pool_scale"], "even_mixer_fwd")
    else:
        y, aux, h_next = _odd_mixer_fwd(p, h, w_out, sl, w["conv_w"], w["conv_b"], w["w_rg"], w["b_rg"], w["w_ig"],
                                        w["b_ig"], w["lam"], "odd_mixer_fwd")
    return h_next, (h, n, p, aux, y), w_out


def _layer_bwd_weights(even, saved, w, w_out, dhb, after):
    h, n, p, aux, y = saved
    if even:
        dp, dcw, dvec, dpw = _even_mixer_bwd(p, aux, dhb, w_out, after, w["sl"], w["conv_w_rev"], w["ln_g"], w["ln_b"],
                                             w["pool_w"], w["pool_b"], w["pool_scale"], "even_mixer_bwd")
        dw_out = _dw_out(y, dhb, 0, 1, None, "dw_out_even")
        dw_in = _dw_in(n, dp, N_CHIPS, 0, 1, None, "dw_in_even")
        return dp, dw_in, dw_out, dict(conv_w=dcw, vec=dvec, pool_w=dpw)
    dp, dwr, dwi, dvec = _odd_mixer_bwd(p, aux, dhb, w_out, after, w["sl"], w["conv_w"], w["conv_b"], w["w_rg"],
                                        w["b_rg"], w["w_ig"], w["b_ig"], w["lam"], "odd_mixer_bwd")
    dw_out = _dw_out(y, dhb, 0, 1, None, "dw_out_odd")
    dw_in = _dw_in(n, dp, N_CHIPS, 0, 1, None, "dw_in_odd")
    return dp, dw_in, dw_out, dict(w_rg=dwr, w_ig=dwi, vec=dvec)


def _layer_bwd_input(even, saved, w, w_in, dp, dh, after):
    return _dn_proj(dp, w_in, 0, saved[0], w["norm"], w["sl"], dh, after, "dn_proj_even" if even else "dn_proj_odd")


ANY = pl.BlockSpec(memory_space=pl.ANY)


def _mesh_pos():
    return lax.axis_index("x"), lax.axis_index("y"), lax.axis_index("c")


def _other_chips(x, y):
    return [(1 - x, y), (x, 1 - y), (1 - x, 1 - y)]


def _remote(src, dst, ssem, rsem, dev):
    return pltpu.make_async_remote_copy(src_ref=src, dst_ref=dst, send_sem=ssem, recv_sem=rsem, device_id=dev,
                                        device_id_type=MESH)


def _comm_call(body, name, ins, out_shape, scratch, aliases=None):
    return _pallas(body, name=name, in_specs=[ANY] * len(ins), out_specs=[ANY] * len(out_shape), out_shape=out_shape,
                   scratch_shapes=scratch, input_output_aliases=aliases or {},
                   compiler_params=pltpu.CompilerParams(has_side_effects=True))(*ins)


def _cast_shard(w, layer, pos):
    _, R, C = w.shape
    tr = _row_tile(R, C)

    def body(pos_ref, w_ref, o_ref):
        o_ref[...] = w_ref[...].astype(BF16)

    grid_spec = pltpu.PrefetchScalarGridSpec(
        num_scalar_prefetch=1, grid=(R // tr,),
        in_specs=[pl.BlockSpec((None, tr, C), lambda i, pr: (layer, i, 0))],
        out_specs=pl.BlockSpec((None, None, tr, C), lambda i, pr: (0, pr[0], i, 0)))
    return _pallas(body, name="cast_shard", grid_spec=grid_spec,
                   out_shape=jax.ShapeDtypeStruct((1, N_CHIPS, R, C), BF16),
                   compiler_params=_params("parallel"))(pos, w)


def _gather_weights(big, small):
    nA = len(big)
    half = [a.shape[2] // 2 for a in big]

    def body(*refs):
        ins, outs = refs[:nA + 1], refs[nA + 1:2 * nA + 2]
        ssem, rsem, fsem, frsem, lsem = refs[2 * nA + 2:]
        x, y, c = _mesh_pos()
        k = 2 * x + y
        chips = _other_chips(x, y)
        sib = (x, y, 1 - c)

        def slab(a, chip, core):
            return outs[a].at[:, chip, pl.ds(core * half[a], half[a]), :]

        local = [pltpu.make_async_copy(ins[nA], outs[nA].at[k], lsem.at[0])]
        for cp in local:
            cp.start()
        sends = []
        for j, (ox, oy) in enumerate(chips):
            for a in range(nA):
                sends.append(_remote(slab(a, k, c), slab(a, k, c), ssem.at[a, j], rsem.at[a, j], (ox, oy, c)))
            sends.append(_remote(ins[nA], outs[nA].at[k], ssem.at[nA, j], rsem.at[nA, j], (ox, oy, c)))
        for cp in sends:
            cp.start()
        for j, (ox, oy) in enumerate(chips):
            kj = 2 * ox + oy
            for a in range(nA):
                got = slab(a, kj, c)
                _remote(got, got, ssem.at[a, j], rsem.at[a, j], (ox, oy, c)).wait_recv()
                fw = _remote(got, got, fsem.at[a, j], frsem.at[a, j], sib)
                fw.start()
                sends.append(fw)
            gs = outs[nA].at[kj]
            _remote(gs, gs, ssem.at[nA, j], rsem.at[nA, j], (ox, oy, c)).wait_recv()
        for j, (ox, oy) in enumerate(chips):
            kj = 2 * ox + oy
            for a in range(nA):
                theirs = slab(a, kj, 1 - c)
                _remote(theirs, theirs, fsem.at[a, j], frsem.at[a, j], sib).wait_recv()
        for cp in sends:
            cp.wait_send()
        for cp in local:
            cp.wait()

    out_shape = [jax.ShapeDtypeStruct(a.shape, a.dtype) for a in big]
    out_shape.append(jax.ShapeDtypeStruct((N_CHIPS,) + small.shape, small.dtype))
    scratch = [pltpu.SemaphoreType.DMA((nA + 1, 3)), pltpu.SemaphoreType.DMA((nA + 1, 3)),
               pltpu.SemaphoreType.DMA((nA, 3)), pltpu.SemaphoreType.DMA((nA, 3)), pltpu.SemaphoreType.DMA((1,))]
    return _comm_call(body, "gather_weights", list(big) + [small], out_shape, scratch, {a: a for a in range(nA)})


HBM = pl.BlockSpec(memory_space=pltpu.HBM)
SEM = pl.BlockSpec(memory_space=pltpu.SEMAPHORE)
EFFECT = pltpu.SideEffectType.DATAFLOW_SIDE_EFFECTING


def _split_start(arrays, copies, n, name):
    k = len(arrays)

    def body(*refs):
        for cp in copies(refs[k + 2:2 * k + 2], refs[k], refs[k + 1]):
            cp.start()
        refs[2 * k + 2][...] = jnp.zeros((8, 128), F32)

    out = _pallas(
        body, name=name,
        out_shape=(pltpu.SemaphoreType.DMA((n,)), pltpu.SemaphoreType.DMA((n,)),
                   *[pltpu.HBM(a.shape, a.dtype) for a in arrays], jax.ShapeDtypeStruct((8, 128), F32)),
        in_specs=(HBM,) * k, out_specs=(SEM, SEM) + (HBM,) * k + (pl.BlockSpec(memory_space=pltpu.VMEM),),
        input_output_aliases={i: i + 2 for i in range(k)},
        compiler_params=pltpu.CompilerParams(has_side_effects=EFFECT),
    )(*[pltpu.with_memory_space_constraint(a, pltpu.HBM) for a in arrays])
    return out[0], out[1], list(out[2:2 + k]), out[2 + k]


def _split_wait(ssem, rsem, arrays, copies, after, name):
    k = len(arrays)

    def body(*refs):
        for cp in copies(refs[:k], refs[k], refs[k + 1]):
            cp.wait_send()
            cp.wait_recv()

    out = _pallas(
        body, name=name, out_shape=tuple(pltpu.HBM(a.shape, a.dtype) for a in arrays),
        in_specs=(HBM,) * k + (SEM, SEM, ANY), out_specs=(HBM,) * k, input_output_aliases={i: i for i in range(k)},
        compiler_params=pltpu.CompilerParams(has_side_effects=EFFECT),
    )(*arrays, ssem, rsem, after)
    return list(out)


def _gather_copies(shapes):
    half = [s[2] // 2 for s in shapes]

    def copies(refs, ssem, rsem):
        x, y, c = _mesh_pos()
        out = []
        for j, (ox, oy) in enumerate(_other_chips(x, y)):
            for a, ref in enumerate(refs):
                slab = ref.at[:, 2 * x + y, pl.ds(c * half[a], half[a]), :]
                out.append(_remote(slab, slab, ssem.at[3 * a + j], rsem.at[3 * a + j], (ox, oy, c)))
        return out

    return copies


def _chips_copies(n_arr):
    def copies(refs, ssem, rsem):
        x, y, c = _mesh_pos()
        out = []
        for j, (ox, oy) in enumerate(_other_chips(x, y)):
            for a in range(n_arr):
                out.append(_remote(refs[a].at[:, 2 * ox + oy], refs[n_arr + a].at[:, 2 * x + y], ssem.at[3 * a + j],
                                   rsem.at[3 * a + j], (ox, oy, c)))
        return out

    return copies


def _halves_copies(shapes):
    n = len(shapes)
    half = [s[2] // 2 for s in shapes]

    def copies(refs, ssem, rsem):
        x, y, c = _mesh_pos()
        return [_remote(refs[a].at[:, :, pl.ds((1 - c) * half[a], half[a]), :], refs[n + a], ssem.at[a], rsem.at[a],
                        (x, y, 1 - c)) for a in range(n)]

    return copies


def _forward_cores(arrays):
    nA = len(arrays)
    half = [a.shape[2] // 2 for a in arrays]

    def body(*refs):
        outs = refs[nA:2 * nA]
        ssem, rsem = refs[2 * nA:]
        x, y, c = _mesh_pos()
        sib = (x, y, 1 - c)
        sends, waits = [], []
        for j, (ox, oy) in enumerate(_other_chips(x, y)):
            for a in range(nA):
                got = outs[a].at[:, 2 * ox + oy, pl.ds(c * half[a], half[a]), :]
                sends.append(_remote(got, got, ssem.at[a, j], rsem.at[a, j], sib))
                theirs = outs[a].at[:, 2 * ox + oy, pl.ds((1 - c) * half[a], half[a]), :]
                waits.append(_remote(theirs, theirs, ssem.at[a, j], rsem.at[a, j], sib))
        for cp in sends:
            cp.start()
        for cp in waits:
            cp.wait_recv()
        for cp in sends:
            cp.wait_send()

    out_shape = [jax.ShapeDtypeStruct(a.shape, a.dtype) for a in arrays]
    scratch = [pltpu.SemaphoreType.DMA((nA, 3)), pltpu.SemaphoreType.DMA((nA, 3))]
    return _comm_call(body, "forward_cores", list(arrays), out_shape, scratch, {a: a for a in range(nA)})


def _exchange_halves(big):
    nA = len(big)
    half = [a.shape[2] // 2 for a in big]

    def body(*refs):
        ins, outs = refs[:nA], refs[nA:2 * nA]
        ssem, rsem = refs[2 * nA:]
        x, y, c = _mesh_pos()
        sib = (x, y, 1 - c)
        sends = [_remote(ins[a].at[:, :, pl.ds((1 - c) * half[a], half[a]), :], outs[a], ssem.at[a], rsem.at[a], sib)
                 for a in range(nA)]
        for cp in sends:
            cp.start()
        for a in range(nA):
            _remote(outs[a], outs[a], ssem.at[a], rsem.at[a], sib).wait_recv()
        for cp in sends:
            cp.wait_send()

    out_shape = [jax.ShapeDtypeStruct((a.shape[0], N_CHIPS, h, a.shape[3]), a.dtype) for a, h in zip(big, half)]
    scratch = [pltpu.SemaphoreType.DMA((nA,)), pltpu.SemaphoreType.DMA((nA,))]
    return _comm_call(body, "exchange_halves", list(big), out_shape, scratch)


def _exchange_final(grads, everywhere, small):
    nA = len(grads)
    n_remote = sum(7 if ev else 1 for ev in everywhere) + 7

    def body(*refs):
        small_ref, outs, gathered = refs[nA], refs[nA + 1:2 * nA + 1], refs[2 * nA + 1]
        ssem, rsem, lsem = refs[2 * nA + 2:]
        x, y, c = _mesh_pos()
        k = 2 * x + y
        sib = (x, y, 1 - c)
        local = pltpu.make_async_copy(small_ref, gathered.at[2 * k + c], lsem.at[0])
        local.start()
        sends, arrivals, waits = [], [], []
        count = [0]

        def sems():
            count[0] += 1
            return ssem.at[count[0] - 1], rsem.at[count[0] - 1]

        def to_sibling(src, mine, theirs):
            sm = sems()
            sends.append(_remote(src, mine, *sm, sib))
            waits.append(_remote(theirs, theirs, *sm, sib))

        def to_everyone(src, place):
            to_sibling(src, place(k, c), place(k, 1 - c))
            for (ox, oy) in _other_chips(x, y):
                ici, d2d = sems(), sems()
                got = place(2 * ox + oy, c)
                sends.append(_remote(src, place(k, c), *ici, (ox, oy, c)))
                arrivals.append((_remote(got, got, *ici, (ox, oy, c)), _remote(got, got, *d2d, sib)))
                theirs = place(2 * ox + oy, 1 - c)
                waits.append(_remote(theirs, theirs, *d2d, sib))

        to_everyone(small_ref, lambda chip, core: gathered.at[2 * chip + core])
        for a in range(nA):
            if everywhere[a]:
                r2 = grads[a].shape[1] // N_DEV

                def place(chip, core, a=a, r2=r2):
                    return outs[a].at[:, pl.ds((2 * chip + core) * r2, r2), :]

                to_everyone(place(k, c), place)
            else:
                r2 = grads[a].shape[1] // 2
                mine = outs[a].at[:, pl.ds(c * r2, r2), :]
                to_sibling(mine, mine, outs[a].at[:, pl.ds((1 - c) * r2, r2), :])
        for cp in sends:
            cp.start()
        for arrived, onward in arrivals:
            arrived.wait_recv()
            onward.start()
        for cp in waits:
            cp.wait_recv()
        for cp in sends + [onward for _, onward in arrivals]:
            cp.wait_send()
        local.wait()

    out_shape = [jax.ShapeDtypeStruct(g.shape, g.dtype) for g in grads]
    out_shape.append(jax.ShapeDtypeStruct((N_DEV,) + small.shape, small.dtype))
    scratch = [pltpu.SemaphoreType.DMA((n_remote,)), pltpu.SemaphoreType.DMA((n_remote,)), pltpu.SemaphoreType.DMA((1,))]
    return _comm_call(body, "exchange_final", list(grads) + [small], out_shape, scratch, {a: a for a in range(nA)})


BLOCK_BYTES = 4 << 20


def _row_tile(rows, cols, mult=16, limit=BLOCK_BYTES):
    best = mult
    for t in range(mult, rows + 1, mult):
        if rows % t == 0 and t * cols * 4 <= limit:
            best = t
    return best


def _add_cores(own, recv, pos):
    L, _, R, C = own.shape
    r2 = R // 2
    tr = _row_tile(r2, C)
    nb = r2 // tr

    def body(pos_ref, a_ref, r_ref, o_ref):
        o_ref[...] = (a_ref[...].astype(F32) + r_ref[...].astype(F32)).astype(BF16)

    blk = (None, None, tr, C)
    grid_spec = pltpu.PrefetchScalarGridSpec(
        num_scalar_prefetch=1, grid=(L, N_CHIPS, nb),
        in_specs=[pl.BlockSpec(blk, lambda l, s, i, pr: (l, s, pr[1] * nb + i, 0)),
                  pl.BlockSpec(blk, lambda l, s, i, pr: (l, s, i, 0))],
        out_specs=pl.BlockSpec(blk, lambda l, s, i, pr: (l, s, i, 0)))
    return _pallas(body, name="add_cores", grid_spec=grid_spec,
                   out_shape=jax.ShapeDtypeStruct((L, N_CHIPS, r2, C), BF16),
                   compiler_params=_params("parallel", "parallel", "parallel"))(pos, own, recv)


def _sum_chips(own, recv, pos, everywhere, layer, nlayers, prev):
    _, _, r2, C = own.shape
    tr = _row_tile(r2, 2 * C)
    nb = r2 // tr

    def body(pos_ref, a_ref, r_ref, *rest):
        acc = None
        for s in range(N_CHIPS):
            term = jnp.where(pos_ref[0] == s, a_ref[...], r_ref[s]).astype(F32)
            acc = term if acc is None else acc + term
        rest[-1][...] = acc

    if everywhere:
        def out_map(i, pr):
            return (layer, (2 * pr[0] + pr[1]) * nb + i, 0)
    else:
        def out_map(i, pr):
            return (layer, pr[1] * nb + i, 0)

    in_specs = [pl.BlockSpec((None, None, tr, C), lambda i, pr: (0, pr[0], i, 0)),
                pl.BlockSpec((None, N_CHIPS, tr, C), lambda i, pr: (0, 0, i, 0))]
    grid_spec = pltpu.PrefetchScalarGridSpec(
        num_scalar_prefetch=1, grid=(nb,), in_specs=in_specs + ([] if prev is None else [ANY]),
        out_specs=pl.BlockSpec((None, tr, C), out_map))
    rows = (N_DEV if everywhere else 2) * r2
    args = (pos, own, recv) if prev is None else (pos, own, recv, prev)
    return _pallas(body, name="sum_chips", grid_spec=grid_spec, out_shape=jax.ShapeDtypeStruct((nlayers, rows, C), F32),
                   input_output_aliases={} if prev is None else {3: 0},
                   compiler_params=_params("parallel"))(*args)


def _sum_devices(parts):
    n, R, C = parts.shape
    tr = _row_tile(R, C * n, 8)

    def body(p_ref, o_ref):
        acc = p_ref[0]
        for s in range(1, n):
            acc = acc + p_ref[s]
        o_ref[...] = acc

    return _pallas(body, name="sum_devices", grid=(R // tr,), in_specs=[pl.BlockSpec((n, tr, C), lambda i: (0, i, 0))],
                   out_specs=pl.BlockSpec((tr, C), lambda i: (i, 0)), out_shape=jax.ShapeDtypeStruct((R, C), F32),
                   compiler_params=_params("parallel"))(parts)


def _adamw(w, g, m, v, name):
    L, R, C = w.shape
    tr = _row_tile(R, C, 8, BLOCK_BYTES // 2)

    def body(w_ref, g_ref, m_ref, v_ref, d_ref, m2_ref, v2_ref):
        gg = g_ref[...]
        m2 = ADAM_B1 * m_ref[...] + (1.0 - ADAM_B1) * gg
        v2 = ADAM_B2 * v_ref[...] + (1.0 - ADAM_B2) * (gg * gg)
        m_hat = m2 / (1.0 - ADAM_B1 ** ADAM_STEP)
        v_hat = v2 / (1.0 - ADAM_B2 ** ADAM_STEP)
        d_ref[...] = -ADAM_LR * (m_hat / (jnp.sqrt(v_hat) + ADAM_EPS) + ADAM_WD * w_ref[...])
        m2_ref[...] = m2
        v2_ref[...] = v2

    blk = pl.BlockSpec((1, tr, C), lambda l, i: (l, i, 0))
    shp = jax.ShapeDtypeStruct((L, R, C), F32)
    return _pallas(body, name=name, grid=(L, R // tr), in_specs=[blk] * 4, out_specs=[blk] * 3, out_shape=[shp] * 3,
                   compiler_params=_params("parallel", "parallel"))(w, g, m, v)


WEIGHTS = ("norm_even", "w_in_even", "conv_a_w", "conv_a_b", "ln_a_g", "ln_a_b", "pool_w", "pool_b", "pool_scale",
           "w_out_even", "norm_odd", "w_in_odd", "conv_c_w", "conv_c_b", "w_rg", "b_rg", "w_ig", "b_ig", "lru_lambda",
           "w_out_odd", "final_norm")
BIG = ("w_in_even", "w_out_even", "pool_w", "w_in_odd", "w_out_odd", "w_rg", "w_ig")
SMALL = tuple(n for n in WEIGHTS if n not in BIG)
SMALL_SHARDED = ("conv_a_w", "pool_b", "norm_odd", "conv_c_w", "conv_c_b", "b_rg", "b_ig", "lru_lambda")


def _pack(arrs):
    flat = jnp.concatenate([a.reshape(-1) for a in arrs])
    rows = -(-flat.shape[0] // (64 * 128)) * 64
    return jnp.pad(flat, (0, rows * 128 - flat.shape[0])).reshape(rows, 128)


def _unpack(buf, shapes, lead=()):
    flat = buf.reshape(tuple(lead) + (-1,))
    out, o = [], 0
    for s in shapes:
        n = 1
        for d in s:
            n *= d
        out.append(flat[..., o:o + n].reshape(tuple(lead) + tuple(s)))
        o += n
    return out


def _shard(full, axis, k):
    n = full.shape[axis] // N_CHIPS
    return lax.dynamic_slice_in_dim(full, k * n, n, axis)


def kernel(x, norm_even, w_in_even, conv_a_w, conv_a_b, ln_a_g, ln_a_b, pool_w, pool_b, pool_scale, w_out_even, norm_odd, w_in_odd, conv_c_w, conv_c_b, w_rg, b_rg, w_ig, b_ig, lru_lambda, w_out_odd, final_norm, loss_target, m_norm_even, m_w_in_even, m_conv_a_w, m_conv_a_b, m_ln_a_g, m_ln_a_b, m_pool_w, m_pool_b, m_pool_scale, m_w_out_even, m_norm_odd, m_w_in_odd, m_conv_c_w, m_conv_c_b, m_w_rg, m_b_rg, m_w_ig, m_b_ig, m_lru_lambda, m_w_out_odd, m_final_norm, v_norm_even, v_w_in_even, v_conv_a_w, v_conv_a_b, v_ln_a_g, v_ln_a_b, v_pool_w, v_pool_b, v_pool_scale, v_w_out_even, v_norm_odd, v_w_in_odd, v_conv_c_w, v_conv_c_b, v_w_rg, v_b_rg, v_w_ig, v_b_ig, v_lru_lambda, v_w_out_odd, v_final_norm):
    P = dict(locals())
    xi, yi, ci = _mesh_pos()
    k = 2 * xi + yi
    L = w_in_even.shape[0]
    D = D_MODEL

    pos = jnp.stack([k, ci]).astype(jnp.int32)
    depth = 2 * L
    pool_w3 = pool_w.reshape(L, 4 * 64, POOL_GW)

    def cast_group(layer):
        j = layer // 2
        if layer % 2 == 0:
            return [_cast_shard(w_in_even, j, pos), _cast_shard(w_out_even, j, pos), _cast_shard(pool_w3, j, pos)]
        return [_cast_shard(w_in_odd, j, pos), _cast_shard(w_out_odd, j, pos)]

    first = cast_group(0)
    g_in, g_pool, g_small = _gather_weights([first[0], first[2]], _pack([P[n] for n in SMALL_SHARDED]))
    copies0 = _gather_copies([first[1].shape])
    ssem0, rsem0, late0, token0 = _split_start([first[1]], copies0, 3, "gather_start0")
    group = [g_in, None, g_pool]
    full = {}
    for n, a in zip(SMALL_SHARDED, _unpack(g_small, [P[n].shape for n in SMALL_SHARDED], lead=(N_CHIPS,))):
        a = jnp.moveaxis(a, 0, -2)
        full[n] = a.reshape(a.shape[:-2] + (N_CHIPS * a.shape[-1],))

    small_even = dict(norm=norm_even[:, None], conv_w=_pad_rows(full["conv_a_w"], 32),
                      conv_w_rev=_pad_rows(full["conv_a_w"][:, ::-1], 32), conv_b=conv_a_b[:, None], ln_g=ln_a_g[:, None],
                      ln_b=ln_a_b[:, None], pool_b=full["pool_b"].reshape(L, 1, D), pool_scale=pool_scale[:, None])
    small_odd = dict(norm=full["norm_odd"][:, None], conv_w=_pad_rows(full["conv_c_w"], 8),
                     conv_b=full["conv_c_b"][:, None], w_rg=w_rg.astype(BF16), b_rg=full["b_rg"][:, None],
                     w_ig=w_ig.astype(BF16), b_ig=full["b_ig"][:, None], lam=full["lru_lambda"][:, None])

    def small_weights(layer, group):
        if layer % 2 == 0:
            pw = group[2].reshape(N_CHIPS, 4, 64, POOL_GW).transpose(1, 0, 2, 3).reshape(4, POOL_GW, POOL_GW)
            return dict(small_even, sl=layer // 2, pool_w=pw)
        return dict(small_odd, sl=layer // 2)

    no_token = jnp.zeros((8, 128), F32)
    h = x[0]
    saved, big_w, small_w = [], [], []
    for layer in range(depth):
        started = []

        def start_next(layer=layer, started=started):
            nxt = cast_group(layer + 1)
            copies = _gather_copies([a.shape for a in nxt])
            started.append((copies,) + _split_start(nxt, copies, 3 * len(nxt), "gather_start%d" % (layer + 1)))

        if layer == 0:
            def w_out(p):
                arrived = _forward_cores(_split_wait(ssem0, rsem0, late0, copies0, p, "gather_wait0"))
                start_next()
                return arrived[0].reshape(1, -1, D)

            token = token0
        else:
            token = no_token
            if layer + 1 < depth:
                start_next()
                token = started[0][4]
            w_out = group[1].reshape(1, -1, D)
        small_w.append(small_weights(layer, group))
        h, sv, w_out = _layer_fwd(layer % 2 == 0, h, small_w[layer], group[0], w_out, token)
        big_w.append((group[0], w_out))
        saved.append(sv)
        if layer + 1 < depth:
            copies, ssem, rsem, nxt, _ = started[0]
            group = _forward_cores(_split_wait(ssem, rsem, nxt, copies, h, "gather_wait%d" % (layer + 1)))

    dh, dhb, d_final, loss = _loss_head(h, final_norm[None], loss_target[0])
    loss = lax.psum(loss[0, 0], ("x", "y", "c"))
    everywhere = [False, False, False, False, False, True, True]
    final = [None] * len(everywhere)
    small_of = [None] * depth

    def finish(pending, after):
        ssem, rsem, arrs, copies, slots, pj, pl_ = pending
        arrs = _split_wait(ssem, rsem, arrs, copies, after, "chips_wait%d" % pl_)
        for a, r, s in zip(arrs[:len(slots)], arrs[len(slots):], slots):
            final[s] = _sum_chips(a, r, pos, everywhere[s], pj, L, final[s])

    pending = None
    token = no_token
    for layer in reversed(range(depth)):
        j = layer // 2
        even_layer = layer % 2 == 0
        dp, dw_in, dw_out, sm = _layer_bwd_weights(even_layer, saved[layer], small_w[layer], big_w[layer][1], dhb, token)
        if even_layer:
            dpw = sm["pool_w"].reshape(4, N_CHIPS, 64, POOL_GW).transpose(1, 0, 2, 3)
            parts = [dw_in, dw_out.reshape(1, N_CHIPS, -1, D), dpw.reshape(1, N_CHIPS, 4 * 64, POOL_GW).astype(BF16)]
            slots = [0, 1, 2]
        else:
            parts = [dw_in, dw_out.reshape(1, N_CHIPS, -1, D),
                     sm["w_rg"].reshape(1, N_CHIPS, -1, LRU_HD).astype(BF16),
                     sm["w_ig"].reshape(1, N_CHIPS, -1, LRU_HD).astype(BF16)]
            slots = [3, 4, 5, 6]
        n = len(parts)
        if layer > 0:
            hcopies = _halves_copies([a.shape for a in parts])
            hland = [lax.empty((1, N_CHIPS, a.shape[2] // 2, a.shape[3]), a.dtype) for a in parts]
            hs, hr, harrs, htoken = _split_start(parts + hland, hcopies, n, "halves_start%d" % layer)
            dh, dhb, sm["norm"] = _layer_bwd_input(even_layer, saved[layer], small_w[layer], big_w[layer][0], dp, dh,
                                                   htoken)
            harrs = _split_wait(hs, hr, harrs, hcopies, dh, "halves_wait%d" % layer)
            parts, recv = harrs[:n], harrs[n:]
        else:
            recv = _exchange_halves(parts)
        pair = [_add_cores(a, r, pos) for a, r in zip(parts, recv)]
        copies = _chips_copies(n)
        land = [lax.empty(a.shape, a.dtype) for a in pair]
        ssem, rsem, arrs, token = _split_start(pair + land, copies, 3 * n, "chips_start%d" % layer)
        if layer == 0:
            dh, dhb, sm["norm"] = _layer_bwd_input(even_layer, saved[layer], small_w[layer], big_w[layer][0], dp, dh, token)
        small_of[layer] = sm
        if pending is not None:
            finish(pending, dh)
        pending = (ssem, rsem, arrs, copies, slots, j, layer)
    grad_x = dh
    small_g = []
    for jj in range(L):
        ge, go = small_of[2 * jj], small_of[2 * jj + 1]
        small_g += [ge["conv_w"].reshape(32, 8, D).sum(axis=1)[:CONV_K], ge["vec"][0:5], ge["norm"], go["vec"], go["norm"]]
    small_g.append(d_final)
    small_shapes = [a.shape for a in small_g]
    packed_small = _pack(small_g)
    finish(pending, packed_small)
    *gw, recv_small = _exchange_final(final, everywhere, packed_small)
    sg = _unpack(_sum_devices(recv_small), small_shapes)

    grads = dict(w_in_even=gw[0], w_out_even=gw[1], pool_w=gw[2].reshape(pool_w.shape), w_in_odd=gw[3], w_out_odd=gw[4],
                 w_rg=gw[5].reshape(w_rg.shape), w_ig=gw[6].reshape(w_ig.shape), final_norm=sg[-1][0])
    ev = [sg[5 * j + 1] for j in range(L)]
    ov = [sg[5 * j + 3] for j in range(L)]
    grads["conv_a_w"] = _shard(jnp.stack([sg[5 * j] for j in range(L)]), 2, k)
    grads["norm_even"] = jnp.stack([sg[5 * j + 2][0] for j in range(L)])
    grads["norm_odd"] = _shard(jnp.stack([sg[5 * j + 4][0] for j in range(L)]), 1, k)
    for r, n in enumerate(("conv_a_b", "ln_a_g", "ln_a_b", "pool_scale")):
        grads[n] = jnp.stack([e[r] for e in ev])
    grads["pool_b"] = _shard(jnp.stack([e[4].reshape(4, POOL_GW) for e in ev]), 2, k)
    grads["conv_c_w"] = _shard(jnp.stack([o[0:4] for o in ov]), 2, k)
    for r, n in zip((4, 5, 6, 7), ("conv_c_b", "b_rg", "b_ig", "lru_lambda")):
        grads[n] = _shard(jnp.stack([o[r] for o in ov]), 1, k)

    delta, new_m, new_v = {}, {}, {}
    for n in BIG:
        s3 = (L, -1, P[n].shape[-1])
        d, m2, v2 = _adamw(P[n].reshape(s3), grads[n].reshape(s3), P["m_" + n].reshape(s3), P["v_" + n].reshape(s3), "adamw")
        delta[n], new_m[n], new_v[n] = d.reshape(P[n].shape), m2.reshape(P[n].shape), v2.reshape(P[n].shape)
    shapes = [P[n].shape for n in SMALL]
    packed = [_pack([src[n] for n in SMALL])[None] for src in
              (P, grads, {n: P["m_" + n] for n in SMALL}, {n: P["v_" + n] for n in SMALL})]
    for res, out in zip(_adamw(*packed, "adamw_small"), (delta, new_m, new_v)):
        for n, a in zip(SMALL, _unpack(res[0], shapes)):
            out[n] = a

    return (loss, grad_x[None], *[grads[n] for n in WEIGHTS], *[delta[n] for n in WEIGHTS],
            *[new_m[n] for n in WEIGHTS], *[new_v[n] for n in WEIGHTS])
```

```python
import jax
import jax.numpy as jnp
from jax import lax
from jax.experimental import pallas as pl
from jax.experimental.pallas import tpu as pltpu

F32 = jnp.float32
BF16 = jnp.bfloat16
MESH = pl.DeviceIdType.MESH

D_MODEL = 1024
N_CHIPS = 4
N_DEV = 8
EPS_RMS = 1e-6
EPS_LN = 1e-5
CONV_K = 31
POOL_WINDOWS = (2, 4, 8, 16)
POOL_GW = 256
LRU_HEADS = 12
LRU_HD = 128
W_LRU = LRU_HEADS * LRU_HD
LRU_CONV_K = 4
LRU_C = 8.0
ADAM_LR = 0.001
ADAM_B1 = 0.9
ADAM_B2 = 0.999
ADAM_EPS = 1e-08
ADAM_WD = 0.01
ADAM_STEP = 10

VMEM_LIMIT_BYTES = 56 * 1024 * 1024
ROW_TILE = 512
MIX_TILE = 256
EVEN_HALO = 32
ODD_HALO = 8


def _pallas(body, **kw):
    return pl.pallas_call(body, **kw)


def _params(*sem):
    return pltpu.CompilerParams(dimension_semantics=sem if sem else None, vmem_limit_bytes=VMEM_LIMIT_BYTES)


def _sigmoid(x):
    return 0.5 * jnp.tanh(0.5 * x) + 0.5


def _dsilu(x, s):
    return s * (1.0 + x * (1.0 - s))


def _nt(a, b):
    return lax.dot_general(a, b, (((1,), (1,)), ((), ())), preferred_element_type=F32)


def _tn(a, b):
    return lax.dot_general(a, b, (((0,), (0,)), ((), ())), preferred_element_type=F32)


def _in_proj(h, g, glayer, wg, layer, after, name):
    T, D = h.shape
    _, nblk, _, nb = wg.shape

    nrow = T // ROW_TILE

    def body(h_ref, g_ref, w_ref, after_ref, p_ref, n_ref, n_all):
        j, i = pl.program_id(0), pl.program_id(1)

        @pl.when(j == 0)
        def _():
            x = h_ref[...]
            r = lax.rsqrt(jnp.mean(x * x, axis=-1, keepdims=True) + EPS_RMS)
            nn = (x * r * g_ref[...]).astype(BF16)
            n_ref[...] = nn
            n_all[i] = nn

        p_ref[...] = jnp.dot(n_all[i], w_ref[0], preferred_element_type=F32)

    def rows_once(j, i):
        return (jnp.where(j == 0, i, nrow - 1), 0)

    return _pallas(
        body, name=name, grid=(nblk, nrow),
        in_specs=[pl.BlockSpec((ROW_TILE, D), rows_once), pl.BlockSpec((None, 1, D), lambda j, i: (glayer, 0, 0)),
                  pl.BlockSpec((None, 1, D, nb), lambda j, i: (layer, j, 0, 0)),
                  pl.BlockSpec((8, 128), lambda j, i: (0, 0))],
        out_specs=[pl.BlockSpec((ROW_TILE, nb), lambda j, i: (i, j)), pl.BlockSpec((ROW_TILE, D), rows_once)],
        out_shape=[jax.ShapeDtypeStruct((T, nblk * nb), F32), jax.ShapeDtypeStruct((T, D), BF16)],
        scratch_shapes=[pltpu.VMEM((nrow, ROW_TILE, D), BF16)],
        compiler_params=_params("arbitrary", "arbitrary"))(h, g, wg, after)


def _dn_proj(dp, wg, layer, h, g, glayer, dres, after, name):
    T, D = h.shape
    _, nblk, _, nb = wg.shape

    nrow = T // ROW_TILE

    def body(dp_ref, w_ref, h_ref, g_ref, dres_ref, after_ref, dh_ref, dhb_ref, dg_ref, acc_ref):
        j, i = pl.program_id(0), pl.program_id(1)
        part = _nt(dp_ref[...], w_ref[0])

        @pl.when(j == 0)
        def _():
            acc_ref[i] = part

        @pl.when(j > 0)
        def _():
            acc_ref[i] += part

        @pl.when(j == nblk - 1)
        def _():
            x = h_ref[...]
            r = lax.rsqrt(jnp.mean(x * x, axis=-1, keepdims=True) + EPS_RMS)
            dn = acc_ref[i]
            q = dn * g_ref[...]
            dh = dres_ref[...] + r * q - x * ((r * r * r) * jnp.mean(q * x, axis=-1, keepdims=True))
            dh_ref[...] = dh
            dhb_ref[...] = dh.astype(BF16)
            dgp = jnp.sum(dn * (x * r), axis=0, keepdims=True)

            @pl.when(i == 0)
            def _():
                dg_ref[...] = dgp

            @pl.when(i > 0)
            def _():
                dg_ref[...] += dgp

    def rows_last(j, i):
        return (jnp.where(j == nblk - 1, i, 0), 0)

    return _pallas(
        body, name=name, grid=(nblk, nrow),
        in_specs=[pl.BlockSpec((ROW_TILE, nb), lambda j, i: (i, j)),
                  pl.BlockSpec((None, 1, D, nb), lambda j, i: (layer, j, 0, 0)),
                  pl.BlockSpec((ROW_TILE, D), rows_last), pl.BlockSpec((None, 1, D), lambda j, i: (glayer, 0, 0)),
                  pl.BlockSpec((ROW_TILE, D), rows_last), pl.BlockSpec((8, 128), lambda j, i: (0, 0))],
        out_specs=[pl.BlockSpec((ROW_TILE, D), rows_last), pl.BlockSpec((ROW_TILE, D), rows_last),
                   pl.BlockSpec((1, D), lambda j, i: (0, 0))],
        out_shape=[jax.ShapeDtypeStruct((T, D), F32), jax.ShapeDtypeStruct((T, D), BF16),
                   jax.ShapeDtypeStruct((1, D), F32)],
        scratch_shapes=[pltpu.VMEM((nrow, ROW_TILE, D), F32)],
        compiler_params=_params("arbitrary", "arbitrary"))(dp, wg, h, g, dres, after)


def _dw_in(n, dp, nblk, layer, nlayers, prev, name):
    T, D = n.shape
    nb = dp.shape[1] // nblk
    ta = D

    def body(n_ref, dp_ref, *rest):
        rest[-1][0] = _tn(n_ref[...], dp_ref[...]).astype(BF16)

    in_specs = [pl.BlockSpec((T, ta), lambda j, i: (0, i)), pl.BlockSpec((T, nb), lambda j, i: (0, j))]
    args = (n, dp) if prev is None else (n, dp, prev)
    return _pallas(
        body, name=name, grid=(nblk, D // ta), in_specs=in_specs + ([] if prev is None else [ANY]),
        out_specs=pl.BlockSpec((None, 1, ta, nb), lambda j, i: (layer, j, i, 0)),
        out_shape=jax.ShapeDtypeStruct((nlayers, nblk, D, nb), BF16),
        input_output_aliases={} if prev is None else {2: 0},
        compiler_params=_params("parallel", "parallel"))(*args)


def _dw_out(y, dout, layer, nlayers, prev, name):
    T, K = y.shape
    D = dout.shape[1]
    tk = 512

    def body(y_ref, d_ref, *rest):
        rest[-1][...] = _tn(y_ref[...], d_ref[...]).astype(BF16)

    in_specs = [pl.BlockSpec((T, tk), lambda i: (0, i)), pl.BlockSpec((T, D), lambda i: (0, 0))]
    args = (y, dout) if prev is None else (y, dout, prev)
    return _pallas(
        body, name=name, grid=(K // tk,), in_specs=in_specs + ([] if prev is None else [ANY]),
        out_specs=pl.BlockSpec((None, tk, D), lambda i: (layer, i, 0)),
        out_shape=jax.ShapeDtypeStruct((nlayers, K, D), BF16),
        input_output_aliases={} if prev is None else {2: 0},
        compiler_params=_params("parallel"))(*args)


def _loss_head(h, g, tgt):
    T, D = h.shape
    tm = MIX_TILE

    def body(h_ref, g_ref, t_ref, dh_ref, dhb_ref, dg_ref, loss_ref):
        i = pl.program_id(0)
        x = h_ref[...]
        gg = g_ref[...]
        r = lax.rsqrt(jnp.mean(x * x, axis=-1, keepdims=True) + EPS_RMS)
        xr = x * r
        e = xr * gg - t_ref[...]
        lp = 0.5 * jnp.sum(jnp.mean(e * e, axis=-1, keepdims=True), axis=0, keepdims=True)
        dn = e * (1.0 / D)
        q = dn * gg
        dh = r * q - x * ((r * r * r) * jnp.mean(q * x, axis=-1, keepdims=True))
        dh_ref[...] = dh
        dhb_ref[...] = dh.astype(BF16)
        dgp = jnp.sum(dn * xr, axis=0, keepdims=True)

        @pl.when(i == 0)
        def _():
            dg_ref[...] = dgp
            loss_ref[...] = lp

        @pl.when(i > 0)
        def _():
            dg_ref[...] += dgp
            loss_ref[...] += lp

    return _pallas(
        body, name="loss_head", grid=(T // tm,),
        in_specs=[pl.BlockSpec((tm, D), lambda i: (i, 0)), pl.BlockSpec((1, D), lambda i: (0, 0)),
                  pl.BlockSpec((tm, D), lambda i: (i, 0))],
        out_specs=[pl.BlockSpec((tm, D), lambda i: (i, 0)), pl.BlockSpec((tm, D), lambda i: (i, 0)),
                   pl.BlockSpec((1, D), lambda i: (0, 0)), pl.BlockSpec((1, 1), lambda i: (0, 0))],
        out_shape=[jax.ShapeDtypeStruct((T, D), F32), jax.ShapeDtypeStruct((T, D), BF16),
                   jax.ShapeDtypeStruct((1, D), F32), jax.ShapeDtypeStruct((1, 1), F32)],
        compiler_params=_params("arbitrary"))(h, g, tgt)


def _shift_up(x, j):
    return x if j == 0 else pltpu.roll(x, x.shape[0] - j, 0)


def _shift_down(x, j):
    return x if j == 0 else pltpu.roll(x, j, 0)


def _fill_shifted(dst_ref, src_ref):
    rows = dst_ref.shape[1]
    for s in range(8):
        dst_ref[s] = src_ref[pl.ds(s, rows), :]


def _fill_taps(wb_ref, w_ref):
    for k in range(w_ref.shape[0]):
        wb_ref[k] = jnp.broadcast_to(w_ref[k:k + 1, :], wb_ref.shape[1:])


def _tap_sum(sh_ref, wb_ref, r0, nrows, offsets):
    accs = [None] * (nrows // 8)
    for k, o in enumerate(offsets):
        wk = wb_ref[k]
        for u in range(nrows // 8):
            term = wk * sh_ref[o % 8, pl.ds(r0 + (o // 8) * 8 + 8 * u, 8), :]
            accs[u] = term if accs[u] is None else accs[u] + term
    return jnp.concatenate(accs, axis=0)


def _pool_sums(vx, up):
    sh = _shift_up if up else _shift_down
    outs = []
    for gi, w in enumerate(POOL_WINDOWS):
        s = vx[:, gi * POOL_GW:(gi + 1) * POOL_GW]
        j = 1
        while j < w:
            s = s + sh(s, j)
            j *= 2
        outs.append(s)
    return outs


def _inv_count(row0, nrows):
    pos = (row0 + 1 + lax.broadcasted_iota(jnp.int32, (nrows, 1), 0)).astype(F32)
    return [1.0 / jnp.minimum(pos, float(w)) for w in POOL_WINDOWS]


def _even_mixer_fwd(p, h, w_out, sl, cw, cb, lg, lb, pw, pb, sc, name):
    T = p.shape[0]
    C = D_MODEL
    tT, HL = MIX_TILE, EVEN_HALO
    hb = tT // HL
    chunk = 32

    def body(pm_ref, ph_ref, cw_ref, cb_ref, lg_ref, lb_ref, pw_ref, pb_ref, sc_ref, h_ref, wo_ref, y_ref, u1_ref,
             hn_ref, u0x_ref, sh_ref, wb_ref):
        i = pl.program_id(0)
        keep = (i > 0).astype(F32)

        @pl.when(i == 0)
        def _():
            _fill_taps(wb_ref, cw_ref)

        u0x_ref[0:HL] = ph_ref[:, 0:C] * _sigmoid(ph_ref[:, C:2 * C]) * keep
        u0x_ref[HL:HL + tT] = pm_ref[:, 0:C] * _sigmoid(pm_ref[:, C:2 * C])
        u0x_ref[HL + tT:HL + tT + 8] = jnp.zeros((8, C), F32)
        _fill_shifted(sh_ref, u0x_ref)
        offs = [HL - (CONV_K - 1) + k for k in range(CONV_K)]

        def conv_chunk(c, carry):
            r0 = pl.multiple_of(c * chunk, chunk)
            u1_ref[pl.ds(r0, chunk), :] = _tap_sum(sh_ref, wb_ref, r0, chunk, offs) + cb_ref[...]
            return carry

        lax.fori_loop(0, tT // chunk, conv_chunk, 0)
        u1 = u1_ref[...]
        mu = jnp.mean(u1, axis=-1, keepdims=True)
        xc = u1 - mu
        rs = lax.rsqrt(jnp.mean(xc * xc, axis=-1, keepdims=True) + EPS_LN)
        u2 = xc * rs * lg_ref[...] + lb_ref[...]
        u3 = u2 * _sigmoid(u2)
        ag = pm_ref[:, 2 * C:3 * C]
        y_ref[:, 0:C] = (u3 * (ag * _sigmoid(ag))).astype(BF16)
        vx = jnp.concatenate([ph_ref[:, 3 * C:4 * C] * keep, pm_ref[:, 3 * C:4 * C]], axis=0)
        sums = _pool_sums(vx, up=False)
        inv = _inv_count(i * tT, tT)
        for gi in range(len(POOL_WINDOWS)):
            cols = slice(gi * POOL_GW, (gi + 1) * POOL_GW)
            d0 = sums[gi][HL:] * inv[gi] - vx[HL:, cols]
            d1 = jnp.dot(d0.astype(BF16), pw_ref[gi], preferred_element_type=F32) + pb_ref[:, cols]
            bg = pm_ref[:, 4 * C + gi * POOL_GW:4 * C + (gi + 1) * POOL_GW]
            y_ref[:, C + gi * POOL_GW:C + (gi + 1) * POOL_GW] = (d1 * sc_ref[:, cols] * (bg * _sigmoid(bg))).astype(BF16)
        hn_ref[...] = h_ref[...] + jnp.dot(y_ref[...], wo_ref[...], preferred_element_type=F32)

    vec = pl.BlockSpec((None, 1, C), lambda i: (sl, 0, 0))
    rows = pl.BlockSpec((tT, C), lambda i: (i, 0))
    return _pallas(
        body, name=name, grid=(T // tT,),
        in_specs=[pl.BlockSpec((tT, 5 * C), lambda i: (i, 0)),
                  pl.BlockSpec((HL, 5 * C), lambda i: (jnp.maximum(i * hb - 1, 0), 0)),
                  pl.BlockSpec((None, 32, C), lambda i: (sl, 0, 0)), vec, vec, vec,
                  pl.BlockSpec((4, POOL_GW, POOL_GW), lambda i: (0, 0, 0)), vec, vec,
                  rows, pl.BlockSpec((None, 2 * C, C), lambda i: (0, 0, 0))],
        out_specs=[pl.BlockSpec((tT, 2 * C), lambda i: (i, 0)), rows, rows],
        out_shape=[jax.ShapeDtypeStruct((T, 2 * C), BF16), jax.ShapeDtypeStruct((T, C), F32),
                   jax.ShapeDtypeStruct((T, C), F32)],
        scratch_shapes=[pltpu.VMEM((HL + tT + 8, C), F32), pltpu.VMEM((8, HL + tT, C), F32),
                        pltpu.VMEM((32, 8, C), F32)],
        compiler_params=_params("arbitrary"))(p, p, cw, cb, lg, lb, pw, pb, sc, h, w_out)


def _even_mixer_bwd(p, u1, dout, w_out, after, sl, cwr, lg, lb, pw, pb, sc, name):
    T = p.shape[0]
    C = D_MODEL
    tT, HL = MIX_TILE, EVEN_HALO
    hb = tT // HL
    nT = T // tT
    R1 = tT + HL
    chunk = 32

    def body(pm_ref, pp_ref, pn_ref, u1m_ref, u1n_ref, dom_ref, don_ref, wo_ref, after_ref, cwr_ref, lg_ref, lb_ref,
             pw_ref, pb_ref, sc_ref, dp_ref, dcw_ref, dvec_ref, dpw_ref, x_ref, sh_ref, du0_ref, wb_ref):
        i = pl.program_id(0)
        dy = _nt(jnp.concatenate([dom_ref[...], don_ref[...]], axis=0), wo_ref[...])

        @pl.when(i == 0)
        def _():
            _fill_taps(wb_ref, cwr_ref)

        keep_prev = (i > 0).astype(F32)
        keep_next = (i < nT - 1).astype(F32)
        row = lax.broadcasted_iota(jnp.int32, (R1, 1), 0)
        live = jnp.where(row < tT, 1.0, keep_next)

        def cat(m, n):
            return jnp.concatenate([m, n], axis=0)

        u1 = cat(u1m_ref[...], u1n_ref[...])
        mu = jnp.mean(u1, axis=-1, keepdims=True)
        xc = u1 - mu
        rs = lax.rsqrt(jnp.mean(xc * xc, axis=-1, keepdims=True) + EPS_LN)
        xh = xc * rs
        u2 = xh * lg_ref[...] + lb_ref[...]
        s2 = _sigmoid(u2)
        u3 = u2 * s2
        ag = cat(pm_ref[:, 2 * C:3 * C], pn_ref[:, 2 * C:3 * C])
        sa = _sigmoid(ag)
        dya = dy[:, 0:C]
        dp_ref[:, 2 * C:3 * C] = (dya * u3 * _dsilu(ag, sa))[0:tT].astype(BF16)
        du2 = dya * (ag * sa) * _dsilu(u2, s2)
        dlg = jnp.sum((du2 * xh)[0:tT], axis=0, keepdims=True)
        dlb = jnp.sum(du2[0:tT], axis=0, keepdims=True)
        dxh = du2 * lg_ref[...]
        du1 = rs * (dxh - jnp.mean(dxh, axis=-1, keepdims=True) - xh * jnp.mean(dxh * xh, axis=-1, keepdims=True))
        du1 = du1 * live
        dcb = jnp.sum(du1[0:tT], axis=0, keepdims=True)
        x_ref[0:R1] = du1
        x_ref[R1:R1 + 8] = jnp.zeros((8, C), F32)
        _fill_shifted(sh_ref, x_ref)

        def du0_chunk(c, carry):
            r0 = pl.multiple_of(c * chunk, chunk)
            du0_ref[pl.ds(r0, chunk), :] = _tap_sum(sh_ref, wb_ref, r0, chunk, list(range(CONV_K)))
            return carry

        lax.fori_loop(0, tT // chunk, du0_chunk, 0)
        av, agl = pm_ref[:, 0:C], pm_ref[:, C:2 * C]
        sg = _sigmoid(agl)
        du0 = du0_ref[...]
        dp_ref[:, 0:C] = (du0 * sg).astype(BF16)
        dp_ref[:, C:2 * C] = (du0 * av * sg * (1.0 - sg)).astype(BF16)
        du0_ref[...] = du1[0:tT]
        x_ref[0:HL] = pp_ref[:, 0:C] * _sigmoid(pp_ref[:, C:2 * C]) * keep_prev
        x_ref[HL:HL + tT] = av * sg
        x_ref[HL + tT:HL + tT + 8] = jnp.zeros((8, C), F32)
        _fill_shifted(sh_ref, x_ref)

        @pl.when(i == 0)
        def _():
            dcw_ref[...] = jnp.zeros_like(dcw_ref)

        for k0 in range(0, CONV_K, 2):
            taps = [k for k in (k0, k0 + 1) if k < CONV_K]
            offs = [HL - (CONV_K - 1) + k for k in taps]

            def dw_chunk(c, accs, offs=offs):
                r0 = pl.multiple_of(c * 64, 64)
                accs = list(accs)
                for u in range(0, 64, 8):
                    d = du0_ref[pl.ds(r0 + u, 8), :]
                    for t, o in enumerate(offs):
                        accs[t] = accs[t] + d * sh_ref[o % 8, pl.ds(r0 + u + (o // 8) * 8, 8), :]
                return tuple(accs)

            sums = lax.fori_loop(0, tT // 64, dw_chunk, tuple(jnp.zeros((8, C), F32) for _ in taps))
            for k, acc in zip(taps, sums):
                dcw_ref[8 * k:8 * k + 8, :] += acc

        bg = cat(pm_ref[:, 4 * C:5 * C], pn_ref[:, 4 * C:5 * C])
        sb = _sigmoid(bg)
        dyb = dy[:, C:2 * C]
        dyb0 = dyb * (bg * sb)
        dd1 = dyb0 * sc_ref[...]
        dpb = jnp.sum(dd1[0:tT], axis=0, keepdims=True)
        inv1 = _inv_count(i * tT, R1)
        z_parts, dd0_parts = [], []
        for gi in range(len(POOL_WINDOWS)):
            cols = slice(gi * POOL_GW, (gi + 1) * POOL_GW)
            dd0 = _nt(dd1[:, cols].astype(BF16), pw_ref[gi])
            dd0_parts.append(dd0)
            z_parts.append(dd0 * inv1[gi] * live)
        fsum = _pool_sums(jnp.concatenate(z_parts, axis=1), up=True)
        vx = cat(pp_ref[:, 3 * C:4 * C] * keep_prev, pm_ref[:, 3 * C:4 * C])
        sums = _pool_sums(vx, up=False)
        inv0 = _inv_count(i * tT, tT)
        dsc_parts = []
        for gi in range(len(POOL_WINDOWS)):
            cols = slice(gi * POOL_GW, (gi + 1) * POOL_GW)
            dp_ref[:, 3 * C + gi * POOL_GW:3 * C + (gi + 1) * POOL_GW] = (fsum[gi][0:tT] - dd0_parts[gi][0:tT]).astype(BF16)
            d0 = (sums[gi][HL:] * inv0[gi] - vx[HL:, cols]).astype(BF16)
            d1 = jnp.dot(d0, pw_ref[gi], preferred_element_type=F32) + pb_ref[:, cols]
            bgm, sbm = bg[0:tT, cols], sb[0:tT, cols]
            dp_ref[:, 4 * C + gi * POOL_GW:4 * C + (gi + 1) * POOL_GW] = (
                dyb[0:tT, cols] * d1 * sc_ref[:, cols] * _dsilu(bgm, sbm)).astype(BF16)
            dsc_parts.append(jnp.sum(dyb0[0:tT, cols] * d1, axis=0, keepdims=True))
            dpw_g = _tn(d0, dd1[0:tT, cols].astype(BF16))

            @pl.when(i == 0)
            def _(gi=gi, dpw_g=dpw_g):
                dpw_ref[gi] = dpw_g

            @pl.when(i > 0)
            def _(gi=gi, dpw_g=dpw_g):
                dpw_ref[gi] += dpw_g

        dsc = jnp.concatenate(dsc_parts, axis=1)
        vecs = jnp.concatenate([dcb, dlg, dlb, dsc, dpb, jnp.zeros((3, C), F32)], axis=0)

        @pl.when(i == 0)
        def _():
            dvec_ref[...] = vecs

        @pl.when(i > 0)
        def _():
            dvec_ref[...] += vecs

    vec = pl.BlockSpec((None, 1, C), lambda i: (sl, 0, 0))
    taps = pl.BlockSpec((None, 32, C), lambda i: (sl, 0, 0))

    def prev_blk(i):
        return (jnp.maximum(i * hb - 1, 0), 0)

    def next_blk(i):
        return (jnp.minimum((i + 1) * hb, T // HL - 1), 0)

    return _pallas(
        body, name=name, grid=(nT,),
        in_specs=[pl.BlockSpec((tT, 5 * C), lambda i: (i, 0)), pl.BlockSpec((HL, 5 * C), prev_blk),
                  pl.BlockSpec((HL, 5 * C), next_blk),
                  pl.BlockSpec((tT, C), lambda i: (i, 0)), pl.BlockSpec((HL, C), next_blk),
                  pl.BlockSpec((tT, C), lambda i: (i, 0)), pl.BlockSpec((HL, C), next_blk),
                  pl.BlockSpec((None, 2 * C, C), lambda i: (0, 0, 0)), pl.BlockSpec((8, 128), lambda i: (0, 0)),
                  taps, vec, vec, pl.BlockSpec((4, POOL_GW, POOL_GW), lambda i: (0, 0, 0)), vec, vec],
        out_specs=[pl.BlockSpec((tT, 5 * C), lambda i: (i, 0)), pl.BlockSpec((32 * 8, C), lambda i: (0, 0)),
                   pl.BlockSpec((8, C), lambda i: (0, 0)), pl.BlockSpec((4, POOL_GW, POOL_GW), lambda i: (0, 0, 0))],
        out_shape=[jax.ShapeDtypeStruct((T, 5 * C), BF16), jax.ShapeDtypeStruct((32 * 8, C), F32),
                   jax.ShapeDtypeStruct((8, C), F32), jax.ShapeDtypeStruct((4, POOL_GW, POOL_GW), F32)],
        scratch_shapes=[pltpu.VMEM((R1 + 8, C), F32), pltpu.VMEM((8, R1, C), F32), pltpu.VMEM((tT, C), F32),
                        pltpu.VMEM((32, 8, C), F32)],
        compiler_params=_params("arbitrary"))(p, p, p, u1, u1, dout, dout, w_out, after, cwr, lg, lb, pw, pb, sc)


def _softplus(z):
    u = jnp.exp(-jnp.abs(z))
    w = 1.0 + u
    l1p = jnp.where(w == 1.0, u, u * jnp.log(w) / jnp.where(w == 1.0, 1.0, w - 1.0))
    return jnp.maximum(z, 0.0) + l1p


def _lru_gates(xrx, cw_ref, cb_ref, wr_ref, br_ref, wi_ref, bi_ref, lam_ref):
    HL = ODD_HALO
    xc = cb_ref[...] + cw_ref[LRU_CONV_K - 1:LRU_CONV_K, :] * xrx[HL:]
    for k in range(LRU_CONV_K - 1):
        xc = xc + cw_ref[k:k + 1, :] * _shift_down(xrx, LRU_CONV_K - 1 - k)[HL:]
    xcb = xc.astype(BF16)
    rp, ip = [], []
    for hd in range(LRU_HEADS):
        cols = slice(hd * LRU_HD, (hd + 1) * LRU_HD)
        rp.append(jnp.dot(xcb[:, cols], wr_ref[hd], preferred_element_type=F32))
        ip.append(jnp.dot(xcb[:, cols], wi_ref[hd], preferred_element_type=F32))
    r = _sigmoid(jnp.concatenate(rp, axis=1) + br_ref[...])
    ig = _sigmoid(jnp.concatenate(ip, axis=1) + bi_ref[...])
    sp = _softplus(-lam_ref[...])
    log_a = (-LRU_C) * r * sp
    a = jnp.exp(log_a)
    m2 = jnp.maximum(-jnp.tanh(log_a) * (a * a + 1.0), 1e-30)
    inv_mult = lax.rsqrt(m2)
    return xc, xcb, r, ig, sp, a, m2 * inv_mult, inv_mult


def _group_scan(a, b, reverse):
    n, w = a.shape
    a, b = a.reshape(n // 8, 8, w), b.reshape(n // 8, 8, w)
    pos = lax.broadcasted_iota(jnp.int32, (1, 8, 1), 1)
    s = 1
    while s < 8:
        ok = (pos < 8 - s) if reverse else (pos >= s)
        shift = (8 - s) if reverse else s
        a_sh = jnp.where(ok, pltpu.roll(a, shift, 1), 1.0)
        b_sh = jnp.where(ok, pltpu.roll(b, shift, 1), 0.0)
        b = a * b_sh + b
        a = a * a_sh
        s *= 2
    return a.reshape(n, w), b.reshape(n, w)


def _apply_carries(a_ref, b_ref, out_ref, c0, reverse):
    ng = a_ref.shape[0] // 8

    def step(t, c):
        r0 = pl.multiple_of(((ng - 1 - t) if reverse else t) * 8, 8)
        x = a_ref[pl.ds(r0, 8), :] * c + b_ref[pl.ds(r0, 8), :]
        out_ref[pl.ds(r0, 8), :] = x
        return x[0:1, :] if reverse else x[7:8, :]

    return lax.fori_loop(0, ng, step, c0)


def _odd_mixer_fwd(p, h, w_out, sl, cw, cb, wr, br, wi, bi, lam, name):
    T = p.shape[0]
    W = W_LRU
    D = D_MODEL
    tT, HL = MIX_TILE, ODD_HALO
    hb = tT // HL

    def body(pm_ref, ph_ref, cw_ref, cb_ref, wr_ref, br_ref, wi_ref, bi_ref, lam_ref, h_ref, wo_ref, y_ref, hs_ref,
             hn_ref, carry_ref, sa_ref, sb_ref):
        i = pl.program_id(0)
        keep = (i > 0).astype(F32)

        @pl.when(i == 0)
        def _():
            carry_ref[...] = jnp.zeros_like(carry_ref)

        xrx = jnp.concatenate([ph_ref[:, 0:W] * keep, pm_ref[:, 0:W]], axis=0)
        xc, _, _, ig, _, a, mult, _ = _lru_gates(xrx, cw_ref, cb_ref, wr_ref, br_ref, wi_ref, bi_ref, lam_ref)
        sa_ref[...], sb_ref[...] = _group_scan(a, mult * (ig * xc), reverse=False)
        last = _apply_carries(sa_ref, sb_ref, hs_ref, carry_ref[0:1, :], reverse=False)
        carry_ref[...] = jnp.broadcast_to(last, (8, W))
        hs = hs_ref[...]
        gt = pm_ref[:, W:2 * W]
        y_ref[...] = (hs * (gt * _sigmoid(gt))).astype(BF16)
        hn_ref[...] = h_ref[...] + jnp.dot(y_ref[...], wo_ref[...], preferred_element_type=F32)

    vec = pl.BlockSpec((None, 1, W), lambda i: (sl, 0, 0))
    heads = pl.BlockSpec((None, LRU_HEADS, LRU_HD, LRU_HD), lambda i: (sl, 0, 0, 0))
    rows = pl.BlockSpec((tT, D), lambda i: (i, 0))
    wide = pl.BlockSpec((tT, W), lambda i: (i, 0))
    return _pallas(
        body, name=name, grid=(T // tT,),
        in_specs=[pl.BlockSpec((tT, 2 * W), lambda i: (i, 0)),
                  pl.BlockSpec((HL, 2 * W), lambda i: (jnp.maximum(i * hb - 1, 0), 0)),
                  pl.BlockSpec((None, 8, W), lambda i: (sl, 0, 0)), vec, heads, vec, heads, vec, vec,
                  rows, pl.BlockSpec((None, W, D), lambda i: (0, 0, 0))],
        out_specs=[wide, wide, rows],
        out_shape=[jax.ShapeDtypeStruct((T, W), BF16), jax.ShapeDtypeStruct((T, W), F32),
                   jax.ShapeDtypeStruct((T, D), F32)],
        scratch_shapes=[pltpu.VMEM((8, W), F32), pltpu.VMEM((tT, W), F32), pltpu.VMEM((tT, W), F32)],
        compiler_params=_params("arbitrary"))(p, p, cw, cb, wr, br, wi, bi, lam, h, w_out)


def _odd_mixer_bwd(p, hs, dout, w_out, after, sl, cw, cb, wr, br, wi, bi, lam, name):
    T = p.shape[0]
    W = W_LRU
    D = dout.shape[1]
    tT, HL = MIX_TILE, ODD_HALO
    hb = tT // HL
    nT = T // tT

    def body(pm_ref, ph_ref, hsm_ref, hsh_ref, do_ref, wo_ref, after_ref, cw_ref, cb_ref, wr_ref, br_ref, wi_ref,
             bi_ref, lam_ref, dp_ref, dwr_ref, dwi_ref, dvec_ref, gcarry_ref, xcarry_ref, sa_ref, sb_ref, g_ref):
        i = pl.program_id(0)
        keep = (i < nT - 1).astype(F32)

        @pl.when(i == 0)
        def _():
            gcarry_ref[...] = jnp.zeros_like(gcarry_ref)
            xcarry_ref[...] = jnp.zeros_like(xcarry_ref)

        xrx = jnp.concatenate([ph_ref[:, 0:W] * keep, pm_ref[:, 0:W]], axis=0)
        xc, xcb, r, ig, sp, a, mult, inv_mult = _lru_gates(xrx, cw_ref, cb_ref, wr_ref, br_ref, wi_ref, bi_ref, lam_ref)
        hs = hsm_ref[...]
        gt = pm_ref[:, W:2 * W]
        sg = _sigmoid(gt)
        dyv = _nt(do_ref[...], wo_ref[...])
        dp_ref[:, W:2 * W] = (dyv * hs * _dsilu(gt, sg)).astype(BF16)
        row = lax.broadcasted_iota(jnp.int32, (tT, 1), 0)
        m = jnp.where(row == tT - 1, 1.0, _shift_up(a, 1))
        sa_ref[...], sb_ref[...] = _group_scan(m, dyv * (gt * sg), reverse=True)
        first = _apply_carries(sa_ref, sb_ref, g_ref, gcarry_ref[0:1, :], reverse=True)
        G = g_ref[...]
        gcarry_ref[...] = jnp.broadcast_to(a[0:1, :] * first, (8, W))
        hs_prev = jnp.where(row == 0, hsh_ref[HL - 1:HL, :] * keep, _shift_down(hs, 1))
        da = G * hs_prev
        dmult = G * (ig * xc)
        di = G * mult * xc
        dxc = G * mult * ig
        dlog_a = da * a - dmult * (a * a) * inv_mult
        drp = dlog_a * ((-LRU_C) * sp) * r * (1.0 - r)
        dip = di * ig * (1.0 - ig)
        dlam = jnp.sum(dlog_a * ((-LRU_C) * r), axis=0, keepdims=True) * (-_sigmoid(-lam_ref[...]))
        drb, dib = drp.astype(BF16), dip.astype(BF16)
        back = []
        for hd in range(LRU_HEADS):
            cols = slice(hd * LRU_HD, (hd + 1) * LRU_HD)
            back.append(_nt(drb[:, cols], wr_ref[hd]) + _nt(dib[:, cols], wi_ref[hd]))
            dwr_h = _tn(xcb[:, cols], drb[:, cols])
            dwi_h = _tn(xcb[:, cols], dib[:, cols])

            @pl.when(i == 0)
            def _(hd=hd, dwr_h=dwr_h, dwi_h=dwi_h):
                dwr_ref[hd] = dwr_h
                dwi_ref[hd] = dwi_h

            @pl.when(i > 0)
            def _(hd=hd, dwr_h=dwr_h, dwi_h=dwi_h):
                dwr_ref[hd] += dwr_h
                dwi_ref[hd] += dwi_h

        dxc = dxc + jnp.concatenate(back, axis=1)
        dxcx = jnp.concatenate([dxc, xcarry_ref[...]], axis=0)
        dxr = cw_ref[LRU_CONV_K - 1:LRU_CONV_K, :] * dxc
        rows = []
        for k in range(LRU_CONV_K - 1):
            j = LRU_CONV_K - 1 - k
            dxr = dxr + cw_ref[k:k + 1, :] * _shift_up(dxcx, j)[0:tT]
            rows.append(jnp.sum(dxc * _shift_down(xrx, j)[HL:], axis=0, keepdims=True))
        rows.append(jnp.sum(dxc * xrx[HL:], axis=0, keepdims=True))
        dp_ref[:, 0:W] = dxr.astype(BF16)
        xcarry_ref[...] = dxc[0:8]
        rows += [jnp.sum(dxc, axis=0, keepdims=True), jnp.sum(drp, axis=0, keepdims=True),
                 jnp.sum(dip, axis=0, keepdims=True), dlam]
        vecs = jnp.concatenate(rows, axis=0)

        @pl.when(i == 0)
        def _():
            dvec_ref[...] = vecs

        @pl.when(i > 0)
        def _():
            dvec_ref[...] += vecs

    vec = pl.BlockSpec((None, 1, W), lambda i: (sl, 0, 0))
    heads = pl.BlockSpec((None, LRU_HEADS, LRU_HD, LRU_HD), lambda i: (sl, 0, 0, 0))
    dheads = pl.BlockSpec((LRU_HEADS, LRU_HD, LRU_HD), lambda i: (0, 0, 0))

    def tile(i):
        return (nT - 1 - i, 0)

    def prev_blk(i):
        return (jnp.maximum((nT - 1 - i) * hb - 1, 0), 0)

    return _pallas(
        body, name=name, grid=(nT,),
        in_specs=[pl.BlockSpec((tT, 2 * W), tile), pl.BlockSpec((HL, 2 * W), prev_blk),
                  pl.BlockSpec((tT, W), tile), pl.BlockSpec((HL, W), prev_blk), pl.BlockSpec((tT, D), tile),
                  pl.BlockSpec((None, W, D), lambda i: (0, 0, 0)), pl.BlockSpec((8, 128), lambda i: (0, 0)),
                  pl.BlockSpec((None, 8, W), lambda i: (sl, 0, 0)), vec, heads, vec, heads, vec, vec],
        out_specs=[pl.BlockSpec((tT, 2 * W), tile), dheads, dheads, pl.BlockSpec((8, W), lambda i: (0, 0))],
        out_shape=[jax.ShapeDtypeStruct((T, 2 * W), BF16), jax.ShapeDtypeStruct((LRU_HEADS, LRU_HD, LRU_HD), F32),
                   jax.ShapeDtypeStruct((LRU_HEADS, LRU_HD, LRU_HD), F32), jax.ShapeDtypeStruct((8, W), F32)],
        scratch_shapes=[pltpu.VMEM((8, W), F32), pltpu.VMEM((8, W), F32), pltpu.VMEM((tT, W), F32),
                        pltpu.VMEM((tT, W), F32), pltpu.VMEM((tT, W), F32)],
        compiler_params=_params("arbitrary"))(p, p, hs, hs, dout, w_out, after, cw, cb, wr, br, wi, bi, lam)


def _pad_rows(a, rows):
    return jnp.pad(a, ((0, 0), (0, rows - a.shape[1]), (0, 0)))


def _layer_fwd(even, h, w, w_in, w_out, after):
    sl = w["sl"]
    p, n = _in_proj(h, w["norm"], sl, w_in, 0, after, "in_proj_even" if even else "in_proj_odd")
    if even:
        y, aux, h_next = _even_mixer_fwd(p, h, w_out, sl, w["conv_w"], w["conv_b"], w["ln_g"], w["ln_b"], w["pool_w"],
                                         w["pool_b"], w["pool_scale"], "even_mixer_fwd")
    else:
        y, aux, h_next = _odd_mixer_fwd(p, h, w_out, sl, w["conv_w"], w["conv_b"], w["w_rg"], w["b_rg"], w["w_ig"],
                                        w["b_ig"], w["lam"], "odd_mixer_fwd")
    return h_next, (h, n, p, aux, y)


def _layer_bwd_weights(even, saved, w, w_out, dhb, after):
    h, n, p, aux, y = saved
    if even:
        dp, dcw, dvec, dpw = _even_mixer_bwd(p, aux, dhb, w_out, after, w["sl"], w["conv_w_rev"], w["ln_g"], w["ln_b"],
                                             w["pool_w"], w["pool_b"], w["pool_scale"], "even_mixer_bwd")
        dw_out = _dw_out(y, dhb, 0, 1, None, "dw_out_even")
        dw_in = _dw_in(n, dp, N_CHIPS, 0, 1, None, "dw_in_even")
        return dp, dw_in, dw_out, dict(conv_w=dcw, vec=dvec, pool_w=dpw)
    dp, dwr, dwi, dvec = _odd_mixer_bwd(p, aux, dhb, w_out, after, w["sl"], w["conv_w"], w["conv_b"], w["w_rg"],
                                        w["b_rg"], w["w_ig"], w["b_ig"], w["lam"], "odd_mixer_bwd")
    dw_out = _dw_out(y, dhb, 0, 1, None, "dw_out_odd")
    dw_in = _dw_in(n, dp, N_CHIPS, 0, 1, None, "dw_in_odd")
    return dp, dw_in, dw_out, dict(w_rg=dwr, w_ig=dwi, vec=dvec)


def _layer_bwd_input(even, saved, w, w_in, dp, dh, after):
    return _dn_proj(dp, w_in, 0, saved[0], w["norm"], w["sl"], dh, after, "dn_proj_even" if even else "dn_proj_odd")


ANY = pl.BlockSpec(memory_space=pl.ANY)


def _mesh_pos():
    return lax.axis_index("x"), lax.axis_index("y"), lax.axis_index("c")


def _other_chips(x, y):
    return [(1 - x, y), (x, 1 - y), (1 - x, 1 - y)]


def _remote(src, dst, ssem, rsem, dev):
    return pltpu.make_async_remote_copy(src_ref=src, dst_ref=dst, send_sem=ssem, recv_sem=rsem, device_id=dev,
                                        device_id_type=MESH)


def _comm_call(body, name, ins, out_shape, scratch, aliases=None):
    return _pallas(body, name=name, in_specs=[ANY] * len(ins), out_specs=[ANY] * len(out_shape), out_shape=out_shape,
                   scratch_shapes=scratch, input_output_aliases=aliases or {},
                   compiler_params=pltpu.CompilerParams(has_side_effects=True))(*ins)


def _cast_shard(w, layer, pos):
    _, R, C = w.shape
    tr = _row_tile(R, C)

    def body(pos_ref, w_ref, o_ref):
        o_ref[...] = w_ref[...].astype(BF16)

    grid_spec = pltpu.PrefetchScalarGridSpec(
        num_scalar_prefetch=1, grid=(R // tr,),
        in_specs=[pl.BlockSpec((None, tr, C), lambda i, pr: (layer, i, 0))],
        out_specs=pl.BlockSpec((None, None, tr, C), lambda i, pr: (0, pr[0], i, 0)))
    return _pallas(body, name="cast_shard", grid_spec=grid_spec,
                   out_shape=jax.ShapeDtypeStruct((1, N_CHIPS, R, C), BF16),
                   compiler_params=_params("parallel"))(pos, w)


def _gather_weights(big, small):
    nA = len(big)
    half = [a.shape[2] // 2 for a in big]

    def body(*refs):
        ins, outs = refs[:nA + 1], refs[nA + 1:2 * nA + 2]
        ssem, rsem, fsem, frsem, lsem = refs[2 * nA + 2:]
        x, y, c = _mesh_pos()
        k = 2 * x + y
        chips = _other_chips(x, y)
        sib = (x, y, 1 - c)

        def slab(a, chip, core):
            return outs[a].at[:, chip, pl.ds(core * half[a], half[a]), :]

        local = [pltpu.make_async_copy(ins[nA], outs[nA].at[k], lsem.at[0])]
        for cp in local:
            cp.start()
        sends = []
        for j, (ox, oy) in enumerate(chips):
            for a in range(nA):
                sends.append(_remote(slab(a, k, c), slab(a, k, c), ssem.at[a, j], rsem.at[a, j], (ox, oy, c)))
            sends.append(_remote(ins[nA], outs[nA].at[k], ssem.at[nA, j], rsem.at[nA, j], (ox, oy, c)))
        for cp in sends:
            cp.start()
        for j, (ox, oy) in enumerate(chips):
            kj = 2 * ox + oy
            for a in range(nA):
                got = slab(a, kj, c)
                _remote(got, got, ssem.at[a, j], rsem.at[a, j], (ox, oy, c)).wait_recv()
                fw = _remote(got, got, fsem.at[a, j], frsem.at[a, j], sib)
                fw.start()
                sends.append(fw)
            gs = outs[nA].at[kj]
            _remote(gs, gs, ssem.at[nA, j], rsem.at[nA, j], (ox, oy, c)).wait_recv()
        for j, (ox, oy) in enumerate(chips):
            kj = 2 * ox + oy
            for a in range(nA):
                theirs = slab(a, kj, 1 - c)
                _remote(theirs, theirs, fsem.at[a, j], frsem.at[a, j], sib).wait_recv()
        for cp in sends:
            cp.wait_send()
        for cp in local:
            cp.wait()

    out_shape = [jax.ShapeDtypeStruct(a.shape, a.dtype) for a in big]
    out_shape.append(jax.ShapeDtypeStruct((N_CHIPS,) + small.shape, small.dtype))
    scratch = [pltpu.SemaphoreType.DMA((nA + 1, 3)), pltpu.SemaphoreType.DMA((nA + 1, 3)),
               pltpu.SemaphoreType.DMA((nA, 3)), pltpu.SemaphoreType.DMA((nA, 3)), pltpu.SemaphoreType.DMA((1,))]
    return _comm_call(body, "gather_weights", list(big) + [small], out_shape, scratch, {a: a for a in range(nA)})


HBM = pl.BlockSpec(memory_space=pltpu.HBM)
SEM = pl.BlockSpec(memory_space=pltpu.SEMAPHORE)
EFFECT = pltpu.SideEffectType.DATAFLOW_SIDE_EFFECTING


def _split_start(arrays, copies, n, name):
    k = len(arrays)

    def body(*refs):
        for cp in copies(refs[k + 2:2 * k + 2], refs[k], refs[k + 1]):
            cp.start()
        refs[2 * k + 2][...] = jnp.zeros((8, 128), F32)

    out = _pallas(
        body, name=name,
        out_shape=(pltpu.SemaphoreType.DMA((n,)), pltpu.SemaphoreType.DMA((n,)),
                   *[pltpu.HBM(a.shape, a.dtype) for a in arrays], jax.ShapeDtypeStruct((8, 128), F32)),
        in_specs=(HBM,) * k, out_specs=(SEM, SEM) + (HBM,) * k + (pl.BlockSpec(memory_space=pltpu.VMEM),),
        input_output_aliases={i: i + 2 for i in range(k)},
        compiler_params=pltpu.CompilerParams(has_side_effects=EFFECT),
    )(*[pltpu.with_memory_space_constraint(a, pltpu.HBM) for a in arrays])
    return out[0], out[1], list(out[2:2 + k]), out[2 + k]


def _split_wait(ssem, rsem, arrays, copies, after, name):
    k = len(arrays)

    def body(*refs):
        for cp in copies(refs[:k], refs[k], refs[k + 1]):
            cp.wait_send()
            cp.wait_recv()

    out = _pallas(
        body, name=name, out_shape=tuple(pltpu.HBM(a.shape, a.dtype) for a in arrays),
        in_specs=(HBM,) * k + (SEM, SEM, ANY), out_specs=(HBM,) * k, input_output_aliases={i: i for i in range(k)},
        compiler_params=pltpu.CompilerParams(has_side_effects=EFFECT),
    )(*arrays, ssem, rsem, after)
    return list(out)


def _gather_copies(shapes):
    half = [s[2] // 2 for s in shapes]

    def copies(refs, ssem, rsem):
        x, y, c = _mesh_pos()
        out = []
        for j, (ox, oy) in enumerate(_other_chips(x, y)):
            for a, ref in enumerate(refs):
                slab = ref.at[:, 2 * x + y, pl.ds(c * half[a], half[a]), :]
                out.append(_remote(slab, slab, ssem.at[3 * a + j], rsem.at[3 * a + j], (ox, oy, c)))
        return out

    return copies


def _chips_copies(n_arr):
    def copies(refs, ssem, rsem):
        x, y, c = _mesh_pos()
        out = []
        for j, (ox, oy) in enumerate(_other_chips(x, y)):
            for a in range(n_arr):
                out.append(_remote(refs[a].at[:, 2 * ox + oy], refs[n_arr + a].at[:, 2 * x + y], ssem.at[3 * a + j],
                                   rsem.at[3 * a + j], (ox, oy, c)))
        return out

    return copies


def _halves_copies(shapes):
    n = len(shapes)
    half = [s[2] // 2 for s in shapes]

    def copies(refs, ssem, rsem):
        x, y, c = _mesh_pos()
        return [_remote(refs[a].at[:, :, pl.ds((1 - c) * half[a], half[a]), :], refs[n + a], ssem.at[a], rsem.at[a],
                        (x, y, 1 - c)) for a in range(n)]

    return copies


def _forward_cores(arrays):
    nA = len(arrays)
    half = [a.shape[2] // 2 for a in arrays]

    def body(*refs):
        outs = refs[nA:2 * nA]
        ssem, rsem = refs[2 * nA:]
        x, y, c = _mesh_pos()
        sib = (x, y, 1 - c)
        sends, waits = [], []
        for j, (ox, oy) in enumerate(_other_chips(x, y)):
            for a in range(nA):
                got = outs[a].at[:, 2 * ox + oy, pl.ds(c * half[a], half[a]), :]
                sends.append(_remote(got, got, ssem.at[a, j], rsem.at[a, j], sib))
                theirs = outs[a].at[:, 2 * ox + oy, pl.ds((1 - c) * half[a], half[a]), :]
                waits.append(_remote(theirs, theirs, ssem.at[a, j], rsem.at[a, j], sib))
        for cp in sends:
            cp.start()
        for cp in waits:
            cp.wait_recv()
        for cp in sends:
            cp.wait_send()

    out_shape = [jax.ShapeDtypeStruct(a.shape, a.dtype) for a in arrays]
    scratch = [pltpu.SemaphoreType.DMA((nA, 3)), pltpu.SemaphoreType.DMA((nA, 3))]
    return _comm_call(body, "forward_cores", list(arrays), out_shape, scratch, {a: a for a in range(nA)})


def _exchange_halves(big):
    nA = len(big)
    half = [a.shape[2] // 2 for a in big]

    def body(*refs):
        ins, outs = refs[:nA], refs[nA:2 * nA]
        ssem, rsem = refs[2 * nA:]
        x, y, c = _mesh_pos()
        sib = (x, y, 1 - c)
        sends = [_remote(ins[a].at[:, :, pl.ds((1 - c) * half[a], half[a]), :], outs[a], ssem.at[a], rsem.at[a], sib)
                 for a in range(nA)]
        for cp in sends:
            cp.start()
        for a in range(nA):
            _remote(outs[a], outs[a], ssem.at[a], rsem.at[a], sib).wait_recv()
        for cp in sends:
            cp.wait_send()

    out_shape = [jax.ShapeDtypeStruct((a.shape[0], N_CHIPS, h, a.shape[3]), a.dtype) for a, h in zip(big, half)]
    scratch = [pltpu.SemaphoreType.DMA((nA,)), pltpu.SemaphoreType.DMA((nA,))]
    return _comm_call(body, "exchange_halves", list(big), out_shape, scratch)


def _exchange_final(grads, everywhere, small):
    nA = len(grads)
    n_remote = sum(7 if ev else 1 for ev in everywhere) + 7

    def body(*refs):
        small_ref, outs, gathered = refs[nA], refs[nA + 1:2 * nA + 1], refs[2 * nA + 1]
        ssem, rsem, lsem = refs[2 * nA + 2:]
        x, y, c = _mesh_pos()
        k = 2 * x + y
        sib = (x, y, 1 - c)
        local = pltpu.make_async_copy(small_ref, gathered.at[2 * k + c], lsem.at[0])
        local.start()
        sends, arrivals, waits = [], [], []
        count = [0]

        def sems():
            count[0] += 1
            return ssem.at[count[0] - 1], rsem.at[count[0] - 1]

        def to_sibling(src, mine, theirs):
            sm = sems()
            sends.append(_remote(src, mine, *sm, sib))
            waits.append(_remote(theirs, theirs, *sm, sib))

        def to_everyone(src, place):
            to_sibling(src, place(k, c), place(k, 1 - c))
            for (ox, oy) in _other_chips(x, y):
                ici, d2d = sems(), sems()
                got = place(2 * ox + oy, c)
                sends.append(_remote(src, place(k, c), *ici, (ox, oy, c)))
                arrivals.append((_remote(got, got, *ici, (ox, oy, c)), _remote(got, got, *d2d, sib)))
                theirs = place(2 * ox + oy, 1 - c)
                waits.append(_remote(theirs, theirs, *d2d, sib))

        to_everyone(small_ref, lambda chip, core: gathered.at[2 * chip + core])
        for a in range(nA):
            if everywhere[a]:
                r2 = grads[a].shape[1] // N_DEV

                def place(chip, core, a=a, r2=r2):
                    return outs[a].at[:, pl.ds((2 * chip + core) * r2, r2), :]

                to_everyone(place(k, c), place)
            else:
                r2 = grads[a].shape[1] // 2
                mine = outs[a].at[:, pl.ds(c * r2, r2), :]
                to_sibling(mine, mine, outs[a].at[:, pl.ds((1 - c) * r2, r2), :])
        for cp in sends:
            cp.start()
        for arrived, onward in arrivals:
            arrived.wait_recv()
            onward.start()
        for cp in waits:
            cp.wait_recv()
        for cp in sends + [onward for _, onward in arrivals]:
            cp.wait_send()
        local.wait()

    out_shape = [jax.ShapeDtypeStruct(g.shape, g.dtype) for g in grads]
    out_shape.append(jax.ShapeDtypeStruct((N_DEV,) + small.shape, small.dtype))
    scratch = [pltpu.SemaphoreType.DMA((n_remote,)), pltpu.SemaphoreType.DMA((n_remote,)), pltpu.SemaphoreType.DMA((1,))]
    return _comm_call(body, "exchange_final", list(grads) + [small], out_shape, scratch, {a: a for a in range(nA)})


BLOCK_BYTES = 4 << 20


def _row_tile(rows, cols, mult=16, limit=BLOCK_BYTES):
    best = mult
    for t in range(mult, rows + 1, mult):
        if rows % t == 0 and t * cols * 4 <= limit:
            best = t
    return best


def _add_cores(own, recv, pos):
    L, _, R, C = own.shape
    r2 = R // 2
    tr = _row_tile(r2, C)
    nb = r2 // tr

    def body(pos_ref, a_ref, r_ref, o_ref):
        o_ref[...] = (a_ref[...].astype(F32) + r_ref[...].astype(F32)).astype(BF16)

    blk = (None, None, tr, C)
    grid_spec = pltpu.PrefetchScalarGridSpec(
        num_scalar_prefetch=1, grid=(L, N_CHIPS, nb),
        in_specs=[pl.BlockSpec(blk, lambda l, s, i, pr: (l, s, pr[1] * nb + i, 0)),
                  pl.BlockSpec(blk, lambda l, s, i, pr: (l, s, i, 0))],
        out_specs=pl.BlockSpec(blk, lambda l, s, i, pr: (l, s, i, 0)))
    return _pallas(body, name="add_cores", grid_spec=grid_spec,
                   out_shape=jax.ShapeDtypeStruct((L, N_CHIPS, r2, C), BF16),
                   compiler_params=_params("parallel", "parallel", "parallel"))(pos, own, recv)


def _sum_chips(own, recv, pos, everywhere, layer, nlayers, prev):
    _, _, r2, C = own.shape
    tr = _row_tile(r2, 2 * C)
    nb = r2 // tr

    def body(pos_ref, a_ref, r_ref, *rest):
        acc = None
        for s in range(N_CHIPS):
            term = jnp.where(pos_ref[0] == s, a_ref[...], r_ref[s]).astype(F32)
            acc = term if acc is None else acc + term
        rest[-1][...] = acc

    if everywhere:
        def out_map(i, pr):
            return (layer, (2 * pr[0] + pr[1]) * nb + i, 0)
    else:
        def out_map(i, pr):
            return (layer, pr[1] * nb + i, 0)

    in_specs = [pl.BlockSpec((None, None, tr, C), lambda i, pr: (0, pr[0], i, 0)),
                pl.BlockSpec((None, N_CHIPS, tr, C), lambda i, pr: (0, 0, i, 0))]
    grid_spec = pltpu.PrefetchScalarGridSpec(
        num_scalar_prefetch=1, grid=(nb,), in_specs=in_specs + ([] if prev is None else [ANY]),
        out_specs=pl.BlockSpec((None, tr, C), out_map))
    rows = (N_DEV if everywhere else 2) * r2
    args = (pos, own, recv) if prev is None else (pos, own, recv, prev)
    return _pallas(body, name="sum_chips", grid_spec=grid_spec, out_shape=jax.ShapeDtypeStruct((nlayers, rows, C), F32),
                   input_output_aliases={} if prev is None else {3: 0},
                   compiler_params=_params("parallel"))(*args)


def _sum_devices(parts):
    n, R, C = parts.shape
    tr = _row_tile(R, C * n, 8)

    def body(p_ref, o_ref):
        acc = p_ref[0]
        for s in range(1, n):
            acc = acc + p_ref[s]
        o_ref[...] = acc

    return _pallas(body, name="sum_devices", grid=(R // tr,), in_specs=[pl.BlockSpec((n, tr, C), lambda i: (0, i, 0))],
                   out_specs=pl.BlockSpec((tr, C), lambda i: (i, 0)), out_shape=jax.ShapeDtypeStruct((R, C), F32),
                   compiler_params=_params("parallel"))(parts)


def _adamw(w, g, m, v, name):
    L, R, C = w.shape
    tr = _row_tile(R, C, 8, BLOCK_BYTES // 2)

    def body(w_ref, g_ref, m_ref, v_ref, d_ref, m2_ref, v2_ref):
        gg = g_ref[...]
        m2 = ADAM_B1 * m_ref[...] + (1.0 - ADAM_B1) * gg
        v2 = ADAM_B2 * v_ref[...] + (1.0 - ADAM_B2) * (gg * gg)
        m_hat = m2 / (1.0 - ADAM_B1 ** ADAM_STEP)
        v_hat = v2 / (1.0 - ADAM_B2 ** ADAM_STEP)
        d_ref[...] = -ADAM_LR * (m_hat / (jnp.sqrt(v_hat) + ADAM_EPS) + ADAM_WD * w_ref[...])
        m2_ref[...] = m2
        v2_ref[...] = v2

    blk = pl.BlockSpec((1, tr, C), lambda l, i: (l, i, 0))
    shp = jax.ShapeDtypeStruct((L, R, C), F32)
    return _pallas(body, name=name, grid=(L, R // tr), in_specs=[blk] * 4, out_specs=[blk] * 3, out_shape=[shp] * 3,
                   compiler_params=_params("parallel", "parallel"))(w, g, m, v)


WEIGHTS = ("norm_even", "w_in_even", "conv_a_w", "conv_a_b", "ln_a_g", "ln_a_b", "pool_w", "pool_b", "pool_scale",
           "w_out_even", "norm_odd", "w_in_odd", "conv_c_w", "conv_c_b", "w_rg", "b_rg", "w_ig", "b_ig", "lru_lambda",
           "w_out_odd", "final_norm")
BIG = ("w_in_even", "w_out_even", "pool_w", "w_in_odd", "w_out_odd", "w_rg", "w_ig")
SMALL = tuple(n for n in WEIGHTS if n not in BIG)
SMALL_SHARDED = ("conv_a_w", "pool_b", "norm_odd", "conv_c_w", "conv_c_b", "b_rg", "b_ig", "lru_lambda")


def _pack(arrs):
    flat = jnp.concatenate([a.reshape(-1) for a in arrs])
    rows = -(-flat.shape[0] // (64 * 128)) * 64
    return jnp.pad(flat, (0, rows * 128 - flat.shape[0])).reshape(rows, 128)


def _unpack(buf, shapes, lead=()):
    flat = buf.reshape(tuple(lead) + (-1,))
    out, o = [], 0
    for s in shapes:
        n = 1
        for d in s:
            n *= d
        out.append(flat[..., o:o + n].reshape(tuple(lead) + tuple(s)))
        o += n
    return out


def _shard(full, axis, k):
    n = full.shape[axis] // N_CHIPS
    return lax.dynamic_slice_in_dim(full, k * n, n, axis)


def kernel(x, norm_even, w_in_even, conv_a_w, conv_a_b, ln_a_g, ln_a_b, pool_w, pool_b, pool_scale, w_out_even, norm_odd, w_in_odd, conv_c_w, conv_c_b, w_rg, b_rg, w_ig, b_ig, lru_lambda, w_out_odd, final_norm, loss_target, m_norm_even, m_w_in_even, m_conv_a_w, m_conv_a_b, m_ln_a_g, m_ln_a_b, m_pool_w, m_pool_b, m_pool_scale, m_w_out_even, m_norm_odd, m_w_in_odd, m_conv_c_w, m_conv_c_b, m_w_rg, m_b_rg, m_w_ig, m_b_ig, m_lru_lambda, m_w_out_odd, m_final_norm, v_norm_even, v_w_in_even, v_conv_a_w, v_conv_a_b, v_ln_a_g, v_ln_a_b, v_pool_w, v_pool_b, v_pool_scale, v_w_out_even, v_norm_odd, v_w_in_odd, v_conv_c_w, v_conv_c_b, v_w_rg, v_b_rg, v_w_ig, v_b_ig, v_lru_lambda, v_w_out_odd, v_final_norm):
    P = dict(locals())
    xi, yi, ci = _mesh_pos()
    k = 2 * xi + yi
    L = w_in_even.shape[0]
    D = D_MODEL

    pos = jnp.stack([k, ci]).astype(jnp.int32)
    depth = 2 * L
    pool_w3 = pool_w.reshape(L, 4 * 64, POOL_GW)

    def cast_group(layer):
        j = layer // 2
        if layer % 2 == 0:
            return [_cast_shard(w_in_even, j, pos), _cast_shard(w_out_even, j, pos), _cast_shard(pool_w3, j, pos)]
        return [_cast_shard(w_in_odd, j, pos), _cast_shard(w_out_odd, j, pos)]

    *group, g_small = _gather_weights(cast_group(0), _pack([P[n] for n in SMALL_SHARDED]))
    full = {}
    for n, a in zip(SMALL_SHARDED, _unpack(g_small, [P[n].shape for n in SMALL_SHARDED], lead=(N_CHIPS,))):
        a = jnp.moveaxis(a, 0, -2)
        full[n] = a.reshape(a.shape[:-2] + (N_CHIPS * a.shape[-1],))

    small_even = dict(norm=norm_even[:, None], conv_w=_pad_rows(full["conv_a_w"], 32),
                      conv_w_rev=_pad_rows(full["conv_a_w"][:, ::-1], 32), conv_b=conv_a_b[:, None], ln_g=ln_a_g[:, None],
                      ln_b=ln_a_b[:, None], pool_b=full["pool_b"].reshape(L, 1, D), pool_scale=pool_scale[:, None])
    small_odd = dict(norm=full["norm_odd"][:, None], conv_w=_pad_rows(full["conv_c_w"], 8),
                     conv_b=full["conv_c_b"][:, None], w_rg=w_rg.astype(BF16), b_rg=full["b_rg"][:, None],
                     w_ig=w_ig.astype(BF16), b_ig=full["b_ig"][:, None], lam=full["lru_lambda"][:, None])

    def small_weights(layer, group):
        if layer % 2 == 0:
            pw = group[2].reshape(N_CHIPS, 4, 64, POOL_GW).transpose(1, 0, 2, 3).reshape(4, POOL_GW, POOL_GW)
            return dict(small_even, sl=layer // 2, pool_w=pw)
        return dict(small_odd, sl=layer // 2)

    no_token = jnp.zeros((8, 128), F32)
    h = x[0]
    saved, big_w, small_w = [], [], []
    for layer in range(depth):
        token = no_token
        if layer + 1 < depth:
            nxt = cast_group(layer + 1)
            copies = _gather_copies([a.shape for a in nxt])
            ssem, rsem, nxt, token = _split_start(nxt, copies, 3 * len(nxt), "gather_start%d" % (layer + 1))
        small_w.append(small_weights(layer, group))
        big_w.append((group[0], group[1].reshape(1, -1, D)))
        h, sv = _layer_fwd(layer % 2 == 0, h, small_w[layer], *big_w[layer], token)
        saved.append(sv)
        if layer + 1 < depth:
            group = _forward_cores(_split_wait(ssem, rsem, nxt, copies, h, "gather_wait%d" % (layer + 1)))

    dh, dhb, d_final, loss_part = _loss_head(h, final_norm[None], loss_target[0])
    everywhere = [False, False, False, False, False, True, True]
    final = [None] * len(everywhere)
    small_of = [None] * depth

    def finish(pending, after):
        ssem, rsem, arrs, copies, slots, pj, pl_ = pending
        arrs = _split_wait(ssem, rsem, arrs, copies, after, "chips_wait%d" % pl_)
        for a, r, s in zip(arrs[:len(slots)], arrs[len(slots):], slots):
            final[s] = _sum_chips(a, r, pos, everywhere[s], pj, L, final[s])

    pending = None
    token = no_token
    for layer in reversed(range(depth)):
        j = layer // 2
        even_layer = layer % 2 == 0
        dp, dw_in, dw_out, sm = _layer_bwd_weights(even_layer, saved[layer], small_w[layer], big_w[layer][1], dhb, token)
        if even_layer:
            dpw = sm["pool_w"].reshape(4, N_CHIPS, 64, POOL_GW).transpose(1, 0, 2, 3)
            parts = [dw_in, dw_out.reshape(1, N_CHIPS, -1, D), dpw.reshape(1, N_CHIPS, 4 * 64, POOL_GW).astype(BF16)]
            slots = [0, 1, 2]
        else:
            parts = [dw_in, dw_out.reshape(1, N_CHIPS, -1, D),
                     sm["w_rg"].reshape(1, N_CHIPS, -1, LRU_HD).astype(BF16),
                     sm["w_ig"].reshape(1, N_CHIPS, -1, LRU_HD).astype(BF16)]
            slots = [3, 4, 5, 6]
        n = len(parts)
        if layer > 0:
            hcopies = _halves_copies([a.shape for a in parts])
            hland = [lax.empty((1, N_CHIPS, a.shape[2] // 2, a.shape[3]), a.dtype) for a in parts]
            hs, hr, harrs, htoken = _split_start(parts + hland, hcopies, n, "halves_start%d" % layer)
            dh, dhb, sm["norm"] = _layer_bwd_input(even_layer, saved[layer], small_w[layer], big_w[layer][0], dp, dh,
                                                   htoken)
            harrs = _split_wait(hs, hr, harrs, hcopies, dh, "halves_wait%d" % layer)
            parts, recv = harrs[:n], harrs[n:]
        else:
            recv = _exchange_halves(parts)
        pair = [_add_cores(a, r, pos) for a, r in zip(parts, recv)]
        copies = _chips_copies(n)
        land = [lax.empty(a.shape, a.dtype) for a in pair]
        ssem, rsem, arrs, token = _split_start(pair + land, copies, 3 * n, "chips_start%d" % layer)
        if layer == 0:
            dh, dhb, sm["norm"] = _layer_bwd_input(even_layer, saved[layer], small_w[layer], big_w[layer][0], dp, dh, token)
        small_of[layer] = sm
        if pending is not None:
            finish(pending, dh)
        pending = (ssem, rsem, arrs, copies, slots, j, layer)
    grad_x = dh
    small_g = []
    for jj in range(L):
        ge, go = small_of[2 * jj], small_of[2 * jj + 1]
        small_g += [ge["conv_w"].reshape(32, 8, D).sum(axis=1)[:CONV_K], ge["vec"][0:5], ge["norm"], go["vec"], go["norm"]]
    small_g += [d_final, loss_part]
    small_shapes = [a.shape for a in small_g]
    packed_small = _pack(small_g)
    finish(pending, packed_small)
    *gw, recv_small = _exchange_final(final, everywhere, packed_small)
    sg = _unpack(_sum_devices(recv_small), small_shapes)

    grads = dict(w_in_even=gw[0], w_out_even=gw[1], pool_w=gw[2].reshape(pool_w.shape), w_in_odd=gw[3], w_out_odd=gw[4],
                 w_rg=gw[5].reshape(w_rg.shape), w_ig=gw[6].reshape(w_ig.shape), final_norm=sg[-2][0])
    loss = sg[-1][0, 0]
    ev = [sg[5 * j + 1] for j in range(L)]
    ov = [sg[5 * j + 3] for j in range(L)]
    grads["conv_a_w"] = _shard(jnp.stack([sg[5 * j] for j in range(L)]), 2, k)
    grads["norm_even"] = jnp.stack([sg[5 * j + 2][0] for j in range(L)])
    grads["norm_odd"] = _shard(jnp.stack([sg[5 * j + 4][0] for j in range(L)]), 1, k)
    for r, n in enumerate(("conv_a_b", "ln_a_g", "ln_a_b", "pool_scale")):
        grads[n] = jnp.stack([e[r] for e in ev])
    grads["pool_b"] = _shard(jnp.stack([e[4].reshape(4, POOL_GW) for e in ev]), 2, k)
    grads["conv_c_w"] = _shard(jnp.stack([o[0:4] for o in ov]), 2, k)
    for r, n in zip((4, 5, 6, 7), ("conv_c_b", "b_rg", "b_ig", "lru_lambda")):
        grads[n] = _shard(jnp.stack([o[r] for o in ov]), 1, k)

    delta, new_m, new_v = {}, {}, {}
    for n in BIG:
        s3 = (L, -1, P[n].shape[-1])
        d, m2, v2 = _adamw(P[n].reshape(s3), grads[n].reshape(s3), P["m_" + n].reshape(s3), P["v_" + n].reshape(s3), "adamw")
        delta[n], new_m[n], new_v[n] = d.reshape(P[n].shape), m2.reshape(P[n].shape), v2.reshape(P[n].shape)
    shapes = [P[n].shape for n in SMALL]
    packed = [_pack([src[n] for n in SMALL])[None] for src in
              (P, grads, {n: P["m_" + n] for n in SMALL}, {n: P["v_" + n] for n in SMALL})]
    for res, out in zip(_adamw(*packed, "adamw_small"), (delta, new_m, new_v)):
        for n, a in zip(SMALL, _unpack(res[0], shapes)):
            out[n] = a

    return (loss, grad_x[None], *[grads[n] for n in WEIGHTS], *[delta[n] for n in WEIGHTS],
            *[new_m[n] for n in WEIGHTS], *[new_v[n] for n in WEIGHTS])
```

```python
import jax
import jax.numpy as jnp
from jax import lax
from jax.experimental import pallas as pl
from jax.experimental.pallas import tpu as pltpu

F32 = jnp.float32
BF16 = jnp.bfloat16
MESH = pl.DeviceIdType.MESH

D_MODEL = 1024
N_CHIPS = 4
N_DEV = 8
EPS_RMS = 1e-6
EPS_LN = 1e-5
CONV_K = 31
POOL_WINDOWS = (2, 4, 8, 16)
POOL_GW = 256
LRU_HEADS = 12
LRU_HD = 128
W_LRU = LRU_HEADS * LRU_HD
LRU_CONV_K = 4
LRU_C = 8.0
ADAM_LR = 0.001
ADAM_B1 = 0.9
ADAM_B2 = 0.999
ADAM_EPS = 1e-08
ADAM_WD = 0.01
ADAM_STEP = 10

VMEM_LIMIT_BYTES = 56 * 1024 * 1024
ROW_TILE = 512
MIX_TILE = 256
EVEN_HALO = 32
ODD_HALO = 8


def _pallas(body, **kw):
    return pl.pallas_call(body, **kw)


def _params(*sem):
    return pltpu.CompilerParams(dimension_semantics=sem if sem else None, vmem_limit_bytes=VMEM_LIMIT_BYTES)


def _sigmoid(x):
    return 0.5 * jnp.tanh(0.5 * x) + 0.5


def _dsilu(x, s):
    return s * (1.0 + x * (1.0 - s))


def _nt(a, b):
    return lax.dot_general(a, b, (((1,), (1,)), ((), ())), preferred_element_type=F32)


def _tn(a, b):
    return lax.dot_general(a, b, (((0,), (0,)), ((), ())), preferred_element_type=F32)


def _in_proj(h, g, glayer, wg, layer, after, name):
    T, D = h.shape
    _, nblk, _, nb = wg.shape

    nrow = T // ROW_TILE

    def body(h_ref, g_ref, w_ref, after_ref, p_ref, n_ref, n_all):
        j, i = pl.program_id(0), pl.program_id(1)

        @pl.when(j == 0)
        def _():
            x = h_ref[...]
            r = lax.rsqrt(jnp.mean(x * x, axis=-1, keepdims=True) + EPS_RMS)
            nn = (x * r * g_ref[...]).astype(BF16)
            n_ref[...] = nn
            n_all[i] = nn

        p_ref[...] = jnp.dot(n_all[i], w_ref[0], preferred_element_type=F32)

    def rows_once(j, i):
        return (jnp.where(j == 0, i, nrow - 1), 0)

    return _pallas(
        body, name=name, grid=(nblk, nrow),
        in_specs=[pl.BlockSpec((ROW_TILE, D), rows_once), pl.BlockSpec((None, 1, D), lambda j, i: (glayer, 0, 0)),
                  pl.BlockSpec((None, 1, D, nb), lambda j, i: (layer, j, 0, 0)),
                  pl.BlockSpec((8, 128), lambda j, i: (0, 0))],
        out_specs=[pl.BlockSpec((ROW_TILE, nb), lambda j, i: (i, j)), pl.BlockSpec((ROW_TILE, D), rows_once)],
        out_shape=[jax.ShapeDtypeStruct((T, nblk * nb), F32), jax.ShapeDtypeStruct((T, D), BF16)],
        scratch_shapes=[pltpu.VMEM((nrow, ROW_TILE, D), BF16)],
        compiler_params=_params("arbitrary", "arbitrary"))(h, g, wg, after)


def _dn_proj(dp, wg, layer, h, g, glayer, dres, after, name):
    T, D = h.shape
    _, nblk, _, nb = wg.shape

    nrow = T // ROW_TILE

    def body(dp_ref, w_ref, h_ref, g_ref, dres_ref, after_ref, dh_ref, dhb_ref, dg_ref, acc_ref):
        j, i = pl.program_id(0), pl.program_id(1)
        part = _nt(dp_ref[...], w_ref[0])

        @pl.when(j == 0)
        def _():
            acc_ref[i] = part

        @pl.when(j > 0)
        def _():
            acc_ref[i] += part

        @pl.when(j == nblk - 1)
        def _():
            x = h_ref[...]
            r = lax.rsqrt(jnp.mean(x * x, axis=-1, keepdims=True) + EPS_RMS)
            dn = acc_ref[i]
            q = dn * g_ref[...]
            dh = dres_ref[...] + r * q - x * ((r * r * r) * jnp.mean(q * x, axis=-1, keepdims=True))
            dh_ref[...] = dh
            dhb_ref[...] = dh.astype(BF16)
            dgp = jnp.sum(dn * (x * r), axis=0, keepdims=True)

            @pl.when(i == 0)
            def _():
                dg_ref[...] = dgp

            @pl.when(i > 0)
            def _():
                dg_ref[...] += dgp

    def rows_last(j, i):
        return (jnp.where(j == nblk - 1, i, 0), 0)

    return _pallas(
        body, name=name, grid=(nblk, nrow),
        in_specs=[pl.BlockSpec((ROW_TILE, nb), lambda j, i: (i, j)),
                  pl.BlockSpec((None, 1, D, nb), lambda j, i: (layer, j, 0, 0)),
                  pl.BlockSpec((ROW_TILE, D), rows_last), pl.BlockSpec((None, 1, D), lambda j, i: (glayer, 0, 0)),
                  pl.BlockSpec((ROW_TILE, D), rows_last), pl.BlockSpec((8, 128), lambda j, i: (0, 0))],
        out_specs=[pl.BlockSpec((ROW_TILE, D), rows_last), pl.BlockSpec((ROW_TILE, D), rows_last),
                   pl.BlockSpec((1, D), lambda j, i: (0, 0))],
        out_shape=[jax.ShapeDtypeStruct((T, D), F32), jax.ShapeDtypeStruct((T, D), BF16),
                   jax.ShapeDtypeStruct((1, D), F32)],
        scratch_shapes=[pltpu.VMEM((nrow, ROW_TILE, D), F32)],
        compiler_params=_params("arbitrary", "arbitrary"))(dp, wg, h, g, dres, after)


def _dw_in(n, dp, nblk, layer, nlayers, prev, name):
    T, D = n.shape
    nb = dp.shape[1] // nblk
    ta = D

    def body(n_ref, dp_ref, *rest):
        rest[-1][0] = _tn(n_ref[...], dp_ref[...]).astype(BF16)

    in_specs = [pl.BlockSpec((T, ta), lambda j, i: (0, i)), pl.BlockSpec((T, nb), lambda j, i: (0, j))]
    args = (n, dp) if prev is None else (n, dp, prev)
    return _pallas(
        body, name=name, grid=(nblk, D // ta), in_specs=in_specs + ([] if prev is None else [ANY]),
        out_specs=pl.BlockSpec((None, 1, ta, nb), lambda j, i: (layer, j, i, 0)),
        out_shape=jax.ShapeDtypeStruct((nlayers, nblk, D, nb), BF16),
        input_output_aliases={} if prev is None else {2: 0},
        compiler_params=_params("parallel", "parallel"))(*args)


def _dw_out(y, dout, layer, nlayers, prev, name):
    T, K = y.shape
    D = dout.shape[1]
    tk = 512

    def body(y_ref, d_ref, *rest):
        rest[-1][...] = _tn(y_ref[...], d_ref[...]).astype(BF16)

    in_specs = [pl.BlockSpec((T, tk), lambda i: (0, i)), pl.BlockSpec((T, D), lambda i: (0, 0))]
    args = (y, dout) if prev is None else (y, dout, prev)
    return _pallas(
        body, name=name, grid=(K // tk,), in_specs=in_specs + ([] if prev is None else [ANY]),
        out_specs=pl.BlockSpec((None, tk, D), lambda i: (layer, i, 0)),
        out_shape=jax.ShapeDtypeStruct((nlayers, K, D), BF16),
        input_output_aliases={} if prev is None else {2: 0},
        compiler_params=_params("parallel"))(*args)


def _loss_head(h, g, tgt):
    T, D = h.shape
    tm = MIX_TILE

    def body(h_ref, g_ref, t_ref, dh_ref, dhb_ref, dg_ref, loss_ref):
        i = pl.program_id(0)
        x = h_ref[...]
        gg = g_ref[...]
        r = lax.rsqrt(jnp.mean(x * x, axis=-1, keepdims=True) + EPS_RMS)
        xr = x * r
        e = xr * gg - t_ref[...]
        lp = 0.5 * jnp.sum(jnp.mean(e * e, axis=-1, keepdims=True), axis=0, keepdims=True)
        dn = e * (1.0 / D)
        q = dn * gg
        dh = r * q - x * ((r * r * r) * jnp.mean(q * x, axis=-1, keepdims=True))
        dh_ref[...] = dh
        dhb_ref[...] = dh.astype(BF16)
        dgp = jnp.sum(dn * xr, axis=0, keepdims=True)

        @pl.when(i == 0)
        def _():
            dg_ref[...] = dgp
            loss_ref[...] = lp

        @pl.when(i > 0)
        def _():
            dg_ref[...] += dgp
            loss_ref[...] += lp

    return _pallas(
        body, name="loss_head", grid=(T // tm,),
        in_specs=[pl.BlockSpec((tm, D), lambda i: (i, 0)), pl.BlockSpec((1, D), lambda i: (0, 0)),
                  pl.BlockSpec((tm, D), lambda i: (i, 0))],
        out_specs=[pl.BlockSpec((tm, D), lambda i: (i, 0)), pl.BlockSpec((tm, D), lambda i: (i, 0)),
                   pl.BlockSpec((1, D), lambda i: (0, 0)), pl.BlockSpec((1, 1), lambda i: (0, 0))],
        out_shape=[jax.ShapeDtypeStruct((T, D), F32), jax.ShapeDtypeStruct((T, D), BF16),
                   jax.ShapeDtypeStruct((1, D), F32), jax.ShapeDtypeStruct((1, 1), F32)],
        compiler_params=_params("arbitrary"))(h, g, tgt)


def _shift_up(x, j):
    return x if j == 0 else pltpu.roll(x, x.shape[0] - j, 0)


def _shift_down(x, j):
    return x if j == 0 else pltpu.roll(x, j, 0)


def _fill_shifted(dst_ref, src_ref):
    rows = dst_ref.shape[1]
    for s in range(8):
        dst_ref[s] = src_ref[pl.ds(s, rows), :]


def _fill_taps(wb_ref, w_ref):
    for k in range(w_ref.shape[0]):
        wb_ref[k] = jnp.broadcast_to(w_ref[k:k + 1, :], wb_ref.shape[1:])


def _tap_sum(sh_ref, wb_ref, r0, nrows, offsets):
    accs = [None] * (nrows // 8)
    for k, o in enumerate(offsets):
        wk = wb_ref[k]
        for u in range(nrows // 8):
            term = wk * sh_ref[o % 8, pl.ds(r0 + (o // 8) * 8 + 8 * u, 8), :]
            accs[u] = term if accs[u] is None else accs[u] + term
    return jnp.concatenate(accs, axis=0)


def _pool_sums(vx, up):
    sh = _shift_up if up else _shift_down
    outs = []
    for gi, w in enumerate(POOL_WINDOWS):
        s = vx[:, gi * POOL_GW:(gi + 1) * POOL_GW]
        j = 1
        while j < w:
            s = s + sh(s, j)
            j *= 2
        outs.append(s)
    return outs


def _inv_count(row0, nrows):
    pos = (row0 + 1 + lax.broadcasted_iota(jnp.int32, (nrows, 1), 0)).astype(F32)
    return [1.0 / jnp.minimum(pos, float(w)) for w in POOL_WINDOWS]


def _even_mixer_fwd(p, h, w_out, sl, cw, cb, lg, lb, pw, pb, sc, name):
    T = p.shape[0]
    C = D_MODEL
    tT, HL = MIX_TILE, EVEN_HALO
    hb = tT // HL
    chunk = 32

    def body(pm_ref, ph_ref, cw_ref, cb_ref, lg_ref, lb_ref, pw_ref, pb_ref, sc_ref, h_ref, wo_ref, y_ref, u1_ref,
             hn_ref, u0x_ref, sh_ref, wb_ref):
        i = pl.program_id(0)
        keep = (i > 0).astype(F32)

        @pl.when(i == 0)
        def _():
            _fill_taps(wb_ref, cw_ref)

        u0x_ref[0:HL] = ph_ref[:, 0:C] * _sigmoid(ph_ref[:, C:2 * C]) * keep
        u0x_ref[HL:HL + tT] = pm_ref[:, 0:C] * _sigmoid(pm_ref[:, C:2 * C])
        u0x_ref[HL + tT:HL + tT + 8] = jnp.zeros((8, C), F32)
        _fill_shifted(sh_ref, u0x_ref)
        offs = [HL - (CONV_K - 1) + k for k in range(CONV_K)]

        def conv_chunk(c, carry):
            r0 = pl.multiple_of(c * chunk, chunk)
            u1_ref[pl.ds(r0, chunk), :] = _tap_sum(sh_ref, wb_ref, r0, chunk, offs) + cb_ref[...]
            return carry

        lax.fori_loop(0, tT // chunk, conv_chunk, 0)
        u1 = u1_ref[...]
        mu = jnp.mean(u1, axis=-1, keepdims=True)
        xc = u1 - mu
        rs = lax.rsqrt(jnp.mean(xc * xc, axis=-1, keepdims=True) + EPS_LN)
        u2 = xc * rs * lg_ref[...] + lb_ref[...]
        u3 = u2 * _sigmoid(u2)
        ag = pm_ref[:, 2 * C:3 * C]
        y_ref[:, 0:C] = (u3 * (ag * _sigmoid(ag))).astype(BF16)
        vx = jnp.concatenate([ph_ref[:, 3 * C:4 * C] * keep, pm_ref[:, 3 * C:4 * C]], axis=0)
        sums = _pool_sums(vx, up=False)
        inv = _inv_count(i * tT, tT)
        for gi in range(len(POOL_WINDOWS)):
            cols = slice(gi * POOL_GW, (gi + 1) * POOL_GW)
            d0 = sums[gi][HL:] * inv[gi] - vx[HL:, cols]
            d1 = jnp.dot(d0.astype(BF16), pw_ref[gi], preferred_element_type=F32) + pb_ref[:, cols]
            bg = pm_ref[:, 4 * C + gi * POOL_GW:4 * C + (gi + 1) * POOL_GW]
            y_ref[:, C + gi * POOL_GW:C + (gi + 1) * POOL_GW] = (d1 * sc_ref[:, cols] * (bg * _sigmoid(bg))).astype(BF16)
        hn_ref[...] = h_ref[...] + jnp.dot(y_ref[...], wo_ref[...], preferred_element_type=F32)

    vec = pl.BlockSpec((None, 1, C), lambda i: (sl, 0, 0))
    rows = pl.BlockSpec((tT, C), lambda i: (i, 0))
    return _pallas(
        body, name=name, grid=(T // tT,),
        in_specs=[pl.BlockSpec((tT, 5 * C), lambda i: (i, 0)),
                  pl.BlockSpec((HL, 5 * C), lambda i: (jnp.maximum(i * hb - 1, 0), 0)),
                  pl.BlockSpec((None, 32, C), lambda i: (sl, 0, 0)), vec, vec, vec,
                  pl.BlockSpec((4, POOL_GW, POOL_GW), lambda i: (0, 0, 0)), vec, vec,
                  rows, pl.BlockSpec((None, 2 * C, C), lambda i: (0, 0, 0))],
        out_specs=[pl.BlockSpec((tT, 2 * C), lambda i: (i, 0)), rows, rows],
        out_shape=[jax.ShapeDtypeStruct((T, 2 * C), BF16), jax.ShapeDtypeStruct((T, C), F32),
                   jax.ShapeDtypeStruct((T, C), F32)],
        scratch_shapes=[pltpu.VMEM((HL + tT + 8, C), F32), pltpu.VMEM((8, HL + tT, C), F32),
                        pltpu.VMEM((32, 8, C), F32)],
        compiler_params=_params("arbitrary"))(p, p, cw, cb, lg, lb, pw, pb, sc, h, w_out)


def _even_mixer_bwd(p, u1, dout, w_out, after, sl, cwr, lg, lb, pw, pb, sc, name):
    T = p.shape[0]
    C = D_MODEL
    tT, HL = MIX_TILE, EVEN_HALO
    hb = tT // HL
    nT = T // tT
    R1 = tT + HL
    chunk = 32

    def body(pm_ref, pp_ref, pn_ref, u1m_ref, u1n_ref, dom_ref, don_ref, wo_ref, after_ref, cwr_ref, lg_ref, lb_ref,
             pw_ref, pb_ref, sc_ref, dp_ref, dcw_ref, dvec_ref, dpw_ref, x_ref, sh_ref, du0_ref, wb_ref):
        i = pl.program_id(0)
        dy = _nt(jnp.concatenate([dom_ref[...], don_ref[...]], axis=0), wo_ref[...])

        @pl.when(i == 0)
        def _():
            _fill_taps(wb_ref, cwr_ref)

        keep_prev = (i > 0).astype(F32)
        keep_next = (i < nT - 1).astype(F32)
        row = lax.broadcasted_iota(jnp.int32, (R1, 1), 0)
        live = jnp.where(row < tT, 1.0, keep_next)

        def cat(m, n):
            return jnp.concatenate([m, n], axis=0)

        u1 = cat(u1m_ref[...], u1n_ref[...])
        mu = jnp.mean(u1, axis=-1, keepdims=True)
        xc = u1 - mu
        rs = lax.rsqrt(jnp.mean(xc * xc, axis=-1, keepdims=True) + EPS_LN)
        xh = xc * rs
        u2 = xh * lg_ref[...] + lb_ref[...]
        s2 = _sigmoid(u2)
        u3 = u2 * s2
        ag = cat(pm_ref[:, 2 * C:3 * C], pn_ref[:, 2 * C:3 * C])
        sa = _sigmoid(ag)
        dya = dy[:, 0:C]
        dp_ref[:, 2 * C:3 * C] = (dya * u3 * _dsilu(ag, sa))[0:tT].astype(BF16)
        du2 = dya * (ag * sa) * _dsilu(u2, s2)
        dlg = jnp.sum((du2 * xh)[0:tT], axis=0, keepdims=True)
        dlb = jnp.sum(du2[0:tT], axis=0, keepdims=True)
        dxh = du2 * lg_ref[...]
        du1 = rs * (dxh - jnp.mean(dxh, axis=-1, keepdims=True) - xh * jnp.mean(dxh * xh, axis=-1, keepdims=True))
        du1 = du1 * live
        dcb = jnp.sum(du1[0:tT], axis=0, keepdims=True)
        x_ref[0:R1] = du1
        x_ref[R1:R1 + 8] = jnp.zeros((8, C), F32)
        _fill_shifted(sh_ref, x_ref)

        def du0_chunk(c, carry):
            r0 = pl.multiple_of(c * chunk, chunk)
            du0_ref[pl.ds(r0, chunk), :] = _tap_sum(sh_ref, wb_ref, r0, chunk, list(range(CONV_K)))
            return carry

        lax.fori_loop(0, tT // chunk, du0_chunk, 0)
        av, agl = pm_ref[:, 0:C], pm_ref[:, C:2 * C]
        sg = _sigmoid(agl)
        du0 = du0_ref[...]
        dp_ref[:, 0:C] = (du0 * sg).astype(BF16)
        dp_ref[:, C:2 * C] = (du0 * av * sg * (1.0 - sg)).astype(BF16)
        du0_ref[...] = du1[0:tT]
        x_ref[0:HL] = pp_ref[:, 0:C] * _sigmoid(pp_ref[:, C:2 * C]) * keep_prev
        x_ref[HL:HL + tT] = av * sg
        x_ref[HL + tT:HL + tT + 8] = jnp.zeros((8, C), F32)
        _fill_shifted(sh_ref, x_ref)

        @pl.when(i == 0)
        def _():
            dcw_ref[...] = jnp.zeros_like(dcw_ref)

        for k0 in range(0, CONV_K, 2):
            taps = [k for k in (k0, k0 + 1) if k < CONV_K]
            offs = [HL - (CONV_K - 1) + k for k in taps]

            def dw_chunk(c, accs, offs=offs):
                r0 = pl.multiple_of(c * 64, 64)
                accs = list(accs)
                for u in range(0, 64, 8):
                    d = du0_ref[pl.ds(r0 + u, 8), :]
                    for t, o in enumerate(offs):
                        accs[t] = accs[t] + d * sh_ref[o % 8, pl.ds(r0 + u + (o // 8) * 8, 8), :]
                return tuple(accs)

            sums = lax.fori_loop(0, tT // 64, dw_chunk, tuple(jnp.zeros((8, C), F32) for _ in taps))
            for k, acc in zip(taps, sums):
                dcw_ref[8 * k:8 * k + 8, :] += acc

        bg = cat(pm_ref[:, 4 * C:5 * C], pn_ref[:, 4 * C:5 * C])
        sb = _sigmoid(bg)
        dyb = dy[:, C:2 * C]
        dyb0 = dyb * (bg * sb)
        dd1 = dyb0 * sc_ref[...]
        dpb = jnp.sum(dd1[0:tT], axis=0, keepdims=True)
        inv1 = _inv_count(i * tT, R1)
        z_parts, dd0_parts = [], []
        for gi in range(len(POOL_WINDOWS)):
            cols = slice(gi * POOL_GW, (gi + 1) * POOL_GW)
            dd0 = _nt(dd1[:, cols].astype(BF16), pw_ref[gi])
            dd0_parts.append(dd0)
            z_parts.append(dd0 * inv1[gi] * live)
        fsum = _pool_sums(jnp.concatenate(z_parts, axis=1), up=True)
        vx = cat(pp_ref[:, 3 * C:4 * C] * keep_prev, pm_ref[:, 3 * C:4 * C])
        sums = _pool_sums(vx, up=False)
        inv0 = _inv_count(i * tT, tT)
        dsc_parts = []
        for gi in range(len(POOL_WINDOWS)):
            cols = slice(gi * POOL_GW, (gi + 1) * POOL_GW)
            dp_ref[:, 3 * C + gi * POOL_GW:3 * C + (gi + 1) * POOL_GW] = (fsum[gi][0:tT] - dd0_parts[gi][0:tT]).astype(BF16)
            d0 = (sums[gi][HL:] * inv0[gi] - vx[HL:, cols]).astype(BF16)
            d1 = jnp.dot(d0, pw_ref[gi], preferred_element_type=F32) + pb_ref[:, cols]
            bgm, sbm = bg[0:tT, cols], sb[0:tT, cols]
            dp_ref[:, 4 * C + gi * POOL_GW:4 * C + (gi + 1) * POOL_GW] = (
                dyb[0:tT, cols] * d1 * sc_ref[:, cols] * _dsilu(bgm, sbm)).astype(BF16)
            dsc_parts.append(jnp.sum(dyb0[0:tT, cols] * d1, axis=0, keepdims=True))
            dpw_g = _tn(d0, dd1[0:tT, cols].astype(BF16))

            @pl.when(i == 0)
            def _(gi=gi, dpw_g=dpw_g):
                dpw_ref[gi] = dpw_g

            @pl.when(i > 0)
            def _(gi=gi, dpw_g=dpw_g):
                dpw_ref[gi] += dpw_g

        dsc = jnp.concatenate(dsc_parts, axis=1)
        vecs = jnp.concatenate([dcb, dlg, dlb, dsc, dpb, jnp.zeros((3, C), F32)], axis=0)

        @pl.when(i == 0)
        def _():
            dvec_ref[...] = vecs

        @pl.when(i > 0)
        def _():
            dvec_ref[...] += vecs

    vec = pl.BlockSpec((None, 1, C), lambda i: (sl, 0, 0))
    taps = pl.BlockSpec((None, 32, C), lambda i: (sl, 0, 0))

    def prev_blk(i):
        return (jnp.maximum(i * hb - 1, 0), 0)

    def next_blk(i):
        return (jnp.minimum((i + 1) * hb, T // HL - 1), 0)

    return _pallas(
        body, name=name, grid=(nT,),
        in_specs=[pl.BlockSpec((tT, 5 * C), lambda i: (i, 0)), pl.BlockSpec((HL, 5 * C), prev_blk),
                  pl.BlockSpec((HL, 5 * C), next_blk),
                  pl.BlockSpec((tT, C), lambda i: (i, 0)), pl.BlockSpec((HL, C), next_blk),
                  pl.BlockSpec((tT, C), lambda i: (i, 0)), pl.BlockSpec((HL, C), next_blk),
                  pl.BlockSpec((None, 2 * C, C), lambda i: (0, 0, 0)), pl.BlockSpec((8, 128), lambda i: (0, 0)),
                  taps, vec, vec, pl.BlockSpec((4, POOL_GW, POOL_GW), lambda i: (0, 0, 0)), vec, vec],
        out_specs=[pl.BlockSpec((tT, 5 * C), lambda i: (i, 0)), pl.BlockSpec((32 * 8, C), lambda i: (0, 0)),
                   pl.BlockSpec((8, C), lambda i: (0, 0)), pl.BlockSpec((4, POOL_GW, POOL_GW), lambda i: (0, 0, 0))],
        out_shape=[jax.ShapeDtypeStruct((T, 5 * C), BF16), jax.ShapeDtypeStruct((32 * 8, C), F32),
                   jax.ShapeDtypeStruct((8, C), F32), jax.ShapeDtypeStruct((4, POOL_GW, POOL_GW), F32)],
        scratch_shapes=[pltpu.VMEM((R1 + 8, C), F32), pltpu.VMEM((8, R1, C), F32), pltpu.VMEM((tT, C), F32),
                        pltpu.VMEM((32, 8, C), F32)],
        compiler_params=_params("arbitrary"))(p, p, p, u1, u1, dout, dout, w_out, after, cwr, lg, lb, pw, pb, sc)


def _softplus(z):
    u = jnp.exp(-jnp.abs(z))
    w = 1.0 + u
    l1p = jnp.where(w == 1.0, u, u * jnp.log(w) / jnp.where(w == 1.0, 1.0, w - 1.0))
    return jnp.maximum(z, 0.0) + l1p


def _lru_gates(xrx, cw_ref, cb_ref, wr_ref, br_ref, wi_ref, bi_ref, lam_ref):
    HL = ODD_HALO
    xc = cb_ref[...] + cw_ref[LRU_CONV_K - 1:LRU_CONV_K, :] * xrx[HL:]
    for k in range(LRU_CONV_K - 1):
        xc = xc + cw_ref[k:k + 1, :] * _shift_down(xrx, LRU_CONV_K - 1 - k)[HL:]
    xcb = xc.astype(BF16)
    rp, ip = [], []
    for hd in range(LRU_HEADS):
        cols = slice(hd * LRU_HD, (hd + 1) * LRU_HD)
        rp.append(jnp.dot(xcb[:, cols], wr_ref[hd], preferred_element_type=F32))
        ip.append(jnp.dot(xcb[:, cols], wi_ref[hd], preferred_element_type=F32))
    r = _sigmoid(jnp.concatenate(rp, axis=1) + br_ref[...])
    ig = _sigmoid(jnp.concatenate(ip, axis=1) + bi_ref[...])
    sp = _softplus(-lam_ref[...])
    log_a = (-LRU_C) * r * sp
    a = jnp.exp(log_a)
    m2 = jnp.maximum(-jnp.tanh(log_a) * (a * a + 1.0), 1e-30)
    inv_mult = lax.rsqrt(m2)
    return xc, xcb, r, ig, sp, a, m2 * inv_mult, inv_mult


def _group_scan(a, b, reverse):
    n, w = a.shape
    a, b = a.reshape(n // 8, 8, w), b.reshape(n // 8, 8, w)
    pos = lax.broadcasted_iota(jnp.int32, (1, 8, 1), 1)
    s = 1
    while s < 8:
        ok = (pos < 8 - s) if reverse else (pos >= s)
        shift = (8 - s) if reverse else s
        a_sh = jnp.where(ok, pltpu.roll(a, shift, 1), 1.0)
        b_sh = jnp.where(ok, pltpu.roll(b, shift, 1), 0.0)
        b = a * b_sh + b
        a = a * a_sh
        s *= 2
    return a.reshape(n, w), b.reshape(n, w)


def _apply_carries(a_ref, b_ref, out_ref, c0, reverse):
    ng = a_ref.shape[0] // 8

    def step(t, c):
        r0 = pl.multiple_of(((ng - 1 - t) if reverse else t) * 8, 8)
        x = a_ref[pl.ds(r0, 8), :] * c + b_ref[pl.ds(r0, 8), :]
        out_ref[pl.ds(r0, 8), :] = x
        return x[0:1, :] if reverse else x[7:8, :]

    return lax.fori_loop(0, ng, step, c0)


def _odd_mixer_fwd(p, h, w_out, sl, cw, cb, wr, br, wi, bi, lam, name):
    T = p.shape[0]
    W = W_LRU
    D = D_MODEL
    tT, HL = MIX_TILE, ODD_HALO
    hb = tT // HL

    def body(pm_ref, ph_ref, cw_ref, cb_ref, wr_ref, br_ref, wi_ref, bi_ref, lam_ref, h_ref, wo_ref, y_ref, hs_ref,
             hn_ref, carry_ref, sa_ref, sb_ref):
        i = pl.program_id(0)
        keep = (i > 0).astype(F32)

        @pl.when(i == 0)
        def _():
            carry_ref[...] = jnp.zeros_like(carry_ref)

        xrx = jnp.concatenate([ph_ref[:, 0:W] * keep, pm_ref[:, 0:W]], axis=0)
        xc, _, _, ig, _, a, mult, _ = _lru_gates(xrx, cw_ref, cb_ref, wr_ref, br_ref, wi_ref, bi_ref, lam_ref)
        sa_ref[...], sb_ref[...] = _group_scan(a, mult * (ig * xc), reverse=False)
        last = _apply_carries(sa_ref, sb_ref, hs_ref, carry_ref[0:1, :], reverse=False)
        carry_ref[...] = jnp.broadcast_to(last, (8, W))
        hs = hs_ref[...]
        gt = pm_ref[:, W:2 * W]
        y_ref[...] = (hs * (gt * _sigmoid(gt))).astype(BF16)
        hn_ref[...] = h_ref[...] + jnp.dot(y_ref[...], wo_ref[...], preferred_element_type=F32)

    vec = pl.BlockSpec((None, 1, W), lambda i: (sl, 0, 0))
    heads = pl.BlockSpec((None, LRU_HEADS, LRU_HD, LRU_HD), lambda i: (sl, 0, 0, 0))
    rows = pl.BlockSpec((tT, D), lambda i: (i, 0))
    wide = pl.BlockSpec((tT, W), lambda i: (i, 0))
    return _pallas(
        body, name=name, grid=(T // tT,),
        in_specs=[pl.BlockSpec((tT, 2 * W), lambda i: (i, 0)),
                  pl.BlockSpec((HL, 2 * W), lambda i: (jnp.maximum(i * hb - 1, 0), 0)),
                  pl.BlockSpec((None, 8, W), lambda i: (sl, 0, 0)), vec, heads, vec, heads, vec, vec,
                  rows, pl.BlockSpec((None, W, D), lambda i: (0, 0, 0))],
        out_specs=[wide, wide, rows],
        out_shape=[jax.ShapeDtypeStruct((T, W), BF16), jax.ShapeDtypeStruct((T, W), F32),
                   jax.ShapeDtypeStruct((T, D), F32)],
        scratch_shapes=[pltpu.VMEM((8, W), F32), pltpu.VMEM((tT, W), F32), pltpu.VMEM((tT, W), F32)],
        compiler_params=_params("arbitrary"))(p, p, cw, cb, wr, br, wi, bi, lam, h, w_out)


def _odd_mixer_bwd(p, hs, dout, w_out, after, sl, cw, cb, wr, br, wi, bi, lam, name):
    T = p.shape[0]
    W = W_LRU
    D = dout.shape[1]
    tT, HL = MIX_TILE, ODD_HALO
    hb = tT // HL
    nT = T // tT

    def body(pm_ref, ph_ref, hsm_ref, hsh_ref, do_ref, wo_ref, after_ref, cw_ref, cb_ref, wr_ref, br_ref, wi_ref,
             bi_ref, lam_ref, dp_ref, dwr_ref, dwi_ref, dvec_ref, gcarry_ref, xcarry_ref, sa_ref, sb_ref, g_ref):
        i = pl.program_id(0)
        keep = (i < nT - 1).astype(F32)

        @pl.when(i == 0)
        def _():
            gcarry_ref[...] = jnp.zeros_like(gcarry_ref)
            xcarry_ref[...] = jnp.zeros_like(xcarry_ref)

        xrx = jnp.concatenate([ph_ref[:, 0:W] * keep, pm_ref[:, 0:W]], axis=0)
        xc, xcb, r, ig, sp, a, mult, inv_mult = _lru_gates(xrx, cw_ref, cb_ref, wr_ref, br_ref, wi_ref, bi_ref, lam_ref)
        hs = hsm_ref[...]
        gt = pm_ref[:, W:2 * W]
        sg = _sigmoid(gt)
        dyv = _nt(do_ref[...], wo_ref[...])
        dp_ref[:, W:2 * W] = (dyv * hs * _dsilu(gt, sg)).astype(BF16)
        row = lax.broadcasted_iota(jnp.int32, (tT, 1), 0)
        m = jnp.where(row == tT - 1, 1.0, _shift_up(a, 1))
        sa_ref[...], sb_ref[...] = _group_scan(m, dyv * (gt * sg), reverse=True)
        first = _apply_carries(sa_ref, sb_ref, g_ref, gcarry_ref[0:1, :], reverse=True)
        G = g_ref[...]
        gcarry_ref[...] = jnp.broadcast_to(a[0:1, :] * first, (8, W))
        hs_prev = jnp.where(row == 0, hsh_ref[HL - 1:HL, :] * keep, _shift_down(hs, 1))
        da = G * hs_prev
        dmult = G * (ig * xc)
        di = G * mult * xc
        dxc = G * mult * ig
        dlog_a = da * a - dmult * (a * a) * inv_mult
        drp = dlog_a * ((-LRU_C) * sp) * r * (1.0 - r)
        dip = di * ig * (1.0 - ig)
        dlam = jnp.sum(dlog_a * ((-LRU_C) * r), axis=0, keepdims=True) * (-_sigmoid(-lam_ref[...]))
        drb, dib = drp.astype(BF16), dip.astype(BF16)
        back = []
        for hd in range(LRU_HEADS):
            cols = slice(hd * LRU_HD, (hd + 1) * LRU_HD)
            back.append(_nt(drb[:, cols], wr_ref[hd]) + _nt(dib[:, cols], wi_ref[hd]))
            dwr_h = _tn(xcb[:, cols], drb[:, cols])
            dwi_h = _tn(xcb[:, cols], dib[:, cols])

            @pl.when(i == 0)
            def _(hd=hd, dwr_h=dwr_h, dwi_h=dwi_h):
                dwr_ref[hd] = dwr_h
                dwi_ref[hd] = dwi_h

            @pl.when(i > 0)
            def _(hd=hd, dwr_h=dwr_h, dwi_h=dwi_h):
                dwr_ref[hd] += dwr_h
                dwi_ref[hd] += dwi_h

        dxc = dxc + jnp.concatenate(back, axis=1)
        dxcx = jnp.concatenate([dxc, xcarry_ref[...]], axis=0)
        dxr = cw_ref[LRU_CONV_K - 1:LRU_CONV_K, :] * dxc
        rows = []
        for k in range(LRU_CONV_K - 1):
            j = LRU_CONV_K - 1 - k
            dxr = dxr + cw_ref[k:k + 1, :] * _shift_up(dxcx, j)[0:tT]
            rows.append(jnp.sum(dxc * _shift_down(xrx, j)[HL:], axis=0, keepdims=True))
        rows.append(jnp.sum(dxc * xrx[HL:], axis=0, keepdims=True))
        dp_ref[:, 0:W] = dxr.astype(BF16)
        xcarry_ref[...] = dxc[0:8]
        rows += [jnp.sum(dxc, axis=0, keepdims=True), jnp.sum(drp, axis=0, keepdims=True),
                 jnp.sum(dip, axis=0, keepdims=True), dlam]
        vecs = jnp.concatenate(rows, axis=0)

        @pl.when(i == 0)
        def _():
            dvec_ref[...] = vecs

        @pl.when(i > 0)
        def _():
            dvec_ref[...] += vecs

    vec = pl.BlockSpec((None, 1, W), lambda i: (sl, 0, 0))
    heads = pl.BlockSpec((None, LRU_HEADS, LRU_HD, LRU_HD), lambda i: (sl, 0, 0, 0))
    dheads = pl.BlockSpec((LRU_HEADS, LRU_HD, LRU_HD), lambda i: (0, 0, 0))

    def tile(i):
        return (nT - 1 - i, 0)

    def prev_blk(i):
        return (jnp.maximum((nT - 1 - i) * hb - 1, 0), 0)

    return _pallas(
        body, name=name, grid=(nT,),
        in_specs=[pl.BlockSpec((tT, 2 * W), tile), pl.BlockSpec((HL, 2 * W), prev_blk),
                  pl.BlockSpec((tT, W), tile), pl.BlockSpec((HL, W), prev_blk), pl.BlockSpec((tT, D), tile),
                  pl.BlockSpec((None, W, D), lambda i: (0, 0, 0)), pl.BlockSpec((8, 128), lambda i: (0, 0)),
                  pl.BlockSpec((None, 8, W), lambda i: (sl, 0, 0)), vec, heads, vec, heads, vec, vec],
        out_specs=[pl.BlockSpec((tT, 2 * W), tile), dheads, dheads, pl.BlockSpec((8, W), lambda i: (0, 0))],
        out_shape=[jax.ShapeDtypeStruct((T, 2 * W), BF16), jax.ShapeDtypeStruct((LRU_HEADS, LRU_HD, LRU_HD), F32),
                   jax.ShapeDtypeStruct((LRU_HEADS, LRU_HD, LRU_HD), F32), jax.ShapeDtypeStruct((8, W), F32)],
        scratch_shapes=[pltpu.VMEM((8, W), F32), pltpu.VMEM((8, W), F32), pltpu.VMEM((tT, W), F32),
                        pltpu.VMEM((tT, W), F32), pltpu.VMEM((tT, W), F32)],
        compiler_params=_params("arbitrary"))(p, p, hs, hs, dout, w_out, after, cw, cb, wr, br, wi, bi, lam)


def _pad_rows(a, rows):
    return jnp.pad(a, ((0, 0), (0, rows - a.shape[1]), (0, 0)))


def _layer_fwd(even, h, w, w_in, w_out, after):
    sl = w["sl"]
    p, n = _in_proj(h, w["norm"], sl, w_in, 0, after, "in_proj_even" if even else "in_proj_odd")
    if even:
        y, aux, h_next = _even_mixer_fwd(p, h, w_out, sl, w["conv_w"], w["conv_b"], w["ln_g"], w["ln_b"], w["pool_w"],
                                         w["pool_b"], w["pool_scale"], "even_mixer_fwd")
    else:
        y, aux, h_next = _odd_mixer_fwd(p, h, w_out, sl, w["conv_w"], w["conv_b"], w["w_rg"], w["b_rg"], w["w_ig"],
                                        w["b_ig"], w["lam"], "odd_mixer_fwd")
    return h_next, (h, n, p, aux, y)


def _layer_bwd_weights(even, saved, w, w_out, dhb, after):
    h, n, p, aux, y = saved
    if even:
        dp, dcw, dvec, dpw = _even_mixer_bwd(p, aux, dhb, w_out, after, w["sl"], w["conv_w_rev"], w["ln_g"], w["ln_b"],
                                             w["pool_w"], w["pool_b"], w["pool_scale"], "even_mixer_bwd")
        dw_out = _dw_out(y, dhb, 0, 1, None, "dw_out_even")
        dw_in = _dw_in(n, dp, N_CHIPS, 0, 1, None, "dw_in_even")
        return dp, dw_in, dw_out, dict(conv_w=dcw, vec=dvec, pool_w=dpw)
    dp, dwr, dwi, dvec = _odd_mixer_bwd(p, aux, dhb, w_out, after, w["sl"], w["conv_w"], w["conv_b"], w["w_rg"],
                                        w["b_rg"], w["w_ig"], w["b_ig"], w["lam"], "odd_mixer_bwd")
    dw_out = _dw_out(y, dhb, 0, 1, None, "dw_out_odd")
    dw_in = _dw_in(n, dp, N_CHIPS, 0, 1, None, "dw_in_odd")
    return dp, dw_in, dw_out, dict(w_rg=dwr, w_ig=dwi, vec=dvec)


def _layer_bwd_input(even, saved, w, w_in, dp, dh, after):
    return _dn_proj(dp, w_in, 0, saved[0], w["norm"], w["sl"], dh, after, "dn_proj_even" if even else "dn_proj_odd")


ANY = pl.BlockSpec(memory_space=pl.ANY)


def _mesh_pos():
    return lax.axis_index("x"), lax.axis_index("y"), lax.axis_index("c")


def _other_chips(x, y):
    return [(1 - x, y), (x, 1 - y), (1 - x, 1 - y)]


def _remote(src, dst, ssem, rsem, dev):
    return pltpu.make_async_remote_copy(src_ref=src, dst_ref=dst, send_sem=ssem, recv_sem=rsem, device_id=dev,
                                        device_id_type=MESH)


def _comm_call(body, name, ins, out_shape, scratch, aliases=None):
    return _pallas(body, name=name, in_specs=[ANY] * len(ins), out_specs=[ANY] * len(out_shape), out_shape=out_shape,
                   scratch_shapes=scratch, input_output_aliases=aliases or {},
                   compiler_params=pltpu.CompilerParams(has_side_effects=True))(*ins)


def _cast_shard(w, layer, pos):
    _, R, C = w.shape
    tr = _row_tile(R, C)

    def body(pos_ref, w_ref, o_ref):
        o_ref[...] = w_ref[...].astype(BF16)

    grid_spec = pltpu.PrefetchScalarGridSpec(
        num_scalar_prefetch=1, grid=(R // tr,),
        in_specs=[pl.BlockSpec((None, tr, C), lambda i, pr: (layer, i, 0))],
        out_specs=pl.BlockSpec((None, None, tr, C), lambda i, pr: (0, pr[0], i, 0)))
    return _pallas(body, name="cast_shard", grid_spec=grid_spec,
                   out_shape=jax.ShapeDtypeStruct((1, N_CHIPS, R, C), BF16),
                   compiler_params=_params("parallel"))(pos, w)


def _gather_weights(big, small):
    nA = len(big)
    half = [a.shape[2] // 2 for a in big]

    def body(*refs):
        ins, outs = refs[:nA + 1], refs[nA + 1:2 * nA + 2]
        ssem, rsem, fsem, frsem, lsem = refs[2 * nA + 2:]
        x, y, c = _mesh_pos()
        k = 2 * x + y
        chips = _other_chips(x, y)
        sib = (x, y, 1 - c)

        def slab(a, chip, core):
            return outs[a].at[:, chip, pl.ds(core * half[a], half[a]), :]

        local = [pltpu.make_async_copy(ins[nA], outs[nA].at[k], lsem.at[0])]
        for cp in local:
            cp.start()
        sends = []
        for j, (ox, oy) in enumerate(chips):
            for a in range(nA):
                sends.append(_remote(slab(a, k, c), slab(a, k, c), ssem.at[a, j], rsem.at[a, j], (ox, oy, c)))
            sends.append(_remote(ins[nA], outs[nA].at[k], ssem.at[nA, j], rsem.at[nA, j], (ox, oy, c)))
        for cp in sends:
            cp.start()
        for j, (ox, oy) in enumerate(chips):
            kj = 2 * ox + oy
            for a in range(nA):
                got = slab(a, kj, c)
                _remote(got, got, ssem.at[a, j], rsem.at[a, j], (ox, oy, c)).wait_recv()
                fw = _remote(got, got, fsem.at[a, j], frsem.at[a, j], sib)
                fw.start()
                sends.append(fw)
            gs = outs[nA].at[kj]
            _remote(gs, gs, ssem.at[nA, j], rsem.at[nA, j], (ox, oy, c)).wait_recv()
        for j, (ox, oy) in enumerate(chips):
            kj = 2 * ox + oy
            for a in range(nA):
                theirs = slab(a, kj, 1 - c)
                _remote(theirs, theirs, fsem.at[a, j], frsem.at[a, j], sib).wait_recv()
        for cp in sends:
            cp.wait_send()
        for cp in local:
            cp.wait()

    out_shape = [jax.ShapeDtypeStruct(a.shape, a.dtype) for a in big]
    out_shape.append(jax.ShapeDtypeStruct((N_CHIPS,) + small.shape, small.dtype))
    scratch = [pltpu.SemaphoreType.DMA((nA + 1, 3)), pltpu.SemaphoreType.DMA((nA + 1, 3)),
               pltpu.SemaphoreType.DMA((nA, 3)), pltpu.SemaphoreType.DMA((nA, 3)), pltpu.SemaphoreType.DMA((1,))]
    return _comm_call(body, "gather_weights", list(big) + [small], out_shape, scratch, {a: a for a in range(nA)})


HBM = pl.BlockSpec(memory_space=pltpu.HBM)
SEM = pl.BlockSpec(memory_space=pltpu.SEMAPHORE)
EFFECT = pltpu.SideEffectType.DATAFLOW_SIDE_EFFECTING


def _split_start(arrays, copies, n, name):
    k = len(arrays)

    def body(*refs):
        for cp in copies(refs[k + 2:2 * k + 2], refs[k], refs[k + 1]):
            cp.start()
        refs[2 * k + 2][...] = jnp.zeros((8, 128), F32)

    out = _pallas(
        body, name=name,
        out_shape=(pltpu.SemaphoreType.DMA((n,)), pltpu.SemaphoreType.DMA((n,)),
                   *[pltpu.HBM(a.shape, a.dtype) for a in arrays], jax.ShapeDtypeStruct((8, 128), F32)),
        in_specs=(HBM,) * k, out_specs=(SEM, SEM) + (HBM,) * k + (pl.BlockSpec(memory_space=pltpu.VMEM),),
        input_output_aliases={i: i + 2 for i in range(k)},
        compiler_params=pltpu.CompilerParams(has_side_effects=EFFECT),
    )(*[pltpu.with_memory_space_constraint(a, pltpu.HBM) for a in arrays])
    return out[0], out[1], list(out[2:2 + k]), out[2 + k]


def _split_wait(ssem, rsem, arrays, copies, after, name):
    k = len(arrays)

    def body(*refs):
        for cp in copies(refs[:k], refs[k], refs[k + 1]):
            cp.wait_send()
            cp.wait_recv()

    out = _pallas(
        body, name=name, out_shape=tuple(pltpu.HBM(a.shape, a.dtype) for a in arrays),
        in_specs=(HBM,) * k + (SEM, SEM, ANY), out_specs=(HBM,) * k, input_output_aliases={i: i for i in range(k)},
        compiler_params=pltpu.CompilerParams(has_side_effects=EFFECT),
    )(*arrays, ssem, rsem, after)
    return list(out)


def _gather_copies(shapes):
    half = [s[2] // 2 for s in shapes]

    def copies(refs, ssem, rsem):
        x, y, c = _mesh_pos()
        out = []
        for j, (ox, oy) in enumerate(_other_chips(x, y)):
            for a, ref in enumerate(refs):
                slab = ref.at[:, 2 * x + y, pl.ds(c * half[a], half[a]), :]
                out.append(_remote(slab, slab, ssem.at[3 * a + j], rsem.at[3 * a + j], (ox, oy, c)))
        return out

    return copies


def _chips_copies(n_arr):
    def copies(refs, ssem, rsem):
        x, y, c = _mesh_pos()
        out = []
        for j, (ox, oy) in enumerate(_other_chips(x, y)):
            for a in range(n_arr):
                out.append(_remote(refs[a].at[:, 2 * ox + oy], refs[n_arr + a].at[:, 2 * x + y], ssem.at[3 * a + j],
                                   rsem.at[3 * a + j], (ox, oy, c)))
        return out

    return copies


def _halves_copies(shapes):
    n = len(shapes)
    half = [s[2] // 2 for s in shapes]

    def copies(refs, ssem, rsem):
        x, y, c = _mesh_pos()
        return [_remote(refs[a].at[:, :, pl.ds((1 - c) * half[a], half[a]), :], refs[n + a], ssem.at[a], rsem.at[a],
                        (x, y, 1 - c)) for a in range(n)]

    return copies


def _forward_cores(arrays):
    nA = len(arrays)
    half = [a.shape[2] // 2 for a in arrays]

    def body(*refs):
        outs = refs[nA:2 * nA]
        ssem, rsem = refs[2 * nA:]
        x, y, c = _mesh_pos()
        sib = (x, y, 1 - c)
        sends, waits = [], []
        for j, (ox, oy) in enumerate(_other_chips(x, y)):
            for a in range(nA):
                got = outs[a].at[:, 2 * ox + oy, pl.ds(c * half[a], half[a]), :]
                sends.append(_remote(got, got, ssem.at[a, j], rsem.at[a, j], sib))
                theirs = outs[a].at[:, 2 * ox + oy, pl.ds((1 - c) * half[a], half[a]), :]
                waits.append(_remote(theirs, theirs, ssem.at[a, j], rsem.at[a, j], sib))
        for cp in sends:
            cp.start()
        for cp in waits:
            cp.wait_recv()
        for cp in sends:
            cp.wait_send()

    out_shape = [jax.ShapeDtypeStruct(a.shape, a.dtype) for a in arrays]
    scratch = [pltpu.SemaphoreType.DMA((nA, 3)), pltpu.SemaphoreType.DMA((nA, 3))]
    return _comm_call(body, "forward_cores", list(arrays), out_shape, scratch, {a: a for a in range(nA)})


def _exchange_halves(big):
    nA = len(big)
    half = [a.shape[2] // 2 for a in big]

    def body(*refs):
        ins, outs = refs[:nA], refs[nA:2 * nA]
        ssem, rsem = refs[2 * nA:]
        x, y, c = _mesh_pos()
        sib = (x, y, 1 - c)
        sends = [_remote(ins[a].at[:, :, pl.ds((1 - c) * half[a], half[a]), :], outs[a], ssem.at[a], rsem.at[a], sib)
                 for a in range(nA)]
        for cp in sends:
            cp.start()
        for a in range(nA):
            _remote(outs[a], outs[a], ssem.at[a], rsem.at[a], sib).wait_recv()
        for cp in sends:
            cp.wait_send()

    out_shape = [jax.ShapeDtypeStruct((a.shape[0], N_CHIPS, h, a.shape[3]), a.dtype) for a, h in zip(big, half)]
    scratch = [pltpu.SemaphoreType.DMA((nA,)), pltpu.SemaphoreType.DMA((nA,))]
    return _comm_call(body, "exchange_halves", list(big), out_shape, scratch)


def _exchange_final(grads, everywhere, small):
    nA = len(grads)
    n_remote = sum(7 if ev else 1 for ev in everywhere) + 7

    def body(*refs):
        small_ref, outs, gathered = refs[nA], refs[nA + 1:2 * nA + 1], refs[2 * nA + 1]
        ssem, rsem, lsem = refs[2 * nA + 2:]
        x, y, c = _mesh_pos()
        k = 2 * x + y
        sib = (x, y, 1 - c)
        local = pltpu.make_async_copy(small_ref, gathered.at[2 * k + c], lsem.at[0])
        local.start()
        sends, arrivals, waits = [], [], []
        count = [0]

        def sems():
            count[0] += 1
            return ssem.at[count[0] - 1], rsem.at[count[0] - 1]

        def to_sibling(src, mine, theirs):
            sm = sems()
            sends.append(_remote(src, mine, *sm, sib))
            waits.append(_remote(theirs, theirs, *sm, sib))

        def to_everyone(src, place):
            to_sibling(src, place(k, c), place(k, 1 - c))
            for (ox, oy) in _other_chips(x, y):
                ici, d2d = sems(), sems()
                got = place(2 * ox + oy, c)
                sends.append(_remote(src, place(k, c), *ici, (ox, oy, c)))
                arrivals.append((_remote(got, got, *ici, (ox, oy, c)), _remote(got, got, *d2d, sib)))
                theirs = place(2 * ox + oy, 1 - c)
                waits.append(_remote(theirs, theirs, *d2d, sib))

        to_everyone(small_ref, lambda chip, core: gathered.at[2 * chip + core])
        for a in range(nA):
            if everywhere[a]:
                r2 = grads[a].shape[1] // N_DEV

                def place(chip, core, a=a, r2=r2):
                    return outs[a].at[:, pl.ds((2 * chip + core) * r2, r2), :]

                to_everyone(place(k, c), place)
            else:
                r2 = grads[a].shape[1] // 2
                mine = outs[a].at[:, pl.ds(c * r2, r2), :]
                to_sibling(mine, mine, outs[a].at[:, pl.ds((1 - c) * r2, r2), :])
        for cp in sends:
            cp.start()
        for arrived, onward in arrivals:
            arrived.wait_recv()
            onward.start()
        for cp in waits:
            cp.wait_recv()
        for cp in sends + [onward for _, onward in arrivals]:
            cp.wait_send()
        local.wait()

    out_shape = [jax.ShapeDtypeStruct(g.shape, g.dtype) for g in grads]
    out_shape.append(jax.ShapeDtypeStruct((N_DEV,) + small.shape, small.dtype))
    scratch = [pltpu.SemaphoreType.DMA((n_remote,)), pltpu.SemaphoreType.DMA((n_remote,)), pltpu.SemaphoreType.DMA((1,))]
    return _comm_call(body, "exchange_final", list(grads) + [small], out_shape, scratch, {a: a for a in range(nA)})


BLOCK_BYTES = 4 << 20


def _row_tile(rows, cols, mult=16, limit=BLOCK_BYTES):
    best = mult
    for t in range(mult, rows + 1, mult):
        if rows % t == 0 and t * cols * 4 <= limit:
            best = t
    return best


def _add_cores(own, recv, pos):
    L, _, R, C = own.shape
    r2 = R // 2
    tr = _row_tile(r2, C)
    nb = r2 // tr

    def body(pos_ref, a_ref, r_ref, o_ref):
        o_ref[...] = (a_ref[...].astype(F32) + r_ref[...].astype(F32)).astype(BF16)

    blk = (None, None, tr, C)
    grid_spec = pltpu.PrefetchScalarGridSpec(
        num_scalar_prefetch=1, grid=(L, N_CHIPS, nb),
        in_specs=[pl.BlockSpec(blk, lambda l, s, i, pr: (l, s, pr[1] * nb + i, 0)),
                  pl.BlockSpec(blk, lambda l, s, i, pr: (l, s, i, 0))],
        out_specs=pl.BlockSpec(blk, lambda l, s, i, pr: (l, s, i, 0)))
    return _pallas(body, name="add_cores", grid_spec=grid_spec,
                   out_shape=jax.ShapeDtypeStruct((L, N_CHIPS, r2, C), BF16),
                   compiler_params=_params("parallel", "parallel", "parallel"))(pos, own, recv)


def _sum_chips(own, recv, pos, everywhere, layer, nlayers, prev):
    _, _, r2, C = own.shape
    tr = _row_tile(r2, 2 * C)
    nb = r2 // tr

    def body(pos_ref, a_ref, r_ref, *rest):
        acc = None
        for s in range(N_CHIPS):
            term = jnp.where(pos_ref[0] == s, a_ref[...], r_ref[s]).astype(F32)
            acc = term if acc is None else acc + term
        rest[-1][...] = acc

    if everywhere:
        def out_map(i, pr):
            return (layer, (2 * pr[0] + pr[1]) * nb + i, 0)
    else:
        def out_map(i, pr):
            return (layer, pr[1] * nb + i, 0)

    in_specs = [pl.BlockSpec((None, None, tr, C), lambda i, pr: (0, pr[0], i, 0)),
                pl.BlockSpec((None, N_CHIPS, tr, C), lambda i, pr: (0, 0, i, 0))]
    grid_spec = pltpu.PrefetchScalarGridSpec(
        num_scalar_prefetch=1, grid=(nb,), in_specs=in_specs + ([] if prev is None else [ANY]),
        out_specs=pl.BlockSpec((None, tr, C), out_map))
    rows = (N_DEV if everywhere else 2) * r2
    args = (pos, own, recv) if prev is None else (pos, own, recv, prev)
    return _pallas(body, name="sum_chips", grid_spec=grid_spec, out_shape=jax.ShapeDtypeStruct((nlayers, rows, C), F32),
                   input_output_aliases={} if prev is None else {3: 0},
                   compiler_params=_params("parallel"))(*args)


def _sum_devices(parts):
    n, R, C = parts.shape
    tr = _row_tile(R, C * n, 8)

    def body(p_ref, o_ref):
        acc = p_ref[0]
        for s in range(1, n):
            acc = acc + p_ref[s]
        o_ref[...] = acc

    return _pallas(body, name="sum_devices", grid=(R // tr,), in_specs=[pl.BlockSpec((n, tr, C), lambda i: (0, i, 0))],
                   out_specs=pl.BlockSpec((tr, C), lambda i: (i, 0)), out_shape=jax.ShapeDtypeStruct((R, C), F32),
                   compiler_params=_params("parallel"))(parts)


def _adamw(w, g, m, v, name):
    L, R, C = w.shape
    tr = _row_tile(R, C, 8, BLOCK_BYTES // 2)

    def body(w_ref, g_ref, m_ref, v_ref, d_ref, m2_ref, v2_ref, g2_ref):
        gg = g_ref[...]
        g2_ref[...] = gg
        m2 = ADAM_B1 * m_ref[...] + (1.0 - ADAM_B1) * gg
        v2 = ADAM_B2 * v_ref[...] + (1.0 - ADAM_B2) * (gg * gg)
        m_hat = m2 / (1.0 - ADAM_B1 ** ADAM_STEP)
        v_hat = v2 / (1.0 - ADAM_B2 ** ADAM_STEP)
        d_ref[...] = -ADAM_LR * (m_hat / (jnp.sqrt(v_hat) + ADAM_EPS) + ADAM_WD * w_ref[...])
        m2_ref[...] = m2
        v2_ref[...] = v2

    blk = pl.BlockSpec((1, tr, C), lambda l, i: (l, i, 0))
    shp = jax.ShapeDtypeStruct((L, R, C), F32)
    return _pallas(body, name=name, grid=(L, R // tr), in_specs=[blk] * 4, out_specs=[blk] * 4, out_shape=[shp] * 4,
                   compiler_params=_params("parallel", "parallel"))(w, g, m, v)


WEIGHTS = ("norm_even", "w_in_even", "conv_a_w", "conv_a_b", "ln_a_g", "ln_a_b", "pool_w", "pool_b", "pool_scale",
           "w_out_even", "norm_odd", "w_in_odd", "conv_c_w", "conv_c_b", "w_rg", "b_rg", "w_ig", "b_ig", "lru_lambda",
           "w_out_odd", "final_norm")
BIG = ("w_in_even", "w_out_even", "pool_w", "w_in_odd", "w_out_odd", "w_rg", "w_ig")
SMALL = tuple(n for n in WEIGHTS if n not in BIG)
SMALL_SHARDED = ("conv_a_w", "pool_b", "norm_odd", "conv_c_w", "conv_c_b", "b_rg", "b_ig", "lru_lambda")


def _pack(arrs):
    flat = jnp.concatenate([a.reshape(-1) for a in arrs])
    rows = -(-flat.shape[0] // (64 * 128)) * 64
    return jnp.pad(flat, (0, rows * 128 - flat.shape[0])).reshape(rows, 128)


def _unpack(buf, shapes, lead=()):
    flat = buf.reshape(tuple(lead) + (-1,))
    out, o = [], 0
    for s in shapes:
        n = 1
        for d in s:
            n *= d
        out.append(flat[..., o:o + n].reshape(tuple(lead) + tuple(s)))
        o += n
    return out


def _shard(full, axis, k):
    n = full.shape[axis] // N_CHIPS
    return lax.dynamic_slice_in_dim(full, k * n, n, axis)


def kernel(x, norm_even, w_in_even, conv_a_w, conv_a_b, ln_a_g, ln_a_b, pool_w, pool_b, pool_scale, w_out_even, norm_odd, w_in_odd, conv_c_w, conv_c_b, w_rg, b_rg, w_ig, b_ig, lru_lambda, w_out_odd, final_norm, loss_target, m_norm_even, m_w_in_even, m_conv_a_w, m_conv_a_b, m_ln_a_g, m_ln_a_b, m_pool_w, m_pool_b, m_pool_scale, m_w_out_even, m_norm_odd, m_w_in_odd, m_conv_c_w, m_conv_c_b, m_w_rg, m_b_rg, m_w_ig, m_b_ig, m_lru_lambda, m_w_out_odd, m_final_norm, v_norm_even, v_w_in_even, v_conv_a_w, v_conv_a_b, v_ln_a_g, v_ln_a_b, v_pool_w, v_pool_b, v_pool_scale, v_w_out_even, v_norm_odd, v_w_in_odd, v_conv_c_w, v_conv_c_b, v_w_rg, v_b_rg, v_w_ig, v_b_ig, v_lru_lambda, v_w_out_odd, v_final_norm):
    P = dict(locals())
    xi, yi, ci = _mesh_pos()
    k = 2 * xi + yi
    L = w_in_even.shape[0]
    D = D_MODEL

    pos = jnp.stack([k, ci]).astype(jnp.int32)
    depth = 2 * L
    pool_w3 = pool_w.reshape(L, 4 * 64, POOL_GW)

    def cast_group(layer):
        j = layer // 2
        if layer % 2 == 0:
            return [_cast_shard(w_in_even, j, pos), _cast_shard(w_out_even, j, pos), _cast_shard(pool_w3, j, pos)]
        return [_cast_shard(w_in_odd, j, pos), _cast_shard(w_out_odd, j, pos)]

    *group, g_small = _gather_weights(cast_group(0), _pack([P[n] for n in SMALL_SHARDED]))
    full = {}
    for n, a in zip(SMALL_SHARDED, _unpack(g_small, [P[n].shape for n in SMALL_SHARDED], lead=(N_CHIPS,))):
        a = jnp.moveaxis(a, 0, -2)
        full[n] = a.reshape(a.shape[:-2] + (N_CHIPS * a.shape[-1],))

    small_even = dict(norm=norm_even[:, None], conv_w=_pad_rows(full["conv_a_w"], 32),
                      conv_w_rev=_pad_rows(full["conv_a_w"][:, ::-1], 32), conv_b=conv_a_b[:, None], ln_g=ln_a_g[:, None],
                      ln_b=ln_a_b[:, None], pool_b=full["pool_b"].reshape(L, 1, D), pool_scale=pool_scale[:, None])
    small_odd = dict(norm=full["norm_odd"][:, None], conv_w=_pad_rows(full["conv_c_w"], 8),
                     conv_b=full["conv_c_b"][:, None], w_rg=w_rg.astype(BF16), b_rg=full["b_rg"][:, None],
                     w_ig=w_ig.astype(BF16), b_ig=full["b_ig"][:, None], lam=full["lru_lambda"][:, None])

    def small_weights(layer, group):
        if layer % 2 == 0:
            pw = group[2].reshape(N_CHIPS, 4, 64, POOL_GW).transpose(1, 0, 2, 3).reshape(4, POOL_GW, POOL_GW)
            return dict(small_even, sl=layer // 2, pool_w=pw)
        return dict(small_odd, sl=layer // 2)

    no_token = jnp.zeros((8, 128), F32)
    h = x[0]
    saved, big_w, small_w = [], [], []
    for layer in range(depth):
        token = no_token
        if layer + 1 < depth:
            nxt = cast_group(layer + 1)
            copies = _gather_copies([a.shape for a in nxt])
            ssem, rsem, nxt, token = _split_start(nxt, copies, 3 * len(nxt), "gather_start%d" % (layer + 1))
        small_w.append(small_weights(layer, group))
        big_w.append((group[0], group[1].reshape(1, -1, D)))
        h, sv = _layer_fwd(layer % 2 == 0, h, small_w[layer], *big_w[layer], token)
        saved.append(sv)
        if layer + 1 < depth:
            group = _forward_cores(_split_wait(ssem, rsem, nxt, copies, h, "gather_wait%d" % (layer + 1)))

    dh, dhb, d_final, loss_part = _loss_head(h, final_norm[None], loss_target[0])
    everywhere = [False, False, False, False, False, True, True]
    final = [None] * len(everywhere)
    small_of = [None] * depth

    def finish(pending, after):
        ssem, rsem, arrs, copies, slots, pj, pl_ = pending
        arrs = _split_wait(ssem, rsem, arrs, copies, after, "chips_wait%d" % pl_)
        for a, r, s in zip(arrs[:len(slots)], arrs[len(slots):], slots):
            final[s] = _sum_chips(a, r, pos, everywhere[s], pj, L, final[s])

    pending = None
    token = no_token
    for layer in reversed(range(depth)):
        j = layer // 2
        even_layer = layer % 2 == 0
        dp, dw_in, dw_out, sm = _layer_bwd_weights(even_layer, saved[layer], small_w[layer], big_w[layer][1], dhb, token)
        if even_layer:
            dpw = sm["pool_w"].reshape(4, N_CHIPS, 64, POOL_GW).transpose(1, 0, 2, 3)
            parts = [dw_in, dw_out.reshape(1, N_CHIPS, -1, D), dpw.reshape(1, N_CHIPS, 4 * 64, POOL_GW).astype(BF16)]
            slots = [0, 1, 2]
        else:
            parts = [dw_in, dw_out.reshape(1, N_CHIPS, -1, D),
                     sm["w_rg"].reshape(1, N_CHIPS, -1, LRU_HD).astype(BF16),
                     sm["w_ig"].reshape(1, N_CHIPS, -1, LRU_HD).astype(BF16)]
            slots = [3, 4, 5, 6]
        n = len(parts)
        if layer > 0:
            hcopies = _halves_copies([a.shape for a in parts])
            hland = [lax.empty((1, N_CHIPS, a.shape[2] // 2, a.shape[3]), a.dtype) for a in parts]
            hs, hr, harrs, htoken = _split_start(parts + hland, hcopies, n, "halves_start%d" % layer)
            dh, dhb, sm["norm"] = _layer_bwd_input(even_layer, saved[layer], small_w[layer], big_w[layer][0], dp, dh,
                                                   htoken)
            harrs = _split_wait(hs, hr, harrs, hcopies, dh, "halves_wait%d" % layer)
            parts, recv = harrs[:n], harrs[n:]
        else:
            recv = _exchange_halves(parts)
        pair = [_add_cores(a, r, pos) for a, r in zip(parts, recv)]
        copies = _chips_copies(n)
        land = [lax.empty(a.shape, a.dtype) for a in pair]
        ssem, rsem, arrs, token = _split_start(pair + land, copies, 3 * n, "chips_start%d" % layer)
        if layer == 0:
            dh, dhb, sm["norm"] = _layer_bwd_input(even_layer, saved[layer], small_w[layer], big_w[layer][0], dp, dh, token)
        small_of[layer] = sm
        if pending is not None:
            finish(pending, dh)
        pending = (ssem, rsem, arrs, copies, slots, j, layer)
    grad_x = dh
    small_g = []
    for jj in range(L):
        ge, go = small_of[2 * jj], small_of[2 * jj + 1]
        small_g += [ge["conv_w"].reshape(32, 8, D).sum(axis=1)[:CONV_K], ge["vec"][0:5], ge["norm"], go["vec"], go["norm"]]
    small_g += [d_final, loss_part]
    small_shapes = [a.shape for a in small_g]
    packed_small = _pack(small_g)
    finish(pending, packed_small)
    *gw, recv_small = _exchange_final(final, everywhere, packed_small)
    sg = _unpack(_sum_devices(recv_small), small_shapes)

    grads = dict(w_in_even=gw[0], w_out_even=gw[1], pool_w=gw[2].reshape(pool_w.shape), w_in_odd=gw[3], w_out_odd=gw[4],
                 w_rg=gw[5].reshape(w_rg.shape), w_ig=gw[6].reshape(w_ig.shape), final_norm=sg[-2][0])
    loss = sg[-1][0, 0]
    ev = [sg[5 * j + 1] for j in range(L)]
    ov = [sg[5 * j + 3] for j in range(L)]
    grads["conv_a_w"] = _shard(jnp.stack([sg[5 * j] for j in range(L)]), 2, k)
    grads["norm_even"] = jnp.stack([sg[5 * j + 2][0] for j in range(L)])
    grads["norm_odd"] = _shard(jnp.stack([sg[5 * j + 4][0] for j in range(L)]), 1, k)
    for r, n in enumerate(("conv_a_b", "ln_a_g", "ln_a_b", "pool_scale")):
        grads[n] = jnp.stack([e[r] for e in ev])
    grads["pool_b"] = _shard(jnp.stack([e[4].reshape(4, POOL_GW) for e in ev]), 2, k)
    grads["conv_c_w"] = _shard(jnp.stack([o[0:4] for o in ov]), 2, k)
    for r, n in zip((4, 5, 6, 7), ("conv_c_b", "b_rg", "b_ig", "lru_lambda")):
        grads[n] = _shard(jnp.stack([o[r] for o in ov]), 1, k)

    delta, new_m, new_v = {}, {}, {}
    for n in BIG:
        s3 = (L, -1, P[n].shape[-1])
        d, m2, v2, g2 = _adamw(P[n].reshape(s3), grads[n].reshape(s3), P["m_" + n].reshape(s3), P["v_" + n].reshape(s3),
                               "adamw")
        delta[n], new_m[n], new_v[n] = d.reshape(P[n].shape), m2.reshape(P[n].shape), v2.reshape(P[n].shape)
        grads[n] = g2.reshape(P[n].shape)
    shapes = [P[n].shape for n in SMALL]
    packed = [_pack([src[n] for n in SMALL])[None] for src in
              (P, grads, {n: P["m_" + n] for n in SMALL}, {n: P["v_" + n] for n in SMALL})]
    for res, out in zip(_adamw(*packed, "adamw_small")[:3], (delta, new_m, new_v)):
        for n, a in zip(SMALL, _unpack(res[0], shapes)):
            out[n] = a

    return (loss, grad_x[None], *[grads[n] for n in WEIGHTS], *[delta[n] for n in WEIGHTS],
            *[new_m[n] for n in WEIGHTS], *[new_v[n] for n in WEIGHTS])
```

```python
import jax
import jax.numpy as jnp
from jax import lax
from jax.experimental import pallas as pl
from jax.experimental.pallas import tpu as pltpu

F32 = jnp.float32
BF16 = jnp.bfloat16
MESH = pl.DeviceIdType.MESH

D_MODEL = 1024
N_CHIPS = 4
N_DEV = 8
EPS_RMS = 1e-6
EPS_LN = 1e-5
CONV_K = 31
POOL_WINDOWS = (2, 4, 8, 16)
POOL_GW = 256
LRU_HEADS = 12
LRU_HD = 128
W_LRU = LRU_HEADS * LRU_HD
LRU_CONV_K = 4
LRU_C = 8.0
ADAM_LR = 0.001
ADAM_B1 = 0.9
ADAM_B2 = 0.999
ADAM_EPS = 1e-08
ADAM_WD = 0.01
ADAM_STEP = 10

VMEM_LIMIT_BYTES = 56 * 1024 * 1024
ROW_TILE = 1024
MIX_TILE = 256
EVEN_HALO = 32
ODD_HALO = 8


def _pallas(body, **kw):
    return pl.pallas_call(body, **kw)


def _params(*sem):
    return pltpu.CompilerParams(dimension_semantics=sem if sem else None, vmem_limit_bytes=VMEM_LIMIT_BYTES)


def _sigmoid(x):
    return 0.5 * jnp.tanh(0.5 * x) + 0.5


def _dsilu(x, s):
    return s * (1.0 + x * (1.0 - s))


def _nt(a, b):
    return lax.dot_general(a, b, (((1,), (1,)), ((), ())), preferred_element_type=F32)


def _tn(a, b):
    return lax.dot_general(a, b, (((0,), (0,)), ((), ())), preferred_element_type=F32)


def _in_proj(h, g, glayer, wg, layer, after, name):
    T, D = h.shape
    _, nblk, _, nb = wg.shape

    nrow = T // ROW_TILE

    def body(h_ref, g_ref, w_ref, after_ref, p_ref, n_ref, n_all):
        j, i = pl.program_id(0), pl.program_id(1)

        @pl.when(j == 0)
        def _():
            x = h_ref[...]
            r = lax.rsqrt(jnp.mean(x * x, axis=-1, keepdims=True) + EPS_RMS)
            nn = (x * r * g_ref[...]).astype(BF16)
            n_ref[...] = nn
            n_all[i] = nn

        p_ref[...] = jnp.dot(n_all[i], w_ref[0], preferred_element_type=F32)

    def rows_once(j, i):
        return (jnp.where(j == 0, i, nrow - 1), 0)

    return _pallas(
        body, name=name, grid=(nblk, nrow),
        in_specs=[pl.BlockSpec((ROW_TILE, D), rows_once), pl.BlockSpec((None, 1, D), lambda j, i: (glayer, 0, 0)),
                  pl.BlockSpec((None, 1, D, nb), lambda j, i: (layer, j, 0, 0)),
                  pl.BlockSpec((8, 128), lambda j, i: (0, 0))],
        out_specs=[pl.BlockSpec((ROW_TILE, nb), lambda j, i: (i, j)), pl.BlockSpec((ROW_TILE, D), rows_once)],
        out_shape=[jax.ShapeDtypeStruct((T, nblk * nb), F32), jax.ShapeDtypeStruct((T, D), BF16)],
        scratch_shapes=[pltpu.VMEM((nrow, ROW_TILE, D), BF16)],
        compiler_params=_params("arbitrary", "arbitrary"))(h, g, wg, after)


def _dn_proj(dp, wg, layer, h, g, glayer, dres, after, name):
    T, D = h.shape
    _, nblk, _, nb = wg.shape

    nrow = T // ROW_TILE

    def body(dp_ref, w_ref, h_ref, g_ref, dres_ref, after_ref, dh_ref, dhb_ref, dg_ref, acc_ref):
        j, i = pl.program_id(0), pl.program_id(1)
        part = _nt(dp_ref[...], w_ref[0])

        @pl.when(j == 0)
        def _():
            acc_ref[i] = part

        @pl.when(j > 0)
        def _():
            acc_ref[i] += part

        @pl.when(j == nblk - 1)
        def _():
            x = h_ref[...]
            r = lax.rsqrt(jnp.mean(x * x, axis=-1, keepdims=True) + EPS_RMS)
            dn = acc_ref[i]
            q = dn * g_ref[...]
            dh = dres_ref[...] + r * q - x * ((r * r * r) * jnp.mean(q * x, axis=-1, keepdims=True))
            dh_ref[...] = dh
            dhb_ref[...] = dh.astype(BF16)
            dgp = jnp.sum(dn * (x * r), axis=0, keepdims=True)

            @pl.when(i == 0)
            def _():
                dg_ref[...] = dgp

            @pl.when(i > 0)
            def _():
                dg_ref[...] += dgp

    def rows_last(j, i):
        return (jnp.where(j == nblk - 1, i, 0), 0)

    return _pallas(
        body, name=name, grid=(nblk, nrow),
        in_specs=[pl.BlockSpec((ROW_TILE, nb), lambda j, i: (i, j)),
                  pl.BlockSpec((None, 1, D, nb), lambda j, i: (layer, j, 0, 0)),
                  pl.BlockSpec((ROW_TILE, D), rows_last), pl.BlockSpec((None, 1, D), lambda j, i: (glayer, 0, 0)),
                  pl.BlockSpec((ROW_TILE, D), rows_last), pl.BlockSpec((8, 128), lambda j, i: (0, 0))],
        out_specs=[pl.BlockSpec((ROW_TILE, D), rows_last), pl.BlockSpec((ROW_TILE, D), rows_last),
                   pl.BlockSpec((1, D), lambda j, i: (0, 0))],
        out_shape=[jax.ShapeDtypeStruct((T, D), F32), jax.ShapeDtypeStruct((T, D), BF16),
                   jax.ShapeDtypeStruct((1, D), F32)],
        scratch_shapes=[pltpu.VMEM((nrow, ROW_TILE, D), F32)],
        compiler_params=_params("arbitrary", "arbitrary"))(dp, wg, h, g, dres, after)


def _dw_in(n, dp, nblk, layer, nlayers, prev, name):
    T, D = n.shape
    nb = dp.shape[1] // nblk
    ta = D

    def body(n_ref, dp_ref, *rest):
        rest[-1][0] = _tn(n_ref[...], dp_ref[...]).astype(BF16)

    in_specs = [pl.BlockSpec((T, ta), lambda j, i: (0, i)), pl.BlockSpec((T, nb), lambda j, i: (0, j))]
    args = (n, dp) if prev is None else (n, dp, prev)
    return _pallas(
        body, name=name, grid=(nblk, D // ta), in_specs=in_specs + ([] if prev is None else [ANY]),
        out_specs=pl.BlockSpec((None, 1, ta, nb), lambda j, i: (layer, j, i, 0)),
        out_shape=jax.ShapeDtypeStruct((nlayers, nblk, D, nb), BF16),
        input_output_aliases={} if prev is None else {2: 0},
        compiler_params=_params("parallel", "parallel"))(*args)


def _dw_out(y, dout, layer, nlayers, prev, name):
    T, K = y.shape
    D = dout.shape[1]
    tk = 512

    def body(y_ref, d_ref, *rest):
        rest[-1][...] = _tn(y_ref[...], d_ref[...]).astype(BF16)

    in_specs = [pl.BlockSpec((T, tk), lambda i: (0, i)), pl.BlockSpec((T, D), lambda i: (0, 0))]
    args = (y, dout) if prev is None else (y, dout, prev)
    return _pallas(
        body, name=name, grid=(K // tk,), in_specs=in_specs + ([] if prev is None else [ANY]),
        out_specs=pl.BlockSpec((None, tk, D), lambda i: (layer, i, 0)),
        out_shape=jax.ShapeDtypeStruct((nlayers, K, D), BF16),
        input_output_aliases={} if prev is None else {2: 0},
        compiler_params=_params("parallel"))(*args)


def _loss_head(h, g, tgt):
    T, D = h.shape
    tm = MIX_TILE

    def body(h_ref, g_ref, t_ref, dh_ref, dhb_ref, dg_ref, loss_ref):
        i = pl.program_id(0)
        x = h_ref[...]
        gg = g_ref[...]
        r = lax.rsqrt(jnp.mean(x * x, axis=-1, keepdims=True) + EPS_RMS)
        xr = x * r
        e = xr * gg - t_ref[...]
        lp = 0.5 * jnp.sum(jnp.mean(e * e, axis=-1, keepdims=True), axis=0, keepdims=True)
        dn = e * (1.0 / D)
        q = dn * gg
        dh = r * q - x * ((r * r * r) * jnp.mean(q * x, axis=-1, keepdims=True))
        dh_ref[...] = dh
        dhb_ref[...] = dh.astype(BF16)
        dgp = jnp.sum(dn * xr, axis=0, keepdims=True)

        @pl.when(i == 0)
        def _():
            dg_ref[...] = dgp
            loss_ref[...] = lp

        @pl.when(i > 0)
        def _():
            dg_ref[...] += dgp
            loss_ref[...] += lp

    return _pallas(
        body, name="loss_head", grid=(T // tm,),
        in_specs=[pl.BlockSpec((tm, D), lambda i: (i, 0)), pl.BlockSpec((1, D), lambda i: (0, 0)),
                  pl.BlockSpec((tm, D), lambda i: (i, 0))],
        out_specs=[pl.BlockSpec((tm, D), lambda i: (i, 0)), pl.BlockSpec((tm, D), lambda i: (i, 0)),
                   pl.BlockSpec((1, D), lambda i: (0, 0)), pl.BlockSpec((1, 1), lambda i: (0, 0))],
        out_shape=[jax.ShapeDtypeStruct((T, D), F32), jax.ShapeDtypeStruct((T, D), BF16),
                   jax.ShapeDtypeStruct((1, D), F32), jax.ShapeDtypeStruct((1, 1), F32)],
        compiler_params=_params("arbitrary"))(h, g, tgt)


def _shift_up(x, j):
    return x if j == 0 else pltpu.roll(x, x.shape[0] - j, 0)


def _shift_down(x, j):
    return x if j == 0 else pltpu.roll(x, j, 0)


def _fill_shifted(dst_ref, src_ref):
    rows = dst_ref.shape[1]
    for s in range(8):
        dst_ref[s] = src_ref[pl.ds(s, rows), :]


def _fill_taps(wb_ref, w_ref):
    for k in range(w_ref.shape[0]):
        wb_ref[k] = jnp.broadcast_to(w_ref[k:k + 1, :], wb_ref.shape[1:])


def _tap_sum(sh_ref, wb_ref, r0, nrows, offsets):
    accs = [None] * (nrows // 8)
    for k, o in enumerate(offsets):
        wk = wb_ref[k]
        for u in range(nrows // 8):
            term = wk * sh_ref[o % 8, pl.ds(r0 + (o // 8) * 8 + 8 * u, 8), :]
            accs[u] = term if accs[u] is None else accs[u] + term
    return jnp.concatenate(accs, axis=0)


def _pool_sums(vx, up):
    sh = _shift_up if up else _shift_down
    outs = []
    for gi, w in enumerate(POOL_WINDOWS):
        s = vx[:, gi * POOL_GW:(gi + 1) * POOL_GW]
        j = 1
        while j < w:
            s = s + sh(s, j)
            j *= 2
        outs.append(s)
    return outs


def _inv_count(row0, nrows):
    pos = (row0 + 1 + lax.broadcasted_iota(jnp.int32, (nrows, 1), 0)).astype(F32)
    return [1.0 / jnp.minimum(pos, float(w)) for w in POOL_WINDOWS]


def _even_mixer_fwd(p, h, w_out, sl, cw, cb, lg, lb, pw, pb, sc, name):
    T = p.shape[0]
    C = D_MODEL
    tT, HL = MIX_TILE, EVEN_HALO
    hb = tT // HL
    chunk = 32

    def body(pm_ref, ph_ref, cw_ref, cb_ref, lg_ref, lb_ref, pw_ref, pb_ref, sc_ref, h_ref, wo_ref, y_ref, u1_ref,
             hn_ref, u0x_ref, sh_ref, wb_ref):
        i = pl.program_id(0)
        keep = (i > 0).astype(F32)

        @pl.when(i == 0)
        def _():
            _fill_taps(wb_ref, cw_ref)

        u0x_ref[0:HL] = ph_ref[:, 0:C] * _sigmoid(ph_ref[:, C:2 * C]) * keep
        u0x_ref[HL:HL + tT] = pm_ref[:, 0:C] * _sigmoid(pm_ref[:, C:2 * C])
        u0x_ref[HL + tT:HL + tT + 8] = jnp.zeros((8, C), F32)
        _fill_shifted(sh_ref, u0x_ref)
        offs = [HL - (CONV_K - 1) + k for k in range(CONV_K)]

        def conv_chunk(c, carry):
            r0 = pl.multiple_of(c * chunk, chunk)
            u1_ref[pl.ds(r0, chunk), :] = _tap_sum(sh_ref, wb_ref, r0, chunk, offs) + cb_ref[...]
            return carry

        lax.fori_loop(0, tT // chunk, conv_chunk, 0)
        u1 = u1_ref[...]
        mu = jnp.mean(u1, axis=-1, keepdims=True)
        xc = u1 - mu
        rs = lax.rsqrt(jnp.mean(xc * xc, axis=-1, keepdims=True) + EPS_LN)
        u2 = xc * rs * lg_ref[...] + lb_ref[...]
        u3 = u2 * _sigmoid(u2)
        ag = pm_ref[:, 2 * C:3 * C]
        y_ref[:, 0:C] = (u3 * (ag * _sigmoid(ag))).astype(BF16)
        vx = jnp.concatenate([ph_ref[:, 3 * C:4 * C] * keep, pm_ref[:, 3 * C:4 * C]], axis=0)
        sums = _pool_sums(vx, up=False)
        inv = _inv_count(i * tT, tT)
        for gi in range(len(POOL_WINDOWS)):
            cols = slice(gi * POOL_GW, (gi + 1) * POOL_GW)
            d0 = sums[gi][HL:] * inv[gi] - vx[HL:, cols]
            d1 = jnp.dot(d0.astype(BF16), pw_ref[gi], preferred_element_type=F32) + pb_ref[:, cols]
            bg = pm_ref[:, 4 * C + gi * POOL_GW:4 * C + (gi + 1) * POOL_GW]
            y_ref[:, C + gi * POOL_GW:C + (gi + 1) * POOL_GW] = (d1 * sc_ref[:, cols] * (bg * _sigmoid(bg))).astype(BF16)
        hn_ref[...] = h_ref[...] + jnp.dot(y_ref[...], wo_ref[...], preferred_element_type=F32)

    vec = pl.BlockSpec((None, 1, C), lambda i: (sl, 0, 0))
    rows = pl.BlockSpec((tT, C), lambda i: (i, 0))
    return _pallas(
        body, name=name, grid=(T // tT,),
        in_specs=[pl.BlockSpec((tT, 5 * C), lambda i: (i, 0)),
                  pl.BlockSpec((HL, 5 * C), lambda i: (jnp.maximum(i * hb - 1, 0), 0)),
                  pl.BlockSpec((None, 32, C), lambda i: (sl, 0, 0)), vec, vec, vec,
                  pl.BlockSpec((4, POOL_GW, POOL_GW), lambda i: (0, 0, 0)), vec, vec,
                  rows, pl.BlockSpec((None, 2 * C, C), lambda i: (0, 0, 0))],
        out_specs=[pl.BlockSpec((tT, 2 * C), lambda i: (i, 0)), rows, rows],
        out_shape=[jax.ShapeDtypeStruct((T, 2 * C), BF16), jax.ShapeDtypeStruct((T, C), F32),
                   jax.ShapeDtypeStruct((T, C), F32)],
        scratch_shapes=[pltpu.VMEM((HL + tT + 8, C), F32), pltpu.VMEM((8, HL + tT, C), F32),
                        pltpu.VMEM((32, 8, C), F32)],
        compiler_params=_params("arbitrary"))(p, p, cw, cb, lg, lb, pw, pb, sc, h, w_out)


def _even_mixer_bwd(p, u1, dout, w_out, after, sl, cwr, lg, lb, pw, pb, sc, name):
    T = p.shape[0]
    C = D_MODEL
    tT, HL = MIX_TILE, EVEN_HALO
    hb = tT // HL
    nT = T // tT
    R1 = tT + HL
    chunk = 32

    def body(pm_ref, pp_ref, pn_ref, u1m_ref, u1n_ref, dom_ref, don_ref, wo_ref, after_ref, cwr_ref, lg_ref, lb_ref,
             pw_ref, pb_ref, sc_ref, dp_ref, dcw_ref, dvec_ref, dpw_ref, x_ref, sh_ref, du0_ref, wb_ref):
        i = pl.program_id(0)
        dy = _nt(jnp.concatenate([dom_ref[...], don_ref[...]], axis=0), wo_ref[...])

        @pl.when(i == 0)
        def _():
            _fill_taps(wb_ref, cwr_ref)

        keep_prev = (i > 0).astype(F32)
        keep_next = (i < nT - 1).astype(F32)
        row = lax.broadcasted_iota(jnp.int32, (R1, 1), 0)
        live = jnp.where(row < tT, 1.0, keep_next)

        def cat(m, n):
            return jnp.concatenate([m, n], axis=0)

        u1 = cat(u1m_ref[...], u1n_ref[...])
        mu = jnp.mean(u1, axis=-1, keepdims=True)
        xc = u1 - mu
        rs = lax.rsqrt(jnp.mean(xc * xc, axis=-1, keepdims=True) + EPS_LN)
        xh = xc * rs
        u2 = xh * lg_ref[...] + lb_ref[...]
        s2 = _sigmoid(u2)
        u3 = u2 * s2
        ag = cat(pm_ref[:, 2 * C:3 * C], pn_ref[:, 2 * C:3 * C])
        sa = _sigmoid(ag)
        dya = dy[:, 0:C]
        dp_ref[:, 2 * C:3 * C] = (dya * u3 * _dsilu(ag, sa))[0:tT].astype(BF16)
        du2 = dya * (ag * sa) * _dsilu(u2, s2)
        dlg = jnp.sum((du2 * xh)[0:tT], axis=0, keepdims=True)
        dlb = jnp.sum(du2[0:tT], axis=0, keepdims=True)
        dxh = du2 * lg_ref[...]
        du1 = rs * (dxh - jnp.mean(dxh, axis=-1, keepdims=True) - xh * jnp.mean(dxh * xh, axis=-1, keepdims=True))
        du1 = du1 * live
        dcb = jnp.sum(du1[0:tT], axis=0, keepdims=True)
        x_ref[0:R1] = du1
        x_ref[R1:R1 + 8] = jnp.zeros((8, C), F32)
        _fill_shifted(sh_ref, x_ref)

        def du0_chunk(c, carry):
            r0 = pl.multiple_of(c * chunk, chunk)
            du0_ref[pl.ds(r0, chunk), :] = _tap_sum(sh_ref, wb_ref, r0, chunk, list(range(CONV_K)))
            return carry

        lax.fori_loop(0, tT // chunk, du0_chunk, 0)
        av, agl = pm_ref[:, 0:C], pm_ref[:, C:2 * C]
        sg = _sigmoid(agl)
        du0 = du0_ref[...]
        dp_ref[:, 0:C] = (du0 * sg).astype(BF16)
        dp_ref[:, C:2 * C] = (du0 * av * sg * (1.0 - sg)).astype(BF16)
        du0_ref[...] = du1[0:tT]
        x_ref[0:HL] = pp_ref[:, 0:C] * _sigmoid(pp_ref[:, C:2 * C]) * keep_prev
        x_ref[HL:HL + tT] = av * sg
        x_ref[HL + tT:HL + tT + 8] = jnp.zeros((8, C), F32)
        _fill_shifted(sh_ref, x_ref)

        @pl.when(i == 0)
        def _():
            dcw_ref[...] = jnp.zeros_like(dcw_ref)

        for k0 in range(0, CONV_K, 2):
            taps = [k for k in (k0, k0 + 1) if k < CONV_K]
            offs = [HL - (CONV_K - 1) + k for k in taps]

            def dw_chunk(c, accs, offs=offs):
                r0 = pl.multiple_of(c * 64, 64)
                accs = list(accs)
                for u in range(0, 64, 8):
                    d = du0_ref[pl.ds(r0 + u, 8), :]
                    for t, o in enumerate(offs):
                        accs[t] = accs[t] + d * sh_ref[o % 8, pl.ds(r0 + u + (o // 8) * 8, 8), :]
                return tuple(accs)

            sums = lax.fori_loop(0, tT // 64, dw_chunk, tuple(jnp.zeros((8, C), F32) for _ in taps))
            for k, acc in zip(taps, sums):
                dcw_ref[8 * k:8 * k + 8, :] += acc

        bg = cat(pm_ref[:, 4 * C:5 * C], pn_ref[:, 4 * C:5 * C])
        sb = _sigmoid(bg)
        dyb = dy[:, C:2 * C]
        dyb0 = dyb * (bg * sb)
        dd1 = dyb0 * sc_ref[...]
        dpb = jnp.sum(dd1[0:tT], axis=0, keepdims=True)
        inv1 = _inv_count(i * tT, R1)
        z_parts, dd0_parts = [], []
        for gi in range(len(POOL_WINDOWS)):
            cols = slice(gi * POOL_GW, (gi + 1) * POOL_GW)
            dd0 = _nt(dd1[:, cols].astype(BF16), pw_ref[gi])
            dd0_parts.append(dd0)
            z_parts.append(dd0 * inv1[gi] * live)
        fsum = _pool_sums(jnp.concatenate(z_parts, axis=1), up=True)
        vx = cat(pp_ref[:, 3 * C:4 * C] * keep_prev, pm_ref[:, 3 * C:4 * C])
        sums = _pool_sums(vx, up=False)
        inv0 = _inv_count(i * tT, tT)
        dsc_parts = []
        for gi in range(len(POOL_WINDOWS)):
            cols = slice(gi * POOL_GW, (gi + 1) * POOL_GW)
            dp_ref[:, 3 * C + gi * POOL_GW:3 * C + (gi + 1) * POOL_GW] = (fsum[gi][0:tT] - dd0_parts[gi][0:tT]).astype(BF16)
            d0 = (sums[gi][HL:] * inv0[gi] - vx[HL:, cols]).astype(BF16)
            d1 = jnp.dot(d0, pw_ref[gi], preferred_element_type=F32) + pb_ref[:, cols]
            bgm, sbm = bg[0:tT, cols], sb[0:tT, cols]
            dp_ref[:, 4 * C + gi * POOL_GW:4 * C + (gi + 1) * POOL_GW] = (
                dyb[0:tT, cols] * d1 * sc_ref[:, cols] * _dsilu(bgm, sbm)).astype(BF16)
            dsc_parts.append(jnp.sum(dyb0[0:tT, cols] * d1, axis=0, keepdims=True))
            dpw_g = _tn(d0, dd1[0:tT, cols].astype(BF16))

            @pl.when(i == 0)
            def _(gi=gi, dpw_g=dpw_g):
                dpw_ref[gi] = dpw_g

            @pl.when(i > 0)
            def _(gi=gi, dpw_g=dpw_g):
                dpw_ref[gi] += dpw_g

        dsc = jnp.concatenate(dsc_parts, axis=1)
        vecs = jnp.concatenate([dcb, dlg, dlb, dsc, dpb, jnp.zeros((3, C), F32)], axis=0)

        @pl.when(i == 0)
        def _():
            dvec_ref[...] = vecs

        @pl.when(i > 0)
        def _():
            dvec_ref[...] += vecs

    vec = pl.BlockSpec((None, 1, C), lambda i: (sl, 0, 0))
    taps = pl.BlockSpec((None, 32, C), lambda i: (sl, 0, 0))

    def prev_blk(i):
        return (jnp.maximum(i * hb - 1, 0), 0)

    def next_blk(i):
        return (jnp.minimum((i + 1) * hb, T // HL - 1), 0)

    return _pallas(
        body, name=name, grid=(nT,),
        in_specs=[pl.BlockSpec((tT, 5 * C), lambda i: (i, 0)), pl.BlockSpec((HL, 5 * C), prev_blk),
                  pl.BlockSpec((HL, 5 * C), next_blk),
                  pl.BlockSpec((tT, C), lambda i: (i, 0)), pl.BlockSpec((HL, C), next_blk),
                  pl.BlockSpec((tT, C), lambda i: (i, 0)), pl.BlockSpec((HL, C), next_blk),
                  pl.BlockSpec((None, 2 * C, C), lambda i: (0, 0, 0)), pl.BlockSpec((8, 128), lambda i: (0, 0)),
                  taps, vec, vec, pl.BlockSpec((4, POOL_GW, POOL_GW), lambda i: (0, 0, 0)), vec, vec],
        out_specs=[pl.BlockSpec((tT, 5 * C), lambda i: (i, 0)), pl.BlockSpec((32 * 8, C), lambda i: (0, 0)),
                   pl.BlockSpec((8, C), lambda i: (0, 0)), pl.BlockSpec((4, POOL_GW, POOL_GW), lambda i: (0, 0, 0))],
        out_shape=[jax.ShapeDtypeStruct((T, 5 * C), BF16), jax.ShapeDtypeStruct((32 * 8, C), F32),
                   jax.ShapeDtypeStruct((8, C), F32), jax.ShapeDtypeStruct((4, POOL_GW, POOL_GW), F32)],
        scratch_shapes=[pltpu.VMEM((R1 + 8, C), F32), pltpu.VMEM((8, R1, C), F32), pltpu.VMEM((tT, C), F32),
                        pltpu.VMEM((32, 8, C), F32)],
        compiler_params=_params("arbitrary"))(p, p, p, u1, u1, dout, dout, w_out, after, cwr, lg, lb, pw, pb, sc)


def _softplus(z):
    u = jnp.exp(-jnp.abs(z))
    w = 1.0 + u
    l1p = jnp.where(w == 1.0, u, u * jnp.log(w) / jnp.where(w == 1.0, 1.0, w - 1.0))
    return jnp.maximum(z, 0.0) + l1p


def _lru_gates(xrx, cw_ref, cb_ref, wr_ref, br_ref, wi_ref, bi_ref, lam_ref):
    HL = ODD_HALO
    xc = cb_ref[...] + cw_ref[LRU_CONV_K - 1:LRU_CONV_K, :] * xrx[HL:]
    for k in range(LRU_CONV_K - 1):
        xc = xc + cw_ref[k:k + 1, :] * _shift_down(xrx, LRU_CONV_K - 1 - k)[HL:]
    xcb = xc.astype(BF16)
    rp, ip = [], []
    for hd in range(LRU_HEADS):
        cols = slice(hd * LRU_HD, (hd + 1) * LRU_HD)
        rp.append(jnp.dot(xcb[:, cols], wr_ref[hd], preferred_element_type=F32))
        ip.append(jnp.dot(xcb[:, cols], wi_ref[hd], preferred_element_type=F32))
    r = _sigmoid(jnp.concatenate(rp, axis=1) + br_ref[...])
    ig = _sigmoid(jnp.concatenate(ip, axis=1) + bi_ref[...])
    sp = _softplus(-lam_ref[...])
    log_a = (-LRU_C) * r * sp
    a = jnp.exp(log_a)
    m2 = jnp.maximum(-jnp.tanh(log_a) * (a * a + 1.0), 1e-30)
    inv_mult = lax.rsqrt(m2)
    return xc, xcb, r, ig, sp, a, m2 * inv_mult, inv_mult


def _group_scan(a, b, reverse):
    n, w = a.shape
    a, b = a.reshape(n // 8, 8, w), b.reshape(n // 8, 8, w)
    pos = lax.broadcasted_iota(jnp.int32, (1, 8, 1), 1)
    s = 1
    while s < 8:
        ok = (pos < 8 - s) if reverse else (pos >= s)
        shift = (8 - s) if reverse else s
        a_sh = jnp.where(ok, pltpu.roll(a, shift, 1), 1.0)
        b_sh = jnp.where(ok, pltpu.roll(b, shift, 1), 0.0)
        b = a * b_sh + b
        a = a * a_sh
        s *= 2
    return a.reshape(n, w), b.reshape(n, w)


def _apply_carries(a_ref, b_ref, out_ref, c0, reverse):
    ng = a_ref.shape[0] // 8

    def step(t, c):
        r0 = pl.multiple_of(((ng - 1 - t) if reverse else t) * 8, 8)
        x = a_ref[pl.ds(r0, 8), :] * c + b_ref[pl.ds(r0, 8), :]
        out_ref[pl.ds(r0, 8), :] = x
        return x[0:1, :] if reverse else x[7:8, :]

    return lax.fori_loop(0, ng, step, c0)


def _odd_mixer_fwd(p, h, w_out, sl, cw, cb, wr, br, wi, bi, lam, name):
    T = p.shape[0]
    W = W_LRU
    D = D_MODEL
    tT, HL = MIX_TILE, ODD_HALO
    hb = tT // HL

    def body(pm_ref, ph_ref, cw_ref, cb_ref, wr_ref, br_ref, wi_ref, bi_ref, lam_ref, h_ref, wo_ref, y_ref, hs_ref,
             hn_ref, carry_ref, sa_ref, sb_ref):
        i = pl.program_id(0)
        keep = (i > 0).astype(F32)

        @pl.when(i == 0)
        def _():
            carry_ref[...] = jnp.zeros_like(carry_ref)

        xrx = jnp.concatenate([ph_ref[:, 0:W] * keep, pm_ref[:, 0:W]], axis=0)
        xc, _, _, ig, _, a, mult, _ = _lru_gates(xrx, cw_ref, cb_ref, wr_ref, br_ref, wi_ref, bi_ref, lam_ref)
        sa_ref[...], sb_ref[...] = _group_scan(a, mult * (ig * xc), reverse=False)
        last = _apply_carries(sa_ref, sb_ref, hs_ref, carry_ref[0:1, :], reverse=False)
        carry_ref[...] = jnp.broadcast_to(last, (8, W))
        hs = hs_ref[...]
        gt = pm_ref[:, W:2 * W]
        y_ref[...] = (hs * (gt * _sigmoid(gt))).astype(BF16)
        hn_ref[...] = h_ref[...] + jnp.dot(y_ref[...], wo_ref[...], preferred_element_type=F32)

    vec = pl.BlockSpec((None, 1, W), lambda i: (sl, 0, 0))
    heads = pl.BlockSpec((None, LRU_HEADS, LRU_HD, LRU_HD), lambda i: (sl, 0, 0, 0))
    rows = pl.BlockSpec((tT, D), lambda i: (i, 0))
    wide = pl.BlockSpec((tT, W), lambda i: (i, 0))
    return _pallas(
        body, name=name, grid=(T // tT,),
        in_specs=[pl.BlockSpec((tT, 2 * W), lambda i: (i, 0)),
                  pl.BlockSpec((HL, 2 * W), lambda i: (jnp.maximum(i * hb - 1, 0), 0)),
                  pl.BlockSpec((None, 8, W), lambda i: (sl, 0, 0)), vec, heads, vec, heads, vec, vec,
                  rows, pl.BlockSpec((None, W, D), lambda i: (0, 0, 0))],
        out_specs=[wide, wide, rows],
        out_shape=[jax.ShapeDtypeStruct((T, W), BF16), jax.ShapeDtypeStruct((T, W), F32),
                   jax.ShapeDtypeStruct((T, D), F32)],
        scratch_shapes=[pltpu.VMEM((8, W), F32), pltpu.VMEM((tT, W), F32), pltpu.VMEM((tT, W), F32)],
        compiler_params=_params("arbitrary"))(p, p, cw, cb, wr, br, wi, bi, lam, h, w_out)


def _odd_mixer_bwd(p, hs, dout, w_out, after, sl, cw, cb, wr, br, wi, bi, lam, name):
    T = p.shape[0]
    W = W_LRU
    D = dout.shape[1]
    tT, HL = MIX_TILE, ODD_HALO
    hb = tT // HL
    nT = T // tT

    def body(pm_ref, ph_ref, hsm_ref, hsh_ref, do_ref, wo_ref, after_ref, cw_ref, cb_ref, wr_ref, br_ref, wi_ref,
             bi_ref, lam_ref, dp_ref, dwr_ref, dwi_ref, dvec_ref, gcarry_ref, xcarry_ref, sa_ref, sb_ref, g_ref):
        i = pl.program_id(0)
        keep = (i < nT - 1).astype(F32)

        @pl.when(i == 0)
        def _():
            gcarry_ref[...] = jnp.zeros_like(gcarry_ref)
            xcarry_ref[...] = jnp.zeros_like(xcarry_ref)

        xrx = jnp.concatenate([ph_ref[:, 0:W] * keep, pm_ref[:, 0:W]], axis=0)
        xc, xcb, r, ig, sp, a, mult, inv_mult = _lru_gates(xrx, cw_ref, cb_ref, wr_ref, br_ref, wi_ref, bi_ref, lam_ref)
        hs = hsm_ref[...]
        gt = pm_ref[:, W:2 * W]
        sg = _sigmoid(gt)
        dyv = _nt(do_ref[...], wo_ref[...])
        dp_ref[:, W:2 * W] = (dyv * hs * _dsilu(gt, sg)).astype(BF16)
        row = lax.broadcasted_iota(jnp.int32, (tT, 1), 0)
        m = jnp.where(row == tT - 1, 1.0, _shift_up(a, 1))
        sa_ref[...], sb_ref[...] = _group_scan(m, dyv * (gt * sg), reverse=True)
        first = _apply_carries(sa_ref, sb_ref, g_ref, gcarry_ref[0:1, :], reverse=True)
        G = g_ref[...]
        gcarry_ref[...] = jnp.broadcast_to(a[0:1, :] * first, (8, W))
        hs_prev = jnp.where(row == 0, hsh_ref[HL - 1:HL, :] * keep, _shift_down(hs, 1))
        da = G * hs_prev
        dmult = G * (ig * xc)
        di = G * mult * xc
        dxc = G * mult * ig
        dlog_a = da * a - dmult * (a * a) * inv_mult
        drp = dlog_a * ((-LRU_C) * sp) * r * (1.0 - r)
        dip = di * ig * (1.0 - ig)
        dlam = jnp.sum(dlog_a * ((-LRU_C) * r), axis=0, keepdims=True) * (-_sigmoid(-lam_ref[...]))
        drb, dib = drp.astype(BF16), dip.astype(BF16)
        back = []
        for hd in range(LRU_HEADS):
            cols = slice(hd * LRU_HD, (hd + 1) * LRU_HD)
            back.append(_nt(drb[:, cols], wr_ref[hd]) + _nt(dib[:, cols], wi_ref[hd]))
            dwr_h = _tn(xcb[:, cols], drb[:, cols])
            dwi_h = _tn(xcb[:, cols], dib[:, cols])

            @pl.when(i == 0)
            def _(hd=hd, dwr_h=dwr_h, dwi_h=dwi_h):
                dwr_ref[hd] = dwr_h
                dwi_ref[hd] = dwi_h

            @pl.when(i > 0)
            def _(hd=hd, dwr_h=dwr_h, dwi_h=dwi_h):
                dwr_ref[hd] += dwr_h
                dwi_ref[hd] += dwi_h

        dxc = dxc + jnp.concatenate(back, axis=1)
        dxcx = jnp.concatenate([dxc, xcarry_ref[...]], axis=0)
        dxr = cw_ref[LRU_CONV_K - 1:LRU_CONV_K, :] * dxc
        rows = []
        for k in range(LRU_CONV_K - 1):
            j = LRU_CONV_K - 1 - k
            dxr = dxr + cw_ref[k:k + 1, :] * _shift_up(dxcx, j)[0:tT]
            rows.append(jnp.sum(dxc * _shift_down(xrx, j)[HL:], axis=0, keepdims=True))
        rows.append(jnp.sum(dxc * xrx[HL:], axis=0, keepdims=True))
        dp_ref[:, 0:W] = dxr.astype(BF16)
        xcarry_ref[...] = dxc[0:8]
        rows += [jnp.sum(dxc, axis=0, keepdims=True), jnp.sum(drp, axis=0, keepdims=True),
                 jnp.sum(dip, axis=0, keepdims=True), dlam]
        vecs = jnp.concatenate(rows, axis=0)

        @pl.when(i == 0)
        def _():
            dvec_ref[...] = vecs

        @pl.when(i > 0)
        def _():
            dvec_ref[...] += vecs

    vec = pl.BlockSpec((None, 1, W), lambda i: (sl, 0, 0))
    heads = pl.BlockSpec((None, LRU_HEADS, LRU_HD, LRU_HD), lambda i: (sl, 0, 0, 0))
    dheads = pl.BlockSpec((LRU_HEADS, LRU_HD, LRU_HD), lambda i: (0, 0, 0))

    def tile(i):
        return (nT - 1 - i, 0)

    def prev_blk(i):
        return (jnp.maximum((nT - 1 - i) * hb - 1, 0), 0)

    return _pallas(
        body, name=name, grid=(nT,),
        in_specs=[pl.BlockSpec((tT, 2 * W), tile), pl.BlockSpec((HL, 2 * W), prev_blk),
                  pl.BlockSpec((tT, W), tile), pl.BlockSpec((HL, W), prev_blk), pl.BlockSpec((tT, D), tile),
                  pl.BlockSpec((None, W, D), lambda i: (0, 0, 0)), pl.BlockSpec((8, 128), lambda i: (0, 0)),
                  pl.BlockSpec((None, 8, W), lambda i: (sl, 0, 0)), vec, heads, vec, heads, vec, vec],
        out_specs=[pl.BlockSpec((tT, 2 * W), tile), dheads, dheads, pl.BlockSpec((8, W), lambda i: (0, 0))],
        out_shape=[jax.ShapeDtypeStruct((T, 2 * W), BF16), jax.ShapeDtypeStruct((LRU_HEADS, LRU_HD, LRU_HD), F32),
                   jax.ShapeDtypeStruct((LRU_HEADS, LRU_HD, LRU_HD), F32), jax.ShapeDtypeStruct((8, W), F32)],
        scratch_shapes=[pltpu.VMEM((8, W), F32), pltpu.VMEM((8, W), F32), pltpu.VMEM((tT, W), F32),
                        pltpu.VMEM((tT, W), F32), pltpu.VMEM((tT, W), F32)],
        compiler_params=_params("arbitrary"))(p, p, hs, hs, dout, w_out, after, cw, cb, wr, br, wi, bi, lam)


def _pad_rows(a, rows):
    return jnp.pad(a, ((0, 0), (0, rows - a.shape[1]), (0, 0)))


def _layer_fwd(even, h, w, w_in, w_out, after):
    sl = w["sl"]
    p, n = _in_proj(h, w["norm"], sl, w_in, 0, after, "in_proj_even" if even else "in_proj_odd")
    if even:
        y, aux, h_next = _even_mixer_fwd(p, h, w_out, sl, w["conv_w"], w["conv_b"], w["ln_g"], w["ln_b"], w["pool_w"],
                                         w["pool_b"], w["pool_scale"], "even_mixer_fwd")
    else:
        y, aux, h_next = _odd_mixer_fwd(p, h, w_out, sl, w["conv_w"], w["conv_b"], w["w_rg"], w["b_rg"], w["w_ig"],
                                        w["b_ig"], w["lam"], "odd_mixer_fwd")
    return h_next, (h, n, p, aux, y)


def _layer_bwd_weights(even, saved, w, w_out, dhb, after):
    h, n, p, aux, y = saved
    if even:
        dp, dcw, dvec, dpw = _even_mixer_bwd(p, aux, dhb, w_out, after, w["sl"], w["conv_w_rev"], w["ln_g"], w["ln_b"],
                                             w["pool_w"], w["pool_b"], w["pool_scale"], "even_mixer_bwd")
        dw_out = _dw_out(y, dhb, 0, 1, None, "dw_out_even")
        dw_in = _dw_in(n, dp, N_CHIPS, 0, 1, None, "dw_in_even")
        return dp, dw_in, dw_out, dict(conv_w=dcw, vec=dvec, pool_w=dpw)
    dp, dwr, dwi, dvec = _odd_mixer_bwd(p, aux, dhb, w_out, after, w["sl"], w["conv_w"], w["conv_b"], w["w_rg"],
                                        w["b_rg"], w["w_ig"], w["b_ig"], w["lam"], "odd_mixer_bwd")
    dw_out = _dw_out(y, dhb, 0, 1, None, "dw_out_odd")
    dw_in = _dw_in(n, dp, N_CHIPS, 0, 1, None, "dw_in_odd")
    return dp, dw_in, dw_out, dict(w_rg=dwr, w_ig=dwi, vec=dvec)


def _layer_bwd_input(even, saved, w, w_in, dp, dh, after):
    return _dn_proj(dp, w_in, 0, saved[0], w["norm"], w["sl"], dh, after, "dn_proj_even" if even else "dn_proj_odd")


ANY = pl.BlockSpec(memory_space=pl.ANY)


def _mesh_pos():
    return lax.axis_index("x"), lax.axis_index("y"), lax.axis_index("c")


def _other_chips(x, y):
    return [(1 - x, y), (x, 1 - y), (1 - x, 1 - y)]


def _remote(src, dst, ssem, rsem, dev):
    return pltpu.make_async_remote_copy(src_ref=src, dst_ref=dst, send_sem=ssem, recv_sem=rsem, device_id=dev,
                                        device_id_type=MESH)


def _comm_call(body, name, ins, out_shape, scratch, aliases=None):
    return _pallas(body, name=name, in_specs=[ANY] * len(ins), out_specs=[ANY] * len(out_shape), out_shape=out_shape,
                   scratch_shapes=scratch, input_output_aliases=aliases or {},
                   compiler_params=pltpu.CompilerParams(has_side_effects=True))(*ins)


def _cast_shard(w, layer, pos):
    _, R, C = w.shape
    tr = _row_tile(R, C)

    def body(pos_ref, w_ref, o_ref):
        o_ref[...] = w_ref[...].astype(BF16)

    grid_spec = pltpu.PrefetchScalarGridSpec(
        num_scalar_prefetch=1, grid=(R // tr,),
        in_specs=[pl.BlockSpec((None, tr, C), lambda i, pr: (layer, i, 0))],
        out_specs=pl.BlockSpec((None, None, tr, C), lambda i, pr: (0, pr[0], i, 0)))
    return _pallas(body, name="cast_shard", grid_spec=grid_spec,
                   out_shape=jax.ShapeDtypeStruct((1, N_CHIPS, R, C), BF16),
                   compiler_params=_params("parallel"))(pos, w)


def _gather_weights(big, small):
    nA = len(big)
    half = [a.shape[2] // 2 for a in big]

    def body(*refs):
        ins, outs = refs[:nA + 1], refs[nA + 1:2 * nA + 2]
        ssem, rsem, fsem, frsem, lsem = refs[2 * nA + 2:]
        x, y, c = _mesh_pos()
        k = 2 * x + y
        chips = _other_chips(x, y)
        sib = (x, y, 1 - c)

        def slab(a, chip, core):
            return outs[a].at[:, chip, pl.ds(core * half[a], half[a]), :]

        local = [pltpu.make_async_copy(ins[nA], outs[nA].at[k], lsem.at[0])]
        for cp in local:
            cp.start()
        sends = []
        for j, (ox, oy) in enumerate(chips):
            for a in range(nA):
                sends.append(_remote(slab(a, k, c), slab(a, k, c), ssem.at[a, j], rsem.at[a, j], (ox, oy, c)))
            sends.append(_remote(ins[nA], outs[nA].at[k], ssem.at[nA, j], rsem.at[nA, j], (ox, oy, c)))
        for cp in sends:
            cp.start()
        for j, (ox, oy) in enumerate(chips):
            kj = 2 * ox + oy
            for a in range(nA):
                got = slab(a, kj, c)
                _remote(got, got, ssem.at[a, j], rsem.at[a, j], (ox, oy, c)).wait_recv()
                fw = _remote(got, got, fsem.at[a, j], frsem.at[a, j], sib)
                fw.start()
                sends.append(fw)
            gs = outs[nA].at[kj]
            _remote(gs, gs, ssem.at[nA, j], rsem.at[nA, j], (ox, oy, c)).wait_recv()
        for j, (ox, oy) in enumerate(chips):
            kj = 2 * ox + oy
            for a in range(nA):
                theirs = slab(a, kj, 1 - c)
                _remote(theirs, theirs, fsem.at[a, j], frsem.at[a, j], sib).wait_recv()
        for cp in sends:
            cp.wait_send()
        for cp in local:
            cp.wait()

    out_shape = [jax.ShapeDtypeStruct(a.shape, a.dtype) for a in big]
    out_shape.append(jax.ShapeDtypeStruct((N_CHIPS,) + small.shape, small.dtype))
    scratch = [pltpu.SemaphoreType.DMA((nA + 1, 3)), pltpu.SemaphoreType.DMA((nA + 1, 3)),
               pltpu.SemaphoreType.DMA((nA, 3)), pltpu.SemaphoreType.DMA((nA, 3)), pltpu.SemaphoreType.DMA((1,))]
    return _comm_call(body, "gather_weights", list(big) + [small], out_shape, scratch, {a: a for a in range(nA)})


HBM = pl.BlockSpec(memory_space=pltpu.HBM)
SEM = pl.BlockSpec(memory_space=pltpu.SEMAPHORE)
EFFECT = pltpu.SideEffectType.DATAFLOW_SIDE_EFFECTING


def _split_start(arrays, copies, n, name):
    k = len(arrays)

    def body(*refs):
        for cp in copies(refs[k + 2:2 * k + 2], refs[k], refs[k + 1]):
            cp.start()
        refs[2 * k + 2][...] = jnp.zeros((8, 128), F32)

    out = _pallas(
        body, name=name,
        out_shape=(pltpu.SemaphoreType.DMA((n,)), pltpu.SemaphoreType.DMA((n,)),
                   *[pltpu.HBM(a.shape, a.dtype) for a in arrays], jax.ShapeDtypeStruct((8, 128), F32)),
        in_specs=(HBM,) * k, out_specs=(SEM, SEM) + (HBM,) * k + (pl.BlockSpec(memory_space=pltpu.VMEM),),
        input_output_aliases={i: i + 2 for i in range(k)},
        compiler_params=pltpu.CompilerParams(has_side_effects=EFFECT),
    )(*[pltpu.with_memory_space_constraint(a, pltpu.HBM) for a in arrays])
    return out[0], out[1], list(out[2:2 + k]), out[2 + k]


def _split_wait(ssem, rsem, arrays, copies, after, name):
    k = len(arrays)

    def body(*refs):
        for cp in copies(refs[:k], refs[k], refs[k + 1]):
            cp.wait_send()
            cp.wait_recv()

    out = _pallas(
        body, name=name, out_shape=tuple(pltpu.HBM(a.shape, a.dtype) for a in arrays),
        in_specs=(HBM,) * k + (SEM, SEM, ANY), out_specs=(HBM,) * k, input_output_aliases={i: i for i in range(k)},
        compiler_params=pltpu.CompilerParams(has_side_effects=EFFECT),
    )(*arrays, ssem, rsem, after)
    return list(out)


def _gather_copies(shapes):
    half = [s[2] // 2 for s in shapes]

    def copies(refs, ssem, rsem):
        x, y, c = _mesh_pos()
        out = []
        for j, (ox, oy) in enumerate(_other_chips(x, y)):
            for a, ref in enumerate(refs):
                slab = ref.at[:, 2 * x + y, pl.ds(c * half[a], half[a]), :]
                out.append(_remote(slab, slab, ssem.at[3 * a + j], rsem.at[3 * a + j], (ox, oy, c)))
        return out

    return copies


def _chips_copies(n_arr):
    def copies(refs, ssem, rsem):
        x, y, c = _mesh_pos()
        out = []
        for j, (ox, oy) in enumerate(_other_chips(x, y)):
            for a in range(n_arr):
                out.append(_remote(refs[a].at[:, 2 * ox + oy], refs[n_arr + a].at[:, 2 * x + y], ssem.at[3 * a + j],
                                   rsem.at[3 * a + j], (ox, oy, c)))
        return out

    return copies


def _halves_copies(shapes):
    n = len(shapes)
    half = [s[2] // 2 for s in shapes]

    def copies(refs, ssem, rsem):
        x, y, c = _mesh_pos()
        return [_remote(refs[a].at[:, :, pl.ds((1 - c) * half[a], half[a]), :], refs[n + a], ssem.at[a], rsem.at[a],
                        (x, y, 1 - c)) for a in range(n)]

    return copies


def _forward_cores(arrays):
    nA = len(arrays)
    half = [a.shape[2] // 2 for a in arrays]

    def body(*refs):
        outs = refs[nA:2 * nA]
        ssem, rsem = refs[2 * nA:]
        x, y, c = _mesh_pos()
        sib = (x, y, 1 - c)
        sends, waits = [], []
        for j, (ox, oy) in enumerate(_other_chips(x, y)):
            for a in range(nA):
                got = outs[a].at[:, 2 * ox + oy, pl.ds(c * half[a], half[a]), :]
                sends.append(_remote(got, got, ssem.at[a, j], rsem.at[a, j], sib))
                theirs = outs[a].at[:, 2 * ox + oy, pl.ds((1 - c) * half[a], half[a]), :]
                waits.append(_remote(theirs, theirs, ssem.at[a, j], rsem.at[a, j], sib))
        for cp in sends:
            cp.start()
        for cp in waits:
            cp.wait_recv()
        for cp in sends:
            cp.wait_send()

    out_shape = [jax.ShapeDtypeStruct(a.shape, a.dtype) for a in arrays]
    scratch = [pltpu.SemaphoreType.DMA((nA, 3)), pltpu.SemaphoreType.DMA((nA, 3))]
    return _comm_call(body, "forward_cores", list(arrays), out_shape, scratch, {a: a for a in range(nA)})


def _exchange_halves(big):
    nA = len(big)
    half = [a.shape[2] // 2 for a in big]

    def body(*refs):
        ins, outs = refs[:nA], refs[nA:2 * nA]
        ssem, rsem = refs[2 * nA:]
        x, y, c = _mesh_pos()
        sib = (x, y, 1 - c)
        sends = [_remote(ins[a].at[:, :, pl.ds((1 - c) * half[a], half[a]), :], outs[a], ssem.at[a], rsem.at[a], sib)
                 for a in range(nA)]
        for cp in sends:
            cp.start()
        for a in range(nA):
            _remote(outs[a], outs[a], ssem.at[a], rsem.at[a], sib).wait_recv()
        for cp in sends:
            cp.wait_send()

    out_shape = [jax.ShapeDtypeStruct((a.shape[0], N_CHIPS, h, a.shape[3]), a.dtype) for a, h in zip(big, half)]
    scratch = [pltpu.SemaphoreType.DMA((nA,)), pltpu.SemaphoreType.DMA((nA,))]
    return _comm_call(body, "exchange_halves", list(big), out_shape, scratch)


def _exchange_final(grads, everywhere, small):
    nA = len(grads)
    n_remote = sum(7 if ev else 1 for ev in everywhere) + 7

    def body(*refs):
        small_ref, outs, gathered = refs[nA], refs[nA + 1:2 * nA + 1], refs[2 * nA + 1]
        ssem, rsem, lsem = refs[2 * nA + 2:]
        x, y, c = _mesh_pos()
        k = 2 * x + y
        sib = (x, y, 1 - c)
        local = pltpu.make_async_copy(small_ref, gathered.at[2 * k + c], lsem.at[0])
        local.start()
        sends, arrivals, waits = [], [], []
        count = [0]

        def sems():
            count[0] += 1
            return ssem.at[count[0] - 1], rsem.at[count[0] - 1]

        def to_sibling(src, mine, theirs):
            sm = sems()
            sends.append(_remote(src, mine, *sm, sib))
            waits.append(_remote(theirs, theirs, *sm, sib))

        def to_everyone(src, place):
            to_sibling(src, place(k, c), place(k, 1 - c))
            for (ox, oy) in _other_chips(x, y):
                ici, d2d = sems(), sems()
                got = place(2 * ox + oy, c)
                sends.append(_remote(src, place(k, c), *ici, (ox, oy, c)))
                arrivals.append((_remote(got, got, *ici, (ox, oy, c)), _remote(got, got, *d2d, sib)))
                theirs = place(2 * ox + oy, 1 - c)
                waits.append(_remote(theirs, theirs, *d2d, sib))

        to_everyone(small_ref, lambda chip, core: gathered.at[2 * chip + core])
        for a in range(nA):
            if everywhere[a]:
                r2 = grads[a].shape[1] // N_DEV

                def place(chip, core, a=a, r2=r2):
                    return outs[a].at[:, pl.ds((2 * chip + core) * r2, r2), :]

                to_everyone(place(k, c), place)
            else:
                r2 = grads[a].shape[1] // 2
                mine = outs[a].at[:, pl.ds(c * r2, r2), :]
                to_sibling(mine, mine, outs[a].at[:, pl.ds((1 - c) * r2, r2), :])
        for cp in sends:
            cp.start()
        for arrived, onward in arrivals:
            arrived.wait_recv()
            onward.start()
        for cp in waits:
            cp.wait_recv()
        for cp in sends + [onward for _, onward in arrivals]:
            cp.wait_send()
        local.wait()

    out_shape = [jax.ShapeDtypeStruct(g.shape, g.dtype) for g in grads]
    out_shape.append(jax.ShapeDtypeStruct((N_DEV,) + small.shape, small.dtype))
    scratch = [pltpu.SemaphoreType.DMA((n_remote,)), pltpu.SemaphoreType.DMA((n_remote,)), pltpu.SemaphoreType.DMA((1,))]
    return _comm_call(body, "exchange_final", list(grads) + [small], out_shape, scratch, {a: a for a in range(nA)})


BLOCK_BYTES = 4 << 20


def _row_tile(rows, cols, mult=16, limit=BLOCK_BYTES):
    best = mult
    for t in range(mult, rows + 1, mult):
        if rows % t == 0 and t * cols * 4 <= limit:
            best = t
    return best


def _add_cores(own, recv, pos):
    L, _, R, C = own.shape
    r2 = R // 2
    tr = _row_tile(r2, C)
    nb = r2 // tr

    def body(pos_ref, a_ref, r_ref, o_ref):
        o_ref[...] = (a_ref[...].astype(F32) + r_ref[...].astype(F32)).astype(BF16)

    blk = (None, None, tr, C)
    grid_spec = pltpu.PrefetchScalarGridSpec(
        num_scalar_prefetch=1, grid=(L, N_CHIPS, nb),
        in_specs=[pl.BlockSpec(blk, lambda l, s, i, pr: (l, s, pr[1] * nb + i, 0)),
                  pl.BlockSpec(blk, lambda l, s, i, pr: (l, s, i, 0))],
        out_specs=pl.BlockSpec(blk, lambda l, s, i, pr: (l, s, i, 0)))
    return _pallas(body, name="add_cores", grid_spec=grid_spec,
                   out_shape=jax.ShapeDtypeStruct((L, N_CHIPS, r2, C), BF16),
                   compiler_params=_params("parallel", "parallel", "parallel"))(pos, own, recv)


def _sum_chips(own, recv, pos, everywhere, layer, nlayers, prev):
    _, _, r2, C = own.shape
    tr = _row_tile(r2, 2 * C)
    nb = r2 // tr

    def body(pos_ref, a_ref, r_ref, *rest):
        acc = None
        for s in range(N_CHIPS):
            term = jnp.where(pos_ref[0] == s, a_ref[...], r_ref[s]).astype(F32)
            acc = term if acc is None else acc + term
        rest[-1][...] = acc

    if everywhere:
        def out_map(i, pr):
            return (layer, (2 * pr[0] + pr[1]) * nb + i, 0)
    else:
        def out_map(i, pr):
            return (layer, pr[1] * nb + i, 0)

    in_specs = [pl.BlockSpec((None, None, tr, C), lambda i, pr: (0, pr[0], i, 0)),
                pl.BlockSpec((None, N_CHIPS, tr, C), lambda i, pr: (0, 0, i, 0))]
    grid_spec = pltpu.PrefetchScalarGridSpec(
        num_scalar_prefetch=1, grid=(nb,), in_specs=in_specs + ([] if prev is None else [ANY]),
        out_specs=pl.BlockSpec((None, tr, C), out_map))
    rows = (N_DEV if everywhere else 2) * r2
    args = (pos, own, recv) if prev is None else (pos, own, recv, prev)
    return _pallas(body, name="sum_chips", grid_spec=grid_spec, out_shape=jax.ShapeDtypeStruct((nlayers, rows, C), F32),
                   input_output_aliases={} if prev is None else {3: 0},
                   compiler_params=_params("parallel"))(*args)


def _sum_devices(parts):
    n, R, C = parts.shape
    tr = _row_tile(R, C * n, 8)

    def body(p_ref, o_ref):
        acc = p_ref[0]
        for s in range(1, n):
            acc = acc + p_ref[s]
        o_ref[...] = acc

    return _pallas(body, name="sum_devices", grid=(R // tr,), in_specs=[pl.BlockSpec((n, tr, C), lambda i: (0, i, 0))],
                   out_specs=pl.BlockSpec((tr, C), lambda i: (i, 0)), out_shape=jax.ShapeDtypeStruct((R, C), F32),
                   compiler_params=_params("parallel"))(parts)


def _adamw(w, g, m, v, name):
    L, R, C = w.shape
    tr = _row_tile(R, C, 8, BLOCK_BYTES // 2)

    def body(w_ref, g_ref, m_ref, v_ref, d_ref, m2_ref, v2_ref, g2_ref):
        gg = g_ref[...]
        g2_ref[...] = gg
        m2 = ADAM_B1 * m_ref[...] + (1.0 - ADAM_B1) * gg
        v2 = ADAM_B2 * v_ref[...] + (1.0 - ADAM_B2) * (gg * gg)
        m_hat = m2 / (1.0 - ADAM_B1 ** ADAM_STEP)
        v_hat = v2 / (1.0 - ADAM_B2 ** ADAM_STEP)
        d_ref[...] = -ADAM_LR * (m_hat / (jnp.sqrt(v_hat) + ADAM_EPS) + ADAM_WD * w_ref[...])
        m2_ref[...] = m2
        v2_ref[...] = v2

    blk = pl.BlockSpec((1, tr, C), lambda l, i: (l, i, 0))
    shp = jax.ShapeDtypeStruct((L, R, C), F32)
    return _pallas(body, name=name, grid=(L, R // tr), in_specs=[blk] * 4, out_specs=[blk] * 4, out_shape=[shp] * 4,
                   compiler_params=_params("parallel", "parallel"))(w, g, m, v)


WEIGHTS = ("norm_even", "w_in_even", "conv_a_w", "conv_a_b", "ln_a_g", "ln_a_b", "pool_w", "pool_b", "pool_scale",
           "w_out_even", "norm_odd", "w_in_odd", "conv_c_w", "conv_c_b", "w_rg", "b_rg", "w_ig", "b_ig", "lru_lambda",
           "w_out_odd", "final_norm")
BIG = ("w_in_even", "w_out_even", "pool_w", "w_in_odd", "w_out_odd", "w_rg", "w_ig")
SMALL = tuple(n for n in WEIGHTS if n not in BIG)
SMALL_SHARDED = ("conv_a_w", "pool_b", "norm_odd", "conv_c_w", "conv_c_b", "b_rg", "b_ig", "lru_lambda")


def _pack(arrs):
    flat = jnp.concatenate([a.reshape(-1) for a in arrs])
    rows = -(-flat.shape[0] // (64 * 128)) * 64
    return jnp.pad(flat, (0, rows * 128 - flat.shape[0])).reshape(rows, 128)


def _unpack(buf, shapes, lead=()):
    flat = buf.reshape(tuple(lead) + (-1,))
    out, o = [], 0
    for s in shapes:
        n = 1
        for d in s:
            n *= d
        out.append(flat[..., o:o + n].reshape(tuple(lead) + tuple(s)))
        o += n
    return out


def _shard(full, axis, k):
    n = full.shape[axis] // N_CHIPS
    return lax.dynamic_slice_in_dim(full, k * n, n, axis)


def kernel(x, norm_even, w_in_even, conv_a_w, conv_a_b, ln_a_g, ln_a_b, pool_w, pool_b, pool_scale, w_out_even, norm_odd, w_in_odd, conv_c_w, conv_c_b, w_rg, b_rg, w_ig, b_ig, lru_lambda, w_out_odd, final_norm, loss_target, m_norm_even, m_w_in_even, m_conv_a_w, m_conv_a_b, m_ln_a_g, m_ln_a_b, m_pool_w, m_pool_b, m_pool_scale, m_w_out_even, m_norm_odd, m_w_in_odd, m_conv_c_w, m_conv_c_b, m_w_rg, m_b_rg, m_w_ig, m_b_ig, m_lru_lambda, m_w_out_odd, m_final_norm, v_norm_even, v_w_in_even, v_conv_a_w, v_conv_a_b, v_ln_a_g, v_ln_a_b, v_pool_w, v_pool_b, v_pool_scale, v_w_out_even, v_norm_odd, v_w_in_odd, v_conv_c_w, v_conv_c_b, v_w_rg, v_b_rg, v_w_ig, v_b_ig, v_lru_lambda, v_w_out_odd, v_final_norm):
    P = dict(locals())
    xi, yi, ci = _mesh_pos()
    k = 2 * xi + yi
    L = w_in_even.shape[0]
    D = D_MODEL

    pos = jnp.stack([k, ci]).astype(jnp.int32)
    depth = 2 * L
    pool_w3 = pool_w.reshape(L, 4 * 64, POOL_GW)

    def cast_group(layer):
        j = layer // 2
        if layer % 2 == 0:
            return [_cast_shard(w_in_even, j, pos), _cast_shard(w_out_even, j, pos), _cast_shard(pool_w3, j, pos)]
        return [_cast_shard(w_in_odd, j, pos), _cast_shard(w_out_odd, j, pos)]

    *group, g_small = _gather_weights(cast_group(0), _pack([P[n] for n in SMALL_SHARDED]))
    full = {}
    for n, a in zip(SMALL_SHARDED, _unpack(g_small, [P[n].shape for n in SMALL_SHARDED], lead=(N_CHIPS,))):
        a = jnp.moveaxis(a, 0, -2)
        full[n] = a.reshape(a.shape[:-2] + (N_CHIPS * a.shape[-1],))

    small_even = dict(norm=norm_even[:, None], conv_w=_pad_rows(full["conv_a_w"], 32),
                      conv_w_rev=_pad_rows(full["conv_a_w"][:, ::-1], 32), conv_b=conv_a_b[:, None], ln_g=ln_a_g[:, None],
                      ln_b=ln_a_b[:, None], pool_b=full["pool_b"].reshape(L, 1, D), pool_scale=pool_scale[:, None])
    small_odd = dict(norm=full["norm_odd"][:, None], conv_w=_pad_rows(full["conv_c_w"], 8),
                     conv_b=full["conv_c_b"][:, None], w_rg=w_rg.astype(BF16), b_rg=full["b_rg"][:, None],
                     w_ig=w_ig.astype(BF16), b_ig=full["b_ig"][:, None], lam=full["lru_lambda"][:, None])

    def small_weights(layer, group):
        if layer % 2 == 0:
            pw = group[2].reshape(N_CHIPS, 4, 64, POOL_GW).transpose(1, 0, 2, 3).reshape(4, POOL_GW, POOL_GW)
            return dict(small_even, sl=layer // 2, pool_w=pw)
        return dict(small_odd, sl=layer // 2)

    no_token = jnp.zeros((8, 128), F32)
    h = x[0]
    saved, big_w, small_w = [], [], []
    for layer in range(depth):
        token = no_token
        if layer + 1 < depth:
            nxt = cast_group(layer + 1)
            copies = _gather_copies([a.shape for a in nxt])
            ssem, rsem, nxt, token = _split_start(nxt, copies, 3 * len(nxt), "gather_start%d" % (layer + 1))
        small_w.append(small_weights(layer, group))
        big_w.append((group[0], group[1].reshape(1, -1, D)))
        h, sv = _layer_fwd(layer % 2 == 0, h, small_w[layer], *big_w[layer], token)
        saved.append(sv)
        if layer + 1 < depth:
            group = _forward_cores(_split_wait(ssem, rsem, nxt, copies, h, "gather_wait%d" % (layer + 1)))

    dh, dhb, d_final, loss_part = _loss_head(h, final_norm[None], loss_target[0])
    everywhere = [False, False, False, False, False, True, True]
    final = [None] * len(everywhere)
    small_of = [None] * depth

    def finish(pending, after):
        ssem, rsem, arrs, copies, slots, pj, pl_ = pending
        arrs = _split_wait(ssem, rsem, arrs, copies, after, "chips_wait%d" % pl_)
        for a, r, s in zip(arrs[:len(slots)], arrs[len(slots):], slots):
            final[s] = _sum_chips(a, r, pos, everywhere[s], pj, L, final[s])

    pending = None
    token = no_token
    for layer in reversed(range(depth)):
        j = layer // 2
        even_layer = layer % 2 == 0
        dp, dw_in, dw_out, sm = _layer_bwd_weights(even_layer, saved[layer], small_w[layer], big_w[layer][1], dhb, token)
        if even_layer:
            dpw = sm["pool_w"].reshape(4, N_CHIPS, 64, POOL_GW).transpose(1, 0, 2, 3)
            parts = [dw_in, dw_out.reshape(1, N_CHIPS, -1, D), dpw.reshape(1, N_CHIPS, 4 * 64, POOL_GW).astype(BF16)]
            slots = [0, 1, 2]
        else:
            parts = [dw_in, dw_out.reshape(1, N_CHIPS, -1, D),
                     sm["w_rg"].reshape(1, N_CHIPS, -1, LRU_HD).astype(BF16),
                     sm["w_ig"].reshape(1, N_CHIPS, -1, LRU_HD).astype(BF16)]
            slots = [3, 4, 5, 6]
        n = len(parts)
        if layer > 0:
            hcopies = _halves_copies([a.shape for a in parts])
            hland = [lax.empty((1, N_CHIPS, a.shape[2] // 2, a.shape[3]), a.dtype) for a in parts]
            hs, hr, harrs, htoken = _split_start(parts + hland, hcopies, n, "halves_start%d" % layer)
            dh, dhb, sm["norm"] = _layer_bwd_input(even_layer, saved[layer], small_w[layer], big_w[layer][0], dp, dh,
                                                   htoken)
            harrs = _split_wait(hs, hr, harrs, hcopies, dh, "halves_wait%d" % layer)
            parts, recv = harrs[:n], harrs[n:]
        else:
            recv = _exchange_halves(parts)
        pair = [_add_cores(a, r, pos) for a, r in zip(parts, recv)]
        copies = _chips_copies(n)
        land = [lax.empty(a.shape, a.dtype) for a in pair]
        ssem, rsem, arrs, token = _split_start(pair + land, copies, 3 * n, "chips_start%d" % layer)
        if layer == 0:
            dh, dhb, sm["norm"] = _layer_bwd_input(even_layer, saved[layer], small_w[layer], big_w[layer][0], dp, dh, token)
        small_of[layer] = sm
        if pending is not None:
            finish(pending, dh)
        pending = (ssem, rsem, arrs, copies, slots, j, layer)
    grad_x = dh
    small_g = []
    for jj in range(L):
        ge, go = small_of[2 * jj], small_of[2 * jj + 1]
        small_g += [ge["conv_w"].reshape(32, 8, D).sum(axis=1)[:CONV_K], ge["vec"][0:5], ge["norm"], go["vec"], go["norm"]]
    small_g += [d_final, loss_part]
    small_shapes = [a.shape for a in small_g]
    packed_small = _pack(small_g)
    finish(pending, packed_small)
    *gw, recv_small = _exchange_final(final, everywhere, packed_small)
    sg = _unpack(_sum_devices(recv_small), small_shapes)

    grads = dict(w_in_even=gw[0], w_out_even=gw[1], pool_w=gw[2].reshape(pool_w.shape), w_in_odd=gw[3], w_out_odd=gw[4],
                 w_rg=gw[5].reshape(w_rg.shape), w_ig=gw[6].reshape(w_ig.shape), final_norm=sg[-2][0])
    loss = sg[-1][0, 0]
    ev = [sg[5 * j + 1] for j in range(L)]
    ov = [sg[5 * j + 3] for j in range(L)]
    grads["conv_a_w"] = _shard(jnp.stack([sg[5 * j] for j in range(L)]), 2, k)
    grads["norm_even"] = jnp.stack([sg[5 * j + 2][0] for j in range(L)])
    grads["norm_odd"] = _shard(jnp.stack([sg[5 * j + 4][0] for j in range(L)]), 1, k)
    for r, n in enumerate(("conv_a_b", "ln_a_g", "ln_a_b", "pool_scale")):
        grads[n] = jnp.stack([e[r] for e in ev])
    grads["pool_b"] = _shard(jnp.stack([e[4].reshape(4, POOL_GW) for e in ev]), 2, k)
    grads["conv_c_w"] = _shard(jnp.stack([o[0:4] for o in ov]), 2, k)
    for r, n in zip((4, 5, 6, 7), ("conv_c_b", "b_rg", "b_ig", "lru_lambda")):
        grads[n] = _shard(jnp.stack([o[r] for o in ov]), 1, k)

    delta, new_m, new_v = {}, {}, {}
    for n in BIG:
        s3 = (L, -1, P[n].shape[-1])
        d, m2, v2, g2 = _adamw(P[n].reshape(s3), grads[n].reshape(s3), P["m_" + n].reshape(s3), P["v_" + n].reshape(s3),
                               "adamw")
        delta[n], new_m[n], new_v[n] = d.reshape(P[n].shape), m2.reshape(P[n].shape), v2.reshape(P[n].shape)
        grads[n] = g2.reshape(P[n].shape)
    shapes = [P[n].shape for n in SMALL]
    packed = [_pack([src[n] for n in SMALL])[None] for src in
              (P, grads, {n: P["m_" + n] for n in SMALL}, {n: P["v_" + n] for n in SMALL})]
    for res, out in zip(_adamw(*packed, "adamw_small")[:3], (delta, new_m, new_v)):
        for n, a in zip(SMALL, _unpack(res[0], shapes)):
            out[n] = a

    return (loss, grad_x[None], *[grads[n] for n in WEIGHTS], *[delta[n] for n in WEIGHTS],
            *[new_m[n] for n in WEIGHTS], *[new_v[n] for n in WEIGHTS])
```

```python
import jax
import jax.numpy as jnp
from jax import lax
from jax.experimental import pallas as pl
from jax.experimental.pallas import tpu as pltpu

F32 = jnp.float32
BF16 = jnp.bfloat16
MESH = pl.DeviceIdType.MESH

D_MODEL = 1024
N_CHIPS = 4
N_DEV = 8
EPS_RMS = 1e-6
EPS_LN = 1e-5
CONV_K = 31
POOL_WINDOWS = (2, 4, 8, 16)
POOL_GW = 256
LRU_HEADS = 12
LRU_HD = 128
W_LRU = LRU_HEADS * LRU_HD
LRU_CONV_K = 4
LRU_C = 8.0
ADAM_LR = 0.001
ADAM_B1 = 0.9
ADAM_B2 = 0.999
ADAM_EPS = 1e-08
ADAM_WD = 0.01
ADAM_STEP = 10

VMEM_LIMIT_BYTES = 56 * 1024 * 1024
ROW_TILE = 1024
MIX_TILE = 256
EVEN_HALO = 32
ODD_HALO = 8


def _pallas(body, **kw):
    return pl.pallas_call(body, **kw)


def _params(*sem):
    return pltpu.CompilerParams(dimension_semantics=sem if sem else None, vmem_limit_bytes=VMEM_LIMIT_BYTES)


def _sigmoid(x):
    return 0.5 * jnp.tanh(0.5 * x) + 0.5


def _dsilu(x, s):
    return s * (1.0 + x * (1.0 - s))


def _nt(a, b):
    return lax.dot_general(a, b, (((1,), (1,)), ((), ())), preferred_element_type=F32)


def _tn(a, b):
    return lax.dot_general(a, b, (((0,), (0,)), ((), ())), preferred_element_type=F32)


def _in_proj(h, g, glayer, wg, layer, after, name):
    T, D = h.shape
    _, nblk, _, nb = wg.shape

    nrow = T // ROW_TILE

    def body(h_ref, g_ref, w_ref, after_ref, p_ref, n_ref, n_all):
        j, i = pl.program_id(0), pl.program_id(1)

        @pl.when(j == 0)
        def _():
            x = h_ref[...]
            r = lax.rsqrt(jnp.mean(x * x, axis=-1, keepdims=True) + EPS_RMS)
            nn = (x * r * g_ref[...]).astype(BF16)
            n_ref[...] = nn
            n_all[i] = nn

        p_ref[...] = jnp.dot(n_all[i], w_ref[0], preferred_element_type=F32)

    def rows_once(j, i):
        return (jnp.where(j == 0, i, nrow - 1), 0)

    return _pallas(
        body, name=name, grid=(nblk, nrow),
        in_specs=[pl.BlockSpec((ROW_TILE, D), rows_once), pl.BlockSpec((None, 1, D), lambda j, i: (glayer, 0, 0)),
                  pl.BlockSpec((None, 1, D, nb), lambda j, i: (layer, j, 0, 0)),
                  pl.BlockSpec((8, 128), lambda j, i: (0, 0))],
        out_specs=[pl.BlockSpec((ROW_TILE, nb), lambda j, i: (i, j)), pl.BlockSpec((ROW_TILE, D), rows_once)],
        out_shape=[jax.ShapeDtypeStruct((T, nblk * nb), F32), jax.ShapeDtypeStruct((T, D), BF16)],
        scratch_shapes=[pltpu.VMEM((nrow, ROW_TILE, D), BF16)],
        compiler_params=_params("arbitrary", "arbitrary"))(h, g, wg, after)


def _dn_proj(dp, wg, layer, h, g, glayer, dres, after, name):
    T, D = h.shape
    _, nblk, _, nb = wg.shape

    nrow = T // ROW_TILE

    def body(dp_ref, w_ref, h_ref, g_ref, dres_ref, after_ref, dh_ref, dhb_ref, dg_ref, acc_ref):
        j, i = pl.program_id(0), pl.program_id(1)
        part = _nt(dp_ref[...], w_ref[0])

        @pl.when(j == 0)
        def _():
            acc_ref[i] = part

        @pl.when(j > 0)
        def _():
            acc_ref[i] += part

        @pl.when(j == nblk - 1)
        def _():
            x = h_ref[...]
            r = lax.rsqrt(jnp.mean(x * x, axis=-1, keepdims=True) + EPS_RMS)
            dn = acc_ref[i]
            q = dn * g_ref[...]
            dh = dres_ref[...] + r * q - x * ((r * r * r) * jnp.mean(q * x, axis=-1, keepdims=True))
            dh_ref[...] = dh
            dhb_ref[...] = dh.astype(BF16)
            dgp = jnp.sum(dn * (x * r), axis=0, keepdims=True)

            @pl.when(i == 0)
            def _():
                dg_ref[...] = dgp

            @pl.when(i > 0)
            def _():
                dg_ref[...] += dgp

    def rows_last(j, i):
        return (jnp.where(j == nblk - 1, i, 0), 0)

    return _pallas(
        body, name=name, grid=(nblk, nrow),
        in_specs=[pl.BlockSpec((ROW_TILE, nb), lambda j, i: (i, j)),
                  pl.BlockSpec((None, 1, D, nb), lambda j, i: (layer, j, 0, 0)),
                  pl.BlockSpec((ROW_TILE, D), rows_last), pl.BlockSpec((None, 1, D), lambda j, i: (glayer, 0, 0)),
                  pl.BlockSpec((ROW_TILE, D), rows_last), pl.BlockSpec((8, 128), lambda j, i: (0, 0))],
        out_specs=[pl.BlockSpec((ROW_TILE, D), rows_last), pl.BlockSpec((ROW_TILE, D), rows_last),
                   pl.BlockSpec((1, D), lambda j, i: (0, 0))],
        out_shape=[jax.ShapeDtypeStruct((T, D), F32), jax.ShapeDtypeStruct((T, D), BF16),
                   jax.ShapeDtypeStruct((1, D), F32)],
        scratch_shapes=[pltpu.VMEM((nrow, ROW_TILE, D), F32)],
        compiler_params=_params("arbitrary", "arbitrary"))(dp, wg, h, g, dres, after)


def _dw_in(n, dp, nblk, layer, nlayers, prev, name):
    T, D = n.shape
    nb = dp.shape[1] // nblk
    ta = D

    def body(n_ref, dp_ref, *rest):
        rest[-1][0] = _tn(n_ref[...], dp_ref[...]).astype(BF16)

    in_specs = [pl.BlockSpec((T, ta), lambda j, i: (0, i)), pl.BlockSpec((T, nb), lambda j, i: (0, j))]
    args = (n, dp) if prev is None else (n, dp, prev)
    return _pallas(
        body, name=name, grid=(nblk, D // ta), in_specs=in_specs + ([] if prev is None else [ANY]),
        out_specs=pl.BlockSpec((None, 1, ta, nb), lambda j, i: (layer, j, i, 0)),
        out_shape=jax.ShapeDtypeStruct((nlayers, nblk, D, nb), BF16),
        input_output_aliases={} if prev is None else {2: 0},
        compiler_params=_params("parallel", "parallel"))(*args)


def _dw_out(y, dout, layer, nlayers, prev, name):
    T, K = y.shape
    D = dout.shape[1]
    tk = 512

    def body(y_ref, d_ref, *rest):
        rest[-1][...] = _tn(y_ref[...], d_ref[...]).astype(BF16)

    in_specs = [pl.BlockSpec((T, tk), lambda i: (0, i)), pl.BlockSpec((T, D), lambda i: (0, 0))]
    args = (y, dout) if prev is None else (y, dout, prev)
    return _pallas(
        body, name=name, grid=(K // tk,), in_specs=in_specs + ([] if prev is None else [ANY]),
        out_specs=pl.BlockSpec((None, tk, D), lambda i: (layer, i, 0)),
        out_shape=jax.ShapeDtypeStruct((nlayers, K, D), BF16),
        input_output_aliases={} if prev is None else {2: 0},
        compiler_params=_params("parallel"))(*args)


def _loss_head(h, g, tgt):
    T, D = h.shape
    tm = MIX_TILE

    def body(h_ref, g_ref, t_ref, dh_ref, dhb_ref, dg_ref, loss_ref):
        i = pl.program_id(0)
        x = h_ref[...]
        gg = g_ref[...]
        r = lax.rsqrt(jnp.mean(x * x, axis=-1, keepdims=True) + EPS_RMS)
        xr = x * r
        e = xr * gg - t_ref[...]
        lp = 0.5 * jnp.sum(jnp.mean(e * e, axis=-1, keepdims=True), axis=0, keepdims=True)
        dn = e * (1.0 / D)
        q = dn * gg
        dh = r * q - x * ((r * r * r) * jnp.mean(q * x, axis=-1, keepdims=True))
        dh_ref[...] = dh
        dhb_ref[...] = dh.astype(BF16)
        dgp = jnp.sum(dn * xr, axis=0, keepdims=True)

        @pl.when(i == 0)
        def _():
            dg_ref[...] = dgp
            loss_ref[...] = lp

        @pl.when(i > 0)
        def _():
            dg_ref[...] += dgp
            loss_ref[...] += lp

    return _pallas(
        body, name="loss_head", grid=(T // tm,),
        in_specs=[pl.BlockSpec((tm, D), lambda i: (i, 0)), pl.BlockSpec((1, D), lambda i: (0, 0)),
                  pl.BlockSpec((tm, D), lambda i: (i, 0))],
        out_specs=[pl.BlockSpec((tm, D), lambda i: (i, 0)), pl.BlockSpec((tm, D), lambda i: (i, 0)),
                   pl.BlockSpec((1, D), lambda i: (0, 0)), pl.BlockSpec((1, 1), lambda i: (0, 0))],
        out_shape=[jax.ShapeDtypeStruct((T, D), F32), jax.ShapeDtypeStruct((T, D), BF16),
                   jax.ShapeDtypeStruct((1, D), F32), jax.ShapeDtypeStruct((1, 1), F32)],
        compiler_params=_params("arbitrary"))(h, g, tgt)


def _shift_up(x, j):
    return x if j == 0 else pltpu.roll(x, x.shape[0] - j, 0)


def _shift_down(x, j):
    return x if j == 0 else pltpu.roll(x, j, 0)


def _fill_shifted(dst_ref, src_ref):
    rows = dst_ref.shape[1]
    for s in range(8):
        dst_ref[s] = src_ref[pl.ds(s, rows), :]


def _fill_taps(wb_ref, w_ref):
    for k in range(w_ref.shape[0]):
        wb_ref[k] = jnp.broadcast_to(w_ref[k:k + 1, :], wb_ref.shape[1:])


def _tap_sum(sh_ref, wb_ref, r0, nrows, offsets):
    accs = [None] * (nrows // 8)
    for k, o in enumerate(offsets):
        wk = wb_ref[k]
        for u in range(nrows // 8):
            term = wk * sh_ref[o % 8, pl.ds(r0 + (o // 8) * 8 + 8 * u, 8), :]
            accs[u] = term if accs[u] is None else accs[u] + term
    return jnp.concatenate(accs, axis=0)


def _pool_sums(vx, up):
    sh = _shift_up if up else _shift_down
    outs = []
    for gi, w in enumerate(POOL_WINDOWS):
        s = vx[:, gi * POOL_GW:(gi + 1) * POOL_GW]
        j = 1
        while j < w:
            s = s + sh(s, j)
            j *= 2
        outs.append(s)
    return outs


def _inv_count(row0, nrows):
    pos = (row0 + 1 + lax.broadcasted_iota(jnp.int32, (nrows, 1), 0)).astype(F32)
    return [1.0 / jnp.minimum(pos, float(w)) for w in POOL_WINDOWS]


def _even_mixer_fwd(p, h, w_out, sl, cw, cb, lg, lb, pw, pb, sc, name):
    T = p.shape[0]
    C = D_MODEL
    tT, HL = MIX_TILE, EVEN_HALO
    hb = tT // HL
    chunk = 32

    def body(pm_ref, ph_ref, cw_ref, cb_ref, lg_ref, lb_ref, pw_ref, pb_ref, sc_ref, h_ref, wo_ref, y_ref, u1_ref,
             hn_ref, u0x_ref, sh_ref, wb_ref):
        i = pl.program_id(0)
        keep = (i > 0).astype(F32)

        @pl.when(i == 0)
        def _():
            _fill_taps(wb_ref, cw_ref)

        u0x_ref[0:HL] = ph_ref[:, 0:C] * _sigmoid(ph_ref[:, C:2 * C]) * keep
        u0x_ref[HL:HL + tT] = pm_ref[:, 0:C] * _sigmoid(pm_ref[:, C:2 * C])
        u0x_ref[HL + tT:HL + tT + 8] = jnp.zeros((8, C), F32)
        _fill_shifted(sh_ref, u0x_ref)
        offs = [HL - (CONV_K - 1) + k for k in range(CONV_K)]

        def conv_chunk(c, carry):
            r0 = pl.multiple_of(c * chunk, chunk)
            u1_ref[pl.ds(r0, chunk), :] = _tap_sum(sh_ref, wb_ref, r0, chunk, offs) + cb_ref[...]
            return carry

        lax.fori_loop(0, tT // chunk, conv_chunk, 0)
        u1 = u1_ref[...]
        mu = jnp.mean(u1, axis=-1, keepdims=True)
        xc = u1 - mu
        rs = lax.rsqrt(jnp.mean(xc * xc, axis=-1, keepdims=True) + EPS_LN)
        u2 = xc * rs * lg_ref[...] + lb_ref[...]
        u3 = u2 * _sigmoid(u2)
        ag = pm_ref[:, 2 * C:3 * C]
        y_ref[:, 0:C] = (u3 * (ag * _sigmoid(ag))).astype(BF16)
        vx = jnp.concatenate([ph_ref[:, 3 * C:4 * C] * keep, pm_ref[:, 3 * C:4 * C]], axis=0)
        sums = _pool_sums(vx, up=False)
        inv = _inv_count(i * tT, tT)
        for gi in range(len(POOL_WINDOWS)):
            cols = slice(gi * POOL_GW, (gi + 1) * POOL_GW)
            d0 = sums[gi][HL:] * inv[gi] - vx[HL:, cols]
            d1 = jnp.dot(d0.astype(BF16), pw_ref[gi], preferred_element_type=F32) + pb_ref[:, cols]
            bg = pm_ref[:, 4 * C + gi * POOL_GW:4 * C + (gi + 1) * POOL_GW]
            y_ref[:, C + gi * POOL_GW:C + (gi + 1) * POOL_GW] = (d1 * sc_ref[:, cols] * (bg * _sigmoid(bg))).astype(BF16)
        hn_ref[...] = h_ref[...] + jnp.dot(y_ref[...], wo_ref[...], preferred_element_type=F32)

    vec = pl.BlockSpec((None, 1, C), lambda i: (sl, 0, 0))
    rows = pl.BlockSpec((tT, C), lambda i: (i, 0))
    return _pallas(
        body, name=name, grid=(T // tT,),
        in_specs=[pl.BlockSpec((tT, 5 * C), lambda i: (i, 0)),
                  pl.BlockSpec((HL, 5 * C), lambda i: (jnp.maximum(i * hb - 1, 0), 0)),
                  pl.BlockSpec((None, 32, C), lambda i: (sl, 0, 0)), vec, vec, vec,
                  pl.BlockSpec((4, POOL_GW, POOL_GW), lambda i: (0, 0, 0)), vec, vec,
                  rows, pl.BlockSpec((None, 2 * C, C), lambda i: (0, 0, 0))],
        out_specs=[pl.BlockSpec((tT, 2 * C), lambda i: (i, 0)), rows, rows],
        out_shape=[jax.ShapeDtypeStruct((T, 2 * C), BF16), jax.ShapeDtypeStruct((T, C), F32),
                   jax.ShapeDtypeStruct((T, C), F32)],
        scratch_shapes=[pltpu.VMEM((HL + tT + 8, C), F32), pltpu.VMEM((8, HL + tT, C), F32),
                        pltpu.VMEM((32, 8, C), F32)],
        compiler_params=_params("arbitrary"))(p, p, cw, cb, lg, lb, pw, pb, sc, h, w_out)


def _even_mixer_bwd(p, u1, dout, w_out, after, sl, cwr, lg, lb, pw, pb, sc, name):
    T = p.shape[0]
    C = D_MODEL
    tT, HL = MIX_TILE, EVEN_HALO
    hb = tT // HL
    nT = T // tT
    R1 = tT + HL
    chunk = 32

    def body(pm_ref, pp_ref, pn_ref, u1m_ref, u1n_ref, dom_ref, don_ref, wo_ref, after_ref, cwr_ref, lg_ref, lb_ref,
             pw_ref, pb_ref, sc_ref, dp_ref, dcw_ref, dvec_ref, dpw_ref, x_ref, sh_ref, du0_ref, wb_ref):
        i = pl.program_id(0)
        dy = _nt(jnp.concatenate([dom_ref[...], don_ref[...]], axis=0), wo_ref[...])

        @pl.when(i == 0)
        def _():
            _fill_taps(wb_ref, cwr_ref)

        keep_prev = (i > 0).astype(F32)
        keep_next = (i < nT - 1).astype(F32)
        row = lax.broadcasted_iota(jnp.int32, (R1, 1), 0)
        live = jnp.where(row < tT, 1.0, keep_next)

        def cat(m, n):
            return jnp.concatenate([m, n], axis=0)

        u1 = cat(u1m_ref[...], u1n_ref[...])
        mu = jnp.mean(u1, axis=-1, keepdims=True)
        xc = u1 - mu
        rs = lax.rsqrt(jnp.mean(xc * xc, axis=-1, keepdims=True) + EPS_LN)
        xh = xc * rs
        u2 = xh * lg_ref[...] + lb_ref[...]
        s2 = _sigmoid(u2)
        u3 = u2 * s2
        ag = cat(pm_ref[:, 2 * C:3 * C], pn_ref[:, 2 * C:3 * C])
        sa = _sigmoid(ag)
        dya = dy[:, 0:C]
        dp_ref[:, 2 * C:3 * C] = (dya * u3 * _dsilu(ag, sa))[0:tT].astype(BF16)
        du2 = dya * (ag * sa) * _dsilu(u2, s2)
        dlg = jnp.sum((du2 * xh)[0:tT], axis=0, keepdims=True)
        dlb = jnp.sum(du2[0:tT], axis=0, keepdims=True)
        dxh = du2 * lg_ref[...]
        du1 = rs * (dxh - jnp.mean(dxh, axis=-1, keepdims=True) - xh * jnp.mean(dxh * xh, axis=-1, keepdims=True))
        du1 = du1 * live
        dcb = jnp.sum(du1[0:tT], axis=0, keepdims=True)
        x_ref[0:R1] = du1
        x_ref[R1:R1 + 8] = jnp.zeros((8, C), F32)
        _fill_shifted(sh_ref, x_ref)

        def du0_chunk(c, carry):
            r0 = pl.multiple_of(c * chunk, chunk)
            du0_ref[pl.ds(r0, chunk), :] = _tap_sum(sh_ref, wb_ref, r0, chunk, list(range(CONV_K)))
            return carry

        lax.fori_loop(0, tT // chunk, du0_chunk, 0)
        av, agl = pm_ref[:, 0:C], pm_ref[:, C:2 * C]
        sg = _sigmoid(agl)
        du0 = du0_ref[...]
        dp_ref[:, 0:C] = (du0 * sg).astype(BF16)
        dp_ref[:, C:2 * C] = (du0 * av * sg * (1.0 - sg)).astype(BF16)
        du0_ref[...] = du1[0:tT]
        x_ref[0:HL] = pp_ref[:, 0:C] * _sigmoid(pp_ref[:, C:2 * C]) * keep_prev
        x_ref[HL:HL + tT] = av * sg
        x_ref[HL + tT:HL + tT + 8] = jnp.zeros((8, C), F32)
        _fill_shifted(sh_ref, x_ref)

        @pl.when(i == 0)
        def _():
            dcw_ref[...] = jnp.zeros_like(dcw_ref)

        for k0 in range(0, CONV_K, 2):
            taps = [k for k in (k0, k0 + 1) if k < CONV_K]
            offs = [HL - (CONV_K - 1) + k for k in taps]

            def dw_chunk(c, accs, offs=offs):
                r0 = pl.multiple_of(c * 64, 64)
                accs = list(accs)
                for u in range(0, 64, 8):
                    d = du0_ref[pl.ds(r0 + u, 8), :]
                    for t, o in enumerate(offs):
                        accs[t] = accs[t] + d * sh_ref[o % 8, pl.ds(r0 + u + (o // 8) * 8, 8), :]
                return tuple(accs)

            sums = lax.fori_loop(0, tT // 64, dw_chunk, tuple(jnp.zeros((8, C), F32) for _ in taps))
            for k, acc in zip(taps, sums):
                dcw_ref[8 * k:8 * k + 8, :] += acc

        bg = cat(pm_ref[:, 4 * C:5 * C], pn_ref[:, 4 * C:5 * C])
        sb = _sigmoid(bg)
        dyb = dy[:, C:2 * C]
        dyb0 = dyb * (bg * sb)
        dd1 = dyb0 * sc_ref[...]
        dpb = jnp.sum(dd1[0:tT], axis=0, keepdims=True)
        inv1 = _inv_count(i * tT, R1)
        z_parts, dd0_parts = [], []
        for gi in range(len(POOL_WINDOWS)):
            cols = slice(gi * POOL_GW, (gi + 1) * POOL_GW)
            dd0 = _nt(dd1[:, cols].astype(BF16), pw_ref[gi])
            dd0_parts.append(dd0)
            z_parts.append(dd0 * inv1[gi] * live)
        fsum = _pool_sums(jnp.concatenate(z_parts, axis=1), up=True)
        vx = cat(pp_ref[:, 3 * C:4 * C] * keep_prev, pm_ref[:, 3 * C:4 * C])
        sums = _pool_sums(vx, up=False)
        inv0 = _inv_count(i * tT, tT)
        dsc_parts = []
        for gi in range(len(POOL_WINDOWS)):
            cols = slice(gi * POOL_GW, (gi + 1) * POOL_GW)
            dp_ref[:, 3 * C + gi * POOL_GW:3 * C + (gi + 1) * POOL_GW] = (fsum[gi][0:tT] - dd0_parts[gi][0:tT]).astype(BF16)
            d0 = (sums[gi][HL:] * inv0[gi] - vx[HL:, cols]).astype(BF16)
            d1 = jnp.dot(d0, pw_ref[gi], preferred_element_type=F32) + pb_ref[:, cols]
            bgm, sbm = bg[0:tT, cols], sb[0:tT, cols]
            dp_ref[:, 4 * C + gi * POOL_GW:4 * C + (gi + 1) * POOL_GW] = (
                dyb[0:tT, cols] * d1 * sc_ref[:, cols] * _dsilu(bgm, sbm)).astype(BF16)
            dsc_parts.append(jnp.sum(dyb0[0:tT, cols] * d1, axis=0, keepdims=True))
            dpw_g = _tn(d0, dd1[0:tT, cols].astype(BF16))

            @pl.when(i == 0)
            def _(gi=gi, dpw_g=dpw_g):
                dpw_ref[gi] = dpw_g

            @pl.when(i > 0)
            def _(gi=gi, dpw_g=dpw_g):
                dpw_ref[gi] += dpw_g

        dsc = jnp.concatenate(dsc_parts, axis=1)
        vecs = jnp.concatenate([dcb, dlg, dlb, dsc, dpb, jnp.zeros((3, C), F32)], axis=0)

        @pl.when(i == 0)
        def _():
            dvec_ref[...] = vecs

        @pl.when(i > 0)
        def _():
            dvec_ref[...] += vecs

    vec = pl.BlockSpec((None, 1, C), lambda i: (sl, 0, 0))
    taps = pl.BlockSpec((None, 32, C), lambda i: (sl, 0, 0))

    def prev_blk(i):
        return (jnp.maximum(i * hb - 1, 0), 0)

    def next_blk(i):
        return (jnp.minimum((i + 1) * hb, T // HL - 1), 0)

    return _pallas(
        body, name=name, grid=(nT,),
        in_specs=[pl.BlockSpec((tT, 5 * C), lambda i: (i, 0)), pl.BlockSpec((HL, 5 * C), prev_blk),
                  pl.BlockSpec((HL, 5 * C), next_blk),
                  pl.BlockSpec((tT, C), lambda i: (i, 0)), pl.BlockSpec((HL, C), next_blk),
                  pl.BlockSpec((tT, C), lambda i: (i, 0)), pl.BlockSpec((HL, C), next_blk),
                  pl.BlockSpec((None, 2 * C, C), lambda i: (0, 0, 0)), pl.BlockSpec((8, 128), lambda i: (0, 0)),
                  taps, vec, vec, pl.BlockSpec((4, POOL_GW, POOL_GW), lambda i: (0, 0, 0)), vec, vec],
        out_specs=[pl.BlockSpec((tT, 5 * C), lambda i: (i, 0)), pl.BlockSpec((32 * 8, C), lambda i: (0, 0)),
                   pl.BlockSpec((8, C), lambda i: (0, 0)), pl.BlockSpec((4, POOL_GW, POOL_GW), lambda i: (0, 0, 0))],
        out_shape=[jax.ShapeDtypeStruct((T, 5 * C), BF16), jax.ShapeDtypeStruct((32 * 8, C), F32),
                   jax.ShapeDtypeStruct((8, C), F32), jax.ShapeDtypeStruct((4, POOL_GW, POOL_GW), F32)],
        scratch_shapes=[pltpu.VMEM((R1 + 8, C), F32), pltpu.VMEM((8, R1, C), F32), pltpu.VMEM((tT, C), F32),
                        pltpu.VMEM((32, 8, C), F32)],
        compiler_params=_params("arbitrary"))(p, p, p, u1, u1, dout, dout, w_out, after, cwr, lg, lb, pw, pb, sc)


def _softplus(z):
    u = jnp.exp(-jnp.abs(z))
    w = 1.0 + u
    l1p = jnp.where(w == 1.0, u, u * jnp.log(w) / jnp.where(w == 1.0, 1.0, w - 1.0))
    return jnp.maximum(z, 0.0) + l1p


def _lru_gates(xrx, cw_ref, cb_ref, wr_ref, br_ref, wi_ref, bi_ref, lam_ref):
    HL = ODD_HALO
    xc = cb_ref[...] + cw_ref[LRU_CONV_K - 1:LRU_CONV_K, :] * xrx[HL:]
    for k in range(LRU_CONV_K - 1):
        xc = xc + cw_ref[k:k + 1, :] * _shift_down(xrx, LRU_CONV_K - 1 - k)[HL:]
    xcb = xc.astype(BF16)
    rp, ip = [], []
    for hd in range(LRU_HEADS):
        cols = slice(hd * LRU_HD, (hd + 1) * LRU_HD)
        rp.append(jnp.dot(xcb[:, cols], wr_ref[hd], preferred_element_type=F32))
        ip.append(jnp.dot(xcb[:, cols], wi_ref[hd], preferred_element_type=F32))
    r = _sigmoid(jnp.concatenate(rp, axis=1) + br_ref[...])
    ig = _sigmoid(jnp.concatenate(ip, axis=1) + bi_ref[...])
    sp = _softplus(-lam_ref[...])
    log_a = (-LRU_C) * r * sp
    a = jnp.exp(log_a)
    m2 = jnp.maximum(-jnp.tanh(log_a) * (a * a + 1.0), 1e-30)
    inv_mult = lax.rsqrt(m2)
    return xc, xcb, r, ig, sp, a, m2 * inv_mult, inv_mult


def _group_scan(a, b, reverse):
    n, w = a.shape
    a, b = a.reshape(n // 8, 8, w), b.reshape(n // 8, 8, w)
    pos = lax.broadcasted_iota(jnp.int32, (1, 8, 1), 1)
    s = 1
    while s < 8:
        ok = (pos < 8 - s) if reverse else (pos >= s)
        shift = (8 - s) if reverse else s
        a_sh = jnp.where(ok, pltpu.roll(a, shift, 1), 1.0)
        b_sh = jnp.where(ok, pltpu.roll(b, shift, 1), 0.0)
        b = a * b_sh + b
        a = a * a_sh
        s *= 2
    return a.reshape(n, w), b.reshape(n, w)


def _apply_carries(a_ref, b_ref, out_ref, c0, reverse):
    ng = a_ref.shape[0] // 8

    def step(t, c):
        r0 = pl.multiple_of(((ng - 1 - t) if reverse else t) * 8, 8)
        x = a_ref[pl.ds(r0, 8), :] * c + b_ref[pl.ds(r0, 8), :]
        out_ref[pl.ds(r0, 8), :] = x
        return x[0:1, :] if reverse else x[7:8, :]

    return lax.fori_loop(0, ng, step, c0)


def _odd_mixer_fwd(p, h, w_out, sl, cw, cb, wr, br, wi, bi, lam, name):
    T = p.shape[0]
    W = W_LRU
    D = D_MODEL
    tT, HL = MIX_TILE, ODD_HALO
    hb = tT // HL

    def body(pm_ref, ph_ref, cw_ref, cb_ref, wr_ref, br_ref, wi_ref, bi_ref, lam_ref, h_ref, wo_ref, y_ref, hs_ref,
             hn_ref, carry_ref, sa_ref, sb_ref):
        i = pl.program_id(0)
        keep = (i > 0).astype(F32)

        @pl.when(i == 0)
        def _():
            carry_ref[...] = jnp.zeros_like(carry_ref)

        xrx = jnp.concatenate([ph_ref[:, 0:W] * keep, pm_ref[:, 0:W]], axis=0)
        xc, _, _, ig, _, a, mult, _ = _lru_gates(xrx, cw_ref, cb_ref, wr_ref, br_ref, wi_ref, bi_ref, lam_ref)
        sa_ref[...], sb_ref[...] = _group_scan(a, mult * (ig * xc), reverse=False)
        last = _apply_carries(sa_ref, sb_ref, hs_ref, carry_ref[0:1, :], reverse=False)
        carry_ref[...] = jnp.broadcast_to(last, (8, W))
        hs = hs_ref[...]
        gt = pm_ref[:, W:2 * W]
        y_ref[...] = (hs * (gt * _sigmoid(gt))).astype(BF16)
        hn_ref[...] = h_ref[...] + jnp.dot(y_ref[...], wo_ref[...], preferred_element_type=F32)

    vec = pl.BlockSpec((None, 1, W), lambda i: (sl, 0, 0))
    heads = pl.BlockSpec((None, LRU_HEADS, LRU_HD, LRU_HD), lambda i: (sl, 0, 0, 0))
    rows = pl.BlockSpec((tT, D), lambda i: (i, 0))
    wide = pl.BlockSpec((tT, W), lambda i: (i, 0))
    return _pallas(
        body, name=name, grid=(T // tT,),
        in_specs=[pl.BlockSpec((tT, 2 * W), lambda i: (i, 0)),
                  pl.BlockSpec((HL, 2 * W), lambda i: (jnp.maximum(i * hb - 1, 0), 0)),
                  pl.BlockSpec((None, 8, W), lambda i: (sl, 0, 0)), vec, heads, vec, heads, vec, vec,
                  rows, pl.BlockSpec((None, W, D), lambda i: (0, 0, 0))],
        out_specs=[wide, wide, rows],
        out_shape=[jax.ShapeDtypeStruct((T, W), BF16), jax.ShapeDtypeStruct((T, W), F32),
                   jax.ShapeDtypeStruct((T, D), F32)],
        scratch_shapes=[pltpu.VMEM((8, W), F32), pltpu.VMEM((tT, W), F32), pltpu.VMEM((tT, W), F32)],
        compiler_params=_params("arbitrary"))(p, p, cw, cb, wr, br, wi, bi, lam, h, w_out)


def _odd_mixer_bwd(p, hs, dout, w_out, after, sl, cw, cb, wr, br, wi, bi, lam, name):
    T = p.shape[0]
    W = W_LRU
    D = dout.shape[1]
    tT, HL = MIX_TILE, ODD_HALO
    hb = tT // HL
    nT = T // tT

    def body(pm_ref, ph_ref, hsm_ref, hsh_ref, do_ref, wo_ref, after_ref, cw_ref, cb_ref, wr_ref, br_ref, wi_ref,
             bi_ref, lam_ref, dp_ref, dwr_ref, dwi_ref, dvec_ref, gcarry_ref, xcarry_ref, sa_ref, sb_ref, g_ref):
        i = pl.program_id(0)
        keep = (i < nT - 1).astype(F32)

        @pl.when(i == 0)
        def _():
            gcarry_ref[...] = jnp.zeros_like(gcarry_ref)
            xcarry_ref[...] = jnp.zeros_like(xcarry_ref)

        xrx = jnp.concatenate([ph_ref[:, 0:W] * keep, pm_ref[:, 0:W]], axis=0)
        xc, xcb, r, ig, sp, a, mult, inv_mult = _lru_gates(xrx, cw_ref, cb_ref, wr_ref, br_ref, wi_ref, bi_ref, lam_ref)
        hs = hsm_ref[...]
        gt = pm_ref[:, W:2 * W]
        sg = _sigmoid(gt)
        dyv = _nt(do_ref[...], wo_ref[...])
        dp_ref[:, W:2 * W] = (dyv * hs * _dsilu(gt, sg)).astype(BF16)
        row = lax.broadcasted_iota(jnp.int32, (tT, 1), 0)
        m = jnp.where(row == tT - 1, 1.0, _shift_up(a, 1))
        sa_ref[...], sb_ref[...] = _group_scan(m, dyv * (gt * sg), reverse=True)
        first = _apply_carries(sa_ref, sb_ref, g_ref, gcarry_ref[0:1, :], reverse=True)
        G = g_ref[...]
        gcarry_ref[...] = jnp.broadcast_to(a[0:1, :] * first, (8, W))
        hs_prev = jnp.where(row == 0, hsh_ref[HL - 1:HL, :] * keep, _shift_down(hs, 1))
        da = G * hs_prev
        dmult = G * (ig * xc)
        di = G * mult * xc
        dxc = G * mult * ig
        dlog_a = da * a - dmult * (a * a) * inv_mult
        drp = dlog_a * ((-LRU_C) * sp) * r * (1.0 - r)
        dip = di * ig * (1.0 - ig)
        dlam = jnp.sum(dlog_a * ((-LRU_C) * r), axis=0, keepdims=True) * (-_sigmoid(-lam_ref[...]))
        drb, dib = drp.astype(BF16), dip.astype(BF16)
        back = []
        for hd in range(LRU_HEADS):
            cols = slice(hd * LRU_HD, (hd + 1) * LRU_HD)
            back.append(_nt(drb[:, cols], wr_ref[hd]) + _nt(dib[:, cols], wi_ref[hd]))
            dwr_h = _tn(xcb[:, cols], drb[:, cols])
            dwi_h = _tn(xcb[:, cols], dib[:, cols])

            @pl.when(i == 0)
            def _(hd=hd, dwr_h=dwr_h, dwi_h=dwi_h):
                dwr_ref[hd] = dwr_h
                dwi_ref[hd] = dwi_h

            @pl.when(i > 0)
            def _(hd=hd, dwr_h=dwr_h, dwi_h=dwi_h):
                dwr_ref[hd] += dwr_h
                dwi_ref[hd] += dwi_h

        dxc = dxc + jnp.concatenate(back, axis=1)
        dxcx = jnp.concatenate([dxc, xcarry_ref[...]], axis=0)
        dxr = cw_ref[LRU_CONV_K - 1:LRU_CONV_K, :] * dxc
        rows = []
        for k in range(LRU_CONV_K - 1):
            j = LRU_CONV_K - 1 - k
            dxr = dxr + cw_ref[k:k + 1, :] * _shift_up(dxcx, j)[0:tT]
            rows.append(jnp.sum(dxc * _shift_down(xrx, j)[HL:], axis=0, keepdims=True))
        rows.append(jnp.sum(dxc * xrx[HL:], axis=0, keepdims=True))
        dp_ref[:, 0:W] = dxr.astype(BF16)
        xcarry_ref[...] = dxc[0:8]
        rows += [jnp.sum(dxc, axis=0, keepdims=True), jnp.sum(drp, axis=0, keepdims=True),
                 jnp.sum(dip, axis=0, keepdims=True), dlam]
        vecs = jnp.concatenate(rows, axis=0)

        @pl.when(i == 0)
        def _():
            dvec_ref[...] = vecs

        @pl.when(i > 0)
        def _():
            dvec_ref[...] += vecs

    vec = pl.BlockSpec((None, 1, W), lambda i: (sl, 0, 0))
    heads = pl.BlockSpec((None, LRU_HEADS, LRU_HD, LRU_HD), lambda i: (sl, 0, 0, 0))
    dheads = pl.BlockSpec((LRU_HEADS, LRU_HD, LRU_HD), lambda i: (0, 0, 0))

    def tile(i):
        return (nT - 1 - i, 0)

    def prev_blk(i):
        return (jnp.maximum((nT - 1 - i) * hb - 1, 0), 0)

    return _pallas(
        body, name=name, grid=(nT,),
        in_specs=[pl.BlockSpec((tT, 2 * W), tile), pl.BlockSpec((HL, 2 * W), prev_blk),
                  pl.BlockSpec((tT, W), tile), pl.BlockSpec((HL, W), prev_blk), pl.BlockSpec((tT, D), tile),
                  pl.BlockSpec((None, W, D), lambda i: (0, 0, 0)), pl.BlockSpec((8, 128), lambda i: (0, 0)),
                  pl.BlockSpec((None, 8, W), lambda i: (sl, 0, 0)), vec, heads, vec, heads, vec, vec],
        out_specs=[pl.BlockSpec((tT, 2 * W), tile), dheads, dheads, pl.BlockSpec((8, W), lambda i: (0, 0))],
        out_shape=[jax.ShapeDtypeStruct((T, 2 * W), BF16), jax.ShapeDtypeStruct((LRU_HEADS, LRU_HD, LRU_HD), F32),
                   jax.ShapeDtypeStruct((LRU_HEADS, LRU_HD, LRU_HD), F32), jax.ShapeDtypeStruct((8, W), F32)],
        scratch_shapes=[pltpu.VMEM((8, W), F32), pltpu.VMEM((8, W), F32), pltpu.VMEM((tT, W), F32),
                        pltpu.VMEM((tT, W), F32), pltpu.VMEM((tT, W), F32)],
        compiler_params=_params("arbitrary"))(p, p, hs, hs, dout, w_out, after, cw, cb, wr, br, wi, bi, lam)


def _pad_rows(a, rows):
    return jnp.pad(a, ((0, 0), (0, rows - a.shape[1]), (0, 0)))


def _layer_fwd(even, h, w, w_in, w_out, after):
    sl = w["sl"]
    p, n = _in_proj(h, w["norm"], sl, w_in, 0, after, "in_proj_even" if even else "in_proj_odd")
    if even:
        y, aux, h_next = _even_mixer_fwd(p, h, w_out, sl, w["conv_w"], w["conv_b"], w["ln_g"], w["ln_b"], w["pool_w"],
                                         w["pool_b"], w["pool_scale"], "even_mixer_fwd")
    else:
        y, aux, h_next = _odd_mixer_fwd(p, h, w_out, sl, w["conv_w"], w["conv_b"], w["w_rg"], w["b_rg"], w["w_ig"],
                                        w["b_ig"], w["lam"], "odd_mixer_fwd")
    return h_next, (h, n, p, aux, y)


def _layer_bwd_weights(even, saved, w, w_out, dhb, after):
    h, n, p, aux, y = saved
    if even:
        dp, dcw, dvec, dpw = _even_mixer_bwd(p, aux, dhb, w_out, after, w["sl"], w["conv_w_rev"], w["ln_g"], w["ln_b"],
                                             w["pool_w"], w["pool_b"], w["pool_scale"], "even_mixer_bwd")
        dw_out = _dw_out(y, dhb, 0, 1, None, "dw_out_even")
        dw_in = _dw_in(n, dp, N_CHIPS, 0, 1, None, "dw_in_even")
        return dp, dw_in, dw_out, dict(conv_w=dcw, vec=dvec, pool_w=dpw)
    dp, dwr, dwi, dvec = _odd_mixer_bwd(p, aux, dhb, w_out, after, w["sl"], w["conv_w"], w["conv_b"], w["w_rg"],
                                        w["b_rg"], w["w_ig"], w["b_ig"], w["lam"], "odd_mixer_bwd")
    dw_out = _dw_out(y, dhb, 0, 1, None, "dw_out_odd")
    dw_in = _dw_in(n, dp, N_CHIPS, 0, 1, None, "dw_in_odd")
    return dp, dw_in, dw_out, dict(w_rg=dwr, w_ig=dwi, vec=dvec)


def _layer_bwd_input(even, saved, w, w_in, dp, dh, after):
    return _dn_proj(dp, w_in, 0, saved[0], w["norm"], w["sl"], dh, after, "dn_proj_even" if even else "dn_proj_odd")


ANY = pl.BlockSpec(memory_space=pl.ANY)


def _mesh_pos():
    return lax.axis_index("x"), lax.axis_index("y"), lax.axis_index("c")


def _other_chips(x, y):
    return [(1 - x, y), (x, 1 - y), (1 - x, 1 - y)]


def _remote(src, dst, ssem, rsem, dev):
    return pltpu.make_async_remote_copy(src_ref=src, dst_ref=dst, send_sem=ssem, recv_sem=rsem, device_id=dev,
                                        device_id_type=MESH)


def _comm_call(body, name, ins, out_shape, scratch, aliases=None):
    return _pallas(body, name=name, in_specs=[ANY] * len(ins), out_specs=[ANY] * len(out_shape), out_shape=out_shape,
                   scratch_shapes=scratch, input_output_aliases=aliases or {},
                   compiler_params=pltpu.CompilerParams(has_side_effects=True))(*ins)


def _cast_shard(w, layer, pos):
    _, R, C = w.shape
    tr = _row_tile(R, C)

    def body(pos_ref, w_ref, o_ref):
        o_ref[...] = w_ref[...].astype(BF16)

    grid_spec = pltpu.PrefetchScalarGridSpec(
        num_scalar_prefetch=1, grid=(R // tr,),
        in_specs=[pl.BlockSpec((None, tr, C), lambda i, pr: (layer, i, 0))],
        out_specs=pl.BlockSpec((None, None, tr, C), lambda i, pr: (0, pr[0], i, 0)))
    return _pallas(body, name="cast_shard", grid_spec=grid_spec,
                   out_shape=jax.ShapeDtypeStruct((1, N_CHIPS, R, C), BF16),
                   compiler_params=_params("parallel"))(pos, w)


def _gather_weights(big, small):
    nA = len(big)
    half = [a.shape[2] // 2 for a in big]

    def body(*refs):
        ins, outs = refs[:nA + 1], refs[nA + 1:2 * nA + 2]
        ssem, rsem, fsem, frsem, lsem = refs[2 * nA + 2:]
        x, y, c = _mesh_pos()
        k = 2 * x + y
        chips = _other_chips(x, y)
        sib = (x, y, 1 - c)

        def slab(a, chip, core):
            return outs[a].at[:, chip, pl.ds(core * half[a], half[a]), :]

        local = [pltpu.make_async_copy(ins[nA], outs[nA].at[k], lsem.at[0])]
        for cp in local:
            cp.start()
        sends = []
        for j, (ox, oy) in enumerate(chips):
            for a in range(nA):
                sends.append(_remote(slab(a, k, c), slab(a, k, c), ssem.at[a, j], rsem.at[a, j], (ox, oy, c)))
            sends.append(_remote(ins[nA], outs[nA].at[k], ssem.at[nA, j], rsem.at[nA, j], (ox, oy, c)))
        for cp in sends:
            cp.start()
        for j, (ox, oy) in enumerate(chips):
            kj = 2 * ox + oy
            for a in range(nA):
                got = slab(a, kj, c)
                _remote(got, got, ssem.at[a, j], rsem.at[a, j], (ox, oy, c)).wait_recv()
                fw = _remote(got, got, fsem.at[a, j], frsem.at[a, j], sib)
                fw.start()
                sends.append(fw)
            gs = outs[nA].at[kj]
            _remote(gs, gs, ssem.at[nA, j], rsem.at[nA, j], (ox, oy, c)).wait_recv()
        for j, (ox, oy) in enumerate(chips):
            kj = 2 * ox + oy
            for a in range(nA):
                theirs = slab(a, kj, 1 - c)
                _remote(theirs, theirs, fsem.at[a, j], frsem.at[a, j], sib).wait_recv()
        for cp in sends:
            cp.wait_send()
        for cp in local:
            cp.wait()

    out_shape = [jax.ShapeDtypeStruct(a.shape, a.dtype) for a in big]
    out_shape.append(jax.ShapeDtypeStruct((N_CHIPS,) + small.shape, small.dtype))
    scratch = [pltpu.SemaphoreType.DMA((nA + 1, 3)), pltpu.SemaphoreType.DMA((nA + 1, 3)),
               pltpu.SemaphoreType.DMA((nA, 3)), pltpu.SemaphoreType.DMA((nA, 3)), pltpu.SemaphoreType.DMA((1,))]
    return _comm_call(body, "gather_weights", list(big) + [small], out_shape, scratch, {a: a for a in range(nA)})


HBM = pl.BlockSpec(memory_space=pltpu.HBM)
SEM = pl.BlockSpec(memory_space=pltpu.SEMAPHORE)
EFFECT = pltpu.SideEffectType.DATAFLOW_SIDE_EFFECTING


def _split_start(arrays, copies, n, name):
    k = len(arrays)

    def body(*refs):
        for cp in copies(refs[k + 2:2 * k + 2], refs[k], refs[k + 1]):
            cp.start()
        refs[2 * k + 2][...] = jnp.zeros((8, 128), F32)

    out = _pallas(
        body, name=name,
        out_shape=(pltpu.SemaphoreType.DMA((n,)), pltpu.SemaphoreType.DMA((n,)),
                   *[pltpu.HBM(a.shape, a.dtype) for a in arrays], jax.ShapeDtypeStruct((8, 128), F32)),
        in_specs=(HBM,) * k, out_specs=(SEM, SEM) + (HBM,) * k + (pl.BlockSpec(memory_space=pltpu.VMEM),),
        input_output_aliases={i: i + 2 for i in range(k)},
        compiler_params=pltpu.CompilerParams(has_side_effects=EFFECT),
    )(*[pltpu.with_memory_space_constraint(a, pltpu.HBM) for a in arrays])
    return out[0], out[1], list(out[2:2 + k]), out[2 + k]


def _split_wait(ssem, rsem, arrays, copies, after, name):
    k = len(arrays)

    def body(*refs):
        for cp in copies(refs[:k], refs[k], refs[k + 1]):
            cp.wait_send()
            cp.wait_recv()

    out = _pallas(
        body, name=name, out_shape=tuple(pltpu.HBM(a.shape, a.dtype) for a in arrays),
        in_specs=(HBM,) * k + (SEM, SEM, ANY), out_specs=(HBM,) * k, input_output_aliases={i: i for i in range(k)},
        compiler_params=pltpu.CompilerParams(has_side_effects=EFFECT),
    )(*arrays, ssem, rsem, after)
    return list(out)


def _gather_copies(shapes):
    half = [s[2] // 2 for s in shapes]

    def copies(refs, ssem, rsem):
        x, y, c = _mesh_pos()
        out = []
        for j, (ox, oy) in enumerate(_other_chips(x, y)):
            for a, ref in enumerate(refs):
                slab = ref.at[:, 2 * x + y, pl.ds(c * half[a], half[a]), :]
                out.append(_remote(slab, slab, ssem.at[3 * a + j], rsem.at[3 * a + j], (ox, oy, c)))
        return out

    return copies


def _chips_copies(n_arr):
    def copies(refs, ssem, rsem):
        x, y, c = _mesh_pos()
        out = []
        for j, (ox, oy) in enumerate(_other_chips(x, y)):
            for a in range(n_arr):
                out.append(_remote(refs[a].at[:, 2 * ox + oy], refs[n_arr + a].at[:, 2 * x + y], ssem.at[3 * a + j],
                                   rsem.at[3 * a + j], (ox, oy, c)))
        return out

    return copies


def _halves_copies(shapes):
    n = len(shapes)
    half = [s[2] // 2 for s in shapes]

    def copies(refs, ssem, rsem):
        x, y, c = _mesh_pos()
        return [_remote(refs[a].at[:, :, pl.ds((1 - c) * half[a], half[a]), :], refs[n + a], ssem.at[a], rsem.at[a],
                        (x, y, 1 - c)) for a in range(n)]

    return copies


def _forward_cores(arrays):
    nA = len(arrays)
    half = [a.shape[2] // 2 for a in arrays]

    def body(*refs):
        outs = refs[nA:2 * nA]
        ssem, rsem = refs[2 * nA:]
        x, y, c = _mesh_pos()
        sib = (x, y, 1 - c)
        sends, waits = [], []
        for j, (ox, oy) in enumerate(_other_chips(x, y)):
            for a in range(nA):
                got = outs[a].at[:, 2 * ox + oy, pl.ds(c * half[a], half[a]), :]
                sends.append(_remote(got, got, ssem.at[a, j], rsem.at[a, j], sib))
                theirs = outs[a].at[:, 2 * ox + oy, pl.ds((1 - c) * half[a], half[a]), :]
                waits.append(_remote(theirs, theirs, ssem.at[a, j], rsem.at[a, j], sib))
        for cp in sends:
            cp.start()
        for cp in waits:
            cp.wait_recv()
        for cp in sends:
            cp.wait_send()

    out_shape = [jax.ShapeDtypeStruct(a.shape, a.dtype) for a in arrays]
    scratch = [pltpu.SemaphoreType.DMA((nA, 3)), pltpu.SemaphoreType.DMA((nA, 3))]
    return _comm_call(body, "forward_cores", list(arrays), out_shape, scratch, {a: a for a in range(nA)})


def _exchange_halves(big):
    nA = len(big)
    half = [a.shape[2] // 2 for a in big]

    def body(*refs):
        ins, outs = refs[:nA], refs[nA:2 * nA]
        ssem, rsem = refs[2 * nA:]
        x, y, c = _mesh_pos()
        sib = (x, y, 1 - c)
        sends = [_remote(ins[a].at[:, :, pl.ds((1 - c) * half[a], half[a]), :], outs[a], ssem.at[a], rsem.at[a], sib)
                 for a in range(nA)]
        for cp in sends:
            cp.start()
        for a in range(nA):
            _remote(outs[a], outs[a], ssem.at[a], rsem.at[a], sib).wait_recv()
        for cp in sends:
            cp.wait_send()

    out_shape = [jax.ShapeDtypeStruct((a.shape[0], N_CHIPS, h, a.shape[3]), a.dtype) for a, h in zip(big, half)]
    scratch = [pltpu.SemaphoreType.DMA((nA,)), pltpu.SemaphoreType.DMA((nA,))]
    return _comm_call(body, "exchange_halves", list(big), out_shape, scratch)


def _exchange_final(grads, everywhere, small):
    nA = len(grads)
    n_remote = sum(7 if ev else 1 for ev in everywhere) + 7

    def body(*refs):
        small_ref, outs, gathered = refs[nA], refs[nA + 1:2 * nA + 1], refs[2 * nA + 1]
        ssem, rsem, lsem = refs[2 * nA + 2:]
        x, y, c = _mesh_pos()
        k = 2 * x + y
        sib = (x, y, 1 - c)
        local = pltpu.make_async_copy(small_ref, gathered.at[2 * k + c], lsem.at[0])
        local.start()
        sends, arrivals, waits = [], [], []
        count = [0]

        def sems():
            count[0] += 1
            return ssem.at[count[0] - 1], rsem.at[count[0] - 1]

        def to_sibling(src, mine, theirs):
            sm = sems()
            sends.append(_remote(src, mine, *sm, sib))
            waits.append(_remote(theirs, theirs, *sm, sib))

        def to_everyone(src, place):
            to_sibling(src, place(k, c), place(k, 1 - c))
            for (ox, oy) in _other_chips(x, y):
                ici, d2d = sems(), sems()
                got = place(2 * ox + oy, c)
                sends.append(_remote(src, place(k, c), *ici, (ox, oy, c)))
                arrivals.append((_remote(got, got, *ici, (ox, oy, c)), _remote(got, got, *d2d, sib)))
                theirs = place(2 * ox + oy, 1 - c)
                waits.append(_remote(theirs, theirs, *d2d, sib))

        to_everyone(small_ref, lambda chip, core: gathered.at[2 * chip + core])
        for a in range(nA):
            if everywhere[a]:
                r2 = grads[a].shape[1] // N_DEV

                def place(chip, core, a=a, r2=r2):
                    return outs[a].at[:, pl.ds((2 * chip + core) * r2, r2), :]

                to_everyone(place(k, c), place)
            else:
                r2 = grads[a].shape[1] // 2
                mine = outs[a].at[:, pl.ds(c * r2, r2), :]
                to_sibling(mine, mine, outs[a].at[:, pl.ds((1 - c) * r2, r2), :])
        for cp in sends:
            cp.start()
        for arrived, onward in arrivals:
            arrived.wait_recv()
            onward.start()
        for cp in waits:
            cp.wait_recv()
        for cp in sends + [onward for _, onward in arrivals]:
            cp.wait_send()
        local.wait()

    out_shape = [jax.ShapeDtypeStruct(g.shape, g.dtype) for g in grads]
    out_shape.append(jax.ShapeDtypeStruct((N_DEV,) + small.shape, small.dtype))
    scratch = [pltpu.SemaphoreType.DMA((n_remote,)), pltpu.SemaphoreType.DMA((n_remote,)), pltpu.SemaphoreType.DMA((1,))]
    return _comm_call(body, "exchange_final", list(grads) + [small], out_shape, scratch, {a: a for a in range(nA)})


BLOCK_BYTES = 8 << 20


def _row_tile(rows, cols, mult=16, limit=BLOCK_BYTES):
    best = mult
    for t in range(mult, rows + 1, mult):
        if rows % t == 0 and t * cols * 4 <= limit:
            best = t
    return best


def _add_cores(own, recv, pos):
    L, _, R, C = own.shape
    r2 = R // 2
    tr = _row_tile(r2, C)
    nb = r2 // tr

    def body(pos_ref, a_ref, r_ref, o_ref):
        o_ref[...] = (a_ref[...].astype(F32) + r_ref[...].astype(F32)).astype(BF16)

    blk = (None, None, tr, C)
    grid_spec = pltpu.PrefetchScalarGridSpec(
        num_scalar_prefetch=1, grid=(L, N_CHIPS, nb),
        in_specs=[pl.BlockSpec(blk, lambda l, s, i, pr: (l, s, pr[1] * nb + i, 0)),
                  pl.BlockSpec(blk, lambda l, s, i, pr: (l, s, i, 0))],
        out_specs=pl.BlockSpec(blk, lambda l, s, i, pr: (l, s, i, 0)))
    return _pallas(body, name="add_cores", grid_spec=grid_spec,
                   out_shape=jax.ShapeDtypeStruct((L, N_CHIPS, r2, C), BF16),
                   compiler_params=_params("parallel", "parallel", "parallel"))(pos, own, recv)


def _sum_chips(own, recv, pos, everywhere, layer, nlayers, prev):
    _, _, r2, C = own.shape
    tr = _row_tile(r2, 2 * C)
    nb = r2 // tr

    def body(pos_ref, a_ref, r_ref, *rest):
        acc = None
        for s in range(N_CHIPS):
            term = jnp.where(pos_ref[0] == s, a_ref[...], r_ref[s]).astype(F32)
            acc = term if acc is None else acc + term
        rest[-1][...] = acc

    if everywhere:
        def out_map(i, pr):
            return (layer, (2 * pr[0] + pr[1]) * nb + i, 0)
    else:
        def out_map(i, pr):
            return (layer, pr[1] * nb + i, 0)

    in_specs = [pl.BlockSpec((None, None, tr, C), lambda i, pr: (0, pr[0], i, 0)),
                pl.BlockSpec((None, N_CHIPS, tr, C), lambda i, pr: (0, 0, i, 0))]
    grid_spec = pltpu.PrefetchScalarGridSpec(
        num_scalar_prefetch=1, grid=(nb,), in_specs=in_specs + ([] if prev is None else [ANY]),
        out_specs=pl.BlockSpec((None, tr, C), out_map))
    rows = (N_DEV if everywhere else 2) * r2
    args = (pos, own, recv) if prev is None else (pos, own, recv, prev)
    return _pallas(body, name="sum_chips", grid_spec=grid_spec, out_shape=jax.ShapeDtypeStruct((nlayers, rows, C), F32),
                   input_output_aliases={} if prev is None else {3: 0},
                   compiler_params=_params("parallel"))(*args)


def _sum_devices(parts):
    n, R, C = parts.shape
    tr = _row_tile(R, C * n, 8)

    def body(p_ref, o_ref):
        acc = p_ref[0]
        for s in range(1, n):
            acc = acc + p_ref[s]
        o_ref[...] = acc

    return _pallas(body, name="sum_devices", grid=(R // tr,), in_specs=[pl.BlockSpec((n, tr, C), lambda i: (0, i, 0))],
                   out_specs=pl.BlockSpec((tr, C), lambda i: (i, 0)), out_shape=jax.ShapeDtypeStruct((R, C), F32),
                   compiler_params=_params("parallel"))(parts)


def _adamw(w, g, m, v, name):
    L, R, C = w.shape
    tr = _row_tile(R, C, 8, BLOCK_BYTES // 4)

    def body(w_ref, g_ref, m_ref, v_ref, d_ref, m2_ref, v2_ref, g2_ref):
        gg = g_ref[...]
        g2_ref[...] = gg
        m2 = ADAM_B1 * m_ref[...] + (1.0 - ADAM_B1) * gg
        v2 = ADAM_B2 * v_ref[...] + (1.0 - ADAM_B2) * (gg * gg)
        m_hat = m2 / (1.0 - ADAM_B1 ** ADAM_STEP)
        v_hat = v2 / (1.0 - ADAM_B2 ** ADAM_STEP)
        d_ref[...] = -ADAM_LR * (m_hat / (jnp.sqrt(v_hat) + ADAM_EPS) + ADAM_WD * w_ref[...])
        m2_ref[...] = m2
        v2_ref[...] = v2

    blk = pl.BlockSpec((1, tr, C), lambda l, i: (l, i, 0))
    shp = jax.ShapeDtypeStruct((L, R, C), F32)
    return _pallas(body, name=name, grid=(L, R // tr), in_specs=[blk] * 4, out_specs=[blk] * 4, out_shape=[shp] * 4,
                   compiler_params=_params("parallel", "parallel"))(w, g, m, v)


WEIGHTS = ("norm_even", "w_in_even", "conv_a_w", "conv_a_b", "ln_a_g", "ln_a_b", "pool_w", "pool_b", "pool_scale",
           "w_out_even", "norm_odd", "w_in_odd", "conv_c_w", "conv_c_b", "w_rg", "b_rg", "w_ig", "b_ig", "lru_lambda",
           "w_out_odd", "final_norm")
BIG = ("w_in_even", "w_out_even", "pool_w", "w_in_odd", "w_out_odd", "w_rg", "w_ig")
SMALL = tuple(n for n in WEIGHTS if n not in BIG)
SMALL_SHARDED = ("conv_a_w", "pool_b", "norm_odd", "conv_c_w", "conv_c_b", "b_rg", "b_ig", "lru_lambda")


def _pack(arrs):
    flat = jnp.concatenate([a.reshape(-1) for a in arrs])
    rows = -(-flat.shape[0] // (64 * 128)) * 64
    return jnp.pad(flat, (0, rows * 128 - flat.shape[0])).reshape(rows, 128)


def _unpack(buf, shapes, lead=()):
    flat = buf.reshape(tuple(lead) + (-1,))
    out, o = [], 0
    for s in shapes:
        n = 1
        for d in s:
            n *= d
        out.append(flat[..., o:o + n].reshape(tuple(lead) + tuple(s)))
        o += n
    return out


def _shard(full, axis, k):
    n = full.shape[axis] // N_CHIPS
    return lax.dynamic_slice_in_dim(full, k * n, n, axis)


def kernel(x, norm_even, w_in_even, conv_a_w, conv_a_b, ln_a_g, ln_a_b, pool_w, pool_b, pool_scale, w_out_even, norm_odd, w_in_odd, conv_c_w, conv_c_b, w_rg, b_rg, w_ig, b_ig, lru_lambda, w_out_odd, final_norm, loss_target, m_norm_even, m_w_in_even, m_conv_a_w, m_conv_a_b, m_ln_a_g, m_ln_a_b, m_pool_w, m_pool_b, m_pool_scale, m_w_out_even, m_norm_odd, m_w_in_odd, m_conv_c_w, m_conv_c_b, m_w_rg, m_b_rg, m_w_ig, m_b_ig, m_lru_lambda, m_w_out_odd, m_final_norm, v_norm_even, v_w_in_even, v_conv_a_w, v_conv_a_b, v_ln_a_g, v_ln_a_b, v_pool_w, v_pool_b, v_pool_scale, v_w_out_even, v_norm_odd, v_w_in_odd, v_conv_c_w, v_conv_c_b, v_w_rg, v_b_rg, v_w_ig, v_b_ig, v_lru_lambda, v_w_out_odd, v_final_norm):
    P = dict(locals())
    xi, yi, ci = _mesh_pos()
    k = 2 * xi + yi
    L = w_in_even.shape[0]
    D = D_MODEL

    pos = jnp.stack([k, ci]).astype(jnp.int32)
    depth = 2 * L
    pool_w3 = pool_w.reshape(L, 4 * 64, POOL_GW)

    def cast_group(layer):
        j = layer // 2
        if layer % 2 == 0:
            return [_cast_shard(w_in_even, j, pos), _cast_shard(w_out_even, j, pos), _cast_shard(pool_w3, j, pos)]
        return [_cast_shard(w_in_odd, j, pos), _cast_shard(w_out_odd, j, pos)]

    *group, g_small = _gather_weights(cast_group(0), _pack([P[n] for n in SMALL_SHARDED]))
    full = {}
    for n, a in zip(SMALL_SHARDED, _unpack(g_small, [P[n].shape for n in SMALL_SHARDED], lead=(N_CHIPS,))):
        a = jnp.moveaxis(a, 0, -2)
        full[n] = a.reshape(a.shape[:-2] + (N_CHIPS * a.shape[-1],))

    small_even = dict(norm=norm_even[:, None], conv_w=_pad_rows(full["conv_a_w"], 32),
                      conv_w_rev=_pad_rows(full["conv_a_w"][:, ::-1], 32), conv_b=conv_a_b[:, None], ln_g=ln_a_g[:, None],
                      ln_b=ln_a_b[:, None], pool_b=full["pool_b"].reshape(L, 1, D), pool_scale=pool_scale[:, None])
    small_odd = dict(norm=full["norm_odd"][:, None], conv_w=_pad_rows(full["conv_c_w"], 8),
                     conv_b=full["conv_c_b"][:, None], w_rg=w_rg.astype(BF16), b_rg=full["b_rg"][:, None],
                     w_ig=w_ig.astype(BF16), b_ig=full["b_ig"][:, None], lam=full["lru_lambda"][:, None])

    def small_weights(layer, group):
        if layer % 2 == 0:
            pw = group[2].reshape(N_CHIPS, 4, 64, POOL_GW).transpose(1, 0, 2, 3).reshape(4, POOL_GW, POOL_GW)
            return dict(small_even, sl=layer // 2, pool_w=pw)
        return dict(small_odd, sl=layer // 2)

    no_token = jnp.zeros((8, 128), F32)
    h = x[0]
    saved, big_w, small_w = [], [], []
    for layer in range(depth):
        token = no_token
        if layer + 1 < depth:
            nxt = cast_group(layer + 1)
            copies = _gather_copies([a.shape for a in nxt])
            ssem, rsem, nxt, token = _split_start(nxt, copies, 3 * len(nxt), "gather_start%d" % (layer + 1))
        small_w.append(small_weights(layer, group))
        big_w.append((group[0], group[1].reshape(1, -1, D)))
        h, sv = _layer_fwd(layer % 2 == 0, h, small_w[layer], *big_w[layer], token)
        saved.append(sv)
        if layer + 1 < depth:
            group = _forward_cores(_split_wait(ssem, rsem, nxt, copies, h, "gather_wait%d" % (layer + 1)))

    dh, dhb, d_final, loss_part = _loss_head(h, final_norm[None], loss_target[0])
    everywhere = [False, False, False, False, False, True, True]
    final = [None] * len(everywhere)
    small_of = [None] * depth

    def finish(pending, after):
        ssem, rsem, arrs, copies, slots, pj, pl_ = pending
        arrs = _split_wait(ssem, rsem, arrs, copies, after, "chips_wait%d" % pl_)
        for a, r, s in zip(arrs[:len(slots)], arrs[len(slots):], slots):
            final[s] = _sum_chips(a, r, pos, everywhere[s], pj, L, final[s])

    pending = None
    token = no_token
    for layer in reversed(range(depth)):
        j = layer // 2
        even_layer = layer % 2 == 0
        dp, dw_in, dw_out, sm = _layer_bwd_weights(even_layer, saved[layer], small_w[layer], big_w[layer][1], dhb, token)
        if even_layer:
            dpw = sm["pool_w"].reshape(4, N_CHIPS, 64, POOL_GW).transpose(1, 0, 2, 3)
            parts = [dw_in, dw_out.reshape(1, N_CHIPS, -1, D), dpw.reshape(1, N_CHIPS, 4 * 64, POOL_GW).astype(BF16)]
            slots = [0, 1, 2]
        else:
            parts = [dw_in, dw_out.reshape(1, N_CHIPS, -1, D),
                     sm["w_rg"].reshape(1, N_CHIPS, -1, LRU_HD).astype(BF16),
                     sm["w_ig"].reshape(1, N_CHIPS, -1, LRU_HD).astype(BF16)]
            slots = [3, 4, 5, 6]
        n = len(parts)
        if layer > 0:
            hcopies = _halves_copies([a.shape for a in parts])
            hland = [lax.empty((1, N_CHIPS, a.shape[2] // 2, a.shape[3]), a.dtype) for a in parts]
            hs, hr, harrs, htoken = _split_start(parts + hland, hcopies, n, "halves_start%d" % layer)
            dh, dhb, sm["norm"] = _layer_bwd_input(even_layer, saved[layer], small_w[layer], big_w[layer][0], dp, dh,
                                                   htoken)
            harrs = _split_wait(hs, hr, harrs, hcopies, dh, "halves_wait%d" % layer)
            parts, recv = harrs[:n], harrs[n:]
        else:
            recv = _exchange_halves(parts)
        pair = [_add_cores(a, r, pos) for a, r in zip(parts, recv)]
        copies = _chips_copies(n)
        land = [lax.empty(a.shape, a.dtype) for a in pair]
        ssem, rsem, arrs, token = _split_start(pair + land, copies, 3 * n, "chips_start%d" % layer)
        if layer == 0:
            dh, dhb, sm["norm"] = _layer_bwd_input(even_layer, saved[layer], small_w[layer], big_w[layer][0], dp, dh, token)
        small_of[layer] = sm
        if pending is not None:
            finish(pending, dh)
        pending = (ssem, rsem, arrs, copies, slots, j, layer)
    grad_x = dh
    small_g = []
    for jj in range(L):
        ge, go = small_of[2 * jj], small_of[2 * jj + 1]
        small_g += [ge["conv_w"].reshape(32, 8, D).sum(axis=1)[:CONV_K], ge["vec"][0:5], ge["norm"], go["vec"], go["norm"]]
    small_g += [d_final, loss_part]
    small_shapes = [a.shape for a in small_g]
    packed_small = _pack(small_g)
    finish(pending, packed_small)
    *gw, recv_small = _exchange_final(final, everywhere, packed_small)
    sg = _unpack(_sum_devices(recv_small), small_shapes)

    grads = dict(w_in_even=gw[0], w_out_even=gw[1], pool_w=gw[2].reshape(pool_w.shape), w_in_odd=gw[3], w_out_odd=gw[4],
                 w_rg=gw[5].reshape(w_rg.shape), w_ig=gw[6].reshape(w_ig.shape), final_norm=sg[-2][0])
    loss = sg[-1][0, 0]
    ev = [sg[5 * j + 1] for j in range(L)]
    ov = [sg[5 * j + 3] for j in range(L)]
    grads["conv_a_w"] = _shard(jnp.stack([sg[5 * j] for j in range(L)]), 2, k)
    grads["norm_even"] = jnp.stack([sg[5 * j + 2][0] for j in range(L)])
    grads["norm_odd"] = _shard(jnp.stack([sg[5 * j + 4][0] for j in range(L)]), 1, k)
    for r, n in enumerate(("conv_a_b", "ln_a_g", "ln_a_b", "pool_scale")):
        grads[n] = jnp.stack([e[r] for e in ev])
    grads["pool_b"] = _shard(jnp.stack([e[4].reshape(4, POOL_GW) for e in ev]), 2, k)
    grads["conv_c_w"] = _shard(jnp.stack([o[0:4] for o in ov]), 2, k)
    for r, n in zip((4, 5, 6, 7), ("conv_c_b", "b_rg", "b_ig", "lru_lambda")):
        grads[n] = _shard(jnp.stack([o[r] for o in ov]), 1, k)

    delta, new_m, new_v = {}, {}, {}
    for n in BIG:
        s3 = (L, -1, P[n].shape[-1])
        d, m2, v2, g2 = _adamw(P[n].reshape(s3), grads[n].reshape(s3), P["m_" + n].reshape(s3), P["v_" + n].reshape(s3),
                               "adamw")
        delta[n], new_m[n], new_v[n] = d.reshape(P[n].shape), m2.reshape(P[n].shape), v2.reshape(P[n].shape)
        grads[n] = g2.reshape(P[n].shape)
    shapes = [P[n].shape for n in SMALL]
    packed = [_pack([src[n] for n in SMALL])[None] for src in
              (P, grads, {n: P["m_" + n] for n in SMALL}, {n: P["v_" + n] for n in SMALL})]
    for res, out in zip(_adamw(*packed, "adamw_small")[:3], (delta, new_m, new_v)):
        for n, a in zip(SMALL, _unpack(res[0], shapes)):
            out[n] = a

    return (loss, grad_x[None], *[grads[n] for n in WEIGHTS], *[delta[n] for n in WEIGHTS],
            *[new_m[n] for n in WEIGHTS], *[new_v[n] for n in WEIGHTS])
```
